```python
import jax, jax.numpy as jnp
from jax import lax
import numpy as np

D_MODEL = 1024
BATCH = 16
SEQ = 4096
DEPTH = 1

D_MIX = D_MODEL
A_HEADS = 8
A_DK = 64
A_DV = 64
A_KWIDTH = A_HEADS * A_DK
A_WIDTH = A_HEADS * A_DV
CHUNK = 64
B_HEADS = 4
B_NOPE = 128
B_ROPE = 64
B_V = 128
B_WIDTH = B_HEADS * B_V
Q_LORA = 384
KV_LORA = 256
ROPE_THETA = 10000.0
Q_BLOCK = 128
D_FF = ((8 * D_MODEL + 3 * 256 - 1) // (3 * 256)) * 256
EPS = 1e-6
IN_SPLITS = (A_KWIDTH, A_WIDTH, A_KWIDTH, A_KWIDTH, A_WIDTH, Q_LORA, KV_LORA, B_ROPE)
D_IN = A_KWIDTH * 3 + A_WIDTH * 2 + Q_LORA + KV_LORA + B_ROPE

kernel_name = 'hybrid_hgrn2_mla_encoder_block'


def _rmsnorm(x, g):
    xf = x.astype(jnp.float32)
    y = xf * lax.rsqrt(jnp.mean(xf * xf, axis=-1, keepdims=True) + EPS)
    return (y * g.astype(jnp.float32)).astype(x.dtype)


def _rope_tables(seq):
    inv = 1.0 / (ROPE_THETA ** (jnp.arange(0, B_ROPE, 2, dtype=jnp.float32) / B_ROPE))
    ang = jnp.arange(seq, dtype=jnp.float32)[:, None] * inv[None, :]
    return jnp.cos(ang), jnp.sin(ang)


def _apply_rope(x, cos, sin):
    xf = x.astype(jnp.float32)
    x1, x2 = jnp.split(xf, 2, axis=-1)
    out = jnp.concatenate([x1 * cos - x2 * sin, x1 * sin + x2 * cos], axis=-1)
    return out.astype(x.dtype)


def _gla_chunkwise(q, k, v, log_f):
    bsz, nh, seq, dk = q.shape
    dv = v.shape[-1]
    n = seq // CHUNK
    q = q.reshape(bsz, nh, n, CHUNK, dk)
    k = k.reshape(bsz, nh, n, CHUNK, dk)
    log_f = log_f.reshape(bsz, nh, n, CHUNK, dk)
    v = v.reshape(bsz, nh, n, CHUNK, dv)
    cum = jnp.cumsum(log_f, axis=3)
    last = cum[:, :, :, -1:, :]
    q_dec = q * jnp.exp(cum)
    k_inv = k * jnp.exp(-cum)
    k_to_end = k * jnp.exp(last - cum)
    mask = jnp.tril(jnp.ones((CHUNK, CHUNK), dtype=bool))
    scores = jnp.einsum('bhnid,bhnjd->bhnij', q_dec, k_inv)
    o_intra = jnp.einsum('bhnij,bhnje->bhnie', jnp.where(mask, scores, 0.0), v)
    u = jnp.einsum('bhnjd,bhnje->bhnde', k_to_end, v)
    decay = jnp.exp(last[:, :, :, 0, :])

    def step(s, xs):
        d, u_n = xs
        return d[..., None] * s + u_n, s

    s0 = jnp.zeros((bsz, nh, dk, dv), dtype=q.dtype)
    _, s_prev = lax.scan(step, s0, (jnp.moveaxis(decay, 2, 0), jnp.moveaxis(u, 2, 0)))
    s_prev = jnp.moveaxis(s_prev, 0, 2)
    o_inter = jnp.einsum('bhnid,bhnde->bhnie', q_dec, s_prev)
    return (o_intra + o_inter).reshape(bsz, nh, seq, dv)


def _hgrn2_group(hq, hi, hf_fwd, hf_bwd, hg, lb, norm_g):
    bsz, seq, _ = hq.shape
    f32 = jnp.float32

    def to_heads(a, d):
        return a.astype(f32).reshape(bsz, seq, A_HEADS, d).transpose(0, 2, 1, 3)

    q = to_heads(jax.nn.silu(hq), A_DK)
    v = to_heads(hi, A_DV)

    def direction(pre, lb_dir, reverse):
        lb_h = lb_dir.astype(f32).reshape(A_HEADS, 1, A_DK)
        z = to_heads(pre, A_DK)
        log_f = jnp.log(lb_h + (1.0 - lb_h) * jax.nn.sigmoid(z))
        k = (1.0 - lb_h) * jax.nn.sigmoid(-z)
        if reverse:
            o = _gla_chunkwise(jnp.flip(q, 2), jnp.flip(k, 2), jnp.flip(v, 2), jnp.flip(log_f, 2))
            return jnp.flip(o, 2)
        return _gla_chunkwise(q, k, v, log_f)

    o = direction(hf_fwd, lb[0], False) + direction(hf_bwd, lb[1], True)
    o = o.transpose(0, 2, 1, 3)
    o = o * lax.rsqrt(jnp.mean(o * o, axis=-1, keepdims=True) + EPS)
    o = o * norm_g.astype(f32).reshape(A_HEADS, A_DV)
    o = o.reshape(bsz, seq, A_WIDTH) * jax.nn.silu(hg.astype(f32))
    return o.astype(hq.dtype)


def _mla_group(c_q, c_kv, k_rope, g_qa, w_qb, g_kva, w_kvb, g_out):
    bsz, seq, _ = c_q.shape
    cos, sin = _rope_tables(seq)
    q = (_rmsnorm(c_q, g_qa) @ w_qb).reshape(bsz, seq, B_HEADS, B_NOPE + B_ROPE)
    q_nope, q_rope = q[..., :B_NOPE], q[..., B_NOPE:]
    kv = (_rmsnorm(c_kv, g_kva) @ w_kvb).reshape(bsz, seq, B_HEADS, B_NOPE + B_V)
    k_nope, v = kv[..., :B_NOPE], kv[..., B_NOPE:]
    q_rope = _apply_rope(q_rope, cos[:, None, :], sin[:, None, :])
    k_rope = _apply_rope(k_rope, cos, sin)
    scale = (B_NOPE + B_ROPE) ** -0.5
    n_blk = seq // Q_BLOCK
    qn_blocks = q_nope.reshape(bsz, n_blk, Q_BLOCK, B_HEADS, B_NOPE).swapaxes(0, 1)
    qr_blocks = q_rope.reshape(bsz, n_blk, Q_BLOCK, B_HEADS, B_ROPE).swapaxes(0, 1)

    def attend(blk):
        qn, qr = blk
        s = (jnp.einsum('bqhd,bkhd->bhqk', qn, k_nope)
             + jnp.einsum('bqhr,bkr->bhqk', qr, k_rope))
        p = jax.nn.softmax(s.astype(jnp.float32) * scale, axis=-1)
        return jnp.einsum('bhqk,bkhe->bqhe', p.astype(v.dtype), v)

    o = lax.map(attend, (qn_blocks, qr_blocks))
    o = o.swapaxes(0, 1).reshape(bsz, seq, B_WIDTH)
    return _rmsnorm(o, g_out)


def _fwd_setup_inputs(seed: int = 0) -> dict:
    key = jax.random.key(seed)
    ks = jax.random.split(key, 20)
    f32 = jnp.float32

    def nrm(k, shape, fan_in):
        return jax.random.normal(k, shape, f32) * (fan_in ** -0.5)

    def gain(k, shape):
        return 1.0 + 0.02 * jax.random.normal(k, shape, f32)

    return {
        'x': jax.random.normal(ks[0], (BATCH, SEQ, D_MODEL), f32),
        'norm1_g': gain(ks[1], (DEPTH, D_MODEL)),
        'w_in': nrm(ks[2], (DEPTH, D_MODEL, D_IN), D_MODEL),
        'lb_logits': 0.1 * jax.random.normal(ks[3], (2, DEPTH + 1, A_KWIDTH), f32),
        'hgrn_norm_g': gain(ks[4], (DEPTH, A_WIDTH)),
        'q_a_norm_g': gain(ks[5], (DEPTH, Q_LORA)),
        'w_q_b': nrm(ks[6], (DEPTH, Q_LORA, B_HEADS * (B_NOPE + B_ROPE)), Q_LORA),
        'kv_a_norm_g': gain(ks[7], (DEPTH, KV_LORA)),
        'w_kv_b': nrm(ks[8], (DEPTH, KV_LORA, B_HEADS * (B_NOPE + B_V)), KV_LORA),
        'mla_norm_g': gain(ks[9], (DEPTH, B_WIDTH)),
        'w_out': nrm(ks[10], (DEPTH, D_MIX, D_MODEL), D_MIX),
        'norm2_g': gain(ks[11], (DEPTH, D_MODEL)),
        'w_gate': nrm(ks[12], (DEPTH, D_MODEL, D_FF), D_MODEL),
        'w_up': nrm(ks[13], (DEPTH, D_MODEL, D_FF), D_MODEL),
        'w_down': nrm(ks[14], (DEPTH, D_FF, D_MODEL), D_FF),
        'final_norm_g': gain(ks[15], (D_MODEL,)),
    }


def _fwd_reference(x, norm1_g, w_in, lb_logits, hgrn_norm_g, q_a_norm_g, w_q_b, kv_a_norm_g,
              w_kv_b, mla_norm_g, w_out, norm2_g, w_gate, w_up, w_down, final_norm_g):
    p = jax.nn.softmax(lb_logits.astype(jnp.float32), axis=1)
    lower_bounds = jnp.cumsum(p, axis=1)[:, :DEPTH]
    split_at = [int(v) for v in np.cumsum(IN_SPLITS)[:-1]]
    for l in range(DEPTH):
        h = _rmsnorm(x, norm1_g[l])
        proj = h @ w_in[l]
        hq, hi, hf_fwd, hf_bwd, hg, c_q, c_kv, k_r = jnp.split(proj, split_at, axis=-1)
        y_a = _hgrn2_group(hq, hi, hf_fwd, hf_bwd, hg, lower_bounds[:, l], hgrn_norm_g[l])
        y_b = _mla_group(c_q, c_kv, k_r, q_a_norm_g[l], w_q_b[l], kv_a_norm_g[l],
                         w_kv_b[l], mla_norm_g[l])
        x = x + jnp.concatenate([y_a, y_b], axis=-1) @ w_out[l]
        h = _rmsnorm(x, norm2_g[l])
        x = x + (jax.nn.silu(h @ w_gate[l]) * (h @ w_up[l])) @ w_down[l]
    return _rmsnorm(x, final_norm_g)


import jax as _jax
import jax.numpy as _jnp

TWIN_FORMAT = 'train_step'
FWD_PARAMS = ['x', 'norm1_g', 'w_in', 'lb_logits', 'hgrn_norm_g', 'q_a_norm_g', 'w_q_b', 'kv_a_norm_g', 'w_kv_b', 'mla_norm_g', 'w_out', 'norm2_g', 'w_gate', 'w_up', 'w_down', 'final_norm_g']
TWIN_WEIGHTS = ['norm1_g', 'w_in', 'lb_logits', 'hgrn_norm_g', 'q_a_norm_g', 'w_q_b', 'kv_a_norm_g', 'w_kv_b', 'mla_norm_g', 'w_out', 'norm2_g', 'w_gate', 'w_up', 'w_down', 'final_norm_g']
TWIN_DIFF_INPUT = 'x'
TWIN_INPUTS = ['x', 'norm1_g', 'w_in', 'lb_logits', 'hgrn_norm_g', 'q_a_norm_g', 'w_q_b', 'kv_a_norm_g', 'w_kv_b', 'mla_norm_g', 'w_out', 'norm2_g', 'w_gate', 'w_up', 'w_down', 'final_norm_g', 'loss_target', 'm_norm1_g', 'm_w_in', 'm_lb_logits', 'm_hgrn_norm_g', 'm_q_a_norm_g', 'm_w_q_b', 'm_kv_a_norm_g', 'm_w_kv_b', 'm_mla_norm_g', 'm_w_out', 'm_norm2_g', 'm_w_gate', 'm_w_up', 'm_w_down', 'm_final_norm_g', 'v_norm1_g', 'v_w_in', 'v_lb_logits', 'v_hgrn_norm_g', 'v_q_a_norm_g', 'v_w_q_b', 'v_kv_a_norm_g', 'v_w_kv_b', 'v_mla_norm_g', 'v_w_out', 'v_norm2_g', 'v_w_gate', 'v_w_up', 'v_w_down', 'v_final_norm_g']
TWIN_OUTPUTS = ['loss', 'grad_x', 'grad_norm1_g', 'grad_w_in', 'grad_lb_logits', 'grad_hgrn_norm_g', 'grad_q_a_norm_g', 'grad_w_q_b', 'grad_kv_a_norm_g', 'grad_w_kv_b', 'grad_mla_norm_g', 'grad_w_out', 'grad_norm2_g', 'grad_w_gate', 'grad_w_up', 'grad_w_down', 'grad_final_norm_g', 'delta_norm1_g', 'delta_w_in', 'delta_lb_logits', 'delta_hgrn_norm_g', 'delta_q_a_norm_g', 'delta_w_q_b', 'delta_kv_a_norm_g', 'delta_w_kv_b', 'delta_mla_norm_g', 'delta_w_out', 'delta_norm2_g', 'delta_w_gate', 'delta_w_up', 'delta_w_down', 'delta_final_norm_g', 'new_m_norm1_g', 'new_m_w_in', 'new_m_lb_logits', 'new_m_hgrn_norm_g', 'new_m_q_a_norm_g', 'new_m_w_q_b', 'new_m_kv_a_norm_g', 'new_m_w_kv_b', 'new_m_mla_norm_g', 'new_m_w_out', 'new_m_norm2_g', 'new_m_w_gate', 'new_m_w_up', 'new_m_w_down', 'new_m_final_norm_g', 'new_v_norm1_g', 'new_v_w_in', 'new_v_lb_logits', 'new_v_hgrn_norm_g', 'new_v_q_a_norm_g', 'new_v_w_q_b', 'new_v_kv_a_norm_g', 'new_v_w_kv_b', 'new_v_mla_norm_g', 'new_v_w_out', 'new_v_norm2_g', 'new_v_w_gate', 'new_v_w_up', 'new_v_w_down', 'new_v_final_norm_g']
TWIN_LEAF_KINDS = {'loss': 'loss', 'grad_x': 'grad_x', 'grad_norm1_g': 'grad_w', 'grad_w_in': 'grad_w', 'grad_lb_logits': 'grad_w', 'grad_hgrn_norm_g': 'grad_w', 'grad_q_a_norm_g': 'grad_w', 'grad_w_q_b': 'grad_w', 'grad_kv_a_norm_g': 'grad_w', 'grad_w_kv_b': 'grad_w', 'grad_mla_norm_g': 'grad_w', 'grad_w_out': 'grad_w', 'grad_norm2_g': 'grad_w', 'grad_w_gate': 'grad_w', 'grad_w_up': 'grad_w', 'grad_w_down': 'grad_w', 'grad_final_norm_g': 'grad_w', 'delta_norm1_g': 'delta_w', 'delta_w_in': 'delta_w', 'delta_lb_logits': 'delta_w', 'delta_hgrn_norm_g': 'delta_w', 'delta_q_a_norm_g': 'delta_w', 'delta_w_q_b': 'delta_w', 'delta_kv_a_norm_g': 'delta_w', 'delta_w_kv_b': 'delta_w', 'delta_mla_norm_g': 'delta_w', 'delta_w_out': 'delta_w', 'delta_norm2_g': 'delta_w', 'delta_w_gate': 'delta_w', 'delta_w_up': 'delta_w', 'delta_w_down': 'delta_w', 'delta_final_norm_g': 'delta_w', 'new_m_norm1_g': 'new_m', 'new_m_w_in': 'new_m', 'new_m_lb_logits': 'new_m', 'new_m_hgrn_norm_g': 'new_m', 'new_m_q_a_norm_g': 'new_m', 'new_m_w_q_b': 'new_m', 'new_m_kv_a_norm_g': 'new_m', 'new_m_w_kv_b': 'new_m', 'new_m_mla_norm_g': 'new_m', 'new_m_w_out': 'new_m', 'new_m_norm2_g': 'new_m', 'new_m_w_gate': 'new_m', 'new_m_w_up': 'new_m', 'new_m_w_down': 'new_m', 'new_m_final_norm_g': 'new_m', 'new_v_norm1_g': 'new_v', 'new_v_w_in': 'new_v', 'new_v_lb_logits': 'new_v', 'new_v_hgrn_norm_g': 'new_v', 'new_v_q_a_norm_g': 'new_v', 'new_v_w_q_b': 'new_v', 'new_v_kv_a_norm_g': 'new_v', 'new_v_w_kv_b': 'new_v', 'new_v_mla_norm_g': 'new_v', 'new_v_w_out': 'new_v', 'new_v_norm2_g': 'new_v', 'new_v_w_gate': 'new_v', 'new_v_w_up': 'new_v', 'new_v_w_down': 'new_v', 'new_v_final_norm_g': 'new_v'}


def _forward(args):
    return _fwd_reference(*[args[k] for k in FWD_PARAMS])


def _output_shape():
    out = _jax.eval_shape(lambda: _forward(_fwd_setup_inputs(0)))
    return out.shape, out.dtype

N_MICROBATCH = 1
ADAM_LR = 0.001
ADAM_B1 = 0.9
ADAM_B2 = 0.999
ADAM_EPS = 1e-08
ADAM_WD = 0.01
ADAM_STEP = 10
PER_EXAMPLE_BATCH_AXIS = {'x': 0, 'loss_target': 0}
SHARED_INPUTS = []
_WEIGHT_DTYPES = {'norm1_g': _jnp.float32, 'w_in': _jnp.float32, 'lb_logits': _jnp.float32, 'hgrn_norm_g': _jnp.float32, 'q_a_norm_g': _jnp.float32, 'w_q_b': _jnp.float32, 'kv_a_norm_g': _jnp.float32, 'w_kv_b': _jnp.float32, 'mla_norm_g': _jnp.float32, 'w_out': _jnp.float32, 'norm2_g': _jnp.float32, 'w_gate': _jnp.float32, 'w_up': _jnp.float32, 'w_down': _jnp.float32, 'final_norm_g': _jnp.float32}
MOMENT_SCALE = {'norm1_g': 2.784504e-01, 'w_in': 1.521679e-01, 'lb_logits': 1.052554e-02, 'hgrn_norm_g': 1.329364e-01, 'q_a_norm_g': 2.404650e-01, 'w_q_b': 1.786647e-01, 'kv_a_norm_g': 8.248711e-01, 'w_kv_b': 2.180221e-01, 'mla_norm_g': 2.165461e-01, 'w_out': 1.751563e-01, 'norm2_g': 1.505375e-01, 'w_gate': 6.542327e-02, 'w_up': 6.387085e-02, 'w_down': 1.054038e-01, 'final_norm_g': 6.386024e+01}


def _to_microbatches(a, axis):
    t = _jnp.moveaxis(a, axis, 0)
    t = t.reshape((N_MICROBATCH, t.shape[0] // N_MICROBATCH) + t.shape[1:])
    return _jnp.moveaxis(t, 1, axis + 1)


def setup_inputs(seed: int = 0) -> dict:
    inp = _fwd_setup_inputs(seed)
    key = _jax.random.fold_in(_jax.random.key(seed), 7919)
    shape, _ = _output_shape()
    out = dict(inp)
    out["loss_target"] = _jax.random.normal(_jax.random.fold_in(key, 0), shape, _jnp.float32)
    for i, name in enumerate(TWIN_WEIGHTS):
        w = inp[name].astype(_jnp.float32)
        if MOMENT_SCALE is None:
            s = _jnp.sqrt(_jnp.mean(_jnp.square(w)) + 1e-30)
        else:
            s = MOMENT_SCALE[name]
        km, kv = _jax.random.split(_jax.random.fold_in(key, i + 1))
        out[name] = w
        out["m_" + name] = s * _jax.random.normal(km, w.shape, _jnp.float32)
        out["v_" + name] = (s * s) * _jax.random.uniform(kv, w.shape, _jnp.float32, 0.5, 1.5)
    if N_MICROBATCH > 1:
        for name, axis in PER_EXAMPLE_BATCH_AXIS.items():
            out[name] = _to_microbatches(out[name], axis)
    return {'x': out['x'], 'norm1_g': out['norm1_g'], 'w_in': out['w_in'], 'lb_logits': out['lb_logits'], 'hgrn_norm_g': out['hgrn_norm_g'], 'q_a_norm_g': out['q_a_norm_g'], 'w_q_b': out['w_q_b'], 'kv_a_norm_g': out['kv_a_norm_g'], 'w_kv_b': out['w_kv_b'], 'mla_norm_g': out['mla_norm_g'], 'w_out': out['w_out'], 'norm2_g': out['norm2_g'], 'w_gate': out['w_gate'], 'w_up': out['w_up'], 'w_down': out['w_down'], 'final_norm_g': out['final_norm_g'], 'loss_target': out['loss_target'], 'm_norm1_g': out['m_norm1_g'], 'm_w_in': out['m_w_in'], 'm_lb_logits': out['m_lb_logits'], 'm_hgrn_norm_g': out['m_hgrn_norm_g'], 'm_q_a_norm_g': out['m_q_a_norm_g'], 'm_w_q_b': out['m_w_q_b'], 'm_kv_a_norm_g': out['m_kv_a_norm_g'], 'm_w_kv_b': out['m_w_kv_b'], 'm_mla_norm_g': out['m_mla_norm_g'], 'm_w_out': out['m_w_out'], 'm_norm2_g': out['m_norm2_g'], 'm_w_gate': out['m_w_gate'], 'm_w_up': out['m_w_up'], 'm_w_down': out['m_w_down'], 'm_final_norm_g': out['m_final_norm_g'], 'v_norm1_g': out['v_norm1_g'], 'v_w_in': out['v_w_in'], 'v_lb_logits': out['v_lb_logits'], 'v_hgrn_norm_g': out['v_hgrn_norm_g'], 'v_q_a_norm_g': out['v_q_a_norm_g'], 'v_w_q_b': out['v_w_q_b'], 'v_kv_a_norm_g': out['v_kv_a_norm_g'], 'v_w_kv_b': out['v_w_kv_b'], 'v_mla_norm_g': out['v_mla_norm_g'], 'v_w_out': out['v_w_out'], 'v_norm2_g': out['v_norm2_g'], 'v_w_gate': out['v_w_gate'], 'v_w_up': out['v_w_up'], 'v_w_down': out['v_w_down'], 'v_final_norm_g': out['v_final_norm_g']}


def _loss(weights, diff, rest, loss_target):
    with _jax.named_scope("forward"):
        args = {**rest, TWIN_DIFF_INPUT: diff, **{k: w.astype(_WEIGHT_DTYPES[k]) for k, w in weights.items()}}
        y = _forward(args)
    with _jax.named_scope("loss_head"):
        err = _jnp.square(y.astype(_jnp.float32) - loss_target)
        return 0.5 * _jnp.sum(_jnp.mean(err, axis=-1)) if err.ndim else 0.5 * err


def _adamw(w, g, m, v):
    m = ADAM_B1 * m + (1.0 - ADAM_B1) * g
    v = ADAM_B2 * v + (1.0 - ADAM_B2) * _jnp.square(g)
    m_hat = m / (1.0 - ADAM_B1 ** ADAM_STEP)
    v_hat = v / (1.0 - ADAM_B2 ** ADAM_STEP)
    delta = -ADAM_LR * (m_hat / (_jnp.sqrt(v_hat) + ADAM_EPS) + ADAM_WD * w)
    return delta, m, v


def reference(x, norm1_g, w_in, lb_logits, hgrn_norm_g, q_a_norm_g, w_q_b, kv_a_norm_g, w_kv_b, mla_norm_g, w_out, norm2_g, w_gate, w_up, w_down, final_norm_g, loss_target, m_norm1_g, m_w_in, m_lb_logits, m_hgrn_norm_g, m_q_a_norm_g, m_w_q_b, m_kv_a_norm_g, m_w_kv_b, m_mla_norm_g, m_w_out, m_norm2_g, m_w_gate, m_w_up, m_w_down, m_final_norm_g, v_norm1_g, v_w_in, v_lb_logits, v_hgrn_norm_g, v_q_a_norm_g, v_w_q_b, v_kv_a_norm_g, v_w_kv_b, v_mla_norm_g, v_w_out, v_norm2_g, v_w_gate, v_w_up, v_w_down, v_final_norm_g):
    given = dict(x=x, norm1_g=norm1_g, w_in=w_in, lb_logits=lb_logits, hgrn_norm_g=hgrn_norm_g, q_a_norm_g=q_a_norm_g, w_q_b=w_q_b, kv_a_norm_g=kv_a_norm_g, w_kv_b=w_kv_b, mla_norm_g=mla_norm_g, w_out=w_out, norm2_g=norm2_g, w_gate=w_gate, w_up=w_up, w_down=w_down, final_norm_g=final_norm_g, loss_target=loss_target, m_norm1_g=m_norm1_g, m_w_in=m_w_in, m_lb_logits=m_lb_logits, m_hgrn_norm_g=m_hgrn_norm_g, m_q_a_norm_g=m_q_a_norm_g, m_w_q_b=m_w_q_b, m_kv_a_norm_g=m_kv_a_norm_g, m_w_kv_b=m_w_kv_b, m_mla_norm_g=m_mla_norm_g, m_w_out=m_w_out, m_norm2_g=m_norm2_g, m_w_gate=m_w_gate, m_w_up=m_w_up, m_w_down=m_w_down, m_final_norm_g=m_final_norm_g, v_norm1_g=v_norm1_g, v_w_in=v_w_in, v_lb_logits=v_lb_logits, v_hgrn_norm_g=v_hgrn_norm_g, v_q_a_norm_g=v_q_a_norm_g, v_w_q_b=v_w_q_b, v_kv_a_norm_g=v_kv_a_norm_g, v_w_kv_b=v_w_kv_b, v_mla_norm_g=v_mla_norm_g, v_w_out=v_w_out, v_norm2_g=v_norm2_g, v_w_gate=v_w_gate, v_w_up=v_w_up, v_w_down=v_w_down, v_final_norm_g=v_final_norm_g)
    weights = {n: given[n] for n in TWIN_WEIGHTS}
    shared = {n: given[n] for n in SHARED_INPUTS}
    per_example = {n: given[n] for n in ['x']}
    grad_fn = _jax.value_and_grad(_loss, argnums=(0, 1))

    def one_microbatch(ex, loss_target):
        ex = dict(ex)
        diff = ex.pop(TWIN_DIFF_INPUT)
        return grad_fn(weights, diff, {**shared, **ex}, loss_target)

    if N_MICROBATCH == 1:
        loss, (grad_w, grad_x) = one_microbatch(per_example, given["loss_target"])
    else:
        def body(carry, xs):
            loss_sum, grad_sum = carry
            l_k, (gw_k, gx_k) = one_microbatch(xs[0], xs[1])
            with _jax.named_scope("update"):
                return (loss_sum + l_k, _jax.tree.map(_jnp.add, grad_sum, gw_k)), gx_k

        init = (_jnp.zeros((), _jnp.float32), _jax.tree.map(_jnp.zeros_like, weights))
        (loss, grad_w), grad_x = _jax.lax.scan(body, init, (per_example, given["loss_target"]))
    with _jax.named_scope("update"):
        delta_w, new_m, new_v = {}, {}, {}
        for n in TWIN_WEIGHTS:
            delta_w[n], new_m[n], new_v[n] = _adamw(weights[n], grad_w[n], given["m_" + n], given["v_" + n])
    return (loss, grad_x, *[grad_w[n] for n in TWIN_WEIGHTS], *[delta_w[n] for n in TWIN_WEIGHTS],
            *[new_m[n] for n in TWIN_WEIGHTS], *[new_v[n] for n in TWIN_WEIGHTS])
```

```python
import functools
import math

import jax
import jax.numpy as jnp
from jax import lax
from jax.experimental import pallas as pl
from jax.experimental.pallas import tpu as pltpu

F32 = jnp.float32
BF16 = jnp.bfloat16

N_DEV = 8
D_MODEL = 1024
D_FF = 2816
A_WIDTH = 512
HEAD_PAIR = 128
CHUNK = 64
B_HEADS = 4
B_NOPE = 128
B_ROPE = 64
B_V = 128
QK_PAD = 256
Q_LORA = 384
KV_LORA = 256
D_IN = 3264
D_IN_PAD = 3328
IN_WIDTHS = (512, 512, 512, 512, 512, Q_LORA, KV_LORA, 128)
ROPE_THETA = 10000.0
EPS = 1e-6
ATTN_SCALE = (B_NOPE + B_ROPE) ** -0.5
ADAM_LR, ADAM_B1, ADAM_B2, ADAM_EPS, ADAM_WD, ADAM_STEP = 0.001, 0.9, 0.999, 1e-08, 0.01, 10
VMEM_LIMIT = 56 * 1024 * 1024
MESH = pl.DeviceIdType.MESH

BIG = (("w_in", 1024, D_IN, 1), ("w_q_b", Q_LORA, 768, 1), ("w_kv_b", KV_LORA, 1024, 1), ("w_out", 1024, 1024, 0),
       ("w_gate", 1024, D_FF, 1), ("w_up", 1024, D_FF, 1), ("w_down", D_FF, 1024, 0))
BIG_ROWS = tuple(r * c // N_DEV // 128 for _, r, c, _ in BIG)
PACK_ROWS = sum(BIG_ROWS)
SMALL = (("norm1_g", 1024), ("hgrn_norm_g", 512), ("q_a_norm_g", 384), ("kv_a_norm_g", 256), ("mla_norm_g", 512),
         ("norm2_g", 1024), ("final_norm_g", 1024))
SMALL_ROWS = sum(n // 128 for _, n in SMALL)
LB_ROWS = 16
SMALL_PACK_ROWS = 56


def _params(**kw):
    return pltpu.CompilerParams(vmem_limit_bytes=VMEM_LIMIT, **kw)


def _const_spec(shape):
    return pl.BlockSpec(shape, lambda *_: (0,) * len(shape), pipeline_mode=pl.Buffered(1))


def _dot(a, b):
    return jnp.dot(a, b, preferred_element_type=F32)


def _dot_nt(a, b):
    return lax.dot_general(a, b, (((1,), (1,)), ((), ())), preferred_element_type=F32)


def _dot_tn(a, b):
    return lax.dot_general(a, b, (((0,), (0,)), ((), ())), preferred_element_type=F32)


@jax.custom_vjp
def _mm(a, b):
    return _dot(a.astype(BF16), b.astype(BF16))


def _mm_fwd(a, b):
    return _mm(a, b), (a, b)


def _mm_bwd(res, g):
    a, b = res
    gb = g.astype(BF16)
    return _dot_nt(gb, b.astype(BF16)), _dot_tn(a.astype(BF16), gb)


_mm.defvjp(_mm_fwd, _mm_bwd)


@jax.custom_vjp
def _mm_nt(a, b):
    return _dot_nt(a.astype(BF16), b.astype(BF16))


def _mm_nt_fwd(a, b):
    return _mm_nt(a, b), (a, b)


def _mm_nt_bwd(res, g):
    a, b = res
    gb = g.astype(BF16)
    return _dot(gb, b.astype(BF16)), _dot_tn(gb, a.astype(BF16))


_mm_nt.defvjp(_mm_nt_fwd, _mm_nt_bwd)


@jax.custom_vjp
def _mm_tn(a, b):
    return _dot_tn(a.astype(BF16), b.astype(BF16))


def _mm_tn_fwd(a, b):
    return _mm_tn(a, b), (a, b)


def _mm_tn_bwd(res, g):
    a, b = res
    gb = g.astype(BF16)
    return _dot_nt(b.astype(BF16), gb), _dot(a.astype(BF16), gb)


_mm_tn.defvjp(_mm_tn_fwd, _mm_tn_bwd)


def _split3(a):
    hi = a.astype(BF16)
    r = a - hi.astype(F32)
    mid = r.astype(BF16)
    lo = (r - mid.astype(F32)).astype(BF16)
    return hi, mid, lo


def _dot_exact_rhs(a, m):
    hi, mid, lo = _split3(a)
    return _dot(hi, m) + _dot(mid, m) + _dot(lo, m)


@jax.custom_vjp
def _group_mean(a, m):
    return _dot_exact_rhs(a, m)


def _group_mean_fwd(a, m):
    return _group_mean(a, m), m


def _group_mean_bwd(m, g):
    return _dot_exact_rhs(g, m), jnp.zeros_like(m)


_group_mean.defvjp(_group_mean_fwd, _group_mean_bwd)


def _roll_rows(a, shift):
    return pltpu.roll(a, shift, 0)


def _cumsum_rows_raw(a, reverse):
    n = a.shape[0]
    row = lax.broadcasted_iota(jnp.int32, a.shape, 0)
    s = 1
    while s < n:
        if reverse:
            a = a + jnp.where(row < n - s, _roll_rows(a, n - s), 0.0)
        else:
            a = a + jnp.where(row >= s, _roll_rows(a, s), 0.0)
        s *= 2
    return a


@functools.partial(jax.custom_vjp, nondiff_argnums=(1,))
def _cumsum_rows(a, reverse):
    return _cumsum_rows_raw(a, reverse)


def _cumsum_rows_fwd(a, reverse):
    return _cumsum_rows_raw(a, reverse), None


def _cumsum_rows_bwd(reverse, _, g):
    return (_cumsum_rows_raw(g, not reverse),)


_cumsum_rows.defvjp(_cumsum_rows_fwd, _cumsum_rows_bwd)


def _rms(x, g):
    r = lax.rsqrt(jnp.mean(x * x, axis=-1, keepdims=True) + EPS)
    return x * r * g


def _rms_bwd(x, g, dy):
    r = lax.rsqrt(jnp.mean(x * x, axis=-1, keepdims=True) + EPS)
    xh = x * r
    dg = jnp.sum(dy * xh, axis=0, keepdims=True)
    dxh = dy * g
    dx = r * (dxh - xh * jnp.mean(dxh * xh, axis=-1, keepdims=True))
    return dx, dg


def _sigmoid(a):
    return jax.nn.sigmoid(a)


def _mesh_place():
    x, y, c = lax.axis_index("x"), lax.axis_index("y"), lax.axis_index("c")
    return x, y, c


def _dev_index(p):
    return 4 * p[0] + 2 * p[1] + p[2]


def _all_gather(blocks):
    n = len(blocks)

    def body(*refs):
        ins, outs = refs[:n], refs[n:2 * n]
        send_sems, recv_sems, local_sems = refs[2 * n:]
        x, y, c = _mesh_place()
        me, sibling = (x, y, c), (x, y, 1 - c)
        chips = [(1 - x, y), (x, 1 - y), (1 - x, 1 - y)]

        def copy(a, k, block, to, src=None):
            rows = outs[a].at[_dev_index(block)]
            return pltpu.make_async_remote_copy(
                src_ref=rows if src is None else src, dst_ref=rows,
                send_sem=send_sems.at[a, k], recv_sem=recv_sems.at[a, k],
                device_id=to, device_id_type=MESH)

        started = []
        for a in range(n):
            mine = pltpu.make_async_copy(ins[a], outs[a].at[_dev_index(me)], local_sems.at[a])
            mine.start()
            first = [copy(a, 0, me, sibling, src=ins[a])]
            first += [copy(a, 1 + j, me, (*chip, c), src=ins[a]) for j, chip in enumerate(chips)]
            for cp in first:
                cp.start()
            started += [mine.wait] + [cp.wait_send for cp in first]
        for a in range(n):
            for j, chip in enumerate(chips):
                copy(a, 1 + j, (*chip, c), me).wait_recv()
                passed = copy(a, 4 + j, (*chip, c), sibling)
                passed.start()
                started.append(passed.wait_send)
        for a in range(n):
            copy(a, 0, sibling, me).wait_recv()
            for j, chip in enumerate(chips):
                copy(a, 4 + j, (*chip, 1 - c), me).wait_recv()
        for wait in started:
            wait()

    any_spec = pl.BlockSpec(memory_space=pl.ANY)
    return pl.pallas_call(
        body, name="weights_all_gather",
        out_shape=[jax.ShapeDtypeStruct((N_DEV,) + b.shape, b.dtype) for b in blocks],
        in_specs=[any_spec] * n, out_specs=[any_spec] * n,
        scratch_shapes=[pltpu.SemaphoreType.DMA((n, 7)), pltpu.SemaphoreType.DMA((n, 7)), pltpu.SemaphoreType.DMA((n,))],
    )(*blocks)


def _grad_exchange(gpack, small):
    def body(g_ref, s_ref, rg_ref, rs_ref, send_sems, recv_sems, local_sems):
        x, y, c = _mesh_place()
        me = (x, y, c)
        my_id = _dev_index(me)
        rels = [(dx, dy, dc) for dx in (0, 1) for dy in (0, 1) for dc in (0, 1)][1:]

        def peer_of(rel):
            return tuple(1 - v if d else v for v, d in zip(me, rel))

        def copies(k, rel):
            peer = peer_of(rel)
            big = pltpu.make_async_remote_copy(
                src_ref=g_ref.at[_dev_index(peer)], dst_ref=rg_ref.at[my_id],
                send_sem=send_sems.at[0, k], recv_sem=recv_sems.at[0, k], device_id=peer, device_id_type=MESH)
            sml = pltpu.make_async_remote_copy(
                src_ref=s_ref, dst_ref=rs_ref.at[my_id],
                send_sem=send_sems.at[1, k], recv_sem=recv_sems.at[1, k], device_id=peer, device_id_type=MESH)
            return big, sml

        def arrivals(k, rel):
            peer = peer_of(rel)
            big = pltpu.make_async_remote_copy(
                src_ref=g_ref.at[my_id], dst_ref=rg_ref.at[_dev_index(peer)],
                send_sem=send_sems.at[0, k], recv_sem=recv_sems.at[0, k], device_id=peer, device_id_type=MESH)
            sml = pltpu.make_async_remote_copy(
                src_ref=s_ref, dst_ref=rs_ref.at[_dev_index(peer)],
                send_sem=send_sems.at[1, k], recv_sem=recv_sems.at[1, k], device_id=peer, device_id_type=MESH)
            return big, sml

        own_g = pltpu.make_async_copy(g_ref.at[my_id], rg_ref.at[my_id], local_sems.at[0])
        own_s = pltpu.make_async_copy(s_ref, rs_ref.at[my_id], local_sems.at[1])
        own_g.start()
        own_s.start()
        sent = []
        for k, rel in enumerate(rels):
            for cp in copies(k, rel):
                cp.start()
                sent.append(cp)
        for k, rel in enumerate(rels):
            for cp in arrivals(k, rel):
                cp.wait_recv()
        for cp in sent:
            cp.wait_send()
        own_g.wait()
        own_s.wait()

    any_spec = pl.BlockSpec(memory_space=pl.ANY)
    return pl.pallas_call(
        body, name="grad_exchange",
        out_shape=[jax.ShapeDtypeStruct(gpack.shape, gpack.dtype), jax.ShapeDtypeStruct((N_DEV,) + small.shape, small.dtype)],
        in_specs=[any_spec] * 2, out_specs=[any_spec] * 2,
        scratch_shapes=[pltpu.SemaphoreType.DMA((2, 7)), pltpu.SemaphoreType.DMA((2, 7)), pltpu.SemaphoreType.DMA((2,))],
    )(gpack, small)


def _sum_slots(recv, rows_per_step):
    _, r, _ = recv.shape

    def body(in_ref, out_ref):
        acc = in_ref[0].astype(F32)
        for j in range(1, N_DEV):
            acc = acc + in_ref[j].astype(F32)
        out_ref[...] = acc

    return pl.pallas_call(
        body, name="grad_sum_slots_%d" % r,
        out_shape=jax.ShapeDtypeStruct((r, 128), F32), grid=(r // rows_per_step,),
        in_specs=[pl.BlockSpec((N_DEV, rows_per_step, 128), lambda i: (0, i, 0))],
        out_specs=pl.BlockSpec((rows_per_step, 128), lambda i: (i, 0)),
        compiler_params=_params(),
    )(recv)


def _adamw(w, g, m, v, tag):
    bc1 = 1.0 - ADAM_B1 ** ADAM_STEP
    bc2 = 1.0 - ADAM_B2 ** ADAM_STEP

    def body(w_ref, g_ref, m_ref, v_ref, d_ref, nm_ref, nv_ref):
        gg = g_ref[...]
        nm = ADAM_B1 * m_ref[...] + (1.0 - ADAM_B1) * gg
        nv = ADAM_B2 * v_ref[...] + (1.0 - ADAM_B2) * (gg * gg)
        d_ref[...] = -ADAM_LR * ((nm / bc1) / (jnp.sqrt(nv / bc2) + ADAM_EPS) + ADAM_WD * w_ref[...])
        nm_ref[...] = nm
        nv_ref[...] = nv

    r, c = w.shape
    tr = r
    for cand in (512, 352, 256, 128):
        if r > cand and r % cand == 0:
            tr = cand
            break
    spec = pl.BlockSpec((tr, c), lambda i: (i, 0))
    return pl.pallas_call(
        body, name="adamw_" + tag, out_shape=[jax.ShapeDtypeStruct(w.shape, F32)] * 3, grid=(r // tr,),
        in_specs=[spec] * 4, out_specs=[spec] * 3, compiler_params=_params(),
    )(w, g, m, v)


def _tile(t, want):
    return want if t % want == 0 else t


def _inproj(x, g1, w_in, tm):
    t = x.shape[0]

    def body(x_ref, g_ref, w_ref, *outs):
        h = _rms(x_ref[...], g_ref[...]).astype(BF16)
        off = 0
        for o_ref, wd in zip(outs, IN_WIDTHS):
            o_ref[...] = _dot(h, w_ref[:, off:off + wd])
            off += wd

    return pl.pallas_call(
        body, name="inproj_fwd", grid=(t // tm,),
        out_shape=[jax.ShapeDtypeStruct((t, wd), F32) for wd in IN_WIDTHS],
        in_specs=[pl.BlockSpec((tm, D_MODEL), lambda i: (i, 0)), _const_spec((1, D_MODEL)), _const_spec((D_MODEL, D_IN_PAD))],
        out_specs=[pl.BlockSpec((tm, wd), lambda i: (i, 0)) for wd in IN_WIDTHS],
        compiler_params=_params(),
    )(x, g1, w_in)


def _rope_tables(seq):
    inv = 1.0 / (ROPE_THETA ** (jnp.arange(0, B_ROPE, 2, dtype=F32) / B_ROPE))
    ang = jnp.arange(seq, dtype=F32)[:, None] * inv[None, :]
    cos, sin = jnp.cos(ang), jnp.sin(ang)
    z32, z64 = jnp.zeros_like(cos), jnp.zeros((seq, 64), F32)
    cos_t = jnp.concatenate([cos, cos, z64], axis=1)
    sin_a = jnp.concatenate([-sin, z32, z64], axis=1)
    sin_b = jnp.concatenate([z32, sin, z64], axis=1)
    return cos_t, sin_a, sin_b


def _rope(t, cos_t, sin_a, sin_b):
    return t * cos_t + pltpu.roll(t, 96, 1) * sin_a + pltpu.roll(t, 32, 1) * sin_b


def _rope_t(d, cos_t, sin_a, sin_b):
    return d * cos_t + pltpu.roll(d * sin_a, 32, 1) + pltpu.roll(d * sin_b, 96, 1)


def _mla_qkv(cq, ckv, kr, g_qa, g_kva, w_q, w_k, w_v, tables, seq, tm):
    t = cq.shape[0]
    nblk = seq // tm

    def body(cq_ref, ckv_ref, kr_ref, gq_ref, gk_ref, wq_ref, wk_ref, wv_ref, c_ref, sa_ref, sb_ref, q_out, k_out, v_out):
        cos_t, sin_a, sin_b = c_ref[...], sa_ref[...], sb_ref[...]
        cqn = _rms(cq_ref[...], gq_ref[...]).astype(BF16)
        ckn = _rms(ckv_ref[...], gk_ref[...]).astype(BF16)
        kr_rot = _rope(kr_ref[...], cos_t, sin_a, sin_b).astype(BF16)
        v_out[...] = _dot(ckn, wv_ref[...]).astype(BF16)
        for h in range(B_HEADS):
            lo = h * QK_PAD
            q_out[:, lo:lo + 128] = (_dot(cqn, wq_ref[:, lo:lo + 128]) * ATTN_SCALE).astype(BF16)
            qr = _rope(_dot(cqn, wq_ref[:, lo + 128:lo + 256]), cos_t, sin_a, sin_b)
            q_out[:, lo + 128:lo + 256] = (qr * ATTN_SCALE).astype(BF16)
            k_out[:, lo:lo + 128] = _dot(ckn, wk_ref[:, h * 128:(h + 1) * 128]).astype(BF16)
            k_out[:, lo + 128:lo + 256] = kr_rot

    tok = lambda wd: pl.BlockSpec((tm, wd), lambda i: (i, 0))
    tab = pl.BlockSpec((tm, 128), lambda i: (i % nblk, 0))
    return pl.pallas_call(
        body, name="mla_qkv_fwd", grid=(t // tm,),
        out_shape=[jax.ShapeDtypeStruct((t, B_HEADS * QK_PAD), BF16), jax.ShapeDtypeStruct((t, B_HEADS * QK_PAD), BF16),
                   jax.ShapeDtypeStruct((t, B_HEADS * B_V), BF16)],
        in_specs=[tok(Q_LORA), tok(KV_LORA), tok(128), _const_spec((1, Q_LORA)), _const_spec((1, KV_LORA)),
                  _const_spec((Q_LORA, B_HEADS * QK_PAD)), _const_spec((KV_LORA, 512)), _const_spec((KV_LORA, 512)), tab, tab, tab],
        out_specs=[tok(B_HEADS * QK_PAD), tok(B_HEADS * QK_PAD), tok(B_HEADS * B_V)],
        compiler_params=_params(),
    )(cq, ckv, kr, g_qa, g_kva, w_q, w_k, w_v, *tables)


def _attn_fwd(qcat, kcat, v, nb, seq, tq):
    t = qcat.shape[0]
    nq = seq // tq

    def body(q_ref, k_ref, v_ref, o_ref, lse_ref):
        s = _dot_nt(q_ref[...], k_ref[...])
        m = jnp.max(s, axis=-1, keepdims=True)
        p = jnp.exp(s - m)
        l = jnp.sum(p, axis=-1, keepdims=True)
        o_ref[...] = _dot(p.astype(BF16), v_ref[...]) / l
        lse_ref[0] = m + jnp.log(l)

    return pl.pallas_call(
        body, name="attn_fwd", grid=(nb, B_HEADS, nq),
        out_shape=[jax.ShapeDtypeStruct((t, B_HEADS * B_V), F32), jax.ShapeDtypeStruct((B_HEADS, t, 1), F32)],
        in_specs=[pl.BlockSpec((tq, QK_PAD), lambda b, h, i: (b * nq + i, h)),
                  pl.BlockSpec((seq, QK_PAD), lambda b, h, i: (b, h)),
                  pl.BlockSpec((seq, B_V), lambda b, h, i: (b, h))],
        out_specs=[pl.BlockSpec((tq, B_V), lambda b, h, i: (b * nq + i, h)),
                   pl.BlockSpec((1, tq, 1), lambda b, h, i: (h, b * nq + i, 0))],
        compiler_params=_params(),
    )(qcat, kcat, v)


def _attn_bwd(qcat, kcat, v, o, lse, do, nb, seq, tq):
    t = qcat.shape[0]
    nq = seq // tq

    def body(q_ref, k_ref, v_ref, o_ref, lse_ref, do_ref, dq_ref, dk_ref, dv_ref):
        i = pl.program_id(2)
        q, k = q_ref[...], k_ref[...]
        do_f = do_ref[...]
        delta = jnp.sum(do_f * o_ref[...], axis=-1, keepdims=True)
        dob = do_f.astype(BF16)
        p = jnp.exp(_dot_nt(q, k) - lse_ref[0])
        ds = (p * (_dot_nt(dob, v_ref[...]) - delta)).astype(BF16)
        dq_ref[...] = _dot(ds, k)
        dv_new = _dot_tn(p.astype(BF16), dob)
        dk_new = _dot_tn(ds, q)

        @pl.when(i == 0)
        def _():
            dv_ref[...] = dv_new
            dk_ref[...] = dk_new

        @pl.when(i > 0)
        def _():
            dv_ref[...] += dv_new
            dk_ref[...] += dk_new

    qspec = lambda wd: pl.BlockSpec((tq, wd), lambda b, h, i: (b * nq + i, h))
    kspec = lambda wd: pl.BlockSpec((seq, wd), lambda b, h, i: (b, h))
    return pl.pallas_call(
        body, name="attn_bwd", grid=(nb, B_HEADS, nq),
        out_shape=[jax.ShapeDtypeStruct((t, B_HEADS * QK_PAD), F32), jax.ShapeDtypeStruct((t, B_HEADS * QK_PAD), F32),
                   jax.ShapeDtypeStruct((t, B_HEADS * B_V), F32)],
        in_specs=[qspec(QK_PAD), kspec(QK_PAD), kspec(B_V), qspec(B_V),
                  pl.BlockSpec((1, tq, 1), lambda b, h, i: (h, b * nq + i, 0)), qspec(B_V)],
        out_specs=[qspec(QK_PAD), kspec(QK_PAD), kspec(B_V)],
        compiler_params=_params(),
    )(qcat, kcat, v, o, lse, do)


def _gla_consts(reverse):
    row = lax.broadcasted_iota(jnp.int32, (CHUNK, CHUNK), 0)
    col = lax.broadcasted_iota(jnp.int32, (CHUNK, CHUNK), 1)
    causal = (row <= col) if reverse else (row >= col)
    lane = lax.broadcasted_iota(jnp.int32, (1, HEAD_PAIR), 1)
    m0 = (lane < 64).astype(F32)
    m1 = 1.0 - m0
    r2 = lax.broadcasted_iota(jnp.int32, (HEAD_PAIR, HEAD_PAIR), 0)
    c2 = lax.broadcasted_iota(jnp.int32, (HEAD_PAIR, HEAD_PAIR), 1)
    same_head = ((r2 < 64) == (c2 < 64)).astype(F32)
    return causal, m0, m1, same_head


def _gla_chunk(hq, hi, z, l0, l1, st, consts, reverse):
    causal, m0, m1, same_head = consts
    mx = jnp.maximum(l0, l1)
    e0, e1 = jnp.exp(l0 - mx), jnp.exp(l1 - mx)
    lb = e0 / (e0 + e1)
    q = hq * _sigmoid(hq)
    log_f = jnp.log(lb + (1.0 - lb) * _sigmoid(z))
    k = (1.0 - lb) * _sigmoid(-z)
    cum = _cumsum_rows(log_f, reverse)
    tot = jnp.sum(log_f, axis=0, keepdims=True)
    q_dec = q * jnp.exp(cum)
    k_inv = k * jnp.exp(-cum)
    k_end = k * jnp.exp(tot - cum)
    o = _mm_nt(q_dec, st)
    for mh in (m0, m1):
        s = jnp.where(causal, _mm_nt(q_dec * mh, k_inv), 0.0)
        o = o + _mm(s, hi) * mh
    st_new = st * jnp.exp(tot) + _mm_tn(hi, k_end) * same_head
    return o, st_new


def _gla_fwd(hq, hi, z, lbl, nb, seq, group, reverse):
    t = hq.shape[0]
    rows = group * CHUNK
    nblk = seq // rows
    n_chunks = seq // CHUNK

    def tb(i):
        return nblk - 1 - i if reverse else i

    def body(hq_ref, hi_ref, z_ref, lbl_ref, o_ref, save_ref, st_ref):
        @pl.when(pl.program_id(2) == 0)
        def _():
            st_ref[...] = jnp.zeros_like(st_ref)

        consts = _gla_consts(reverse)
        l0, l1 = lbl_ref[0:1, :], lbl_ref[1:2, :]
        st = st_ref[...]
        for cc in range(group):
            c = group - 1 - cc if reverse else cc
            r = pl.ds(c * CHUNK, CHUNK)
            save_ref[0, 0, c] = st
            o_c, st = _gla_chunk(hq_ref[r, :], hi_ref[r, :], z_ref[r, :], l0, l1, st, consts, reverse)
            o_ref[r, :] = o_c
        st_ref[...] = st

    tok = pl.BlockSpec((rows, HEAD_PAIR), lambda b, p, i: (b * nblk + tb(i), p))
    return pl.pallas_call(
        body, name="gla_fwd_rev" if reverse else "gla_fwd", grid=(nb, 4, nblk),
        out_shape=[jax.ShapeDtypeStruct((t, A_WIDTH), F32),
                   jax.ShapeDtypeStruct((nb, 4, n_chunks, HEAD_PAIR, HEAD_PAIR), F32)],
        in_specs=[tok, tok, tok, pl.BlockSpec((2, HEAD_PAIR), lambda b, p, i: (0, p))],
        out_specs=[tok, pl.BlockSpec((1, 1, group, HEAD_PAIR, HEAD_PAIR), lambda b, p, i: (b, p, tb(i), 0, 0))],
        scratch_shapes=[pltpu.VMEM((HEAD_PAIR, HEAD_PAIR), F32)],
        compiler_params=_params(),
    )(hq, hi, z, lbl)


def _gla_bwd(hq, hi, z, lbl, saved, do, nb, seq, group, reverse):
    t = hq.shape[0]
    rows = group * CHUNK
    nblk = seq // rows

    def tb(i):
        return i if reverse else nblk - 1 - i

    def body(hq_ref, hi_ref, z_ref, lbl_ref, save_ref, do_ref, dq_ref, dv_ref, dz_ref, dl_ref, dst_ref):
        @pl.when(pl.program_id(2) == 0)
        def _():
            dst_ref[...] = jnp.zeros_like(dst_ref)
            dl_ref[...] = jnp.zeros_like(dl_ref)

        consts = _gla_consts(reverse)
        l0, l1 = lbl_ref[0:1, :], lbl_ref[1:2, :]
        dst = dst_ref[...]
        dl0 = jnp.zeros((1, HEAD_PAIR), F32)
        dl1 = jnp.zeros((1, HEAD_PAIR), F32)
        fn = lambda a, b, e, f0, f1, g: _gla_chunk(a, b, e, f0, f1, g, consts, reverse)
        for cc in range(group):
            c = cc if reverse else group - 1 - cc
            r = pl.ds(c * CHUNK, CHUNK)
            _, vjp = jax.vjp(fn, hq_ref[r, :], hi_ref[r, :], z_ref[r, :], l0, l1, save_ref[0, 0, c])
            d_hq, d_hi, d_z, d_l0, d_l1, dst = vjp((do_ref[r, :], dst))
            dq_ref[r, :] = d_hq
            dv_ref[r, :] = d_hi
            dz_ref[r, :] = d_z
            dl0 = dl0 + d_l0
            dl1 = dl1 + d_l1
        dst_ref[...] = dst
        dl_ref[0, 0:1, :] += dl0
        dl_ref[0, 1:2, :] += dl1

    tok = pl.BlockSpec((rows, HEAD_PAIR), lambda b, p, i: (b * nblk + tb(i), p))
    return pl.pallas_call(
        body, name="gla_bwd_rev" if reverse else "gla_bwd", grid=(nb, 4, nblk),
        out_shape=[jax.ShapeDtypeStruct((t, A_WIDTH), F32)] * 3 + [jax.ShapeDtypeStruct((nb, 2, A_WIDTH), F32)],
        in_specs=[tok, tok, tok, pl.BlockSpec((2, HEAD_PAIR), lambda b, p, i: (0, p)),
                  pl.BlockSpec((1, 1, group, HEAD_PAIR, HEAD_PAIR), lambda b, p, i: (b, p, tb(i), 0, 0)), tok],
        out_specs=[tok, tok, tok, pl.BlockSpec((1, 2, HEAD_PAIR), lambda b, p, i: (b, 0, p))],
        scratch_shapes=[pltpu.VMEM((HEAD_PAIR, HEAD_PAIR), F32)],
        compiler_params=_params(),
    )(hq, hi, z, lbl, saved, do)


def _head_mean_matrix():
    r = lax.broadcasted_iota(jnp.int32, (A_WIDTH, A_WIDTH), 0) // 64
    c = lax.broadcasted_iota(jnp.int32, (A_WIDTH, A_WIDTH), 1) // 64
    return jnp.where(r == c, 1.0 / 64.0, 0.0).astype(BF16)


def _gla_out(o_f, o_b, hg, g, mean_mat):
    o = o_f + o_b
    ms = _group_mean(o * o, mean_mat)
    return o * lax.rsqrt(ms + EPS) * g * (hg * _sigmoid(hg))


def _gla_combine(o_f, o_b, hg, g, tm):
    t = o_f.shape[0]

    def body(of_ref, ob_ref, hg_ref, g_ref, y_ref):
        y_ref[...] = _gla_out(of_ref[...], ob_ref[...], hg_ref[...], g_ref[...], _head_mean_matrix())

    tok = pl.BlockSpec((tm, A_WIDTH), lambda i: (i, 0))
    return pl.pallas_call(
        body, name="gla_combine_fwd", grid=(t // tm,), out_shape=jax.ShapeDtypeStruct((t, A_WIDTH), F32),
        in_specs=[tok, tok, tok, _const_spec((1, A_WIDTH))], out_specs=tok, compiler_params=_params(),
    )(o_f, o_b, hg, g)


def _gla_combine_bwd(o_f, o_b, hg, g, dy, tm):
    t = o_f.shape[0]

    def body(of_ref, ob_ref, hg_ref, g_ref, dy_ref, do_ref, dhg_ref, dg_ref):
        mean_mat = _head_mean_matrix()
        fn = lambda o, hgv, gv: _gla_out(o, jnp.zeros_like(o), hgv, gv, mean_mat)
        _, vjp = jax.vjp(fn, of_ref[...] + ob_ref[...], hg_ref[...], g_ref[...])
        d_o, d_hg, d_g = vjp(dy_ref[...])
        do_ref[...] = d_o
        dhg_ref[...] = d_hg

        @pl.when(pl.program_id(0) == 0)
        def _():
            dg_ref[...] = jnp.zeros_like(dg_ref)

        dg_ref[...] += d_g

    tok = pl.BlockSpec((tm, A_WIDTH), lambda i: (i, 0))
    vec = pl.BlockSpec((1, A_WIDTH), lambda i: (0, 0))
    return pl.pallas_call(
        body, name="gla_combine_bwd", grid=(t // tm,),
        out_shape=[jax.ShapeDtypeStruct((t, A_WIDTH), F32)] * 2 + [jax.ShapeDtypeStruct((1, A_WIDTH), F32)],
        in_specs=[tok, tok, tok, _const_spec((1, A_WIDTH)), tok], out_specs=[tok, tok, vec], compiler_params=_params(),
    )(o_f, o_b, hg, g, dy)


def _post_fwd(x, ya, oattn, tgt, g_mla, w_out, g2, w_gate, w_up, w_down, g_fin, tm):
    t = x.shape[0]

    def body(x_ref, ya_ref, oa_ref, tgt_ref, gm_ref, wo_ref, g2_ref, wg_ref, wu_ref, wd_ref, gf_ref, x1_ref, x2_ref, loss_ref):
        yb = _rms(oa_ref[...], gm_ref[...])
        x1 = x_ref[...] + _dot(ya_ref[...].astype(BF16), wo_ref[0:A_WIDTH, :]) + _dot(yb.astype(BF16), wo_ref[A_WIDTH:, :])
        x1_ref[...] = x1
        h2 = _rms(x1, g2_ref[...]).astype(BF16)
        gate = _dot(h2, wg_ref[...])
        act = (gate * _sigmoid(gate) * _dot(h2, wu_ref[...])).astype(BF16)
        x2 = x1 + _dot(act, wd_ref[...])
        x2_ref[...] = x2
        err = _rms(x2, gf_ref[...]) - tgt_ref[...]
        part = 0.5 * jnp.sum(jnp.mean(err * err, axis=-1, keepdims=True), axis=0, keepdims=True)

        @pl.when(pl.program_id(0) == 0)
        def _():
            loss_ref[...] = jnp.zeros_like(loss_ref)

        loss_ref[...] += jnp.broadcast_to(part, loss_ref.shape)

    tok = lambda wd: pl.BlockSpec((tm, wd), lambda i: (i, 0))
    return pl.pallas_call(
        body, name="post_fwd", grid=(t // tm,),
        out_shape=[jax.ShapeDtypeStruct((t, D_MODEL), F32)] * 2 + [jax.ShapeDtypeStruct((1, 128), F32)],
        in_specs=[tok(D_MODEL), tok(A_WIDTH), tok(512), tok(D_MODEL), _const_spec((1, 512)), _const_spec((D_MODEL, D_MODEL)),
                  _const_spec((1, D_MODEL)), _const_spec((D_MODEL, D_FF)), _const_spec((D_MODEL, D_FF)),
                  _const_spec((D_FF, D_MODEL)), _const_spec((1, D_MODEL))],
        out_specs=[tok(D_MODEL), tok(D_MODEL), pl.BlockSpec((1, 128), lambda i: (0, 0))],
        compiler_params=_params(),
    )(x, ya, oattn, tgt, g_mla, w_out, g2, w_gate, w_up, w_down, g_fin)


def _post_bwd(x1, x2, ya, oattn, tgt, g_mla, w_out, g2, w_gate, w_up, w_down, g_fin, tm):
    t = x1.shape[0]

    def body(x1_ref, x2_ref, ya_ref, oa_ref, tgt_ref, gm_ref, wo_ref, g2_ref, wg_ref, wu_ref, wd_ref, gf_ref,
             dx1_ref, dya_ref, doa_ref, ycat_ref, dx1b_ref, h2_ref, dgate_ref, dup_ref, act_ref, dx2b_ref,
             dgm_ref, dg2_ref, dgf_ref):
        x1, x2 = x1_ref[...], x2_ref[...]
        dy = (_rms(x2, gf_ref[...]) - tgt_ref[...]) * (1.0 / D_MODEL)
        dx2, dgf = _rms_bwd(x2, gf_ref[...], dy)
        dx2b = dx2.astype(BF16)
        dx2b_ref[...] = dx2b
        h2 = _rms(x1, g2_ref[...]).astype(BF16)
        h2_ref[...] = h2
        gate, up = _dot(h2, wg_ref[...]), _dot(h2, wu_ref[...])
        sg = _sigmoid(gate)
        sl = gate * sg
        act_ref[...] = (sl * up).astype(BF16)
        dact = _dot_nt(dx2b, wd_ref[...])
        dup = (dact * sl).astype(BF16)
        dgate = (dact * up * (sg * (1.0 + gate * (1.0 - sg)))).astype(BF16)
        dup_ref[...] = dup
        dgate_ref[...] = dgate
        dh2 = _dot_nt(dgate, wg_ref[...]) + _dot_nt(dup, wu_ref[...])
        dx1n, dg2 = _rms_bwd(x1, g2_ref[...], dh2)
        dx1 = dx2 + dx1n
        dx1_ref[...] = dx1
        dx1b = dx1.astype(BF16)
        dx1b_ref[...] = dx1b
        oa = oa_ref[...]
        ycat_ref[:, 0:A_WIDTH] = ya_ref[...].astype(BF16)
        ycat_ref[:, A_WIDTH:] = _rms(oa, gm_ref[...]).astype(BF16)
        dya_ref[...] = _dot_nt(dx1b, wo_ref[0:A_WIDTH, :])
        doa, dgm = _rms_bwd(oa, gm_ref[...], _dot_nt(dx1b, wo_ref[A_WIDTH:, :]))
        doa_ref[...] = doa

        @pl.when(pl.program_id(0) == 0)
        def _():
            dgm_ref[...] = jnp.zeros_like(dgm_ref)
            dg2_ref[...] = jnp.zeros_like(dg2_ref)
            dgf_ref[...] = jnp.zeros_like(dgf_ref)

        dgm_ref[...] += dgm
        dg2_ref[...] += dg2
        dgf_ref[...] += dgf

    tok = lambda wd: pl.BlockSpec((tm, wd), lambda i: (i, 0))
    vec = lambda wd: pl.BlockSpec((1, wd), lambda i: (0, 0))
    sds = lambda wd, dt: jax.ShapeDtypeStruct((t, wd), dt)
    return pl.pallas_call(
        body, name="post_bwd", grid=(t // tm,),
        out_shape=[sds(D_MODEL, F32), sds(512, F32), sds(512, F32), sds(D_MODEL, BF16), sds(D_MODEL, BF16), sds(D_MODEL, BF16),
                   sds(D_FF, BF16), sds(D_FF, BF16), sds(D_FF, BF16), sds(D_MODEL, BF16),
                   jax.ShapeDtypeStruct((1, 512), F32), jax.ShapeDtypeStruct((1, D_MODEL), F32), jax.ShapeDtypeStruct((1, D_MODEL), F32)],
        in_specs=[tok(D_MODEL), tok(D_MODEL), tok(512), tok(512), tok(D_MODEL), _const_spec((1, 512)),
                  _const_spec((D_MODEL, D_MODEL)), _const_spec((1, D_MODEL)), _const_spec((D_MODEL, D_FF)),
                  _const_spec((D_MODEL, D_FF)), _const_spec((D_FF, D_MODEL)), _const_spec((1, D_MODEL))],
        out_specs=[tok(D_MODEL), tok(512), tok(512), tok(D_MODEL), tok(D_MODEL), tok(D_MODEL), tok(D_FF), tok(D_FF), tok(D_FF),
                   tok(D_MODEL), vec(512), vec(D_MODEL), vec(D_MODEL)],
        compiler_params=_params(),
    )(x1, x2, ya, oattn, tgt, g_mla, w_out, g2, w_gate, w_up, w_down, g_fin)


def _matmul_tn(a, b, tn, tt, tag):
    t, k = a.shape
    n = b.shape[1]

    def body(a_ref, b_ref, o_ref):
        part = _dot_tn(a_ref[...], b_ref[...])

        @pl.when(pl.program_id(1) == 0)
        def _():
            o_ref[...] = part

        @pl.when(pl.program_id(1) > 0)
        def _():
            o_ref[...] += part

    return pl.pallas_call(
        body, name="wgrad_" + tag, grid=(n // tn, t // tt), out_shape=jax.ShapeDtypeStruct((k, n), F32),
        in_specs=[pl.BlockSpec((tt, k), lambda j, i: (i, 0)), pl.BlockSpec((tt, tn), lambda j, i: (i, j))],
        out_specs=pl.BlockSpec((k, tn), lambda j, i: (0, j)), compiler_params=_params(),
    )(a, b)


def _mla_qkv_bwd(cq, ckv, g_qa, g_kva, w_q, w_k, w_v, tables, dq, dk, dv, seq, tm):
    t = cq.shape[0]
    nblk = seq // tm

    def body(cq_ref, ckv_ref, gq_ref, gk_ref, wq_ref, wk_ref, wv_ref, c_ref, sa_ref, sb_ref, dq_ref, dk_ref, dv_ref,
             dcq_ref, dckv_ref, dkr_ref, cqn_ref, dqf_ref, ckn_ref, dkn_ref, dvb_ref, dgq_ref, dgk_ref):
        cos_t, sin_a, sin_b = c_ref[...], sa_ref[...], sb_ref[...]
        cqn_ref[...] = _rms(cq_ref[...], gq_ref[...]).astype(BF16)
        ckn_ref[...] = _rms(ckv_ref[...], gk_ref[...]).astype(BF16)
        dkr = jnp.zeros((tm, 128), F32)
        for h in range(B_HEADS):
            lo = h * QK_PAD
            dqf_ref[:, lo:lo + 128] = (dq_ref[:, lo:lo + 128] * ATTN_SCALE).astype(BF16)
            dqf_ref[:, lo + 128:lo + 256] = _rope_t(dq_ref[:, lo + 128:lo + 256] * ATTN_SCALE, cos_t, sin_a, sin_b).astype(BF16)
            dkn_ref[:, h * 128:(h + 1) * 128] = dk_ref[:, lo:lo + 128].astype(BF16)
            dkr = dkr + dk_ref[:, lo + 128:lo + 256]
        dkr_ref[...] = _rope_t(dkr, cos_t, sin_a, sin_b)
        dvb = dv_ref[...].astype(BF16)
        dvb_ref[...] = dvb
        dcq, dgq = _rms_bwd(cq_ref[...], gq_ref[...], _dot_nt(dqf_ref[...], wq_ref[...]))
        dckv, dgk = _rms_bwd(ckv_ref[...], gk_ref[...], _dot_nt(dkn_ref[...], wk_ref[...]) + _dot_nt(dvb, wv_ref[...]))
        dcq_ref[...] = dcq
        dckv_ref[...] = dckv

        @pl.when(pl.program_id(0) == 0)
        def _():
            dgq_ref[...] = jnp.zeros_like(dgq_ref)
            dgk_ref[...] = jnp.zeros_like(dgk_ref)

        dgq_ref[...] += dgq
        dgk_ref[...] += dgk

    tok = lambda wd: pl.BlockSpec((tm, wd), lambda i: (i, 0))
    vec = lambda wd: pl.BlockSpec((1, wd), lambda i: (0, 0))
    tab = pl.BlockSpec((tm, 128), lambda i: (i % nblk, 0))
    sds = lambda wd, dt: jax.ShapeDtypeStruct((t, wd), dt)
    return pl.pallas_call(
        body, name="mla_qkv_bwd", grid=(t // tm,),
        out_shape=[sds(Q_LORA, F32), sds(KV_LORA, F32), sds(128, F32), sds(Q_LORA, BF16), sds(1024, BF16), sds(KV_LORA, BF16),
                   sds(512, BF16), sds(512, BF16), jax.ShapeDtypeStruct((1, Q_LORA), F32), jax.ShapeDtypeStruct((1, KV_LORA), F32)],
        in_specs=[tok(Q_LORA), tok(KV_LORA), _const_spec((1, Q_LORA)), _const_spec((1, KV_LORA)),
                  _const_spec((Q_LORA, 1024)), _const_spec((KV_LORA, 512)), _const_spec((KV_LORA, 512)), tab, tab, tab,
                  tok(1024), tok(1024), tok(512)],
        out_specs=[tok(Q_LORA), tok(KV_LORA), tok(128), tok(Q_LORA), tok(1024), tok(KV_LORA), tok(512), tok(512),
                   vec(Q_LORA), vec(KV_LORA)],
        compiler_params=_params(),
    )(cq, ckv, g_qa, g_kva, w_q, w_k, w_v, *tables, dq, dk, dv)


def _inproj_bwd(x, g1, w_in, dx1, pieces, tm):
    t = x.shape[0]
    counts = [len(p) for p in pieces]
    flat = [a for p in pieces for a in p]
    widths = [wd for wd, p in zip(IN_WIDTHS, pieces) for _ in p]

    def body(x_ref, g_ref, w_ref, dx1_ref, *refs):
        ins = refs[:len(flat)]
        dx_ref, h_ref, dp_ref, dg_ref = refs[len(flat):]
        xv = x_ref[...]
        h_ref[...] = _rms(xv, g_ref[...]).astype(BF16)
        off, j = 0, 0
        for wd, cnt in zip(IN_WIDTHS, counts):
            acc = ins[j][...]
            for jj in range(1, cnt):
                acc = acc + ins[j + jj][...]
            dp_ref[:, off:off + wd] = acc.astype(BF16)
            off += wd
            j += cnt
        dxn, dg = _rms_bwd(xv, g_ref[...], _dot_nt(dp_ref[...], w_ref[...]))
        dx_ref[...] = dx1_ref[...] + dxn

        @pl.when(pl.program_id(0) == 0)
        def _():
            dg_ref[...] = jnp.zeros_like(dg_ref)

        dg_ref[...] += dg

    tok = lambda wd: pl.BlockSpec((tm, wd), lambda i: (i, 0))
    return pl.pallas_call(
        body, name="inproj_bwd", grid=(t // tm,),
        out_shape=[jax.ShapeDtypeStruct((t, D_MODEL), F32), jax.ShapeDtypeStruct((t, D_MODEL), BF16),
                   jax.ShapeDtypeStruct((t, D_IN_PAD), BF16), jax.ShapeDtypeStruct((1, D_MODEL), F32)],
        in_specs=[tok(D_MODEL), _const_spec((1, D_MODEL)), _const_spec((D_MODEL, D_IN_PAD)), tok(D_MODEL)] + [tok(wd) for wd in widths],
        out_specs=[tok(D_MODEL), tok(D_MODEL), tok(D_IN_PAD), pl.BlockSpec((1, D_MODEL), lambda i: (0, 0))],
        compiler_params=_params(),
    )(x, g1, w_in, dx1, *flat)


def _pack_shards(shards):
    return jnp.concatenate([s.reshape(-1, 128) for s in shards], axis=0)


def _unpack_full(gathered):
    out, r0 = [], 0
    for (_, r, c, axis), nrows in zip(BIG, BIG_ROWS):
        blk = gathered[:, r0:r0 + nrows]
        if axis == 1:
            full = blk.reshape(N_DEV, r, c // N_DEV).transpose(1, 0, 2).reshape(r, c)
        else:
            full = blk.reshape(r, c)
        out.append(full)
        r0 += nrows
    return out


def _pack_full_grads(grads):
    parts = []
    for (_, r, c, axis), g in zip(BIG, grads):
        if axis == 1:
            parts.append(g.reshape(r, N_DEV, c // N_DEV).transpose(1, 0, 2).reshape(N_DEV, -1, 128))
        else:
            parts.append(g.reshape(N_DEV, -1, 128))
    return jnp.concatenate(parts, axis=1)


def _unpack_shards(packed):
    out, r0 = [], 0
    for (_, r, c, axis), nrows in zip(BIG, BIG_ROWS):
        shape = (1, r, c // N_DEV) if axis == 1 else (1, r // N_DEV, c)
        out.append(packed[r0:r0 + nrows].reshape(shape))
        r0 += nrows
    return out


def _arrange_weights(w_in, w_q_b, w_kv_b):
    w_in_arr = jnp.concatenate([w_in, jnp.zeros((D_MODEL, D_IN_PAD - D_IN), w_in.dtype)], axis=1)
    q3 = w_q_b.reshape(Q_LORA, B_HEADS, B_NOPE + B_ROPE)
    w_q = jnp.concatenate([q3, jnp.zeros((Q_LORA, B_HEADS, QK_PAD - B_NOPE - B_ROPE), w_q_b.dtype)], axis=2).reshape(Q_LORA, B_HEADS * QK_PAD)
    kv3 = w_kv_b.reshape(KV_LORA, B_HEADS, B_NOPE + B_V)
    w_k = kv3[:, :, :B_NOPE].reshape(KV_LORA, B_HEADS * B_NOPE)
    w_v = kv3[:, :, B_NOPE:].reshape(KV_LORA, B_HEADS * B_V)
    return w_in_arr, w_q, w_k, w_v


def _unarrange_grads(d_in_arr, d_q, d_k, d_v):
    d_in = d_in_arr[:, :D_IN]
    d_qb = d_q.reshape(Q_LORA, B_HEADS, QK_PAD)[:, :, :B_NOPE + B_ROPE].reshape(Q_LORA, B_HEADS * (B_NOPE + B_ROPE))
    d_kvb = jnp.concatenate([d_k.reshape(KV_LORA, B_HEADS, B_NOPE), d_v.reshape(KV_LORA, B_HEADS, B_V)], axis=2).reshape(KV_LORA, -1)
    return d_in, d_qb, d_kvb


def _rows128(a, rows):
    flat = a.reshape(-1, 128)
    pad = rows - flat.shape[0]
    return flat if pad == 0 else jnp.concatenate([flat, jnp.zeros((pad, 128), flat.dtype)], axis=0)


def _step_core(x, loss_target, small_w, lb_full, full_w, seq, group, tiles):
    g1, g_hgrn, g_qa, g_kva, g_mla, g2, g_fin = small_w
    w_in_f, w_qb_f, w_kvb_f, w_out, w_gate, w_up, w_down = full_w
    w_in, w_q, w_k, w_v = _arrange_weights(w_in_f, w_qb_f, w_kvb_f)
    nb = x.shape[0]
    t = nb * seq
    tm, tq_f, tq_b, tt = tiles
    xt = x.reshape(t, D_MODEL)
    tgt = loss_target.reshape(t, D_MODEL)
    tables = _rope_tables(seq)

    hq, hi, zf, zb, hg, cq, ckv, kr = _inproj(xt, g1, w_in, tm)
    lbl_f, lbl_b = lb_full[0], lb_full[1]
    o_f, save_f = _gla_fwd(hq, hi, zf, lbl_f, nb, seq, group, False)
    o_b, save_b = _gla_fwd(hq, hi, zb, lbl_b, nb, seq, group, True)
    ya = _gla_combine(o_f, o_b, hg, g_hgrn, tm)
    qcat, kcat, vv = _mla_qkv(cq, ckv, kr, g_qa, g_kva, w_q, w_k, w_v, tables, seq, tm)
    oattn, lse = _attn_fwd(qcat, kcat, vv, nb, seq, tq_f)
    x1, x2, loss_row = _post_fwd(xt, ya, oattn, tgt, g_mla, w_out, g2, w_gate, w_up, w_down, g_fin, tm)

    (dx1, d_ya, d_oattn, ycat_b, dx1_b, h2_b, dgate_b, dup_b, act_b, dx2_b, d_g_mla, d_g2, d_g_fin) = _post_bwd(
        x1, x2, ya, oattn, tgt, g_mla, w_out, g2, w_gate, w_up, w_down, g_fin, tm)
    d_w_gate = _matmul_tn(h2_b, dgate_b, D_FF // 2, tt, "gate")
    d_w_up = _matmul_tn(h2_b, dup_b, D_FF // 2, tt, "up")
    d_w_down = _matmul_tn(act_b, dx2_b, 512, tt, "down")
    d_w_out = _matmul_tn(ycat_b, dx1_b, D_MODEL, tt, "out")
    dq, dk, dv = _attn_bwd(qcat, kcat, vv, oattn, lse, d_oattn, nb, seq, tq_b)
    (d_cq, d_ckv, d_kr, cqn_b, dqf_b, ckn_b, dkn_b, dvb_b, d_g_qa, d_g_kva) = _mla_qkv_bwd(
        cq, ckv, g_qa, g_kva, w_q, w_k, w_v, tables, dq, dk, dv, seq, tm)
    d_w_q = _matmul_tn(cqn_b, dqf_b, B_HEADS * QK_PAD, tt, "q_b")
    d_w_k = _matmul_tn(ckn_b, dkn_b, 512, tt, "kv_b_k")
    d_w_v = _matmul_tn(ckn_b, dvb_b, 512, tt, "kv_b_v")
    d_o, d_hg, d_g_hgrn = _gla_combine_bwd(o_f, o_b, hg, g_hgrn, d_ya, tm)
    dq_f, dv_f, dz_f, dl_f = _gla_bwd(hq, hi, zf, lbl_f, save_f, d_o, nb, seq, group, False)
    dq_b, dv_b, dz_b, dl_b = _gla_bwd(hq, hi, zb, lbl_b, save_b, d_o, nb, seq, group, True)
    grad_x, h1_b, dproj_b, d_g1 = _inproj_bwd(
        xt, g1, w_in, dx1, [[dq_f, dq_b], [dv_f, dv_b], [dz_f], [dz_b], [d_hg], [d_cq], [d_ckv], [d_kr]], tm)
    d_w_in_arr = _matmul_tn(h1_b, dproj_b, D_IN_PAD // 2, tt, "in")

    d_w_in, d_w_qb, d_w_kvb = _unarrange_grads(d_w_in_arr, d_w_q, d_w_k, d_w_v)
    d_lb = jnp.stack([jnp.sum(dl_f, axis=0), jnp.sum(dl_b, axis=0)], axis=0)
    big_grads = [d_w_in, d_w_qb, d_w_kvb, d_w_out, d_w_gate, d_w_up, d_w_down]
    small_grads = [d_g1, d_g_hgrn, d_g_qa, d_g_kva, d_g_mla, d_g2, d_g_fin]
    return loss_row[0, 0], grad_x.reshape(nb, seq, D_MODEL), big_grads, small_grads, d_lb


def kernel(x, norm1_g, w_in, lb_logits, hgrn_norm_g, q_a_norm_g, w_q_b, kv_a_norm_g, w_kv_b, mla_norm_g, w_out, norm2_g, w_gate, w_up, w_down, final_norm_g, loss_target, m_norm1_g, m_w_in, m_lb_logits, m_hgrn_norm_g, m_q_a_norm_g, m_w_q_b, m_kv_a_norm_g, m_w_kv_b, m_mla_norm_g, m_w_out, m_norm2_g, m_w_gate, m_w_up, m_w_down, m_final_norm_g, v_norm1_g, v_w_in, v_lb_logits, v_hgrn_norm_g, v_q_a_norm_g, v_w_q_b, v_kv_a_norm_g, v_w_kv_b, v_mla_norm_g, v_w_out, v_norm2_g, v_w_gate, v_w_up, v_w_down, v_final_norm_g):
    big_w = [w_in, w_q_b, w_kv_b, w_out, w_gate, w_up, w_down]
    big_m = [m_w_in, m_w_q_b, m_w_kv_b, m_w_out, m_w_gate, m_w_up, m_w_down]
    big_v = [v_w_in, v_w_q_b, v_w_kv_b, v_w_out, v_w_gate, v_w_up, v_w_down]
    small_w = [norm1_g, hgrn_norm_g, q_a_norm_g, kv_a_norm_g, mla_norm_g, norm2_g, final_norm_g]
    small_m = [m_norm1_g, m_hgrn_norm_g, m_q_a_norm_g, m_kv_a_norm_g, m_mla_norm_g, m_norm2_g, m_final_norm_g]
    small_v = [v_norm1_g, v_hgrn_norm_g, v_q_a_norm_g, v_kv_a_norm_g, v_mla_norm_g, v_norm2_g, v_final_norm_g]
    seq = x.shape[1]
    my_id = 4 * lax.axis_index("x") + 2 * lax.axis_index("y") + lax.axis_index("c")

    wpack = _pack_shards(big_w).astype(BF16)
    lb_rows = _rows128(lb_logits, 8)
    gathered, lb_gathered = _all_gather([wpack, lb_rows])
    full_w = _unpack_full(gathered)
    lb_full = lb_gathered[:, :2].reshape(N_DEV, 2, 2, 64).transpose(1, 2, 0, 3).reshape(2, 2, 512)

    loss_part, grad_x, big_grads, small_grads, d_lb = _step_core(
        x, loss_target, [s.reshape(1, -1) for s in small_w], lb_full, full_w, seq, 8, (256, 256, 128, 512))

    gpack = _pack_full_grads(big_grads).astype(BF16)
    spack = _rows128(jnp.concatenate([g.reshape(-1) for g in small_grads] + [d_lb.reshape(-1)]), SMALL_PACK_ROWS)
    recv, recv_small = _grad_exchange(gpack, spack)
    gsum = _sum_slots(recv, PACK_ROWS // 10)
    ssum = _sum_slots(recv_small, SMALL_PACK_ROWS)
    g_big = _unpack_shards(gsum)
    g_small, r0 = [], 0
    for s, (_, n) in zip(small_w, SMALL):
        g_small.append(ssum[r0:r0 + n // 128].reshape(s.shape))
        r0 += n // 128
    g_lb_full = ssum[r0:r0 + LB_ROWS].reshape(2, 2, N_DEV, 64)
    g_lb = lax.dynamic_index_in_dim(g_lb_full, my_id, axis=2, keepdims=False)

    deltas, new_ms, new_vs = {}, {}, {}
    for (name, _, _, _), w, g, m, v in zip(BIG, big_w, g_big, big_m, big_v):
        d, nm, nv = _adamw(w[0], g[0], m[0], v[0], name)
        deltas[name], new_ms[name], new_vs[name] = d[None], nm[None], nv[None]
    pack_small = lambda arrs: _rows128(jnp.concatenate([a.reshape(-1) for a in arrs]), 40)
    d, nm, nv = _adamw(pack_small(small_w), pack_small(g_small), pack_small(small_m), pack_small(small_v), "gains")
    r0 = 0
    for s, (name, n) in zip(small_w, SMALL):
        sl = slice(r0, r0 + n // 128)
        deltas[name], new_ms[name], new_vs[name] = d[sl].reshape(s.shape), nm[sl].reshape(s.shape), nv[sl].reshape(s.shape)
        r0 += n // 128
    as_rows = lambda a: a.reshape(2, 128)
    d, nm, nv = _adamw(as_rows(lb_logits), as_rows(g_lb), as_rows(m_lb_logits), as_rows(v_lb_logits), "lb_logits")
    deltas["lb_logits"], new_ms["lb_logits"], new_vs["lb_logits"] = (a.reshape(lb_logits.shape) for a in (d, nm, nv))

    loss = lax.psum(loss_part, ("x", "y", "c"))
    grads = dict(zip([n for n, _, _, _ in BIG], g_big))
    grads.update(zip([n for n, _ in SMALL], g_small))
    grads["lb_logits"] = g_lb
    order = ["norm1_g", "w_in", "lb_logits", "hgrn_norm_g", "q_a_norm_g", "w_q_b", "kv_a_norm_g", "w_kv_b", "mla_norm_g",
             "w_out", "norm2_g", "w_gate", "w_up", "w_down", "final_norm_g"]
    return (loss, grad_x, *[grads[n] for n in order], *[deltas[n] for n in order],
            *[new_ms[n] for n in order], *[new_vs[n] for n in order])
```

```python
import functools
import math

import jax
import jax.numpy as jnp
from jax import lax
from jax.experimental import pallas as pl
from jax.experimental.pallas import tpu as pltpu

F32 = jnp.float32
BF16 = jnp.bfloat16

N_DEV = 8
D_MODEL = 1024
D_FF = 2816
A_WIDTH = 512
HEAD_PAIR = 128
CHUNK = 64
B_HEADS = 4
B_NOPE = 128
B_ROPE = 64
B_V = 128
QK_PAD = 256
Q_LORA = 384
KV_LORA = 256
D_IN = 3264
D_IN_PAD = 3328
IN_WIDTHS = (512, 512, 512, 512, 512, Q_LORA, KV_LORA, 128)
ROPE_THETA = 10000.0
EPS = 1e-6
ATTN_SCALE = (B_NOPE + B_ROPE) ** -0.5
ATTN_SUB = 256
ATTN_SUB_BWD = 256
ADAM_LR, ADAM_B1, ADAM_B2, ADAM_EPS, ADAM_WD, ADAM_STEP = 0.001, 0.9, 0.999, 1e-08, 0.01, 10
VMEM_LIMIT = 56 * 1024 * 1024
MESH = pl.DeviceIdType.MESH

BIG = (("w_in", 1024, D_IN, 1), ("w_q_b", Q_LORA, 768, 1), ("w_kv_b", KV_LORA, 1024, 1), ("w_out", 1024, 1024, 0),
       ("w_gate", 1024, D_FF, 1), ("w_up", 1024, D_FF, 1), ("w_down", D_FF, 1024, 0))
BIG_ROWS = tuple(r * c // N_DEV // 128 for _, r, c, _ in BIG)
PACK_ROWS = sum(BIG_ROWS)
SMALL = (("norm1_g", 1024), ("hgrn_norm_g", 512), ("q_a_norm_g", 384), ("kv_a_norm_g", 256), ("mla_norm_g", 512),
         ("norm2_g", 1024), ("final_norm_g", 1024))
SMALL_ROWS = sum(n // 128 for _, n in SMALL)
LB_ROWS = 16
SMALL_PACK_ROWS = 56


def _params(**kw):
    return pltpu.CompilerParams(vmem_limit_bytes=VMEM_LIMIT, **kw)


def _const_spec(shape):
    return pl.BlockSpec(shape, lambda *_: (0,) * len(shape), pipeline_mode=pl.Buffered(1))


def _dot(a, b):
    return jnp.dot(a, b, preferred_element_type=F32)


def _dot_nt(a, b):
    return lax.dot_general(a, b, (((1,), (1,)), ((), ())), preferred_element_type=F32)


def _dot_tn(a, b):
    return lax.dot_general(a, b, (((0,), (0,)), ((), ())), preferred_element_type=F32)


@jax.custom_vjp
def _mm(a, b):
    return _dot(a.astype(BF16), b.astype(BF16))


def _mm_fwd(a, b):
    return _mm(a, b), (a, b)


def _mm_bwd(res, g):
    a, b = res
    gb = g.astype(BF16)
    return _dot_nt(gb, b.astype(BF16)), _dot_tn(a.astype(BF16), gb)


_mm.defvjp(_mm_fwd, _mm_bwd)


@jax.custom_vjp
def _mm_nt(a, b):
    return _dot_nt(a.astype(BF16), b.astype(BF16))


def _mm_nt_fwd(a, b):
    return _mm_nt(a, b), (a, b)


def _mm_nt_bwd(res, g):
    a, b = res
    gb = g.astype(BF16)
    return _dot(gb, b.astype(BF16)), _dot_tn(gb, a.astype(BF16))


_mm_nt.defvjp(_mm_nt_fwd, _mm_nt_bwd)


@jax.custom_vjp
def _mm_tn(a, b):
    return _dot_tn(a.astype(BF16), b.astype(BF16))


def _mm_tn_fwd(a, b):
    return _mm_tn(a, b), (a, b)


def _mm_tn_bwd(res, g):
    a, b = res
    gb = g.astype(BF16)
    return _dot_nt(b.astype(BF16), gb), _dot(a.astype(BF16), gb)


_mm_tn.defvjp(_mm_tn_fwd, _mm_tn_bwd)


def _split3(a):
    hi = a.astype(BF16)
    r = a - hi.astype(F32)
    mid = r.astype(BF16)
    lo = (r - mid.astype(F32)).astype(BF16)
    return hi, mid, lo


def _dot_exact_rhs(a, m):
    hi, mid, lo = _split3(a)
    return _dot(hi, m) + _dot(mid, m) + _dot(lo, m)


@jax.custom_vjp
def _group_mean(a, m):
    return _dot_exact_rhs(a, m)


def _group_mean_fwd(a, m):
    return _group_mean(a, m), m


def _group_mean_bwd(m, g):
    return _dot_exact_rhs(g, m), jnp.zeros_like(m)


_group_mean.defvjp(_group_mean_fwd, _group_mean_bwd)


def _roll_rows(a, shift):
    return pltpu.roll(a, shift, 0)


def _cumsum_rows_raw(a, reverse):
    n = a.shape[0]
    row = lax.broadcasted_iota(jnp.int32, a.shape, 0)
    s = 1
    while s < n:
        if reverse:
            a = a + jnp.where(row < n - s, _roll_rows(a, n - s), 0.0)
        else:
            a = a + jnp.where(row >= s, _roll_rows(a, s), 0.0)
        s *= 2
    return a


@functools.partial(jax.custom_vjp, nondiff_argnums=(1,))
def _cumsum_rows(a, reverse):
    return _cumsum_rows_raw(a, reverse)


def _cumsum_rows_fwd(a, reverse):
    return _cumsum_rows_raw(a, reverse), None


def _cumsum_rows_bwd(reverse, _, g):
    return (_cumsum_rows_raw(g, not reverse),)


_cumsum_rows.defvjp(_cumsum_rows_fwd, _cumsum_rows_bwd)


def _rms(x, g):
    r = lax.rsqrt(jnp.mean(x * x, axis=-1, keepdims=True) + EPS)
    return x * r * g


def _rms_bwd(x, g, dy):
    r = lax.rsqrt(jnp.mean(x * x, axis=-1, keepdims=True) + EPS)
    xh = x * r
    dg = jnp.sum(dy * xh, axis=0, keepdims=True)
    dxh = dy * g
    dx = r * (dxh - xh * jnp.mean(dxh * xh, axis=-1, keepdims=True))
    return dx, dg


def _sigmoid(a):
    return jax.nn.sigmoid(a)


def _mesh_place():
    x, y, c = lax.axis_index("x"), lax.axis_index("y"), lax.axis_index("c")
    return x, y, c


def _dev_index(p):
    return 4 * p[0] + 2 * p[1] + p[2]


def _all_gather(blocks):
    n = len(blocks)

    def body(*refs):
        ins, outs = refs[:n], refs[n:2 * n]
        send_sems, recv_sems, local_sems = refs[2 * n:]
        x, y, c = _mesh_place()
        me, sibling = (x, y, c), (x, y, 1 - c)
        chips = [(1 - x, y), (x, 1 - y), (1 - x, 1 - y)]

        def copy(a, k, block, to, src=None):
            rows = outs[a].at[_dev_index(block)]
            return pltpu.make_async_remote_copy(
                src_ref=rows if src is None else src, dst_ref=rows,
                send_sem=send_sems.at[a, k], recv_sem=recv_sems.at[a, k],
                device_id=to, device_id_type=MESH)

        started = []
        for a in range(n):
            mine = pltpu.make_async_copy(ins[a], outs[a].at[_dev_index(me)], local_sems.at[a])
            mine.start()
            first = [copy(a, 0, me, sibling, src=ins[a])]
            first += [copy(a, 1 + j, me, (*chip, c), src=ins[a]) for j, chip in enumerate(chips)]
            for cp in first:
                cp.start()
            started += [mine.wait] + [cp.wait_send for cp in first]
        for a in range(n):
            for j, chip in enumerate(chips):
                copy(a, 1 + j, (*chip, c), me).wait_recv()
                passed = copy(a, 4 + j, (*chip, c), sibling)
                passed.start()
                started.append(passed.wait_send)
        for a in range(n):
            copy(a, 0, sibling, me).wait_recv()
            for j, chip in enumerate(chips):
                copy(a, 4 + j, (*chip, 1 - c), me).wait_recv()
        for wait in started:
            wait()

    any_spec = pl.BlockSpec(memory_space=pl.ANY)
    return pl.pallas_call(
        body, name="weights_all_gather",
        out_shape=[jax.ShapeDtypeStruct((N_DEV,) + b.shape, b.dtype) for b in blocks],
        in_specs=[any_spec] * n, out_specs=[any_spec] * n,
        scratch_shapes=[pltpu.SemaphoreType.DMA((n, 7)), pltpu.SemaphoreType.DMA((n, 7)), pltpu.SemaphoreType.DMA((n,))],
    )(*blocks)


def _grad_exchange(gpack, small):
    def body(g_ref, s_ref, rg_ref, rs_ref, send_sems, recv_sems, local_sems):
        x, y, c = _mesh_place()
        me = (x, y, c)
        my_id = _dev_index(me)
        rels = [(dx, dy, dc) for dx in (0, 1) for dy in (0, 1) for dc in (0, 1)][1:]

        def peer_of(rel):
            return tuple(1 - v if d else v for v, d in zip(me, rel))

        def copies(k, rel):
            peer = peer_of(rel)
            big = pltpu.make_async_remote_copy(
                src_ref=g_ref.at[_dev_index(peer)], dst_ref=rg_ref.at[my_id],
                send_sem=send_sems.at[0, k], recv_sem=recv_sems.at[0, k], device_id=peer, device_id_type=MESH)
            sml = pltpu.make_async_remote_copy(
                src_ref=s_ref, dst_ref=rs_ref.at[my_id],
                send_sem=send_sems.at[1, k], recv_sem=recv_sems.at[1, k], device_id=peer, device_id_type=MESH)
            return big, sml

        def arrivals(k, rel):
            peer = peer_of(rel)
            big = pltpu.make_async_remote_copy(
                src_ref=g_ref.at[my_id], dst_ref=rg_ref.at[_dev_index(peer)],
                send_sem=send_sems.at[0, k], recv_sem=recv_sems.at[0, k], device_id=peer, device_id_type=MESH)
            sml = pltpu.make_async_remote_copy(
                src_ref=s_ref, dst_ref=rs_ref.at[_dev_index(peer)],
                send_sem=send_sems.at[1, k], recv_sem=recv_sems.at[1, k], device_id=peer, device_id_type=MESH)
            return big, sml

        own_g = pltpu.make_async_copy(g_ref.at[my_id], rg_ref.at[my_id], local_sems.at[0])
        own_s = pltpu.make_async_copy(s_ref, rs_ref.at[my_id], local_sems.at[1])
        own_g.start()
        own_s.start()
        sent = []
        for k, rel in enumerate(rels):
            for cp in copies(k, rel):
                cp.start()
                sent.append(cp)
        for k, rel in enumerate(rels):
            for cp in arrivals(k, rel):
                cp.wait_recv()
        for cp in sent:
            cp.wait_send()
        own_g.wait()
        own_s.wait()

    any_spec = pl.BlockSpec(memory_space=pl.ANY)
    return pl.pallas_call(
        body, name="grad_exchange",
        out_shape=[jax.ShapeDtypeStruct(gpack.shape, gpack.dtype), jax.ShapeDtypeStruct((N_DEV,) + small.shape, small.dtype)],
        in_specs=[any_spec] * 2, out_specs=[any_spec] * 2,
        scratch_shapes=[pltpu.SemaphoreType.DMA((2, 7)), pltpu.SemaphoreType.DMA((2, 7)), pltpu.SemaphoreType.DMA((2,))],
    )(gpack, small)


def _sum_slots(recv, rows_per_step):
    _, r, _ = recv.shape

    def body(in_ref, out_ref):
        acc = in_ref[0].astype(F32)
        for j in range(1, N_DEV):
            acc = acc + in_ref[j].astype(F32)
        out_ref[...] = acc

    return pl.pallas_call(
        body, name="grad_sum_slots_%d" % r,
        out_shape=jax.ShapeDtypeStruct((r, 128), F32), grid=(r // rows_per_step,),
        in_specs=[pl.BlockSpec((N_DEV, rows_per_step, 128), lambda i: (0, i, 0))],
        out_specs=pl.BlockSpec((rows_per_step, 128), lambda i: (i, 0)),
        compiler_params=_params(),
    )(recv)


def _adamw(w, g, m, v, tag):
    bc1 = 1.0 - ADAM_B1 ** ADAM_STEP
    bc2 = 1.0 - ADAM_B2 ** ADAM_STEP

    def body(w_ref, g_ref, m_ref, v_ref, d_ref, nm_ref, nv_ref):
        gg = g_ref[...]
        nm = ADAM_B1 * m_ref[...] + (1.0 - ADAM_B1) * gg
        nv = ADAM_B2 * v_ref[...] + (1.0 - ADAM_B2) * (gg * gg)
        d_ref[...] = -ADAM_LR * ((nm / bc1) / (jnp.sqrt(nv / bc2) + ADAM_EPS) + ADAM_WD * w_ref[...])
        nm_ref[...] = nm
        nv_ref[...] = nv

    r, c = w.shape
    tr = r
    for cand in (512, 352, 256, 128):
        if r > cand and r % cand == 0:
            tr = cand
            break
    spec = pl.BlockSpec((tr, c), lambda i: (i, 0))
    return pl.pallas_call(
        body, name="adamw_" + tag, out_shape=[jax.ShapeDtypeStruct(w.shape, F32)] * 3, grid=(r // tr,),
        in_specs=[spec] * 4, out_specs=[spec] * 3, compiler_params=_params(),
    )(w, g, m, v)


def _tile(t, want):
    return want if t % want == 0 else t


def _inproj(x, g1, w_in, tm):
    t = x.shape[0]

    def body(x_ref, g_ref, w_ref, *outs):
        h = _rms(x_ref[...], g_ref[...]).astype(BF16)
        off = 0
        for o_ref, wd in zip(outs, IN_WIDTHS):
            o_ref[...] = _dot(h, w_ref[:, off:off + wd])
            off += wd

    return pl.pallas_call(
        body, name="inproj_fwd", grid=(t // tm,),
        out_shape=[jax.ShapeDtypeStruct((t, wd), F32) for wd in IN_WIDTHS],
        in_specs=[pl.BlockSpec((tm, D_MODEL), lambda i: (i, 0)), _const_spec((1, D_MODEL)), _const_spec((D_MODEL, D_IN_PAD))],
        out_specs=[pl.BlockSpec((tm, wd), lambda i: (i, 0)) for wd in IN_WIDTHS],
        compiler_params=_params(),
    )(x, g1, w_in)


def _rope_tables(seq):
    inv = 1.0 / (ROPE_THETA ** (jnp.arange(0, B_ROPE, 2, dtype=F32) / B_ROPE))
    ang = jnp.arange(seq, dtype=F32)[:, None] * inv[None, :]
    cos, sin = jnp.cos(ang), jnp.sin(ang)
    z32, z64 = jnp.zeros_like(cos), jnp.zeros((seq, 64), F32)
    cos_t = jnp.concatenate([cos, cos, z64], axis=1)
    sin_a = jnp.concatenate([-sin, z32, z64], axis=1)
    sin_b = jnp.concatenate([z32, sin, z64], axis=1)
    return cos_t, sin_a, sin_b


def _rope(t, cos_t, sin_a, sin_b):
    return t * cos_t + pltpu.roll(t, 96, 1) * sin_a + pltpu.roll(t, 32, 1) * sin_b


def _rope_t(d, cos_t, sin_a, sin_b):
    return d * cos_t + pltpu.roll(d * sin_a, 32, 1) + pltpu.roll(d * sin_b, 96, 1)


def _mla_qkv(cq, ckv, kr, g_qa, g_kva, w_q, w_k, w_v, tables, seq, tm):
    t = cq.shape[0]
    nblk = seq // tm

    def body(cq_ref, ckv_ref, kr_ref, gq_ref, gk_ref, wq_ref, wk_ref, wv_ref, c_ref, sa_ref, sb_ref, q_out, k_out, v_out):
        cos_t, sin_a, sin_b = c_ref[...], sa_ref[...], sb_ref[...]
        cqn = _rms(cq_ref[...], gq_ref[...]).astype(BF16)
        ckn = _rms(ckv_ref[...], gk_ref[...]).astype(BF16)
        kr_rot = _rope(kr_ref[...], cos_t, sin_a, sin_b).astype(BF16)
        v_out[...] = _dot(ckn, wv_ref[...]).astype(BF16)
        for h in range(B_HEADS):
            lo = h * QK_PAD
            q_out[:, lo:lo + 128] = (_dot(cqn, wq_ref[:, lo:lo + 128]) * ATTN_SCALE).astype(BF16)
            qr = _rope(_dot(cqn, wq_ref[:, lo + 128:lo + 256]), cos_t, sin_a, sin_b)
            q_out[:, lo + 128:lo + 256] = (qr * ATTN_SCALE).astype(BF16)
            k_out[:, lo:lo + 128] = _dot(ckn, wk_ref[:, h * 128:(h + 1) * 128]).astype(BF16)
            k_out[:, lo + 128:lo + 256] = kr_rot

    tok = lambda wd: pl.BlockSpec((tm, wd), lambda i: (i, 0))
    tab = pl.BlockSpec((tm, 128), lambda i: (i % nblk, 0))
    return pl.pallas_call(
        body, name="mla_qkv_fwd", grid=(t // tm,),
        out_shape=[jax.ShapeDtypeStruct((t, B_HEADS * QK_PAD), BF16), jax.ShapeDtypeStruct((t, B_HEADS * QK_PAD), BF16),
                   jax.ShapeDtypeStruct((t, B_HEADS * B_V), BF16)],
        in_specs=[tok(Q_LORA), tok(KV_LORA), tok(128), _const_spec((1, Q_LORA)), _const_spec((1, KV_LORA)),
                  _const_spec((Q_LORA, B_HEADS * QK_PAD)), _const_spec((KV_LORA, 512)), _const_spec((KV_LORA, 512)), tab, tab, tab],
        out_specs=[tok(B_HEADS * QK_PAD), tok(B_HEADS * QK_PAD), tok(B_HEADS * B_V)],
        compiler_params=_params(),
    )(cq, ckv, kr, g_qa, g_kva, w_q, w_k, w_v, *tables)


def _attn_fwd(qcat, kcat, v, nb, seq, tq):
    t = qcat.shape[0]
    nq = seq // tq

    def body(q_ref, k_ref, v_ref, o_ref, lse_ref):
        for j in range(tq // ATTN_SUB):
            r = pl.ds(j * ATTN_SUB, ATTN_SUB)
            s = _dot_nt(q_ref[r, :], k_ref[...])
            m = jnp.max(s, axis=-1, keepdims=True)
            p = jnp.exp(s - m)
            l = jnp.sum(p, axis=-1, keepdims=True)
            o_ref[r, :] = _dot(p.astype(BF16), v_ref[...]) / l
            lse_ref[0, r, :] = m + jnp.log(l)

    return pl.pallas_call(
        body, name="attn_fwd", grid=(nb, B_HEADS, nq),
        out_shape=[jax.ShapeDtypeStruct((t, B_HEADS * B_V), F32), jax.ShapeDtypeStruct((B_HEADS, t, 1), F32)],
        in_specs=[pl.BlockSpec((tq, QK_PAD), lambda b, h, i: (b * nq + i, h)),
                  pl.BlockSpec((seq, QK_PAD), lambda b, h, i: (b, h)),
                  pl.BlockSpec((seq, B_V), lambda b, h, i: (b, h))],
        out_specs=[pl.BlockSpec((tq, B_V), lambda b, h, i: (b * nq + i, h)),
                   pl.BlockSpec((1, tq, 1), lambda b, h, i: (h, b * nq + i, 0))],
        compiler_params=_params(),
    )(qcat, kcat, v)


def _attn_bwd(qcat, kcat, v, o, lse, do, nb, seq, tq):
    t = qcat.shape[0]
    nq = seq // tq

    def body(q_ref, k_ref, v_ref, o_ref, lse_ref, do_ref, dq_ref, dk_ref, dv_ref):
        @pl.when(pl.program_id(2) == 0)
        def _():
            dv_ref[...] = jnp.zeros_like(dv_ref)
            dk_ref[...] = jnp.zeros_like(dk_ref)

        for j in range(tq // ATTN_SUB_BWD):
            r = pl.ds(j * ATTN_SUB_BWD, ATTN_SUB_BWD)
            q, k = q_ref[r, :], k_ref[...]
            do_f = do_ref[r, :]
            delta = jnp.sum(do_f * o_ref[r, :], axis=-1, keepdims=True)
            dob = do_f.astype(BF16)
            p = jnp.exp(_dot_nt(q, k) - lse_ref[0, r, :])
            ds = (p * (_dot_nt(dob, v_ref[...]) - delta)).astype(BF16)
            dq_ref[r, :] = _dot(ds, k)
            dv_ref[...] += _dot_tn(p.astype(BF16), dob)
            dk_ref[...] += _dot_tn(ds, q)

    qspec = lambda wd: pl.BlockSpec((tq, wd), lambda b, h, i: (b * nq + i, h))
    kspec = lambda wd: pl.BlockSpec((seq, wd), lambda b, h, i: (b, h))
    return pl.pallas_call(
        body, name="attn_bwd", grid=(nb, B_HEADS, nq),
        out_shape=[jax.ShapeDtypeStruct((t, B_HEADS * QK_PAD), F32), jax.ShapeDtypeStruct((t, B_HEADS * QK_PAD), F32),
                   jax.ShapeDtypeStruct((t, B_HEADS * B_V), F32)],
        in_specs=[qspec(QK_PAD), kspec(QK_PAD), kspec(B_V), qspec(B_V),
                  pl.BlockSpec((1, tq, 1), lambda b, h, i: (h, b * nq + i, 0)), qspec(B_V)],
        out_specs=[qspec(QK_PAD), kspec(QK_PAD), kspec(B_V)],
        compiler_params=_params(),
    )(qcat, kcat, v, o, lse, do)


def _gla_consts(reverse):
    row = lax.broadcasted_iota(jnp.int32, (CHUNK, CHUNK), 0)
    col = lax.broadcasted_iota(jnp.int32, (CHUNK, CHUNK), 1)
    causal = (row <= col) if reverse else (row >= col)
    lane = lax.broadcasted_iota(jnp.int32, (1, HEAD_PAIR), 1)
    m0 = (lane < 64).astype(F32)
    m1 = 1.0 - m0
    r2 = lax.broadcasted_iota(jnp.int32, (HEAD_PAIR, HEAD_PAIR), 0)
    c2 = lax.broadcasted_iota(jnp.int32, (HEAD_PAIR, HEAD_PAIR), 1)
    same_head = ((r2 < 64) == (c2 < 64)).astype(F32)
    return causal, m0, m1, same_head


def _gla_chunk(hq, hi, z, l0, l1, st, consts, reverse):
    causal, m0, m1, same_head = consts
    mx = jnp.maximum(l0, l1)
    e0, e1 = jnp.exp(l0 - mx), jnp.exp(l1 - mx)
    lb = e0 / (e0 + e1)
    q = hq * _sigmoid(hq)
    log_f = jnp.log(lb + (1.0 - lb) * _sigmoid(z))
    k = (1.0 - lb) * _sigmoid(-z)
    cum = _cumsum_rows(log_f, reverse)
    tot = jnp.sum(log_f, axis=0, keepdims=True)
    q_dec = q * jnp.exp(cum)
    k_inv = k * jnp.exp(-cum)
    k_end = k * jnp.exp(tot - cum)
    o = _mm_nt(q_dec, st)
    for mh in (m0, m1):
        s = jnp.where(causal, _mm_nt(q_dec * mh, k_inv), 0.0)
        o = o + _mm(s, hi) * mh
    st_new = st * jnp.exp(tot) + _mm_tn(hi, k_end) * same_head
    return o, st_new


def _gla_fwd(hq, hi, z, lbl, nb, seq, group, reverse):
    t = hq.shape[0]
    rows = group * CHUNK
    nblk = seq // rows
    n_chunks = seq // CHUNK

    def tb(i):
        return nblk - 1 - i if reverse else i

    def body(hq_ref, hi_ref, z_ref, lbl_ref, o_ref, save_ref, st_ref):
        @pl.when(pl.program_id(2) == 0)
        def _():
            st_ref[...] = jnp.zeros_like(st_ref)

        consts = _gla_consts(reverse)
        l0, l1 = lbl_ref[0:1, :], lbl_ref[1:2, :]
        st = st_ref[...]
        for cc in range(group):
            c = group - 1 - cc if reverse else cc
            r = pl.ds(c * CHUNK, CHUNK)
            save_ref[0, 0, c] = st
            o_c, st = _gla_chunk(hq_ref[r, :], hi_ref[r, :], z_ref[r, :], l0, l1, st, consts, reverse)
            o_ref[r, :] = o_c
        st_ref[...] = st

    tok = pl.BlockSpec((rows, HEAD_PAIR), lambda b, p, i: (b * nblk + tb(i), p))
    return pl.pallas_call(
        body, name="gla_fwd_rev" if reverse else "gla_fwd", grid=(nb, 4, nblk),
        out_shape=[jax.ShapeDtypeStruct((t, A_WIDTH), F32),
                   jax.ShapeDtypeStruct((nb, 4, n_chunks, HEAD_PAIR, HEAD_PAIR), F32)],
        in_specs=[tok, tok, tok, pl.BlockSpec((2, HEAD_PAIR), lambda b, p, i: (0, p))],
        out_specs=[tok, pl.BlockSpec((1, 1, group, HEAD_PAIR, HEAD_PAIR), lambda b, p, i: (b, p, tb(i), 0, 0))],
        scratch_shapes=[pltpu.VMEM((HEAD_PAIR, HEAD_PAIR), F32)],
        compiler_params=_params(),
    )(hq, hi, z, lbl)


def _gla_bwd(hq, hi, z, lbl, saved, do, nb, seq, group, reverse):
    t = hq.shape[0]
    rows = group * CHUNK
    nblk = seq // rows

    def tb(i):
        return i if reverse else nblk - 1 - i

    def body(hq_ref, hi_ref, z_ref, lbl_ref, save_ref, do_ref, dq_ref, dv_ref, dz_ref, dl_ref, dst_ref):
        @pl.when(pl.program_id(2) == 0)
        def _():
            dst_ref[...] = jnp.zeros_like(dst_ref)
            dl_ref[...] = jnp.zeros_like(dl_ref)

        consts = _gla_consts(reverse)
        l0, l1 = lbl_ref[0:1, :], lbl_ref[1:2, :]
        dst = dst_ref[...]
        dl0 = jnp.zeros((1, HEAD_PAIR), F32)
        dl1 = jnp.zeros((1, HEAD_PAIR), F32)
        fn = lambda a, b, e, f0, f1, g: _gla_chunk(a, b, e, f0, f1, g, consts, reverse)
        for cc in range(group):
            c = cc if reverse else group - 1 - cc
            r = pl.ds(c * CHUNK, CHUNK)
            _, vjp = jax.vjp(fn, hq_ref[r, :], hi_ref[r, :], z_ref[r, :], l0, l1, save_ref[0, 0, c])
            d_hq, d_hi, d_z, d_l0, d_l1, dst = vjp((do_ref[r, :], dst))
            dq_ref[r, :] = d_hq
            dv_ref[r, :] = d_hi
            dz_ref[r, :] = d_z
            dl0 = dl0 + d_l0
            dl1 = dl1 + d_l1
        dst_ref[...] = dst
        dl_ref[0, 0:1, :] += dl0
        dl_ref[0, 1:2, :] += dl1

    tok = pl.BlockSpec((rows, HEAD_PAIR), lambda b, p, i: (b * nblk + tb(i), p))
    return pl.pallas_call(
        body, name="gla_bwd_rev" if reverse else "gla_bwd", grid=(nb, 4, nblk),
        out_shape=[jax.ShapeDtypeStruct((t, A_WIDTH), F32)] * 3 + [jax.ShapeDtypeStruct((nb, 2, A_WIDTH), F32)],
        in_specs=[tok, tok, tok, pl.BlockSpec((2, HEAD_PAIR), lambda b, p, i: (0, p)),
                  pl.BlockSpec((1, 1, group, HEAD_PAIR, HEAD_PAIR), lambda b, p, i: (b, p, tb(i), 0, 0)), tok],
        out_specs=[tok, tok, tok, pl.BlockSpec((1, 2, HEAD_PAIR), lambda b, p, i: (b, 0, p))],
        scratch_shapes=[pltpu.VMEM((HEAD_PAIR, HEAD_PAIR), F32)],
        compiler_params=_params(),
    )(hq, hi, z, lbl, saved, do)


def _head_mean_matrix():
    r = lax.broadcasted_iota(jnp.int32, (A_WIDTH, A_WIDTH), 0) // 64
    c = lax.broadcasted_iota(jnp.int32, (A_WIDTH, A_WIDTH), 1) // 64
    return jnp.where(r == c, 1.0 / 64.0, 0.0).astype(BF16)


def _gla_out(o_f, o_b, hg, g, mean_mat):
    o = o_f + o_b
    ms = _group_mean(o * o, mean_mat)
    return o * lax.rsqrt(ms + EPS) * g * (hg * _sigmoid(hg))


def _gla_combine(o_f, o_b, hg, g, tm):
    t = o_f.shape[0]

    def body(of_ref, ob_ref, hg_ref, g_ref, y_ref):
        y_ref[...] = _gla_out(of_ref[...], ob_ref[...], hg_ref[...], g_ref[...], _head_mean_matrix())

    tok = pl.BlockSpec((tm, A_WIDTH), lambda i: (i, 0))
    return pl.pallas_call(
        body, name="gla_combine_fwd", grid=(t // tm,), out_shape=jax.ShapeDtypeStruct((t, A_WIDTH), F32),
        in_specs=[tok, tok, tok, _const_spec((1, A_WIDTH))], out_specs=tok, compiler_params=_params(),
    )(o_f, o_b, hg, g)


def _gla_combine_bwd(o_f, o_b, hg, g, dy, tm):
    t = o_f.shape[0]

    def body(of_ref, ob_ref, hg_ref, g_ref, dy_ref, do_ref, dhg_ref, dg_ref):
        mean_mat = _head_mean_matrix()
        fn = lambda o, hgv, gv: _gla_out(o, jnp.zeros_like(o), hgv, gv, mean_mat)
        _, vjp = jax.vjp(fn, of_ref[...] + ob_ref[...], hg_ref[...], g_ref[...])
        d_o, d_hg, d_g = vjp(dy_ref[...])
        do_ref[...] = d_o
        dhg_ref[...] = d_hg

        @pl.when(pl.program_id(0) == 0)
        def _():
            dg_ref[...] = jnp.zeros_like(dg_ref)

        dg_ref[...] += d_g

    tok = pl.BlockSpec((tm, A_WIDTH), lambda i: (i, 0))
    vec = pl.BlockSpec((1, A_WIDTH), lambda i: (0, 0))
    return pl.pallas_call(
        body, name="gla_combine_bwd", grid=(t // tm,),
        out_shape=[jax.ShapeDtypeStruct((t, A_WIDTH), F32)] * 2 + [jax.ShapeDtypeStruct((1, A_WIDTH), F32)],
        in_specs=[tok, tok, tok, _const_spec((1, A_WIDTH)), tok], out_specs=[tok, tok, vec], compiler_params=_params(),
    )(o_f, o_b, hg, g, dy)


def _post_fwd(x, ya, oattn, tgt, g_mla, w_out, g2, w_gate, w_up, w_down, g_fin, tm):
    t = x.shape[0]

    def body(x_ref, ya_ref, oa_ref, tgt_ref, gm_ref, wo_ref, g2_ref, wg_ref, wu_ref, wd_ref, gf_ref, x1_ref, x2_ref, loss_ref):
        yb = _rms(oa_ref[...], gm_ref[...])
        x1 = x_ref[...] + _dot(ya_ref[...].astype(BF16), wo_ref[0:A_WIDTH, :]) + _dot(yb.astype(BF16), wo_ref[A_WIDTH:, :])
        x1_ref[...] = x1
        h2 = _rms(x1, g2_ref[...]).astype(BF16)
        gate = _dot(h2, wg_ref[...])
        act = (gate * _sigmoid(gate) * _dot(h2, wu_ref[...])).astype(BF16)
        x2 = x1 + _dot(act, wd_ref[...])
        x2_ref[...] = x2
        err = _rms(x2, gf_ref[...]) - tgt_ref[...]
        part = 0.5 * jnp.sum(jnp.mean(err * err, axis=-1, keepdims=True), axis=0, keepdims=True)

        @pl.when(pl.program_id(0) == 0)
        def _():
            loss_ref[...] = jnp.zeros_like(loss_ref)

        loss_ref[...] += jnp.broadcast_to(part, loss_ref.shape)

    tok = lambda wd: pl.BlockSpec((tm, wd), lambda i: (i, 0))
    return pl.pallas_call(
        body, name="post_fwd", grid=(t // tm,),
        out_shape=[jax.ShapeDtypeStruct((t, D_MODEL), F32)] * 2 + [jax.ShapeDtypeStruct((1, 128), F32)],
        in_specs=[tok(D_MODEL), tok(A_WIDTH), tok(512), tok(D_MODEL), _const_spec((1, 512)), _const_spec((D_MODEL, D_MODEL)),
                  _const_spec((1, D_MODEL)), _const_spec((D_MODEL, D_FF)), _const_spec((D_MODEL, D_FF)),
                  _const_spec((D_FF, D_MODEL)), _const_spec((1, D_MODEL))],
        out_specs=[tok(D_MODEL), tok(D_MODEL), pl.BlockSpec((1, 128), lambda i: (0, 0))],
        compiler_params=_params(),
    )(x, ya, oattn, tgt, g_mla, w_out, g2, w_gate, w_up, w_down, g_fin)


def _post_bwd(x1, x2, ya, oattn, tgt, g_mla, w_out, g2, w_gate, w_up, w_down, g_fin, tm):
    t = x1.shape[0]

    def body(x1_ref, x2_ref, ya_ref, oa_ref, tgt_ref, gm_ref, wo_ref, g2_ref, wg_ref, wu_ref, wd_ref, gf_ref,
             dx1_ref, dya_ref, doa_ref, ycat_ref, dx1b_ref, h2_ref, dgate_ref, dup_ref, act_ref, dx2b_ref,
             dgm_ref, dg2_ref, dgf_ref):
        x1, x2 = x1_ref[...], x2_ref[...]
        dy = (_rms(x2, gf_ref[...]) - tgt_ref[...]) * (1.0 / D_MODEL)
        dx2, dgf = _rms_bwd(x2, gf_ref[...], dy)
        dx2b = dx2.astype(BF16)
        dx2b_ref[...] = dx2b
        h2 = _rms(x1, g2_ref[...]).astype(BF16)
        h2_ref[...] = h2
        gate, up = _dot(h2, wg_ref[...]), _dot(h2, wu_ref[...])
        sg = _sigmoid(gate)
        sl = gate * sg
        act_ref[...] = (sl * up).astype(BF16)
        dact = _dot_nt(dx2b, wd_ref[...])
        dup = (dact * sl).astype(BF16)
        dgate = (dact * up * (sg * (1.0 + gate * (1.0 - sg)))).astype(BF16)
        dup_ref[...] = dup
        dgate_ref[...] = dgate
        dh2 = _dot_nt(dgate, wg_ref[...]) + _dot_nt(dup, wu_ref[...])
        dx1n, dg2 = _rms_bwd(x1, g2_ref[...], dh2)
        dx1 = dx2 + dx1n
        dx1_ref[...] = dx1
        dx1b = dx1.astype(BF16)
        dx1b_ref[...] = dx1b
        oa = oa_ref[...]
        ycat_ref[:, 0:A_WIDTH] = ya_ref[...].astype(BF16)
        ycat_ref[:, A_WIDTH:] = _rms(oa, gm_ref[...]).astype(BF16)
        dya_ref[...] = _dot_nt(dx1b, wo_ref[0:A_WIDTH, :])
        doa, dgm = _rms_bwd(oa, gm_ref[...], _dot_nt(dx1b, wo_ref[A_WIDTH:, :]))
        doa_ref[...] = doa

        @pl.when(pl.program_id(0) == 0)
        def _():
            dgm_ref[...] = jnp.zeros_like(dgm_ref)
            dg2_ref[...] = jnp.zeros_like(dg2_ref)
            dgf_ref[...] = jnp.zeros_like(dgf_ref)

        dgm_ref[...] += dgm
        dg2_ref[...] += dg2
        dgf_ref[...] += dgf

    tok = lambda wd: pl.BlockSpec((tm, wd), lambda i: (i, 0))
    vec = lambda wd: pl.BlockSpec((1, wd), lambda i: (0, 0))
    sds = lambda wd, dt: jax.ShapeDtypeStruct((t, wd), dt)
    return pl.pallas_call(
        body, name="post_bwd", grid=(t // tm,),
        out_shape=[sds(D_MODEL, F32), sds(512, F32), sds(512, F32), sds(D_MODEL, BF16), sds(D_MODEL, BF16), sds(D_MODEL, BF16),
                   sds(D_FF, BF16), sds(D_FF, BF16), sds(D_FF, BF16), sds(D_MODEL, BF16),
                   jax.ShapeDtypeStruct((1, 512), F32), jax.ShapeDtypeStruct((1, D_MODEL), F32), jax.ShapeDtypeStruct((1, D_MODEL), F32)],
        in_specs=[tok(D_MODEL), tok(D_MODEL), tok(512), tok(512), tok(D_MODEL), _const_spec((1, 512)),
                  _const_spec((D_MODEL, D_MODEL)), _const_spec((1, D_MODEL)), _const_spec((D_MODEL, D_FF)),
                  _const_spec((D_MODEL, D_FF)), _const_spec((D_FF, D_MODEL)), _const_spec((1, D_MODEL))],
        out_specs=[tok(D_MODEL), tok(512), tok(512), tok(D_MODEL), tok(D_MODEL), tok(D_MODEL), tok(D_FF), tok(D_FF), tok(D_FF),
                   tok(D_MODEL), vec(512), vec(D_MODEL), vec(D_MODEL)],
        compiler_params=_params(),
    )(x1, x2, ya, oattn, tgt, g_mla, w_out, g2, w_gate, w_up, w_down, g_fin)


def _matmul_tn(a, b, tn, tt, tag):
    t, k = a.shape
    n = b.shape[1]

    def body(a_ref, b_ref, o_ref):
        part = _dot_tn(a_ref[...], b_ref[...])

        @pl.when(pl.program_id(1) == 0)
        def _():
            o_ref[...] = part

        @pl.when(pl.program_id(1) > 0)
        def _():
            o_ref[...] += part

    return pl.pallas_call(
        body, name="wgrad_" + tag, grid=(n // tn, t // tt), out_shape=jax.ShapeDtypeStruct((k, n), F32),
        in_specs=[pl.BlockSpec((tt, k), lambda j, i: (i, 0)), pl.BlockSpec((tt, tn), lambda j, i: (i, j))],
        out_specs=pl.BlockSpec((k, tn), lambda j, i: (0, j)), compiler_params=_params(),
    )(a, b)


def _mla_qkv_bwd(cq, ckv, g_qa, g_kva, w_q, w_k, w_v, tables, dq, dk, dv, seq, tm):
    t = cq.shape[0]
    nblk = seq // tm

    def body(cq_ref, ckv_ref, gq_ref, gk_ref, wq_ref, wk_ref, wv_ref, c_ref, sa_ref, sb_ref, dq_ref, dk_ref, dv_ref,
             dcq_ref, dckv_ref, dkr_ref, cqn_ref, dqf_ref, ckn_ref, dkn_ref, dvb_ref, dgq_ref, dgk_ref):
        cos_t, sin_a, sin_b = c_ref[...], sa_ref[...], sb_ref[...]
        cqn_ref[...] = _rms(cq_ref[...], gq_ref[...]).astype(BF16)
        ckn_ref[...] = _rms(ckv_ref[...], gk_ref[...]).astype(BF16)
        dkr = jnp.zeros((tm, 128), F32)
        for h in range(B_HEADS):
            lo = h * QK_PAD
            dqf_ref[:, lo:lo + 128] = (dq_ref[:, lo:lo + 128] * ATTN_SCALE).astype(BF16)
            dqf_ref[:, lo + 128:lo + 256] = _rope_t(dq_ref[:, lo + 128:lo + 256] * ATTN_SCALE, cos_t, sin_a, sin_b).astype(BF16)
            dkn_ref[:, h * 128:(h + 1) * 128] = dk_ref[:, lo:lo + 128].astype(BF16)
            dkr = dkr + dk_ref[:, lo + 128:lo + 256]
        dkr_ref[...] = _rope_t(dkr, cos_t, sin_a, sin_b)
        dvb = dv_ref[...].astype(BF16)
        dvb_ref[...] = dvb
        dcq, dgq = _rms_bwd(cq_ref[...], gq_ref[...], _dot_nt(dqf_ref[...], wq_ref[...]))
        dckv, dgk = _rms_bwd(ckv_ref[...], gk_ref[...], _dot_nt(dkn_ref[...], wk_ref[...]) + _dot_nt(dvb, wv_ref[...]))
        dcq_ref[...] = dcq
        dckv_ref[...] = dckv

        @pl.when(pl.program_id(0) == 0)
        def _():
            dgq_ref[...] = jnp.zeros_like(dgq_ref)
            dgk_ref[...] = jnp.zeros_like(dgk_ref)

        dgq_ref[...] += dgq
        dgk_ref[...] += dgk

    tok = lambda wd: pl.BlockSpec((tm, wd), lambda i: (i, 0))
    vec = lambda wd: pl.BlockSpec((1, wd), lambda i: (0, 0))
    tab = pl.BlockSpec((tm, 128), lambda i: (i % nblk, 0))
    sds = lambda wd, dt: jax.ShapeDtypeStruct((t, wd), dt)
    return pl.pallas_call(
        body, name="mla_qkv_bwd", grid=(t // tm,),
        out_shape=[sds(Q_LORA, F32), sds(KV_LORA, F32), sds(128, F32), sds(Q_LORA, BF16), sds(1024, BF16), sds(KV_LORA, BF16),
                   sds(512, BF16), sds(512, BF16), jax.ShapeDtypeStruct((1, Q_LORA), F32), jax.ShapeDtypeStruct((1, KV_LORA), F32)],
        in_specs=[tok(Q_LORA), tok(KV_LORA), _const_spec((1, Q_LORA)), _const_spec((1, KV_LORA)),
                  _const_spec((Q_LORA, 1024)), _const_spec((KV_LORA, 512)), _const_spec((KV_LORA, 512)), tab, tab, tab,
                  tok(1024), tok(1024), tok(512)],
        out_specs=[tok(Q_LORA), tok(KV_LORA), tok(128), tok(Q_LORA), tok(1024), tok(KV_LORA), tok(512), tok(512),
                   vec(Q_LORA), vec(KV_LORA)],
        compiler_params=_params(),
    )(cq, ckv, g_qa, g_kva, w_q, w_k, w_v, *tables, dq, dk, dv)


def _inproj_bwd(x, g1, w_in, dx1, pieces, tm):
    t = x.shape[0]
    counts = [len(p) for p in pieces]
    flat = [a for p in pieces for a in p]
    widths = [wd for wd, p in zip(IN_WIDTHS, pieces) for _ in p]

    def body(x_ref, g_ref, w_ref, dx1_ref, *refs):
        ins = refs[:len(flat)]
        dx_ref, h_ref, dp_ref, dg_ref = refs[len(flat):]
        xv = x_ref[...]
        h_ref[...] = _rms(xv, g_ref[...]).astype(BF16)
        off, j = 0, 0
        for wd, cnt in zip(IN_WIDTHS, counts):
            acc = ins[j][...]
            for jj in range(1, cnt):
                acc = acc + ins[j + jj][...]
            dp_ref[:, off:off + wd] = acc.astype(BF16)
            off += wd
            j += cnt
        dxn, dg = _rms_bwd(xv, g_ref[...], _dot_nt(dp_ref[...], w_ref[...]))
        dx_ref[...] = dx1_ref[...] + dxn

        @pl.when(pl.program_id(0) == 0)
        def _():
            dg_ref[...] = jnp.zeros_like(dg_ref)

        dg_ref[...] += dg

    tok = lambda wd: pl.BlockSpec((tm, wd), lambda i: (i, 0))
    return pl.pallas_call(
        body, name="inproj_bwd", grid=(t // tm,),
        out_shape=[jax.ShapeDtypeStruct((t, D_MODEL), F32), jax.ShapeDtypeStruct((t, D_MODEL), BF16),
                   jax.ShapeDtypeStruct((t, D_IN_PAD), BF16), jax.ShapeDtypeStruct((1, D_MODEL), F32)],
        in_specs=[tok(D_MODEL), _const_spec((1, D_MODEL)), _const_spec((D_MODEL, D_IN_PAD)), tok(D_MODEL)] + [tok(wd) for wd in widths],
        out_specs=[tok(D_MODEL), tok(D_MODEL), tok(D_IN_PAD), pl.BlockSpec((1, D_MODEL), lambda i: (0, 0))],
        compiler_params=_params(),
    )(x, g1, w_in, dx1, *flat)


def _pack_shards(shards):
    return jnp.concatenate([s.reshape(-1, 128) for s in shards], axis=0)


def _unpack_full(gathered):
    out, r0 = [], 0
    for (_, r, c, axis), nrows in zip(BIG, BIG_ROWS):
        blk = gathered[:, r0:r0 + nrows]
        if axis == 1:
            full = blk.reshape(N_DEV, r, c // N_DEV).transpose(1, 0, 2).reshape(r, c)
        else:
            full = blk.reshape(r, c)
        out.append(full)
        r0 += nrows
    return out


def _pack_full_grads(grads):
    parts = []
    for (_, r, c, axis), g in zip(BIG, grads):
        if axis == 1:
            parts.append(g.reshape(r, N_DEV, c // N_DEV).transpose(1, 0, 2).reshape(N_DEV, -1, 128))
        else:
            parts.append(g.reshape(N_DEV, -1, 128))
    return jnp.concatenate(parts, axis=1)


def _unpack_shards(packed):
    out, r0 = [], 0
    for (_, r, c, axis), nrows in zip(BIG, BIG_ROWS):
        shape = (1, r, c // N_DEV) if axis == 1 else (1, r // N_DEV, c)
        out.append(packed[r0:r0 + nrows].reshape(shape))
        r0 += nrows
    return out


def _arrange_weights(w_in, w_q_b, w_kv_b):
    w_in_arr = jnp.concatenate([w_in, jnp.zeros((D_MODEL, D_IN_PAD - D_IN), w_in.dtype)], axis=1)
    q3 = w_q_b.reshape(Q_LORA, B_HEADS, B_NOPE + B_ROPE)
    w_q = jnp.concatenate([q3, jnp.zeros((Q_LORA, B_HEADS, QK_PAD - B_NOPE - B_ROPE), w_q_b.dtype)], axis=2).reshape(Q_LORA, B_HEADS * QK_PAD)
    kv3 = w_kv_b.reshape(KV_LORA, B_HEADS, B_NOPE + B_V)
    w_k = kv3[:, :, :B_NOPE].reshape(KV_LORA, B_HEADS * B_NOPE)
    w_v = kv3[:, :, B_NOPE:].reshape(KV_LORA, B_HEADS * B_V)
    return w_in_arr, w_q, w_k, w_v


def _unarrange_grads(d_in_arr, d_q, d_k, d_v):
    d_in = d_in_arr[:, :D_IN]
    d_qb = d_q.reshape(Q_LORA, B_HEADS, QK_PAD)[:, :, :B_NOPE + B_ROPE].reshape(Q_LORA, B_HEADS * (B_NOPE + B_ROPE))
    d_kvb = jnp.concatenate([d_k.reshape(KV_LORA, B_HEADS, B_NOPE), d_v.reshape(KV_LORA, B_HEADS, B_V)], axis=2).reshape(KV_LORA, -1)
    return d_in, d_qb, d_kvb


def _rows128(a, rows):
    flat = a.reshape(-1, 128)
    pad = rows - flat.shape[0]
    return flat if pad == 0 else jnp.concatenate([flat, jnp.zeros((pad, 128), flat.dtype)], axis=0)


def _step_core(x, loss_target, small_w, lb_full, full_w, seq, group, tiles):
    g1, g_hgrn, g_qa, g_kva, g_mla, g2, g_fin = small_w
    w_in_f, w_qb_f, w_kvb_f, w_out, w_gate, w_up, w_down = full_w
    w_in, w_q, w_k, w_v = _arrange_weights(w_in_f, w_qb_f, w_kvb_f)
    nb = x.shape[0]
    t = nb * seq
    tm, tq_f, tq_b, tt = tiles
    xt = x.reshape(t, D_MODEL)
    tgt = loss_target.reshape(t, D_MODEL)
    tables = _rope_tables(seq)

    hq, hi, zf, zb, hg, cq, ckv, kr = _inproj(xt, g1, w_in, tm)
    lbl_f, lbl_b = lb_full[0], lb_full[1]
    o_f, save_f = _gla_fwd(hq, hi, zf, lbl_f, nb, seq, group, False)
    o_b, save_b = _gla_fwd(hq, hi, zb, lbl_b, nb, seq, group, True)
    ya = _gla_combine(o_f, o_b, hg, g_hgrn, tm)
    qcat, kcat, vv = _mla_qkv(cq, ckv, kr, g_qa, g_kva, w_q, w_k, w_v, tables, seq, tm)
    oattn, lse = _attn_fwd(qcat, kcat, vv, nb, seq, tq_f)
    x1, x2, loss_row = _post_fwd(xt, ya, oattn, tgt, g_mla, w_out, g2, w_gate, w_up, w_down, g_fin, tm)

    (dx1, d_ya, d_oattn, ycat_b, dx1_b, h2_b, dgate_b, dup_b, act_b, dx2_b, d_g_mla, d_g2, d_g_fin) = _post_bwd(
        x1, x2, ya, oattn, tgt, g_mla, w_out, g2, w_gate, w_up, w_down, g_fin, tm)
    d_w_gate = _matmul_tn(h2_b, dgate_b, D_FF // 2, tt, "gate")
    d_w_up = _matmul_tn(h2_b, dup_b, D_FF // 2, tt, "up")
    d_w_down = _matmul_tn(act_b, dx2_b, 512, tt, "down")
    d_w_out = _matmul_tn(ycat_b, dx1_b, D_MODEL, tt, "out")
    dq, dk, dv = _attn_bwd(qcat, kcat, vv, oattn, lse, d_oattn, nb, seq, tq_b)
    (d_cq, d_ckv, d_kr, cqn_b, dqf_b, ckn_b, dkn_b, dvb_b, d_g_qa, d_g_kva) = _mla_qkv_bwd(
        cq, ckv, g_qa, g_kva, w_q, w_k, w_v, tables, dq, dk, dv, seq, tm)
    d_w_q = _matmul_tn(cqn_b, dqf_b, B_HEADS * QK_PAD, tt, "q_b")
    d_w_k = _matmul_tn(ckn_b, dkn_b, 512, tt, "kv_b_k")
    d_w_v = _matmul_tn(ckn_b, dvb_b, 512, tt, "kv_b_v")
    d_o, d_hg, d_g_hgrn = _gla_combine_bwd(o_f, o_b, hg, g_hgrn, d_ya, tm)
    dq_f, dv_f, dz_f, dl_f = _gla_bwd(hq, hi, zf, lbl_f, save_f, d_o, nb, seq, group, False)
    dq_b, dv_b, dz_b, dl_b = _gla_bwd(hq, hi, zb, lbl_b, save_b, d_o, nb, seq, group, True)
    grad_x, h1_b, dproj_b, d_g1 = _inproj_bwd(
        xt, g1, w_in, dx1, [[dq_f, dq_b], [dv_f, dv_b], [dz_f], [dz_b], [d_hg], [d_cq], [d_ckv], [d_kr]], tm)
    d_w_in_arr = _matmul_tn(h1_b, dproj_b, D_IN_PAD // 2, tt, "in")

    d_w_in, d_w_qb, d_w_kvb = _unarrange_grads(d_w_in_arr, d_w_q, d_w_k, d_w_v)
    d_lb = jnp.stack([jnp.sum(dl_f, axis=0), jnp.sum(dl_b, axis=0)], axis=0)
    big_grads = [d_w_in, d_w_qb, d_w_kvb, d_w_out, d_w_gate, d_w_up, d_w_down]
    small_grads = [d_g1, d_g_hgrn, d_g_qa, d_g_kva, d_g_mla, d_g2, d_g_fin]
    return loss_row[0, 0], grad_x.reshape(nb, seq, D_MODEL), big_grads, small_grads, d_lb


def kernel(x, norm1_g, w_in, lb_logits, hgrn_norm_g, q_a_norm_g, w_q_b, kv_a_norm_g, w_kv_b, mla_norm_g, w_out, norm2_g, w_gate, w_up, w_down, final_norm_g, loss_target, m_norm1_g, m_w_in, m_lb_logits, m_hgrn_norm_g, m_q_a_norm_g, m_w_q_b, m_kv_a_norm_g, m_w_kv_b, m_mla_norm_g, m_w_out, m_norm2_g, m_w_gate, m_w_up, m_w_down, m_final_norm_g, v_norm1_g, v_w_in, v_lb_logits, v_hgrn_norm_g, v_q_a_norm_g, v_w_q_b, v_kv_a_norm_g, v_w_kv_b, v_mla_norm_g, v_w_out, v_norm2_g, v_w_gate, v_w_up, v_w_down, v_final_norm_g):
    big_w = [w_in, w_q_b, w_kv_b, w_out, w_gate, w_up, w_down]
    big_m = [m_w_in, m_w_q_b, m_w_kv_b, m_w_out, m_w_gate, m_w_up, m_w_down]
    big_v = [v_w_in, v_w_q_b, v_w_kv_b, v_w_out, v_w_gate, v_w_up, v_w_down]
    small_w = [norm1_g, hgrn_norm_g, q_a_norm_g, kv_a_norm_g, mla_norm_g, norm2_g, final_norm_g]
    small_m = [m_norm1_g, m_hgrn_norm_g, m_q_a_norm_g, m_kv_a_norm_g, m_mla_norm_g, m_norm2_g, m_final_norm_g]
    small_v = [v_norm1_g, v_hgrn_norm_g, v_q_a_norm_g, v_kv_a_norm_g, v_mla_norm_g, v_norm2_g, v_final_norm_g]
    seq = x.shape[1]
    my_id = 4 * lax.axis_index("x") + 2 * lax.axis_index("y") + lax.axis_index("c")

    wpack = _pack_shards(big_w).astype(BF16)
    lb_rows = _rows128(lb_logits, 8)
    gathered, lb_gathered = _all_gather([wpack, lb_rows])
    full_w = _unpack_full(gathered)
    lb_full = lb_gathered[:, :2].reshape(N_DEV, 2, 2, 64).transpose(1, 2, 0, 3).reshape(2, 2, 512)

    loss_part, grad_x, big_grads, small_grads, d_lb = _step_core(
        x, loss_target, [s.reshape(1, -1) for s in small_w], lb_full, full_w, seq, 8, (256, 1024, 512, 512))

    gpack = _pack_full_grads(big_grads).astype(BF16)
    spack = _rows128(jnp.concatenate([g.reshape(-1) for g in small_grads] + [d_lb.reshape(-1), jnp.full((128,), loss_part, F32)]),
                     SMALL_PACK_ROWS)
    recv, recv_small = _grad_exchange(gpack, spack)
    gsum = _sum_slots(recv, PACK_ROWS // 10)
    ssum = _sum_slots(recv_small, SMALL_PACK_ROWS)
    g_big = _unpack_shards(gsum)
    g_small, r0 = [], 0
    for s, (_, n) in zip(small_w, SMALL):
        g_small.append(ssum[r0:r0 + n // 128].reshape(s.shape))
        r0 += n // 128
    g_lb_full = ssum[r0:r0 + LB_ROWS].reshape(2, 2, N_DEV, 64)
    g_lb = lax.dynamic_index_in_dim(g_lb_full, my_id, axis=2, keepdims=False)

    deltas, new_ms, new_vs = {}, {}, {}
    for (name, _, _, _), w, g, m, v in zip(BIG, big_w, g_big, big_m, big_v):
        d, nm, nv = _adamw(w[0], g[0], m[0], v[0], name)
        deltas[name], new_ms[name], new_vs[name] = d[None], nm[None], nv[None]
    pack_small = lambda arrs: _rows128(jnp.concatenate([a.reshape(-1) for a in arrs]), 40)
    d, nm, nv = _adamw(pack_small(small_w), pack_small(g_small), pack_small(small_m), pack_small(small_v), "gains")
    r0 = 0
    for s, (name, n) in zip(small_w, SMALL):
        sl = slice(r0, r0 + n // 128)
        deltas[name], new_ms[name], new_vs[name] = d[sl].reshape(s.shape), nm[sl].reshape(s.shape), nv[sl].reshape(s.shape)
        r0 += n // 128
    as_rows = lambda a: a.reshape(2, 128)
    d, nm, nv = _adamw(as_rows(lb_logits), as_rows(g_lb), as_rows(m_lb_logits), as_rows(v_lb_logits), "lb_logits")
    deltas["lb_logits"], new_ms["lb_logits"], new_vs["lb_logits"] = (a.reshape(lb_logits.shape) for a in (d, nm, nv))

    loss = ssum[SMALL_ROWS + LB_ROWS, 0]
    grads = dict(zip([n for n, _, _, _ in BIG], g_big))
    grads.update(zip([n for n, _ in SMALL], g_small))
    grads["lb_logits"] = g_lb
    order = ["norm1_g", "w_in", "lb_logits", "hgrn_norm_g", "q_a_norm_g", "w_q_b", "kv_a_norm_g", "w_kv_b", "mla_norm_g",
             "w_out", "norm2_g", "w_gate", "w_up", "w_down", "final_norm_g"]
    return (loss, grad_x, *[grads[n] for n in order], *[deltas[n] for n in order],
            *[new_ms[n] for n in order], *[new_vs[n] for n in order])
```

```python
import functools
import math

import jax
import jax.numpy as jnp
from jax import lax
from jax.experimental import pallas as pl
from jax.experimental.pallas import tpu as pltpu

F32 = jnp.float32
BF16 = jnp.bfloat16

N_DEV = 8
D_MODEL = 1024
D_FF = 2816
A_WIDTH = 512
HEAD_PAIR = 128
CHUNK = 64
B_HEADS = 4
B_NOPE = 128
B_ROPE = 64
B_V = 128
QK_PAD = 256
Q_LORA = 384
KV_LORA = 256
D_IN = 3264
D_IN_PAD = 3328
IN_WIDTHS = (512, 512, 512, 512, 512, Q_LORA, KV_LORA, 128)
ROPE_THETA = 10000.0
EPS = 1e-6
ATTN_SCALE = (B_NOPE + B_ROPE) ** -0.5
ATTN_SUB = 256
ATTN_SUB_BWD = 256
ADAM_LR, ADAM_B1, ADAM_B2, ADAM_EPS, ADAM_WD, ADAM_STEP = 0.001, 0.9, 0.999, 1e-08, 0.01, 10
VMEM_LIMIT = 56 * 1024 * 1024
MESH = pl.DeviceIdType.MESH

BIG = (("w_in", 1024, D_IN, 1), ("w_q_b", Q_LORA, 768, 1), ("w_kv_b", KV_LORA, 1024, 1), ("w_out", 1024, 1024, 0),
       ("w_gate", 1024, D_FF, 1), ("w_up", 1024, D_FF, 1), ("w_down", D_FF, 1024, 0))
SMALL = (("norm1_g", 1024), ("hgrn_norm_g", 512), ("q_a_norm_g", 384), ("kv_a_norm_g", 256), ("mla_norm_g", 512),
         ("norm2_g", 1024), ("final_norm_g", 1024))


def _params(**kw):
    return pltpu.CompilerParams(vmem_limit_bytes=VMEM_LIMIT, **kw)


def _const_spec(shape):
    return pl.BlockSpec(shape, lambda *_: (0,) * len(shape), pipeline_mode=pl.Buffered(1))


def _dot(a, b):
    return jnp.dot(a, b, preferred_element_type=F32)


def _dot_nt(a, b):
    return lax.dot_general(a, b, (((1,), (1,)), ((), ())), preferred_element_type=F32)


def _dot_tn(a, b):
    return lax.dot_general(a, b, (((0,), (0,)), ((), ())), preferred_element_type=F32)


@jax.custom_vjp
def _mm(a, b):
    return _dot(a.astype(BF16), b.astype(BF16))


def _mm_fwd(a, b):
    return _mm(a, b), (a, b)


def _mm_bwd(res, g):
    a, b = res
    gb = g.astype(BF16)
    return _dot_nt(gb, b.astype(BF16)), _dot_tn(a.astype(BF16), gb)


_mm.defvjp(_mm_fwd, _mm_bwd)


@jax.custom_vjp
def _mm_nt(a, b):
    return _dot_nt(a.astype(BF16), b.astype(BF16))


def _mm_nt_fwd(a, b):
    return _mm_nt(a, b), (a, b)


def _mm_nt_bwd(res, g):
    a, b = res
    gb = g.astype(BF16)
    return _dot(gb, b.astype(BF16)), _dot_tn(gb, a.astype(BF16))


_mm_nt.defvjp(_mm_nt_fwd, _mm_nt_bwd)


@jax.custom_vjp
def _mm_tn(a, b):
    return _dot_tn(a.astype(BF16), b.astype(BF16))


def _mm_tn_fwd(a, b):
    return _mm_tn(a, b), (a, b)


def _mm_tn_bwd(res, g):
    a, b = res
    gb = g.astype(BF16)
    return _dot_nt(b.astype(BF16), gb), _dot(a.astype(BF16), gb)


_mm_tn.defvjp(_mm_tn_fwd, _mm_tn_bwd)


def _split3(a):
    hi = a.astype(BF16)
    r = a - hi.astype(F32)
    mid = r.astype(BF16)
    lo = (r - mid.astype(F32)).astype(BF16)
    return hi, mid, lo


def _dot_exact_rhs(a, m):
    hi, mid, lo = _split3(a)
    return _dot(hi, m) + _dot(mid, m) + _dot(lo, m)


@jax.custom_vjp
def _group_mean(a, m):
    return _dot_exact_rhs(a, m)


def _group_mean_fwd(a, m):
    return _group_mean(a, m), m


def _group_mean_bwd(m, g):
    return _dot_exact_rhs(g, m), jnp.zeros_like(m)


_group_mean.defvjp(_group_mean_fwd, _group_mean_bwd)


def _roll_rows(a, shift):
    return pltpu.roll(a, shift, 0)


def _cumsum_rows_raw(a, reverse):
    n = a.shape[0]
    row = lax.broadcasted_iota(jnp.int32, a.shape, 0)
    s = 1
    while s < n:
        if reverse:
            a = a + jnp.where(row < n - s, _roll_rows(a, n - s), 0.0)
        else:
            a = a + jnp.where(row >= s, _roll_rows(a, s), 0.0)
        s *= 2
    return a


@functools.partial(jax.custom_vjp, nondiff_argnums=(1,))
def _cumsum_rows(a, reverse):
    return _cumsum_rows_raw(a, reverse)


def _cumsum_rows_fwd(a, reverse):
    return _cumsum_rows_raw(a, reverse), None


def _cumsum_rows_bwd(reverse, _, g):
    return (_cumsum_rows_raw(g, not reverse),)


_cumsum_rows.defvjp(_cumsum_rows_fwd, _cumsum_rows_bwd)


def _rms(x, g):
    r = lax.rsqrt(jnp.mean(x * x, axis=-1, keepdims=True) + EPS)
    return x * r * g


def _rms_bwd(x, g, dy):
    r = lax.rsqrt(jnp.mean(x * x, axis=-1, keepdims=True) + EPS)
    xh = x * r
    dg = jnp.sum(dy * xh, axis=0, keepdims=True)
    dxh = dy * g
    dx = r * (dxh - xh * jnp.mean(dxh * xh, axis=-1, keepdims=True))
    return dx, dg


def _sigmoid(a):
    return jax.nn.sigmoid(a)


def _mesh_place():
    x, y, c = lax.axis_index("x"), lax.axis_index("y"), lax.axis_index("c")
    return x, y, c


def _dev_index(p):
    return 4 * p[0] + 2 * p[1] + p[2]


def _comm_sems(n):
    return [pltpu.SemaphoreType.DMA((n, 7)), pltpu.SemaphoreType.DMA((n, 7)), pltpu.SemaphoreType.DMA((n,))]


def _gather_protocol(ins, outs, send_sems, recv_sems, local_sems):
    n = len(ins)
    x, y, c = _mesh_place()
    me, sibling = (x, y, c), (x, y, 1 - c)
    chips = [(1 - x, y), (x, 1 - y), (1 - x, 1 - y)]

    def copy(a, k, block, to, src=None):
        slot = outs[a].at[_dev_index(block)]
        return pltpu.make_async_remote_copy(
            src_ref=slot if src is None else src, dst_ref=slot,
            send_sem=send_sems.at[a, k], recv_sem=recv_sems.at[a, k], device_id=to, device_id_type=MESH)

    def mine(a):
        return pltpu.make_async_copy(ins[a], outs[a].at[_dev_index(me)], local_sems.at[a])

    def first(a):
        return [copy(a, 0, me, sibling, src=ins[a])] + [copy(a, 1 + j, me, (*chip, c), src=ins[a]) for j, chip in enumerate(chips)]

    def start():
        for a in range(n):
            mine(a).start()
            for cp in first(a):
                cp.start()

    def forward():
        for a in range(n):
            for j, chip in enumerate(chips):
                copy(a, 1 + j, (*chip, c), me).wait_recv()
                copy(a, 4 + j, (*chip, c), sibling).start()

    def finish():
        for a in range(n):
            copy(a, 0, sibling, me).wait_recv()
            for j, chip in enumerate(chips):
                copy(a, 4 + j, (*chip, 1 - c), me).wait_recv()
        for a in range(n):
            mine(a).wait()
            for cp in first(a):
                cp.wait_send()
            for j, chip in enumerate(chips):
                copy(a, 4 + j, (*chip, c), sibling).wait_send()

    return start, forward, finish


def _exchange_protocol(ins, outs, scatter, send_sems, recv_sems, local_sems):
    n = len(ins)
    x, y, c = _mesh_place()
    me = (x, y, c)
    my_id = _dev_index(me)
    rels = [(dx, dy, dc) for dx in (0, 1) for dy in (0, 1) for dc in (0, 1)][1:]

    def peer_of(rel):
        return tuple(1 - v if d else v for v, d in zip(me, rel))

    def src(a, dev):
        return ins[a].at[dev] if scatter[a] else ins[a]

    def send(a, k):
        peer = peer_of(rels[k])
        return pltpu.make_async_remote_copy(
            src_ref=src(a, _dev_index(peer)), dst_ref=outs[a].at[my_id],
            send_sem=send_sems.at[a, k], recv_sem=recv_sems.at[a, k], device_id=peer, device_id_type=MESH)

    def arrival(a, k):
        peer = peer_of(rels[k])
        return pltpu.make_async_remote_copy(
            src_ref=src(a, my_id), dst_ref=outs[a].at[_dev_index(peer)],
            send_sem=send_sems.at[a, k], recv_sem=recv_sems.at[a, k], device_id=peer, device_id_type=MESH)

    def own(a):
        return pltpu.make_async_copy(src(a, my_id), outs[a].at[my_id], local_sems.at[a])

    def start():
        for a in range(n):
            own(a).start()
            for k in range(7):
                send(a, k).start()

    def finish():
        for a in range(n):
            for k in range(7):
                arrival(a, k).wait_recv()
        for a in range(n):
            for k in range(7):
                send(a, k).wait_send()
            own(a).wait()

    return start, finish


def _slot_shapes(blocks, scatter=None):
    return [jax.ShapeDtypeStruct(b.shape if (scatter and scatter[a]) else (N_DEV,) + b.shape, b.dtype) for a, b in enumerate(blocks)]


def _all_gather_call(blocks):
    n = len(blocks)

    def body(*refs):
        start, forward, finish = _gather_protocol(refs[:n], refs[n:2 * n], *refs[2 * n:])
        start()
        forward()
        finish()

    any_spec = pl.BlockSpec(memory_space=pl.ANY)
    return pl.pallas_call(
        body, name="weights_all_gather", out_shape=_slot_shapes(blocks),
        in_specs=[any_spec] * n, out_specs=[any_spec] * n, scratch_shapes=_comm_sems(n),
    )(*blocks)


def _exchange_call(blocks, scatter):
    n = len(blocks)

    def body(*refs):
        start, finish = _exchange_protocol(refs[:n], refs[n:2 * n], scatter, *refs[2 * n:])
        start()
        finish()

    any_spec = pl.BlockSpec(memory_space=pl.ANY)
    return pl.pallas_call(
        body, name="grad_exchange", out_shape=_slot_shapes(blocks, scatter),
        in_specs=[any_spec] * n, out_specs=[any_spec] * n, scratch_shapes=_comm_sems(n),
    )(*blocks)


def _sum_slots_call(recvs):
    n = len(recvs)

    def body(*refs):
        for in_ref, out_ref in zip(refs[:n], refs[n:]):
            acc = in_ref[0]
            for j in range(1, N_DEV):
                acc = acc + in_ref[j]
            out_ref[...] = acc

    return pl.pallas_call(
        body, name="small_grad_sum", out_shape=[jax.ShapeDtypeStruct(r.shape[1:], F32) for r in recvs],
        compiler_params=_params(),
    )(*recvs)


def _adam_update(w, g, m, v):
    nm = ADAM_B1 * m + (1.0 - ADAM_B1) * g
    nv = ADAM_B2 * v + (1.0 - ADAM_B2) * (g * g)
    bc1 = 1.0 - ADAM_B1 ** ADAM_STEP
    bc2 = 1.0 - ADAM_B2 ** ADAM_STEP
    return -ADAM_LR * ((nm / bc1) / (jnp.sqrt(nv / bc2) + ADAM_EPS) + ADAM_WD * w), nm, nv


def _adamw_recv(w, recv, m, v, tag):
    r, c = w.shape
    tr = r
    for cand in (512, 256, 128):
        if r > cand and r % cand == 0:
            tr = cand
            break

    def body(w_ref, r_ref, m_ref, v_ref, g_ref, d_ref, nm_ref, nv_ref):
        g = r_ref[0].astype(F32)
        for j in range(1, N_DEV):
            g = g + r_ref[j].astype(F32)
        g_ref[...] = g
        d_ref[...], nm_ref[...], nv_ref[...] = _adam_update(w_ref[...], g, m_ref[...], v_ref[...])

    spec = pl.BlockSpec((tr, c), lambda i: (i, 0))
    return pl.pallas_call(
        body, name="adamw_" + tag, out_shape=[jax.ShapeDtypeStruct(w.shape, F32)] * 4, grid=(r // tr,),
        in_specs=[spec, pl.BlockSpec((N_DEV, tr, c), lambda i: (0, i, 0)), spec, spec], out_specs=[spec] * 4,
        compiler_params=_params(),
    )(w, recv, m, v)


def _adamw_small(ws, gs, ms, vs):
    n = len(ws)

    def body(*refs):
        ins, outs = refs[:4 * n], refs[4 * n:]
        for a in range(n):
            d, nm, nv = _adam_update(ins[a][...], ins[n + a][...], ins[2 * n + a][...], ins[3 * n + a][...])
            outs[a][...], outs[n + a][...], outs[2 * n + a][...] = d, nm, nv

    out = pl.pallas_call(
        body, name="adamw_small", out_shape=[jax.ShapeDtypeStruct(w.shape, F32) for w in ws] * 3, compiler_params=_params(),
    )(*ws, *gs, *ms, *vs)
    return out[:n], out[n:2 * n], out[2 * n:]


def _tile(t, want):
    return want if t % want == 0 else t


def _inproj(x, g1, w_in, tm):
    t = x.shape[0]

    def body(x_ref, g_ref, w_ref, *outs):
        h = _rms(x_ref[...], g_ref[...]).astype(BF16)
        off = 0
        for o_ref, wd in zip(outs, IN_WIDTHS):
            o_ref[...] = _dot(h, w_ref[:, off:off + wd])
            off += wd

    return pl.pallas_call(
        body, name="inproj_fwd", grid=(t // tm,),
        out_shape=[jax.ShapeDtypeStruct((t, wd), F32) for wd in IN_WIDTHS],
        in_specs=[pl.BlockSpec((tm, D_MODEL), lambda i: (i, 0)), _const_spec((1, D_MODEL)), _const_spec((D_MODEL, D_IN_PAD))],
        out_specs=[pl.BlockSpec((tm, wd), lambda i: (i, 0)) for wd in IN_WIDTHS],
        compiler_params=_params(),
    )(x, g1, w_in)


def _rope_tables(seq):
    inv = 1.0 / (ROPE_THETA ** (jnp.arange(0, B_ROPE, 2, dtype=F32) / B_ROPE))
    ang = jnp.arange(seq, dtype=F32)[:, None] * inv[None, :]
    cos, sin = jnp.cos(ang), jnp.sin(ang)
    z32, z64 = jnp.zeros_like(cos), jnp.zeros((seq, 64), F32)
    cos_t = jnp.concatenate([cos, cos, z64], axis=1)
    sin_a = jnp.concatenate([-sin, z32, z64], axis=1)
    sin_b = jnp.concatenate([z32, sin, z64], axis=1)
    return cos_t, sin_a, sin_b


def _rope(t, cos_t, sin_a, sin_b):
    return t * cos_t + pltpu.roll(t, 96, 1) * sin_a + pltpu.roll(t, 32, 1) * sin_b


def _rope_t(d, cos_t, sin_a, sin_b):
    return d * cos_t + pltpu.roll(d * sin_a, 32, 1) + pltpu.roll(d * sin_b, 96, 1)


def _mla_qkv(cq, ckv, kr, g_qa, g_kva, w_q, w_kv, tables, seq, tm):
    t = cq.shape[0]
    nblk = seq // tm

    def body(cq_ref, ckv_ref, kr_ref, gq_ref, gk_ref, wq_ref, wkv_ref, c_ref, sa_ref, sb_ref, q_out, k_out, v_out):
        cos_t, sin_a, sin_b = c_ref[...], sa_ref[...], sb_ref[...]
        cqn = _rms(cq_ref[...], gq_ref[...]).astype(BF16)
        ckn = _rms(ckv_ref[...], gk_ref[...]).astype(BF16)
        kr_rot = _rope(kr_ref[...], cos_t, sin_a, sin_b).astype(BF16)
        for h in range(B_HEADS):
            lo = h * QK_PAD
            q_out[:, lo:lo + 128] = (_dot(cqn, wq_ref[:, lo:lo + 128]) * ATTN_SCALE).astype(BF16)
            qr = _rope(_dot(cqn, wq_ref[:, lo + 128:lo + 256]), cos_t, sin_a, sin_b)
            q_out[:, lo + 128:lo + 256] = (qr * ATTN_SCALE).astype(BF16)
            k_out[:, lo:lo + 128] = _dot(ckn, wkv_ref[:, lo:lo + 128]).astype(BF16)
            k_out[:, lo + 128:lo + 256] = kr_rot
            v_out[:, h * B_V:(h + 1) * B_V] = _dot(ckn, wkv_ref[:, lo + 128:lo + 256]).astype(BF16)

    tok = lambda wd: pl.BlockSpec((tm, wd), lambda i: (i, 0))
    tab = pl.BlockSpec((tm, 128), lambda i: (i % nblk, 0))
    return pl.pallas_call(
        body, name="mla_qkv_fwd", grid=(t // tm,),
        out_shape=[jax.ShapeDtypeStruct((t, B_HEADS * QK_PAD), BF16), jax.ShapeDtypeStruct((t, B_HEADS * QK_PAD), BF16),
                   jax.ShapeDtypeStruct((t, B_HEADS * B_V), BF16)],
        in_specs=[tok(Q_LORA), tok(KV_LORA), tok(128), _const_spec((1, Q_LORA)), _const_spec((1, KV_LORA)),
                  _const_spec((Q_LORA, B_HEADS * QK_PAD)), _const_spec((KV_LORA, 1024)), tab, tab, tab],
        out_specs=[tok(B_HEADS * QK_PAD), tok(B_HEADS * QK_PAD), tok(B_HEADS * B_V)],
        compiler_params=_params(),
    )(cq, ckv, kr, g_qa, g_kva, w_q, w_kv, *tables)


def _step_index(nq):
    return (pl.program_id(0) * B_HEADS + pl.program_id(1)) * nq + pl.program_id(2)


def _attn_fwd(qcat, kcat, v, nb, seq, tq, gather=()):
    t = qcat.shape[0]
    nq = seq // tq
    ng = len(gather)
    steps = nb * B_HEADS * nq

    def body(q_ref, k_ref, v_ref, *rest):
        o_ref, lse_ref = rest[ng:ng + 2]
        if ng:
            start, forward, finish = _gather_protocol(rest[:ng], rest[ng + 2:2 * ng + 2], *rest[2 * ng + 2:])
            pl.when(_step_index(nq) == 0)(start)
            pl.when(_step_index(nq) == (3 * steps) // 4)(forward)
        for j in range(tq // ATTN_SUB):
            r = pl.ds(j * ATTN_SUB, ATTN_SUB)
            s = _dot_nt(q_ref[r, :], k_ref[...])
            m = jnp.max(s, axis=-1, keepdims=True)
            p = jnp.exp(s - m)
            l = jnp.sum(p, axis=-1, keepdims=True)
            o_ref[r, :] = _dot(p.astype(BF16), v_ref[...]) / l
            lse_ref[0, r, :] = m + jnp.log(l)
        if ng:
            pl.when(_step_index(nq) == steps - 1)(finish)

    any_spec = pl.BlockSpec(memory_space=pl.ANY)
    return pl.pallas_call(
        body, name="attn_fwd", grid=(nb, B_HEADS, nq),
        out_shape=[jax.ShapeDtypeStruct((t, B_HEADS * B_V), F32), jax.ShapeDtypeStruct((B_HEADS, t, 1), F32)] + _slot_shapes(gather),
        in_specs=[pl.BlockSpec((tq, QK_PAD), lambda b, h, i: (b * nq + i, h)),
                  pl.BlockSpec((seq, QK_PAD), lambda b, h, i: (b, h)),
                  pl.BlockSpec((seq, B_V), lambda b, h, i: (b, h))] + [any_spec] * ng,
        out_specs=[pl.BlockSpec((tq, B_V), lambda b, h, i: (b * nq + i, h)),
                   pl.BlockSpec((1, tq, 1), lambda b, h, i: (h, b * nq + i, 0))] + [any_spec] * ng,
        scratch_shapes=_comm_sems(ng) if ng else [],
        compiler_params=_params(),
    )(qcat, kcat, v, *gather)


def _attn_bwd(qcat, kcat, v, o, lse, do, nb, seq, tq, exchange=()):
    t = qcat.shape[0]
    nq = seq // tq
    ne = len(exchange)
    steps = nb * B_HEADS * nq

    def body(q_ref, k_ref, v_ref, o_ref, lse_ref, do_ref, *rest):
        dq_ref, dk_ref, dv_ref = rest[ne:ne + 3]
        if ne:
            start, finish = _exchange_protocol(rest[:ne], rest[ne + 3:2 * ne + 3], [True] * ne, *rest[2 * ne + 3:])
            pl.when(_step_index(nq) == 0)(start)

        @pl.when(pl.program_id(2) == 0)
        def _():
            dv_ref[...] = jnp.zeros_like(dv_ref)
            dk_ref[...] = jnp.zeros_like(dk_ref)

        for j in range(tq // ATTN_SUB_BWD):
            r = pl.ds(j * ATTN_SUB_BWD, ATTN_SUB_BWD)
            q, k = q_ref[r, :], k_ref[...]
            do_f = do_ref[r, :]
            delta = jnp.sum(do_f * o_ref[r, :], axis=-1, keepdims=True)
            dob = do_f.astype(BF16)
            p = jnp.exp(_dot_nt(q, k) - lse_ref[0, r, :])
            ds = (p * (_dot_nt(dob, v_ref[...]) - delta)).astype(BF16)
            dq_ref[r, :] = _dot(ds, k)
            dv_ref[...] += _dot_tn(p.astype(BF16), dob)
            dk_ref[...] += _dot_tn(ds, q)
        if ne:
            pl.when(_step_index(nq) == steps - 1)(finish)

    qspec = lambda wd: pl.BlockSpec((tq, wd), lambda b, h, i: (b * nq + i, h))
    kspec = lambda wd: pl.BlockSpec((seq, wd), lambda b, h, i: (b, h))
    any_spec = pl.BlockSpec(memory_space=pl.ANY)
    return pl.pallas_call(
        body, name="attn_bwd", grid=(nb, B_HEADS, nq),
        out_shape=[jax.ShapeDtypeStruct((t, B_HEADS * QK_PAD), F32), jax.ShapeDtypeStruct((t, B_HEADS * QK_PAD), F32),
                   jax.ShapeDtypeStruct((t, B_HEADS * B_V), F32)] + _slot_shapes(exchange, [True] * ne),
        in_specs=[qspec(QK_PAD), kspec(QK_PAD), kspec(B_V), qspec(B_V),
                  pl.BlockSpec((1, tq, 1), lambda b, h, i: (h, b * nq + i, 0)), qspec(B_V)] + [any_spec] * ne,
        out_specs=[qspec(QK_PAD), kspec(QK_PAD), kspec(B_V)] + [any_spec] * ne,
        scratch_shapes=_comm_sems(ne) if ne else [],
        compiler_params=_params(),
    )(qcat, kcat, v, o, lse, do, *exchange)


def _gla_consts(reverse):
    row = lax.broadcasted_iota(jnp.int32, (CHUNK, CHUNK), 0)
    col = lax.broadcasted_iota(jnp.int32, (CHUNK, CHUNK), 1)
    causal = (row <= col) if reverse else (row >= col)
    lane = lax.broadcasted_iota(jnp.int32, (1, HEAD_PAIR), 1)
    m0 = (lane < 64).astype(F32)
    m1 = 1.0 - m0
    r2 = lax.broadcasted_iota(jnp.int32, (HEAD_PAIR, HEAD_PAIR), 0)
    c2 = lax.broadcasted_iota(jnp.int32, (HEAD_PAIR, HEAD_PAIR), 1)
    same_head = ((r2 < 64) == (c2 < 64)).astype(F32)
    return causal, m0, m1, same_head


def _gla_chunk(hq, hi, z, l0, l1, st, consts, reverse):
    causal, m0, m1, same_head = consts
    mx = jnp.maximum(l0, l1)
    e0, e1 = jnp.exp(l0 - mx), jnp.exp(l1 - mx)
    lb = e0 / (e0 + e1)
    q = hq * _sigmoid(hq)
    log_f = jnp.log(lb + (1.0 - lb) * _sigmoid(z))
    k = (1.0 - lb) * _sigmoid(-z)
    cum = _cumsum_rows(log_f, reverse)
    tot = jnp.sum(log_f, axis=0, keepdims=True)
    q_dec = q * jnp.exp(cum)
    k_inv = k * jnp.exp(-cum)
    k_end = k * jnp.exp(tot - cum)
    o = _mm_nt(q_dec, st)
    for mh in (m0, m1):
        s = jnp.where(causal, _mm_nt(q_dec * mh, k_inv), 0.0)
        o = o + _mm(s, hi) * mh
    st_new = st * jnp.exp(tot) + _mm_tn(hi, k_end) * same_head
    return o, st_new


def _gla_fwd(hq, hi, z, lbl, nb, seq, group, reverse):
    t = hq.shape[0]
    rows = group * CHUNK
    nblk = seq // rows
    n_chunks = seq // CHUNK

    def tb(i):
        return nblk - 1 - i if reverse else i

    def body(hq_ref, hi_ref, z_ref, lbl_ref, o_ref, save_ref, st_ref):
        @pl.when(pl.program_id(2) == 0)
        def _():
            st_ref[...] = jnp.zeros_like(st_ref)

        consts = _gla_consts(reverse)
        l0, l1 = lbl_ref[0:1, :], lbl_ref[1:2, :]
        st = st_ref[...]
        for cc in range(group):
            c = group - 1 - cc if reverse else cc
            r = pl.ds(c * CHUNK, CHUNK)
            save_ref[0, 0, c] = st
            o_c, st = _gla_chunk(hq_ref[r, :], hi_ref[r, :], z_ref[r, :], l0, l1, st, consts, reverse)
            o_ref[r, :] = o_c
        st_ref[...] = st

    tok = pl.BlockSpec((rows, HEAD_PAIR), lambda b, p, i: (b * nblk + tb(i), p))
    return pl.pallas_call(
        body, name="gla_fwd_rev" if reverse else "gla_fwd", grid=(nb, 4, nblk),
        out_shape=[jax.ShapeDtypeStruct((t, A_WIDTH), F32),
                   jax.ShapeDtypeStruct((nb, 4, n_chunks, HEAD_PAIR, HEAD_PAIR), F32)],
        in_specs=[tok, tok, tok, pl.BlockSpec((2, HEAD_PAIR), lambda b, p, i: (0, p))],
        out_specs=[tok, pl.BlockSpec((1, 1, group, HEAD_PAIR, HEAD_PAIR), lambda b, p, i: (b, p, tb(i), 0, 0))],
        scratch_shapes=[pltpu.VMEM((HEAD_PAIR, HEAD_PAIR), F32)],
        compiler_params=_params(),
    )(hq, hi, z, lbl)


def _gla_bwd(hq, hi, z, lbl, saved, do, nb, seq, group, reverse):
    t = hq.shape[0]
    rows = group * CHUNK
    nblk = seq // rows

    def tb(i):
        return i if reverse else nblk - 1 - i

    def body(hq_ref, hi_ref, z_ref, lbl_ref, save_ref, do_ref, dq_ref, dv_ref, dz_ref, dl_ref, dst_ref):
        @pl.when(pl.program_id(2) == 0)
        def _():
            dst_ref[...] = jnp.zeros_like(dst_ref)
            dl_ref[...] = jnp.zeros_like(dl_ref)

        consts = _gla_consts(reverse)
        l0, l1 = lbl_ref[0:1, :], lbl_ref[1:2, :]
        dst = dst_ref[...]
        dl0 = jnp.zeros((1, HEAD_PAIR), F32)
        dl1 = jnp.zeros((1, HEAD_PAIR), F32)
        fn = lambda a, b, e, f0, f1, g: _gla_chunk(a, b, e, f0, f1, g, consts, reverse)
        for cc in range(group):
            c = cc if reverse else group - 1 - cc
            r = pl.ds(c * CHUNK, CHUNK)
            _, vjp = jax.vjp(fn, hq_ref[r, :], hi_ref[r, :], z_ref[r, :], l0, l1, save_ref[0, 0, c])
            d_hq, d_hi, d_z, d_l0, d_l1, dst = vjp((do_ref[r, :], dst))
            dq_ref[r, :] = d_hq
            dv_ref[r, :] = d_hi
            dz_ref[r, :] = d_z
            dl0 = dl0 + d_l0
            dl1 = dl1 + d_l1
        dst_ref[...] = dst
        dl_ref[0, 0:1, :] += dl0
        dl_ref[0, 1:2, :] += dl1

    tok = pl.BlockSpec((rows, HEAD_PAIR), lambda b, p, i: (b * nblk + tb(i), p))
    return pl.pallas_call(
        body, name="gla_bwd_rev" if reverse else "gla_bwd", grid=(nb, 4, nblk),
        out_shape=[jax.ShapeDtypeStruct((t, A_WIDTH), F32)] * 3 + [jax.ShapeDtypeStruct((nb, 2, A_WIDTH), F32)],
        in_specs=[tok, tok, tok, pl.BlockSpec((2, HEAD_PAIR), lambda b, p, i: (0, p)),
                  pl.BlockSpec((1, 1, group, HEAD_PAIR, HEAD_PAIR), lambda b, p, i: (b, p, tb(i), 0, 0)), tok],
        out_specs=[tok, tok, tok, pl.BlockSpec((1, 2, HEAD_PAIR), lambda b, p, i: (b, 0, p))],
        scratch_shapes=[pltpu.VMEM((HEAD_PAIR, HEAD_PAIR), F32)],
        compiler_params=_params(),
    )(hq, hi, z, lbl, saved, do)


def _head_mean_matrix():
    r = lax.broadcasted_iota(jnp.int32, (A_WIDTH, A_WIDTH), 0) // 64
    c = lax.broadcasted_iota(jnp.int32, (A_WIDTH, A_WIDTH), 1) // 64
    return jnp.where(r == c, 1.0 / 64.0, 0.0).astype(BF16)


def _gla_out(o_f, o_b, hg, g, mean_mat):
    o = o_f + o_b
    ms = _group_mean(o * o, mean_mat)
    return o * lax.rsqrt(ms + EPS) * g * (hg * _sigmoid(hg))


def _gla_combine(o_f, o_b, hg, g, tm):
    t = o_f.shape[0]

    def body(of_ref, ob_ref, hg_ref, g_ref, y_ref):
        y_ref[...] = _gla_out(of_ref[...], ob_ref[...], hg_ref[...], g_ref[...], _head_mean_matrix())

    tok = pl.BlockSpec((tm, A_WIDTH), lambda i: (i, 0))
    return pl.pallas_call(
        body, name="gla_combine_fwd", grid=(t // tm,), out_shape=jax.ShapeDtypeStruct((t, A_WIDTH), F32),
        in_specs=[tok, tok, tok, _const_spec((1, A_WIDTH))], out_specs=tok, compiler_params=_params(),
    )(o_f, o_b, hg, g)


def _gla_combine_bwd(o_f, o_b, hg, g, dy, tm):
    t = o_f.shape[0]

    def body(of_ref, ob_ref, hg_ref, g_ref, dy_ref, do_ref, dhg_ref, dg_ref):
        mean_mat = _head_mean_matrix()
        fn = lambda o, hgv, gv: _gla_out(o, jnp.zeros_like(o), hgv, gv, mean_mat)
        _, vjp = jax.vjp(fn, of_ref[...] + ob_ref[...], hg_ref[...], g_ref[...])
        d_o, d_hg, d_g = vjp(dy_ref[...])
        do_ref[...] = d_o
        dhg_ref[...] = d_hg

        @pl.when(pl.program_id(0) == 0)
        def _():
            dg_ref[...] = jnp.zeros_like(dg_ref)

        dg_ref[...] += d_g

    tok = pl.BlockSpec((tm, A_WIDTH), lambda i: (i, 0))
    vec = pl.BlockSpec((1, A_WIDTH), lambda i: (0, 0))
    return pl.pallas_call(
        body, name="gla_combine_bwd", grid=(t // tm,),
        out_shape=[jax.ShapeDtypeStruct((t, A_WIDTH), F32)] * 2 + [jax.ShapeDtypeStruct((1, A_WIDTH), F32)],
        in_specs=[tok, tok, tok, _const_spec((1, A_WIDTH)), tok], out_specs=[tok, tok, vec], compiler_params=_params(),
    )(o_f, o_b, hg, g, dy)


def _post_fwd(x, ya, oattn, tgt, g_mla, w_out, g2, w_gate, w_up, w_down, g_fin, tm):
    t = x.shape[0]

    def body(x_ref, ya_ref, oa_ref, tgt_ref, gm_ref, wo_ref, g2_ref, wg_ref, wu_ref, wd_ref, gf_ref, x1_ref, x2_ref, loss_ref):
        yb = _rms(oa_ref[...], gm_ref[...])
        x1 = x_ref[...] + _dot(ya_ref[...].astype(BF16), wo_ref[0:A_WIDTH, :]) + _dot(yb.astype(BF16), wo_ref[A_WIDTH:, :])
        x1_ref[...] = x1
        h2 = _rms(x1, g2_ref[...]).astype(BF16)
        gate = _dot(h2, wg_ref[...])
        act = (gate * _sigmoid(gate) * _dot(h2, wu_ref[...])).astype(BF16)
        x2 = x1 + _dot(act, wd_ref[...])
        x2_ref[...] = x2
        err = _rms(x2, gf_ref[...]) - tgt_ref[...]
        part = 0.5 * jnp.sum(jnp.mean(err * err, axis=-1, keepdims=True), axis=0, keepdims=True)

        @pl.when(pl.program_id(0) == 0)
        def _():
            loss_ref[...] = jnp.zeros_like(loss_ref)

        loss_ref[...] += jnp.broadcast_to(part, loss_ref.shape)

    tok = lambda wd: pl.BlockSpec((tm, wd), lambda i: (i, 0))
    return pl.pallas_call(
        body, name="post_fwd", grid=(t // tm,),
        out_shape=[jax.ShapeDtypeStruct((t, D_MODEL), F32)] * 2 + [jax.ShapeDtypeStruct((1, 128), F32)],
        in_specs=[tok(D_MODEL), tok(A_WIDTH), tok(512), tok(D_MODEL), _const_spec((1, 512)), _const_spec((D_MODEL, D_MODEL)),
                  _const_spec((1, D_MODEL)), _const_spec((D_MODEL, D_FF)), _const_spec((D_MODEL, D_FF)),
                  _const_spec((D_FF, D_MODEL)), _const_spec((1, D_MODEL))],
        out_specs=[tok(D_MODEL), tok(D_MODEL), pl.BlockSpec((1, 128), lambda i: (0, 0))],
        compiler_params=_params(),
    )(x, ya, oattn, tgt, g_mla, w_out, g2, w_gate, w_up, w_down, g_fin)


def _post_bwd(x1, x2, ya, oattn, tgt, g_mla, w_out, g2, w_gate, w_up, w_down, g_fin, tm):
    t = x1.shape[0]

    def body(x1_ref, x2_ref, ya_ref, oa_ref, tgt_ref, gm_ref, wo_ref, g2_ref, wg_ref, wu_ref, wd_ref, gf_ref,
             dx1_ref, dya_ref, doa_ref, ycat_ref, dx1b_ref, h2_ref, dgate_ref, dup_ref, act_ref, dx2b_ref,
             dgm_ref, dg2_ref, dgf_ref):
        x1, x2 = x1_ref[...], x2_ref[...]
        dy = (_rms(x2, gf_ref[...]) - tgt_ref[...]) * (1.0 / D_MODEL)
        dx2, dgf = _rms_bwd(x2, gf_ref[...], dy)
        dx2b = dx2.astype(BF16)
        dx2b_ref[...] = dx2b
        h2 = _rms(x1, g2_ref[...]).astype(BF16)
        h2_ref[...] = h2
        gate, up = _dot(h2, wg_ref[...]), _dot(h2, wu_ref[...])
        sg = _sigmoid(gate)
        sl = gate * sg
        act_ref[...] = (sl * up).astype(BF16)
        dact = _dot_nt(dx2b, wd_ref[...])
        dup = (dact * sl).astype(BF16)
        dgate = (dact * up * (sg * (1.0 + gate * (1.0 - sg)))).astype(BF16)
        dup_ref[...] = dup
        dgate_ref[...] = dgate
        dh2 = _dot_nt(dgate, wg_ref[...]) + _dot_nt(dup, wu_ref[...])
        dx1n, dg2 = _rms_bwd(x1, g2_ref[...], dh2)
        dx1 = dx2 + dx1n
        dx1_ref[...] = dx1
        dx1b = dx1.astype(BF16)
        dx1b_ref[...] = dx1b
        oa = oa_ref[...]
        ycat_ref[:, 0:A_WIDTH] = ya_ref[...].astype(BF16)
        ycat_ref[:, A_WIDTH:] = _rms(oa, gm_ref[...]).astype(BF16)
        dya_ref[...] = _dot_nt(dx1b, wo_ref[0:A_WIDTH, :])
        doa, dgm = _rms_bwd(oa, gm_ref[...], _dot_nt(dx1b, wo_ref[A_WIDTH:, :]))
        doa_ref[...] = doa

        @pl.when(pl.program_id(0) == 0)
        def _():
            dgm_ref[...] = jnp.zeros_like(dgm_ref)
            dg2_ref[...] = jnp.zeros_like(dg2_ref)
            dgf_ref[...] = jnp.zeros_like(dgf_ref)

        dgm_ref[...] += dgm
        dg2_ref[...] += dg2
        dgf_ref[...] += dgf

    tok = lambda wd: pl.BlockSpec((tm, wd), lambda i: (i, 0))
    vec = lambda wd: pl.BlockSpec((1, wd), lambda i: (0, 0))
    sds = lambda wd, dt: jax.ShapeDtypeStruct((t, wd), dt)
    return pl.pallas_call(
        body, name="post_bwd", grid=(t // tm,),
        out_shape=[sds(D_MODEL, F32), sds(512, F32), sds(512, F32), sds(D_MODEL, BF16), sds(D_MODEL, BF16), sds(D_MODEL, BF16),
                   sds(D_FF, BF16), sds(D_FF, BF16), sds(D_FF, BF16), sds(D_MODEL, BF16),
                   jax.ShapeDtypeStruct((1, 512), F32), jax.ShapeDtypeStruct((1, D_MODEL), F32), jax.ShapeDtypeStruct((1, D_MODEL), F32)],
        in_specs=[tok(D_MODEL), tok(D_MODEL), tok(512), tok(512), tok(D_MODEL), _const_spec((1, 512)),
                  _const_spec((D_MODEL, D_MODEL)), _const_spec((1, D_MODEL)), _const_spec((D_MODEL, D_FF)),
                  _const_spec((D_MODEL, D_FF)), _const_spec((D_FF, D_MODEL)), _const_spec((1, D_MODEL))],
        out_specs=[tok(D_MODEL), tok(512), tok(512), tok(D_MODEL), tok(D_MODEL), tok(D_MODEL), tok(D_FF), tok(D_FF), tok(D_FF),
                   tok(D_MODEL), vec(512), vec(D_MODEL), vec(D_MODEL)],
        compiler_params=_params(),
    )(x1, x2, ya, oattn, tgt, g_mla, w_out, g2, w_gate, w_up, w_down, g_fin)


def _matmul_tn(a, b, tn, tt, tag):
    t, k = a.shape
    n = b.shape[1]
    last = t // tt - 1

    def body(a_ref, b_ref, o_ref, acc_ref):
        part = _dot_tn(a_ref[...], b_ref[...])

        @pl.when(pl.program_id(1) == 0)
        def _():
            acc_ref[...] = part

        @pl.when(pl.program_id(1) > 0)
        def _():
            acc_ref[...] += part

        @pl.when(pl.program_id(1) == last)
        def _():
            o_ref[...] = acc_ref[...].astype(o_ref.dtype)

    return pl.pallas_call(
        body, name="wgrad_" + tag, grid=(n // tn, t // tt), out_shape=jax.ShapeDtypeStruct((k, n), BF16),
        in_specs=[pl.BlockSpec((tt, k), lambda j, i: (i, 0)), pl.BlockSpec((tt, tn), lambda j, i: (i, j))],
        out_specs=pl.BlockSpec((k, tn), lambda j, i: (0, j)), scratch_shapes=[pltpu.VMEM((k, tn), F32)],
        compiler_params=_params(),
    )(a, b)


def _mla_qkv_bwd(cq, ckv, g_qa, g_kva, w_q, w_kv, tables, dq, dk, dv, seq, tm):
    t = cq.shape[0]
    nblk = seq // tm

    def body(cq_ref, ckv_ref, gq_ref, gk_ref, wq_ref, wkv_ref, c_ref, sa_ref, sb_ref, dq_ref, dk_ref, dv_ref,
             dcq_ref, dckv_ref, dkr_ref, cqn_ref, dqf_ref, ckn_ref, dkv_ref, dgq_ref, dgk_ref):
        cos_t, sin_a, sin_b = c_ref[...], sa_ref[...], sb_ref[...]
        cqn_ref[...] = _rms(cq_ref[...], gq_ref[...]).astype(BF16)
        ckn_ref[...] = _rms(ckv_ref[...], gk_ref[...]).astype(BF16)
        dkr = jnp.zeros((tm, 128), F32)
        for h in range(B_HEADS):
            lo = h * QK_PAD
            dqf_ref[:, lo:lo + 128] = (dq_ref[:, lo:lo + 128] * ATTN_SCALE).astype(BF16)
            dqf_ref[:, lo + 128:lo + 256] = _rope_t(dq_ref[:, lo + 128:lo + 256] * ATTN_SCALE, cos_t, sin_a, sin_b).astype(BF16)
            dkv_ref[:, lo:lo + 128] = dk_ref[:, lo:lo + 128].astype(BF16)
            dkv_ref[:, lo + 128:lo + 256] = dv_ref[:, h * B_V:(h + 1) * B_V].astype(BF16)
            dkr = dkr + dk_ref[:, lo + 128:lo + 256]
        dkr_ref[...] = _rope_t(dkr, cos_t, sin_a, sin_b)
        dcq, dgq = _rms_bwd(cq_ref[...], gq_ref[...], _dot_nt(dqf_ref[...], wq_ref[...]))
        dckv, dgk = _rms_bwd(ckv_ref[...], gk_ref[...], _dot_nt(dkv_ref[...], wkv_ref[...]))
        dcq_ref[...] = dcq
        dckv_ref[...] = dckv

        @pl.when(pl.program_id(0) == 0)
        def _():
            dgq_ref[...] = jnp.zeros_like(dgq_ref)
            dgk_ref[...] = jnp.zeros_like(dgk_ref)

        dgq_ref[...] += dgq
        dgk_ref[...] += dgk

    tok = lambda wd: pl.BlockSpec((tm, wd), lambda i: (i, 0))
    vec = lambda wd: pl.BlockSpec((1, wd), lambda i: (0, 0))
    tab = pl.BlockSpec((tm, 128), lambda i: (i % nblk, 0))
    sds = lambda wd, dt: jax.ShapeDtypeStruct((t, wd), dt)
    return pl.pallas_call(
        body, name="mla_qkv_bwd", grid=(t // tm,),
        out_shape=[sds(Q_LORA, F32), sds(KV_LORA, F32), sds(128, F32), sds(Q_LORA, BF16), sds(1024, BF16), sds(KV_LORA, BF16),
                   sds(1024, BF16), jax.ShapeDtypeStruct((1, Q_LORA), F32), jax.ShapeDtypeStruct((1, KV_LORA), F32)],
        in_specs=[tok(Q_LORA), tok(KV_LORA), _const_spec((1, Q_LORA)), _const_spec((1, KV_LORA)),
                  _const_spec((Q_LORA, 1024)), _const_spec((KV_LORA, 1024)), tab, tab, tab,
                  tok(1024), tok(1024), tok(512)],
        out_specs=[tok(Q_LORA), tok(KV_LORA), tok(128), tok(Q_LORA), tok(1024), tok(KV_LORA), tok(1024),
                   vec(Q_LORA), vec(KV_LORA)],
        compiler_params=_params(),
    )(cq, ckv, g_qa, g_kva, w_q, w_kv, *tables, dq, dk, dv)


def _inproj_bwd(x, g1, w_in, dx1, pieces, tm):
    t = x.shape[0]
    counts = [len(p) for p in pieces]
    flat = [a for p in pieces for a in p]
    widths = [wd for wd, p in zip(IN_WIDTHS, pieces) for _ in p]

    def body(x_ref, g_ref, w_ref, dx1_ref, *refs):
        ins = refs[:len(flat)]
        dx_ref, h_ref, dp_ref, dg_ref = refs[len(flat):]
        xv = x_ref[...]
        h_ref[...] = _rms(xv, g_ref[...]).astype(BF16)
        off, j = 0, 0
        for wd, cnt in zip(IN_WIDTHS, counts):
            acc = ins[j][...]
            for jj in range(1, cnt):
                acc = acc + ins[j + jj][...]
            dp_ref[:, off:off + wd] = acc.astype(BF16)
            off += wd
            j += cnt
        dxn, dg = _rms_bwd(xv, g_ref[...], _dot_nt(dp_ref[...], w_ref[...]))
        dx_ref[...] = dx1_ref[...] + dxn

        @pl.when(pl.program_id(0) == 0)
        def _():
            dg_ref[...] = jnp.zeros_like(dg_ref)

        dg_ref[...] += dg

    tok = lambda wd: pl.BlockSpec((tm, wd), lambda i: (i, 0))
    return pl.pallas_call(
        body, name="inproj_bwd", grid=(t // tm,),
        out_shape=[jax.ShapeDtypeStruct((t, D_MODEL), F32), jax.ShapeDtypeStruct((t, D_MODEL), BF16),
                   jax.ShapeDtypeStruct((t, D_IN_PAD), BF16), jax.ShapeDtypeStruct((1, D_MODEL), F32)],
        in_specs=[tok(D_MODEL), _const_spec((1, D_MODEL)), _const_spec((D_MODEL, D_IN_PAD)), tok(D_MODEL)] + [tok(wd) for wd in widths],
        out_specs=[tok(D_MODEL), tok(D_MODEL), tok(D_IN_PAD), pl.BlockSpec((1, D_MODEL), lambda i: (0, 0))],
        compiler_params=_params(),
    )(x, g1, w_in, dx1, *flat)


def _cols_from_slots(g):
    n, r, cs = g.shape
    return g.transpose(1, 0, 2).reshape(r, n * cs)


def _cols_to_slots(full):
    r, c = full.shape
    return full.reshape(r, N_DEV, c // N_DEV).transpose(1, 0, 2)


def _arrange_w_in(w_in):
    return jnp.concatenate([w_in, jnp.zeros((D_MODEL, D_IN_PAD - D_IN), w_in.dtype)], axis=1)


def _arrange_w_q(w_q_b):
    q3 = w_q_b.reshape(Q_LORA, B_HEADS, B_NOPE + B_ROPE)
    pad = jnp.zeros((Q_LORA, B_HEADS, QK_PAD - B_NOPE - B_ROPE), w_q_b.dtype)
    return jnp.concatenate([q3, pad], axis=2).reshape(Q_LORA, B_HEADS * QK_PAD)


def _unarrange_w_q(d_q):
    return d_q.reshape(Q_LORA, B_HEADS, QK_PAD)[:, :, :B_NOPE + B_ROPE].reshape(Q_LORA, B_HEADS * (B_NOPE + B_ROPE))


def _step_core(x, loss_target, small_w, lb_full, early_full, late, seq, group, tiles, distributed):
    g1, g_hgrn, g_qa, g_kva, g_mla, g2, g_fin = small_w
    w_in, w_q, w_kv = _arrange_w_in(early_full[0]), _arrange_w_q(early_full[1]), early_full[2]
    nb = x.shape[0]
    t = nb * seq
    tm, tq_f, tq_b, tt = tiles
    xt = x.reshape(t, D_MODEL)
    tgt = loss_target.reshape(t, D_MODEL)
    tables = _rope_tables(seq)

    hq, hi, zf, zb, hg, cq, ckv, kr = _inproj(xt, g1, w_in, tm)
    qcat, kcat, vv = _mla_qkv(cq, ckv, kr, g_qa, g_kva, w_q, w_kv, tables, seq, tm)
    if distributed:
        oattn, lse, *late_slots = _attn_fwd(qcat, kcat, vv, nb, seq, tq_f, gather=tuple(late))
    else:
        oattn, lse = _attn_fwd(qcat, kcat, vv, nb, seq, tq_f)
        late_slots = late
    w_out = late_slots[0].reshape(D_MODEL, D_MODEL)
    w_gate, w_up = _cols_from_slots(late_slots[1]), _cols_from_slots(late_slots[2])
    w_down = late_slots[3].reshape(D_FF, D_MODEL)
    lbl_f, lbl_b = lb_full[0], lb_full[1]
    o_f, save_f = _gla_fwd(hq, hi, zf, lbl_f, nb, seq, group, False)
    o_b, save_b = _gla_fwd(hq, hi, zb, lbl_b, nb, seq, group, True)
    ya = _gla_combine(o_f, o_b, hg, g_hgrn, tm)
    x1, x2, loss_row = _post_fwd(xt, ya, oattn, tgt, g_mla, w_out, g2, w_gate, w_up, w_down, g_fin, tm)

    (dx1, d_ya, d_oattn, ycat_b, dx1_b, h2_b, dgate_b, dup_b, act_b, dx2_b, d_g_mla, d_g2, d_g_fin) = _post_bwd(
        x1, x2, ya, oattn, tgt, g_mla, w_out, g2, w_gate, w_up, w_down, g_fin, tm)
    d_w_gate = _matmul_tn(h2_b, dgate_b, D_FF // 2, tt, "gate")
    d_w_up = _matmul_tn(h2_b, dup_b, D_FF // 2, tt, "up")
    d_w_down = _matmul_tn(act_b, dx2_b, 512, tt, "down")
    d_w_out = _matmul_tn(ycat_b, dx1_b, D_MODEL, tt, "out")
    late_g = [d_w_out.reshape(N_DEV, D_MODEL // N_DEV, D_MODEL), _cols_to_slots(d_w_gate), _cols_to_slots(d_w_up),
              d_w_down.reshape(N_DEV, D_FF // N_DEV, D_MODEL)]
    if distributed:
        dq, dk, dv, *late_g = _attn_bwd(qcat, kcat, vv, oattn, lse, d_oattn, nb, seq, tq_b, exchange=tuple(late_g))
    else:
        dq, dk, dv = _attn_bwd(qcat, kcat, vv, oattn, lse, d_oattn, nb, seq, tq_b)
    (d_cq, d_ckv, d_kr, cqn_b, dqf_b, ckn_b, dkv_b, d_g_qa, d_g_kva) = _mla_qkv_bwd(
        cq, ckv, g_qa, g_kva, w_q, w_kv, tables, dq, dk, dv, seq, tm)
    d_w_q = _matmul_tn(cqn_b, dqf_b, B_HEADS * QK_PAD, tt, "q_b")
    d_w_kv = _matmul_tn(ckn_b, dkv_b, B_HEADS * (B_NOPE + B_V), tt, "kv_b")
    d_o, d_hg, d_g_hgrn = _gla_combine_bwd(o_f, o_b, hg, g_hgrn, d_ya, tm)
    dq_f, dv_f, dz_f, dl_f = _gla_bwd(hq, hi, zf, lbl_f, save_f, d_o, nb, seq, group, False)
    dq_b, dv_b, dz_b, dl_b = _gla_bwd(hq, hi, zb, lbl_b, save_b, d_o, nb, seq, group, True)
    grad_x, h1_b, dproj_b, d_g1 = _inproj_bwd(
        xt, g1, w_in, dx1, [[dq_f, dq_b], [dv_f, dv_b], [dz_f], [dz_b], [d_hg], [d_cq], [d_ckv], [d_kr]], tm)
    d_w_in_arr = _matmul_tn(h1_b, dproj_b, D_IN_PAD // 2, tt, "in")

    early_g = [_cols_to_slots(d_w_in_arr[:, :D_IN]), _cols_to_slots(_unarrange_w_q(d_w_q)), _cols_to_slots(d_w_kv)]
    d_lb = jnp.stack([jnp.sum(dl_f, axis=0), jnp.sum(dl_b, axis=0)], axis=0)
    small_grads = [d_g1, d_g_hgrn, d_g_qa, d_g_kva, d_g_mla, d_g2, d_g_fin]
    return loss_row, grad_x.reshape(nb, seq, D_MODEL), early_g, late_g, small_grads, d_lb


def kernel(x, norm1_g, w_in, lb_logits, hgrn_norm_g, q_a_norm_g, w_q_b, kv_a_norm_g, w_kv_b, mla_norm_g, w_out, norm2_g, w_gate, w_up, w_down, final_norm_g, loss_target, m_norm1_g, m_w_in, m_lb_logits, m_hgrn_norm_g, m_q_a_norm_g, m_w_q_b, m_kv_a_norm_g, m_w_kv_b, m_mla_norm_g, m_w_out, m_norm2_g, m_w_gate, m_w_up, m_w_down, m_final_norm_g, v_norm1_g, v_w_in, v_lb_logits, v_hgrn_norm_g, v_q_a_norm_g, v_w_q_b, v_kv_a_norm_g, v_w_kv_b, v_mla_norm_g, v_w_out, v_norm2_g, v_w_gate, v_w_up, v_w_down, v_final_norm_g):
    big_w = [w_in, w_q_b, w_kv_b, w_out, w_gate, w_up, w_down]
    big_m = [m_w_in, m_w_q_b, m_w_kv_b, m_w_out, m_w_gate, m_w_up, m_w_down]
    big_v = [v_w_in, v_w_q_b, v_w_kv_b, v_w_out, v_w_gate, v_w_up, v_w_down]
    small_w = [norm1_g, hgrn_norm_g, q_a_norm_g, kv_a_norm_g, mla_norm_g, norm2_g, final_norm_g]
    small_m = [m_norm1_g, m_hgrn_norm_g, m_q_a_norm_g, m_kv_a_norm_g, m_mla_norm_g, m_norm2_g, m_final_norm_g]
    small_v = [v_norm1_g, v_hgrn_norm_g, v_q_a_norm_g, v_kv_a_norm_g, v_mla_norm_g, v_norm2_g, v_final_norm_g]
    seq = x.shape[1]
    my_id = 4 * lax.axis_index("x") + 2 * lax.axis_index("y") + lax.axis_index("c")

    shard = lambda w: w[0].astype(BF16)
    g_in, g_q, g_kv, g_lb = _all_gather_call([shard(w_in), shard(w_q_b), shard(w_kv_b), lb_logits.reshape(4, 64)])
    early_full = (_cols_from_slots(g_in), _cols_from_slots(g_q), _cols_from_slots(g_kv))
    lb_full = g_lb.reshape(N_DEV, 2, 2, 64).transpose(1, 2, 0, 3).reshape(2, 2, 512)

    as_row = lambda a: a.reshape(1, -1)
    loss_row, grad_x, early_g, late_recv, small_g, d_lb = _step_core(
        x, loss_target, [as_row(s) for s in small_w], lb_full, early_full,
        [shard(w_out), shard(w_gate), shard(w_up), shard(w_down)], seq, min(8, seq // CHUNK),
        (256, min(1024, seq), min(512, seq), 512), True)

    n_small = len(small_g)
    recv = _exchange_call(early_g + small_g + [d_lb.reshape(4, 512), loss_row], [True] * 3 + [False] * (n_small + 2))
    sums = _sum_slots_call(recv[3:])
    g_small = [g.reshape(s.shape) for g, s in zip(sums[:n_small], small_w)]
    g_lb_own = lax.dynamic_index_in_dim(sums[n_small].reshape(2, 2, N_DEV, 64), my_id, axis=2, keepdims=False)
    loss = sums[n_small + 1][0, 0]

    grads, deltas, new_ms, new_vs = {}, {}, {}, {}
    big_recv = dict(zip(["w_in", "w_q_b", "w_kv_b", "w_out", "w_gate", "w_up", "w_down"], list(recv[:3]) + list(late_recv)))
    for (name, _, _, _), w, m, v in zip(BIG, big_w, big_m, big_v):
        g, d, nm, nv = _adamw_recv(w[0], big_recv[name], m[0], v[0], name)
        grads[name], deltas[name], new_ms[name], new_vs[name] = g[None], d[None], nm[None], nv[None]
    lb_rows = lambda a: a.reshape(4, 64)
    d_s, nm_s, nv_s = _adamw_small(
        [as_row(a) for a in small_w] + [lb_rows(lb_logits)], [as_row(a) for a in g_small] + [lb_rows(g_lb_own)],
        [as_row(a) for a in small_m] + [lb_rows(m_lb_logits)], [as_row(a) for a in small_v] + [lb_rows(v_lb_logits)])
    for i, (s, (name, _)) in enumerate(zip(small_w + [lb_logits], SMALL + (("lb_logits", 0),))):
        grads[name] = (g_small + [g_lb_own])[i]
        deltas[name], new_ms[name], new_vs[name] = d_s[i].reshape(s.shape), nm_s[i].reshape(s.shape), nv_s[i].reshape(s.shape)

    order = ["norm1_g", "w_in", "lb_logits", "hgrn_norm_g", "q_a_norm_g", "w_q_b", "kv_a_norm_g", "w_kv_b", "mla_norm_g",
             "w_out", "norm2_g", "w_gate", "w_up", "w_down", "final_norm_g"]
    return (loss, grad_x, *[grads[n] for n in order], *[deltas[n] for n in order],
            *[new_ms[n] for n in order], *[new_vs[n] for n in order])
```

```python
import functools
import math

import jax
import jax.numpy as jnp
from jax import lax
from jax.experimental import pallas as pl
from jax.experimental.pallas import tpu as pltpu

F32 = jnp.float32
BF16 = jnp.bfloat16

N_DEV = 8
D_MODEL = 1024
D_FF = 2816
A_WIDTH = 512
HEAD_PAIR = 128
CHUNK = 64
B_HEADS = 4
B_NOPE = 128
B_ROPE = 64
B_V = 128
QK_PAD = 256
Q_LORA = 384
KV_LORA = 256
D_IN = 3264
D_IN_PAD = 3328
IN_WIDTHS = (512, 512, 512, 512, 512, Q_LORA, KV_LORA, 128)
ROPE_THETA = 10000.0
EPS = 1e-6
ATTN_SCALE = (B_NOPE + B_ROPE) ** -0.5
ATTN_SUB = 256
ATTN_SUB_BWD = 256
ROW_SUB = 256
ADAM_LR, ADAM_B1, ADAM_B2, ADAM_EPS, ADAM_WD, ADAM_STEP = 0.001, 0.9, 0.999, 1e-08, 0.01, 10
VMEM_LIMIT = 56 * 1024 * 1024
MESH = pl.DeviceIdType.MESH

BIG = (("w_in", 1024, D_IN, 1), ("w_q_b", Q_LORA, 768, 1), ("w_kv_b", KV_LORA, 1024, 1), ("w_out", 1024, 1024, 0),
       ("w_gate", 1024, D_FF, 1), ("w_up", 1024, D_FF, 1), ("w_down", D_FF, 1024, 0))
SMALL = (("norm1_g", 1024), ("hgrn_norm_g", 512), ("q_a_norm_g", 384), ("kv_a_norm_g", 256), ("mla_norm_g", 512),
         ("norm2_g", 1024), ("final_norm_g", 1024))


def _params(**kw):
    return pltpu.CompilerParams(vmem_limit_bytes=VMEM_LIMIT, **kw)


def _const_spec(shape):
    return pl.BlockSpec(shape, lambda *_: (0,) * len(shape), pipeline_mode=pl.Buffered(1))


def _dot(a, b):
    return jnp.dot(a, b, preferred_element_type=F32)


def _dot_nt(a, b):
    return lax.dot_general(a, b, (((1,), (1,)), ((), ())), preferred_element_type=F32)


def _dot_tn(a, b):
    return lax.dot_general(a, b, (((0,), (0,)), ((), ())), preferred_element_type=F32)


@jax.custom_vjp
def _mm(a, b):
    return _dot(a.astype(BF16), b.astype(BF16))


def _mm_fwd(a, b):
    return _mm(a, b), (a, b)


def _mm_bwd(res, g):
    a, b = res
    gb = g.astype(BF16)
    return _dot_nt(gb, b.astype(BF16)), _dot_tn(a.astype(BF16), gb)


_mm.defvjp(_mm_fwd, _mm_bwd)


@jax.custom_vjp
def _mm_nt(a, b):
    return _dot_nt(a.astype(BF16), b.astype(BF16))


def _mm_nt_fwd(a, b):
    return _mm_nt(a, b), (a, b)


def _mm_nt_bwd(res, g):
    a, b = res
    gb = g.astype(BF16)
    return _dot(gb, b.astype(BF16)), _dot_tn(gb, a.astype(BF16))


_mm_nt.defvjp(_mm_nt_fwd, _mm_nt_bwd)


@jax.custom_vjp
def _mm_tn(a, b):
    return _dot_tn(a.astype(BF16), b.astype(BF16))


def _mm_tn_fwd(a, b):
    return _mm_tn(a, b), (a, b)


def _mm_tn_bwd(res, g):
    a, b = res
    gb = g.astype(BF16)
    return _dot_nt(b.astype(BF16), gb), _dot(a.astype(BF16), gb)


_mm_tn.defvjp(_mm_tn_fwd, _mm_tn_bwd)


def _split3(a):
    hi = a.astype(BF16)
    r = a - hi.astype(F32)
    mid = r.astype(BF16)
    lo = (r - mid.astype(F32)).astype(BF16)
    return hi, mid, lo


def _dot_exact_rhs(a, m):
    hi, mid, lo = _split3(a)
    return _dot(hi, m) + _dot(mid, m) + _dot(lo, m)


@jax.custom_vjp
def _group_mean(a, m):
    return _dot_exact_rhs(a, m)


def _group_mean_fwd(a, m):
    return _group_mean(a, m), m


def _group_mean_bwd(m, g):
    return _dot_exact_rhs(g, m), jnp.zeros_like(m)


_group_mean.defvjp(_group_mean_fwd, _group_mean_bwd)


def _roll_rows(a, shift):
    return pltpu.roll(a, shift, 0)


def _cumsum_rows_raw(a, reverse):
    n = a.shape[0]
    row = lax.broadcasted_iota(jnp.int32, a.shape, 0)
    s = 1
    while s < n:
        if reverse:
            a = a + jnp.where(row < n - s, _roll_rows(a, n - s), 0.0)
        else:
            a = a + jnp.where(row >= s, _roll_rows(a, s), 0.0)
        s *= 2
    return a


@functools.partial(jax.custom_vjp, nondiff_argnums=(1,))
def _cumsum_rows(a, reverse):
    return _cumsum_rows_raw(a, reverse)


def _cumsum_rows_fwd(a, reverse):
    return _cumsum_rows_raw(a, reverse), None


def _cumsum_rows_bwd(reverse, _, g):
    return (_cumsum_rows_raw(g, not reverse),)


_cumsum_rows.defvjp(_cumsum_rows_fwd, _cumsum_rows_bwd)


def _rms(x, g):
    r = lax.rsqrt(jnp.mean(x * x, axis=-1, keepdims=True) + EPS)
    return x * r * g


def _rms_bwd(x, g, dy):
    r = lax.rsqrt(jnp.mean(x * x, axis=-1, keepdims=True) + EPS)
    xh = x * r
    dg = jnp.sum(dy * xh, axis=0, keepdims=True)
    dxh = dy * g
    dx = r * (dxh - xh * jnp.mean(dxh * xh, axis=-1, keepdims=True))
    return dx, dg


def _sigmoid(a):
    return jax.nn.sigmoid(a)


def _mesh_place():
    x, y, c = lax.axis_index("x"), lax.axis_index("y"), lax.axis_index("c")
    return x, y, c


def _dev_index(p):
    return 4 * p[0] + 2 * p[1] + p[2]


def _comm_sems(n):
    return [pltpu.SemaphoreType.DMA((n, 7)), pltpu.SemaphoreType.DMA((n, 7)), pltpu.SemaphoreType.DMA((n,))]


def _gather_protocol(ins, outs, send_sems, recv_sems, local_sems):
    n = len(ins)
    x, y, c = _mesh_place()
    me, sibling = (x, y, c), (x, y, 1 - c)
    chips = [(1 - x, y), (x, 1 - y), (1 - x, 1 - y)]

    def copy(a, k, block, to, src=None):
        slot = outs[a].at[_dev_index(block)]
        return pltpu.make_async_remote_copy(
            src_ref=slot if src is None else src, dst_ref=slot,
            send_sem=send_sems.at[a, k], recv_sem=recv_sems.at[a, k], device_id=to, device_id_type=MESH)

    def mine(a):
        return pltpu.make_async_copy(ins[a], outs[a].at[_dev_index(me)], local_sems.at[a])

    def first(a):
        return [copy(a, 0, me, sibling, src=ins[a])] + [copy(a, 1 + j, me, (*chip, c), src=ins[a]) for j, chip in enumerate(chips)]

    def start():
        for a in range(n):
            mine(a).start()
            for cp in first(a):
                cp.start()

    def forward():
        for a in range(n):
            for j, chip in enumerate(chips):
                copy(a, 1 + j, (*chip, c), me).wait_recv()
                copy(a, 4 + j, (*chip, c), sibling).start()

    def finish():
        for a in range(n):
            copy(a, 0, sibling, me).wait_recv()
            for j, chip in enumerate(chips):
                copy(a, 4 + j, (*chip, 1 - c), me).wait_recv()
        for a in range(n):
            mine(a).wait()
            for cp in first(a):
                cp.wait_send()
            for j, chip in enumerate(chips):
                copy(a, 4 + j, (*chip, c), sibling).wait_send()

    return start, forward, finish


def _exchange_protocol(ins, outs, scatter, send_sems, recv_sems, local_sems):
    n = len(ins)
    x, y, c = _mesh_place()
    me = (x, y, c)
    my_id = _dev_index(me)
    rels = [(dx, dy, dc) for dx in (0, 1) for dy in (0, 1) for dc in (0, 1)][1:]

    def peer_of(rel):
        return tuple(1 - v if d else v for v, d in zip(me, rel))

    def src(a, dev):
        return ins[a].at[dev] if scatter[a] else ins[a]

    def send(a, k):
        peer = peer_of(rels[k])
        return pltpu.make_async_remote_copy(
            src_ref=src(a, _dev_index(peer)), dst_ref=outs[a].at[my_id],
            send_sem=send_sems.at[a, k], recv_sem=recv_sems.at[a, k], device_id=peer, device_id_type=MESH)

    def arrival(a, k):
        peer = peer_of(rels[k])
        return pltpu.make_async_remote_copy(
            src_ref=src(a, my_id), dst_ref=outs[a].at[_dev_index(peer)],
            send_sem=send_sems.at[a, k], recv_sem=recv_sems.at[a, k], device_id=peer, device_id_type=MESH)

    def own(a):
        return pltpu.make_async_copy(src(a, my_id), outs[a].at[my_id], local_sems.at[a])

    def start():
        for a in range(n):
            own(a).start()
            for k in range(7):
                send(a, k).start()

    def finish():
        for a in range(n):
            for k in range(7):
                arrival(a, k).wait_recv()
        for a in range(n):
            for k in range(7):
                send(a, k).wait_send()
            own(a).wait()

    return start, finish


def _slot_shapes(blocks, scatter=None):
    return [jax.ShapeDtypeStruct(b.shape if (scatter and scatter[a]) else (N_DEV,) + b.shape, b.dtype) for a, b in enumerate(blocks)]


def _all_gather_call(blocks):
    n = len(blocks)

    def body(*refs):
        start, forward, finish = _gather_protocol(refs[:n], refs[n:2 * n], *refs[2 * n:])
        start()
        forward()
        finish()

    any_spec = pl.BlockSpec(memory_space=pl.ANY)
    return pl.pallas_call(
        body, name="weights_all_gather", out_shape=_slot_shapes(blocks),
        in_specs=[any_spec] * n, out_specs=[any_spec] * n, scratch_shapes=_comm_sems(n),
    )(*blocks)


def _exchange_call(blocks, scatter):
    n = len(blocks)

    def body(*refs):
        start, finish = _exchange_protocol(refs[:n], refs[n:2 * n], scatter, *refs[2 * n:])
        start()
        finish()

    any_spec = pl.BlockSpec(memory_space=pl.ANY)
    return pl.pallas_call(
        body, name="grad_exchange", out_shape=_slot_shapes(blocks, scatter),
        in_specs=[any_spec] * n, out_specs=[any_spec] * n, scratch_shapes=_comm_sems(n),
    )(*blocks)


def _sum_slots_call(recvs):
    n = len(recvs)

    def body(*refs):
        for in_ref, out_ref in zip(refs[:n], refs[n:]):
            acc = in_ref[0]
            for j in range(1, N_DEV):
                acc = acc + in_ref[j]
            out_ref[...] = acc

    return pl.pallas_call(
        body, name="small_grad_sum", out_shape=[jax.ShapeDtypeStruct(r.shape[1:], F32) for r in recvs],
        compiler_params=_params(),
    )(*recvs)


def _adam_update(w, g, m, v):
    nm = ADAM_B1 * m + (1.0 - ADAM_B1) * g
    nv = ADAM_B2 * v + (1.0 - ADAM_B2) * (g * g)
    bc1 = 1.0 - ADAM_B1 ** ADAM_STEP
    bc2 = 1.0 - ADAM_B2 ** ADAM_STEP
    return -ADAM_LR * ((nm / bc1) / (jnp.sqrt(nv / bc2) + ADAM_EPS) + ADAM_WD * w), nm, nv


def _adamw_recv(w, recv, m, v, tag):
    r, c = w.shape
    tr = r
    for cand in (512, 256, 128):
        if r > cand and r % cand == 0:
            tr = cand
            break

    def body(w_ref, r_ref, m_ref, v_ref, g_ref, d_ref, nm_ref, nv_ref):
        g = r_ref[0].astype(F32)
        for j in range(1, N_DEV):
            g = g + r_ref[j].astype(F32)
        g_ref[...] = g
        d_ref[...], nm_ref[...], nv_ref[...] = _adam_update(w_ref[...], g, m_ref[...], v_ref[...])

    spec = pl.BlockSpec((tr, c), lambda i: (i, 0))
    return pl.pallas_call(
        body, name="adamw_" + tag, out_shape=[jax.ShapeDtypeStruct(w.shape, F32)] * 4, grid=(r // tr,),
        in_specs=[spec, pl.BlockSpec((N_DEV, tr, c), lambda i: (0, i, 0)), spec, spec], out_specs=[spec] * 4,
        compiler_params=_params(),
    )(w, recv, m, v)


def _adamw_small(ws, gs, ms, vs):
    n = len(ws)

    def body(*refs):
        ins, outs = refs[:4 * n], refs[4 * n:]
        for a in range(n):
            d, nm, nv = _adam_update(ins[a][...], ins[n + a][...], ins[2 * n + a][...], ins[3 * n + a][...])
            outs[a][...], outs[n + a][...], outs[2 * n + a][...] = d, nm, nv

    out = pl.pallas_call(
        body, name="adamw_small", out_shape=[jax.ShapeDtypeStruct(w.shape, F32) for w in ws] * 3, compiler_params=_params(),
    )(*ws, *gs, *ms, *vs)
    return out[:n], out[n:2 * n], out[2 * n:]


def _tile(t, want):
    return want if t % want == 0 else t


def _inproj(x, g1, w_in, tm):
    t = x.shape[0]

    def body(x_ref, g_ref, w_ref, *outs):
        for j in range(tm // min(tm, ROW_SUB)):
            r = pl.ds(j * min(tm, ROW_SUB), min(tm, ROW_SUB))
            h = _rms(x_ref[r, :], g_ref[...]).astype(BF16)
            off = 0
            for o_ref, wd in zip(outs, IN_WIDTHS):
                o_ref[r, :] = _dot(h, w_ref[:, off:off + wd])
                off += wd

    return pl.pallas_call(
        body, name="inproj_fwd", grid=(t // tm,),
        out_shape=[jax.ShapeDtypeStruct((t, wd), F32) for wd in IN_WIDTHS],
        in_specs=[pl.BlockSpec((tm, D_MODEL), lambda i: (i, 0)), _const_spec((1, D_MODEL)), _const_spec((D_MODEL, D_IN_PAD))],
        out_specs=[pl.BlockSpec((tm, wd), lambda i: (i, 0)) for wd in IN_WIDTHS],
        compiler_params=_params(),
    )(x, g1, w_in)


def _rope_tables(seq):
    inv = 1.0 / (ROPE_THETA ** (jnp.arange(0, B_ROPE, 2, dtype=F32) / B_ROPE))
    ang = jnp.arange(seq, dtype=F32)[:, None] * inv[None, :]
    cos, sin = jnp.cos(ang), jnp.sin(ang)
    z32, z64 = jnp.zeros_like(cos), jnp.zeros((seq, 64), F32)
    cos_t = jnp.concatenate([cos, cos, z64], axis=1)
    sin_a = jnp.concatenate([-sin, z32, z64], axis=1)
    sin_b = jnp.concatenate([z32, sin, z64], axis=1)
    return cos_t, sin_a, sin_b


def _rope(t, cos_t, sin_a, sin_b):
    return t * cos_t + pltpu.roll(t, 96, 1) * sin_a + pltpu.roll(t, 32, 1) * sin_b


def _rope_t(d, cos_t, sin_a, sin_b):
    return d * cos_t + pltpu.roll(d * sin_a, 32, 1) + pltpu.roll(d * sin_b, 96, 1)


def _mla_qkv(cq, ckv, kr, g_qa, g_kva, w_q, w_kv, tables, seq, tm):
    t = cq.shape[0]
    nblk = seq // tm

    def body(cq_ref, ckv_ref, kr_ref, gq_ref, gk_ref, wq_ref, wkv_ref, c_ref, sa_ref, sb_ref, q_out, k_out, v_out):
        cos_t, sin_a, sin_b = c_ref[...], sa_ref[...], sb_ref[...]
        cqn = _rms(cq_ref[...], gq_ref[...]).astype(BF16)
        ckn = _rms(ckv_ref[...], gk_ref[...]).astype(BF16)
        kr_rot = _rope(kr_ref[...], cos_t, sin_a, sin_b).astype(BF16)
        for h in range(B_HEADS):
            lo = h * QK_PAD
            q_out[:, lo:lo + 128] = (_dot(cqn, wq_ref[:, lo:lo + 128]) * ATTN_SCALE).astype(BF16)
            qr = _rope(_dot(cqn, wq_ref[:, lo + 128:lo + 256]), cos_t, sin_a, sin_b)
            q_out[:, lo + 128:lo + 256] = (qr * ATTN_SCALE).astype(BF16)
            k_out[:, lo:lo + 128] = _dot(ckn, wkv_ref[:, lo:lo + 128]).astype(BF16)
            k_out[:, lo + 128:lo + 256] = kr_rot
            v_out[:, h * B_V:(h + 1) * B_V] = _dot(ckn, wkv_ref[:, lo + 128:lo + 256]).astype(BF16)

    tok = lambda wd: pl.BlockSpec((tm, wd), lambda i: (i, 0))
    tab = pl.BlockSpec((tm, 128), lambda i: (i % nblk, 0))
    return pl.pallas_call(
        body, name="mla_qkv_fwd", grid=(t // tm,),
        out_shape=[jax.ShapeDtypeStruct((t, B_HEADS * QK_PAD), BF16), jax.ShapeDtypeStruct((t, B_HEADS * QK_PAD), BF16),
                   jax.ShapeDtypeStruct((t, B_HEADS * B_V), BF16)],
        in_specs=[tok(Q_LORA), tok(KV_LORA), tok(128), _const_spec((1, Q_LORA)), _const_spec((1, KV_LORA)),
                  _const_spec((Q_LORA, B_HEADS * QK_PAD)), _const_spec((KV_LORA, 1024)), tab, tab, tab],
        out_specs=[tok(B_HEADS * QK_PAD), tok(B_HEADS * QK_PAD), tok(B_HEADS * B_V)],
        compiler_params=_params(),
    )(cq, ckv, kr, g_qa, g_kva, w_q, w_kv, *tables)


def _step_index(nq):
    return (pl.program_id(0) * B_HEADS + pl.program_id(1)) * nq + pl.program_id(2)


def _attn_fwd(qcat, kcat, v, nb, seq, tq, gather=()):
    t = qcat.shape[0]
    nq = seq // tq
    ng = len(gather)
    steps = nb * B_HEADS * nq

    def body(q_ref, k_ref, v_ref, *rest):
        o_ref, lse_ref = rest[ng:ng + 2]
        if ng:
            start, forward, finish = _gather_protocol(rest[:ng], rest[ng + 2:2 * ng + 2], *rest[2 * ng + 2:])
            pl.when(_step_index(nq) == 0)(start)
            pl.when(_step_index(nq) == (3 * steps) // 4)(forward)
        for j in range(tq // ATTN_SUB):
            r = pl.ds(j * ATTN_SUB, ATTN_SUB)
            s = _dot_nt(q_ref[r, :], k_ref[...])
            m = jnp.max(s, axis=-1, keepdims=True)
            p = jnp.exp(s - m)
            l = jnp.sum(p, axis=-1, keepdims=True)
            o_ref[r, :] = _dot(p.astype(BF16), v_ref[...]) / l
            lse_ref[0, r, :] = m + jnp.log(l)
        if ng:
            pl.when(_step_index(nq) == steps - 1)(finish)

    any_spec = pl.BlockSpec(memory_space=pl.ANY)
    return pl.pallas_call(
        body, name="attn_fwd", grid=(nb, B_HEADS, nq),
        out_shape=[jax.ShapeDtypeStruct((t, B_HEADS * B_V), F32), jax.ShapeDtypeStruct((B_HEADS, t, 1), F32)] + _slot_shapes(gather),
        in_specs=[pl.BlockSpec((tq, QK_PAD), lambda b, h, i: (b * nq + i, h)),
                  pl.BlockSpec((seq, QK_PAD), lambda b, h, i: (b, h)),
                  pl.BlockSpec((seq, B_V), lambda b, h, i: (b, h))] + [any_spec] * ng,
        out_specs=[pl.BlockSpec((tq, B_V), lambda b, h, i: (b * nq + i, h)),
                   pl.BlockSpec((1, tq, 1), lambda b, h, i: (h, b * nq + i, 0))] + [any_spec] * ng,
        scratch_shapes=_comm_sems(ng) if ng else [],
        compiler_params=_params(),
    )(qcat, kcat, v, *gather)


def _attn_bwd(qcat, kcat, v, o, lse, do, nb, seq, tq, exchange=()):
    t = qcat.shape[0]
    nq = seq // tq
    ne = len(exchange)
    steps = nb * B_HEADS * nq

    def body(q_ref, k_ref, v_ref, o_ref, lse_ref, do_ref, *rest):
        dq_ref, dk_ref, dv_ref = rest[ne:ne + 3]
        if ne:
            start, finish = _exchange_protocol(rest[:ne], rest[ne + 3:2 * ne + 3], [True] * ne, *rest[2 * ne + 3:])
            pl.when(_step_index(nq) == 0)(start)

        @pl.when(pl.program_id(2) == 0)
        def _():
            dv_ref[...] = jnp.zeros_like(dv_ref)
            dk_ref[...] = jnp.zeros_like(dk_ref)

        for j in range(tq // ATTN_SUB_BWD):
            r = pl.ds(j * ATTN_SUB_BWD, ATTN_SUB_BWD)
            q, k = q_ref[r, :], k_ref[...]
            do_f = do_ref[r, :]
            delta = jnp.sum(do_f * o_ref[r, :], axis=-1, keepdims=True)
            dob = do_f.astype(BF16)
            p = jnp.exp(_dot_nt(q, k) - lse_ref[0, r, :])
            ds = (p * (_dot_nt(dob, v_ref[...]) - delta)).astype(BF16)
            dq_ref[r, :] = _dot(ds, k)
            dv_ref[...] += _dot_tn(p.astype(BF16), dob)
            dk_ref[...] += _dot_tn(ds, q)
        if ne:
            pl.when(_step_index(nq) == steps - 1)(finish)

    qspec = lambda wd: pl.BlockSpec((tq, wd), lambda b, h, i: (b * nq + i, h))
    kspec = lambda wd: pl.BlockSpec((seq, wd), lambda b, h, i: (b, h))
    any_spec = pl.BlockSpec(memory_space=pl.ANY)
    return pl.pallas_call(
        body, name="attn_bwd", grid=(nb, B_HEADS, nq),
        out_shape=[jax.ShapeDtypeStruct((t, B_HEADS * QK_PAD), F32), jax.ShapeDtypeStruct((t, B_HEADS * QK_PAD), F32),
                   jax.ShapeDtypeStruct((t, B_HEADS * B_V), F32)] + _slot_shapes(exchange, [True] * ne),
        in_specs=[qspec(QK_PAD), kspec(QK_PAD), kspec(B_V), qspec(B_V),
                  pl.BlockSpec((1, tq, 1), lambda b, h, i: (h, b * nq + i, 0)), qspec(B_V)] + [any_spec] * ne,
        out_specs=[qspec(QK_PAD), kspec(QK_PAD), kspec(B_V)] + [any_spec] * ne,
        scratch_shapes=_comm_sems(ne) if ne else [],
        compiler_params=_params(),
    )(qcat, kcat, v, o, lse, do, *exchange)


def _gla_consts(reverse):
    row = lax.broadcasted_iota(jnp.int32, (CHUNK, CHUNK), 0)
    col = lax.broadcasted_iota(jnp.int32, (CHUNK, CHUNK), 1)
    causal = (row <= col) if reverse else (row >= col)
    lane = lax.broadcasted_iota(jnp.int32, (1, HEAD_PAIR), 1)
    m0 = (lane < 64).astype(F32)
    m1 = 1.0 - m0
    r2 = lax.broadcasted_iota(jnp.int32, (HEAD_PAIR, HEAD_PAIR), 0)
    c2 = lax.broadcasted_iota(jnp.int32, (HEAD_PAIR, HEAD_PAIR), 1)
    same_head = ((r2 < 64) == (c2 < 64)).astype(F32)
    return causal, m0, m1, same_head


def _gla_chunk(hq, hi, z, l0, l1, st, consts, reverse):
    causal, m0, m1, same_head = consts
    mx = jnp.maximum(l0, l1)
    e0, e1 = jnp.exp(l0 - mx), jnp.exp(l1 - mx)
    lb = e0 / (e0 + e1)
    q = hq * _sigmoid(hq)
    log_f = jnp.log(lb + (1.0 - lb) * _sigmoid(z))
    k = (1.0 - lb) * _sigmoid(-z)
    cum = _cumsum_rows(log_f, reverse)
    tot = jnp.sum(log_f, axis=0, keepdims=True)
    q_dec = q * jnp.exp(cum)
    k_inv = k * jnp.exp(-cum)
    k_end = k * jnp.exp(tot - cum)
    o = _mm_nt(q_dec, st)
    for mh in (m0, m1):
        s = jnp.where(causal, _mm_nt(q_dec * mh, k_inv), 0.0)
        o = o + _mm(s, hi) * mh
    st_new = st * jnp.exp(tot) + _mm_tn(hi, k_end) * same_head
    return o, st_new


def _gla_fwd(hq, hi, z, lbl, nb, seq, group, reverse):
    t = hq.shape[0]
    rows = group * CHUNK
    nblk = seq // rows
    n_chunks = seq // CHUNK

    def tb(i):
        return nblk - 1 - i if reverse else i

    def body(hq_ref, hi_ref, z_ref, lbl_ref, o_ref, save_ref, st_ref):
        @pl.when(pl.program_id(2) == 0)
        def _():
            st_ref[...] = jnp.zeros_like(st_ref)

        consts = _gla_consts(reverse)
        l0, l1 = lbl_ref[0:1, :], lbl_ref[1:2, :]
        st = st_ref[...]
        for cc in range(group):
            c = group - 1 - cc if reverse else cc
            r = pl.ds(c * CHUNK, CHUNK)
            save_ref[0, 0, c] = st
            o_c, st = _gla_chunk(hq_ref[r, :], hi_ref[r, :], z_ref[r, :], l0, l1, st, consts, reverse)
            o_ref[r, :] = o_c
        st_ref[...] = st

    tok = pl.BlockSpec((rows, HEAD_PAIR), lambda b, p, i: (b * nblk + tb(i), p))
    return pl.pallas_call(
        body, name="gla_fwd_rev" if reverse else "gla_fwd", grid=(nb, 4, nblk),
        out_shape=[jax.ShapeDtypeStruct((t, A_WIDTH), F32),
                   jax.ShapeDtypeStruct((nb, 4, n_chunks, HEAD_PAIR, HEAD_PAIR), F32)],
        in_specs=[tok, tok, tok, pl.BlockSpec((2, HEAD_PAIR), lambda b, p, i: (0, p))],
        out_specs=[tok, pl.BlockSpec((1, 1, group, HEAD_PAIR, HEAD_PAIR), lambda b, p, i: (b, p, tb(i), 0, 0))],
        scratch_shapes=[pltpu.VMEM((HEAD_PAIR, HEAD_PAIR), F32)],
        compiler_params=_params(),
    )(hq, hi, z, lbl)


def _gla_bwd(hq, hi, z, lbl, saved, do, nb, seq, group, reverse):
    t = hq.shape[0]
    rows = group * CHUNK
    nblk = seq // rows

    def tb(i):
        return i if reverse else nblk - 1 - i

    def body(hq_ref, hi_ref, z_ref, lbl_ref, save_ref, do_ref, dq_ref, dv_ref, dz_ref, dl_ref, dst_ref):
        @pl.when(pl.program_id(2) == 0)
        def _():
            dst_ref[...] = jnp.zeros_like(dst_ref)
            dl_ref[...] = jnp.zeros_like(dl_ref)

        consts = _gla_consts(reverse)
        l0, l1 = lbl_ref[0:1, :], lbl_ref[1:2, :]
        dst = dst_ref[...]
        dl0 = jnp.zeros((1, HEAD_PAIR), F32)
        dl1 = jnp.zeros((1, HEAD_PAIR), F32)
        fn = lambda a, b, e, f0, f1, g: _gla_chunk(a, b, e, f0, f1, g, consts, reverse)
        for cc in range(group):
            c = cc if reverse else group - 1 - cc
            r = pl.ds(c * CHUNK, CHUNK)
            _, vjp = jax.vjp(fn, hq_ref[r, :], hi_ref[r, :], z_ref[r, :], l0, l1, save_ref[0, 0, c])
            d_hq, d_hi, d_z, d_l0, d_l1, dst = vjp((do_ref[r, :], dst))
            dq_ref[r, :] = d_hq
            dv_ref[r, :] = d_hi
            dz_ref[r, :] = d_z
            dl0 = dl0 + d_l0
            dl1 = dl1 + d_l1
        dst_ref[...] = dst
        dl_ref[0, 0:1, :] += dl0
        dl_ref[0, 1:2, :] += dl1

    tok = pl.BlockSpec((rows, HEAD_PAIR), lambda b, p, i: (b * nblk + tb(i), p))
    return pl.pallas_call(
        body, name="gla_bwd_rev" if reverse else "gla_bwd", grid=(nb, 4, nblk),
        out_shape=[jax.ShapeDtypeStruct((t, A_WIDTH), F32)] * 3 + [jax.ShapeDtypeStruct((nb, 2, A_WIDTH), F32)],
        in_specs=[tok, tok, tok, pl.BlockSpec((2, HEAD_PAIR), lambda b, p, i: (0, p)),
                  pl.BlockSpec((1, 1, group, HEAD_PAIR, HEAD_PAIR), lambda b, p, i: (b, p, tb(i), 0, 0)), tok],
        out_specs=[tok, tok, tok, pl.BlockSpec((1, 2, HEAD_PAIR), lambda b, p, i: (b, 0, p))],
        scratch_shapes=[pltpu.VMEM((HEAD_PAIR, HEAD_PAIR), F32)],
        compiler_params=_params(),
    )(hq, hi, z, lbl, saved, do)


def _head_mean_matrix():
    r = lax.broadcasted_iota(jnp.int32, (A_WIDTH, A_WIDTH), 0) // 64
    c = lax.broadcasted_iota(jnp.int32, (A_WIDTH, A_WIDTH), 1) // 64
    return jnp.where(r == c, 1.0 / 64.0, 0.0).astype(BF16)


def _gla_out(o_f, o_b, hg, g, mean_mat):
    o = o_f + o_b
    ms = _group_mean(o * o, mean_mat)
    return o * lax.rsqrt(ms + EPS) * g * (hg * _sigmoid(hg))


def _gla_combine(o_f, o_b, hg, g, tm):
    t = o_f.shape[0]

    def body(of_ref, ob_ref, hg_ref, g_ref, y_ref):
        y_ref[...] = _gla_out(of_ref[...], ob_ref[...], hg_ref[...], g_ref[...], _head_mean_matrix())

    tok = pl.BlockSpec((tm, A_WIDTH), lambda i: (i, 0))
    return pl.pallas_call(
        body, name="gla_combine_fwd", grid=(t // tm,), out_shape=jax.ShapeDtypeStruct((t, A_WIDTH), F32),
        in_specs=[tok, tok, tok, _const_spec((1, A_WIDTH))], out_specs=tok, compiler_params=_params(),
    )(o_f, o_b, hg, g)


def _gla_combine_bwd(o_f, o_b, hg, g, dy, tm):
    t = o_f.shape[0]

    def body(of_ref, ob_ref, hg_ref, g_ref, dy_ref, do_ref, dhg_ref, dg_ref):
        mean_mat = _head_mean_matrix()
        fn = lambda o, hgv, gv: _gla_out(o, jnp.zeros_like(o), hgv, gv, mean_mat)
        _, vjp = jax.vjp(fn, of_ref[...] + ob_ref[...], hg_ref[...], g_ref[...])
        d_o, d_hg, d_g = vjp(dy_ref[...])
        do_ref[...] = d_o
        dhg_ref[...] = d_hg

        @pl.when(pl.program_id(0) == 0)
        def _():
            dg_ref[...] = jnp.zeros_like(dg_ref)

        dg_ref[...] += d_g

    tok = pl.BlockSpec((tm, A_WIDTH), lambda i: (i, 0))
    vec = pl.BlockSpec((1, A_WIDTH), lambda i: (0, 0))
    return pl.pallas_call(
        body, name="gla_combine_bwd", grid=(t // tm,),
        out_shape=[jax.ShapeDtypeStruct((t, A_WIDTH), F32)] * 2 + [jax.ShapeDtypeStruct((1, A_WIDTH), F32)],
        in_specs=[tok, tok, tok, _const_spec((1, A_WIDTH)), tok], out_specs=[tok, tok, vec], compiler_params=_params(),
    )(o_f, o_b, hg, g, dy)


def _post_fwd(x, ya, oattn, tgt, g_mla, w_out, g2, w_gate, w_up, w_down, g_fin, tm):
    t = x.shape[0]

    def body(x_ref, ya_ref, oa_ref, tgt_ref, gm_ref, wo_ref, g2_ref, wg_ref, wu_ref, wd_ref, gf_ref, x1_ref, x2_ref, loss_ref):
        part = jnp.zeros((1, 1), F32)
        for j in range(tm // min(tm, ROW_SUB)):
            r = pl.ds(j * min(tm, ROW_SUB), min(tm, ROW_SUB))
            yb = _rms(oa_ref[r, :], gm_ref[...])
            x1 = x_ref[r, :] + _dot(ya_ref[r, :].astype(BF16), wo_ref[0:A_WIDTH, :]) + _dot(yb.astype(BF16), wo_ref[A_WIDTH:, :])
            x1_ref[r, :] = x1
            h2 = _rms(x1, g2_ref[...]).astype(BF16)
            gate = _dot(h2, wg_ref[...])
            act = (gate * _sigmoid(gate) * _dot(h2, wu_ref[...])).astype(BF16)
            x2 = x1 + _dot(act, wd_ref[...])
            x2_ref[r, :] = x2
            err = _rms(x2, gf_ref[...]) - tgt_ref[r, :]
            part = part + 0.5 * jnp.sum(jnp.mean(err * err, axis=-1, keepdims=True), axis=0, keepdims=True)

        @pl.when(pl.program_id(0) == 0)
        def _():
            loss_ref[...] = jnp.zeros_like(loss_ref)

        loss_ref[...] += jnp.broadcast_to(part, loss_ref.shape)

    tok = lambda wd: pl.BlockSpec((tm, wd), lambda i: (i, 0))
    return pl.pallas_call(
        body, name="post_fwd", grid=(t // tm,),
        out_shape=[jax.ShapeDtypeStruct((t, D_MODEL), F32)] * 2 + [jax.ShapeDtypeStruct((1, 128), F32)],
        in_specs=[tok(D_MODEL), tok(A_WIDTH), tok(512), tok(D_MODEL), _const_spec((1, 512)), _const_spec((D_MODEL, D_MODEL)),
                  _const_spec((1, D_MODEL)), _const_spec((D_MODEL, D_FF)), _const_spec((D_MODEL, D_FF)),
                  _const_spec((D_FF, D_MODEL)), _const_spec((1, D_MODEL))],
        out_specs=[tok(D_MODEL), tok(D_MODEL), pl.BlockSpec((1, 128), lambda i: (0, 0))],
        compiler_params=_params(),
    )(x, ya, oattn, tgt, g_mla, w_out, g2, w_gate, w_up, w_down, g_fin)


def _post_bwd(x1, x2, ya, oattn, tgt, g_mla, w_out, g2, w_gate, w_up, w_down, g_fin, tm):
    t = x1.shape[0]

    def body(x1_ref, x2_ref, ya_ref, oa_ref, tgt_ref, gm_ref, wo_ref, g2_ref, wg_ref, wu_ref, wd_ref, gf_ref,
             dx1_ref, dya_ref, doa_ref, ycat_ref, dx1b_ref, h2_ref, dgate_ref, dup_ref, act_ref, dx2b_ref,
             dgm_ref, dg2_ref, dgf_ref):
        x1, x2 = x1_ref[...], x2_ref[...]
        dy = (_rms(x2, gf_ref[...]) - tgt_ref[...]) * (1.0 / D_MODEL)
        dx2, dgf = _rms_bwd(x2, gf_ref[...], dy)
        dx2b = dx2.astype(BF16)
        dx2b_ref[...] = dx2b
        h2 = _rms(x1, g2_ref[...]).astype(BF16)
        h2_ref[...] = h2
        gate, up = _dot(h2, wg_ref[...]), _dot(h2, wu_ref[...])
        sg = _sigmoid(gate)
        sl = gate * sg
        act_ref[...] = (sl * up).astype(BF16)
        dact = _dot_nt(dx2b, wd_ref[...])
        dup = (dact * sl).astype(BF16)
        dgate = (dact * up * (sg * (1.0 + gate * (1.0 - sg)))).astype(BF16)
        dup_ref[...] = dup
        dgate_ref[...] = dgate
        dh2 = _dot_nt(dgate, wg_ref[...]) + _dot_nt(dup, wu_ref[...])
        dx1n, dg2 = _rms_bwd(x1, g2_ref[...], dh2)
        dx1 = dx2 + dx1n
        dx1_ref[...] = dx1
        dx1b = dx1.astype(BF16)
        dx1b_ref[...] = dx1b
        oa = oa_ref[...]
        ycat_ref[:, 0:A_WIDTH] = ya_ref[...].astype(BF16)
        ycat_ref[:, A_WIDTH:] = _rms(oa, gm_ref[...]).astype(BF16)
        dya_ref[...] = _dot_nt(dx1b, wo_ref[0:A_WIDTH, :])
        doa, dgm = _rms_bwd(oa, gm_ref[...], _dot_nt(dx1b, wo_ref[A_WIDTH:, :]))
        doa_ref[...] = doa

        @pl.when(pl.program_id(0) == 0)
        def _():
            dgm_ref[...] = jnp.zeros_like(dgm_ref)
            dg2_ref[...] = jnp.zeros_like(dg2_ref)
            dgf_ref[...] = jnp.zeros_like(dgf_ref)

        dgm_ref[...] += dgm
        dg2_ref[...] += dg2
        dgf_ref[...] += dgf

    tok = lambda wd: pl.BlockSpec((tm, wd), lambda i: (i, 0))
    vec = lambda wd: pl.BlockSpec((1, wd), lambda i: (0, 0))
    sds = lambda wd, dt: jax.ShapeDtypeStruct((t, wd), dt)
    return pl.pallas_call(
        body, name="post_bwd", grid=(t // tm,),
        out_shape=[sds(D_MODEL, F32), sds(512, F32), sds(512, F32), sds(D_MODEL, BF16), sds(D_MODEL, BF16), sds(D_MODEL, BF16),
                   sds(D_FF, BF16), sds(D_FF, BF16), sds(D_FF, BF16), sds(D_MODEL, BF16),
                   jax.ShapeDtypeStruct((1, 512), F32), jax.ShapeDtypeStruct((1, D_MODEL), F32), jax.ShapeDtypeStruct((1, D_MODEL), F32)],
        in_specs=[tok(D_MODEL), tok(D_MODEL), tok(512), tok(512), tok(D_MODEL), _const_spec((1, 512)),
                  _const_spec((D_MODEL, D_MODEL)), _const_spec((1, D_MODEL)), _const_spec((D_MODEL, D_FF)),
                  _const_spec((D_MODEL, D_FF)), _const_spec((D_FF, D_MODEL)), _const_spec((1, D_MODEL))],
        out_specs=[tok(D_MODEL), tok(512), tok(512), tok(D_MODEL), tok(D_MODEL), tok(D_MODEL), tok(D_FF), tok(D_FF), tok(D_FF),
                   tok(D_MODEL), vec(512), vec(D_MODEL), vec(D_MODEL)],
        compiler_params=_params(),
    )(x1, x2, ya, oattn, tgt, g_mla, w_out, g2, w_gate, w_up, w_down, g_fin)


def _matmul_tn(a, b, tn, tt, tag):
    t, k = a.shape
    n = b.shape[1]
    last = t // tt - 1

    def body(a_ref, b_ref, o_ref, acc_ref):
        part = _dot_tn(a_ref[...], b_ref[...])

        @pl.when(pl.program_id(1) == 0)
        def _():
            acc_ref[...] = part

        @pl.when(pl.program_id(1) > 0)
        def _():
            acc_ref[...] += part

        @pl.when(pl.program_id(1) == last)
        def _():
            o_ref[...] = acc_ref[...].astype(o_ref.dtype)

    return pl.pallas_call(
        body, name="wgrad_" + tag, grid=(n // tn, t // tt), out_shape=jax.ShapeDtypeStruct((k, n), BF16),
        in_specs=[pl.BlockSpec((tt, k), lambda j, i: (i, 0)), pl.BlockSpec((tt, tn), lambda j, i: (i, j))],
        out_specs=pl.BlockSpec((k, tn), lambda j, i: (0, j)), scratch_shapes=[pltpu.VMEM((k, tn), F32)],
        compiler_params=_params(),
    )(a, b)


def _mla_qkv_bwd(cq, ckv, g_qa, g_kva, w_q, w_kv, tables, dq, dk, dv, seq, tm):
    t = cq.shape[0]
    nblk = seq // tm

    def body(cq_ref, ckv_ref, gq_ref, gk_ref, wq_ref, wkv_ref, c_ref, sa_ref, sb_ref, dq_ref, dk_ref, dv_ref,
             dcq_ref, dckv_ref, dkr_ref, cqn_ref, dqf_ref, ckn_ref, dkv_ref, dgq_ref, dgk_ref):
        cos_t, sin_a, sin_b = c_ref[...], sa_ref[...], sb_ref[...]
        cqn_ref[...] = _rms(cq_ref[...], gq_ref[...]).astype(BF16)
        ckn_ref[...] = _rms(ckv_ref[...], gk_ref[...]).astype(BF16)
        dkr = jnp.zeros((tm, 128), F32)
        for h in range(B_HEADS):
            lo = h * QK_PAD
            dqf_ref[:, lo:lo + 128] = (dq_ref[:, lo:lo + 128] * ATTN_SCALE).astype(BF16)
            dqf_ref[:, lo + 128:lo + 256] = _rope_t(dq_ref[:, lo + 128:lo + 256] * ATTN_SCALE, cos_t, sin_a, sin_b).astype(BF16)
            dkv_ref[:, lo:lo + 128] = dk_ref[:, lo:lo + 128].astype(BF16)
            dkv_ref[:, lo + 128:lo + 256] = dv_ref[:, h * B_V:(h + 1) * B_V].astype(BF16)
            dkr = dkr + dk_ref[:, lo + 128:lo + 256]
        dkr_ref[...] = _rope_t(dkr, cos_t, sin_a, sin_b)
        dcq, dgq = _rms_bwd(cq_ref[...], gq_ref[...], _dot_nt(dqf_ref[...], wq_ref[...]))
        dckv, dgk = _rms_bwd(ckv_ref[...], gk_ref[...], _dot_nt(dkv_ref[...], wkv_ref[...]))
        dcq_ref[...] = dcq
        dckv_ref[...] = dckv

        @pl.when(pl.program_id(0) == 0)
        def _():
            dgq_ref[...] = jnp.zeros_like(dgq_ref)
            dgk_ref[...] = jnp.zeros_like(dgk_ref)

        dgq_ref[...] += dgq
        dgk_ref[...] += dgk

    tok = lambda wd: pl.BlockSpec((tm, wd), lambda i: (i, 0))
    vec = lambda wd: pl.BlockSpec((1, wd), lambda i: (0, 0))
    tab = pl.BlockSpec((tm, 128), lambda i: (i % nblk, 0))
    sds = lambda wd, dt: jax.ShapeDtypeStruct((t, wd), dt)
    return pl.pallas_call(
        body, name="mla_qkv_bwd", grid=(t // tm,),
        out_shape=[sds(Q_LORA, F32), sds(KV_LORA, F32), sds(128, F32), sds(Q_LORA, BF16), sds(1024, BF16), sds(KV_LORA, BF16),
                   sds(1024, BF16), jax.ShapeDtypeStruct((1, Q_LORA), F32), jax.ShapeDtypeStruct((1, KV_LORA), F32)],
        in_specs=[tok(Q_LORA), tok(KV_LORA), _const_spec((1, Q_LORA)), _const_spec((1, KV_LORA)),
                  _const_spec((Q_LORA, 1024)), _const_spec((KV_LORA, 1024)), tab, tab, tab,
                  tok(1024), tok(1024), tok(512)],
        out_specs=[tok(Q_LORA), tok(KV_LORA), tok(128), tok(Q_LORA), tok(1024), tok(KV_LORA), tok(1024),
                   vec(Q_LORA), vec(KV_LORA)],
        compiler_params=_params(),
    )(cq, ckv, g_qa, g_kva, w_q, w_kv, *tables, dq, dk, dv)


def _inproj_bwd(x, g1, w_in, dx1, pieces, tm):
    t = x.shape[0]
    counts = [len(p) for p in pieces]
    flat = [a for p in pieces for a in p]
    widths = [wd for wd, p in zip(IN_WIDTHS, pieces) for _ in p]

    def body(x_ref, g_ref, w_ref, dx1_ref, *refs):
        ins = refs[:len(flat)]
        dx_ref, h_ref, dp_ref, dg_ref = refs[len(flat):]
        xv = x_ref[...]
        h_ref[...] = _rms(xv, g_ref[...]).astype(BF16)
        off, j = 0, 0
        for wd, cnt in zip(IN_WIDTHS, counts):
            acc = ins[j][...]
            for jj in range(1, cnt):
                acc = acc + ins[j + jj][...]
            dp_ref[:, off:off + wd] = acc.astype(BF16)
            off += wd
            j += cnt
        dxn, dg = _rms_bwd(xv, g_ref[...], _dot_nt(dp_ref[...], w_ref[...]))
        dx_ref[...] = dx1_ref[...] + dxn

        @pl.when(pl.program_id(0) == 0)
        def _():
            dg_ref[...] = jnp.zeros_like(dg_ref)

        dg_ref[...] += dg

    tok = lambda wd: pl.BlockSpec((tm, wd), lambda i: (i, 0))
    return pl.pallas_call(
        body, name="inproj_bwd", grid=(t // tm,),
        out_shape=[jax.ShapeDtypeStruct((t, D_MODEL), F32), jax.ShapeDtypeStruct((t, D_MODEL), BF16),
                   jax.ShapeDtypeStruct((t, D_IN_PAD), BF16), jax.ShapeDtypeStruct((1, D_MODEL), F32)],
        in_specs=[tok(D_MODEL), _const_spec((1, D_MODEL)), _const_spec((D_MODEL, D_IN_PAD)), tok(D_MODEL)] + [tok(wd) for wd in widths],
        out_specs=[tok(D_MODEL), tok(D_MODEL), tok(D_IN_PAD), pl.BlockSpec((1, D_MODEL), lambda i: (0, 0))],
        compiler_params=_params(),
    )(x, g1, w_in, dx1, *flat)


def _cols_from_slots(g):
    n, r, cs = g.shape
    return g.transpose(1, 0, 2).reshape(r, n * cs)


def _cols_to_slots(full):
    r, c = full.shape
    return full.reshape(r, N_DEV, c // N_DEV).transpose(1, 0, 2)


def _arrange_w_in(w_in):
    return jnp.concatenate([w_in, jnp.zeros((D_MODEL, D_IN_PAD - D_IN), w_in.dtype)], axis=1)


def _arrange_w_q(w_q_b):
    q3 = w_q_b.reshape(Q_LORA, B_HEADS, B_NOPE + B_ROPE)
    pad = jnp.zeros((Q_LORA, B_HEADS, QK_PAD - B_NOPE - B_ROPE), w_q_b.dtype)
    return jnp.concatenate([q3, pad], axis=2).reshape(Q_LORA, B_HEADS * QK_PAD)


def _unarrange_w_q(d_q):
    return d_q.reshape(Q_LORA, B_HEADS, QK_PAD)[:, :, :B_NOPE + B_ROPE].reshape(Q_LORA, B_HEADS * (B_NOPE + B_ROPE))


def _step_core(x, loss_target, small_w, lb_full, early_full, late, seq, group, tiles, distributed):
    g1, g_hgrn, g_qa, g_kva, g_mla, g2, g_fin = small_w
    w_in, w_q, w_kv = _arrange_w_in(early_full[0]), _arrange_w_q(early_full[1]), early_full[2]
    nb = x.shape[0]
    t = nb * seq
    tm, tm_fwd, tq_f, tq_b, tt = tiles
    xt = x.reshape(t, D_MODEL)
    tgt = loss_target.reshape(t, D_MODEL)
    tables = _rope_tables(seq)

    hq, hi, zf, zb, hg, cq, ckv, kr = _inproj(xt, g1, w_in, tm_fwd)
    qcat, kcat, vv = _mla_qkv(cq, ckv, kr, g_qa, g_kva, w_q, w_kv, tables, seq, tm)
    if distributed:
        oattn, lse, *late_slots = _attn_fwd(qcat, kcat, vv, nb, seq, tq_f, gather=tuple(late))
    else:
        oattn, lse = _attn_fwd(qcat, kcat, vv, nb, seq, tq_f)
        late_slots = late
    w_out = late_slots[0].reshape(D_MODEL, D_MODEL)
    w_gate, w_up = _cols_from_slots(late_slots[1]), _cols_from_slots(late_slots[2])
    w_down = late_slots[3].reshape(D_FF, D_MODEL)
    lbl_f, lbl_b = lb_full[0], lb_full[1]
    o_f, save_f = _gla_fwd(hq, hi, zf, lbl_f, nb, seq, group, False)
    o_b, save_b = _gla_fwd(hq, hi, zb, lbl_b, nb, seq, group, True)
    ya = _gla_combine(o_f, o_b, hg, g_hgrn, tm)
    x1, x2, loss_row = _post_fwd(xt, ya, oattn, tgt, g_mla, w_out, g2, w_gate, w_up, w_down, g_fin, tm_fwd)

    (dx1, d_ya, d_oattn, ycat_b, dx1_b, h2_b, dgate_b, dup_b, act_b, dx2_b, d_g_mla, d_g2, d_g_fin) = _post_bwd(
        x1, x2, ya, oattn, tgt, g_mla, w_out, g2, w_gate, w_up, w_down, g_fin, tm)
    d_w_gate = _matmul_tn(h2_b, dgate_b, D_FF // 2, tt, "gate")
    d_w_up = _matmul_tn(h2_b, dup_b, D_FF // 2, tt, "up")
    d_w_down = _matmul_tn(act_b, dx2_b, 512, tt, "down")
    d_w_out = _matmul_tn(ycat_b, dx1_b, D_MODEL, tt, "out")
    late_g = [d_w_out.reshape(N_DEV, D_MODEL // N_DEV, D_MODEL), _cols_to_slots(d_w_gate), _cols_to_slots(d_w_up),
              d_w_down.reshape(N_DEV, D_FF // N_DEV, D_MODEL)]
    if distributed:
        dq, dk, dv, *late_g = _attn_bwd(qcat, kcat, vv, oattn, lse, d_oattn, nb, seq, tq_b, exchange=tuple(late_g))
    else:
        dq, dk, dv = _attn_bwd(qcat, kcat, vv, oattn, lse, d_oattn, nb, seq, tq_b)
    (d_cq, d_ckv, d_kr, cqn_b, dqf_b, ckn_b, dkv_b, d_g_qa, d_g_kva) = _mla_qkv_bwd(
        cq, ckv, g_qa, g_kva, w_q, w_kv, tables, dq, dk, dv, seq, tm)
    d_w_q = _matmul_tn(cqn_b, dqf_b, B_HEADS * QK_PAD, tt, "q_b")
    d_w_kv = _matmul_tn(ckn_b, dkv_b, B_HEADS * (B_NOPE + B_V), tt, "kv_b")
    d_o, d_hg, d_g_hgrn = _gla_combine_bwd(o_f, o_b, hg, g_hgrn, d_ya, tm)
    dq_f, dv_f, dz_f, dl_f = _gla_bwd(hq, hi, zf, lbl_f, save_f, d_o, nb, seq, group, False)
    dq_b, dv_b, dz_b, dl_b = _gla_bwd(hq, hi, zb, lbl_b, save_b, d_o, nb, seq, group, True)
    grad_x, h1_b, dproj_b, d_g1 = _inproj_bwd(
        xt, g1, w_in, dx1, [[dq_f, dq_b], [dv_f, dv_b], [dz_f], [dz_b], [d_hg], [d_cq], [d_ckv], [d_kr]], tm)
    d_w_in_arr = _matmul_tn(h1_b, dproj_b, D_IN_PAD // 2, tt, "in")

    early_g = [_cols_to_slots(d_w_in_arr[:, :D_IN]), _cols_to_slots(_unarrange_w_q(d_w_q)), _cols_to_slots(d_w_kv)]
    d_lb = jnp.stack([jnp.sum(dl_f, axis=0), jnp.sum(dl_b, axis=0)], axis=0)
    small_grads = [d_g1, d_g_hgrn, d_g_qa, d_g_kva, d_g_mla, d_g2, d_g_fin]
    return loss_row, grad_x.reshape(nb, seq, D_MODEL), early_g, late_g, small_grads, d_lb


def kernel(x, norm1_g, w_in, lb_logits, hgrn_norm_g, q_a_norm_g, w_q_b, kv_a_norm_g, w_kv_b, mla_norm_g, w_out, norm2_g, w_gate, w_up, w_down, final_norm_g, loss_target, m_norm1_g, m_w_in, m_lb_logits, m_hgrn_norm_g, m_q_a_norm_g, m_w_q_b, m_kv_a_norm_g, m_w_kv_b, m_mla_norm_g, m_w_out, m_norm2_g, m_w_gate, m_w_up, m_w_down, m_final_norm_g, v_norm1_g, v_w_in, v_lb_logits, v_hgrn_norm_g, v_q_a_norm_g, v_w_q_b, v_kv_a_norm_g, v_w_kv_b, v_mla_norm_g, v_w_out, v_norm2_g, v_w_gate, v_w_up, v_w_down, v_final_norm_g):
    big_w = [w_in, w_q_b, w_kv_b, w_out, w_gate, w_up, w_down]
    big_m = [m_w_in, m_w_q_b, m_w_kv_b, m_w_out, m_w_gate, m_w_up, m_w_down]
    big_v = [v_w_in, v_w_q_b, v_w_kv_b, v_w_out, v_w_gate, v_w_up, v_w_down]
    small_w = [norm1_g, hgrn_norm_g, q_a_norm_g, kv_a_norm_g, mla_norm_g, norm2_g, final_norm_g]
    small_m = [m_norm1_g, m_hgrn_norm_g, m_q_a_norm_g, m_kv_a_norm_g, m_mla_norm_g, m_norm2_g, m_final_norm_g]
    small_v = [v_norm1_g, v_hgrn_norm_g, v_q_a_norm_g, v_kv_a_norm_g, v_mla_norm_g, v_norm2_g, v_final_norm_g]
    seq = x.shape[1]
    my_id = 4 * lax.axis_index("x") + 2 * lax.axis_index("y") + lax.axis_index("c")

    shard = lambda w: w[0].astype(BF16)
    g_in, g_q, g_kv, g_lb = _all_gather_call([shard(w_in), shard(w_q_b), shard(w_kv_b), lb_logits.reshape(4, 64)])
    early_full = (_cols_from_slots(g_in), _cols_from_slots(g_q), _cols_from_slots(g_kv))
    lb_full = g_lb.reshape(N_DEV, 2, 2, 64).transpose(1, 2, 0, 3).reshape(2, 2, 512)

    as_row = lambda a: a.reshape(1, -1)
    loss_row, grad_x, early_g, late_recv, small_g, d_lb = _step_core(
        x, loss_target, [as_row(s) for s in small_w], lb_full, early_full,
        [shard(w_out), shard(w_gate), shard(w_up), shard(w_down)], seq, min(8, seq // CHUNK),
        (256, 512, min(1024, seq), min(512, seq), min(2048, 2 * seq)), True)

    n_small = len(small_g)
    recv = _exchange_call(early_g + small_g + [d_lb.reshape(4, 512), loss_row], [True] * 3 + [False] * (n_small + 2))
    sums = _sum_slots_call(recv[3:])
    g_small = [g.reshape(s.shape) for g, s in zip(sums[:n_small], small_w)]
    g_lb_own = lax.dynamic_index_in_dim(sums[n_small].reshape(2, 2, N_DEV, 64), my_id, axis=2, keepdims=False)
    loss = sums[n_small + 1][0, 0]

    grads, deltas, new_ms, new_vs = {}, {}, {}, {}
    big_recv = dict(zip(["w_in", "w_q_b", "w_kv_b", "w_out", "w_gate", "w_up", "w_down"], list(recv[:3]) + list(late_recv)))
    for (name, _, _, _), w, m, v in zip(BIG, big_w, big_m, big_v):
        g, d, nm, nv = _adamw_recv(w[0], big_recv[name], m[0], v[0], name)
        grads[name], deltas[name], new_ms[name], new_vs[name] = g[None], d[None], nm[None], nv[None]
    lb_rows = lambda a: a.reshape(4, 64)
    d_s, nm_s, nv_s = _adamw_small(
        [as_row(a) for a in small_w] + [lb_rows(lb_logits)], [as_row(a) for a in g_small] + [lb_rows(g_lb_own)],
        [as_row(a) for a in small_m] + [lb_rows(m_lb_logits)], [as_row(a) for a in small_v] + [lb_rows(v_lb_logits)])
    for i, (s, (name, _)) in enumerate(zip(small_w + [lb_logits], SMALL + (("lb_logits", 0),))):
        grads[name] = (g_small + [g_lb_own])[i]
        deltas[name], new_ms[name], new_vs[name] = d_s[i].reshape(s.shape), nm_s[i].reshape(s.shape), nv_s[i].reshape(s.shape)

    order = ["norm1_g", "w_in", "lb_logits", "hgrn_norm_g", "q_a_norm_g", "w_q_b", "kv_a_norm_g", "w_kv_b", "mla_norm_g",
             "w_out", "norm2_g", "w_gate", "w_up", "w_down", "final_norm_g"]
    return (loss, grad_x, *[grads[n] for n in order], *[deltas[n] for n in order],
            *[new_ms[n] for n in order], *[new_vs[n] for n in order])
```

```python
import functools
import math

import jax
import jax.numpy as jnp
from jax import lax
from jax.experimental import pallas as pl
from jax.experimental.pallas import tpu as pltpu

F32 = jnp.float32
BF16 = jnp.bfloat16

N_DEV = 8
D_MODEL = 1024
D_FF = 2816
A_WIDTH = 512
HEAD_PAIR = 128
CHUNK = 64
B_HEADS = 4
B_NOPE = 128
B_ROPE = 64
B_V = 128
QK_PAD = 256
Q_LORA = 384
KV_LORA = 256
D_IN = 3264
D_IN_PAD = 3328
IN_WIDTHS = (512, 512, 512, 512, 512, Q_LORA, KV_LORA, 128)
ROPE_THETA = 10000.0
EPS = 1e-6
ATTN_SCALE = (B_NOPE + B_ROPE) ** -0.5
ATTN_SUB = 256
ATTN_SUB_BWD = 256
ROW_SUB = 256
ADAM_LR, ADAM_B1, ADAM_B2, ADAM_EPS, ADAM_WD, ADAM_STEP = 0.001, 0.9, 0.999, 1e-08, 0.01, 10
VMEM_LIMIT = 56 * 1024 * 1024
MESH = pl.DeviceIdType.MESH

BIG = (("w_in", 1024, D_IN, 1), ("w_q_b", Q_LORA, 768, 1), ("w_kv_b", KV_LORA, 1024, 1), ("w_out", 1024, 1024, 0),
       ("w_gate", 1024, D_FF, 1), ("w_up", 1024, D_FF, 1), ("w_down", D_FF, 1024, 0))
SMALL = (("norm1_g", 1024), ("hgrn_norm_g", 512), ("q_a_norm_g", 384), ("kv_a_norm_g", 256), ("mla_norm_g", 512),
         ("norm2_g", 1024), ("final_norm_g", 1024))


def _params(**kw):
    return pltpu.CompilerParams(vmem_limit_bytes=VMEM_LIMIT, **kw)


def _const_spec(shape):
    return pl.BlockSpec(shape, lambda *_: (0,) * len(shape), pipeline_mode=pl.Buffered(1))


def _dot(a, b):
    return jnp.dot(a, b, preferred_element_type=F32)


def _dot_nt(a, b):
    return lax.dot_general(a, b, (((1,), (1,)), ((), ())), preferred_element_type=F32)


def _dot_tn(a, b):
    return lax.dot_general(a, b, (((0,), (0,)), ((), ())), preferred_element_type=F32)


@jax.custom_vjp
def _mm(a, b):
    return _dot(a.astype(BF16), b.astype(BF16))


def _mm_fwd(a, b):
    return _mm(a, b), (a, b)


def _mm_bwd(res, g):
    a, b = res
    gb = g.astype(BF16)
    return _dot_nt(gb, b.astype(BF16)), _dot_tn(a.astype(BF16), gb)


_mm.defvjp(_mm_fwd, _mm_bwd)


@jax.custom_vjp
def _mm_nt(a, b):
    return _dot_nt(a.astype(BF16), b.astype(BF16))


def _mm_nt_fwd(a, b):
    return _mm_nt(a, b), (a, b)


def _mm_nt_bwd(res, g):
    a, b = res
    gb = g.astype(BF16)
    return _dot(gb, b.astype(BF16)), _dot_tn(gb, a.astype(BF16))


_mm_nt.defvjp(_mm_nt_fwd, _mm_nt_bwd)


@jax.custom_vjp
def _mm_tn(a, b):
    return _dot_tn(a.astype(BF16), b.astype(BF16))


def _mm_tn_fwd(a, b):
    return _mm_tn(a, b), (a, b)


def _mm_tn_bwd(res, g):
    a, b = res
    gb = g.astype(BF16)
    return _dot_nt(b.astype(BF16), gb), _dot(a.astype(BF16), gb)


_mm_tn.defvjp(_mm_tn_fwd, _mm_tn_bwd)


def _split3(a):
    hi = a.astype(BF16)
    r = a - hi.astype(F32)
    mid = r.astype(BF16)
    lo = (r - mid.astype(F32)).astype(BF16)
    return hi, mid, lo


def _dot_exact_rhs(a, m):
    hi, mid, lo = _split3(a)
    return _dot(hi, m) + _dot(mid, m) + _dot(lo, m)


@jax.custom_vjp
def _group_mean(a, m):
    return _dot_exact_rhs(a, m)


def _group_mean_fwd(a, m):
    return _group_mean(a, m), m


def _group_mean_bwd(m, g):
    return _dot_exact_rhs(g, m), jnp.zeros_like(m)


_group_mean.defvjp(_group_mean_fwd, _group_mean_bwd)


def _roll_rows(a, shift):
    return pltpu.roll(a, shift, 0)


def _cumsum_rows_raw(a, reverse):
    n = a.shape[0]
    row = lax.broadcasted_iota(jnp.int32, a.shape, 0)
    s = 1
    while s < n:
        if reverse:
            a = a + jnp.where(row < n - s, _roll_rows(a, n - s), 0.0)
        else:
            a = a + jnp.where(row >= s, _roll_rows(a, s), 0.0)
        s *= 2
    return a


@functools.partial(jax.custom_vjp, nondiff_argnums=(1,))
def _cumsum_rows(a, reverse):
    return _cumsum_rows_raw(a, reverse)


def _cumsum_rows_fwd(a, reverse):
    return _cumsum_rows_raw(a, reverse), None


def _cumsum_rows_bwd(reverse, _, g):
    return (_cumsum_rows_raw(g, not reverse),)


_cumsum_rows.defvjp(_cumsum_rows_fwd, _cumsum_rows_bwd)


def _rms(x, g):
    r = lax.rsqrt(jnp.mean(x * x, axis=-1, keepdims=True) + EPS)
    return x * r * g


def _rms_bwd(x, g, dy):
    r = lax.rsqrt(jnp.mean(x * x, axis=-1, keepdims=True) + EPS)
    xh = x * r
    dg = jnp.sum(dy * xh, axis=0, keepdims=True)
    dxh = dy * g
    dx = r * (dxh - xh * jnp.mean(dxh * xh, axis=-1, keepdims=True))
    return dx, dg


def _sigmoid(a):
    return jax.nn.sigmoid(a)


def _mesh_place():
    x, y, c = lax.axis_index("x"), lax.axis_index("y"), lax.axis_index("c")
    return x, y, c


def _dev_index(p):
    return 4 * p[0] + 2 * p[1] + p[2]


def _comm_sems(n):
    return [pltpu.SemaphoreType.DMA((n, 7)), pltpu.SemaphoreType.DMA((n, 7)), pltpu.SemaphoreType.DMA((n,))]


def _gather_protocol(ins, outs, send_sems, recv_sems, local_sems):
    n = len(ins)
    x, y, c = _mesh_place()
    me, sibling = (x, y, c), (x, y, 1 - c)
    chips = [(1 - x, y), (x, 1 - y), (1 - x, 1 - y)]

    def copy(a, k, block, to, src=None):
        slot = outs[a].at[_dev_index(block)]
        return pltpu.make_async_remote_copy(
            src_ref=slot if src is None else src, dst_ref=slot,
            send_sem=send_sems.at[a, k], recv_sem=recv_sems.at[a, k], device_id=to, device_id_type=MESH)

    def mine(a):
        return pltpu.make_async_copy(ins[a], outs[a].at[_dev_index(me)], local_sems.at[a])

    def first(a):
        return [copy(a, 0, me, sibling, src=ins[a])] + [copy(a, 1 + j, me, (*chip, c), src=ins[a]) for j, chip in enumerate(chips)]

    def start():
        for a in range(n):
            mine(a).start()
            for cp in first(a):
                cp.start()

    def forward():
        for a in range(n):
            for j, chip in enumerate(chips):
                copy(a, 1 + j, (*chip, c), me).wait_recv()
                copy(a, 4 + j, (*chip, c), sibling).start()

    def finish():
        for a in range(n):
            copy(a, 0, sibling, me).wait_recv()
            for j, chip in enumerate(chips):
                copy(a, 4 + j, (*chip, 1 - c), me).wait_recv()
        for a in range(n):
            mine(a).wait()
            for cp in first(a):
                cp.wait_send()
            for j, chip in enumerate(chips):
                copy(a, 4 + j, (*chip, c), sibling).wait_send()

    return start, forward, finish


def _exchange_protocol(ins, outs, scatter, send_sems, recv_sems, local_sems):
    n = len(ins)
    x, y, c = _mesh_place()
    me = (x, y, c)
    my_id = _dev_index(me)
    rels = [(dx, dy, dc) for dx in (0, 1) for dy in (0, 1) for dc in (0, 1)][1:]

    def peer_of(rel):
        return tuple(1 - v if d else v for v, d in zip(me, rel))

    def src(a, dev):
        return ins[a].at[dev] if scatter[a] else ins[a]

    def send(a, k):
        peer = peer_of(rels[k])
        return pltpu.make_async_remote_copy(
            src_ref=src(a, _dev_index(peer)), dst_ref=outs[a].at[my_id],
            send_sem=send_sems.at[a, k], recv_sem=recv_sems.at[a, k], device_id=peer, device_id_type=MESH)

    def arrival(a, k):
        peer = peer_of(rels[k])
        return pltpu.make_async_remote_copy(
            src_ref=src(a, my_id), dst_ref=outs[a].at[_dev_index(peer)],
            send_sem=send_sems.at[a, k], recv_sem=recv_sems.at[a, k], device_id=peer, device_id_type=MESH)

    def own(a):
        return pltpu.make_async_copy(src(a, my_id), outs[a].at[my_id], local_sems.at[a])

    def start():
        for a in range(n):
            own(a).start()
            for k in range(7):
                send(a, k).start()

    def finish():
        for a in range(n):
            for k in range(7):
                arrival(a, k).wait_recv()
        for a in range(n):
            for k in range(7):
                send(a, k).wait_send()
            own(a).wait()

    return start, finish


def _slot_shapes(blocks, scatter=None):
    return [jax.ShapeDtypeStruct(b.shape if (scatter and scatter[a]) else (N_DEV,) + b.shape, b.dtype) for a, b in enumerate(blocks)]


def _all_gather_call(blocks):
    n = len(blocks)

    def body(*refs):
        start, forward, finish = _gather_protocol(refs[:n], refs[n:2 * n], *refs[2 * n:])
        start()
        forward()
        finish()

    any_spec = pl.BlockSpec(memory_space=pl.ANY)
    return pl.pallas_call(
        body, name="weights_all_gather", out_shape=_slot_shapes(blocks),
        in_specs=[any_spec] * n, out_specs=[any_spec] * n, scratch_shapes=_comm_sems(n),
    )(*blocks)


def _exchange_call(blocks, scatter):
    n = len(blocks)

    def body(*refs):
        start, finish = _exchange_protocol(refs[:n], refs[n:2 * n], scatter, *refs[2 * n:])
        start()
        finish()

    any_spec = pl.BlockSpec(memory_space=pl.ANY)
    return pl.pallas_call(
        body, name="grad_exchange", out_shape=_slot_shapes(blocks, scatter),
        in_specs=[any_spec] * n, out_specs=[any_spec] * n, scratch_shapes=_comm_sems(n),
    )(*blocks)


def _sum_slots_call(recvs):
    n = len(recvs)

    def body(*refs):
        for in_ref, out_ref in zip(refs[:n], refs[n:]):
            acc = in_ref[0]
            for j in range(1, N_DEV):
                acc = acc + in_ref[j]
            out_ref[...] = acc

    return pl.pallas_call(
        body, name="small_grad_sum", out_shape=[jax.ShapeDtypeStruct(r.shape[1:], F32) for r in recvs],
        compiler_params=_params(),
    )(*recvs)


def _adam_update(w, g, m, v):
    nm = ADAM_B1 * m + (1.0 - ADAM_B1) * g
    nv = ADAM_B2 * v + (1.0 - ADAM_B2) * (g * g)
    bc1 = 1.0 - ADAM_B1 ** ADAM_STEP
    bc2 = 1.0 - ADAM_B2 ** ADAM_STEP
    return -ADAM_LR * ((nm / bc1) / (jnp.sqrt(nv / bc2) + ADAM_EPS) + ADAM_WD * w), nm, nv


def _adamw_recv(w, recv, m, v, tag):
    r, c = w.shape
    tr = r
    for cand in (512, 256, 128):
        if r > cand and r % cand == 0:
            tr = cand
            break

    def body(w_ref, r_ref, m_ref, v_ref, g_ref, d_ref, nm_ref, nv_ref):
        g = r_ref[0].astype(F32)
        for j in range(1, N_DEV):
            g = g + r_ref[j].astype(F32)
        g_ref[...] = g
        d_ref[...], nm_ref[...], nv_ref[...] = _adam_update(w_ref[...], g, m_ref[...], v_ref[...])

    spec = pl.BlockSpec((tr, c), lambda i: (i, 0))
    return pl.pallas_call(
        body, name="adamw_" + tag, out_shape=[jax.ShapeDtypeStruct(w.shape, F32)] * 4, grid=(r // tr,),
        in_specs=[spec, pl.BlockSpec((N_DEV, tr, c), lambda i: (0, i, 0)), spec, spec], out_specs=[spec] * 4,
        compiler_params=_params(),
    )(w, recv, m, v)


def _adamw_small(ws, gs, ms, vs):
    n = len(ws)

    def body(*refs):
        ins, outs = refs[:4 * n], refs[4 * n:]
        for a in range(n):
            d, nm, nv = _adam_update(ins[a][...], ins[n + a][...], ins[2 * n + a][...], ins[3 * n + a][...])
            outs[a][...], outs[n + a][...], outs[2 * n + a][...] = d, nm, nv

    out = pl.pallas_call(
        body, name="adamw_small", out_shape=[jax.ShapeDtypeStruct(w.shape, F32) for w in ws] * 3, compiler_params=_params(),
    )(*ws, *gs, *ms, *vs)
    return out[:n], out[n:2 * n], out[2 * n:]


def _tile(t, want):
    return want if t % want == 0 else t


def _inproj(x, g1, w_in, tm):
    t = x.shape[0]

    def body(x_ref, g_ref, w_ref, *outs):
        for j in range(tm // min(tm, ROW_SUB)):
            r = pl.ds(j * min(tm, ROW_SUB), min(tm, ROW_SUB))
            h = _rms(x_ref[r, :], g_ref[...]).astype(BF16)
            off = 0
            for o_ref, wd in zip(outs, IN_WIDTHS):
                o_ref[r, :] = _dot(h, w_ref[:, off:off + wd])
                off += wd

    return pl.pallas_call(
        body, name="inproj_fwd", grid=(t // tm,),
        out_shape=[jax.ShapeDtypeStruct((t, wd), F32) for wd in IN_WIDTHS],
        in_specs=[pl.BlockSpec((tm, D_MODEL), lambda i: (i, 0)), _const_spec((1, D_MODEL)), _const_spec((D_MODEL, D_IN_PAD))],
        out_specs=[pl.BlockSpec((tm, wd), lambda i: (i, 0)) for wd in IN_WIDTHS],
        compiler_params=_params(),
    )(x, g1, w_in)


def _rope_tables(seq):
    inv = 1.0 / (ROPE_THETA ** (jnp.arange(0, B_ROPE, 2, dtype=F32) / B_ROPE))
    ang = jnp.arange(seq, dtype=F32)[:, None] * inv[None, :]
    cos, sin = jnp.cos(ang), jnp.sin(ang)
    z32, z64 = jnp.zeros_like(cos), jnp.zeros((seq, 64), F32)
    cos_t = jnp.concatenate([cos, cos, z64], axis=1)
    sin_a = jnp.concatenate([-sin, z32, z64], axis=1)
    sin_b = jnp.concatenate([z32, sin, z64], axis=1)
    return cos_t, sin_a, sin_b


def _rope(t, cos_t, sin_a, sin_b):
    return t * cos_t + pltpu.roll(t, 96, 1) * sin_a + pltpu.roll(t, 32, 1) * sin_b


def _rope_t(d, cos_t, sin_a, sin_b):
    return d * cos_t + pltpu.roll(d * sin_a, 32, 1) + pltpu.roll(d * sin_b, 96, 1)


def _mla_qkv(cq, ckv, kr, g_qa, g_kva, w_q, w_kv, tables, seq, tm):
    t = cq.shape[0]
    nblk = seq // tm

    def body(cq_ref, ckv_ref, kr_ref, gq_ref, gk_ref, wq_ref, wkv_ref, c_ref, sa_ref, sb_ref, q_out, k_out, v_out):
        cos_t, sin_a, sin_b = c_ref[...], sa_ref[...], sb_ref[...]
        cqn = _rms(cq_ref[...], gq_ref[...]).astype(BF16)
        ckn = _rms(ckv_ref[...], gk_ref[...]).astype(BF16)
        kr_rot = _rope(kr_ref[...], cos_t, sin_a, sin_b).astype(BF16)
        for h in range(B_HEADS):
            lo = h * QK_PAD
            q_out[:, lo:lo + 128] = (_dot(cqn, wq_ref[:, lo:lo + 128]) * ATTN_SCALE).astype(BF16)
            qr = _rope(_dot(cqn, wq_ref[:, lo + 128:lo + 256]), cos_t, sin_a, sin_b)
            q_out[:, lo + 128:lo + 256] = (qr * ATTN_SCALE).astype(BF16)
            k_out[:, lo:lo + 128] = _dot(ckn, wkv_ref[:, lo:lo + 128]).astype(BF16)
            k_out[:, lo + 128:lo + 256] = kr_rot
            v_out[:, h * B_V:(h + 1) * B_V] = _dot(ckn, wkv_ref[:, lo + 128:lo + 256]).astype(BF16)

    tok = lambda wd: pl.BlockSpec((tm, wd), lambda i: (i, 0))
    tab = pl.BlockSpec((tm, 128), lambda i: (i % nblk, 0))
    return pl.pallas_call(
        body, name="mla_qkv_fwd", grid=(t // tm,),
        out_shape=[jax.ShapeDtypeStruct((t, B_HEADS * QK_PAD), BF16), jax.ShapeDtypeStruct((t, B_HEADS * QK_PAD), BF16),
                   jax.ShapeDtypeStruct((t, B_HEADS * B_V), BF16)],
        in_specs=[tok(Q_LORA), tok(KV_LORA), tok(128), _const_spec((1, Q_LORA)), _const_spec((1, KV_LORA)),
                  _const_spec((Q_LORA, B_HEADS * QK_PAD)), _const_spec((KV_LORA, 1024)), tab, tab, tab],
        out_specs=[tok(B_HEADS * QK_PAD), tok(B_HEADS * QK_PAD), tok(B_HEADS * B_V)],
        compiler_params=_params(),
    )(cq, ckv, kr, g_qa, g_kva, w_q, w_kv, *tables)


def _step_index(nq):
    return (pl.program_id(0) * B_HEADS + pl.program_id(1)) * nq + pl.program_id(2)


def _attn_fwd(qcat, kcat, v, nb, seq, tq, gather=()):
    t = qcat.shape[0]
    nq = seq // tq
    ng = len(gather)
    steps = nb * B_HEADS * nq

    def body(q_ref, k_ref, v_ref, *rest):
        o_ref, lse_ref = rest[ng:ng + 2]
        if ng:
            start, forward, finish = _gather_protocol(rest[:ng], rest[ng + 2:2 * ng + 2], *rest[2 * ng + 2:])
            pl.when(_step_index(nq) == 0)(start)
            pl.when(_step_index(nq) == (3 * steps) // 4)(forward)
        for j in range(tq // ATTN_SUB):
            r = pl.ds(j * ATTN_SUB, ATTN_SUB)
            s = _dot_nt(q_ref[r, :], k_ref[...])
            m = jnp.max(s, axis=-1, keepdims=True)
            p = jnp.exp(s - m)
            l = jnp.sum(p, axis=-1, keepdims=True)
            o_ref[r, :] = _dot(p.astype(BF16), v_ref[...]) / l
            lse_ref[0, r, :] = m + jnp.log(l)
        if ng:
            pl.when(_step_index(nq) == steps - 1)(finish)

    any_spec = pl.BlockSpec(memory_space=pl.ANY)
    return pl.pallas_call(
        body, name="attn_fwd", grid=(nb, B_HEADS, nq),
        out_shape=[jax.ShapeDtypeStruct((t, B_HEADS * B_V), F32), jax.ShapeDtypeStruct((B_HEADS, t, 1), F32)] + _slot_shapes(gather),
        in_specs=[pl.BlockSpec((tq, QK_PAD), lambda b, h, i: (b * nq + i, h)),
                  pl.BlockSpec((seq, QK_PAD), lambda b, h, i: (b, h)),
                  pl.BlockSpec((seq, B_V), lambda b, h, i: (b, h))] + [any_spec] * ng,
        out_specs=[pl.BlockSpec((tq, B_V), lambda b, h, i: (b * nq + i, h)),
                   pl.BlockSpec((1, tq, 1), lambda b, h, i: (h, b * nq + i, 0))] + [any_spec] * ng,
        scratch_shapes=_comm_sems(ng) if ng else [],
        compiler_params=_params(),
    )(qcat, kcat, v, *gather)


def _attn_bwd(qcat, kcat, v, o, lse, do, nb, seq, tq, exchange=()):
    t = qcat.shape[0]
    nq = seq // tq
    ne = len(exchange)
    steps = nb * B_HEADS * nq

    def body(q_ref, k_ref, v_ref, o_ref, lse_ref, do_ref, *rest):
        dq_ref, dk_ref, dv_ref = rest[ne:ne + 3]
        if ne:
            start, finish = _exchange_protocol(rest[:ne], rest[ne + 3:2 * ne + 3], [True] * ne, *rest[2 * ne + 3:])
            pl.when(_step_index(nq) == 0)(start)

        @pl.when(pl.program_id(2) == 0)
        def _():
            dv_ref[...] = jnp.zeros_like(dv_ref)
            dk_ref[...] = jnp.zeros_like(dk_ref)

        for j in range(tq // ATTN_SUB_BWD):
            r = pl.ds(j * ATTN_SUB_BWD, ATTN_SUB_BWD)
            q, k = q_ref[r, :], k_ref[...]
            do_f = do_ref[r, :]
            delta = jnp.sum(do_f * o_ref[r, :], axis=-1, keepdims=True)
            dob = do_f.astype(BF16)
            p = jnp.exp(_dot_nt(q, k) - lse_ref[0, r, :])
            ds = (p * (_dot_nt(dob, v_ref[...]) - delta)).astype(BF16)
            dq_ref[r, :] = _dot(ds, k)
            dv_ref[...] += _dot_tn(p.astype(BF16), dob)
            dk_ref[...] += _dot_tn(ds, q)
        if ne:
            pl.when(_step_index(nq) == steps - 1)(finish)

    qspec = lambda wd: pl.BlockSpec((tq, wd), lambda b, h, i: (b * nq + i, h))
    kspec = lambda wd: pl.BlockSpec((seq, wd), lambda b, h, i: (b, h))
    any_spec = pl.BlockSpec(memory_space=pl.ANY)
    return pl.pallas_call(
        body, name="attn_bwd", grid=(nb, B_HEADS, nq),
        out_shape=[jax.ShapeDtypeStruct((t, B_HEADS * QK_PAD), F32), jax.ShapeDtypeStruct((t, B_HEADS * QK_PAD), F32),
                   jax.ShapeDtypeStruct((t, B_HEADS * B_V), F32)] + _slot_shapes(exchange, [True] * ne),
        in_specs=[qspec(QK_PAD), kspec(QK_PAD), kspec(B_V), qspec(B_V),
                  pl.BlockSpec((1, tq, 1), lambda b, h, i: (h, b * nq + i, 0)), qspec(B_V)] + [any_spec] * ne,
        out_specs=[qspec(QK_PAD), kspec(QK_PAD), kspec(B_V)] + [any_spec] * ne,
        scratch_shapes=_comm_sems(ne) if ne else [],
        compiler_params=_params(),
    )(qcat, kcat, v, o, lse, do, *exchange)


def _gla_consts(reverse):
    row = lax.broadcasted_iota(jnp.int32, (CHUNK, CHUNK), 0)
    col = lax.broadcasted_iota(jnp.int32, (CHUNK, CHUNK), 1)
    causal = (row <= col) if reverse else (row >= col)
    lane = lax.broadcasted_iota(jnp.int32, (1, HEAD_PAIR), 1)
    m0 = (lane < 64).astype(F32)
    m1 = 1.0 - m0
    r2 = lax.broadcasted_iota(jnp.int32, (HEAD_PAIR, HEAD_PAIR), 0)
    c2 = lax.broadcasted_iota(jnp.int32, (HEAD_PAIR, HEAD_PAIR), 1)
    same_head = ((r2 < 64) == (c2 < 64)).astype(F32)
    return causal, m0, m1, same_head


def _gla_chunk(hq, hi, z, l0, l1, st, consts, reverse):
    causal, m0, m1, same_head = consts
    mx = jnp.maximum(l0, l1)
    e0, e1 = jnp.exp(l0 - mx), jnp.exp(l1 - mx)
    lb = e0 / (e0 + e1)
    q = hq * _sigmoid(hq)
    log_f = jnp.log(lb + (1.0 - lb) * _sigmoid(z))
    k = (1.0 - lb) * _sigmoid(-z)
    cum = _cumsum_rows(log_f, reverse)
    tot = jnp.sum(log_f, axis=0, keepdims=True)
    q_dec = q * jnp.exp(cum)
    k_inv = k * jnp.exp(-cum)
    k_end = k * jnp.exp(tot - cum)
    o = _mm_nt(q_dec, st)
    for mh in (m0, m1):
        s = jnp.where(causal, _mm_nt(q_dec * mh, k_inv), 0.0)
        o = o + _mm(s, hi) * mh
    st_new = st * jnp.exp(tot) + _mm_tn(hi, k_end) * same_head
    return o, st_new


GLA_DIRS = (False, True)


def _gla_fwd(hq, hi, zs, lbls, nb, seq, group):
    t = hq.shape[0]
    rows = group * CHUNK
    nblk = seq // rows
    n_chunks = seq // CHUNK
    nd = len(GLA_DIRS)

    def body(*refs):
        ins, outs, st_refs = refs[:4 * nd], refs[4 * nd:6 * nd], refs[6 * nd:]
        @pl.when(pl.program_id(2) == 0)
        def _():
            for st_ref in st_refs:
                st_ref[...] = jnp.zeros_like(st_ref)

        consts = [_gla_consts(rev) for rev in GLA_DIRS]
        sts = [st_ref[...] for st_ref in st_refs]
        for cc in range(group):
            for d, rev in enumerate(GLA_DIRS):
                hq_ref, hi_ref, z_ref, lbl_ref = ins[4 * d:4 * d + 4]
                o_ref, save_ref = outs[d], outs[nd + d]
                c = group - 1 - cc if rev else cc
                r = pl.ds(c * CHUNK, CHUNK)
                save_ref[0, 0, c] = sts[d]
                o_c, sts[d] = _gla_chunk(hq_ref[r, :], hi_ref[r, :], z_ref[r, :], lbl_ref[0:1, :], lbl_ref[1:2, :], sts[d],
                                         consts[d], rev)
                o_ref[r, :] = o_c
        for st_ref, st in zip(st_refs, sts):
            st_ref[...] = st

    def tb(rev):
        return (lambda i: nblk - 1 - i) if rev else (lambda i: i)

    tok = lambda rev: pl.BlockSpec((rows, HEAD_PAIR), lambda b, p, i: (b * nblk + tb(rev)(i), p))
    lspec = pl.BlockSpec((2, HEAD_PAIR), lambda b, p, i: (0, p))
    sspec = lambda rev: pl.BlockSpec((1, 1, group, HEAD_PAIR, HEAD_PAIR), lambda b, p, i: (b, p, tb(rev)(i), 0, 0))
    args, in_specs = [], []
    for d, rev in enumerate(GLA_DIRS):
        args += [hq, hi, zs[d], lbls[d]]
        in_specs += [tok(rev), tok(rev), tok(rev), lspec]
    return pl.pallas_call(
        body, name="gla_fwd", grid=(nb, 4, nblk),
        out_shape=[jax.ShapeDtypeStruct((t, A_WIDTH), F32)] * nd
        + [jax.ShapeDtypeStruct((nb, 4, n_chunks, HEAD_PAIR, HEAD_PAIR), F32)] * nd,
        in_specs=in_specs, out_specs=[tok(rev) for rev in GLA_DIRS] + [sspec(rev) for rev in GLA_DIRS],
        scratch_shapes=[pltpu.VMEM((HEAD_PAIR, HEAD_PAIR), F32)] * nd,
        compiler_params=_params(),
    )(*args)


def _gla_bwd(hq, hi, zs, lbls, saved, do, nb, seq, group):
    t = hq.shape[0]
    rows = group * CHUNK
    nblk = seq // rows
    nd = len(GLA_DIRS)

    def body(*refs):
        ins, outs, dst_refs = refs[:6 * nd], refs[6 * nd:10 * nd], refs[10 * nd:]
        dl_refs = outs[3 * nd:]

        @pl.when(pl.program_id(2) == 0)
        def _():
            for dst_ref, dl_ref in zip(dst_refs, dl_refs):
                dst_ref[...] = jnp.zeros_like(dst_ref)
                dl_ref[...] = jnp.zeros_like(dl_ref)

        consts = [_gla_consts(rev) for rev in GLA_DIRS]
        dsts = [dst_ref[...] for dst_ref in dst_refs]
        dls = [[jnp.zeros((1, HEAD_PAIR), F32), jnp.zeros((1, HEAD_PAIR), F32)] for _ in GLA_DIRS]
        for cc in range(group):
            for d, rev in enumerate(GLA_DIRS):
                hq_ref, hi_ref, z_ref, lbl_ref, save_ref, do_ref = ins[6 * d:6 * d + 6]
                dq_ref, dv_ref, dz_ref = outs[3 * d:3 * d + 3]
                c = cc if rev else group - 1 - cc
                r = pl.ds(c * CHUNK, CHUNK)
                fn = functools.partial(_gla_chunk, consts=consts[d], reverse=rev)
                _, vjp = jax.vjp(fn, hq_ref[r, :], hi_ref[r, :], z_ref[r, :], lbl_ref[0:1, :], lbl_ref[1:2, :], save_ref[0, 0, c])
                d_hq, d_hi, d_z, d_l0, d_l1, dsts[d] = vjp((do_ref[r, :], dsts[d]))
                dq_ref[r, :] = d_hq
                dv_ref[r, :] = d_hi
                dz_ref[r, :] = d_z
                dls[d] = [dls[d][0] + d_l0, dls[d][1] + d_l1]
        for d in range(nd):
            dst_refs[d][...] = dsts[d]
            dl_refs[d][0, 0:1, :] += dls[d][0]
            dl_refs[d][0, 1:2, :] += dls[d][1]

    def tb(rev):
        return (lambda i: i) if rev else (lambda i: nblk - 1 - i)

    tok = lambda rev: pl.BlockSpec((rows, HEAD_PAIR), lambda b, p, i: (b * nblk + tb(rev)(i), p))
    lspec = pl.BlockSpec((2, HEAD_PAIR), lambda b, p, i: (0, p))
    sspec = lambda rev: pl.BlockSpec((1, 1, group, HEAD_PAIR, HEAD_PAIR), lambda b, p, i: (b, p, tb(rev)(i), 0, 0))
    args, in_specs, out_specs = [], [], []
    for d, rev in enumerate(GLA_DIRS):
        args += [hq, hi, zs[d], lbls[d], saved[d], do]
        in_specs += [tok(rev), tok(rev), tok(rev), lspec, sspec(rev), tok(rev)]
        out_specs += [tok(rev)] * 3
    out_specs += [pl.BlockSpec((1, 2, HEAD_PAIR), lambda b, p, i: (b, 0, p))] * nd
    return pl.pallas_call(
        body, name="gla_bwd", grid=(nb, 4, nblk),
        out_shape=[jax.ShapeDtypeStruct((t, A_WIDTH), F32)] * (3 * nd) + [jax.ShapeDtypeStruct((nb, 2, A_WIDTH), F32)] * nd,
        in_specs=in_specs, out_specs=out_specs,
        scratch_shapes=[pltpu.VMEM((HEAD_PAIR, HEAD_PAIR), F32)] * nd,
        compiler_params=_params(),
    )(*args)


def _head_mean_matrix():
    r = lax.broadcasted_iota(jnp.int32, (A_WIDTH, A_WIDTH), 0) // 64
    c = lax.broadcasted_iota(jnp.int32, (A_WIDTH, A_WIDTH), 1) // 64
    return jnp.where(r == c, 1.0 / 64.0, 0.0).astype(BF16)


def _gla_out(o_f, o_b, hg, g, mean_mat):
    o = o_f + o_b
    ms = _group_mean(o * o, mean_mat)
    return o * lax.rsqrt(ms + EPS) * g * (hg * _sigmoid(hg))


def _gla_combine(o_f, o_b, hg, g, tm):
    t = o_f.shape[0]

    def body(of_ref, ob_ref, hg_ref, g_ref, y_ref):
        y_ref[...] = _gla_out(of_ref[...], ob_ref[...], hg_ref[...], g_ref[...], _head_mean_matrix())

    tok = pl.BlockSpec((tm, A_WIDTH), lambda i: (i, 0))
    return pl.pallas_call(
        body, name="gla_combine_fwd", grid=(t // tm,), out_shape=jax.ShapeDtypeStruct((t, A_WIDTH), F32),
        in_specs=[tok, tok, tok, _const_spec((1, A_WIDTH))], out_specs=tok, compiler_params=_params(),
    )(o_f, o_b, hg, g)


def _gla_combine_bwd(o_f, o_b, hg, g, dy, tm):
    t = o_f.shape[0]

    def body(of_ref, ob_ref, hg_ref, g_ref, dy_ref, do_ref, dhg_ref, dg_ref):
        mean_mat = _head_mean_matrix()
        fn = lambda o, hgv, gv: _gla_out(o, jnp.zeros_like(o), hgv, gv, mean_mat)
        _, vjp = jax.vjp(fn, of_ref[...] + ob_ref[...], hg_ref[...], g_ref[...])
        d_o, d_hg, d_g = vjp(dy_ref[...])
        do_ref[...] = d_o
        dhg_ref[...] = d_hg

        @pl.when(pl.program_id(0) == 0)
        def _():
            dg_ref[...] = jnp.zeros_like(dg_ref)

        dg_ref[...] += d_g

    tok = pl.BlockSpec((tm, A_WIDTH), lambda i: (i, 0))
    vec = pl.BlockSpec((1, A_WIDTH), lambda i: (0, 0))
    return pl.pallas_call(
        body, name="gla_combine_bwd", grid=(t // tm,),
        out_shape=[jax.ShapeDtypeStruct((t, A_WIDTH), F32)] * 2 + [jax.ShapeDtypeStruct((1, A_WIDTH), F32)],
        in_specs=[tok, tok, tok, _const_spec((1, A_WIDTH)), tok], out_specs=[tok, tok, vec], compiler_params=_params(),
    )(o_f, o_b, hg, g, dy)


def _post_fwd(x, ya, oattn, tgt, g_mla, w_out, g2, w_gate, w_up, w_down, g_fin, tm):
    t = x.shape[0]

    def body(x_ref, ya_ref, oa_ref, tgt_ref, gm_ref, wo_ref, g2_ref, wg_ref, wu_ref, wd_ref, gf_ref, x1_ref, x2_ref, loss_ref):
        part = jnp.zeros((1, 1), F32)
        for j in range(tm // min(tm, ROW_SUB)):
            r = pl.ds(j * min(tm, ROW_SUB), min(tm, ROW_SUB))
            yb = _rms(oa_ref[r, :], gm_ref[...])
            x1 = x_ref[r, :] + _dot(ya_ref[r, :].astype(BF16), wo_ref[0:A_WIDTH, :]) + _dot(yb.astype(BF16), wo_ref[A_WIDTH:, :])
            x1_ref[r, :] = x1
            h2 = _rms(x1, g2_ref[...]).astype(BF16)
            gate = _dot(h2, wg_ref[...])
            act = (gate * _sigmoid(gate) * _dot(h2, wu_ref[...])).astype(BF16)
            x2 = x1 + _dot(act, wd_ref[...])
            x2_ref[r, :] = x2
            err = _rms(x2, gf_ref[...]) - tgt_ref[r, :]
            part = part + 0.5 * jnp.sum(jnp.mean(err * err, axis=-1, keepdims=True), axis=0, keepdims=True)

        @pl.when(pl.program_id(0) == 0)
        def _():
            loss_ref[...] = jnp.zeros_like(loss_ref)

        loss_ref[...] += jnp.broadcast_to(part, loss_ref.shape)

    tok = lambda wd: pl.BlockSpec((tm, wd), lambda i: (i, 0))
    return pl.pallas_call(
        body, name="post_fwd", grid=(t // tm,),
        out_shape=[jax.ShapeDtypeStruct((t, D_MODEL), F32)] * 2 + [jax.ShapeDtypeStruct((1, 128), F32)],
        in_specs=[tok(D_MODEL), tok(A_WIDTH), tok(512), tok(D_MODEL), _const_spec((1, 512)), _const_spec((D_MODEL, D_MODEL)),
                  _const_spec((1, D_MODEL)), _const_spec((D_MODEL, D_FF)), _const_spec((D_MODEL, D_FF)),
                  _const_spec((D_FF, D_MODEL)), _const_spec((1, D_MODEL))],
        out_specs=[tok(D_MODEL), tok(D_MODEL), pl.BlockSpec((1, 128), lambda i: (0, 0))],
        compiler_params=_params(),
    )(x, ya, oattn, tgt, g_mla, w_out, g2, w_gate, w_up, w_down, g_fin)


def _post_bwd(x1, x2, ya, oattn, tgt, g_mla, w_out, g2, w_gate, w_up, w_down, g_fin, tm):
    t = x1.shape[0]

    def body(x1_ref, x2_ref, ya_ref, oa_ref, tgt_ref, gm_ref, wo_ref, g2_ref, wg_ref, wu_ref, wd_ref, gf_ref,
             dx1_ref, dya_ref, doa_ref, ycat_ref, dx1b_ref, h2_ref, dgate_ref, dup_ref, act_ref, dx2b_ref,
             dgm_ref, dg2_ref, dgf_ref):
        x1, x2 = x1_ref[...], x2_ref[...]
        dy = (_rms(x2, gf_ref[...]) - tgt_ref[...]) * (1.0 / D_MODEL)
        dx2, dgf = _rms_bwd(x2, gf_ref[...], dy)
        dx2b = dx2.astype(BF16)
        dx2b_ref[...] = dx2b
        h2 = _rms(x1, g2_ref[...]).astype(BF16)
        h2_ref[...] = h2
        gate, up = _dot(h2, wg_ref[...]), _dot(h2, wu_ref[...])
        sg = _sigmoid(gate)
        sl = gate * sg
        act_ref[...] = (sl * up).astype(BF16)
        dact = _dot_nt(dx2b, wd_ref[...])
        dup = (dact * sl).astype(BF16)
        dgate = (dact * up * (sg * (1.0 + gate * (1.0 - sg)))).astype(BF16)
        dup_ref[...] = dup
        dgate_ref[...] = dgate
        dh2 = _dot_nt(dgate, wg_ref[...]) + _dot_nt(dup, wu_ref[...])
        dx1n, dg2 = _rms_bwd(x1, g2_ref[...], dh2)
        dx1 = dx2 + dx1n
        dx1_ref[...] = dx1
        dx1b = dx1.astype(BF16)
        dx1b_ref[...] = dx1b
        oa = oa_ref[...]
        ycat_ref[:, 0:A_WIDTH] = ya_ref[...].astype(BF16)
        ycat_ref[:, A_WIDTH:] = _rms(oa, gm_ref[...]).astype(BF16)
        dya_ref[...] = _dot_nt(dx1b, wo_ref[0:A_WIDTH, :])
        doa, dgm = _rms_bwd(oa, gm_ref[...], _dot_nt(dx1b, wo_ref[A_WIDTH:, :]))
        doa_ref[...] = doa

        @pl.when(pl.program_id(0) == 0)
        def _():
            dgm_ref[...] = jnp.zeros_like(dgm_ref)
            dg2_ref[...] = jnp.zeros_like(dg2_ref)
            dgf_ref[...] = jnp.zeros_like(dgf_ref)

        dgm_ref[...] += dgm
        dg2_ref[...] += dg2
        dgf_ref[...] += dgf

    tok = lambda wd: pl.BlockSpec((tm, wd), lambda i: (i, 0))
    vec = lambda wd: pl.BlockSpec((1, wd), lambda i: (0, 0))
    sds = lambda wd, dt: jax.ShapeDtypeStruct((t, wd), dt)
    return pl.pallas_call(
        body, name="post_bwd", grid=(t // tm,),
        out_shape=[sds(D_MODEL, F32), sds(512, F32), sds(512, F32), sds(D_MODEL, BF16), sds(D_MODEL, BF16), sds(D_MODEL, BF16),
                   sds(D_FF, BF16), sds(D_FF, BF16), sds(D_FF, BF16), sds(D_MODEL, BF16),
                   jax.ShapeDtypeStruct((1, 512), F32), jax.ShapeDtypeStruct((1, D_MODEL), F32), jax.ShapeDtypeStruct((1, D_MODEL), F32)],
        in_specs=[tok(D_MODEL), tok(D_MODEL), tok(512), tok(512), tok(D_MODEL), _const_spec((1, 512)),
                  _const_spec((D_MODEL, D_MODEL)), _const_spec((1, D_MODEL)), _const_spec((D_MODEL, D_FF)),
                  _const_spec((D_MODEL, D_FF)), _const_spec((D_FF, D_MODEL)), _const_spec((1, D_MODEL))],
        out_specs=[tok(D_MODEL), tok(512), tok(512), tok(D_MODEL), tok(D_MODEL), tok(D_MODEL), tok(D_FF), tok(D_FF), tok(D_FF),
                   tok(D_MODEL), vec(512), vec(D_MODEL), vec(D_MODEL)],
        compiler_params=_params(),
    )(x1, x2, ya, oattn, tgt, g_mla, w_out, g2, w_gate, w_up, w_down, g_fin)


def _matmul_tn(a, b, tn, tt, tag):
    t, k = a.shape
    n = b.shape[1]
    last = t // tt - 1

    def body(a_ref, b_ref, o_ref, acc_ref):
        part = _dot_tn(a_ref[...], b_ref[...])

        @pl.when(pl.program_id(1) == 0)
        def _():
            acc_ref[...] = part

        @pl.when(pl.program_id(1) > 0)
        def _():
            acc_ref[...] += part

        @pl.when(pl.program_id(1) == last)
        def _():
            o_ref[...] = acc_ref[...].astype(o_ref.dtype)

    return pl.pallas_call(
        body, name="wgrad_" + tag, grid=(n // tn, t // tt), out_shape=jax.ShapeDtypeStruct((k, n), BF16),
        in_specs=[pl.BlockSpec((tt, k), lambda j, i: (i, 0)), pl.BlockSpec((tt, tn), lambda j, i: (i, j))],
        out_specs=pl.BlockSpec((k, tn), lambda j, i: (0, j)), scratch_shapes=[pltpu.VMEM((k, tn), F32)],
        compiler_params=_params(),
    )(a, b)


def _mla_qkv_bwd(cq, ckv, g_qa, g_kva, w_q, w_kv, tables, dq, dk, dv, seq, tm):
    t = cq.shape[0]
    nblk = seq // tm

    def body(cq_ref, ckv_ref, gq_ref, gk_ref, wq_ref, wkv_ref, c_ref, sa_ref, sb_ref, dq_ref, dk_ref, dv_ref,
             dcq_ref, dckv_ref, dkr_ref, cqn_ref, dqf_ref, ckn_ref, dkv_ref, dgq_ref, dgk_ref):
        cos_t, sin_a, sin_b = c_ref[...], sa_ref[...], sb_ref[...]
        cqn_ref[...] = _rms(cq_ref[...], gq_ref[...]).astype(BF16)
        ckn_ref[...] = _rms(ckv_ref[...], gk_ref[...]).astype(BF16)
        dkr = jnp.zeros((tm, 128), F32)
        for h in range(B_HEADS):
            lo = h * QK_PAD
            dqf_ref[:, lo:lo + 128] = (dq_ref[:, lo:lo + 128] * ATTN_SCALE).astype(BF16)
            dqf_ref[:, lo + 128:lo + 256] = _rope_t(dq_ref[:, lo + 128:lo + 256] * ATTN_SCALE, cos_t, sin_a, sin_b).astype(BF16)
            dkv_ref[:, lo:lo + 128] = dk_ref[:, lo:lo + 128].astype(BF16)
            dkv_ref[:, lo + 128:lo + 256] = dv_ref[:, h * B_V:(h + 1) * B_V].astype(BF16)
            dkr = dkr + dk_ref[:, lo + 128:lo + 256]
        dkr_ref[...] = _rope_t(dkr, cos_t, sin_a, sin_b)
        dcq, dgq = _rms_bwd(cq_ref[...], gq_ref[...], _dot_nt(dqf_ref[...], wq_ref[...]))
        dckv, dgk = _rms_bwd(ckv_ref[...], gk_ref[...], _dot_nt(dkv_ref[...], wkv_ref[...]))
        dcq_ref[...] = dcq
        dckv_ref[...] = dckv

        @pl.when(pl.program_id(0) == 0)
        def _():
            dgq_ref[...] = jnp.zeros_like(dgq_ref)
            dgk_ref[...] = jnp.zeros_like(dgk_ref)

        dgq_ref[...] += dgq
        dgk_ref[...] += dgk

    tok = lambda wd: pl.BlockSpec((tm, wd), lambda i: (i, 0))
    vec = lambda wd: pl.BlockSpec((1, wd), lambda i: (0, 0))
    tab = pl.BlockSpec((tm, 128), lambda i: (i % nblk, 0))
    sds = lambda wd, dt: jax.ShapeDtypeStruct((t, wd), dt)
    return pl.pallas_call(
        body, name="mla_qkv_bwd", grid=(t // tm,),
        out_shape=[sds(Q_LORA, F32), sds(KV_LORA, F32), sds(128, F32), sds(Q_LORA, BF16), sds(1024, BF16), sds(KV_LORA, BF16),
                   sds(1024, BF16), jax.ShapeDtypeStruct((1, Q_LORA), F32), jax.ShapeDtypeStruct((1, KV_LORA), F32)],
        in_specs=[tok(Q_LORA), tok(KV_LORA), _const_spec((1, Q_LORA)), _const_spec((1, KV_LORA)),
                  _const_spec((Q_LORA, 1024)), _const_spec((KV_LORA, 1024)), tab, tab, tab,
                  tok(1024), tok(1024), tok(512)],
        out_specs=[tok(Q_LORA), tok(KV_LORA), tok(128), tok(Q_LORA), tok(1024), tok(KV_LORA), tok(1024),
                   vec(Q_LORA), vec(KV_LORA)],
        compiler_params=_params(),
    )(cq, ckv, g_qa, g_kva, w_q, w_kv, *tables, dq, dk, dv)


def _inproj_bwd(x, g1, w_in, dx1, pieces, tm):
    t = x.shape[0]
    counts = [len(p) for p in pieces]
    flat = [a for p in pieces for a in p]
    widths = [wd for wd, p in zip(IN_WIDTHS, pieces) for _ in p]

    def body(x_ref, g_ref, w_ref, dx1_ref, *refs):
        ins = refs[:len(flat)]
        dx_ref, h_ref, dp_ref, dg_ref = refs[len(flat):]
        xv = x_ref[...]
        h_ref[...] = _rms(xv, g_ref[...]).astype(BF16)
        off, j = 0, 0
        for wd, cnt in zip(IN_WIDTHS, counts):
            acc = ins[j][...]
            for jj in range(1, cnt):
                acc = acc + ins[j + jj][...]
            dp_ref[:, off:off + wd] = acc.astype(BF16)
            off += wd
            j += cnt
        dxn, dg = _rms_bwd(xv, g_ref[...], _dot_nt(dp_ref[...], w_ref[...]))
        dx_ref[...] = dx1_ref[...] + dxn

        @pl.when(pl.program_id(0) == 0)
        def _():
            dg_ref[...] = jnp.zeros_like(dg_ref)

        dg_ref[...] += dg

    tok = lambda wd: pl.BlockSpec((tm, wd), lambda i: (i, 0))
    return pl.pallas_call(
        body, name="inproj_bwd", grid=(t // tm,),
        out_shape=[jax.ShapeDtypeStruct((t, D_MODEL), F32), jax.ShapeDtypeStruct((t, D_MODEL), BF16),
                   jax.ShapeDtypeStruct((t, D_IN_PAD), BF16), jax.ShapeDtypeStruct((1, D_MODEL), F32)],
        in_specs=[tok(D_MODEL), _const_spec((1, D_MODEL)), _const_spec((D_MODEL, D_IN_PAD)), tok(D_MODEL)] + [tok(wd) for wd in widths],
        out_specs=[tok(D_MODEL), tok(D_MODEL), tok(D_IN_PAD), pl.BlockSpec((1, D_MODEL), lambda i: (0, 0))],
        compiler_params=_params(),
    )(x, g1, w_in, dx1, *flat)


def _cols_from_slots(g):
    n, r, cs = g.shape
    return g.transpose(1, 0, 2).reshape(r, n * cs)


def _cols_to_slots(full):
    r, c = full.shape
    return full.reshape(r, N_DEV, c // N_DEV).transpose(1, 0, 2)


def _arrange_w_in(w_in):
    return jnp.concatenate([w_in, jnp.zeros((D_MODEL, D_IN_PAD - D_IN), w_in.dtype)], axis=1)


def _arrange_w_q(w_q_b):
    q3 = w_q_b.reshape(Q_LORA, B_HEADS, B_NOPE + B_ROPE)
    pad = jnp.zeros((Q_LORA, B_HEADS, QK_PAD - B_NOPE - B_ROPE), w_q_b.dtype)
    return jnp.concatenate([q3, pad], axis=2).reshape(Q_LORA, B_HEADS * QK_PAD)


def _unarrange_w_q(d_q):
    return d_q.reshape(Q_LORA, B_HEADS, QK_PAD)[:, :, :B_NOPE + B_ROPE].reshape(Q_LORA, B_HEADS * (B_NOPE + B_ROPE))


def _step_core(x, loss_target, small_w, lb_full, early_full, late, seq, group, tiles, distributed):
    g1, g_hgrn, g_qa, g_kva, g_mla, g2, g_fin = small_w
    w_in, w_q, w_kv = _arrange_w_in(early_full[0]), _arrange_w_q(early_full[1]), early_full[2]
    nb = x.shape[0]
    t = nb * seq
    tm, tm_fwd, tq_f, tq_b, tt = tiles
    xt = x.reshape(t, D_MODEL)
    tgt = loss_target.reshape(t, D_MODEL)
    tables = _rope_tables(seq)

    hq, hi, zf, zb, hg, cq, ckv, kr = _inproj(xt, g1, w_in, tm_fwd)
    qcat, kcat, vv = _mla_qkv(cq, ckv, kr, g_qa, g_kva, w_q, w_kv, tables, seq, tm)
    if distributed:
        oattn, lse, *late_slots = _attn_fwd(qcat, kcat, vv, nb, seq, tq_f, gather=tuple(late))
    else:
        oattn, lse = _attn_fwd(qcat, kcat, vv, nb, seq, tq_f)
        late_slots = late
    w_out = late_slots[0].reshape(D_MODEL, D_MODEL)
    w_gate, w_up = _cols_from_slots(late_slots[1]), _cols_from_slots(late_slots[2])
    w_down = late_slots[3].reshape(D_FF, D_MODEL)
    lbl_f, lbl_b = lb_full[0], lb_full[1]
    o_f, o_b, save_f, save_b = _gla_fwd(hq, hi, (zf, zb), (lbl_f, lbl_b), nb, seq, group)
    ya = _gla_combine(o_f, o_b, hg, g_hgrn, tm)
    x1, x2, loss_row = _post_fwd(xt, ya, oattn, tgt, g_mla, w_out, g2, w_gate, w_up, w_down, g_fin, tm_fwd)

    (dx1, d_ya, d_oattn, ycat_b, dx1_b, h2_b, dgate_b, dup_b, act_b, dx2_b, d_g_mla, d_g2, d_g_fin) = _post_bwd(
        x1, x2, ya, oattn, tgt, g_mla, w_out, g2, w_gate, w_up, w_down, g_fin, tm)
    d_w_gate = _matmul_tn(h2_b, dgate_b, D_FF // 2, tt, "gate")
    d_w_up = _matmul_tn(h2_b, dup_b, D_FF // 2, tt, "up")
    d_w_down = _matmul_tn(act_b, dx2_b, 512, tt, "down")
    d_w_out = _matmul_tn(ycat_b, dx1_b, D_MODEL, tt, "out")
    late_g = [d_w_out.reshape(N_DEV, D_MODEL // N_DEV, D_MODEL), _cols_to_slots(d_w_gate), _cols_to_slots(d_w_up),
              d_w_down.reshape(N_DEV, D_FF // N_DEV, D_MODEL)]
    if distributed:
        dq, dk, dv, *late_g = _attn_bwd(qcat, kcat, vv, oattn, lse, d_oattn, nb, seq, tq_b, exchange=tuple(late_g))
    else:
        dq, dk, dv = _attn_bwd(qcat, kcat, vv, oattn, lse, d_oattn, nb, seq, tq_b)
    (d_cq, d_ckv, d_kr, cqn_b, dqf_b, ckn_b, dkv_b, d_g_qa, d_g_kva) = _mla_qkv_bwd(
        cq, ckv, g_qa, g_kva, w_q, w_kv, tables, dq, dk, dv, seq, tm)
    d_w_q = _matmul_tn(cqn_b, dqf_b, B_HEADS * QK_PAD, tt, "q_b")
    d_w_kv = _matmul_tn(ckn_b, dkv_b, B_HEADS * (B_NOPE + B_V), tt, "kv_b")
    d_o, d_hg, d_g_hgrn = _gla_combine_bwd(o_f, o_b, hg, g_hgrn, d_ya, tm)
    dq_f, dv_f, dz_f, dq_b, dv_b, dz_b, dl_f, dl_b = _gla_bwd(
        hq, hi, (zf, zb), (lbl_f, lbl_b), (save_f, save_b), d_o, nb, seq, group)
    grad_x, h1_b, dproj_b, d_g1 = _inproj_bwd(
        xt, g1, w_in, dx1, [[dq_f, dq_b], [dv_f, dv_b], [dz_f], [dz_b], [d_hg], [d_cq], [d_ckv], [d_kr]], tm)
    d_w_in_arr = _matmul_tn(h1_b, dproj_b, D_IN_PAD // 2, tt, "in")

    early_g = [_cols_to_slots(d_w_in_arr[:, :D_IN]), _cols_to_slots(_unarrange_w_q(d_w_q)), _cols_to_slots(d_w_kv)]
    d_lb = jnp.stack([jnp.sum(dl_f, axis=0), jnp.sum(dl_b, axis=0)], axis=0)
    small_grads = [d_g1, d_g_hgrn, d_g_qa, d_g_kva, d_g_mla, d_g2, d_g_fin]
    return loss_row, grad_x.reshape(nb, seq, D_MODEL), early_g, late_g, small_grads, d_lb


def kernel(x, norm1_g, w_in, lb_logits, hgrn_norm_g, q_a_norm_g, w_q_b, kv_a_norm_g, w_kv_b, mla_norm_g, w_out, norm2_g, w_gate, w_up, w_down, final_norm_g, loss_target, m_norm1_g, m_w_in, m_lb_logits, m_hgrn_norm_g, m_q_a_norm_g, m_w_q_b, m_kv_a_norm_g, m_w_kv_b, m_mla_norm_g, m_w_out, m_norm2_g, m_w_gate, m_w_up, m_w_down, m_final_norm_g, v_norm1_g, v_w_in, v_lb_logits, v_hgrn_norm_g, v_q_a_norm_g, v_w_q_b, v_kv_a_norm_g, v_w_kv_b, v_mla_norm_g, v_w_out, v_norm2_g, v_w_gate, v_w_up, v_w_down, v_final_norm_g):
    big_w = [w_in, w_q_b, w_kv_b, w_out, w_gate, w_up, w_down]
    big_m = [m_w_in, m_w_q_b, m_w_kv_b, m_w_out, m_w_gate, m_w_up, m_w_down]
    big_v = [v_w_in, v_w_q_b, v_w_kv_b, v_w_out, v_w_gate, v_w_up, v_w_down]
    small_w = [norm1_g, hgrn_norm_g, q_a_norm_g, kv_a_norm_g, mla_norm_g, norm2_g, final_norm_g]
    small_m = [m_norm1_g, m_hgrn_norm_g, m_q_a_norm_g, m_kv_a_norm_g, m_mla_norm_g, m_norm2_g, m_final_norm_g]
    small_v = [v_norm1_g, v_hgrn_norm_g, v_q_a_norm_g, v_kv_a_norm_g, v_mla_norm_g, v_norm2_g, v_final_norm_g]
    seq = x.shape[1]
    my_id = 4 * lax.axis_index("x") + 2 * lax.axis_index("y") + lax.axis_index("c")

    shard = lambda w: w[0].astype(BF16)
    g_in, g_q, g_kv, g_lb = _all_gather_call([shard(w_in), shard(w_q_b), shard(w_kv_b), lb_logits.reshape(4, 64)])
    early_full = (_cols_from_slots(g_in), _cols_from_slots(g_q), _cols_from_slots(g_kv))
    lb_full = g_lb.reshape(N_DEV, 2, 2, 64).transpose(1, 2, 0, 3).reshape(2, 2, 512)

    as_row = lambda a: a.reshape(1, -1)
    loss_row, grad_x, early_g, late_recv, small_g, d_lb = _step_core(
        x, loss_target, [as_row(s) for s in small_w], lb_full, early_full,
        [shard(w_out), shard(w_gate), shard(w_up), shard(w_down)], seq, min(8, seq // CHUNK),
        (256, 512, min(1024, seq), min(512, seq), min(2048, 2 * seq)), True)

    n_small = len(small_g)
    recv = _exchange_call(early_g + small_g + [d_lb.reshape(4, 512), loss_row], [True] * 3 + [False] * (n_small + 2))
    sums = _sum_slots_call(recv[3:])
    g_small = [g.reshape(s.shape) for g, s in zip(sums[:n_small], small_w)]
    g_lb_own = lax.dynamic_index_in_dim(sums[n_small].reshape(2, 2, N_DEV, 64), my_id, axis=2, keepdims=False)
    loss = sums[n_small + 1][0, 0]

    grads, deltas, new_ms, new_vs = {}, {}, {}, {}
    big_recv = dict(zip(["w_in", "w_q_b", "w_kv_b", "w_out", "w_gate", "w_up", "w_down"], list(recv[:3]) + list(late_recv)))
    for (name, _, _, _), w, m, v in zip(BIG, big_w, big_m, big_v):
        g, d, nm, nv = _adamw_recv(w[0], big_recv[name], m[0], v[0], name)
        grads[name], deltas[name], new_ms[name], new_vs[name] = g[None], d[None], nm[None], nv[None]
    lb_rows = lambda a: a.reshape(4, 64)
    d_s, nm_s, nv_s = _adamw_small(
        [as_row(a) for a in small_w] + [lb_rows(lb_logits)], [as_row(a) for a in g_small] + [lb_rows(g_lb_own)],
        [as_row(a) for a in small_m] + [lb_rows(m_lb_logits)], [as_row(a) for a in small_v] + [lb_rows(v_lb_logits)])
    for i, (s, (name, _)) in enumerate(zip(small_w + [lb_logits], SMALL + (("lb_logits", 0),))):
        grads[name] = (g_small + [g_lb_own])[i]
        deltas[name], new_ms[name], new_vs[name] = d_s[i].reshape(s.shape), nm_s[i].reshape(s.shape), nv_s[i].reshape(s.shape)

    order = ["norm1_g", "w_in", "lb_logits", "hgrn_norm_g", "q_a_norm_g", "w_q_b", "kv_a_norm_g", "w_kv_b", "mla_norm_g",
             "w_out", "norm2_g", "w_gate", "w_up", "w_down", "final_norm_g"]
    return (loss, grad_x, *[grads[n] for n in order], *[deltas[n] for n in order],
            *[new_ms[n] for n in order], *[new_vs[n] for n in order])
```

```python
import functools
import math

import jax
import jax.numpy as jnp
from jax import lax
from jax.experimental import pallas as pl
from jax.experimental.pallas import tpu as pltpu

F32 = jnp.float32
BF16 = jnp.bfloat16

N_DEV = 8
D_MODEL = 1024
D_FF = 2816
A_WIDTH = 512
HEAD_PAIR = 128
CHUNK = 64
B_HEADS = 4
B_NOPE = 128
B_ROPE = 64
B_V = 128
QK_PAD = 256
Q_LORA = 384
KV_LORA = 256
D_IN = 3264
D_IN_PAD = 3328
IN_WIDTHS = (512, 512, 512, 512, 512, Q_LORA, KV_LORA, 128)
ROPE_THETA = 10000.0
EPS = 1e-6
ATTN_SCALE = (B_NOPE + B_ROPE) ** -0.5
ATTN_SUB = 256
ATTN_SUB_BWD = 256
ROW_SUB = 256
ADAM_LR, ADAM_B1, ADAM_B2, ADAM_EPS, ADAM_WD, ADAM_STEP = 0.001, 0.9, 0.999, 1e-08, 0.01, 10
VMEM_LIMIT = 56 * 1024 * 1024
MESH = pl.DeviceIdType.MESH

BIG = (("w_in", 1024, D_IN, 1), ("w_q_b", Q_LORA, 768, 1), ("w_kv_b", KV_LORA, 1024, 1), ("w_out", 1024, 1024, 0),
       ("w_gate", 1024, D_FF, 1), ("w_up", 1024, D_FF, 1), ("w_down", D_FF, 1024, 0))
SMALL = (("norm1_g", 1024), ("hgrn_norm_g", 512), ("q_a_norm_g", 384), ("kv_a_norm_g", 256), ("mla_norm_g", 512),
         ("norm2_g", 1024), ("final_norm_g", 1024))


def _params(**kw):
    return pltpu.CompilerParams(vmem_limit_bytes=VMEM_LIMIT, **kw)


def _const_spec(shape):
    return pl.BlockSpec(shape, lambda *_: (0,) * len(shape), pipeline_mode=pl.Buffered(1))


def _dot(a, b):
    return jnp.dot(a, b, preferred_element_type=F32)


def _dot_nt(a, b):
    return lax.dot_general(a, b, (((1,), (1,)), ((), ())), preferred_element_type=F32)


def _dot_tn(a, b):
    return lax.dot_general(a, b, (((0,), (0,)), ((), ())), preferred_element_type=F32)


@jax.custom_vjp
def _mm(a, b):
    return _dot(a.astype(BF16), b.astype(BF16))


def _mm_fwd(a, b):
    return _mm(a, b), (a, b)


def _mm_bwd(res, g):
    a, b = res
    gb = g.astype(BF16)
    return _dot_nt(gb, b.astype(BF16)), _dot_tn(a.astype(BF16), gb)


_mm.defvjp(_mm_fwd, _mm_bwd)


@jax.custom_vjp
def _mm_nt(a, b):
    return _dot_nt(a.astype(BF16), b.astype(BF16))


def _mm_nt_fwd(a, b):
    return _mm_nt(a, b), (a, b)


def _mm_nt_bwd(res, g):
    a, b = res
    gb = g.astype(BF16)
    return _dot(gb, b.astype(BF16)), _dot_tn(gb, a.astype(BF16))


_mm_nt.defvjp(_mm_nt_fwd, _mm_nt_bwd)


@jax.custom_vjp
def _mm_tn(a, b):
    return _dot_tn(a.astype(BF16), b.astype(BF16))


def _mm_tn_fwd(a, b):
    return _mm_tn(a, b), (a, b)


def _mm_tn_bwd(res, g):
    a, b = res
    gb = g.astype(BF16)
    return _dot_nt(b.astype(BF16), gb), _dot(a.astype(BF16), gb)


_mm_tn.defvjp(_mm_tn_fwd, _mm_tn_bwd)


def _split3(a):
    hi = a.astype(BF16)
    r = a - hi.astype(F32)
    mid = r.astype(BF16)
    lo = (r - mid.astype(F32)).astype(BF16)
    return hi, mid, lo


def _dot_exact_rhs(a, m):
    hi, mid, lo = _split3(a)
    return _dot(hi, m) + _dot(mid, m) + _dot(lo, m)


@jax.custom_vjp
def _group_mean(a, m):
    return _dot_exact_rhs(a, m)


def _group_mean_fwd(a, m):
    return _group_mean(a, m), m


def _group_mean_bwd(m, g):
    return _dot_exact_rhs(g, m), jnp.zeros_like(m)


_group_mean.defvjp(_group_mean_fwd, _group_mean_bwd)


def _roll_rows(a, shift):
    return pltpu.roll(a, shift, 0)


def _cumsum_rows_raw(a, reverse):
    n = a.shape[0]
    row = lax.broadcasted_iota(jnp.int32, a.shape, 0)
    s = 1
    while s < n:
        if reverse:
            a = a + jnp.where(row < n - s, _roll_rows(a, n - s), 0.0)
        else:
            a = a + jnp.where(row >= s, _roll_rows(a, s), 0.0)
        s *= 2
    return a


@functools.partial(jax.custom_vjp, nondiff_argnums=(1,))
def _cumsum_rows(a, reverse):
    return _cumsum_rows_raw(a, reverse)


def _cumsum_rows_fwd(a, reverse):
    return _cumsum_rows_raw(a, reverse), None


def _cumsum_rows_bwd(reverse, _, g):
    return (_cumsum_rows_raw(g, not reverse),)


_cumsum_rows.defvjp(_cumsum_rows_fwd, _cumsum_rows_bwd)


def _rms(x, g):
    r = lax.rsqrt(jnp.mean(x * x, axis=-1, keepdims=True) + EPS)
    return x * r * g


def _rms_bwd(x, g, dy):
    r = lax.rsqrt(jnp.mean(x * x, axis=-1, keepdims=True) + EPS)
    xh = x * r
    dg = jnp.sum(dy * xh, axis=0, keepdims=True)
    dxh = dy * g
    dx = r * (dxh - xh * jnp.mean(dxh * xh, axis=-1, keepdims=True))
    return dx, dg


def _sigmoid(a):
    return jax.nn.sigmoid(a)


def _mesh_place():
    x, y, c = lax.axis_index("x"), lax.axis_index("y"), lax.axis_index("c")
    return x, y, c


def _dev_index(p):
    return 4 * p[0] + 2 * p[1] + p[2]


def _comm_sems(n):
    return [pltpu.SemaphoreType.DMA((n, 7)), pltpu.SemaphoreType.DMA((n, 7)), pltpu.SemaphoreType.DMA((n,))]


def _gather_protocol(ins, outs, send_sems, recv_sems, local_sems):
    n = len(ins)
    x, y, c = _mesh_place()
    me, sibling = (x, y, c), (x, y, 1 - c)
    chips = [(1 - x, y), (x, 1 - y), (1 - x, 1 - y)]

    def copy(a, k, block, to, src=None):
        slot = outs[a].at[_dev_index(block)]
        return pltpu.make_async_remote_copy(
            src_ref=slot if src is None else src, dst_ref=slot,
            send_sem=send_sems.at[a, k], recv_sem=recv_sems.at[a, k], device_id=to, device_id_type=MESH)

    def mine(a):
        return pltpu.make_async_copy(ins[a], outs[a].at[_dev_index(me)], local_sems.at[a])

    def first(a):
        return [copy(a, 0, me, sibling, src=ins[a])] + [copy(a, 1 + j, me, (*chip, c), src=ins[a]) for j, chip in enumerate(chips)]

    def start():
        for a in range(n):
            mine(a).start()
            for cp in first(a):
                cp.start()

    def forward():
        for a in range(n):
            for j, chip in enumerate(chips):
                copy(a, 1 + j, (*chip, c), me).wait_recv()
                copy(a, 4 + j, (*chip, c), sibling).start()

    def finish():
        for a in range(n):
            copy(a, 0, sibling, me).wait_recv()
            for j, chip in enumerate(chips):
                copy(a, 4 + j, (*chip, 1 - c), me).wait_recv()
        for a in range(n):
            mine(a).wait()
            for cp in first(a):
                cp.wait_send()
            for j, chip in enumerate(chips):
                copy(a, 4 + j, (*chip, c), sibling).wait_send()

    return start, forward, finish


def _exchange_protocol(ins, outs, scatter, send_sems, recv_sems, local_sems):
    n = len(ins)
    x, y, c = _mesh_place()
    me = (x, y, c)
    my_id = _dev_index(me)
    rels = [(dx, dy, dc) for dx in (0, 1) for dy in (0, 1) for dc in (0, 1)][1:]

    def peer_of(rel):
        return tuple(1 - v if d else v for v, d in zip(me, rel))

    def src(a, dev):
        return ins[a].at[dev] if scatter[a] else ins[a]

    def send(a, k):
        peer = peer_of(rels[k])
        return pltpu.make_async_remote_copy(
            src_ref=src(a, _dev_index(peer)), dst_ref=outs[a].at[my_id],
            send_sem=send_sems.at[a, k], recv_sem=recv_sems.at[a, k], device_id=peer, device_id_type=MESH)

    def arrival(a, k):
        peer = peer_of(rels[k])
        return pltpu.make_async_remote_copy(
            src_ref=src(a, my_id), dst_ref=outs[a].at[_dev_index(peer)],
            send_sem=send_sems.at[a, k], recv_sem=recv_sems.at[a, k], device_id=peer, device_id_type=MESH)

    def own(a):
        return pltpu.make_async_copy(src(a, my_id), outs[a].at[my_id], local_sems.at[a])

    def start():
        for a in range(n):
            own(a).start()
            for k in range(7):
                send(a, k).start()

    def finish():
        for a in range(n):
            for k in range(7):
                arrival(a, k).wait_recv()
        for a in range(n):
            for k in range(7):
                send(a, k).wait_send()
            own(a).wait()

    return start, finish


def _slot_shapes(blocks, scatter=None):
    return [jax.ShapeDtypeStruct(b.shape if (scatter and scatter[a]) else (N_DEV,) + b.shape, b.dtype) for a, b in enumerate(blocks)]


def _all_gather_call(blocks):
    n = len(blocks)

    def body(*refs):
        start, forward, finish = _gather_protocol(refs[:n], refs[n:2 * n], *refs[2 * n:])
        start()
        forward()
        finish()

    any_spec = pl.BlockSpec(memory_space=pl.ANY)
    return pl.pallas_call(
        body, name="weights_all_gather", out_shape=_slot_shapes(blocks),
        in_specs=[any_spec] * n, out_specs=[any_spec] * n, scratch_shapes=_comm_sems(n),
    )(*blocks)


def _exchange_call(blocks, scatter):
    n = len(blocks)

    def body(*refs):
        start, finish = _exchange_protocol(refs[:n], refs[n:2 * n], scatter, *refs[2 * n:])
        start()
        finish()

    any_spec = pl.BlockSpec(memory_space=pl.ANY)
    return pl.pallas_call(
        body, name="grad_exchange", out_shape=_slot_shapes(blocks, scatter),
        in_specs=[any_spec] * n, out_specs=[any_spec] * n, scratch_shapes=_comm_sems(n),
    )(*blocks)


def _sum_slots_call(recvs):
    n = len(recvs)

    def body(*refs):
        for in_ref, out_ref in zip(refs[:n], refs[n:]):
            acc = in_ref[0]
            for j in range(1, N_DEV):
                acc = acc + in_ref[j]
            out_ref[...] = acc

    return pl.pallas_call(
        body, name="small_grad_sum", out_shape=[jax.ShapeDtypeStruct(r.shape[1:], F32) for r in recvs],
        compiler_params=_params(),
    )(*recvs)


def _adam_update(w, g, m, v):
    nm = ADAM_B1 * m + (1.0 - ADAM_B1) * g
    nv = ADAM_B2 * v + (1.0 - ADAM_B2) * (g * g)
    bc1 = 1.0 - ADAM_B1 ** ADAM_STEP
    bc2 = 1.0 - ADAM_B2 ** ADAM_STEP
    return -ADAM_LR * ((nm / bc1) / (jnp.sqrt(nv / bc2) + ADAM_EPS) + ADAM_WD * w), nm, nv


def _adamw_recv(w, recv, m, v, tag):
    r, c = w.shape
    tr = r
    for cand in (512, 256, 128):
        if r > cand and r % cand == 0:
            tr = cand
            break

    def body(w_ref, r_ref, m_ref, v_ref, g_ref, d_ref, nm_ref, nv_ref):
        g = r_ref[0].astype(F32)
        for j in range(1, N_DEV):
            g = g + r_ref[j].astype(F32)
        g_ref[...] = g
        d_ref[...], nm_ref[...], nv_ref[...] = _adam_update(w_ref[...], g, m_ref[...], v_ref[...])

    spec = pl.BlockSpec((tr, c), lambda i: (i, 0))
    return pl.pallas_call(
        body, name="adamw_" + tag, out_shape=[jax.ShapeDtypeStruct(w.shape, F32)] * 4, grid=(r // tr,),
        in_specs=[spec, pl.BlockSpec((N_DEV, tr, c), lambda i: (0, i, 0)), spec, spec], out_specs=[spec] * 4,
        compiler_params=_params(),
    )(w, recv, m, v)


def _adamw_small(ws, gs, ms, vs):
    n = len(ws)

    def body(*refs):
        ins, outs = refs[:4 * n], refs[4 * n:]
        for a in range(n):
            d, nm, nv = _adam_update(ins[a][...], ins[n + a][...], ins[2 * n + a][...], ins[3 * n + a][...])
            outs[a][...], outs[n + a][...], outs[2 * n + a][...] = d, nm, nv

    out = pl.pallas_call(
        body, name="adamw_small", out_shape=[jax.ShapeDtypeStruct(w.shape, F32) for w in ws] * 3, compiler_params=_params(),
    )(*ws, *gs, *ms, *vs)
    return out[:n], out[n:2 * n], out[2 * n:]


def _tile(t, want):
    return want if t % want == 0 else t


def _inproj(x, g1, w_in, tm):
    t = x.shape[0]

    def body(x_ref, g_ref, w_ref, *outs):
        for j in range(tm // min(tm, ROW_SUB)):
            r = pl.ds(j * min(tm, ROW_SUB), min(tm, ROW_SUB))
            h = _rms(x_ref[r, :], g_ref[...]).astype(BF16)
            off = 0
            for o_ref, wd in zip(outs, IN_WIDTHS):
                o_ref[r, :] = _dot(h, w_ref[:, off:off + wd])
                off += wd

    return pl.pallas_call(
        body, name="inproj_fwd", grid=(t // tm,),
        out_shape=[jax.ShapeDtypeStruct((t, wd), F32) for wd in IN_WIDTHS],
        in_specs=[pl.BlockSpec((tm, D_MODEL), lambda i: (i, 0)), _const_spec((1, D_MODEL)), _const_spec((D_MODEL, D_IN_PAD))],
        out_specs=[pl.BlockSpec((tm, wd), lambda i: (i, 0)) for wd in IN_WIDTHS],
        compiler_params=_params(),
    )(x, g1, w_in)


def _rope_tables(seq):
    inv = 1.0 / (ROPE_THETA ** (jnp.arange(0, B_ROPE, 2, dtype=F32) / B_ROPE))
    ang = jnp.arange(seq, dtype=F32)[:, None] * inv[None, :]
    cos, sin = jnp.cos(ang), jnp.sin(ang)
    z32, z64 = jnp.zeros_like(cos), jnp.zeros((seq, 64), F32)
    cos_t = jnp.concatenate([cos, cos, z64], axis=1)
    sin_a = jnp.concatenate([-sin, z32, z64], axis=1)
    sin_b = jnp.concatenate([z32, sin, z64], axis=1)
    return cos_t, sin_a, sin_b


def _rope(t, cos_t, sin_a, sin_b):
    return t * cos_t + pltpu.roll(t, 96, 1) * sin_a + pltpu.roll(t, 32, 1) * sin_b


def _rope_t(d, cos_t, sin_a, sin_b):
    return d * cos_t + pltpu.roll(d * sin_a, 32, 1) + pltpu.roll(d * sin_b, 96, 1)


def _mla_qkv(cq, ckv, kr, g_qa, g_kva, w_q, w_kv, tables, seq, tm):
    t = cq.shape[0]
    nblk = seq // tm

    def body(cq_ref, ckv_ref, kr_ref, gq_ref, gk_ref, wq_ref, wkv_ref, c_ref, sa_ref, sb_ref, q_out, k_out, v_out):
        cos_t, sin_a, sin_b = c_ref[...], sa_ref[...], sb_ref[...]
        cqn = _rms(cq_ref[...], gq_ref[...]).astype(BF16)
        ckn = _rms(ckv_ref[...], gk_ref[...]).astype(BF16)
        kr_rot = _rope(kr_ref[...], cos_t, sin_a, sin_b).astype(BF16)
        for h in range(B_HEADS):
            lo = h * QK_PAD
            q_out[:, lo:lo + 128] = (_dot(cqn, wq_ref[:, lo:lo + 128]) * ATTN_SCALE).astype(BF16)
            qr = _rope(_dot(cqn, wq_ref[:, lo + 128:lo + 256]), cos_t, sin_a, sin_b)
            q_out[:, lo + 128:lo + 256] = (qr * ATTN_SCALE).astype(BF16)
            k_out[:, lo:lo + 128] = _dot(ckn, wkv_ref[:, lo:lo + 128]).astype(BF16)
            k_out[:, lo + 128:lo + 256] = kr_rot
            v_out[:, h * B_V:(h + 1) * B_V] = _dot(ckn, wkv_ref[:, lo + 128:lo + 256]).astype(BF16)

    tok = lambda wd: pl.BlockSpec((tm, wd), lambda i: (i, 0))
    tab = pl.BlockSpec((tm, 128), lambda i: (i % nblk, 0))
    return pl.pallas_call(
        body, name="mla_qkv_fwd", grid=(t // tm,),
        out_shape=[jax.ShapeDtypeStruct((t, B_HEADS * QK_PAD), BF16), jax.ShapeDtypeStruct((t, B_HEADS * QK_PAD), BF16),
                   jax.ShapeDtypeStruct((t, B_HEADS * B_V), BF16)],
        in_specs=[tok(Q_LORA), tok(KV_LORA), tok(128), _const_spec((1, Q_LORA)), _const_spec((1, KV_LORA)),
                  _const_spec((Q_LORA, B_HEADS * QK_PAD)), _const_spec((KV_LORA, 1024)), tab, tab, tab],
        out_specs=[tok(B_HEADS * QK_PAD), tok(B_HEADS * QK_PAD), tok(B_HEADS * B_V)],
        compiler_params=_params(),
    )(cq, ckv, kr, g_qa, g_kva, w_q, w_kv, *tables)


def _step_index(nq):
    return (pl.program_id(0) * B_HEADS + pl.program_id(1)) * nq + pl.program_id(2)


def _attn_fwd(qcat, kcat, v, nb, seq, tq, gather=()):
    t = qcat.shape[0]
    nq = seq // tq
    ng = len(gather)
    steps = nb * B_HEADS * nq

    def body(q_ref, k_ref, v_ref, *rest):
        o_ref, lse_ref = rest[ng:ng + 2]
        if ng:
            start, forward, finish = _gather_protocol(rest[:ng], rest[ng + 2:2 * ng + 2], *rest[2 * ng + 2:])
            pl.when(_step_index(nq) == 0)(start)
            pl.when(_step_index(nq) == (3 * steps) // 4)(forward)
        for j in range(tq // ATTN_SUB):
            r = pl.ds(j * ATTN_SUB, ATTN_SUB)
            s = _dot_nt(q_ref[r, :], k_ref[...])
            m = jnp.max(s, axis=-1, keepdims=True)
            p = jnp.exp(s - m)
            l = jnp.sum(p, axis=-1, keepdims=True)
            o_ref[r, :] = _dot(p.astype(BF16), v_ref[...]) / l
            lse_ref[0, r, :] = m + jnp.log(l)
        if ng:
            pl.when(_step_index(nq) == steps - 1)(finish)

    any_spec = pl.BlockSpec(memory_space=pl.ANY)
    return pl.pallas_call(
        body, name="attn_fwd", grid=(nb, B_HEADS, nq),
        out_shape=[jax.ShapeDtypeStruct((t, B_HEADS * B_V), F32), jax.ShapeDtypeStruct((B_HEADS, t, 1), F32)] + _slot_shapes(gather),
        in_specs=[pl.BlockSpec((tq, QK_PAD), lambda b, h, i: (b * nq + i, h)),
                  pl.BlockSpec((seq, QK_PAD), lambda b, h, i: (b, h)),
                  pl.BlockSpec((seq, B_V), lambda b, h, i: (b, h))] + [any_spec] * ng,
        out_specs=[pl.BlockSpec((tq, B_V), lambda b, h, i: (b * nq + i, h)),
                   pl.BlockSpec((1, tq, 1), lambda b, h, i: (h, b * nq + i, 0))] + [any_spec] * ng,
        scratch_shapes=_comm_sems(ng) if ng else [],
        compiler_params=_params(),
    )(qcat, kcat, v, *gather)


def _attn_bwd(qcat, kcat, v, o, lse, do, nb, seq, tq, exchange=()):
    t = qcat.shape[0]
    nq = seq // tq
    ne = len(exchange)
    steps = nb * B_HEADS * nq

    def body(q_ref, k_ref, v_ref, o_ref, lse_ref, do_ref, *rest):
        dq_ref, dk_ref, dv_ref = rest[ne:ne + 3]
        if ne:
            start, finish = _exchange_protocol(rest[:ne], rest[ne + 3:2 * ne + 3], [True] * ne, *rest[2 * ne + 3:])
            pl.when(_step_index(nq) == 0)(start)

        @pl.when(pl.program_id(2) == 0)
        def _():
            dv_ref[...] = jnp.zeros_like(dv_ref)
            dk_ref[...] = jnp.zeros_like(dk_ref)

        for j in range(tq // ATTN_SUB_BWD):
            r = pl.ds(j * ATTN_SUB_BWD, ATTN_SUB_BWD)
            q, k = q_ref[r, :], k_ref[...]
            do_f = do_ref[r, :]
            delta = jnp.sum(do_f * o_ref[r, :], axis=-1, keepdims=True)
            dob = do_f.astype(BF16)
            p = jnp.exp(_dot_nt(q, k) - lse_ref[0, r, :])
            ds = (p * (_dot_nt(dob, v_ref[...]) - delta)).astype(BF16)
            dq_ref[r, :] = _dot(ds, k)
            dv_ref[...] += _dot_tn(p.astype(BF16), dob)
            dk_ref[...] += _dot_tn(ds, q)
        if ne:
            pl.when(_step_index(nq) == steps - 1)(finish)

    qspec = lambda wd: pl.BlockSpec((tq, wd), lambda b, h, i: (b * nq + i, h))
    kspec = lambda wd: pl.BlockSpec((seq, wd), lambda b, h, i: (b, h))
    any_spec = pl.BlockSpec(memory_space=pl.ANY)
    return pl.pallas_call(
        body, name="attn_bwd", grid=(nb, B_HEADS, nq),
        out_shape=[jax.ShapeDtypeStruct((t, B_HEADS * QK_PAD), F32), jax.ShapeDtypeStruct((t, B_HEADS * QK_PAD), F32),
                   jax.ShapeDtypeStruct((t, B_HEADS * B_V), F32)] + _slot_shapes(exchange, [True] * ne),
        in_specs=[qspec(QK_PAD), kspec(QK_PAD), kspec(B_V), qspec(B_V),
                  pl.BlockSpec((1, tq, 1), lambda b, h, i: (h, b * nq + i, 0)), qspec(B_V)] + [any_spec] * ne,
        out_specs=[qspec(QK_PAD), kspec(QK_PAD), kspec(B_V)] + [any_spec] * ne,
        scratch_shapes=_comm_sems(ne) if ne else [],
        compiler_params=_params(),
    )(qcat, kcat, v, o, lse, do, *exchange)


def _gla_consts(reverse):
    row = lax.broadcasted_iota(jnp.int32, (CHUNK, CHUNK), 0)
    col = lax.broadcasted_iota(jnp.int32, (CHUNK, CHUNK), 1)
    causal = (row <= col) if reverse else (row >= col)
    lane = lax.broadcasted_iota(jnp.int32, (1, HEAD_PAIR), 1)
    m0 = (lane < 64).astype(F32)
    m1 = 1.0 - m0
    r2 = lax.broadcasted_iota(jnp.int32, (HEAD_PAIR, HEAD_PAIR), 0)
    c2 = lax.broadcasted_iota(jnp.int32, (HEAD_PAIR, HEAD_PAIR), 1)
    same_head = ((r2 < 64) == (c2 < 64)).astype(F32)
    return causal, m0, m1, same_head


def _gla_chunk(hq, hi, z, l0, l1, st, consts, reverse):
    q_dec, k_inv, k_end, decay = _gla_gates(hq, z, l0, l1, reverse)
    o, st_new = _gla_state(q_dec, st, decay, _gla_increment(hi, k_end, consts))
    return o + _gla_intra(q_dec, k_inv, hi, consts), st_new


def _gla_gates(hq, z, l0, l1, reverse):
    mx = jnp.maximum(l0, l1)
    e0, e1 = jnp.exp(l0 - mx), jnp.exp(l1 - mx)
    lb = e0 / (e0 + e1)
    q = hq * _sigmoid(hq)
    log_f = jnp.log(lb + (1.0 - lb) * _sigmoid(z))
    k = (1.0 - lb) * _sigmoid(-z)
    cum = _cumsum_rows(log_f, reverse)
    tot = jnp.sum(log_f, axis=0, keepdims=True)
    return q * jnp.exp(cum), k * jnp.exp(-cum), k * jnp.exp(tot - cum), jnp.exp(tot)


def _gla_intra(q_dec, k_inv, hi, consts):
    causal, m0, m1, _ = consts
    o = None
    for mh in (m0, m1):
        s = jnp.where(causal, _mm_nt(q_dec * mh, k_inv), 0.0)
        part = _mm(s, hi) * mh
        o = part if o is None else o + part
    return o


def _gla_increment(hi, k_end, consts):
    return _mm_tn(hi, k_end) * consts[3]


def _gla_state(q_dec, st, decay, inc):
    return _mm_nt(q_dec, st), st * decay + inc


GLA_DIRS = (False, True)
GLA_BATCH_FWD = 8
GLA_BATCH_BWD = 4


def _gla_fwd(hq, hi, zs, lbls, nb, seq, group):
    t = hq.shape[0]
    rows = group * CHUNK
    nblk = seq // rows
    n_chunks = seq // CHUNK
    nd = len(GLA_DIRS)

    def body(*refs):
        ins, outs, st_refs = refs[:4 * nd], refs[4 * nd:6 * nd], refs[6 * nd:]
        @pl.when(pl.program_id(2) == 0)
        def _():
            for st_ref in st_refs:
                st_ref[...] = jnp.zeros_like(st_ref)

        consts = [_gla_consts(rev) for rev in GLA_DIRS]
        work = [(d, rev, group - 1 - cc if rev else cc) for cc in range(group) for d, rev in enumerate(GLA_DIRS)]
        rows_of = lambda c: pl.ds(c * CHUNK, CHUNK)
        sts = [st_ref[...] for st_ref in st_refs]
        for w0 in range(0, len(work), GLA_BATCH_FWD):
            batch = work[w0:w0 + GLA_BATCH_FWD]
            gates, intra, incs = {}, {}, {}
            for d, rev, c in batch:
                hq_ref, _, z_ref, lbl_ref = ins[4 * d:4 * d + 4]
                gates[d, c] = _gla_gates(hq_ref[rows_of(c), :], z_ref[rows_of(c), :], lbl_ref[0:1, :], lbl_ref[1:2, :], rev)
            for d, rev, c in batch:
                hi_c = ins[4 * d + 1][rows_of(c), :]
                intra[d, c] = _gla_intra(gates[d, c][0], gates[d, c][1], hi_c, consts[d])
                incs[d, c] = _gla_increment(hi_c, gates[d, c][2], consts[d])
            for d, rev, c in batch:
                outs[nd + d][0, 0, c] = sts[d]
                o_state, sts[d] = _gla_state(gates[d, c][0], sts[d], gates[d, c][3], incs[d, c])
                outs[d][rows_of(c), :] = intra[d, c] + o_state
        for st_ref, st in zip(st_refs, sts):
            st_ref[...] = st

    def tb(rev):
        return (lambda i: nblk - 1 - i) if rev else (lambda i: i)

    tok = lambda rev: pl.BlockSpec((rows, HEAD_PAIR), lambda b, p, i: (b * nblk + tb(rev)(i), p))
    lspec = pl.BlockSpec((2, HEAD_PAIR), lambda b, p, i: (0, p))
    sspec = lambda rev: pl.BlockSpec((1, 1, group, HEAD_PAIR, HEAD_PAIR), lambda b, p, i: (b, p, tb(rev)(i), 0, 0))
    args, in_specs = [], []
    for d, rev in enumerate(GLA_DIRS):
        args += [hq, hi, zs[d], lbls[d]]
        in_specs += [tok(rev), tok(rev), tok(rev), lspec]
    return pl.pallas_call(
        body, name="gla_fwd", grid=(nb, 4, nblk),
        out_shape=[jax.ShapeDtypeStruct((t, A_WIDTH), F32)] * nd
        + [jax.ShapeDtypeStruct((nb, 4, n_chunks, HEAD_PAIR, HEAD_PAIR), F32)] * nd,
        in_specs=in_specs, out_specs=[tok(rev) for rev in GLA_DIRS] + [sspec(rev) for rev in GLA_DIRS],
        scratch_shapes=[pltpu.VMEM((HEAD_PAIR, HEAD_PAIR), F32)] * nd,
        compiler_params=_params(),
    )(*args)


def _gla_bwd(hq, hi, zs, lbls, saved, do, nb, seq, group):
    t = hq.shape[0]
    rows = group * CHUNK
    nblk = seq // rows
    nd = len(GLA_DIRS)

    def body(*refs):
        ins, outs, dst_refs = refs[:6 * nd], refs[6 * nd:10 * nd], refs[10 * nd:]
        dl_refs = outs[3 * nd:]

        @pl.when(pl.program_id(2) == 0)
        def _():
            for dst_ref, dl_ref in zip(dst_refs, dl_refs):
                dst_ref[...] = jnp.zeros_like(dst_ref)
                dl_ref[...] = jnp.zeros_like(dl_ref)

        consts = [_gla_consts(rev) for rev in GLA_DIRS]
        dsts = [dst_ref[...] for dst_ref in dst_refs]
        dls = [[jnp.zeros((1, HEAD_PAIR), F32), jnp.zeros((1, HEAD_PAIR), F32)] for _ in GLA_DIRS]
        work = [(d, rev, cc if rev else group - 1 - cc) for cc in range(group) for d, rev in enumerate(GLA_DIRS)]
        for w0 in range(0, len(work), GLA_BATCH_BWD):
            vjps = {}
            for d, rev, c in work[w0:w0 + GLA_BATCH_BWD]:
                hq_ref, hi_ref, z_ref, lbl_ref, save_ref, _ = ins[6 * d:6 * d + 6]
                r = pl.ds(c * CHUNK, CHUNK)
                fn = functools.partial(_gla_chunk, consts=consts[d], reverse=rev)
                _, vjps[d, c] = jax.vjp(fn, hq_ref[r, :], hi_ref[r, :], z_ref[r, :], lbl_ref[0:1, :], lbl_ref[1:2, :], save_ref[0, 0, c])
            for d, rev, c in work[w0:w0 + GLA_BATCH_BWD]:
                dq_ref, dv_ref, dz_ref = outs[3 * d:3 * d + 3]
                r = pl.ds(c * CHUNK, CHUNK)
                d_hq, d_hi, d_z, d_l0, d_l1, dsts[d] = vjps[d, c]((ins[6 * d + 5][r, :], dsts[d]))
                dq_ref[r, :] = d_hq
                dv_ref[r, :] = d_hi
                dz_ref[r, :] = d_z
                dls[d] = [dls[d][0] + d_l0, dls[d][1] + d_l1]
        for d in range(nd):
            dst_refs[d][...] = dsts[d]
            dl_refs[d][0, 0:1, :] += dls[d][0]
            dl_refs[d][0, 1:2, :] += dls[d][1]

    def tb(rev):
        return (lambda i: i) if rev else (lambda i: nblk - 1 - i)

    tok = lambda rev: pl.BlockSpec((rows, HEAD_PAIR), lambda b, p, i: (b * nblk + tb(rev)(i), p))
    lspec = pl.BlockSpec((2, HEAD_PAIR), lambda b, p, i: (0, p))
    sspec = lambda rev: pl.BlockSpec((1, 1, group, HEAD_PAIR, HEAD_PAIR), lambda b, p, i: (b, p, tb(rev)(i), 0, 0))
    args, in_specs, out_specs = [], [], []
    for d, rev in enumerate(GLA_DIRS):
        args += [hq, hi, zs[d], lbls[d], saved[d], do]
        in_specs += [tok(rev), tok(rev), tok(rev), lspec, sspec(rev), tok(rev)]
        out_specs += [tok(rev)] * 3
    out_specs += [pl.BlockSpec((1, 2, HEAD_PAIR), lambda b, p, i: (b, 0, p))] * nd
    return pl.pallas_call(
        body, name="gla_bwd", grid=(nb, 4, nblk),
        out_shape=[jax.ShapeDtypeStruct((t, A_WIDTH), F32)] * (3 * nd) + [jax.ShapeDtypeStruct((nb, 2, A_WIDTH), F32)] * nd,
        in_specs=in_specs, out_specs=out_specs,
        scratch_shapes=[pltpu.VMEM((HEAD_PAIR, HEAD_PAIR), F32)] * nd,
        compiler_params=_params(),
    )(*args)


def _head_mean_matrix():
    r = lax.broadcasted_iota(jnp.int32, (A_WIDTH, A_WIDTH), 0) // 64
    c = lax.broadcasted_iota(jnp.int32, (A_WIDTH, A_WIDTH), 1) // 64
    return jnp.where(r == c, 1.0 / 64.0, 0.0).astype(BF16)


def _gla_out(o_f, o_b, hg, g, mean_mat):
    o = o_f + o_b
    ms = _group_mean(o * o, mean_mat)
    return o * lax.rsqrt(ms + EPS) * g * (hg * _sigmoid(hg))


def _gla_combine(o_f, o_b, hg, g, tm):
    t = o_f.shape[0]

    def body(of_ref, ob_ref, hg_ref, g_ref, y_ref):
        y_ref[...] = _gla_out(of_ref[...], ob_ref[...], hg_ref[...], g_ref[...], _head_mean_matrix())

    tok = pl.BlockSpec((tm, A_WIDTH), lambda i: (i, 0))
    return pl.pallas_call(
        body, name="gla_combine_fwd", grid=(t // tm,), out_shape=jax.ShapeDtypeStruct((t, A_WIDTH), F32),
        in_specs=[tok, tok, tok, _const_spec((1, A_WIDTH))], out_specs=tok, compiler_params=_params(),
    )(o_f, o_b, hg, g)


def _gla_combine_bwd(o_f, o_b, hg, g, dy, tm):
    t = o_f.shape[0]

    def body(of_ref, ob_ref, hg_ref, g_ref, dy_ref, do_ref, dhg_ref, dg_ref):
        mean_mat = _head_mean_matrix()
        fn = lambda o, hgv, gv: _gla_out(o, jnp.zeros_like(o), hgv, gv, mean_mat)
        _, vjp = jax.vjp(fn, of_ref[...] + ob_ref[...], hg_ref[...], g_ref[...])
        d_o, d_hg, d_g = vjp(dy_ref[...])
        do_ref[...] = d_o
        dhg_ref[...] = d_hg

        @pl.when(pl.program_id(0) == 0)
        def _():
            dg_ref[...] = jnp.zeros_like(dg_ref)

        dg_ref[...] += d_g

    tok = pl.BlockSpec((tm, A_WIDTH), lambda i: (i, 0))
    vec = pl.BlockSpec((1, A_WIDTH), lambda i: (0, 0))
    return pl.pallas_call(
        body, name="gla_combine_bwd", grid=(t // tm,),
        out_shape=[jax.ShapeDtypeStruct((t, A_WIDTH), F32)] * 2 + [jax.ShapeDtypeStruct((1, A_WIDTH), F32)],
        in_specs=[tok, tok, tok, _const_spec((1, A_WIDTH)), tok], out_specs=[tok, tok, vec], compiler_params=_params(),
    )(o_f, o_b, hg, g, dy)


def _post_fwd(x, ya, oattn, tgt, g_mla, w_out, g2, w_gate, w_up, w_down, g_fin, tm):
    t = x.shape[0]

    def body(x_ref, ya_ref, oa_ref, tgt_ref, gm_ref, wo_ref, g2_ref, wg_ref, wu_ref, wd_ref, gf_ref, x1_ref, x2_ref, loss_ref):
        part = jnp.zeros((1, 1), F32)
        for j in range(tm // min(tm, ROW_SUB)):
            r = pl.ds(j * min(tm, ROW_SUB), min(tm, ROW_SUB))
            yb = _rms(oa_ref[r, :], gm_ref[...])
            x1 = x_ref[r, :] + _dot(ya_ref[r, :].astype(BF16), wo_ref[0:A_WIDTH, :]) + _dot(yb.astype(BF16), wo_ref[A_WIDTH:, :])
            x1_ref[r, :] = x1
            h2 = _rms(x1, g2_ref[...]).astype(BF16)
            gate = _dot(h2, wg_ref[...])
            act = (gate * _sigmoid(gate) * _dot(h2, wu_ref[...])).astype(BF16)
            x2 = x1 + _dot(act, wd_ref[...])
            x2_ref[r, :] = x2
            err = _rms(x2, gf_ref[...]) - tgt_ref[r, :]
            part = part + 0.5 * jnp.sum(jnp.mean(err * err, axis=-1, keepdims=True), axis=0, keepdims=True)

        @pl.when(pl.program_id(0) == 0)
        def _():
            loss_ref[...] = jnp.zeros_like(loss_ref)

        loss_ref[...] += jnp.broadcast_to(part, loss_ref.shape)

    tok = lambda wd: pl.BlockSpec((tm, wd), lambda i: (i, 0))
    return pl.pallas_call(
        body, name="post_fwd", grid=(t // tm,),
        out_shape=[jax.ShapeDtypeStruct((t, D_MODEL), F32)] * 2 + [jax.ShapeDtypeStruct((1, 128), F32)],
        in_specs=[tok(D_MODEL), tok(A_WIDTH), tok(512), tok(D_MODEL), _const_spec((1, 512)), _const_spec((D_MODEL, D_MODEL)),
                  _const_spec((1, D_MODEL)), _const_spec((D_MODEL, D_FF)), _const_spec((D_MODEL, D_FF)),
                  _const_spec((D_FF, D_MODEL)), _const_spec((1, D_MODEL))],
        out_specs=[tok(D_MODEL), tok(D_MODEL), pl.BlockSpec((1, 128), lambda i: (0, 0))],
        compiler_params=_params(),
    )(x, ya, oattn, tgt, g_mla, w_out, g2, w_gate, w_up, w_down, g_fin)


def _post_bwd(x1, x2, ya, oattn, tgt, g_mla, w_out, g2, w_gate, w_up, w_down, g_fin, tm):
    t = x1.shape[0]

    def body(x1_ref, x2_ref, ya_ref, oa_ref, tgt_ref, gm_ref, wo_ref, g2_ref, wg_ref, wu_ref, wd_ref, gf_ref,
             dx1_ref, dya_ref, doa_ref, ycat_ref, dx1b_ref, h2_ref, dgate_ref, dup_ref, act_ref, dx2b_ref,
             dgm_ref, dg2_ref, dgf_ref):
        x1, x2 = x1_ref[...], x2_ref[...]
        dy = (_rms(x2, gf_ref[...]) - tgt_ref[...]) * (1.0 / D_MODEL)
        dx2, dgf = _rms_bwd(x2, gf_ref[...], dy)
        dx2b = dx2.astype(BF16)
        dx2b_ref[...] = dx2b
        h2 = _rms(x1, g2_ref[...]).astype(BF16)
        h2_ref[...] = h2
        gate, up = _dot(h2, wg_ref[...]), _dot(h2, wu_ref[...])
        sg = _sigmoid(gate)
        sl = gate * sg
        act_ref[...] = (sl * up).astype(BF16)
        dact = _dot_nt(dx2b, wd_ref[...])
        dup = (dact * sl).astype(BF16)
        dgate = (dact * up * (sg * (1.0 + gate * (1.0 - sg)))).astype(BF16)
        dup_ref[...] = dup
        dgate_ref[...] = dgate
        dh2 = _dot_nt(dgate, wg_ref[...]) + _dot_nt(dup, wu_ref[...])
        dx1n, dg2 = _rms_bwd(x1, g2_ref[...], dh2)
        dx1 = dx2 + dx1n
        dx1_ref[...] = dx1
        dx1b = dx1.astype(BF16)
        dx1b_ref[...] = dx1b
        oa = oa_ref[...]
        ycat_ref[:, 0:A_WIDTH] = ya_ref[...].astype(BF16)
        ycat_ref[:, A_WIDTH:] = _rms(oa, gm_ref[...]).astype(BF16)
        dya_ref[...] = _dot_nt(dx1b, wo_ref[0:A_WIDTH, :])
        doa, dgm = _rms_bwd(oa, gm_ref[...], _dot_nt(dx1b, wo_ref[A_WIDTH:, :]))
        doa_ref[...] = doa

        @pl.when(pl.program_id(0) == 0)
        def _():
            dgm_ref[...] = jnp.zeros_like(dgm_ref)
            dg2_ref[...] = jnp.zeros_like(dg2_ref)
            dgf_ref[...] = jnp.zeros_like(dgf_ref)

        dgm_ref[...] += dgm
        dg2_ref[...] += dg2
        dgf_ref[...] += dgf

    tok = lambda wd: pl.BlockSpec((tm, wd), lambda i: (i, 0))
    vec = lambda wd: pl.BlockSpec((1, wd), lambda i: (0, 0))
    sds = lambda wd, dt: jax.ShapeDtypeStruct((t, wd), dt)
    return pl.pallas_call(
        body, name="post_bwd", grid=(t // tm,),
        out_shape=[sds(D_MODEL, F32), sds(512, F32), sds(512, F32), sds(D_MODEL, BF16), sds(D_MODEL, BF16), sds(D_MODEL, BF16),
                   sds(D_FF, BF16), sds(D_FF, BF16), sds(D_FF, BF16), sds(D_MODEL, BF16),
                   jax.ShapeDtypeStruct((1, 512), F32), jax.ShapeDtypeStruct((1, D_MODEL), F32), jax.ShapeDtypeStruct((1, D_MODEL), F32)],
        in_specs=[tok(D_MODEL), tok(D_MODEL), tok(512), tok(512), tok(D_MODEL), _const_spec((1, 512)),
                  _const_spec((D_MODEL, D_MODEL)), _const_spec((1, D_MODEL)), _const_spec((D_MODEL, D_FF)),
                  _const_spec((D_MODEL, D_FF)), _const_spec((D_FF, D_MODEL)), _const_spec((1, D_MODEL))],
        out_specs=[tok(D_MODEL), tok(512), tok(512), tok(D_MODEL), tok(D_MODEL), tok(D_MODEL), tok(D_FF), tok(D_FF), tok(D_FF),
                   tok(D_MODEL), vec(512), vec(D_MODEL), vec(D_MODEL)],
        compiler_params=_params(),
    )(x1, x2, ya, oattn, tgt, g_mla, w_out, g2, w_gate, w_up, w_down, g_fin)


def _matmul_tn(a, b, tn, tt, tag):
    t, k = a.shape
    n = b.shape[1]
    last = t // tt - 1

    def body(a_ref, b_ref, o_ref, acc_ref):
        part = _dot_tn(a_ref[...], b_ref[...])

        @pl.when(pl.program_id(1) == 0)
        def _():
            acc_ref[...] = part

        @pl.when(pl.program_id(1) > 0)
        def _():
            acc_ref[...] += part

        @pl.when(pl.program_id(1) == last)
        def _():
            o_ref[...] = acc_ref[...].astype(o_ref.dtype)

    return pl.pallas_call(
        body, name="wgrad_" + tag, grid=(n // tn, t // tt), out_shape=jax.ShapeDtypeStruct((k, n), BF16),
        in_specs=[pl.BlockSpec((tt, k), lambda j, i: (i, 0)), pl.BlockSpec((tt, tn), lambda j, i: (i, j))],
        out_specs=pl.BlockSpec((k, tn), lambda j, i: (0, j)), scratch_shapes=[pltpu.VMEM((k, tn), F32)],
        compiler_params=_params(),
    )(a, b)


def _mla_qkv_bwd(cq, ckv, g_qa, g_kva, w_q, w_kv, tables, dq, dk, dv, seq, tm):
    t = cq.shape[0]
    nblk = seq // tm

    def body(cq_ref, ckv_ref, gq_ref, gk_ref, wq_ref, wkv_ref, c_ref, sa_ref, sb_ref, dq_ref, dk_ref, dv_ref,
             dcq_ref, dckv_ref, dkr_ref, cqn_ref, dqf_ref, ckn_ref, dkv_ref, dgq_ref, dgk_ref):
        cos_t, sin_a, sin_b = c_ref[...], sa_ref[...], sb_ref[...]
        cqn_ref[...] = _rms(cq_ref[...], gq_ref[...]).astype(BF16)
        ckn_ref[...] = _rms(ckv_ref[...], gk_ref[...]).astype(BF16)
        dkr = jnp.zeros((tm, 128), F32)
        for h in range(B_HEADS):
            lo = h * QK_PAD
            dqf_ref[:, lo:lo + 128] = (dq_ref[:, lo:lo + 128] * ATTN_SCALE).astype(BF16)
            dqf_ref[:, lo + 128:lo + 256] = _rope_t(dq_ref[:, lo + 128:lo + 256] * ATTN_SCALE, cos_t, sin_a, sin_b).astype(BF16)
            dkv_ref[:, lo:lo + 128] = dk_ref[:, lo:lo + 128].astype(BF16)
            dkv_ref[:, lo + 128:lo + 256] = dv_ref[:, h * B_V:(h + 1) * B_V].astype(BF16)
            dkr = dkr + dk_ref[:, lo + 128:lo + 256]
        dkr_ref[...] = _rope_t(dkr, cos_t, sin_a, sin_b)
        dcq, dgq = _rms_bwd(cq_ref[...], gq_ref[...], _dot_nt(dqf_ref[...], wq_ref[...]))
        dckv, dgk = _rms_bwd(ckv_ref[...], gk_ref[...], _dot_nt(dkv_ref[...], wkv_ref[...]))
        dcq_ref[...] = dcq
        dckv_ref[...] = dckv

        @pl.when(pl.program_id(0) == 0)
        def _():
            dgq_ref[...] = jnp.zeros_like(dgq_ref)
            dgk_ref[...] = jnp.zeros_like(dgk_ref)

        dgq_ref[...] += dgq
        dgk_ref[...] += dgk

    tok = lambda wd: pl.BlockSpec((tm, wd), lambda i: (i, 0))
    vec = lambda wd: pl.BlockSpec((1, wd), lambda i: (0, 0))
    tab = pl.BlockSpec((tm, 128), lambda i: (i % nblk, 0))
    sds = lambda wd, dt: jax.ShapeDtypeStruct((t, wd), dt)
    return pl.pallas_call(
        body, name="mla_qkv_bwd", grid=(t // tm,),
        out_shape=[sds(Q_LORA, F32), sds(KV_LORA, F32), sds(128, F32), sds(Q_LORA, BF16), sds(1024, BF16), sds(KV_LORA, BF16),
                   sds(1024, BF16), jax.ShapeDtypeStruct((1, Q_LORA), F32), jax.ShapeDtypeStruct((1, KV_LORA), F32)],
        in_specs=[tok(Q_LORA), tok(KV_LORA), _const_spec((1, Q_LORA)), _const_spec((1, KV_LORA)),
                  _const_spec((Q_LORA, 1024)), _const_spec((KV_LORA, 1024)), tab, tab, tab,
                  tok(1024), tok(1024), tok(512)],
        out_specs=[tok(Q_LORA), tok(KV_LORA), tok(128), tok(Q_LORA), tok(1024), tok(KV_LORA), tok(1024),
                   vec(Q_LORA), vec(KV_LORA)],
        compiler_params=_params(),
    )(cq, ckv, g_qa, g_kva, w_q, w_kv, *tables, dq, dk, dv)


def _inproj_bwd(x, g1, w_in, dx1, pieces, tm):
    t = x.shape[0]
    counts = [len(p) for p in pieces]
    flat = [a for p in pieces for a in p]
    widths = [wd for wd, p in zip(IN_WIDTHS, pieces) for _ in p]

    def body(x_ref, g_ref, w_ref, dx1_ref, *refs):
        ins = refs[:len(flat)]
        dx_ref, h_ref, dp_ref, dg_ref = refs[len(flat):]
        xv = x_ref[...]
        h_ref[...] = _rms(xv, g_ref[...]).astype(BF16)
        off, j = 0, 0
        for wd, cnt in zip(IN_WIDTHS, counts):
            acc = ins[j][...]
            for jj in range(1, cnt):
                acc = acc + ins[j + jj][...]
            dp_ref[:, off:off + wd] = acc.astype(BF16)
            off += wd
            j += cnt
        dxn, dg = _rms_bwd(xv, g_ref[...], _dot_nt(dp_ref[...], w_ref[...]))
        dx_ref[...] = dx1_ref[...] + dxn

        @pl.when(pl.program_id(0) == 0)
        def _():
            dg_ref[...] = jnp.zeros_like(dg_ref)

        dg_ref[...] += dg

    tok = lambda wd: pl.BlockSpec((tm, wd), lambda i: (i, 0))
    return pl.pallas_call(
        body, name="inproj_bwd", grid=(t // tm,),
        out_shape=[jax.ShapeDtypeStruct((t, D_MODEL), F32), jax.ShapeDtypeStruct((t, D_MODEL), BF16),
                   jax.ShapeDtypeStruct((t, D_IN_PAD), BF16), jax.ShapeDtypeStruct((1, D_MODEL), F32)],
        in_specs=[tok(D_MODEL), _const_spec((1, D_MODEL)), _const_spec((D_MODEL, D_IN_PAD)), tok(D_MODEL)] + [tok(wd) for wd in widths],
        out_specs=[tok(D_MODEL), tok(D_MODEL), tok(D_IN_PAD), pl.BlockSpec((1, D_MODEL), lambda i: (0, 0))],
        compiler_params=_params(),
    )(x, g1, w_in, dx1, *flat)


def _cols_from_slots(g):
    n, r, cs = g.shape
    return g.transpose(1, 0, 2).reshape(r, n * cs)


def _cols_to_slots(full):
    r, c = full.shape
    return full.reshape(r, N_DEV, c // N_DEV).transpose(1, 0, 2)


def _arrange_w_in(w_in):
    return jnp.concatenate([w_in, jnp.zeros((D_MODEL, D_IN_PAD - D_IN), w_in.dtype)], axis=1)


def _arrange_w_q(w_q_b):
    q3 = w_q_b.reshape(Q_LORA, B_HEADS, B_NOPE + B_ROPE)
    pad = jnp.zeros((Q_LORA, B_HEADS, QK_PAD - B_NOPE - B_ROPE), w_q_b.dtype)
    return jnp.concatenate([q3, pad], axis=2).reshape(Q_LORA, B_HEADS * QK_PAD)


def _unarrange_w_q(d_q):
    return d_q.reshape(Q_LORA, B_HEADS, QK_PAD)[:, :, :B_NOPE + B_ROPE].reshape(Q_LORA, B_HEADS * (B_NOPE + B_ROPE))


def _step_core(x, loss_target, small_w, lb_full, early_full, late, seq, group, tiles, distributed):
    g1, g_hgrn, g_qa, g_kva, g_mla, g2, g_fin = small_w
    w_in, w_q, w_kv = _arrange_w_in(early_full[0]), _arrange_w_q(early_full[1]), early_full[2]
    nb = x.shape[0]
    t = nb * seq
    tm, tm_fwd, tq_f, tq_b, tt = tiles
    xt = x.reshape(t, D_MODEL)
    tgt = loss_target.reshape(t, D_MODEL)
    tables = _rope_tables(seq)

    hq, hi, zf, zb, hg, cq, ckv, kr = _inproj(xt, g1, w_in, tm_fwd)
    qcat, kcat, vv = _mla_qkv(cq, ckv, kr, g_qa, g_kva, w_q, w_kv, tables, seq, tm)
    if distributed:
        oattn, lse, *late_slots = _attn_fwd(qcat, kcat, vv, nb, seq, tq_f, gather=tuple(late))
    else:
        oattn, lse = _attn_fwd(qcat, kcat, vv, nb, seq, tq_f)
        late_slots = late
    w_out = late_slots[0].reshape(D_MODEL, D_MODEL)
    w_gate, w_up = _cols_from_slots(late_slots[1]), _cols_from_slots(late_slots[2])
    w_down = late_slots[3].reshape(D_FF, D_MODEL)
    lbl_f, lbl_b = lb_full[0], lb_full[1]
    o_f, o_b, save_f, save_b = _gla_fwd(hq, hi, (zf, zb), (lbl_f, lbl_b), nb, seq, group)
    ya = _gla_combine(o_f, o_b, hg, g_hgrn, tm)
    x1, x2, loss_row = _post_fwd(xt, ya, oattn, tgt, g_mla, w_out, g2, w_gate, w_up, w_down, g_fin, tm_fwd)

    (dx1, d_ya, d_oattn, ycat_b, dx1_b, h2_b, dgate_b, dup_b, act_b, dx2_b, d_g_mla, d_g2, d_g_fin) = _post_bwd(
        x1, x2, ya, oattn, tgt, g_mla, w_out, g2, w_gate, w_up, w_down, g_fin, tm)
    d_w_gate = _matmul_tn(h2_b, dgate_b, D_FF // 2, tt, "gate")
    d_w_up = _matmul_tn(h2_b, dup_b, D_FF // 2, tt, "up")
    d_w_down = _matmul_tn(act_b, dx2_b, 512, tt, "down")
    d_w_out = _matmul_tn(ycat_b, dx1_b, D_MODEL, tt, "out")
    late_g = [d_w_out.reshape(N_DEV, D_MODEL // N_DEV, D_MODEL), _cols_to_slots(d_w_gate), _cols_to_slots(d_w_up),
              d_w_down.reshape(N_DEV, D_FF // N_DEV, D_MODEL)]
    if distributed:
        dq, dk, dv, *late_g = _attn_bwd(qcat, kcat, vv, oattn, lse, d_oattn, nb, seq, tq_b, exchange=tuple(late_g))
    else:
        dq, dk, dv = _attn_bwd(qcat, kcat, vv, oattn, lse, d_oattn, nb, seq, tq_b)
    (d_cq, d_ckv, d_kr, cqn_b, dqf_b, ckn_b, dkv_b, d_g_qa, d_g_kva) = _mla_qkv_bwd(
        cq, ckv, g_qa, g_kva, w_q, w_kv, tables, dq, dk, dv, seq, tm)
    d_w_q = _matmul_tn(cqn_b, dqf_b, B_HEADS * QK_PAD, tt, "q_b")
    d_w_kv = _matmul_tn(ckn_b, dkv_b, B_HEADS * (B_NOPE + B_V), tt, "kv_b")
    d_o, d_hg, d_g_hgrn = _gla_combine_bwd(o_f, o_b, hg, g_hgrn, d_ya, tm)
    dq_f, dv_f, dz_f, dq_b, dv_b, dz_b, dl_f, dl_b = _gla_bwd(
        hq, hi, (zf, zb), (lbl_f, lbl_b), (save_f, save_b), d_o, nb, seq, group)
    grad_x, h1_b, dproj_b, d_g1 = _inproj_bwd(
        xt, g1, w_in, dx1, [[dq_f, dq_b], [dv_f, dv_b], [dz_f], [dz_b], [d_hg], [d_cq], [d_ckv], [d_kr]], tm)
    d_w_in_arr = _matmul_tn(h1_b, dproj_b, D_IN_PAD // 2, tt, "in")

    early_g = [_cols_to_slots(d_w_in_arr[:, :D_IN]), _cols_to_slots(_unarrange_w_q(d_w_q)), _cols_to_slots(d_w_kv)]
    d_lb = jnp.stack([jnp.sum(dl_f, axis=0), jnp.sum(dl_b, axis=0)], axis=0)
    small_grads = [d_g1, d_g_hgrn, d_g_qa, d_g_kva, d_g_mla, d_g2, d_g_fin]
    return loss_row, grad_x.reshape(nb, seq, D_MODEL), early_g, late_g, small_grads, d_lb


def kernel(x, norm1_g, w_in, lb_logits, hgrn_norm_g, q_a_norm_g, w_q_b, kv_a_norm_g, w_kv_b, mla_norm_g, w_out, norm2_g, w_gate, w_up, w_down, final_norm_g, loss_target, m_norm1_g, m_w_in, m_lb_logits, m_hgrn_norm_g, m_q_a_norm_g, m_w_q_b, m_kv_a_norm_g, m_w_kv_b, m_mla_norm_g, m_w_out, m_norm2_g, m_w_gate, m_w_up, m_w_down, m_final_norm_g, v_norm1_g, v_w_in, v_lb_logits, v_hgrn_norm_g, v_q_a_norm_g, v_w_q_b, v_kv_a_norm_g, v_w_kv_b, v_mla_norm_g, v_w_out, v_norm2_g, v_w_gate, v_w_up, v_w_down, v_final_norm_g):
    big_w = [w_in, w_q_b, w_kv_b, w_out, w_gate, w_up, w_down]
    big_m = [m_w_in, m_w_q_b, m_w_kv_b, m_w_out, m_w_gate, m_w_up, m_w_down]
    big_v = [v_w_in, v_w_q_b, v_w_kv_b, v_w_out, v_w_gate, v_w_up, v_w_down]
    small_w = [norm1_g, hgrn_norm_g, q_a_norm_g, kv_a_norm_g, mla_norm_g, norm2_g, final_norm_g]
    small_m = [m_norm1_g, m_hgrn_norm_g, m_q_a_norm_g, m_kv_a_norm_g, m_mla_norm_g, m_norm2_g, m_final_norm_g]
    small_v = [v_norm1_g, v_hgrn_norm_g, v_q_a_norm_g, v_kv_a_norm_g, v_mla_norm_g, v_norm2_g, v_final_norm_g]
    seq = x.shape[1]
    my_id = 4 * lax.axis_index("x") + 2 * lax.axis_index("y") + lax.axis_index("c")

    shard = lambda w: w[0].astype(BF16)
    g_in, g_q, g_kv, g_lb = _all_gather_call([shard(w_in), shard(w_q_b), shard(w_kv_b), lb_logits.reshape(4, 64)])
    early_full = (_cols_from_slots(g_in), _cols_from_slots(g_q), _cols_from_slots(g_kv))
    lb_full = g_lb.reshape(N_DEV, 2, 2, 64).transpose(1, 2, 0, 3).reshape(2, 2, 512)

    as_row = lambda a: a.reshape(1, -1)
    loss_row, grad_x, early_g, late_recv, small_g, d_lb = _step_core(
        x, loss_target, [as_row(s) for s in small_w], lb_full, early_full,
        [shard(w_out), shard(w_gate), shard(w_up), shard(w_down)], seq, min(8, seq // CHUNK),
        (256, 512, min(1024, seq), min(512, seq), min(2048, 2 * seq)), True)

    n_small = len(small_g)
    recv = _exchange_call(early_g + small_g + [d_lb.reshape(4, 512), loss_row], [True] * 3 + [False] * (n_small + 2))
    sums = _sum_slots_call(recv[3:])
    g_small = [g.reshape(s.shape) for g, s in zip(sums[:n_small], small_w)]
    g_lb_own = lax.dynamic_index_in_dim(sums[n_small].reshape(2, 2, N_DEV, 64), my_id, axis=2, keepdims=False)
    loss = sums[n_small + 1][0, 0]

    grads, deltas, new_ms, new_vs = {}, {}, {}, {}
    big_recv = dict(zip(["w_in", "w_q_b", "w_kv_b", "w_out", "w_gate", "w_up", "w_down"], list(recv[:3]) + list(late_recv)))
    for (name, _, _, _), w, m, v in zip(BIG, big_w, big_m, big_v):
        g, d, nm, nv = _adamw_recv(w[0], big_recv[name], m[0], v[0], name)
        grads[name], deltas[name], new_ms[name], new_vs[name] = g[None], d[None], nm[None], nv[None]
    lb_rows = lambda a: a.reshape(4, 64)
    d_s, nm_s, nv_s = _adamw_small(
        [as_row(a) for a in small_w] + [lb_rows(lb_logits)], [as_row(a) for a in g_small] + [lb_rows(g_lb_own)],
        [as_row(a) for a in small_m] + [lb_rows(m_lb_logits)], [as_row(a) for a in small_v] + [lb_rows(v_lb_logits)])
    for i, (s, (name, _)) in enumerate(zip(small_w + [lb_logits], SMALL + (("lb_logits", 0),))):
        grads[name] = (g_small + [g_lb_own])[i]
        deltas[name], new_ms[name], new_vs[name] = d_s[i].reshape(s.shape), nm_s[i].reshape(s.shape), nv_s[i].reshape(s.shape)

    order = ["norm1_g", "w_in", "lb_logits", "hgrn_norm_g", "q_a_norm_g", "w_q_b", "kv_a_norm_g", "w_kv_b", "mla_norm_g",
             "w_out", "norm2_g", "w_gate", "w_up", "w_down", "final_norm_g"]
    return (loss, grad_x, *[grads[n] for n in order], *[deltas[n] for n in order],
            *[new_ms[n] for n in order], *[new_vs[n] for n in order])
```

```python
import functools
import math

import jax
import jax.numpy as jnp
from jax import lax
from jax.experimental import pallas as pl
from jax.experimental.pallas import tpu as pltpu

F32 = jnp.float32
BF16 = jnp.bfloat16

N_DEV = 8
D_MODEL = 1024
D_FF = 2816
A_WIDTH = 512
HEAD_PAIR = 128
CHUNK = 64
B_HEADS = 4
B_NOPE = 128
B_ROPE = 64
B_V = 128
QK_PAD = 256
Q_LORA = 384
KV_LORA = 256
D_IN = 3264
D_IN_PAD = 3328
IN_WIDTHS = (512, 512, 512, 512, 512, Q_LORA, KV_LORA, 128)
ROPE_THETA = 10000.0
EPS = 1e-6
ATTN_SCALE = (B_NOPE + B_ROPE) ** -0.5
ATTN_SUB = 256
ATTN_SUB_BWD = 256
ROW_SUB = 256
ADAM_LR, ADAM_B1, ADAM_B2, ADAM_EPS, ADAM_WD, ADAM_STEP = 0.001, 0.9, 0.999, 1e-08, 0.01, 10
VMEM_LIMIT = 60 * 1024 * 1024
MESH = pl.DeviceIdType.MESH

BIG = (("w_in", 1024, D_IN, 1), ("w_q_b", Q_LORA, 768, 1), ("w_kv_b", KV_LORA, 1024, 1), ("w_out", 1024, 1024, 0),
       ("w_gate", 1024, D_FF, 1), ("w_up", 1024, D_FF, 1), ("w_down", D_FF, 1024, 0))
SMALL = (("norm1_g", 1024), ("hgrn_norm_g", 512), ("q_a_norm_g", 384), ("kv_a_norm_g", 256), ("mla_norm_g", 512),
         ("norm2_g", 1024), ("final_norm_g", 1024))


def _params(**kw):
    return pltpu.CompilerParams(vmem_limit_bytes=VMEM_LIMIT, **kw)


def _const_spec(shape):
    return pl.BlockSpec(shape, lambda *_: (0,) * len(shape), pipeline_mode=pl.Buffered(1))


def _dot(a, b):
    return jnp.dot(a, b, preferred_element_type=F32)


def _dot_nt(a, b):
    return lax.dot_general(a, b, (((1,), (1,)), ((), ())), preferred_element_type=F32)


def _dot_tn(a, b):
    return lax.dot_general(a, b, (((0,), (0,)), ((), ())), preferred_element_type=F32)


@jax.custom_vjp
def _mm(a, b):
    return _dot(a.astype(BF16), b.astype(BF16))


def _mm_fwd(a, b):
    return _mm(a, b), (a, b)


def _mm_bwd(res, g):
    a, b = res
    gb = g.astype(BF16)
    return _dot_nt(gb, b.astype(BF16)), _dot_tn(a.astype(BF16), gb)


_mm.defvjp(_mm_fwd, _mm_bwd)


@jax.custom_vjp
def _mm_nt(a, b):
    return _dot_nt(a.astype(BF16), b.astype(BF16))


def _mm_nt_fwd(a, b):
    return _mm_nt(a, b), (a, b)


def _mm_nt_bwd(res, g):
    a, b = res
    gb = g.astype(BF16)
    return _dot(gb, b.astype(BF16)), _dot_tn(gb, a.astype(BF16))


_mm_nt.defvjp(_mm_nt_fwd, _mm_nt_bwd)


@jax.custom_vjp
def _mm_tn(a, b):
    return _dot_tn(a.astype(BF16), b.astype(BF16))


def _mm_tn_fwd(a, b):
    return _mm_tn(a, b), (a, b)


def _mm_tn_bwd(res, g):
    a, b = res
    gb = g.astype(BF16)
    return _dot_nt(b.astype(BF16), gb), _dot(a.astype(BF16), gb)


_mm_tn.defvjp(_mm_tn_fwd, _mm_tn_bwd)


def _split3(a):
    hi = a.astype(BF16)
    r = a - hi.astype(F32)
    mid = r.astype(BF16)
    lo = (r - mid.astype(F32)).astype(BF16)
    return hi, mid, lo


def _dot_exact_rhs(a, m):
    hi, mid, lo = _split3(a)
    return _dot(hi, m) + _dot(mid, m) + _dot(lo, m)


@jax.custom_vjp
def _group_mean(a, m):
    return _dot_exact_rhs(a, m)


def _group_mean_fwd(a, m):
    return _group_mean(a, m), m


def _group_mean_bwd(m, g):
    return _dot_exact_rhs(g, m), jnp.zeros_like(m)


_group_mean.defvjp(_group_mean_fwd, _group_mean_bwd)


def _roll_rows(a, shift):
    return pltpu.roll(a, shift, 0)


def _cumsum_rows_raw(a, reverse):
    n = a.shape[0]
    row = lax.broadcasted_iota(jnp.int32, a.shape, 0)
    s = 1
    while s < n:
        if reverse:
            a = a + jnp.where(row < n - s, _roll_rows(a, n - s), 0.0)
        else:
            a = a + jnp.where(row >= s, _roll_rows(a, s), 0.0)
        s *= 2
    return a


@functools.partial(jax.custom_vjp, nondiff_argnums=(1,))
def _cumsum_rows(a, reverse):
    return _cumsum_rows_raw(a, reverse)


def _cumsum_rows_fwd(a, reverse):
    return _cumsum_rows_raw(a, reverse), None


def _cumsum_rows_bwd(reverse, _, g):
    return (_cumsum_rows_raw(g, not reverse),)


_cumsum_rows.defvjp(_cumsum_rows_fwd, _cumsum_rows_bwd)


def _rms(x, g):
    r = lax.rsqrt(jnp.mean(x * x, axis=-1, keepdims=True) + EPS)
    return x * r * g


def _rms_bwd(x, g, dy):
    r = lax.rsqrt(jnp.mean(x * x, axis=-1, keepdims=True) + EPS)
    xh = x * r
    dg = jnp.sum(dy * xh, axis=0, keepdims=True)
    dxh = dy * g
    dx = r * (dxh - xh * jnp.mean(dxh * xh, axis=-1, keepdims=True))
    return dx, dg


def _sigmoid(a):
    return jax.nn.sigmoid(a)


def _mesh_place():
    x, y, c = lax.axis_index("x"), lax.axis_index("y"), lax.axis_index("c")
    return x, y, c


def _dev_index(p):
    return 4 * p[0] + 2 * p[1] + p[2]


def _comm_sems(n):
    return [pltpu.SemaphoreType.DMA((n, 7)), pltpu.SemaphoreType.DMA((n, 7)), pltpu.SemaphoreType.DMA((n,))]


def _gather_protocol(ins, outs, send_sems, recv_sems, local_sems):
    n = len(ins)
    x, y, c = _mesh_place()
    me, sibling = (x, y, c), (x, y, 1 - c)
    chips = [(1 - x, y), (x, 1 - y), (1 - x, 1 - y)]

    def copy(a, k, block, to, src=None):
        slot = outs[a].at[_dev_index(block)]
        return pltpu.make_async_remote_copy(
            src_ref=slot if src is None else src, dst_ref=slot,
            send_sem=send_sems.at[a, k], recv_sem=recv_sems.at[a, k], device_id=to, device_id_type=MESH)

    def mine(a):
        return pltpu.make_async_copy(ins[a], outs[a].at[_dev_index(me)], local_sems.at[a])

    def first(a):
        return [copy(a, 0, me, sibling, src=ins[a])] + [copy(a, 1 + j, me, (*chip, c), src=ins[a]) for j, chip in enumerate(chips)]

    def start():
        for a in range(n):
            mine(a).start()
            for cp in first(a):
                cp.start()

    def forward():
        for a in range(n):
            for j, chip in enumerate(chips):
                copy(a, 1 + j, (*chip, c), me).wait_recv()
                copy(a, 4 + j, (*chip, c), sibling).start()

    def finish():
        for a in range(n):
            copy(a, 0, sibling, me).wait_recv()
            for j, chip in enumerate(chips):
                copy(a, 4 + j, (*chip, 1 - c), me).wait_recv()
        for a in range(n):
            mine(a).wait()
            for cp in first(a):
                cp.wait_send()
            for j, chip in enumerate(chips):
                copy(a, 4 + j, (*chip, c), sibling).wait_send()

    return start, forward, finish


def _exchange_protocol(ins, outs, scatter, send_sems, recv_sems, local_sems):
    n = len(ins)
    x, y, c = _mesh_place()
    me = (x, y, c)
    my_id = _dev_index(me)
    rels = [(dx, dy, dc) for dx in (0, 1) for dy in (0, 1) for dc in (0, 1)][1:]

    def peer_of(rel):
        return tuple(1 - v if d else v for v, d in zip(me, rel))

    def src(a, dev):
        return ins[a].at[dev] if scatter[a] else ins[a]

    def send(a, k):
        peer = peer_of(rels[k])
        return pltpu.make_async_remote_copy(
            src_ref=src(a, _dev_index(peer)), dst_ref=outs[a].at[my_id],
            send_sem=send_sems.at[a, k], recv_sem=recv_sems.at[a, k], device_id=peer, device_id_type=MESH)

    def arrival(a, k):
        peer = peer_of(rels[k])
        return pltpu.make_async_remote_copy(
            src_ref=src(a, my_id), dst_ref=outs[a].at[_dev_index(peer)],
            send_sem=send_sems.at[a, k], recv_sem=recv_sems.at[a, k], device_id=peer, device_id_type=MESH)

    def own(a):
        return pltpu.make_async_copy(src(a, my_id), outs[a].at[my_id], local_sems.at[a])

    def start():
        for a in range(n):
            own(a).start()
            for k in range(7):
                send(a, k).start()

    def finish():
        for a in range(n):
            for k in range(7):
                arrival(a, k).wait_recv()
        for a in range(n):
            for k in range(7):
                send(a, k).wait_send()
            own(a).wait()

    return start, finish


def _slot_shapes(blocks, scatter=None):
    return [jax.ShapeDtypeStruct(b.shape if (scatter and scatter[a]) else (N_DEV,) + b.shape, b.dtype) for a, b in enumerate(blocks)]


def _all_gather_call(blocks):
    n = len(blocks)

    def body(*refs):
        start, forward, finish = _gather_protocol(refs[:n], refs[n:2 * n], *refs[2 * n:])
        start()
        forward()
        finish()

    any_spec = pl.BlockSpec(memory_space=pl.ANY)
    return pl.pallas_call(
        body, name="weights_all_gather", out_shape=_slot_shapes(blocks),
        in_specs=[any_spec] * n, out_specs=[any_spec] * n, scratch_shapes=_comm_sems(n),
    )(*blocks)


def _exchange_call(blocks, scatter):
    n = len(blocks)

    def body(*refs):
        start, finish = _exchange_protocol(refs[:n], refs[n:2 * n], scatter, *refs[2 * n:])
        start()
        finish()

    any_spec = pl.BlockSpec(memory_space=pl.ANY)
    return pl.pallas_call(
        body, name="grad_exchange", out_shape=_slot_shapes(blocks, scatter),
        in_specs=[any_spec] * n, out_specs=[any_spec] * n, scratch_shapes=_comm_sems(n),
    )(*blocks)


def _sum_slots_call(recvs):
    n = len(recvs)

    def body(*refs):
        for in_ref, out_ref in zip(refs[:n], refs[n:]):
            acc = in_ref[0]
            for j in range(1, N_DEV):
                acc = acc + in_ref[j]
            out_ref[...] = acc

    return pl.pallas_call(
        body, name="small_grad_sum", out_shape=[jax.ShapeDtypeStruct(r.shape[1:], F32) for r in recvs],
        compiler_params=_params(),
    )(*recvs)


def _adam_update(w, g, m, v):
    nm = ADAM_B1 * m + (1.0 - ADAM_B1) * g
    nv = ADAM_B2 * v + (1.0 - ADAM_B2) * (g * g)
    bc1 = 1.0 - ADAM_B1 ** ADAM_STEP
    bc2 = 1.0 - ADAM_B2 ** ADAM_STEP
    return -ADAM_LR * ((nm / bc1) / (jnp.sqrt(nv / bc2) + ADAM_EPS) + ADAM_WD * w), nm, nv


def _adamw_recv(w, recv, m, v, tag):
    r, c = w.shape
    tr = r
    for cand in (512, 256, 128):
        if r > cand and r % cand == 0:
            tr = cand
            break

    def body(w_ref, r_ref, m_ref, v_ref, g_ref, d_ref, nm_ref, nv_ref):
        g = r_ref[0].astype(F32)
        for j in range(1, N_DEV):
            g = g + r_ref[j].astype(F32)
        g_ref[...] = g
        d_ref[...], nm_ref[...], nv_ref[...] = _adam_update(w_ref[...], g, m_ref[...], v_ref[...])

    spec = pl.BlockSpec((tr, c), lambda i: (i, 0))
    return pl.pallas_call(
        body, name="adamw_" + tag, out_shape=[jax.ShapeDtypeStruct(w.shape, F32)] * 4, grid=(r // tr,),
        in_specs=[spec, pl.BlockSpec((N_DEV, tr, c), lambda i: (0, i, 0)), spec, spec], out_specs=[spec] * 4,
        compiler_params=_params(),
    )(w, recv, m, v)


def _adamw_small(ws, gs, ms, vs):
    n = len(ws)

    def body(*refs):
        ins, outs = refs[:4 * n], refs[4 * n:]
        for a in range(n):
            d, nm, nv = _adam_update(ins[a][...], ins[n + a][...], ins[2 * n + a][...], ins[3 * n + a][...])
            outs[a][...], outs[n + a][...], outs[2 * n + a][...] = d, nm, nv

    out = pl.pallas_call(
        body, name="adamw_small", out_shape=[jax.ShapeDtypeStruct(w.shape, F32) for w in ws] * 3, compiler_params=_params(),
    )(*ws, *gs, *ms, *vs)
    return out[:n], out[n:2 * n], out[2 * n:]


def _tile(t, want):
    return want if t % want == 0 else t


def _inproj(x, g1, w_in, tm):
    t = x.shape[0]

    def body(x_ref, g_ref, w_ref, *outs):
        for j in range(tm // min(tm, ROW_SUB)):
            r = pl.ds(j * min(tm, ROW_SUB), min(tm, ROW_SUB))
            h = _rms(x_ref[r, :], g_ref[...]).astype(BF16)
            off = 0
            for o_ref, wd in zip(outs, IN_WIDTHS):
                o_ref[r, :] = _dot(h, w_ref[:, off:off + wd])
                off += wd

    return pl.pallas_call(
        body, name="inproj_fwd", grid=(t // tm,),
        out_shape=[jax.ShapeDtypeStruct((t, wd), F32) for wd in IN_WIDTHS],
        in_specs=[pl.BlockSpec((tm, D_MODEL), lambda i: (i, 0)), _const_spec((1, D_MODEL)), _const_spec((D_MODEL, D_IN_PAD))],
        out_specs=[pl.BlockSpec((tm, wd), lambda i: (i, 0)) for wd in IN_WIDTHS],
        compiler_params=_params(),
    )(x, g1, w_in)


def _rope_tables(seq):
    inv = 1.0 / (ROPE_THETA ** (jnp.arange(0, B_ROPE, 2, dtype=F32) / B_ROPE))
    ang = jnp.arange(seq, dtype=F32)[:, None] * inv[None, :]
    cos, sin = jnp.cos(ang), jnp.sin(ang)
    z32, z64 = jnp.zeros_like(cos), jnp.zeros((seq, 64), F32)
    cos_t = jnp.concatenate([cos, cos, z64], axis=1)
    sin_a = jnp.concatenate([-sin, z32, z64], axis=1)
    sin_b = jnp.concatenate([z32, sin, z64], axis=1)
    return cos_t, sin_a, sin_b


def _rope(t, cos_t, sin_a, sin_b):
    return t * cos_t + pltpu.roll(t, 96, 1) * sin_a + pltpu.roll(t, 32, 1) * sin_b


def _rope_t(d, cos_t, sin_a, sin_b):
    return d * cos_t + pltpu.roll(d * sin_a, 32, 1) + pltpu.roll(d * sin_b, 96, 1)


def _mla_qkv(cq, ckv, kr, g_qa, g_kva, w_q, w_kv, tables, seq, tm):
    t = cq.shape[0]
    nblk = seq // tm

    def body(cq_ref, ckv_ref, kr_ref, gq_ref, gk_ref, wq_ref, wkv_ref, c_ref, sa_ref, sb_ref, q_out, k_out, v_out):
        cos_t, sin_a, sin_b = c_ref[...], sa_ref[...], sb_ref[...]
        cqn = _rms(cq_ref[...], gq_ref[...]).astype(BF16)
        ckn = _rms(ckv_ref[...], gk_ref[...]).astype(BF16)
        kr_rot = _rope(kr_ref[...], cos_t, sin_a, sin_b).astype(BF16)
        for h in range(B_HEADS):
            lo = h * QK_PAD
            q_out[:, lo:lo + 128] = (_dot(cqn, wq_ref[:, lo:lo + 128]) * ATTN_SCALE).astype(BF16)
            qr = _rope(_dot(cqn, wq_ref[:, lo + 128:lo + 256]), cos_t, sin_a, sin_b)
            q_out[:, lo + 128:lo + 256] = (qr * ATTN_SCALE).astype(BF16)
            k_out[:, lo:lo + 128] = _dot(ckn, wkv_ref[:, lo:lo + 128]).astype(BF16)
            k_out[:, lo + 128:lo + 256] = kr_rot
            v_out[:, h * B_V:(h + 1) * B_V] = _dot(ckn, wkv_ref[:, lo + 128:lo + 256]).astype(BF16)

    tok = lambda wd: pl.BlockSpec((tm, wd), lambda i: (i, 0))
    tab = pl.BlockSpec((tm, 128), lambda i: (i % nblk, 0))
    return pl.pallas_call(
        body, name="mla_qkv_fwd", grid=(t // tm,),
        out_shape=[jax.ShapeDtypeStruct((t, B_HEADS * QK_PAD), BF16), jax.ShapeDtypeStruct((t, B_HEADS * QK_PAD), BF16),
                   jax.ShapeDtypeStruct((t, B_HEADS * B_V), BF16)],
        in_specs=[tok(Q_LORA), tok(KV_LORA), tok(128), _const_spec((1, Q_LORA)), _const_spec((1, KV_LORA)),
                  _const_spec((Q_LORA, B_HEADS * QK_PAD)), _const_spec((KV_LORA, 1024)), tab, tab, tab],
        out_specs=[tok(B_HEADS * QK_PAD), tok(B_HEADS * QK_PAD), tok(B_HEADS * B_V)],
        compiler_params=_params(),
    )(cq, ckv, kr, g_qa, g_kva, w_q, w_kv, *tables)


def _step_index(nq):
    return (pl.program_id(0) * B_HEADS + pl.program_id(1)) * nq + pl.program_id(2)


def _attn_fwd(qcat, kcat, v, nb, seq, tq, gather=()):
    t = qcat.shape[0]
    nq = seq // tq
    ng = len(gather)
    steps = nb * B_HEADS * nq

    def body(q_ref, k_ref, v_ref, *rest):
        o_ref, lse_ref = rest[ng:ng + 2]
        if ng:
            start, forward, finish = _gather_protocol(rest[:ng], rest[ng + 2:2 * ng + 2], *rest[2 * ng + 2:])
            pl.when(_step_index(nq) == 0)(start)
            pl.when(_step_index(nq) == (3 * steps) // 4)(forward)
        for j in range(tq // ATTN_SUB):
            r = pl.ds(j * ATTN_SUB, ATTN_SUB)
            s = _dot_nt(q_ref[r, :], k_ref[...])
            m = jnp.max(s, axis=-1, keepdims=True)
            p = jnp.exp(s - m)
            l = jnp.sum(p, axis=-1, keepdims=True)
            o_ref[r, :] = _dot(p.astype(BF16), v_ref[...]) / l
            lse_ref[0, r, :] = m + jnp.log(l)
        if ng:
            pl.when(_step_index(nq) == steps - 1)(finish)

    any_spec = pl.BlockSpec(memory_space=pl.ANY)
    return pl.pallas_call(
        body, name="attn_fwd", grid=(nb, B_HEADS, nq),
        out_shape=[jax.ShapeDtypeStruct((t, B_HEADS * B_V), F32), jax.ShapeDtypeStruct((B_HEADS, t, 1), F32)] + _slot_shapes(gather),
        in_specs=[pl.BlockSpec((tq, QK_PAD), lambda b, h, i: (b * nq + i, h)),
                  pl.BlockSpec((seq, QK_PAD), lambda b, h, i: (b, h)),
                  pl.BlockSpec((seq, B_V), lambda b, h, i: (b, h))] + [any_spec] * ng,
        out_specs=[pl.BlockSpec((tq, B_V), lambda b, h, i: (b * nq + i, h)),
                   pl.BlockSpec((1, tq, 1), lambda b, h, i: (h, b * nq + i, 0))] + [any_spec] * ng,
        scratch_shapes=_comm_sems(ng) if ng else [],
        compiler_params=_params(),
    )(qcat, kcat, v, *gather)


def _attn_bwd(qcat, kcat, v, o, lse, do, nb, seq, tq, exchange=()):
    t = qcat.shape[0]
    nq = seq // tq
    ne = len(exchange)
    steps = nb * B_HEADS * nq

    def body(q_ref, k_ref, v_ref, o_ref, lse_ref, do_ref, *rest):
        dq_ref, dk_ref, dv_ref = rest[ne:ne + 3]
        if ne:
            start, finish = _exchange_protocol(rest[:ne], rest[ne + 3:2 * ne + 3], [True] * ne, *rest[2 * ne + 3:])
            pl.when(_step_index(nq) == 0)(start)

        @pl.when(pl.program_id(2) == 0)
        def _():
            dv_ref[...] = jnp.zeros_like(dv_ref)
            dk_ref[...] = jnp.zeros_like(dk_ref)

        for j in range(tq // ATTN_SUB_BWD):
            r = pl.ds(j * ATTN_SUB_BWD, ATTN_SUB_BWD)
            q, k = q_ref[r, :], k_ref[...]
            do_f = do_ref[r, :]
            delta = jnp.sum(do_f * o_ref[r, :], axis=-1, keepdims=True)
            dob = do_f.astype(BF16)
            p = jnp.exp(_dot_nt(q, k) - lse_ref[0, r, :])
            ds = (p * (_dot_nt(dob, v_ref[...]) - delta)).astype(BF16)
            dq_ref[r, :] = _dot(ds, k)
            dv_ref[...] += _dot_tn(p.astype(BF16), dob)
            dk_ref[...] += _dot_tn(ds, q)
        if ne:
            pl.when(_step_index(nq) == steps - 1)(finish)

    qspec = lambda wd: pl.BlockSpec((tq, wd), lambda b, h, i: (b * nq + i, h))
    kspec = lambda wd: pl.BlockSpec((seq, wd), lambda b, h, i: (b, h))
    any_spec = pl.BlockSpec(memory_space=pl.ANY)
    return pl.pallas_call(
        body, name="attn_bwd", grid=(nb, B_HEADS, nq),
        out_shape=[jax.ShapeDtypeStruct((t, B_HEADS * QK_PAD), F32), jax.ShapeDtypeStruct((t, B_HEADS * QK_PAD), F32),
                   jax.ShapeDtypeStruct((t, B_HEADS * B_V), F32)] + _slot_shapes(exchange, [True] * ne),
        in_specs=[qspec(QK_PAD), kspec(QK_PAD), kspec(B_V), qspec(B_V),
                  pl.BlockSpec((1, tq, 1), lambda b, h, i: (h, b * nq + i, 0)), qspec(B_V)] + [any_spec] * ne,
        out_specs=[qspec(QK_PAD), kspec(QK_PAD), kspec(B_V)] + [any_spec] * ne,
        scratch_shapes=_comm_sems(ne) if ne else [],
        compiler_params=_params(),
    )(qcat, kcat, v, o, lse, do, *exchange)


def _gla_consts(reverse):
    row = lax.broadcasted_iota(jnp.int32, (CHUNK, CHUNK), 0)
    col = lax.broadcasted_iota(jnp.int32, (CHUNK, CHUNK), 1)
    causal = (row <= col) if reverse else (row >= col)
    lane = lax.broadcasted_iota(jnp.int32, (1, HEAD_PAIR), 1)
    m0 = (lane < 64).astype(F32)
    m1 = 1.0 - m0
    r2 = lax.broadcasted_iota(jnp.int32, (HEAD_PAIR, HEAD_PAIR), 0)
    c2 = lax.broadcasted_iota(jnp.int32, (HEAD_PAIR, HEAD_PAIR), 1)
    same_head = ((r2 < 64) == (c2 < 64)).astype(F32)
    return causal, m0, m1, same_head


def _gla_chunk(hq, hi, z, l0, l1, st, consts, reverse):
    q_dec, k_inv, k_end, decay = _gla_gates(hq, z, l0, l1, reverse)
    o, st_new = _gla_state(q_dec, st, decay, _gla_increment(hi, k_end, consts))
    return o + _gla_intra(q_dec, k_inv, hi, consts), st_new


def _gla_gates(hq, z, l0, l1, reverse):
    mx = jnp.maximum(l0, l1)
    e0, e1 = jnp.exp(l0 - mx), jnp.exp(l1 - mx)
    lb = e0 / (e0 + e1)
    q = hq * _sigmoid(hq)
    log_f = jnp.log(lb + (1.0 - lb) * _sigmoid(z))
    k = (1.0 - lb) * _sigmoid(-z)
    cum = _cumsum_rows(log_f, reverse)
    tot = jnp.sum(log_f, axis=0, keepdims=True)
    return q * jnp.exp(cum), k * jnp.exp(-cum), k * jnp.exp(tot - cum), jnp.exp(tot)


def _gla_intra(q_dec, k_inv, hi, consts):
    causal, m0, m1, _ = consts
    o = None
    for mh in (m0, m1):
        s = jnp.where(causal, _mm_nt(q_dec * mh, k_inv), 0.0)
        part = _mm(s, hi) * mh
        o = part if o is None else o + part
    return o


def _gla_increment(hi, k_end, consts):
    return _mm_tn(hi, k_end) * consts[3]


def _gla_state(q_dec, st, decay, inc):
    return _mm_nt(q_dec, st), st * decay + inc


GLA_DIRS = (False, True)
GLA_BATCH_FWD = 8
GLA_BATCH_BWD = 4


def _gla_fwd(hq, hi, zs, lbls, nb, seq, group):
    t = hq.shape[0]
    rows = group * CHUNK
    nblk = seq // rows
    n_chunks = seq // CHUNK
    nd = len(GLA_DIRS)

    def body(*refs):
        ins, outs, st_refs = refs[:4 * nd], refs[4 * nd:6 * nd], refs[6 * nd:]
        @pl.when(pl.program_id(2) == 0)
        def _():
            for st_ref in st_refs:
                st_ref[...] = jnp.zeros_like(st_ref)

        consts = [_gla_consts(rev) for rev in GLA_DIRS]
        work = [(d, rev, group - 1 - cc if rev else cc) for cc in range(group) for d, rev in enumerate(GLA_DIRS)]
        rows_of = lambda c: pl.ds(c * CHUNK, CHUNK)
        sts = [st_ref[...] for st_ref in st_refs]
        for w0 in range(0, len(work), GLA_BATCH_FWD):
            batch = work[w0:w0 + GLA_BATCH_FWD]
            gates, intra, incs = {}, {}, {}
            for d, rev, c in batch:
                hq_ref, _, z_ref, lbl_ref = ins[4 * d:4 * d + 4]
                gates[d, c] = _gla_gates(hq_ref[rows_of(c), :], z_ref[rows_of(c), :], lbl_ref[0:1, :], lbl_ref[1:2, :], rev)
            for d, rev, c in batch:
                hi_c = ins[4 * d + 1][rows_of(c), :]
                intra[d, c] = _gla_intra(gates[d, c][0], gates[d, c][1], hi_c, consts[d])
                incs[d, c] = _gla_increment(hi_c, gates[d, c][2], consts[d])
            for d, rev, c in batch:
                outs[nd + d][0, 0, c] = sts[d]
                o_state, sts[d] = _gla_state(gates[d, c][0], sts[d], gates[d, c][3], incs[d, c])
                outs[d][rows_of(c), :] = intra[d, c] + o_state
        for st_ref, st in zip(st_refs, sts):
            st_ref[...] = st

    def tb(rev):
        return (lambda i: nblk - 1 - i) if rev else (lambda i: i)

    tok = lambda rev: pl.BlockSpec((rows, HEAD_PAIR), lambda b, p, i: (b * nblk + tb(rev)(i), p))
    lspec = pl.BlockSpec((2, HEAD_PAIR), lambda b, p, i: (0, p))
    sspec = lambda rev: pl.BlockSpec((1, 1, group, HEAD_PAIR, HEAD_PAIR), lambda b, p, i: (b, p, tb(rev)(i), 0, 0))
    args, in_specs = [], []
    for d, rev in enumerate(GLA_DIRS):
        args += [hq, hi, zs[d], lbls[d]]
        in_specs += [tok(rev), tok(rev), tok(rev), lspec]
    return pl.pallas_call(
        body, name="gla_fwd", grid=(nb, 4, nblk),
        out_shape=[jax.ShapeDtypeStruct((t, A_WIDTH), F32)] * nd
        + [jax.ShapeDtypeStruct((nb, 4, n_chunks, HEAD_PAIR, HEAD_PAIR), F32)] * nd,
        in_specs=in_specs, out_specs=[tok(rev) for rev in GLA_DIRS] + [sspec(rev) for rev in GLA_DIRS],
        scratch_shapes=[pltpu.VMEM((HEAD_PAIR, HEAD_PAIR), F32)] * nd,
        compiler_params=_params(),
    )(*args)


def _gla_bwd(hq, hi, zs, lbls, saved, do, nb, seq, group):
    t = hq.shape[0]
    rows = group * CHUNK
    nblk = seq // rows
    nd = len(GLA_DIRS)

    def body(*refs):
        ins, outs, dst_refs = refs[:6 * nd], refs[6 * nd:10 * nd], refs[10 * nd:]
        dl_refs = outs[3 * nd:]

        @pl.when(pl.program_id(2) == 0)
        def _():
            for dst_ref, dl_ref in zip(dst_refs, dl_refs):
                dst_ref[...] = jnp.zeros_like(dst_ref)
                dl_ref[...] = jnp.zeros_like(dl_ref)

        consts = [_gla_consts(rev) for rev in GLA_DIRS]
        dsts = [dst_ref[...] for dst_ref in dst_refs]
        dls = [[jnp.zeros((1, HEAD_PAIR), F32), jnp.zeros((1, HEAD_PAIR), F32)] for _ in GLA_DIRS]
        work = [(d, rev, cc if rev else group - 1 - cc) for cc in range(group) for d, rev in enumerate(GLA_DIRS)]
        for w0 in range(0, len(work), GLA_BATCH_BWD):
            vjps = {}
            for d, rev, c in work[w0:w0 + GLA_BATCH_BWD]:
                hq_ref, hi_ref, z_ref, lbl_ref, save_ref, _ = ins[6 * d:6 * d + 6]
                r = pl.ds(c * CHUNK, CHUNK)
                fn = functools.partial(_gla_chunk, consts=consts[d], reverse=rev)
                _, vjps[d, c] = jax.vjp(fn, hq_ref[r, :], hi_ref[r, :], z_ref[r, :], lbl_ref[0:1, :], lbl_ref[1:2, :], save_ref[0, 0, c])
            for d, rev, c in work[w0:w0 + GLA_BATCH_BWD]:
                dq_ref, dv_ref, dz_ref = outs[3 * d:3 * d + 3]
                r = pl.ds(c * CHUNK, CHUNK)
                d_hq, d_hi, d_z, d_l0, d_l1, dsts[d] = vjps[d, c]((ins[6 * d + 5][r, :], dsts[d]))
                dq_ref[r, :] = d_hq
                dv_ref[r, :] = d_hi
                dz_ref[r, :] = d_z
                dls[d] = [dls[d][0] + d_l0, dls[d][1] + d_l1]
        for d in range(nd):
            dst_refs[d][...] = dsts[d]
            dl_refs[d][0, 0:1, :] += dls[d][0]
            dl_refs[d][0, 1:2, :] += dls[d][1]

    def tb(rev):
        return (lambda i: i) if rev else (lambda i: nblk - 1 - i)

    tok = lambda rev: pl.BlockSpec((rows, HEAD_PAIR), lambda b, p, i: (b * nblk + tb(rev)(i), p))
    lspec = pl.BlockSpec((2, HEAD_PAIR), lambda b, p, i: (0, p))
    sspec = lambda rev: pl.BlockSpec((1, 1, group, HEAD_PAIR, HEAD_PAIR), lambda b, p, i: (b, p, tb(rev)(i), 0, 0))
    args, in_specs, out_specs = [], [], []
    for d, rev in enumerate(GLA_DIRS):
        args += [hq, hi, zs[d], lbls[d], saved[d], do]
        in_specs += [tok(rev), tok(rev), tok(rev), lspec, sspec(rev), tok(rev)]
        out_specs += [tok(rev)] * 3
    out_specs += [pl.BlockSpec((1, 2, HEAD_PAIR), lambda b, p, i: (b, 0, p))] * nd
    return pl.pallas_call(
        body, name="gla_bwd", grid=(nb, 4, nblk),
        out_shape=[jax.ShapeDtypeStruct((t, A_WIDTH), F32)] * (3 * nd) + [jax.ShapeDtypeStruct((nb, 2, A_WIDTH), F32)] * nd,
        in_specs=in_specs, out_specs=out_specs,
        scratch_shapes=[pltpu.VMEM((HEAD_PAIR, HEAD_PAIR), F32)] * nd,
        compiler_params=_params(),
    )(*args)


def _head_mean_matrix():
    r = lax.broadcasted_iota(jnp.int32, (A_WIDTH, A_WIDTH), 0) // 64
    c = lax.broadcasted_iota(jnp.int32, (A_WIDTH, A_WIDTH), 1) // 64
    return jnp.where(r == c, 1.0 / 64.0, 0.0).astype(BF16)


def _gla_out(o_f, o_b, hg, g, mean_mat):
    o = o_f + o_b
    ms = _group_mean(o * o, mean_mat)
    return o * lax.rsqrt(ms + EPS) * g * (hg * _sigmoid(hg))


def _gla_combine(o_f, o_b, hg, g, tm):
    t = o_f.shape[0]

    def body(of_ref, ob_ref, hg_ref, g_ref, y_ref):
        y_ref[...] = _gla_out(of_ref[...], ob_ref[...], hg_ref[...], g_ref[...], _head_mean_matrix())

    tok = pl.BlockSpec((tm, A_WIDTH), lambda i: (i, 0))
    return pl.pallas_call(
        body, name="gla_combine_fwd", grid=(t // tm,), out_shape=jax.ShapeDtypeStruct((t, A_WIDTH), F32),
        in_specs=[tok, tok, tok, _const_spec((1, A_WIDTH))], out_specs=tok, compiler_params=_params(),
    )(o_f, o_b, hg, g)


def _gla_combine_bwd(o_f, o_b, hg, g, dy, tm):
    t = o_f.shape[0]

    def body(of_ref, ob_ref, hg_ref, g_ref, dy_ref, do_ref, dhg_ref, dg_ref):
        mean_mat = _head_mean_matrix()
        fn = lambda o, hgv, gv: _gla_out(o, jnp.zeros_like(o), hgv, gv, mean_mat)
        _, vjp = jax.vjp(fn, of_ref[...] + ob_ref[...], hg_ref[...], g_ref[...])
        d_o, d_hg, d_g = vjp(dy_ref[...])
        do_ref[...] = d_o
        dhg_ref[...] = d_hg

        @pl.when(pl.program_id(0) == 0)
        def _():
            dg_ref[...] = jnp.zeros_like(dg_ref)

        dg_ref[...] += d_g

    tok = pl.BlockSpec((tm, A_WIDTH), lambda i: (i, 0))
    vec = pl.BlockSpec((1, A_WIDTH), lambda i: (0, 0))
    return pl.pallas_call(
        body, name="gla_combine_bwd", grid=(t // tm,),
        out_shape=[jax.ShapeDtypeStruct((t, A_WIDTH), F32)] * 2 + [jax.ShapeDtypeStruct((1, A_WIDTH), F32)],
        in_specs=[tok, tok, tok, _const_spec((1, A_WIDTH)), tok], out_specs=[tok, tok, vec], compiler_params=_params(),
    )(o_f, o_b, hg, g, dy)


def _post_fwd(x, ya, oattn, tgt, g_mla, w_out, g2, w_gate, w_up, w_down, g_fin, tm):
    t = x.shape[0]

    def body(x_ref, ya_ref, oa_ref, tgt_ref, gm_ref, wo_ref, g2_ref, wg_ref, wu_ref, wd_ref, gf_ref,
             x1_ref, x2_ref, gate_ref, up_ref, loss_ref):
        part = jnp.zeros((1, 1), F32)
        for j in range(tm // min(tm, ROW_SUB)):
            r = pl.ds(j * min(tm, ROW_SUB), min(tm, ROW_SUB))
            yb = _rms(oa_ref[r, :], gm_ref[...])
            x1 = x_ref[r, :] + _dot(ya_ref[r, :].astype(BF16), wo_ref[0:A_WIDTH, :]) + _dot(yb.astype(BF16), wo_ref[A_WIDTH:, :])
            x1_ref[r, :] = x1
            h2 = _rms(x1, g2_ref[...]).astype(BF16)
            gate, up = _dot(h2, wg_ref[...]), _dot(h2, wu_ref[...])
            gate_ref[r, :] = gate.astype(BF16)
            up_ref[r, :] = up.astype(BF16)
            act = (gate * _sigmoid(gate) * up).astype(BF16)
            x2 = x1 + _dot(act, wd_ref[...])
            x2_ref[r, :] = x2
            err = _rms(x2, gf_ref[...]) - tgt_ref[r, :]
            part = part + 0.5 * jnp.sum(jnp.mean(err * err, axis=-1, keepdims=True), axis=0, keepdims=True)

        @pl.when(pl.program_id(0) == 0)
        def _():
            loss_ref[...] = jnp.zeros_like(loss_ref)

        loss_ref[...] += jnp.broadcast_to(part, loss_ref.shape)

    tok = lambda wd: pl.BlockSpec((tm, wd), lambda i: (i, 0))
    return pl.pallas_call(
        body, name="post_fwd", grid=(t // tm,),
        out_shape=[jax.ShapeDtypeStruct((t, D_MODEL), F32)] * 2 + [jax.ShapeDtypeStruct((t, D_FF), BF16)] * 2
        + [jax.ShapeDtypeStruct((1, 128), F32)],
        in_specs=[tok(D_MODEL), tok(A_WIDTH), tok(512), tok(D_MODEL), _const_spec((1, 512)), _const_spec((D_MODEL, D_MODEL)),
                  _const_spec((1, D_MODEL)), _const_spec((D_MODEL, D_FF)), _const_spec((D_MODEL, D_FF)),
                  _const_spec((D_FF, D_MODEL)), _const_spec((1, D_MODEL))],
        out_specs=[tok(D_MODEL), tok(D_MODEL), tok(D_FF), tok(D_FF), pl.BlockSpec((1, 128), lambda i: (0, 0))],
        compiler_params=_params(),
    )(x, ya, oattn, tgt, g_mla, w_out, g2, w_gate, w_up, w_down, g_fin)


def _post_bwd(x1, x2, gate_b, up_b, ya, oattn, tgt, g_mla, w_out, g2, w_gate, w_up, w_down, g_fin, tm):
    t = x1.shape[0]

    def body(x1_ref, x2_ref, gate_ref, up_ref, ya_ref, oa_ref, tgt_ref, gm_ref, wo_ref, g2_ref, wg_ref, wu_ref, wd_ref, gf_ref,
             dx1_ref, dya_ref, doa_ref, ycat_ref, dx1b_ref, h2_ref, dgate_ref, dup_ref, act_ref, dx2b_ref,
             dgm_ref, dg2_ref, dgf_ref):
        x1, x2 = x1_ref[...], x2_ref[...]
        dy = (_rms(x2, gf_ref[...]) - tgt_ref[...]) * (1.0 / D_MODEL)
        dx2, dgf = _rms_bwd(x2, gf_ref[...], dy)
        dx2b = dx2.astype(BF16)
        dx2b_ref[...] = dx2b
        h2_ref[...] = _rms(x1, g2_ref[...]).astype(BF16)
        gate, up = gate_ref[...].astype(F32), up_ref[...].astype(F32)
        sg = _sigmoid(gate)
        sl = gate * sg
        act_ref[...] = (sl * up).astype(BF16)
        dact = _dot_nt(dx2b, wd_ref[...])
        dup = (dact * sl).astype(BF16)
        dgate = (dact * up * (sg * (1.0 + gate * (1.0 - sg)))).astype(BF16)
        dup_ref[...] = dup
        dgate_ref[...] = dgate
        dh2 = _dot_nt(dgate, wg_ref[...]) + _dot_nt(dup, wu_ref[...])
        dx1n, dg2 = _rms_bwd(x1, g2_ref[...], dh2)
        dx1 = dx2 + dx1n
        dx1_ref[...] = dx1
        dx1b = dx1.astype(BF16)
        dx1b_ref[...] = dx1b
        oa = oa_ref[...]
        ycat_ref[:, 0:A_WIDTH] = ya_ref[...].astype(BF16)
        ycat_ref[:, A_WIDTH:] = _rms(oa, gm_ref[...]).astype(BF16)
        dya_ref[...] = _dot_nt(dx1b, wo_ref[0:A_WIDTH, :])
        doa, dgm = _rms_bwd(oa, gm_ref[...], _dot_nt(dx1b, wo_ref[A_WIDTH:, :]))
        doa_ref[...] = doa

        @pl.when(pl.program_id(0) == 0)
        def _():
            dgm_ref[...] = jnp.zeros_like(dgm_ref)
            dg2_ref[...] = jnp.zeros_like(dg2_ref)
            dgf_ref[...] = jnp.zeros_like(dgf_ref)

        dgm_ref[...] += dgm
        dg2_ref[...] += dg2
        dgf_ref[...] += dgf

    tok = lambda wd: pl.BlockSpec((tm, wd), lambda i: (i, 0))
    vec = lambda wd: pl.BlockSpec((1, wd), lambda i: (0, 0))
    sds = lambda wd, dt: jax.ShapeDtypeStruct((t, wd), dt)
    return pl.pallas_call(
        body, name="post_bwd", grid=(t // tm,),
        out_shape=[sds(D_MODEL, F32), sds(512, F32), sds(512, F32), sds(D_MODEL, BF16), sds(D_MODEL, BF16), sds(D_MODEL, BF16),
                   sds(D_FF, BF16), sds(D_FF, BF16), sds(D_FF, BF16), sds(D_MODEL, BF16),
                   jax.ShapeDtypeStruct((1, 512), F32), jax.ShapeDtypeStruct((1, D_MODEL), F32), jax.ShapeDtypeStruct((1, D_MODEL), F32)],
        in_specs=[tok(D_MODEL), tok(D_MODEL), tok(D_FF), tok(D_FF), tok(512), tok(512), tok(D_MODEL), _const_spec((1, 512)),
                  _const_spec((D_MODEL, D_MODEL)), _const_spec((1, D_MODEL)), _const_spec((D_MODEL, D_FF)),
                  _const_spec((D_MODEL, D_FF)), _const_spec((D_FF, D_MODEL)), _const_spec((1, D_MODEL))],
        out_specs=[tok(D_MODEL), tok(512), tok(512), tok(D_MODEL), tok(D_MODEL), tok(D_MODEL), tok(D_FF), tok(D_FF), tok(D_FF),
                   tok(D_MODEL), vec(512), vec(D_MODEL), vec(D_MODEL)],
        compiler_params=_params(),
    )(x1, x2, gate_b, up_b, ya, oattn, tgt, g_mla, w_out, g2, w_gate, w_up, w_down, g_fin)


def _matmul_tn(a, b, tn, tt, tag):
    t, k = a.shape
    n = b.shape[1]
    last = t // tt - 1

    def body(a_ref, b_ref, o_ref, acc_ref):
        part = _dot_tn(a_ref[...], b_ref[...])

        @pl.when(pl.program_id(1) == 0)
        def _():
            acc_ref[...] = part

        @pl.when(pl.program_id(1) > 0)
        def _():
            acc_ref[...] += part

        @pl.when(pl.program_id(1) == last)
        def _():
            o_ref[...] = acc_ref[...].astype(o_ref.dtype)

    return pl.pallas_call(
        body, name="wgrad_" + tag, grid=(n // tn, t // tt), out_shape=jax.ShapeDtypeStruct((k, n), BF16),
        in_specs=[pl.BlockSpec((tt, k), lambda j, i: (i, 0)), pl.BlockSpec((tt, tn), lambda j, i: (i, j))],
        out_specs=pl.BlockSpec((k, tn), lambda j, i: (0, j)), scratch_shapes=[pltpu.VMEM((k, tn), F32)],
        compiler_params=_params(),
    )(a, b)


def _mla_qkv_bwd(cq, ckv, g_qa, g_kva, w_q, w_kv, tables, dq, dk, dv, seq, tm):
    t = cq.shape[0]
    nblk = seq // tm

    def body(cq_ref, ckv_ref, gq_ref, gk_ref, wq_ref, wkv_ref, c_ref, sa_ref, sb_ref, dq_ref, dk_ref, dv_ref,
             dcq_ref, dckv_ref, dkr_ref, cqn_ref, dqf_ref, ckn_ref, dkv_ref, dgq_ref, dgk_ref):
        cos_t, sin_a, sin_b = c_ref[...], sa_ref[...], sb_ref[...]
        cqn_ref[...] = _rms(cq_ref[...], gq_ref[...]).astype(BF16)
        ckn_ref[...] = _rms(ckv_ref[...], gk_ref[...]).astype(BF16)
        dkr = jnp.zeros((tm, 128), F32)
        for h in range(B_HEADS):
            lo = h * QK_PAD
            dqf_ref[:, lo:lo + 128] = (dq_ref[:, lo:lo + 128] * ATTN_SCALE).astype(BF16)
            dqf_ref[:, lo + 128:lo + 256] = _rope_t(dq_ref[:, lo + 128:lo + 256] * ATTN_SCALE, cos_t, sin_a, sin_b).astype(BF16)
            dkv_ref[:, lo:lo + 128] = dk_ref[:, lo:lo + 128].astype(BF16)
            dkv_ref[:, lo + 128:lo + 256] = dv_ref[:, h * B_V:(h + 1) * B_V].astype(BF16)
            dkr = dkr + dk_ref[:, lo + 128:lo + 256]
        dkr_ref[...] = _rope_t(dkr, cos_t, sin_a, sin_b)
        dcq, dgq = _rms_bwd(cq_ref[...], gq_ref[...], _dot_nt(dqf_ref[...], wq_ref[...]))
        dckv, dgk = _rms_bwd(ckv_ref[...], gk_ref[...], _dot_nt(dkv_ref[...], wkv_ref[...]))
        dcq_ref[...] = dcq
        dckv_ref[...] = dckv

        @pl.when(pl.program_id(0) == 0)
        def _():
            dgq_ref[...] = jnp.zeros_like(dgq_ref)
            dgk_ref[...] = jnp.zeros_like(dgk_ref)

        dgq_ref[...] += dgq
        dgk_ref[...] += dgk

    tok = lambda wd: pl.BlockSpec((tm, wd), lambda i: (i, 0))
    vec = lambda wd: pl.BlockSpec((1, wd), lambda i: (0, 0))
    tab = pl.BlockSpec((tm, 128), lambda i: (i % nblk, 0))
    sds = lambda wd, dt: jax.ShapeDtypeStruct((t, wd), dt)
    return pl.pallas_call(
        body, name="mla_qkv_bwd", grid=(t // tm,),
        out_shape=[sds(Q_LORA, F32), sds(KV_LORA, F32), sds(128, F32), sds(Q_LORA, BF16), sds(1024, BF16), sds(KV_LORA, BF16),
                   sds(1024, BF16), jax.ShapeDtypeStruct((1, Q_LORA), F32), jax.ShapeDtypeStruct((1, KV_LORA), F32)],
        in_specs=[tok(Q_LORA), tok(KV_LORA), _const_spec((1, Q_LORA)), _const_spec((1, KV_LORA)),
                  _const_spec((Q_LORA, 1024)), _const_spec((KV_LORA, 1024)), tab, tab, tab,
                  tok(1024), tok(1024), tok(512)],
        out_specs=[tok(Q_LORA), tok(KV_LORA), tok(128), tok(Q_LORA), tok(1024), tok(KV_LORA), tok(1024),
                   vec(Q_LORA), vec(KV_LORA)],
        compiler_params=_params(),
    )(cq, ckv, g_qa, g_kva, w_q, w_kv, *tables, dq, dk, dv)


def _inproj_bwd(x, g1, w_in, dx1, pieces, tm):
    t = x.shape[0]
    counts = [len(p) for p in pieces]
    flat = [a for p in pieces for a in p]
    widths = [wd for wd, p in zip(IN_WIDTHS, pieces) for _ in p]

    def body(x_ref, g_ref, w_ref, dx1_ref, *refs):
        ins = refs[:len(flat)]
        dx_ref, h_ref, dp_ref, dg_ref = refs[len(flat):]
        xv = x_ref[...]
        h_ref[...] = _rms(xv, g_ref[...]).astype(BF16)
        off, j = 0, 0
        for wd, cnt in zip(IN_WIDTHS, counts):
            acc = ins[j][...]
            for jj in range(1, cnt):
                acc = acc + ins[j + jj][...]
            dp_ref[:, off:off + wd] = acc.astype(BF16)
            off += wd
            j += cnt
        dxn, dg = _rms_bwd(xv, g_ref[...], _dot_nt(dp_ref[...], w_ref[...]))
        dx_ref[...] = dx1_ref[...] + dxn

        @pl.when(pl.program_id(0) == 0)
        def _():
            dg_ref[...] = jnp.zeros_like(dg_ref)

        dg_ref[...] += dg

    tok = lambda wd: pl.BlockSpec((tm, wd), lambda i: (i, 0))
    return pl.pallas_call(
        body, name="inproj_bwd", grid=(t // tm,),
        out_shape=[jax.ShapeDtypeStruct((t, D_MODEL), F32), jax.ShapeDtypeStruct((t, D_MODEL), BF16),
                   jax.ShapeDtypeStruct((t, D_IN_PAD), BF16), jax.ShapeDtypeStruct((1, D_MODEL), F32)],
        in_specs=[tok(D_MODEL), _const_spec((1, D_MODEL)), _const_spec((D_MODEL, D_IN_PAD)), tok(D_MODEL)] + [tok(wd) for wd in widths],
        out_specs=[tok(D_MODEL), tok(D_MODEL), tok(D_IN_PAD), pl.BlockSpec((1, D_MODEL), lambda i: (0, 0))],
        compiler_params=_params(),
    )(x, g1, w_in, dx1, *flat)


def _cols_from_slots(g):
    n, r, cs = g.shape
    return g.transpose(1, 0, 2).reshape(r, n * cs)


def _cols_to_slots(full):
    r, c = full.shape
    return full.reshape(r, N_DEV, c // N_DEV).transpose(1, 0, 2)


def _arrange_w_in(w_in):
    return jnp.concatenate([w_in, jnp.zeros((D_MODEL, D_IN_PAD - D_IN), w_in.dtype)], axis=1)


def _arrange_w_q(w_q_b):
    q3 = w_q_b.reshape(Q_LORA, B_HEADS, B_NOPE + B_ROPE)
    pad = jnp.zeros((Q_LORA, B_HEADS, QK_PAD - B_NOPE - B_ROPE), w_q_b.dtype)
    return jnp.concatenate([q3, pad], axis=2).reshape(Q_LORA, B_HEADS * QK_PAD)


def _unarrange_w_q(d_q):
    return d_q.reshape(Q_LORA, B_HEADS, QK_PAD)[:, :, :B_NOPE + B_ROPE].reshape(Q_LORA, B_HEADS * (B_NOPE + B_ROPE))


def _step_core(x, loss_target, small_w, lb_full, early_full, late, seq, group, tiles, distributed):
    g1, g_hgrn, g_qa, g_kva, g_mla, g2, g_fin = small_w
    w_in, w_q, w_kv = _arrange_w_in(early_full[0]), _arrange_w_q(early_full[1]), early_full[2]
    nb = x.shape[0]
    t = nb * seq
    tm, tm_fwd, tq_f, tq_b, tt = tiles
    xt = x.reshape(t, D_MODEL)
    tgt = loss_target.reshape(t, D_MODEL)
    tables = _rope_tables(seq)

    hq, hi, zf, zb, hg, cq, ckv, kr = _inproj(xt, g1, w_in, tm_fwd)
    qcat, kcat, vv = _mla_qkv(cq, ckv, kr, g_qa, g_kva, w_q, w_kv, tables, seq, tm)
    if distributed:
        oattn, lse, *late_slots = _attn_fwd(qcat, kcat, vv, nb, seq, tq_f, gather=tuple(late))
    else:
        oattn, lse = _attn_fwd(qcat, kcat, vv, nb, seq, tq_f)
        late_slots = late
    w_out = late_slots[0].reshape(D_MODEL, D_MODEL)
    w_gate, w_up = _cols_from_slots(late_slots[1]), _cols_from_slots(late_slots[2])
    w_down = late_slots[3].reshape(D_FF, D_MODEL)
    lbl_f, lbl_b = lb_full[0], lb_full[1]
    o_f, o_b, save_f, save_b = _gla_fwd(hq, hi, (zf, zb), (lbl_f, lbl_b), nb, seq, group)
    ya = _gla_combine(o_f, o_b, hg, g_hgrn, tm)
    x1, x2, gate_b, up_b, loss_row = _post_fwd(xt, ya, oattn, tgt, g_mla, w_out, g2, w_gate, w_up, w_down, g_fin, tm_fwd)

    (dx1, d_ya, d_oattn, ycat_b, dx1_b, h2_b, dgate_b, dup_b, act_b, dx2_b, d_g_mla, d_g2, d_g_fin) = _post_bwd(
        x1, x2, gate_b, up_b, ya, oattn, tgt, g_mla, w_out, g2, w_gate, w_up, w_down, g_fin, tm)
    d_w_gate = _matmul_tn(h2_b, dgate_b, D_FF // 2, tt, "gate")
    d_w_up = _matmul_tn(h2_b, dup_b, D_FF // 2, tt, "up")
    d_w_down = _matmul_tn(act_b, dx2_b, 512, tt, "down")
    d_w_out = _matmul_tn(ycat_b, dx1_b, D_MODEL, tt, "out")
    late_g = [d_w_out.reshape(N_DEV, D_MODEL // N_DEV, D_MODEL), _cols_to_slots(d_w_gate), _cols_to_slots(d_w_up),
              d_w_down.reshape(N_DEV, D_FF // N_DEV, D_MODEL)]
    if distributed:
        dq, dk, dv, *late_g = _attn_bwd(qcat, kcat, vv, oattn, lse, d_oattn, nb, seq, tq_b, exchange=tuple(late_g))
    else:
        dq, dk, dv = _attn_bwd(qcat, kcat, vv, oattn, lse, d_oattn, nb, seq, tq_b)
    (d_cq, d_ckv, d_kr, cqn_b, dqf_b, ckn_b, dkv_b, d_g_qa, d_g_kva) = _mla_qkv_bwd(
        cq, ckv, g_qa, g_kva, w_q, w_kv, tables, dq, dk, dv, seq, tm)
    d_w_q = _matmul_tn(cqn_b, dqf_b, B_HEADS * QK_PAD, tt, "q_b")
    d_w_kv = _matmul_tn(ckn_b, dkv_b, B_HEADS * (B_NOPE + B_V), tt, "kv_b")
    d_o, d_hg, d_g_hgrn = _gla_combine_bwd(o_f, o_b, hg, g_hgrn, d_ya, tm)
    dq_f, dv_f, dz_f, dq_b, dv_b, dz_b, dl_f, dl_b = _gla_bwd(
        hq, hi, (zf, zb), (lbl_f, lbl_b), (save_f, save_b), d_o, nb, seq, group)
    grad_x, h1_b, dproj_b, d_g1 = _inproj_bwd(
        xt, g1, w_in, dx1, [[dq_f, dq_b], [dv_f, dv_b], [dz_f], [dz_b], [d_hg], [d_cq], [d_ckv], [d_kr]], tm)
    d_w_in_arr = _matmul_tn(h1_b, dproj_b, D_IN_PAD // 2, tt, "in")

    early_g = [_cols_to_slots(d_w_in_arr[:, :D_IN]), _cols_to_slots(_unarrange_w_q(d_w_q)), _cols_to_slots(d_w_kv)]
    d_lb = jnp.stack([jnp.sum(dl_f, axis=0), jnp.sum(dl_b, axis=0)], axis=0)
    small_grads = [d_g1, d_g_hgrn, d_g_qa, d_g_kva, d_g_mla, d_g2, d_g_fin]
    return loss_row, grad_x.reshape(nb, seq, D_MODEL), early_g, late_g, small_grads, d_lb


def kernel(x, norm1_g, w_in, lb_logits, hgrn_norm_g, q_a_norm_g, w_q_b, kv_a_norm_g, w_kv_b, mla_norm_g, w_out, norm2_g, w_gate, w_up, w_down, final_norm_g, loss_target, m_norm1_g, m_w_in, m_lb_logits, m_hgrn_norm_g, m_q_a_norm_g, m_w_q_b, m_kv_a_norm_g, m_w_kv_b, m_mla_norm_g, m_w_out, m_norm2_g, m_w_gate, m_w_up, m_w_down, m_final_norm_g, v_norm1_g, v_w_in, v_lb_logits, v_hgrn_norm_g, v_q_a_norm_g, v_w_q_b, v_kv_a_norm_g, v_w_kv_b, v_mla_norm_g, v_w_out, v_norm2_g, v_w_gate, v_w_up, v_w_down, v_final_norm_g):
    big_w = [w_in, w_q_b, w_kv_b, w_out, w_gate, w_up, w_down]
    big_m = [m_w_in, m_w_q_b, m_w_kv_b, m_w_out, m_w_gate, m_w_up, m_w_down]
    big_v = [v_w_in, v_w_q_b, v_w_kv_b, v_w_out, v_w_gate, v_w_up, v_w_down]
    small_w = [norm1_g, hgrn_norm_g, q_a_norm_g, kv_a_norm_g, mla_norm_g, norm2_g, final_norm_g]
    small_m = [m_norm1_g, m_hgrn_norm_g, m_q_a_norm_g, m_kv_a_norm_g, m_mla_norm_g, m_norm2_g, m_final_norm_g]
    small_v = [v_norm1_g, v_hgrn_norm_g, v_q_a_norm_g, v_kv_a_norm_g, v_mla_norm_g, v_norm2_g, v_final_norm_g]
    seq = x.shape[1]
    my_id = 4 * lax.axis_index("x") + 2 * lax.axis_index("y") + lax.axis_index("c")

    shard = lambda w: w[0].astype(BF16)
    g_in, g_q, g_kv, g_lb = _all_gather_call([shard(w_in), shard(w_q_b), shard(w_kv_b), lb_logits.reshape(4, 64)])
    early_full = (_cols_from_slots(g_in), _cols_from_slots(g_q), _cols_from_slots(g_kv))
    lb_full = g_lb.reshape(N_DEV, 2, 2, 64).transpose(1, 2, 0, 3).reshape(2, 2, 512)

    as_row = lambda a: a.reshape(1, -1)
    loss_row, grad_x, early_g, late_recv, small_g, d_lb = _step_core(
        x, loss_target, [as_row(s) for s in small_w], lb_full, early_full,
        [shard(w_out), shard(w_gate), shard(w_up), shard(w_down)], seq, min(8, seq // CHUNK),
        (256, 512, min(1024, seq), min(512, seq), min(2048, 2 * seq)), True)

    n_small = len(small_g)
    recv = _exchange_call(early_g + small_g + [d_lb.reshape(4, 512), loss_row], [True] * 3 + [False] * (n_small + 2))
    sums = _sum_slots_call(recv[3:])
    g_small = [g.reshape(s.shape) for g, s in zip(sums[:n_small], small_w)]
    g_lb_own = lax.dynamic_index_in_dim(sums[n_small].reshape(2, 2, N_DEV, 64), my_id, axis=2, keepdims=False)
    loss = sums[n_small + 1][0, 0]

    grads, deltas, new_ms, new_vs = {}, {}, {}, {}
    big_recv = dict(zip(["w_in", "w_q_b", "w_kv_b", "w_out", "w_gate", "w_up", "w_down"], list(recv[:3]) + list(late_recv)))
    for (name, _, _, _), w, m, v in zip(BIG, big_w, big_m, big_v):
        g, d, nm, nv = _adamw_recv(w[0], big_recv[name], m[0], v[0], name)
        grads[name], deltas[name], new_ms[name], new_vs[name] = g[None], d[None], nm[None], nv[None]
    lb_rows = lambda a: a.reshape(4, 64)
    d_s, nm_s, nv_s = _adamw_small(
        [as_row(a) for a in small_w] + [lb_rows(lb_logits)], [as_row(a) for a in g_small] + [lb_rows(g_lb_own)],
        [as_row(a) for a in small_m] + [lb_rows(m_lb_logits)], [as_row(a) for a in small_v] + [lb_rows(v_lb_logits)])
    for i, (s, (name, _)) in enumerate(zip(small_w + [lb_logits], SMALL + (("lb_logits", 0),))):
        grads[name] = (g_small + [g_lb_own])[i]
        deltas[name], new_ms[name], new_vs[name] = d_s[i].reshape(s.shape), nm_s[i].reshape(s.shape), nv_s[i].reshape(s.shape)

    order = ["norm1_g", "w_in", "lb_logits", "hgrn_norm_g", "q_a_norm_g", "w_q_b", "kv_a_norm_g", "w_kv_b", "mla_norm_g",
             "w_out", "norm2_g", "w_gate", "w_up", "w_down", "final_norm_g"]
    return (loss, grad_x, *[grads[n] for n in order], *[deltas[n] for n in order],
            *[new_ms[n] for n in order], *[new_vs[n] for n in order])
```

```python
import functools
import math

import jax
import jax.numpy as jnp
from jax import lax
from jax.experimental import pallas as pl
from jax.experimental.pallas import tpu as pltpu

F32 = jnp.float32
BF16 = jnp.bfloat16

N_DEV = 8
D_MODEL = 1024
D_FF = 2816
A_WIDTH = 512
HEAD_PAIR = 128
CHUNK = 64
B_HEADS = 4
B_NOPE = 128
B_ROPE = 64
B_V = 128
QK_PAD = 256
Q_LORA = 384
KV_LORA = 256
D_IN = 3264
D_IN_PAD = 3328
IN_WIDTHS = (512, 512, 512, 512, 512, Q_LORA, KV_LORA, 128)
ROPE_THETA = 10000.0
EPS = 1e-6
ATTN_SCALE = (B_NOPE + B_ROPE) ** -0.5
ATTN_SUB = 256
ATTN_SUB_BWD = 256
ROW_SUB = 256
ADAM_LR, ADAM_B1, ADAM_B2, ADAM_EPS, ADAM_WD, ADAM_STEP = 0.001, 0.9, 0.999, 1e-08, 0.01, 10
VMEM_LIMIT = 60 * 1024 * 1024
MESH = pl.DeviceIdType.MESH

BIG = (("w_in", 1024, D_IN, 1), ("w_q_b", Q_LORA, 768, 1), ("w_kv_b", KV_LORA, 1024, 1), ("w_out", 1024, 1024, 0),
       ("w_gate", 1024, D_FF, 1), ("w_up", 1024, D_FF, 1), ("w_down", D_FF, 1024, 0))
SMALL = (("norm1_g", 1024), ("hgrn_norm_g", 512), ("q_a_norm_g", 384), ("kv_a_norm_g", 256), ("mla_norm_g", 512),
         ("norm2_g", 1024), ("final_norm_g", 1024))


def _params(**kw):
    return pltpu.CompilerParams(vmem_limit_bytes=VMEM_LIMIT, **kw)


def _const_spec(shape):
    return pl.BlockSpec(shape, lambda *_: (0,) * len(shape), pipeline_mode=pl.Buffered(1))


def _dot(a, b):
    return jnp.dot(a, b, preferred_element_type=F32)


def _dot_nt(a, b):
    return lax.dot_general(a, b, (((1,), (1,)), ((), ())), preferred_element_type=F32)


def _dot_tn(a, b):
    return lax.dot_general(a, b, (((0,), (0,)), ((), ())), preferred_element_type=F32)


@jax.custom_vjp
def _mm(a, b):
    return _dot(a.astype(BF16), b.astype(BF16))


def _mm_fwd(a, b):
    return _mm(a, b), (a, b)


def _mm_bwd(res, g):
    a, b = res
    gb = g.astype(BF16)
    return _dot_nt(gb, b.astype(BF16)), _dot_tn(a.astype(BF16), gb)


_mm.defvjp(_mm_fwd, _mm_bwd)


@jax.custom_vjp
def _mm_nt(a, b):
    return _dot_nt(a.astype(BF16), b.astype(BF16))


def _mm_nt_fwd(a, b):
    return _mm_nt(a, b), (a, b)


def _mm_nt_bwd(res, g):
    a, b = res
    gb = g.astype(BF16)
    return _dot(gb, b.astype(BF16)), _dot_tn(gb, a.astype(BF16))


_mm_nt.defvjp(_mm_nt_fwd, _mm_nt_bwd)


@jax.custom_vjp
def _mm_tn(a, b):
    return _dot_tn(a.astype(BF16), b.astype(BF16))


def _mm_tn_fwd(a, b):
    return _mm_tn(a, b), (a, b)


def _mm_tn_bwd(res, g):
    a, b = res
    gb = g.astype(BF16)
    return _dot_nt(b.astype(BF16), gb), _dot(a.astype(BF16), gb)


_mm_tn.defvjp(_mm_tn_fwd, _mm_tn_bwd)


def _dot_exact_rhs(a, m):
    hi = a.astype(BF16)
    lo = (a - hi.astype(F32)).astype(BF16)
    return _dot(hi, m) + _dot(lo, m)


@jax.custom_vjp
def _group_mean(a, m):
    return _dot_exact_rhs(a, m)


def _group_mean_fwd(a, m):
    return _group_mean(a, m), m


def _group_mean_bwd(m, g):
    return _dot_exact_rhs(g, m), jnp.zeros_like(m)


_group_mean.defvjp(_group_mean_fwd, _group_mean_bwd)


def _roll_rows(a, shift):
    return pltpu.roll(a, shift, 0)


def _cumsum_rows_raw(a, reverse):
    n = a.shape[0]
    row = lax.broadcasted_iota(jnp.int32, a.shape, 0)
    s = 1
    while s < n:
        if reverse:
            a = a + jnp.where(row < n - s, _roll_rows(a, n - s), 0.0)
        else:
            a = a + jnp.where(row >= s, _roll_rows(a, s), 0.0)
        s *= 2
    return a


@functools.partial(jax.custom_vjp, nondiff_argnums=(1,))
def _cumsum_rows(a, reverse):
    return _cumsum_rows_raw(a, reverse)


def _cumsum_rows_fwd(a, reverse):
    return _cumsum_rows_raw(a, reverse), None


def _cumsum_rows_bwd(reverse, _, g):
    return (_cumsum_rows_raw(g, not reverse),)


_cumsum_rows.defvjp(_cumsum_rows_fwd, _cumsum_rows_bwd)


def _rms(x, g):
    r = lax.rsqrt(jnp.mean(x * x, axis=-1, keepdims=True) + EPS)
    return x * r * g


def _rms_bwd(x, g, dy):
    r = lax.rsqrt(jnp.mean(x * x, axis=-1, keepdims=True) + EPS)
    xh = x * r
    dg = jnp.sum(dy * xh, axis=0, keepdims=True)
    dxh = dy * g
    dx = r * (dxh - xh * jnp.mean(dxh * xh, axis=-1, keepdims=True))
    return dx, dg


def _sigmoid(a):
    return jax.nn.sigmoid(a)


def _mesh_place():
    x, y, c = lax.axis_index("x"), lax.axis_index("y"), lax.axis_index("c")
    return x, y, c


def _dev_index(p):
    return 4 * p[0] + 2 * p[1] + p[2]


def _comm_sems(n):
    return [pltpu.SemaphoreType.DMA((n, 7)), pltpu.SemaphoreType.DMA((n, 7)), pltpu.SemaphoreType.DMA((n,))]


def _gather_protocol(ins, outs, send_sems, recv_sems, local_sems):
    n = len(ins)
    x, y, c = _mesh_place()
    me, sibling = (x, y, c), (x, y, 1 - c)
    chips = [(1 - x, y), (x, 1 - y), (1 - x, 1 - y)]

    def copy(a, k, block, to, src=None):
        slot = outs[a].at[_dev_index(block)]
        return pltpu.make_async_remote_copy(
            src_ref=slot if src is None else src, dst_ref=slot,
            send_sem=send_sems.at[a, k], recv_sem=recv_sems.at[a, k], device_id=to, device_id_type=MESH)

    def mine(a):
        return pltpu.make_async_copy(ins[a], outs[a].at[_dev_index(me)], local_sems.at[a])

    def first(a):
        return [copy(a, 0, me, sibling, src=ins[a])] + [copy(a, 1 + j, me, (*chip, c), src=ins[a]) for j, chip in enumerate(chips)]

    def start():
        for a in range(n):
            mine(a).start()
            for cp in first(a):
                cp.start()

    def forward():
        for a in range(n):
            for j, chip in enumerate(chips):
                copy(a, 1 + j, (*chip, c), me).wait_recv()
                copy(a, 4 + j, (*chip, c), sibling).start()

    def finish():
        for a in range(n):
            copy(a, 0, sibling, me).wait_recv()
            for j, chip in enumerate(chips):
                copy(a, 4 + j, (*chip, 1 - c), me).wait_recv()
        for a in range(n):
            mine(a).wait()
            for cp in first(a):
                cp.wait_send()
            for j, chip in enumerate(chips):
                copy(a, 4 + j, (*chip, c), sibling).wait_send()

    return start, forward, finish


def _exchange_protocol(ins, outs, scatter, send_sems, recv_sems, local_sems):
    n = len(ins)
    x, y, c = _mesh_place()
    me = (x, y, c)
    my_id = _dev_index(me)
    rels = [(dx, dy, dc) for dx in (0, 1) for dy in (0, 1) for dc in (0, 1)][1:]

    def peer_of(rel):
        return tuple(1 - v if d else v for v, d in zip(me, rel))

    def src(a, dev):
        return ins[a].at[dev] if scatter[a] else ins[a]

    def send(a, k):
        peer = peer_of(rels[k])
        return pltpu.make_async_remote_copy(
            src_ref=src(a, _dev_index(peer)), dst_ref=outs[a].at[my_id],
            send_sem=send_sems.at[a, k], recv_sem=recv_sems.at[a, k], device_id=peer, device_id_type=MESH)

    def arrival(a, k):
        peer = peer_of(rels[k])
        return pltpu.make_async_remote_copy(
            src_ref=src(a, my_id), dst_ref=outs[a].at[_dev_index(peer)],
            send_sem=send_sems.at[a, k], recv_sem=recv_sems.at[a, k], device_id=peer, device_id_type=MESH)

    def own(a):
        return pltpu.make_async_copy(src(a, my_id), outs[a].at[my_id], local_sems.at[a])

    def start():
        for a in range(n):
            own(a).start()
            for k in range(7):
                send(a, k).start()

    def finish():
        for a in range(n):
            for k in range(7):
                arrival(a, k).wait_recv()
        for a in range(n):
            for k in range(7):
                send(a, k).wait_send()
            own(a).wait()

    return start, finish


def _slot_shapes(blocks, scatter=None):
    return [jax.ShapeDtypeStruct(b.shape if (scatter and scatter[a]) else (N_DEV,) + b.shape, b.dtype) for a, b in enumerate(blocks)]


def _all_gather_call(blocks):
    n = len(blocks)

    def body(*refs):
        start, forward, finish = _gather_protocol(refs[:n], refs[n:2 * n], *refs[2 * n:])
        start()
        forward()
        finish()

    any_spec = pl.BlockSpec(memory_space=pl.ANY)
    return pl.pallas_call(
        body, name="weights_all_gather", out_shape=_slot_shapes(blocks),
        in_specs=[any_spec] * n, out_specs=[any_spec] * n, scratch_shapes=_comm_sems(n),
    )(*blocks)


def _exchange_call(blocks, scatter):
    n = len(blocks)

    def body(*refs):
        start, finish = _exchange_protocol(refs[:n], refs[n:2 * n], scatter, *refs[2 * n:])
        start()
        finish()

    any_spec = pl.BlockSpec(memory_space=pl.ANY)
    return pl.pallas_call(
        body, name="grad_exchange", out_shape=_slot_shapes(blocks, scatter),
        in_specs=[any_spec] * n, out_specs=[any_spec] * n, scratch_shapes=_comm_sems(n),
    )(*blocks)


def _sum_slots_call(recvs):
    n = len(recvs)

    def body(*refs):
        for in_ref, out_ref in zip(refs[:n], refs[n:]):
            acc = in_ref[0]
            for j in range(1, N_DEV):
                acc = acc + in_ref[j]
            out_ref[...] = acc

    return pl.pallas_call(
        body, name="small_grad_sum", out_shape=[jax.ShapeDtypeStruct(r.shape[1:], F32) for r in recvs],
        compiler_params=_params(),
    )(*recvs)


def _adam_update(w, g, m, v):
    nm = ADAM_B1 * m + (1.0 - ADAM_B1) * g
    nv = ADAM_B2 * v + (1.0 - ADAM_B2) * (g * g)
    bc1 = 1.0 - ADAM_B1 ** ADAM_STEP
    bc2 = 1.0 - ADAM_B2 ** ADAM_STEP
    return -ADAM_LR * ((nm / bc1) / (jnp.sqrt(nv / bc2) + ADAM_EPS) + ADAM_WD * w), nm, nv


def _adamw_recv(w, recv, m, v, tag):
    r, c = w.shape
    tr = r
    for cand in (512, 256, 128):
        if r > cand and r % cand == 0:
            tr = cand
            break

    def body(w_ref, r_ref, m_ref, v_ref, g_ref, d_ref, nm_ref, nv_ref):
        g = r_ref[0].astype(F32)
        for j in range(1, N_DEV):
            g = g + r_ref[j].astype(F32)
        g_ref[...] = g
        d_ref[...], nm_ref[...], nv_ref[...] = _adam_update(w_ref[...], g, m_ref[...], v_ref[...])

    spec = pl.BlockSpec((tr, c), lambda i: (i, 0))
    return pl.pallas_call(
        body, name="adamw_" + tag, out_shape=[jax.ShapeDtypeStruct(w.shape, F32)] * 4, grid=(r // tr,),
        in_specs=[spec, pl.BlockSpec((N_DEV, tr, c), lambda i: (0, i, 0)), spec, spec], out_specs=[spec] * 4,
        compiler_params=_params(),
    )(w, recv, m, v)


def _adamw_small(ws, gs, ms, vs):
    n = len(ws)

    def body(*refs):
        ins, outs = refs[:4 * n], refs[4 * n:]
        for a in range(n):
            d, nm, nv = _adam_update(ins[a][...], ins[n + a][...], ins[2 * n + a][...], ins[3 * n + a][...])
            outs[a][...], outs[n + a][...], outs[2 * n + a][...] = d, nm, nv

    out = pl.pallas_call(
        body, name="adamw_small", out_shape=[jax.ShapeDtypeStruct(w.shape, F32) for w in ws] * 3, compiler_params=_params(),
    )(*ws, *gs, *ms, *vs)
    return out[:n], out[n:2 * n], out[2 * n:]


def _tile(t, want):
    return want if t % want == 0 else t


def _inproj(x, g1, w_in, tm):
    t = x.shape[0]

    def body(x_ref, g_ref, w_ref, *outs):
        for j in range(tm // min(tm, ROW_SUB)):
            r = pl.ds(j * min(tm, ROW_SUB), min(tm, ROW_SUB))
            h = _rms(x_ref[r, :], g_ref[...]).astype(BF16)
            off = 0
            for o_ref, wd in zip(outs, IN_WIDTHS):
                o_ref[r, :] = _dot(h, w_ref[:, off:off + wd])
                off += wd

    return pl.pallas_call(
        body, name="inproj_fwd", grid=(t // tm,),
        out_shape=[jax.ShapeDtypeStruct((t, wd), F32) for wd in IN_WIDTHS],
        in_specs=[pl.BlockSpec((tm, D_MODEL), lambda i: (i, 0)), _const_spec((1, D_MODEL)), _const_spec((D_MODEL, D_IN_PAD))],
        out_specs=[pl.BlockSpec((tm, wd), lambda i: (i, 0)) for wd in IN_WIDTHS],
        compiler_params=_params(),
    )(x, g1, w_in)


def _rope_tables(seq):
    inv = 1.0 / (ROPE_THETA ** (jnp.arange(0, B_ROPE, 2, dtype=F32) / B_ROPE))
    ang = jnp.arange(seq, dtype=F32)[:, None] * inv[None, :]
    cos, sin = jnp.cos(ang), jnp.sin(ang)
    z32, z64 = jnp.zeros_like(cos), jnp.zeros((seq, 64), F32)
    cos_t = jnp.concatenate([cos, cos, z64], axis=1)
    sin_a = jnp.concatenate([-sin, z32, z64], axis=1)
    sin_b = jnp.concatenate([z32, sin, z64], axis=1)
    return cos_t, sin_a, sin_b


def _rope(t, cos_t, sin_a, sin_b):
    return t * cos_t + pltpu.roll(t, 96, 1) * sin_a + pltpu.roll(t, 32, 1) * sin_b


def _rope_t(d, cos_t, sin_a, sin_b):
    return d * cos_t + pltpu.roll(d * sin_a, 32, 1) + pltpu.roll(d * sin_b, 96, 1)


def _mla_qkv(cq, ckv, kr, g_qa, g_kva, w_q, w_kv, tables, seq, tm):
    t = cq.shape[0]
    nblk = seq // tm

    def body(cq_ref, ckv_ref, kr_ref, gq_ref, gk_ref, wq_ref, wkv_ref, c_ref, sa_ref, sb_ref, q_out, k_out, v_out):
        cos_t, sin_a, sin_b = c_ref[...], sa_ref[...], sb_ref[...]
        cqn = _rms(cq_ref[...], gq_ref[...]).astype(BF16)
        ckn = _rms(ckv_ref[...], gk_ref[...]).astype(BF16)
        kr_rot = _rope(kr_ref[...], cos_t, sin_a, sin_b).astype(BF16)
        for h in range(B_HEADS):
            lo = h * QK_PAD
            q_out[:, lo:lo + 128] = (_dot(cqn, wq_ref[:, lo:lo + 128]) * ATTN_SCALE).astype(BF16)
            qr = _rope(_dot(cqn, wq_ref[:, lo + 128:lo + 256]), cos_t, sin_a, sin_b)
            q_out[:, lo + 128:lo + 256] = (qr * ATTN_SCALE).astype(BF16)
            k_out[:, lo:lo + 128] = _dot(ckn, wkv_ref[:, lo:lo + 128]).astype(BF16)
            k_out[:, lo + 128:lo + 256] = kr_rot
            v_out[:, h * B_V:(h + 1) * B_V] = _dot(ckn, wkv_ref[:, lo + 128:lo + 256]).astype(BF16)

    tok = lambda wd: pl.BlockSpec((tm, wd), lambda i: (i, 0))
    tab = pl.BlockSpec((tm, 128), lambda i: (i % nblk, 0))
    return pl.pallas_call(
        body, name="mla_qkv_fwd", grid=(t // tm,),
        out_shape=[jax.ShapeDtypeStruct((t, B_HEADS * QK_PAD), BF16), jax.ShapeDtypeStruct((t, B_HEADS * QK_PAD), BF16),
                   jax.ShapeDtypeStruct((t, B_HEADS * B_V), BF16)],
        in_specs=[tok(Q_LORA), tok(KV_LORA), tok(128), _const_spec((1, Q_LORA)), _const_spec((1, KV_LORA)),
                  _const_spec((Q_LORA, B_HEADS * QK_PAD)), _const_spec((KV_LORA, 1024)), tab, tab, tab],
        out_specs=[tok(B_HEADS * QK_PAD), tok(B_HEADS * QK_PAD), tok(B_HEADS * B_V)],
        compiler_params=_params(),
    )(cq, ckv, kr, g_qa, g_kva, w_q, w_kv, *tables)


def _step_index(nq):
    return (pl.program_id(0) * B_HEADS + pl.program_id(1)) * nq + pl.program_id(2)


def _attn_fwd(qcat, kcat, v, nb, seq, tq, gather=()):
    t = qcat.shape[0]
    nq = seq // tq
    ng = len(gather)
    steps = nb * B_HEADS * nq

    def body(q_ref, k_ref, v_ref, *rest):
        o_ref, lse_ref = rest[ng:ng + 2]
        if ng:
            start, forward, finish = _gather_protocol(rest[:ng], rest[ng + 2:2 * ng + 2], *rest[2 * ng + 2:])
            pl.when(_step_index(nq) == 0)(start)
            pl.when(_step_index(nq) == (3 * steps) // 4)(forward)
        for j in range(tq // ATTN_SUB):
            r = pl.ds(j * ATTN_SUB, ATTN_SUB)
            s = _dot_nt(q_ref[r, :], k_ref[...])
            m = jnp.max(s, axis=-1, keepdims=True)
            p = jnp.exp(s - m)
            l = jnp.sum(p, axis=-1, keepdims=True)
            o_ref[r, :] = _dot(p.astype(BF16), v_ref[...]) / l
            lse_ref[0, r, :] = m + jnp.log(l)
        if ng:
            pl.when(_step_index(nq) == steps - 1)(finish)

    any_spec = pl.BlockSpec(memory_space=pl.ANY)
    return pl.pallas_call(
        body, name="attn_fwd", grid=(nb, B_HEADS, nq),
        out_shape=[jax.ShapeDtypeStruct((t, B_HEADS * B_V), F32), jax.ShapeDtypeStruct((B_HEADS, t, 1), F32)] + _slot_shapes(gather),
        in_specs=[pl.BlockSpec((tq, QK_PAD), lambda b, h, i: (b * nq + i, h)),
                  pl.BlockSpec((seq, QK_PAD), lambda b, h, i: (b, h)),
                  pl.BlockSpec((seq, B_V), lambda b, h, i: (b, h))] + [any_spec] * ng,
        out_specs=[pl.BlockSpec((tq, B_V), lambda b, h, i: (b * nq + i, h)),
                   pl.BlockSpec((1, tq, 1), lambda b, h, i: (h, b * nq + i, 0))] + [any_spec] * ng,
        scratch_shapes=_comm_sems(ng) if ng else [],
        compiler_params=_params(),
    )(qcat, kcat, v, *gather)


def _attn_bwd(qcat, kcat, v, o, lse, do, nb, seq, tq, exchange=()):
    t = qcat.shape[0]
    nq = seq // tq
    ne = len(exchange)
    steps = nb * B_HEADS * nq

    def body(q_ref, k_ref, v_ref, o_ref, lse_ref, do_ref, *rest):
        dq_ref, dk_ref, dv_ref = rest[ne:ne + 3]
        if ne:
            start, finish = _exchange_protocol(rest[:ne], rest[ne + 3:2 * ne + 3], [True] * ne, *rest[2 * ne + 3:])
            pl.when(_step_index(nq) == 0)(start)

        @pl.when(pl.program_id(2) == 0)
        def _():
            dv_ref[...] = jnp.zeros_like(dv_ref)
            dk_ref[...] = jnp.zeros_like(dk_ref)

        for j in range(tq // ATTN_SUB_BWD):
            r = pl.ds(j * ATTN_SUB_BWD, ATTN_SUB_BWD)
            q, k = q_ref[r, :], k_ref[...]
            do_f = do_ref[r, :]
            delta = jnp.sum(do_f * o_ref[r, :], axis=-1, keepdims=True)
            dob = do_f.astype(BF16)
            p = jnp.exp(_dot_nt(q, k) - lse_ref[0, r, :])
            ds = (p * (_dot_nt(dob, v_ref[...]) - delta)).astype(BF16)
            dq_ref[r, :] = _dot(ds, k).astype(dq_ref.dtype)
            dv_ref[...] += _dot_tn(p.astype(BF16), dob)
            dk_ref[...] += _dot_tn(ds, q)
        if ne:
            pl.when(_step_index(nq) == steps - 1)(finish)

    qspec = lambda wd: pl.BlockSpec((tq, wd), lambda b, h, i: (b * nq + i, h))
    kspec = lambda wd: pl.BlockSpec((seq, wd), lambda b, h, i: (b, h))
    any_spec = pl.BlockSpec(memory_space=pl.ANY)
    return pl.pallas_call(
        body, name="attn_bwd", grid=(nb, B_HEADS, nq),
        out_shape=[jax.ShapeDtypeStruct((t, B_HEADS * QK_PAD), BF16), jax.ShapeDtypeStruct((t, B_HEADS * QK_PAD), F32),
                   jax.ShapeDtypeStruct((t, B_HEADS * B_V), F32)] + _slot_shapes(exchange, [True] * ne),
        in_specs=[qspec(QK_PAD), kspec(QK_PAD), kspec(B_V), qspec(B_V),
                  pl.BlockSpec((1, tq, 1), lambda b, h, i: (h, b * nq + i, 0)), qspec(B_V)] + [any_spec] * ne,
        out_specs=[qspec(QK_PAD), kspec(QK_PAD), kspec(B_V)] + [any_spec] * ne,
        scratch_shapes=_comm_sems(ne) if ne else [],
        compiler_params=_params(),
    )(qcat, kcat, v, o, lse, do, *exchange)


def _gla_consts(reverse):
    row = lax.broadcasted_iota(jnp.int32, (CHUNK, CHUNK), 0)
    col = lax.broadcasted_iota(jnp.int32, (CHUNK, CHUNK), 1)
    causal = (row <= col) if reverse else (row >= col)
    lane = lax.broadcasted_iota(jnp.int32, (1, HEAD_PAIR), 1)
    m0 = (lane < 64).astype(F32)
    m1 = 1.0 - m0
    r2 = lax.broadcasted_iota(jnp.int32, (HEAD_PAIR, HEAD_PAIR), 0)
    c2 = lax.broadcasted_iota(jnp.int32, (HEAD_PAIR, HEAD_PAIR), 1)
    same_head = ((r2 < 64) == (c2 < 64)).astype(F32)
    return causal, m0, m1, same_head


def _gla_chunk(hq, hi, z, l0, l1, st, consts, reverse):
    q_dec, k_inv, k_end, decay = _gla_gates(hq, z, l0, l1, reverse)
    o, st_new = _gla_state(q_dec, st, decay, _gla_increment(hi, k_end, consts))
    return o + _gla_intra(q_dec, k_inv, hi, consts), st_new


def _gla_gates(hq, z, l0, l1, reverse):
    mx = jnp.maximum(l0, l1)
    e0, e1 = jnp.exp(l0 - mx), jnp.exp(l1 - mx)
    lb = e0 / (e0 + e1)
    q = hq * _sigmoid(hq)
    sz = _sigmoid(z)
    log_f = jnp.log(lb + (1.0 - lb) * sz)
    k = (1.0 - lb) * (1.0 - sz)
    cum = _cumsum_rows(log_f, reverse)
    decay = jnp.exp(jnp.sum(log_f, axis=0, keepdims=True))
    k_inv = k * jnp.exp(-cum)
    return q * jnp.exp(cum), k_inv, k_inv * decay, decay


def _gla_intra(q_dec, k_inv, hi, consts):
    causal, m0, m1, _ = consts
    o = None
    for mh in (m0, m1):
        s = jnp.where(causal, _mm_nt(q_dec * mh, k_inv), 0.0)
        part = _mm(s, hi) * mh
        o = part if o is None else o + part
    return o


def _gla_increment(hi, k_end, consts):
    return _mm_tn(hi, k_end) * consts[3]


def _gla_state(q_dec, st, decay, inc):
    return _mm_nt(q_dec, st), st * decay + inc


GLA_DIRS = (False, True)
GLA_BATCH_FWD = 8
GLA_BATCH_BWD = 4


def _gla_fwd(hq, hi, zs, lbls, nb, seq, group):
    t = hq.shape[0]
    rows = group * CHUNK
    nblk = seq // rows
    n_chunks = seq // CHUNK
    nd = len(GLA_DIRS)

    def body(*refs):
        ins, outs, st_refs = refs[:4 * nd], refs[4 * nd:6 * nd], refs[6 * nd:]
        @pl.when(pl.program_id(2) == 0)
        def _():
            for st_ref in st_refs:
                st_ref[...] = jnp.zeros_like(st_ref)

        consts = [_gla_consts(rev) for rev in GLA_DIRS]
        work = [(d, rev, group - 1 - cc if rev else cc) for cc in range(group) for d, rev in enumerate(GLA_DIRS)]
        rows_of = lambda c: pl.ds(c * CHUNK, CHUNK)
        sts = [st_ref[...] for st_ref in st_refs]
        for w0 in range(0, len(work), GLA_BATCH_FWD):
            batch = work[w0:w0 + GLA_BATCH_FWD]
            gates, intra, incs = {}, {}, {}
            for d, rev, c in batch:
                hq_ref, _, z_ref, lbl_ref = ins[4 * d:4 * d + 4]
                gates[d, c] = _gla_gates(hq_ref[rows_of(c), :], z_ref[rows_of(c), :], lbl_ref[0:1, :], lbl_ref[1:2, :], rev)
            for d, rev, c in batch:
                hi_c = ins[4 * d + 1][rows_of(c), :]
                intra[d, c] = _gla_intra(gates[d, c][0], gates[d, c][1], hi_c, consts[d])
                incs[d, c] = _gla_increment(hi_c, gates[d, c][2], consts[d])
            for d, rev, c in batch:
                outs[nd + d][0, 0, c] = sts[d]
                o_state, sts[d] = _gla_state(gates[d, c][0], sts[d], gates[d, c][3], incs[d, c])
                outs[d][rows_of(c), :] = intra[d, c] + o_state
        for st_ref, st in zip(st_refs, sts):
            st_ref[...] = st

    def tb(rev):
        return (lambda i: nblk - 1 - i) if rev else (lambda i: i)

    tok = lambda rev: pl.BlockSpec((rows, HEAD_PAIR), lambda b, p, i: (b * nblk + tb(rev)(i), p))
    lspec = pl.BlockSpec((2, HEAD_PAIR), lambda b, p, i: (0, p))
    sspec = lambda rev: pl.BlockSpec((1, 1, group, HEAD_PAIR, HEAD_PAIR), lambda b, p, i: (b, p, tb(rev)(i), 0, 0))
    args, in_specs = [], []
    for d, rev in enumerate(GLA_DIRS):
        args += [hq, hi, zs[d], lbls[d]]
        in_specs += [tok(rev), tok(rev), tok(rev), lspec]
    return pl.pallas_call(
        body, name="gla_fwd", grid=(nb, 4, nblk),
        out_shape=[jax.ShapeDtypeStruct((t, A_WIDTH), F32)] * nd
        + [jax.ShapeDtypeStruct((nb, 4, n_chunks, HEAD_PAIR, HEAD_PAIR), F32)] * nd,
        in_specs=in_specs, out_specs=[tok(rev) for rev in GLA_DIRS] + [sspec(rev) for rev in GLA_DIRS],
        scratch_shapes=[pltpu.VMEM((HEAD_PAIR, HEAD_PAIR), F32)] * nd,
        compiler_params=_params(),
    )(*args)


def _gla_bwd(hq, hi, zs, lbls, saved, do, nb, seq, group):
    t = hq.shape[0]
    rows = group * CHUNK
    nblk = seq // rows
    nd = len(GLA_DIRS)

    def body(*refs):
        ins, outs, dst_refs = refs[:6 * nd], refs[6 * nd:10 * nd], refs[10 * nd:]
        dl_refs = outs[3 * nd:]

        @pl.when(pl.program_id(2) == 0)
        def _():
            for dst_ref, dl_ref in zip(dst_refs, dl_refs):
                dst_ref[...] = jnp.zeros_like(dst_ref)
                dl_ref[...] = jnp.zeros_like(dl_ref)

        consts = [_gla_consts(rev) for rev in GLA_DIRS]
        dsts = [dst_ref[...] for dst_ref in dst_refs]
        dls = [[jnp.zeros((1, HEAD_PAIR), F32), jnp.zeros((1, HEAD_PAIR), F32)] for _ in GLA_DIRS]
        work = [(d, rev, cc if rev else group - 1 - cc) for cc in range(group) for d, rev in enumerate(GLA_DIRS)]
        for w0 in range(0, len(work), GLA_BATCH_BWD):
            vjps = {}
            for d, rev, c in work[w0:w0 + GLA_BATCH_BWD]:
                hq_ref, hi_ref, z_ref, lbl_ref, save_ref, _ = ins[6 * d:6 * d + 6]
                r = pl.ds(c * CHUNK, CHUNK)
                fn = functools.partial(_gla_chunk, consts=consts[d], reverse=rev)
                _, vjps[d, c] = jax.vjp(fn, hq_ref[r, :], hi_ref[r, :], z_ref[r, :], lbl_ref[0:1, :], lbl_ref[1:2, :], save_ref[0, 0, c])
            for d, rev, c in work[w0:w0 + GLA_BATCH_BWD]:
                dq_ref, dv_ref, dz_ref = outs[3 * d:3 * d + 3]
                r = pl.ds(c * CHUNK, CHUNK)
                d_hq, d_hi, d_z, d_l0, d_l1, dsts[d] = vjps[d, c]((ins[6 * d + 5][r, :], dsts[d]))
                dq_ref[r, :] = d_hq.astype(dq_ref.dtype)
                dv_ref[r, :] = d_hi.astype(dv_ref.dtype)
                dz_ref[r, :] = d_z.astype(dz_ref.dtype)
                dls[d] = [dls[d][0] + d_l0, dls[d][1] + d_l1]
        for d in range(nd):
            dst_refs[d][...] = dsts[d]
            dl_refs[d][0, 0:1, :] += dls[d][0]
            dl_refs[d][0, 1:2, :] += dls[d][1]

    def tb(rev):
        return (lambda i: i) if rev else (lambda i: nblk - 1 - i)

    tok = lambda rev: pl.BlockSpec((rows, HEAD_PAIR), lambda b, p, i: (b * nblk + tb(rev)(i), p))
    lspec = pl.BlockSpec((2, HEAD_PAIR), lambda b, p, i: (0, p))
    sspec = lambda rev: pl.BlockSpec((1, 1, group, HEAD_PAIR, HEAD_PAIR), lambda b, p, i: (b, p, tb(rev)(i), 0, 0))
    args, in_specs, out_specs = [], [], []
    for d, rev in enumerate(GLA_DIRS):
        args += [hq, hi, zs[d], lbls[d], saved[d], do]
        in_specs += [tok(rev), tok(rev), tok(rev), lspec, sspec(rev), tok(rev)]
        out_specs += [tok(rev)] * 3
    out_specs += [pl.BlockSpec((1, 2, HEAD_PAIR), lambda b, p, i: (b, 0, p))] * nd
    return pl.pallas_call(
        body, name="gla_bwd", grid=(nb, 4, nblk),
        out_shape=[jax.ShapeDtypeStruct((t, A_WIDTH), BF16)] * (3 * nd) + [jax.ShapeDtypeStruct((nb, 2, A_WIDTH), F32)] * nd,
        in_specs=in_specs, out_specs=out_specs,
        scratch_shapes=[pltpu.VMEM((HEAD_PAIR, HEAD_PAIR), F32)] * nd,
        compiler_params=_params(),
    )(*args)


def _head_mean_matrix():
    r = lax.broadcasted_iota(jnp.int32, (A_WIDTH, A_WIDTH), 0) // 64
    c = lax.broadcasted_iota(jnp.int32, (A_WIDTH, A_WIDTH), 1) // 64
    return jnp.where(r == c, 1.0 / 64.0, 0.0).astype(BF16)


def _gla_out(o_f, o_b, hg, g, mean_mat):
    o = o_f + o_b
    ms = _group_mean(o * o, mean_mat)
    return o * lax.rsqrt(ms + EPS) * g * (hg * _sigmoid(hg))


def _gla_combine(o_f, o_b, hg, g, tm):
    t = o_f.shape[0]

    def body(of_ref, ob_ref, hg_ref, g_ref, y_ref):
        y_ref[...] = _gla_out(of_ref[...], ob_ref[...], hg_ref[...], g_ref[...], _head_mean_matrix())

    tok = pl.BlockSpec((tm, A_WIDTH), lambda i: (i, 0))
    return pl.pallas_call(
        body, name="gla_combine_fwd", grid=(t // tm,), out_shape=jax.ShapeDtypeStruct((t, A_WIDTH), F32),
        in_specs=[tok, tok, tok, _const_spec((1, A_WIDTH))], out_specs=tok, compiler_params=_params(),
    )(o_f, o_b, hg, g)


def _gla_combine_bwd(o_f, o_b, hg, g, dy, tm):
    t = o_f.shape[0]

    def body(of_ref, ob_ref, hg_ref, g_ref, dy_ref, do_ref, dhg_ref, dg_ref):
        mean_mat = _head_mean_matrix()
        fn = lambda o, hgv, gv: _gla_out(o, jnp.zeros_like(o), hgv, gv, mean_mat)
        _, vjp = jax.vjp(fn, of_ref[...] + ob_ref[...], hg_ref[...], g_ref[...])
        d_o, d_hg, d_g = vjp(dy_ref[...])
        do_ref[...] = d_o
        dhg_ref[...] = d_hg.astype(dhg_ref.dtype)

        @pl.when(pl.program_id(0) == 0)
        def _():
            dg_ref[...] = jnp.zeros_like(dg_ref)

        dg_ref[...] += d_g

    tok = pl.BlockSpec((tm, A_WIDTH), lambda i: (i, 0))
    vec = pl.BlockSpec((1, A_WIDTH), lambda i: (0, 0))
    return pl.pallas_call(
        body, name="gla_combine_bwd", grid=(t // tm,),
        out_shape=[jax.ShapeDtypeStruct((t, A_WIDTH), F32), jax.ShapeDtypeStruct((t, A_WIDTH), BF16),
                   jax.ShapeDtypeStruct((1, A_WIDTH), F32)],
        in_specs=[tok, tok, tok, _const_spec((1, A_WIDTH)), tok], out_specs=[tok, tok, vec], compiler_params=_params(),
    )(o_f, o_b, hg, g, dy)


def _post_fwd(x, ya, oattn, tgt, g_mla, w_out, g2, w_gate, w_up, w_down, g_fin, tm):
    t = x.shape[0]

    def body(x_ref, ya_ref, oa_ref, tgt_ref, gm_ref, wo_ref, g2_ref, wg_ref, wu_ref, wd_ref, gf_ref,
             x1_ref, x2_ref, gate_ref, up_ref, loss_ref):
        part = jnp.zeros((1, 1), F32)
        for j in range(tm // min(tm, ROW_SUB)):
            r = pl.ds(j * min(tm, ROW_SUB), min(tm, ROW_SUB))
            yb = _rms(oa_ref[r, :], gm_ref[...])
            x1 = x_ref[r, :] + _dot(ya_ref[r, :].astype(BF16), wo_ref[0:A_WIDTH, :]) + _dot(yb.astype(BF16), wo_ref[A_WIDTH:, :])
            x1_ref[r, :] = x1
            h2 = _rms(x1, g2_ref[...]).astype(BF16)
            gate, up = _dot(h2, wg_ref[...]), _dot(h2, wu_ref[...])
            gate_ref[r, :] = gate.astype(BF16)
            up_ref[r, :] = up.astype(BF16)
            act = (gate * _sigmoid(gate) * up).astype(BF16)
            x2 = x1 + _dot(act, wd_ref[...])
            x2_ref[r, :] = x2
            err = _rms(x2, gf_ref[...]) - tgt_ref[r, :]
            part = part + 0.5 * jnp.sum(jnp.mean(err * err, axis=-1, keepdims=True), axis=0, keepdims=True)

        @pl.when(pl.program_id(0) == 0)
        def _():
            loss_ref[...] = jnp.zeros_like(loss_ref)

        loss_ref[...] += jnp.broadcast_to(part, loss_ref.shape)

    tok = lambda wd: pl.BlockSpec((tm, wd), lambda i: (i, 0))
    return pl.pallas_call(
        body, name="post_fwd", grid=(t // tm,),
        out_shape=[jax.ShapeDtypeStruct((t, D_MODEL), F32)] * 2 + [jax.ShapeDtypeStruct((t, D_FF), BF16)] * 2
        + [jax.ShapeDtypeStruct((1, 128), F32)],
        in_specs=[tok(D_MODEL), tok(A_WIDTH), tok(512), tok(D_MODEL), _const_spec((1, 512)), _const_spec((D_MODEL, D_MODEL)),
                  _const_spec((1, D_MODEL)), _const_spec((D_MODEL, D_FF)), _const_spec((D_MODEL, D_FF)),
                  _const_spec((D_FF, D_MODEL)), _const_spec((1, D_MODEL))],
        out_specs=[tok(D_MODEL), tok(D_MODEL), tok(D_FF), tok(D_FF), pl.BlockSpec((1, 128), lambda i: (0, 0))],
        compiler_params=_params(),
    )(x, ya, oattn, tgt, g_mla, w_out, g2, w_gate, w_up, w_down, g_fin)


def _post_bwd(x1, x2, gate_b, up_b, ya, oattn, tgt, g_mla, w_out, g2, w_gate, w_up, w_down, g_fin, tm):
    t = x1.shape[0]

    def body(x1_ref, x2_ref, gate_ref, up_ref, ya_ref, oa_ref, tgt_ref, gm_ref, wo_ref, g2_ref, wg_ref, wu_ref, wd_ref, gf_ref,
             dx1_ref, dya_ref, doa_ref, ycat_ref, dx1b_ref, h2_ref, dgate_ref, dup_ref, act_ref, dx2b_ref,
             dgm_ref, dg2_ref, dgf_ref):
        x1, x2 = x1_ref[...], x2_ref[...]
        dy = (_rms(x2, gf_ref[...]) - tgt_ref[...]) * (1.0 / D_MODEL)
        dx2, dgf = _rms_bwd(x2, gf_ref[...], dy)
        dx2b = dx2.astype(BF16)
        dx2b_ref[...] = dx2b
        h2_ref[...] = _rms(x1, g2_ref[...]).astype(BF16)
        gate, up = gate_ref[...].astype(F32), up_ref[...].astype(F32)
        sg = _sigmoid(gate)
        sl = gate * sg
        act_ref[...] = (sl * up).astype(BF16)
        dact = _dot_nt(dx2b, wd_ref[...])
        dup = (dact * sl).astype(BF16)
        dgate = (dact * up * (sg * (1.0 + gate * (1.0 - sg)))).astype(BF16)
        dup_ref[...] = dup
        dgate_ref[...] = dgate
        dh2 = _dot_nt(dgate, wg_ref[...]) + _dot_nt(dup, wu_ref[...])
        dx1n, dg2 = _rms_bwd(x1, g2_ref[...], dh2)
        dx1 = dx2 + dx1n
        dx1_ref[...] = dx1
        dx1b = dx1.astype(BF16)
        dx1b_ref[...] = dx1b
        oa = oa_ref[...]
        ycat_ref[:, 0:A_WIDTH] = ya_ref[...].astype(BF16)
        ycat_ref[:, A_WIDTH:] = _rms(oa, gm_ref[...]).astype(BF16)
        dya_ref[...] = _dot_nt(dx1b, wo_ref[0:A_WIDTH, :])
        doa, dgm = _rms_bwd(oa, gm_ref[...], _dot_nt(dx1b, wo_ref[A_WIDTH:, :]))
        doa_ref[...] = doa

        @pl.when(pl.program_id(0) == 0)
        def _():
            dgm_ref[...] = jnp.zeros_like(dgm_ref)
            dg2_ref[...] = jnp.zeros_like(dg2_ref)
            dgf_ref[...] = jnp.zeros_like(dgf_ref)

        dgm_ref[...] += dgm
        dg2_ref[...] += dg2
        dgf_ref[...] += dgf

    tok = lambda wd: pl.BlockSpec((tm, wd), lambda i: (i, 0))
    vec = lambda wd: pl.BlockSpec((1, wd), lambda i: (0, 0))
    sds = lambda wd, dt: jax.ShapeDtypeStruct((t, wd), dt)
    return pl.pallas_call(
        body, name="post_bwd", grid=(t // tm,),
        out_shape=[sds(D_MODEL, F32), sds(512, F32), sds(512, F32), sds(D_MODEL, BF16), sds(D_MODEL, BF16), sds(D_MODEL, BF16),
                   sds(D_FF, BF16), sds(D_FF, BF16), sds(D_FF, BF16), sds(D_MODEL, BF16),
                   jax.ShapeDtypeStruct((1, 512), F32), jax.ShapeDtypeStruct((1, D_MODEL), F32), jax.ShapeDtypeStruct((1, D_MODEL), F32)],
        in_specs=[tok(D_MODEL), tok(D_MODEL), tok(D_FF), tok(D_FF), tok(512), tok(512), tok(D_MODEL), _const_spec((1, 512)),
                  _const_spec((D_MODEL, D_MODEL)), _const_spec((1, D_MODEL)), _const_spec((D_MODEL, D_FF)),
                  _const_spec((D_MODEL, D_FF)), _const_spec((D_FF, D_MODEL)), _const_spec((1, D_MODEL))],
        out_specs=[tok(D_MODEL), tok(512), tok(512), tok(D_MODEL), tok(D_MODEL), tok(D_MODEL), tok(D_FF), tok(D_FF), tok(D_FF),
                   tok(D_MODEL), vec(512), vec(D_MODEL), vec(D_MODEL)],
        compiler_params=_params(),
    )(x1, x2, gate_b, up_b, ya, oattn, tgt, g_mla, w_out, g2, w_gate, w_up, w_down, g_fin)


def _matmul_tn(a, b, tn, tt, tag):
    t, k = a.shape
    n = b.shape[1]
    last = t // tt - 1

    def body(a_ref, b_ref, o_ref, acc_ref):
        part = _dot_tn(a_ref[...], b_ref[...])

        @pl.when(pl.program_id(1) == 0)
        def _():
            acc_ref[...] = part

        @pl.when(pl.program_id(1) > 0)
        def _():
            acc_ref[...] += part

        @pl.when(pl.program_id(1) == last)
        def _():
            o_ref[...] = acc_ref[...].astype(o_ref.dtype)

    return pl.pallas_call(
        body, name="wgrad_" + tag, grid=(n // tn, t // tt), out_shape=jax.ShapeDtypeStruct((k, n), BF16),
        in_specs=[pl.BlockSpec((tt, k), lambda j, i: (i, 0)), pl.BlockSpec((tt, tn), lambda j, i: (i, j))],
        out_specs=pl.BlockSpec((k, tn), lambda j, i: (0, j)), scratch_shapes=[pltpu.VMEM((k, tn), F32)],
        compiler_params=_params(),
    )(a, b)


def _mla_qkv_bwd(cq, ckv, g_qa, g_kva, w_q, w_kv, tables, dq, dk, dv, seq, tm):
    t = cq.shape[0]
    nblk = seq // tm

    def body(cq_ref, ckv_ref, gq_ref, gk_ref, wq_ref, wkv_ref, c_ref, sa_ref, sb_ref, dq_ref, dk_ref, dv_ref,
             dcq_ref, dckv_ref, dkr_ref, cqn_ref, dqf_ref, ckn_ref, dkv_ref, dgq_ref, dgk_ref):
        cos_t, sin_a, sin_b = c_ref[...], sa_ref[...], sb_ref[...]
        cqn_ref[...] = _rms(cq_ref[...], gq_ref[...]).astype(BF16)
        ckn_ref[...] = _rms(ckv_ref[...], gk_ref[...]).astype(BF16)
        dkr = jnp.zeros((tm, 128), F32)
        for h in range(B_HEADS):
            lo = h * QK_PAD
            dqf_ref[:, lo:lo + 128] = (dq_ref[:, lo:lo + 128].astype(F32) * ATTN_SCALE).astype(BF16)
            dq_rope = dq_ref[:, lo + 128:lo + 256].astype(F32) * ATTN_SCALE
            dqf_ref[:, lo + 128:lo + 256] = _rope_t(dq_rope, cos_t, sin_a, sin_b).astype(BF16)
            dkv_ref[:, lo:lo + 128] = dk_ref[:, lo:lo + 128].astype(BF16)
            dkv_ref[:, lo + 128:lo + 256] = dv_ref[:, h * B_V:(h + 1) * B_V].astype(BF16)
            dkr = dkr + dk_ref[:, lo + 128:lo + 256]
        dkr_ref[...] = _rope_t(dkr, cos_t, sin_a, sin_b).astype(dkr_ref.dtype)
        dcq, dgq = _rms_bwd(cq_ref[...], gq_ref[...], _dot_nt(dqf_ref[...], wq_ref[...]))
        dckv, dgk = _rms_bwd(ckv_ref[...], gk_ref[...], _dot_nt(dkv_ref[...], wkv_ref[...]))
        dcq_ref[...] = dcq.astype(dcq_ref.dtype)
        dckv_ref[...] = dckv.astype(dckv_ref.dtype)

        @pl.when(pl.program_id(0) == 0)
        def _():
            dgq_ref[...] = jnp.zeros_like(dgq_ref)
            dgk_ref[...] = jnp.zeros_like(dgk_ref)

        dgq_ref[...] += dgq
        dgk_ref[...] += dgk

    tok = lambda wd: pl.BlockSpec((tm, wd), lambda i: (i, 0))
    vec = lambda wd: pl.BlockSpec((1, wd), lambda i: (0, 0))
    tab = pl.BlockSpec((tm, 128), lambda i: (i % nblk, 0))
    sds = lambda wd, dt: jax.ShapeDtypeStruct((t, wd), dt)
    return pl.pallas_call(
        body, name="mla_qkv_bwd", grid=(t // tm,),
        out_shape=[sds(Q_LORA, BF16), sds(KV_LORA, BF16), sds(128, BF16), sds(Q_LORA, BF16), sds(1024, BF16), sds(KV_LORA, BF16),
                   sds(1024, BF16), jax.ShapeDtypeStruct((1, Q_LORA), F32), jax.ShapeDtypeStruct((1, KV_LORA), F32)],
        in_specs=[tok(Q_LORA), tok(KV_LORA), _const_spec((1, Q_LORA)), _const_spec((1, KV_LORA)),
                  _const_spec((Q_LORA, 1024)), _const_spec((KV_LORA, 1024)), tab, tab, tab,
                  tok(1024), tok(1024), tok(512)],
        out_specs=[tok(Q_LORA), tok(KV_LORA), tok(128), tok(Q_LORA), tok(1024), tok(KV_LORA), tok(1024),
                   vec(Q_LORA), vec(KV_LORA)],
        compiler_params=_params(),
    )(cq, ckv, g_qa, g_kva, w_q, w_kv, *tables, dq, dk, dv)


def _inproj_bwd(x, g1, w_in, dx1, pieces, tm):
    t = x.shape[0]
    counts = [len(p) for p in pieces]
    flat = [a for p in pieces for a in p]
    widths = [wd for wd, p in zip(IN_WIDTHS, pieces) for _ in p]

    def body(x_ref, g_ref, w_ref, dx1_ref, *refs):
        ins = refs[:len(flat)]
        dx_ref, h_ref, dp_ref, dg_ref = refs[len(flat):]
        xv = x_ref[...]
        h_ref[...] = _rms(xv, g_ref[...]).astype(BF16)
        off, j = 0, 0
        for wd, cnt in zip(IN_WIDTHS, counts):
            acc = ins[j][...].astype(F32)
            for jj in range(1, cnt):
                acc = acc + ins[j + jj][...].astype(F32)
            dp_ref[:, off:off + wd] = acc.astype(BF16)
            off += wd
            j += cnt
        dxn, dg = _rms_bwd(xv, g_ref[...], _dot_nt(dp_ref[...], w_ref[...]))
        dx_ref[...] = dx1_ref[...] + dxn

        @pl.when(pl.program_id(0) == 0)
        def _():
            dg_ref[...] = jnp.zeros_like(dg_ref)

        dg_ref[...] += dg

    tok = lambda wd: pl.BlockSpec((tm, wd), lambda i: (i, 0))
    return pl.pallas_call(
        body, name="inproj_bwd", grid=(t // tm,),
        out_shape=[jax.ShapeDtypeStruct((t, D_MODEL), F32), jax.ShapeDtypeStruct((t, D_MODEL), BF16),
                   jax.ShapeDtypeStruct((t, D_IN_PAD), BF16), jax.ShapeDtypeStruct((1, D_MODEL), F32)],
        in_specs=[tok(D_MODEL), _const_spec((1, D_MODEL)), _const_spec((D_MODEL, D_IN_PAD)), tok(D_MODEL)] + [tok(wd) for wd in widths],
        out_specs=[tok(D_MODEL), tok(D_MODEL), tok(D_IN_PAD), pl.BlockSpec((1, D_MODEL), lambda i: (0, 0))],
        compiler_params=_params(),
    )(x, g1, w_in, dx1, *flat)


def _cols_from_slots(g):
    n, r, cs = g.shape
    return g.transpose(1, 0, 2).reshape(r, n * cs)


def _cols_to_slots(full):
    r, c = full.shape
    return full.reshape(r, N_DEV, c // N_DEV).transpose(1, 0, 2)


def _arrange_w_in(w_in):
    return jnp.concatenate([w_in, jnp.zeros((D_MODEL, D_IN_PAD - D_IN), w_in.dtype)], axis=1)


def _arrange_w_q(w_q_b):
    q3 = w_q_b.reshape(Q_LORA, B_HEADS, B_NOPE + B_ROPE)
    pad = jnp.zeros((Q_LORA, B_HEADS, QK_PAD - B_NOPE - B_ROPE), w_q_b.dtype)
    return jnp.concatenate([q3, pad], axis=2).reshape(Q_LORA, B_HEADS * QK_PAD)


def _unarrange_w_q(d_q):
    return d_q.reshape(Q_LORA, B_HEADS, QK_PAD)[:, :, :B_NOPE + B_ROPE].reshape(Q_LORA, B_HEADS * (B_NOPE + B_ROPE))


def _step_core(x, loss_target, small_w, lb_full, early_full, late, seq, group, tiles, distributed):
    g1, g_hgrn, g_qa, g_kva, g_mla, g2, g_fin = small_w
    w_in, w_q, w_kv = _arrange_w_in(early_full[0]), _arrange_w_q(early_full[1]), early_full[2]
    nb = x.shape[0]
    t = nb * seq
    tm, tm_fwd, tq_f, tq_b, tt = tiles
    xt = x.reshape(t, D_MODEL)
    tgt = loss_target.reshape(t, D_MODEL)
    tables = _rope_tables(seq)

    hq, hi, zf, zb, hg, cq, ckv, kr = _inproj(xt, g1, w_in, tm_fwd)
    qcat, kcat, vv = _mla_qkv(cq, ckv, kr, g_qa, g_kva, w_q, w_kv, tables, seq, tm)
    if distributed:
        oattn, lse, *late_slots = _attn_fwd(qcat, kcat, vv, nb, seq, tq_f, gather=tuple(late))
    else:
        oattn, lse = _attn_fwd(qcat, kcat, vv, nb, seq, tq_f)
        late_slots = late
    w_out = late_slots[0].reshape(D_MODEL, D_MODEL)
    w_gate, w_up = _cols_from_slots(late_slots[1]), _cols_from_slots(late_slots[2])
    w_down = late_slots[3].reshape(D_FF, D_MODEL)
    lbl_f, lbl_b = lb_full[0], lb_full[1]
    o_f, o_b, save_f, save_b = _gla_fwd(hq, hi, (zf, zb), (lbl_f, lbl_b), nb, seq, group)
    ya = _gla_combine(o_f, o_b, hg, g_hgrn, tm)
    x1, x2, gate_b, up_b, loss_row = _post_fwd(xt, ya, oattn, tgt, g_mla, w_out, g2, w_gate, w_up, w_down, g_fin, tm_fwd)

    (dx1, d_ya, d_oattn, ycat_b, dx1_b, h2_b, dgate_b, dup_b, act_b, dx2_b, d_g_mla, d_g2, d_g_fin) = _post_bwd(
        x1, x2, gate_b, up_b, ya, oattn, tgt, g_mla, w_out, g2, w_gate, w_up, w_down, g_fin, tm)
    d_w_gate = _matmul_tn(h2_b, dgate_b, D_FF // 2, tt, "gate")
    d_w_up = _matmul_tn(h2_b, dup_b, D_FF // 2, tt, "up")
    d_w_down = _matmul_tn(act_b, dx2_b, 512, tt, "down")
    d_w_out = _matmul_tn(ycat_b, dx1_b, D_MODEL, tt, "out")
    late_g = [d_w_out.reshape(N_DEV, D_MODEL // N_DEV, D_MODEL), _cols_to_slots(d_w_gate), _cols_to_slots(d_w_up),
              d_w_down.reshape(N_DEV, D_FF // N_DEV, D_MODEL)]
    if distributed:
        dq, dk, dv, *late_g = _attn_bwd(qcat, kcat, vv, oattn, lse, d_oattn, nb, seq, tq_b, exchange=tuple(late_g))
    else:
        dq, dk, dv = _attn_bwd(qcat, kcat, vv, oattn, lse, d_oattn, nb, seq, tq_b)
    (d_cq, d_ckv, d_kr, cqn_b, dqf_b, ckn_b, dkv_b, d_g_qa, d_g_kva) = _mla_qkv_bwd(
        cq, ckv, g_qa, g_kva, w_q, w_kv, tables, dq, dk, dv, seq, tm)
    d_w_q = _matmul_tn(cqn_b, dqf_b, B_HEADS * QK_PAD, tt, "q_b")
    d_w_kv = _matmul_tn(ckn_b, dkv_b, B_HEADS * (B_NOPE + B_V), tt, "kv_b")
    d_o, d_hg, d_g_hgrn = _gla_combine_bwd(o_f, o_b, hg, g_hgrn, d_ya, tm)
    dq_f, dv_f, dz_f, dq_b, dv_b, dz_b, dl_f, dl_b = _gla_bwd(
        hq, hi, (zf, zb), (lbl_f, lbl_b), (save_f, save_b), d_o, nb, seq, group)
    grad_x, h1_b, dproj_b, d_g1 = _inproj_bwd(
        xt, g1, w_in, dx1, [[dq_f, dq_b], [dv_f, dv_b], [dz_f], [dz_b], [d_hg], [d_cq], [d_ckv], [d_kr]], tm)
    d_w_in_arr = _matmul_tn(h1_b, dproj_b, D_IN_PAD // 2, tt, "in")

    early_g = [_cols_to_slots(d_w_in_arr[:, :D_IN]), _cols_to_slots(_unarrange_w_q(d_w_q)), _cols_to_slots(d_w_kv)]
    d_lb = jnp.stack([jnp.sum(dl_f, axis=0), jnp.sum(dl_b, axis=0)], axis=0)
    small_grads = [d_g1, d_g_hgrn, d_g_qa, d_g_kva, d_g_mla, d_g2, d_g_fin]
    return loss_row, grad_x.reshape(nb, seq, D_MODEL), early_g, late_g, small_grads, d_lb


def kernel(x, norm1_g, w_in, lb_logits, hgrn_norm_g, q_a_norm_g, w_q_b, kv_a_norm_g, w_kv_b, mla_norm_g, w_out, norm2_g, w_gate, w_up, w_down, final_norm_g, loss_target, m_norm1_g, m_w_in, m_lb_logits, m_hgrn_norm_g, m_q_a_norm_g, m_w_q_b, m_kv_a_norm_g, m_w_kv_b, m_mla_norm_g, m_w_out, m_norm2_g, m_w_gate, m_w_up, m_w_down, m_final_norm_g, v_norm1_g, v_w_in, v_lb_logits, v_hgrn_norm_g, v_q_a_norm_g, v_w_q_b, v_kv_a_norm_g, v_w_kv_b, v_mla_norm_g, v_w_out, v_norm2_g, v_w_gate, v_w_up, v_w_down, v_final_norm_g):
    big_w = [w_in, w_q_b, w_kv_b, w_out, w_gate, w_up, w_down]
    big_m = [m_w_in, m_w_q_b, m_w_kv_b, m_w_out, m_w_gate, m_w_up, m_w_down]
    big_v = [v_w_in, v_w_q_b, v_w_kv_b, v_w_out, v_w_gate, v_w_up, v_w_down]
    small_w = [norm1_g, hgrn_norm_g, q_a_norm_g, kv_a_norm_g, mla_norm_g, norm2_g, final_norm_g]
    small_m = [m_norm1_g, m_hgrn_norm_g, m_q_a_norm_g, m_kv_a_norm_g, m_mla_norm_g, m_norm2_g, m_final_norm_g]
    small_v = [v_norm1_g, v_hgrn_norm_g, v_q_a_norm_g, v_kv_a_norm_g, v_mla_norm_g, v_norm2_g, v_final_norm_g]
    seq = x.shape[1]
    my_id = 4 * lax.axis_index("x") + 2 * lax.axis_index("y") + lax.axis_index("c")

    shard = lambda w: w[0].astype(BF16)
    g_in, g_q, g_kv, g_lb = _all_gather_call([shard(w_in), shard(w_q_b), shard(w_kv_b), lb_logits.reshape(4, 64)])
    early_full = (_cols_from_slots(g_in), _cols_from_slots(g_q), _cols_from_slots(g_kv))
    lb_full = g_lb.reshape(N_DEV, 2, 2, 64).transpose(1, 2, 0, 3).reshape(2, 2, 512)

    as_row = lambda a: a.reshape(1, -1)
    loss_row, grad_x, early_g, late_recv, small_g, d_lb = _step_core(
        x, loss_target, [as_row(s) for s in small_w], lb_full, early_full,
        [shard(w_out), shard(w_gate), shard(w_up), shard(w_down)], seq, min(8, seq // CHUNK),
        (256, 512, min(1024, seq), min(512, seq), min(2048, 2 * seq)), True)

    n_small = len(small_g)
    recv = _exchange_call(early_g + small_g + [d_lb.reshape(4, 512), loss_row], [True] * 3 + [False] * (n_small + 2))
    sums = _sum_slots_call(recv[3:])
    g_small = [g.reshape(s.shape) for g, s in zip(sums[:n_small], small_w)]
    g_lb_own = lax.dynamic_index_in_dim(sums[n_small].reshape(2, 2, N_DEV, 64), my_id, axis=2, keepdims=False)
    loss = sums[n_small + 1][0, 0]

    grads, deltas, new_ms, new_vs = {}, {}, {}, {}
    big_recv = dict(zip(["w_in", "w_q_b", "w_kv_b", "w_out", "w_gate", "w_up", "w_down"], list(recv[:3]) + list(late_recv)))
    for (name, _, _, _), w, m, v in zip(BIG, big_w, big_m, big_v):
        g, d, nm, nv = _adamw_recv(w[0], big_recv[name], m[0], v[0], name)
        grads[name], deltas[name], new_ms[name], new_vs[name] = g[None], d[None], nm[None], nv[None]
    lb_rows = lambda a: a.reshape(4, 64)
    d_s, nm_s, nv_s = _adamw_small(
        [as_row(a) for a in small_w] + [lb_rows(lb_logits)], [as_row(a) for a in g_small] + [lb_rows(g_lb_own)],
        [as_row(a) for a in small_m] + [lb_rows(m_lb_logits)], [as_row(a) for a in small_v] + [lb_rows(v_lb_logits)])
    for i, (s, (name, _)) in enumerate(zip(small_w + [lb_logits], SMALL + (("lb_logits", 0),))):
        grads[name] = (g_small + [g_lb_own])[i]
        deltas[name], new_ms[name], new_vs[name] = d_s[i].reshape(s.shape), nm_s[i].reshape(s.shape), nv_s[i].reshape(s.shape)

    order = ["norm1_g", "w_in", "lb_logits", "hgrn_norm_g", "q_a_norm_g", "w_q_b", "kv_a_norm_g", "w_kv_b", "mla_norm_g",
             "w_out", "norm2_g", "w_gate", "w_up", "w_down", "final_norm_g"]
    return (loss, grad_x, *[grads[n] for n in order], *[deltas[n] for n in order],
            *[new_ms[n] for n in order], *[new_vs[n] for n in order])
```

```python
import functools
import math

import jax
import jax.numpy as jnp
from jax import lax
from jax.experimental import pallas as pl
from jax.experimental.pallas import tpu as pltpu

F32 = jnp.float32
BF16 = jnp.bfloat16

N_DEV = 8
D_MODEL = 1024
D_FF = 2816
A_WIDTH = 512
HEAD_PAIR = 128
CHUNK = 64
B_HEADS = 4
B_NOPE = 128
B_ROPE = 64
B_V = 128
QK_PAD = 256
Q_LORA = 384
KV_LORA = 256
D_IN = 3264
D_IN_PAD = 3328
IN_WIDTHS = (512, 512, 512, 512, 512, Q_LORA, KV_LORA, 128)
ROPE_THETA = 10000.0
EPS = 1e-6
ATTN_SCALE = (B_NOPE + B_ROPE) ** -0.5
ATTN_SUB = 256
ATTN_SUB_BWD = 256
ROW_SUB = 256
ADAM_LR, ADAM_B1, ADAM_B2, ADAM_EPS, ADAM_WD, ADAM_STEP = 0.001, 0.9, 0.999, 1e-08, 0.01, 10
VMEM_LIMIT = 60 * 1024 * 1024
MESH = pl.DeviceIdType.MESH

BIG = (("w_in", 1024, D_IN, 1), ("w_q_b", Q_LORA, 768, 1), ("w_kv_b", KV_LORA, 1024, 1), ("w_out", 1024, 1024, 0),
       ("w_gate", 1024, D_FF, 1), ("w_up", 1024, D_FF, 1), ("w_down", D_FF, 1024, 0))
SMALL = (("norm1_g", 1024), ("hgrn_norm_g", 512), ("q_a_norm_g", 384), ("kv_a_norm_g", 256), ("mla_norm_g", 512),
         ("norm2_g", 1024), ("final_norm_g", 1024))


def _params(**kw):
    return pltpu.CompilerParams(vmem_limit_bytes=VMEM_LIMIT, **kw)


def _const_spec(shape):
    return pl.BlockSpec(shape, lambda *_: (0,) * len(shape), pipeline_mode=pl.Buffered(1))


def _dot(a, b):
    return jnp.dot(a, b, preferred_element_type=F32)


def _dot_nt(a, b):
    return lax.dot_general(a, b, (((1,), (1,)), ((), ())), preferred_element_type=F32)


def _dot_tn(a, b):
    return lax.dot_general(a, b, (((0,), (0,)), ((), ())), preferred_element_type=F32)


@jax.custom_vjp
def _mm(a, b):
    return _dot(a.astype(BF16), b.astype(BF16))


def _mm_fwd(a, b):
    return _mm(a, b), (a, b)


def _mm_bwd(res, g):
    a, b = res
    gb = g.astype(BF16)
    return _dot_nt(gb, b.astype(BF16)), _dot_tn(a.astype(BF16), gb)


_mm.defvjp(_mm_fwd, _mm_bwd)


@jax.custom_vjp
def _mm_nt(a, b):
    return _dot_nt(a.astype(BF16), b.astype(BF16))


def _mm_nt_fwd(a, b):
    return _mm_nt(a, b), (a, b)


def _mm_nt_bwd(res, g):
    a, b = res
    gb = g.astype(BF16)
    return _dot(gb, b.astype(BF16)), _dot_tn(gb, a.astype(BF16))


_mm_nt.defvjp(_mm_nt_fwd, _mm_nt_bwd)


@jax.custom_vjp
def _mm_tn(a, b):
    return _dot_tn(a.astype(BF16), b.astype(BF16))


def _mm_tn_fwd(a, b):
    return _mm_tn(a, b), (a, b)


def _mm_tn_bwd(res, g):
    a, b = res
    gb = g.astype(BF16)
    return _dot_nt(b.astype(BF16), gb), _dot(a.astype(BF16), gb)


_mm_tn.defvjp(_mm_tn_fwd, _mm_tn_bwd)


def _dot_exact_rhs(a, m):
    hi = a.astype(BF16)
    lo = (a - hi.astype(F32)).astype(BF16)
    return _dot(hi, m) + _dot(lo, m)


@jax.custom_vjp
def _group_mean(a, m):
    return _dot_exact_rhs(a, m)


def _group_mean_fwd(a, m):
    return _group_mean(a, m), m


def _group_mean_bwd(m, g):
    return _dot_exact_rhs(g, m), jnp.zeros_like(m)


_group_mean.defvjp(_group_mean_fwd, _group_mean_bwd)


def _roll_rows(a, shift):
    return pltpu.roll(a, shift, 0)


def _cumsum_rows_raw(a, reverse):
    n = a.shape[0]
    row = lax.broadcasted_iota(jnp.int32, a.shape, 0)
    s = 1
    while s < n:
        if reverse:
            a = a + jnp.where(row < n - s, _roll_rows(a, n - s), 0.0)
        else:
            a = a + jnp.where(row >= s, _roll_rows(a, s), 0.0)
        s *= 2
    return a


@functools.partial(jax.custom_vjp, nondiff_argnums=(1,))
def _cumsum_rows(a, reverse):
    return _cumsum_rows_raw(a, reverse)


def _cumsum_rows_fwd(a, reverse):
    return _cumsum_rows_raw(a, reverse), None


def _cumsum_rows_bwd(reverse, _, g):
    return (_cumsum_rows_raw(g, not reverse),)


_cumsum_rows.defvjp(_cumsum_rows_fwd, _cumsum_rows_bwd)


def _rms(x, g):
    r = lax.rsqrt(jnp.mean(x * x, axis=-1, keepdims=True) + EPS)
    return x * r * g


def _rms_bwd(x, g, dy):
    r = lax.rsqrt(jnp.mean(x * x, axis=-1, keepdims=True) + EPS)
    xh = x * r
    dg = jnp.sum(dy * xh, axis=0, keepdims=True)
    dxh = dy * g
    dx = r * (dxh - xh * jnp.mean(dxh * xh, axis=-1, keepdims=True))
    return dx, dg


def _sigmoid(a):
    return jax.nn.sigmoid(a)


def _mesh_place():
    x, y, c = lax.axis_index("x"), lax.axis_index("y"), lax.axis_index("c")
    return x, y, c


def _dev_index(p):
    return 4 * p[0] + 2 * p[1] + p[2]


def _comm_sems(n):
    return [pltpu.SemaphoreType.DMA((n, 7)), pltpu.SemaphoreType.DMA((n, 7)), pltpu.SemaphoreType.DMA((n,))]


def _gather_protocol(ins, outs, send_sems, recv_sems, local_sems):
    n = len(ins)
    x, y, c = _mesh_place()
    me, sibling = (x, y, c), (x, y, 1 - c)
    chips = [(1 - x, y), (x, 1 - y), (1 - x, 1 - y)]

    def copy(a, k, block, to, src=None):
        slot = outs[a].at[_dev_index(block)]
        return pltpu.make_async_remote_copy(
            src_ref=slot if src is None else src, dst_ref=slot,
            send_sem=send_sems.at[a, k], recv_sem=recv_sems.at[a, k], device_id=to, device_id_type=MESH)

    def mine(a):
        return pltpu.make_async_copy(ins[a], outs[a].at[_dev_index(me)], local_sems.at[a])

    def first(a):
        return [copy(a, 0, me, sibling, src=ins[a])] + [copy(a, 1 + j, me, (*chip, c), src=ins[a]) for j, chip in enumerate(chips)]

    def start():
        for a in range(n):
            mine(a).start()
            for cp in first(a):
                cp.start()

    def forward():
        for a in range(n):
            for j, chip in enumerate(chips):
                copy(a, 1 + j, (*chip, c), me).wait_recv()
                copy(a, 4 + j, (*chip, c), sibling).start()

    def finish():
        for a in range(n):
            copy(a, 0, sibling, me).wait_recv()
            for j, chip in enumerate(chips):
                copy(a, 4 + j, (*chip, 1 - c), me).wait_recv()
        for a in range(n):
            mine(a).wait()
            for cp in first(a):
                cp.wait_send()
            for j, chip in enumerate(chips):
                copy(a, 4 + j, (*chip, c), sibling).wait_send()

    return start, forward, finish


def _exchange_protocol(ins, outs, scatter, send_sems, recv_sems, local_sems):
    n = len(ins)
    x, y, c = _mesh_place()
    me = (x, y, c)
    my_id = _dev_index(me)
    rels = [(dx, dy, dc) for dx in (0, 1) for dy in (0, 1) for dc in (0, 1)][1:]

    def peer_of(rel):
        return tuple(1 - v if d else v for v, d in zip(me, rel))

    def src(a, dev):
        return ins[a].at[dev] if scatter[a] else ins[a]

    def send(a, k):
        peer = peer_of(rels[k])
        return pltpu.make_async_remote_copy(
            src_ref=src(a, _dev_index(peer)), dst_ref=outs[a].at[my_id],
            send_sem=send_sems.at[a, k], recv_sem=recv_sems.at[a, k], device_id=peer, device_id_type=MESH)

    def arrival(a, k):
        peer = peer_of(rels[k])
        return pltpu.make_async_remote_copy(
            src_ref=src(a, my_id), dst_ref=outs[a].at[_dev_index(peer)],
            send_sem=send_sems.at[a, k], recv_sem=recv_sems.at[a, k], device_id=peer, device_id_type=MESH)

    def own(a):
        return pltpu.make_async_copy(src(a, my_id), outs[a].at[my_id], local_sems.at[a])

    def start():
        for a in range(n):
            own(a).start()
            for k in range(7):
                send(a, k).start()

    def finish():
        for a in range(n):
            for k in range(7):
                arrival(a, k).wait_recv()
        for a in range(n):
            for k in range(7):
                send(a, k).wait_send()
            own(a).wait()

    return start, finish


def _slot_shapes(blocks, scatter=None):
    return [jax.ShapeDtypeStruct(b.shape if (scatter and scatter[a]) else (N_DEV,) + b.shape, b.dtype) for a, b in enumerate(blocks)]


def _all_gather_call(blocks):
    n = len(blocks)

    def body(*refs):
        start, forward, finish = _gather_protocol(refs[:n], refs[n:2 * n], *refs[2 * n:])
        start()
        forward()
        finish()

    any_spec = pl.BlockSpec(memory_space=pl.ANY)
    return pl.pallas_call(
        body, name="weights_all_gather", out_shape=_slot_shapes(blocks),
        in_specs=[any_spec] * n, out_specs=[any_spec] * n, scratch_shapes=_comm_sems(n),
    )(*blocks)


def _exchange_call(blocks, scatter):
    n = len(blocks)

    def body(*refs):
        start, finish = _exchange_protocol(refs[:n], refs[n:2 * n], scatter, *refs[2 * n:])
        start()
        finish()

    any_spec = pl.BlockSpec(memory_space=pl.ANY)
    return pl.pallas_call(
        body, name="grad_exchange", out_shape=_slot_shapes(blocks, scatter),
        in_specs=[any_spec] * n, out_specs=[any_spec] * n, scratch_shapes=_comm_sems(n),
    )(*blocks)


def _sum_slots_call(recvs):
    n = len(recvs)

    def body(*refs):
        for in_ref, out_ref in zip(refs[:n], refs[n:]):
            acc = in_ref[0]
            for j in range(1, N_DEV):
                acc = acc + in_ref[j]
            out_ref[...] = acc

    return pl.pallas_call(
        body, name="small_grad_sum", out_shape=[jax.ShapeDtypeStruct(r.shape[1:], F32) for r in recvs],
        compiler_params=_params(),
    )(*recvs)


def _adam_update(w, g, m, v):
    nm = ADAM_B1 * m + (1.0 - ADAM_B1) * g
    nv = ADAM_B2 * v + (1.0 - ADAM_B2) * (g * g)
    bc1 = 1.0 - ADAM_B1 ** ADAM_STEP
    bc2 = 1.0 - ADAM_B2 ** ADAM_STEP
    return -ADAM_LR * ((nm / bc1) / (jnp.sqrt(nv / bc2) + ADAM_EPS) + ADAM_WD * w), nm, nv


def _adamw_recv(w, recv, m, v, tag):
    r, c = w.shape
    tr = r
    for cand in (512, 256, 128):
        if r > cand and r % cand == 0:
            tr = cand
            break

    def body(w_ref, r_ref, m_ref, v_ref, g_ref, d_ref, nm_ref, nv_ref):
        g = r_ref[0].astype(F32)
        for j in range(1, N_DEV):
            g = g + r_ref[j].astype(F32)
        g_ref[...] = g
        d_ref[...], nm_ref[...], nv_ref[...] = _adam_update(w_ref[...], g, m_ref[...], v_ref[...])

    spec = pl.BlockSpec((tr, c), lambda i: (i, 0))
    return pl.pallas_call(
        body, name="adamw_" + tag, out_shape=[jax.ShapeDtypeStruct(w.shape, F32)] * 4, grid=(r // tr,),
        in_specs=[spec, pl.BlockSpec((N_DEV, tr, c), lambda i: (0, i, 0)), spec, spec], out_specs=[spec] * 4,
        compiler_params=_params(),
    )(w, recv, m, v)


def _adamw_small(ws, gs, ms, vs):
    n = len(ws)

    def body(*refs):
        ins, outs = refs[:4 * n], refs[4 * n:]
        for a in range(n):
            d, nm, nv = _adam_update(ins[a][...], ins[n + a][...], ins[2 * n + a][...], ins[3 * n + a][...])
            outs[a][...], outs[n + a][...], outs[2 * n + a][...] = d, nm, nv

    out = pl.pallas_call(
        body, name="adamw_small", out_shape=[jax.ShapeDtypeStruct(w.shape, F32) for w in ws] * 3, compiler_params=_params(),
    )(*ws, *gs, *ms, *vs)
    return out[:n], out[n:2 * n], out[2 * n:]


def _tile(t, want):
    return want if t % want == 0 else t


def _inproj(x, g1, w_in, tm):
    t = x.shape[0]

    def body(x_ref, g_ref, w_ref, *outs):
        for j in range(tm // min(tm, ROW_SUB)):
            r = pl.ds(j * min(tm, ROW_SUB), min(tm, ROW_SUB))
            h = _rms(x_ref[r, :], g_ref[...]).astype(BF16)
            off = 0
            for o_ref, wd in zip(outs, IN_WIDTHS):
                o_ref[r, :] = _dot_nt(h, w_ref[off:off + wd, :])
                off += wd

    return pl.pallas_call(
        body, name="inproj_fwd", grid=(t // tm,),
        out_shape=[jax.ShapeDtypeStruct((t, wd), F32) for wd in IN_WIDTHS],
        in_specs=[pl.BlockSpec((tm, D_MODEL), lambda i: (i, 0)), _const_spec((1, D_MODEL)), _const_spec((D_IN_PAD, D_MODEL))],
        out_specs=[pl.BlockSpec((tm, wd), lambda i: (i, 0)) for wd in IN_WIDTHS],
        compiler_params=_params(),
    )(x, g1, w_in)


def _rope_tables(seq):
    inv = 1.0 / (ROPE_THETA ** (jnp.arange(0, B_ROPE, 2, dtype=F32) / B_ROPE))
    ang = jnp.arange(seq, dtype=F32)[:, None] * inv[None, :]
    cos, sin = jnp.cos(ang), jnp.sin(ang)
    z32, z64 = jnp.zeros_like(cos), jnp.zeros((seq, 64), F32)
    cos_t = jnp.concatenate([cos, cos, z64], axis=1)
    sin_a = jnp.concatenate([-sin, z32, z64], axis=1)
    sin_b = jnp.concatenate([z32, sin, z64], axis=1)
    return cos_t, sin_a, sin_b


def _rope(t, cos_t, sin_a, sin_b):
    return t * cos_t + pltpu.roll(t, 96, 1) * sin_a + pltpu.roll(t, 32, 1) * sin_b


def _rope_t(d, cos_t, sin_a, sin_b):
    return d * cos_t + pltpu.roll(d * sin_a, 32, 1) + pltpu.roll(d * sin_b, 96, 1)


def _mla_qkv(cq, ckv, kr, g_qa, g_kva, w_q, w_kv, tables, seq, tm):
    t = cq.shape[0]
    nblk = seq // tm

    def body(cq_ref, ckv_ref, kr_ref, gq_ref, gk_ref, wq_ref, wkv_ref, c_ref, sa_ref, sb_ref, q_out, k_out, v_out):
        cos_t, sin_a, sin_b = c_ref[...], sa_ref[...], sb_ref[...]
        cqn = _rms(cq_ref[...], gq_ref[...]).astype(BF16)
        ckn = _rms(ckv_ref[...], gk_ref[...]).astype(BF16)
        kr_rot = _rope(kr_ref[...], cos_t, sin_a, sin_b).astype(BF16)
        for h in range(B_HEADS):
            lo = h * QK_PAD
            q_out[:, lo:lo + 128] = (_dot_nt(cqn, wq_ref[lo:lo + 128, :]) * ATTN_SCALE).astype(BF16)
            qr = _rope(_dot_nt(cqn, wq_ref[lo + 128:lo + 256, :]), cos_t, sin_a, sin_b)
            q_out[:, lo + 128:lo + 256] = (qr * ATTN_SCALE).astype(BF16)
            k_out[:, lo:lo + 128] = _dot(ckn, wkv_ref[:, lo:lo + 128]).astype(BF16)
            k_out[:, lo + 128:lo + 256] = kr_rot
            v_out[:, h * B_V:(h + 1) * B_V] = _dot(ckn, wkv_ref[:, lo + 128:lo + 256]).astype(BF16)

    tok = lambda wd: pl.BlockSpec((tm, wd), lambda i: (i, 0))
    tab = pl.BlockSpec((tm, 128), lambda i: (i % nblk, 0))
    return pl.pallas_call(
        body, name="mla_qkv_fwd", grid=(t // tm,),
        out_shape=[jax.ShapeDtypeStruct((t, B_HEADS * QK_PAD), BF16), jax.ShapeDtypeStruct((t, B_HEADS * QK_PAD), BF16),
                   jax.ShapeDtypeStruct((t, B_HEADS * B_V), BF16)],
        in_specs=[tok(Q_LORA), tok(KV_LORA), tok(128), _const_spec((1, Q_LORA)), _const_spec((1, KV_LORA)),
                  _const_spec((B_HEADS * QK_PAD, Q_LORA)), _const_spec((KV_LORA, 1024)), tab, tab, tab],
        out_specs=[tok(B_HEADS * QK_PAD), tok(B_HEADS * QK_PAD), tok(B_HEADS * B_V)],
        compiler_params=_params(),
    )(cq, ckv, kr, g_qa, g_kva, w_q, w_kv, *tables)


def _step_index(nq):
    return (pl.program_id(0) * B_HEADS + pl.program_id(1)) * nq + pl.program_id(2)


def _attn_fwd(qcat, kcat, v, nb, seq, tq, gather=()):
    t = qcat.shape[0]
    nq = seq // tq
    ng = len(gather)
    steps = nb * B_HEADS * nq

    def body(q_ref, k_ref, v_ref, *rest):
        o_ref, lse_ref = rest[ng:ng + 2]
        if ng:
            start, forward, finish = _gather_protocol(rest[:ng], rest[ng + 2:2 * ng + 2], *rest[2 * ng + 2:])
            pl.when(_step_index(nq) == 0)(start)
            pl.when(_step_index(nq) == (3 * steps) // 4)(forward)
        for j in range(tq // ATTN_SUB):
            r = pl.ds(j * ATTN_SUB, ATTN_SUB)
            s = _dot_nt(q_ref[r, :], k_ref[...])
            m = jnp.max(s, axis=-1, keepdims=True)
            p = jnp.exp(s - m)
            l = jnp.sum(p, axis=-1, keepdims=True)
            o_ref[r, :] = _dot(p.astype(BF16), v_ref[...]) / l
            lse_ref[0, r, :] = m + jnp.log(l)
        if ng:
            pl.when(_step_index(nq) == steps - 1)(finish)

    any_spec = pl.BlockSpec(memory_space=pl.ANY)
    return pl.pallas_call(
        body, name="attn_fwd", grid=(nb, B_HEADS, nq),
        out_shape=[jax.ShapeDtypeStruct((t, B_HEADS * B_V), F32), jax.ShapeDtypeStruct((B_HEADS, t, 1), F32)] + _slot_shapes(gather),
        in_specs=[pl.BlockSpec((tq, QK_PAD), lambda b, h, i: (b * nq + i, h)),
                  pl.BlockSpec((seq, QK_PAD), lambda b, h, i: (b, h)),
                  pl.BlockSpec((seq, B_V), lambda b, h, i: (b, h))] + [any_spec] * ng,
        out_specs=[pl.BlockSpec((tq, B_V), lambda b, h, i: (b * nq + i, h)),
                   pl.BlockSpec((1, tq, 1), lambda b, h, i: (h, b * nq + i, 0))] + [any_spec] * ng,
        scratch_shapes=_comm_sems(ng) if ng else [],
        compiler_params=_params(),
    )(qcat, kcat, v, *gather)


def _attn_bwd(qcat, kcat, v, o, lse, do, nb, seq, tq, exchange=()):
    t = qcat.shape[0]
    nq = seq // tq
    ne = len(exchange)
    steps = nb * B_HEADS * nq

    def body(q_ref, k_ref, v_ref, o_ref, lse_ref, do_ref, *rest):
        dq_ref, dk_ref, dv_ref = rest[ne:ne + 3]
        if ne:
            start, finish = _exchange_protocol(rest[:ne], rest[ne + 3:2 * ne + 3], [True] * ne, *rest[2 * ne + 3:])
            pl.when(_step_index(nq) == 0)(start)

        @pl.when(pl.program_id(2) == 0)
        def _():
            dv_ref[...] = jnp.zeros_like(dv_ref)
            dk_ref[...] = jnp.zeros_like(dk_ref)

        for j in range(tq // ATTN_SUB_BWD):
            r = pl.ds(j * ATTN_SUB_BWD, ATTN_SUB_BWD)
            q, k = q_ref[r, :], k_ref[...]
            do_f = do_ref[r, :]
            delta = jnp.sum(do_f * o_ref[r, :], axis=-1, keepdims=True)
            dob = do_f.astype(BF16)
            p = jnp.exp(_dot_nt(q, k) - lse_ref[0, r, :])
            ds = (p * (_dot_nt(dob, v_ref[...]) - delta)).astype(BF16)
            dq_ref[r, :] = _dot(ds, k).astype(dq_ref.dtype)
            dv_ref[...] += _dot_tn(p.astype(BF16), dob)
            dk_ref[...] += _dot_tn(ds, q)
        if ne:
            pl.when(_step_index(nq) == steps - 1)(finish)

    qspec = lambda wd: pl.BlockSpec((tq, wd), lambda b, h, i: (b * nq + i, h))
    kspec = lambda wd: pl.BlockSpec((seq, wd), lambda b, h, i: (b, h))
    any_spec = pl.BlockSpec(memory_space=pl.ANY)
    return pl.pallas_call(
        body, name="attn_bwd", grid=(nb, B_HEADS, nq),
        out_shape=[jax.ShapeDtypeStruct((t, B_HEADS * QK_PAD), BF16), jax.ShapeDtypeStruct((t, B_HEADS * QK_PAD), F32),
                   jax.ShapeDtypeStruct((t, B_HEADS * B_V), F32)] + _slot_shapes(exchange, [True] * ne),
        in_specs=[qspec(QK_PAD), kspec(QK_PAD), kspec(B_V), qspec(B_V),
                  pl.BlockSpec((1, tq, 1), lambda b, h, i: (h, b * nq + i, 0)), qspec(B_V)] + [any_spec] * ne,
        out_specs=[qspec(QK_PAD), kspec(QK_PAD), kspec(B_V)] + [any_spec] * ne,
        scratch_shapes=_comm_sems(ne) if ne else [],
        compiler_params=_params(),
    )(qcat, kcat, v, o, lse, do, *exchange)


def _gla_consts(reverse):
    row = lax.broadcasted_iota(jnp.int32, (CHUNK, CHUNK), 0)
    col = lax.broadcasted_iota(jnp.int32, (CHUNK, CHUNK), 1)
    causal = (row <= col) if reverse else (row >= col)
    lane = lax.broadcasted_iota(jnp.int32, (1, HEAD_PAIR), 1)
    m0 = (lane < 64).astype(F32)
    m1 = 1.0 - m0
    r2 = lax.broadcasted_iota(jnp.int32, (HEAD_PAIR, HEAD_PAIR), 0)
    c2 = lax.broadcasted_iota(jnp.int32, (HEAD_PAIR, HEAD_PAIR), 1)
    same_head = ((r2 < 64) == (c2 < 64)).astype(F32)
    return causal, m0, m1, same_head


def _gla_chunk(hq, hi, z, l0, l1, st, consts, reverse):
    q_dec, k_inv, k_end, decay = _gla_gates(hq, z, l0, l1, reverse)
    o, st_new = _gla_state(q_dec, st, decay, _gla_increment(hi, k_end, consts))
    return o + _gla_intra(q_dec, k_inv, hi, consts), st_new


def _gla_gates(hq, z, l0, l1, reverse):
    mx = jnp.maximum(l0, l1)
    e0, e1 = jnp.exp(l0 - mx), jnp.exp(l1 - mx)
    lb = e0 / (e0 + e1)
    q = hq * _sigmoid(hq)
    sz = _sigmoid(z)
    log_f = jnp.log(lb + (1.0 - lb) * sz)
    k = (1.0 - lb) * (1.0 - sz)
    cum = _cumsum_rows(log_f, reverse)
    decay = jnp.exp(jnp.sum(log_f, axis=0, keepdims=True))
    k_inv = k * jnp.exp(-cum)
    return q * jnp.exp(cum), k_inv, k_inv * decay, decay


def _gla_intra(q_dec, k_inv, hi, consts):
    causal, m0, m1, _ = consts
    o = None
    for mh in (m0, m1):
        s = jnp.where(causal, _mm_nt(q_dec * mh, k_inv), 0.0)
        part = _mm(s, hi) * mh
        o = part if o is None else o + part
    return o


def _gla_increment(hi, k_end, consts):
    return _mm_tn(hi, k_end) * consts[3]


def _gla_state(q_dec, st, decay, inc):
    return _mm_nt(q_dec, st), st * decay + inc


GLA_DIRS = (False, True)
GLA_BATCH_FWD = 8
GLA_BATCH_BWD = 4


def _gla_fwd(hq, hi, zs, lbls, nb, seq, group):
    t = hq.shape[0]
    rows = group * CHUNK
    nblk = seq // rows
    n_chunks = seq // CHUNK
    nd = len(GLA_DIRS)

    def body(*refs):
        ins, outs, st_refs = refs[:4 * nd], refs[4 * nd:6 * nd], refs[6 * nd:]
        @pl.when(pl.program_id(2) == 0)
        def _():
            for st_ref in st_refs:
                st_ref[...] = jnp.zeros_like(st_ref)

        consts = [_gla_consts(rev) for rev in GLA_DIRS]
        work = [(d, rev, group - 1 - cc if rev else cc) for cc in range(group) for d, rev in enumerate(GLA_DIRS)]
        rows_of = lambda c: pl.ds(c * CHUNK, CHUNK)
        sts = [st_ref[...] for st_ref in st_refs]
        for w0 in range(0, len(work), GLA_BATCH_FWD):
            batch = work[w0:w0 + GLA_BATCH_FWD]
            gates, intra, incs = {}, {}, {}
            for d, rev, c in batch:
                hq_ref, _, z_ref, lbl_ref = ins[4 * d:4 * d + 4]
                gates[d, c] = _gla_gates(hq_ref[rows_of(c), :], z_ref[rows_of(c), :], lbl_ref[0:1, :], lbl_ref[1:2, :], rev)
            for d, rev, c in batch:
                hi_c = ins[4 * d + 1][rows_of(c), :]
                intra[d, c] = _gla_intra(gates[d, c][0], gates[d, c][1], hi_c, consts[d])
                incs[d, c] = _gla_increment(hi_c, gates[d, c][2], consts[d])
            for d, rev, c in batch:
                outs[nd + d][0, 0, c] = sts[d]
                o_state, sts[d] = _gla_state(gates[d, c][0], sts[d], gates[d, c][3], incs[d, c])
                outs[d][rows_of(c), :] = intra[d, c] + o_state
        for st_ref, st in zip(st_refs, sts):
            st_ref[...] = st

    def tb(rev):
        return (lambda i: nblk - 1 - i) if rev else (lambda i: i)

    tok = lambda rev: pl.BlockSpec((rows, HEAD_PAIR), lambda b, p, i: (b * nblk + tb(rev)(i), p))
    lspec = pl.BlockSpec((2, HEAD_PAIR), lambda b, p, i: (0, p))
    sspec = lambda rev: pl.BlockSpec((1, 1, group, HEAD_PAIR, HEAD_PAIR), lambda b, p, i: (b, p, tb(rev)(i), 0, 0))
    args, in_specs = [], []
    for d, rev in enumerate(GLA_DIRS):
        args += [hq, hi, zs[d], lbls[d]]
        in_specs += [tok(rev), tok(rev), tok(rev), lspec]
    return pl.pallas_call(
        body, name="gla_fwd", grid=(nb, 4, nblk),
        out_shape=[jax.ShapeDtypeStruct((t, A_WIDTH), F32)] * nd
        + [jax.ShapeDtypeStruct((nb, 4, n_chunks, HEAD_PAIR, HEAD_PAIR), F32)] * nd,
        in_specs=in_specs, out_specs=[tok(rev) for rev in GLA_DIRS] + [sspec(rev) for rev in GLA_DIRS],
        scratch_shapes=[pltpu.VMEM((HEAD_PAIR, HEAD_PAIR), F32)] * nd,
        compiler_params=_params(),
    )(*args)


def _gla_bwd(hq, hi, zs, lbls, saved, do, nb, seq, group):
    t = hq.shape[0]
    rows = group * CHUNK
    nblk = seq // rows
    nd = len(GLA_DIRS)

    def body(*refs):
        ins, outs, dst_refs = refs[:6 * nd], refs[6 * nd:10 * nd], refs[10 * nd:]
        dl_refs = outs[3 * nd:]

        @pl.when(pl.program_id(2) == 0)
        def _():
            for dst_ref, dl_ref in zip(dst_refs, dl_refs):
                dst_ref[...] = jnp.zeros_like(dst_ref)
                dl_ref[...] = jnp.zeros_like(dl_ref)

        consts = [_gla_consts(rev) for rev in GLA_DIRS]
        dsts = [dst_ref[...] for dst_ref in dst_refs]
        dls = [[jnp.zeros((1, HEAD_PAIR), F32), jnp.zeros((1, HEAD_PAIR), F32)] for _ in GLA_DIRS]
        work = [(d, rev, cc if rev else group - 1 - cc) for cc in range(group) for d, rev in enumerate(GLA_DIRS)]
        for w0 in range(0, len(work), GLA_BATCH_BWD):
            vjps = {}
            for d, rev, c in work[w0:w0 + GLA_BATCH_BWD]:
                hq_ref, hi_ref, z_ref, lbl_ref, save_ref, _ = ins[6 * d:6 * d + 6]
                r = pl.ds(c * CHUNK, CHUNK)
                fn = functools.partial(_gla_chunk, consts=consts[d], reverse=rev)
                _, vjps[d, c] = jax.vjp(fn, hq_ref[r, :], hi_ref[r, :], z_ref[r, :], lbl_ref[0:1, :], lbl_ref[1:2, :], save_ref[0, 0, c])
            for d, rev, c in work[w0:w0 + GLA_BATCH_BWD]:
                dq_ref, dv_ref, dz_ref = outs[3 * d:3 * d + 3]
                r = pl.ds(c * CHUNK, CHUNK)
                d_hq, d_hi, d_z, d_l0, d_l1, dsts[d] = vjps[d, c]((ins[6 * d + 5][r, :], dsts[d]))
                dq_ref[r, :] = d_hq.astype(dq_ref.dtype)
                dv_ref[r, :] = d_hi.astype(dv_ref.dtype)
                dz_ref[r, :] = d_z.astype(dz_ref.dtype)
                dls[d] = [dls[d][0] + d_l0, dls[d][1] + d_l1]
        for d in range(nd):
            dst_refs[d][...] = dsts[d]
            dl_refs[d][0, 0:1, :] += dls[d][0]
            dl_refs[d][0, 1:2, :] += dls[d][1]

    def tb(rev):
        return (lambda i: i) if rev else (lambda i: nblk - 1 - i)

    tok = lambda rev: pl.BlockSpec((rows, HEAD_PAIR), lambda b, p, i: (b * nblk + tb(rev)(i), p))
    lspec = pl.BlockSpec((2, HEAD_PAIR), lambda b, p, i: (0, p))
    sspec = lambda rev: pl.BlockSpec((1, 1, group, HEAD_PAIR, HEAD_PAIR), lambda b, p, i: (b, p, tb(rev)(i), 0, 0))
    args, in_specs, out_specs = [], [], []
    for d, rev in enumerate(GLA_DIRS):
        args += [hq, hi, zs[d], lbls[d], saved[d], do]
        in_specs += [tok(rev), tok(rev), tok(rev), lspec, sspec(rev), tok(rev)]
        out_specs += [tok(rev)] * 3
    out_specs += [pl.BlockSpec((1, 2, HEAD_PAIR), lambda b, p, i: (b, 0, p))] * nd
    return pl.pallas_call(
        body, name="gla_bwd", grid=(nb, 4, nblk),
        out_shape=[jax.ShapeDtypeStruct((t, A_WIDTH), BF16)] * (3 * nd) + [jax.ShapeDtypeStruct((nb, 2, A_WIDTH), F32)] * nd,
        in_specs=in_specs, out_specs=out_specs,
        scratch_shapes=[pltpu.VMEM((HEAD_PAIR, HEAD_PAIR), F32)] * nd,
        compiler_params=_params(),
    )(*args)


def _head_mean_matrix():
    r = lax.broadcasted_iota(jnp.int32, (A_WIDTH, A_WIDTH), 0) // 64
    c = lax.broadcasted_iota(jnp.int32, (A_WIDTH, A_WIDTH), 1) // 64
    return jnp.where(r == c, 1.0 / 64.0, 0.0).astype(BF16)


def _gla_out(o_f, o_b, hg, g, mean_mat):
    o = o_f + o_b
    ms = _group_mean(o * o, mean_mat)
    return o * lax.rsqrt(ms + EPS) * g * (hg * _sigmoid(hg))


def _gla_combine(o_f, o_b, hg, g, tm):
    t = o_f.shape[0]

    def body(of_ref, ob_ref, hg_ref, g_ref, y_ref):
        y_ref[...] = _gla_out(of_ref[...], ob_ref[...], hg_ref[...], g_ref[...], _head_mean_matrix())

    tok = pl.BlockSpec((tm, A_WIDTH), lambda i: (i, 0))
    return pl.pallas_call(
        body, name="gla_combine_fwd", grid=(t // tm,), out_shape=jax.ShapeDtypeStruct((t, A_WIDTH), F32),
        in_specs=[tok, tok, tok, _const_spec((1, A_WIDTH))], out_specs=tok, compiler_params=_params(),
    )(o_f, o_b, hg, g)


def _gla_combine_bwd(o_f, o_b, hg, g, dy, tm):
    t = o_f.shape[0]

    def body(of_ref, ob_ref, hg_ref, g_ref, dy_ref, do_ref, dhg_ref, dg_ref):
        mean_mat = _head_mean_matrix()
        fn = lambda o, hgv, gv: _gla_out(o, jnp.zeros_like(o), hgv, gv, mean_mat)
        _, vjp = jax.vjp(fn, of_ref[...] + ob_ref[...], hg_ref[...], g_ref[...])
        d_o, d_hg, d_g = vjp(dy_ref[...])
        do_ref[...] = d_o
        dhg_ref[...] = d_hg.astype(dhg_ref.dtype)

        @pl.when(pl.program_id(0) == 0)
        def _():
            dg_ref[...] = jnp.zeros_like(dg_ref)

        dg_ref[...] += d_g

    tok = pl.BlockSpec((tm, A_WIDTH), lambda i: (i, 0))
    vec = pl.BlockSpec((1, A_WIDTH), lambda i: (0, 0))
    return pl.pallas_call(
        body, name="gla_combine_bwd", grid=(t // tm,),
        out_shape=[jax.ShapeDtypeStruct((t, A_WIDTH), F32), jax.ShapeDtypeStruct((t, A_WIDTH), BF16),
                   jax.ShapeDtypeStruct((1, A_WIDTH), F32)],
        in_specs=[tok, tok, tok, _const_spec((1, A_WIDTH)), tok], out_specs=[tok, tok, vec], compiler_params=_params(),
    )(o_f, o_b, hg, g, dy)


def _post_fwd(x, ya, oattn, tgt, g_mla, w_out, g2, w_gate, w_up, w_down, g_fin, tm):
    t = x.shape[0]

    def body(x_ref, ya_ref, oa_ref, tgt_ref, gm_ref, wo_ref, g2_ref, wg_ref, wu_ref, wd_ref, gf_ref,
             x1_ref, x2_ref, gate_ref, up_ref, loss_ref):
        part = jnp.zeros((1, 1), F32)
        for j in range(tm // min(tm, ROW_SUB)):
            r = pl.ds(j * min(tm, ROW_SUB), min(tm, ROW_SUB))
            yb = _rms(oa_ref[r, :], gm_ref[...])
            x1 = x_ref[r, :] + _dot(ya_ref[r, :].astype(BF16), wo_ref[0:A_WIDTH, :]) + _dot(yb.astype(BF16), wo_ref[A_WIDTH:, :])
            x1_ref[r, :] = x1
            h2 = _rms(x1, g2_ref[...]).astype(BF16)
            gate, up = _dot_nt(h2, wg_ref[...]), _dot_nt(h2, wu_ref[...])
            gate_ref[r, :] = gate.astype(BF16)
            up_ref[r, :] = up.astype(BF16)
            act = (gate * _sigmoid(gate) * up).astype(BF16)
            x2 = x1 + _dot(act, wd_ref[...])
            x2_ref[r, :] = x2
            err = _rms(x2, gf_ref[...]) - tgt_ref[r, :]
            part = part + 0.5 * jnp.sum(jnp.mean(err * err, axis=-1, keepdims=True), axis=0, keepdims=True)

        @pl.when(pl.program_id(0) == 0)
        def _():
            loss_ref[...] = jnp.zeros_like(loss_ref)

        loss_ref[...] += jnp.broadcast_to(part, loss_ref.shape)

    tok = lambda wd: pl.BlockSpec((tm, wd), lambda i: (i, 0))
    return pl.pallas_call(
        body, name="post_fwd", grid=(t // tm,),
        out_shape=[jax.ShapeDtypeStruct((t, D_MODEL), F32)] * 2 + [jax.ShapeDtypeStruct((t, D_FF), BF16)] * 2
        + [jax.ShapeDtypeStruct((1, 128), F32)],
        in_specs=[tok(D_MODEL), tok(A_WIDTH), tok(512), tok(D_MODEL), _const_spec((1, 512)), _const_spec((D_MODEL, D_MODEL)),
                  _const_spec((1, D_MODEL)), _const_spec((D_FF, D_MODEL)), _const_spec((D_FF, D_MODEL)),
                  _const_spec((D_FF, D_MODEL)), _const_spec((1, D_MODEL))],
        out_specs=[tok(D_MODEL), tok(D_MODEL), tok(D_FF), tok(D_FF), pl.BlockSpec((1, 128), lambda i: (0, 0))],
        compiler_params=_params(),
    )(x, ya, oattn, tgt, g_mla, w_out, g2, w_gate, w_up, w_down, g_fin)


def _post_bwd(x1, x2, gate_b, up_b, ya, oattn, tgt, g_mla, w_out, g2, w_gate, w_up, w_down, g_fin, tm):
    t = x1.shape[0]

    def body(x1_ref, x2_ref, gate_ref, up_ref, ya_ref, oa_ref, tgt_ref, gm_ref, wo_ref, g2_ref, wg_ref, wu_ref, wd_ref, gf_ref,
             dx1_ref, dya_ref, doa_ref, ycat_ref, dx1b_ref, h2_ref, dgate_ref, dup_ref, act_ref, dx2b_ref,
             dgm_ref, dg2_ref, dgf_ref):
        x1, x2 = x1_ref[...], x2_ref[...]
        dy = (_rms(x2, gf_ref[...]) - tgt_ref[...]) * (1.0 / D_MODEL)
        dx2, dgf = _rms_bwd(x2, gf_ref[...], dy)
        dx2b = dx2.astype(BF16)
        dx2b_ref[...] = dx2b
        h2_ref[...] = _rms(x1, g2_ref[...]).astype(BF16)
        gate, up = gate_ref[...].astype(F32), up_ref[...].astype(F32)
        sg = _sigmoid(gate)
        sl = gate * sg
        act_ref[...] = (sl * up).astype(BF16)
        dact = _dot_nt(dx2b, wd_ref[...])
        dup = (dact * sl).astype(BF16)
        dgate = (dact * up * (sg * (1.0 + gate * (1.0 - sg)))).astype(BF16)
        dup_ref[...] = dup
        dgate_ref[...] = dgate
        dh2 = _dot(dgate, wg_ref[...]) + _dot(dup, wu_ref[...])
        dx1n, dg2 = _rms_bwd(x1, g2_ref[...], dh2)
        dx1 = dx2 + dx1n
        dx1_ref[...] = dx1
        dx1b = dx1.astype(BF16)
        dx1b_ref[...] = dx1b
        oa = oa_ref[...]
        ycat_ref[:, 0:A_WIDTH] = ya_ref[...].astype(BF16)
        ycat_ref[:, A_WIDTH:] = _rms(oa, gm_ref[...]).astype(BF16)
        dya_ref[...] = _dot_nt(dx1b, wo_ref[0:A_WIDTH, :])
        doa, dgm = _rms_bwd(oa, gm_ref[...], _dot_nt(dx1b, wo_ref[A_WIDTH:, :]))
        doa_ref[...] = doa

        @pl.when(pl.program_id(0) == 0)
        def _():
            dgm_ref[...] = jnp.zeros_like(dgm_ref)
            dg2_ref[...] = jnp.zeros_like(dg2_ref)
            dgf_ref[...] = jnp.zeros_like(dgf_ref)

        dgm_ref[...] += dgm
        dg2_ref[...] += dg2
        dgf_ref[...] += dgf

    tok = lambda wd: pl.BlockSpec((tm, wd), lambda i: (i, 0))
    vec = lambda wd: pl.BlockSpec((1, wd), lambda i: (0, 0))
    sds = lambda wd, dt: jax.ShapeDtypeStruct((t, wd), dt)
    return pl.pallas_call(
        body, name="post_bwd", grid=(t // tm,),
        out_shape=[sds(D_MODEL, F32), sds(512, F32), sds(512, F32), sds(D_MODEL, BF16), sds(D_MODEL, BF16), sds(D_MODEL, BF16),
                   sds(D_FF, BF16), sds(D_FF, BF16), sds(D_FF, BF16), sds(D_MODEL, BF16),
                   jax.ShapeDtypeStruct((1, 512), F32), jax.ShapeDtypeStruct((1, D_MODEL), F32), jax.ShapeDtypeStruct((1, D_MODEL), F32)],
        in_specs=[tok(D_MODEL), tok(D_MODEL), tok(D_FF), tok(D_FF), tok(512), tok(512), tok(D_MODEL), _const_spec((1, 512)),
                  _const_spec((D_MODEL, D_MODEL)), _const_spec((1, D_MODEL)), _const_spec((D_FF, D_MODEL)),
                  _const_spec((D_FF, D_MODEL)), _const_spec((D_FF, D_MODEL)), _const_spec((1, D_MODEL))],
        out_specs=[tok(D_MODEL), tok(512), tok(512), tok(D_MODEL), tok(D_MODEL), tok(D_MODEL), tok(D_FF), tok(D_FF), tok(D_FF),
                   tok(D_MODEL), vec(512), vec(D_MODEL), vec(D_MODEL)],
        compiler_params=_params(),
    )(x1, x2, gate_b, up_b, ya, oattn, tgt, g_mla, w_out, g2, w_gate, w_up, w_down, g_fin)


def _matmul_tn(a, b, tn, tt, tag):
    t, k = a.shape
    n = b.shape[1]
    last = t // tt - 1

    def body(a_ref, b_ref, o_ref, acc_ref):
        part = _dot_tn(a_ref[...], b_ref[...])

        @pl.when(pl.program_id(1) == 0)
        def _():
            acc_ref[...] = part

        @pl.when(pl.program_id(1) > 0)
        def _():
            acc_ref[...] += part

        @pl.when(pl.program_id(1) == last)
        def _():
            o_ref[...] = acc_ref[...].astype(o_ref.dtype)

    return pl.pallas_call(
        body, name="wgrad_" + tag, grid=(n // tn, t // tt), out_shape=jax.ShapeDtypeStruct((k, n), BF16),
        in_specs=[pl.BlockSpec((tt, k), lambda j, i: (i, 0)), pl.BlockSpec((tt, tn), lambda j, i: (i, j))],
        out_specs=pl.BlockSpec((k, tn), lambda j, i: (0, j)), scratch_shapes=[pltpu.VMEM((k, tn), F32)],
        compiler_params=_params(),
    )(a, b)


def _mla_qkv_bwd(cq, ckv, g_qa, g_kva, w_q, w_kv, tables, dq, dk, dv, seq, tm):
    t = cq.shape[0]
    nblk = seq // tm

    def body(cq_ref, ckv_ref, gq_ref, gk_ref, wq_ref, wkv_ref, c_ref, sa_ref, sb_ref, dq_ref, dk_ref, dv_ref,
             dcq_ref, dckv_ref, dkr_ref, cqn_ref, dqf_ref, ckn_ref, dkv_ref, dgq_ref, dgk_ref):
        cos_t, sin_a, sin_b = c_ref[...], sa_ref[...], sb_ref[...]
        cqn_ref[...] = _rms(cq_ref[...], gq_ref[...]).astype(BF16)
        ckn_ref[...] = _rms(ckv_ref[...], gk_ref[...]).astype(BF16)
        dkr = jnp.zeros((tm, 128), F32)
        for h in range(B_HEADS):
            lo = h * QK_PAD
            dqf_ref[:, lo:lo + 128] = (dq_ref[:, lo:lo + 128].astype(F32) * ATTN_SCALE).astype(BF16)
            dq_rope = dq_ref[:, lo + 128:lo + 256].astype(F32) * ATTN_SCALE
            dqf_ref[:, lo + 128:lo + 256] = _rope_t(dq_rope, cos_t, sin_a, sin_b).astype(BF16)
            dkv_ref[:, lo:lo + 128] = dk_ref[:, lo:lo + 128].astype(BF16)
            dkv_ref[:, lo + 128:lo + 256] = dv_ref[:, h * B_V:(h + 1) * B_V].astype(BF16)
            dkr = dkr + dk_ref[:, lo + 128:lo + 256]
        dkr_ref[...] = _rope_t(dkr, cos_t, sin_a, sin_b).astype(dkr_ref.dtype)
        dcq, dgq = _rms_bwd(cq_ref[...], gq_ref[...], _dot(dqf_ref[...], wq_ref[...]))
        dckv, dgk = _rms_bwd(ckv_ref[...], gk_ref[...], _dot_nt(dkv_ref[...], wkv_ref[...]))
        dcq_ref[...] = dcq.astype(dcq_ref.dtype)
        dckv_ref[...] = dckv.astype(dckv_ref.dtype)

        @pl.when(pl.program_id(0) == 0)
        def _():
            dgq_ref[...] = jnp.zeros_like(dgq_ref)
            dgk_ref[...] = jnp.zeros_like(dgk_ref)

        dgq_ref[...] += dgq
        dgk_ref[...] += dgk

    tok = lambda wd: pl.BlockSpec((tm, wd), lambda i: (i, 0))
    vec = lambda wd: pl.BlockSpec((1, wd), lambda i: (0, 0))
    tab = pl.BlockSpec((tm, 128), lambda i: (i % nblk, 0))
    sds = lambda wd, dt: jax.ShapeDtypeStruct((t, wd), dt)
    return pl.pallas_call(
        body, name="mla_qkv_bwd", grid=(t // tm,),
        out_shape=[sds(Q_LORA, BF16), sds(KV_LORA, BF16), sds(128, BF16), sds(Q_LORA, BF16), sds(1024, BF16), sds(KV_LORA, BF16),
                   sds(1024, BF16), jax.ShapeDtypeStruct((1, Q_LORA), F32), jax.ShapeDtypeStruct((1, KV_LORA), F32)],
        in_specs=[tok(Q_LORA), tok(KV_LORA), _const_spec((1, Q_LORA)), _const_spec((1, KV_LORA)),
                  _const_spec((1024, Q_LORA)), _const_spec((KV_LORA, 1024)), tab, tab, tab,
                  tok(1024), tok(1024), tok(512)],
        out_specs=[tok(Q_LORA), tok(KV_LORA), tok(128), tok(Q_LORA), tok(1024), tok(KV_LORA), tok(1024),
                   vec(Q_LORA), vec(KV_LORA)],
        compiler_params=_params(),
    )(cq, ckv, g_qa, g_kva, w_q, w_kv, *tables, dq, dk, dv)


def _inproj_bwd(x, g1, w_in, dx1, pieces, tm):
    t = x.shape[0]
    counts = [len(p) for p in pieces]
    flat = [a for p in pieces for a in p]
    widths = [wd for wd, p in zip(IN_WIDTHS, pieces) for _ in p]

    def body(x_ref, g_ref, w_ref, dx1_ref, *refs):
        ins = refs[:len(flat)]
        dx_ref, h_ref, dp_ref, dg_ref = refs[len(flat):]
        xv = x_ref[...]
        h_ref[...] = _rms(xv, g_ref[...]).astype(BF16)
        off, j = 0, 0
        for wd, cnt in zip(IN_WIDTHS, counts):
            acc = ins[j][...].astype(F32)
            for jj in range(1, cnt):
                acc = acc + ins[j + jj][...].astype(F32)
            dp_ref[:, off:off + wd] = acc.astype(BF16)
            off += wd
            j += cnt
        dxn, dg = _rms_bwd(xv, g_ref[...], _dot(dp_ref[...], w_ref[...]))
        dx_ref[...] = dx1_ref[...] + dxn

        @pl.when(pl.program_id(0) == 0)
        def _():
            dg_ref[...] = jnp.zeros_like(dg_ref)

        dg_ref[...] += dg

    tok = lambda wd: pl.BlockSpec((tm, wd), lambda i: (i, 0))
    return pl.pallas_call(
        body, name="inproj_bwd", grid=(t // tm,),
        out_shape=[jax.ShapeDtypeStruct((t, D_MODEL), F32), jax.ShapeDtypeStruct((t, D_MODEL), BF16),
                   jax.ShapeDtypeStruct((t, D_IN_PAD), BF16), jax.ShapeDtypeStruct((1, D_MODEL), F32)],
        in_specs=[tok(D_MODEL), _const_spec((1, D_MODEL)), _const_spec((D_IN_PAD, D_MODEL)), tok(D_MODEL)] + [tok(wd) for wd in widths],
        out_specs=[tok(D_MODEL), tok(D_MODEL), tok(D_IN_PAD), pl.BlockSpec((1, D_MODEL), lambda i: (0, 0))],
        compiler_params=_params(),
    )(x, g1, w_in, dx1, *flat)


def _cols_from_slots(g):
    n, r, cs = g.shape
    return g.transpose(1, 0, 2).reshape(r, n * cs)


def _cols_to_slots(full):
    r, c = full.shape
    return full.reshape(r, N_DEV, c // N_DEV).transpose(1, 0, 2)


def _arrange_w_in_t(w_in_t):
    return jnp.concatenate([w_in_t, jnp.zeros((D_IN_PAD - D_IN, D_MODEL), w_in_t.dtype)], axis=0)


def _arrange_w_q_t(w_q_t):
    q3 = w_q_t.reshape(B_HEADS, B_NOPE + B_ROPE, Q_LORA)
    pad = jnp.zeros((B_HEADS, QK_PAD - B_NOPE - B_ROPE, Q_LORA), w_q_t.dtype)
    return jnp.concatenate([q3, pad], axis=1).reshape(B_HEADS * QK_PAD, Q_LORA)


def _unarrange_w_q_t(d_q_t):
    return d_q_t.reshape(B_HEADS, QK_PAD, Q_LORA)[:, :B_NOPE + B_ROPE].reshape(B_HEADS * (B_NOPE + B_ROPE), Q_LORA)


def _step_core(x, loss_target, small_w, lb_full, early_full, late, seq, group, tiles, distributed):
    g1, g_hgrn, g_qa, g_kva, g_mla, g2, g_fin = small_w
    w_in, w_q, w_kv = _arrange_w_in_t(early_full[0]), _arrange_w_q_t(early_full[1]), early_full[2]
    nb = x.shape[0]
    t = nb * seq
    tm, tm_fwd, tq_f, tq_b, tt = tiles
    xt = x.reshape(t, D_MODEL)
    tgt = loss_target.reshape(t, D_MODEL)
    tables = _rope_tables(seq)

    hq, hi, zf, zb, hg, cq, ckv, kr = _inproj(xt, g1, w_in, tm_fwd)
    qcat, kcat, vv = _mla_qkv(cq, ckv, kr, g_qa, g_kva, w_q, w_kv, tables, seq, tm)
    if distributed:
        oattn, lse, *late_slots = _attn_fwd(qcat, kcat, vv, nb, seq, tq_f, gather=tuple(late))
    else:
        oattn, lse = _attn_fwd(qcat, kcat, vv, nb, seq, tq_f)
        late_slots = late
    w_out = late_slots[0].reshape(D_MODEL, D_MODEL)
    w_gate, w_up = late_slots[1].reshape(D_FF, D_MODEL), late_slots[2].reshape(D_FF, D_MODEL)
    w_down = late_slots[3].reshape(D_FF, D_MODEL)
    lbl_f, lbl_b = lb_full[0], lb_full[1]
    o_f, o_b, save_f, save_b = _gla_fwd(hq, hi, (zf, zb), (lbl_f, lbl_b), nb, seq, group)
    ya = _gla_combine(o_f, o_b, hg, g_hgrn, tm)
    x1, x2, gate_b, up_b, loss_row = _post_fwd(xt, ya, oattn, tgt, g_mla, w_out, g2, w_gate, w_up, w_down, g_fin, tm_fwd)

    (dx1, d_ya, d_oattn, ycat_b, dx1_b, h2_b, dgate_b, dup_b, act_b, dx2_b, d_g_mla, d_g2, d_g_fin) = _post_bwd(
        x1, x2, gate_b, up_b, ya, oattn, tgt, g_mla, w_out, g2, w_gate, w_up, w_down, g_fin, tm)
    d_w_gate = _matmul_tn(dgate_b, h2_b, 512, tt, "gate")
    d_w_up = _matmul_tn(dup_b, h2_b, 512, tt, "up")
    d_w_down = _matmul_tn(act_b, dx2_b, 512, tt, "down")
    d_w_out = _matmul_tn(ycat_b, dx1_b, D_MODEL, tt, "out")
    late_g = [d_w_out.reshape(N_DEV, D_MODEL // N_DEV, D_MODEL)] + [
        g.reshape(N_DEV, D_FF // N_DEV, D_MODEL) for g in (d_w_gate, d_w_up, d_w_down)]
    if distributed:
        dq, dk, dv, *late_g = _attn_bwd(qcat, kcat, vv, oattn, lse, d_oattn, nb, seq, tq_b, exchange=tuple(late_g))
    else:
        dq, dk, dv = _attn_bwd(qcat, kcat, vv, oattn, lse, d_oattn, nb, seq, tq_b)
    (d_cq, d_ckv, d_kr, cqn_b, dqf_b, ckn_b, dkv_b, d_g_qa, d_g_kva) = _mla_qkv_bwd(
        cq, ckv, g_qa, g_kva, w_q, w_kv, tables, dq, dk, dv, seq, tm)
    d_w_q = _matmul_tn(dqf_b, cqn_b, Q_LORA, tt, "q_b")
    d_w_kv = _matmul_tn(ckn_b, dkv_b, B_HEADS * (B_NOPE + B_V), tt, "kv_b")
    d_o, d_hg, d_g_hgrn = _gla_combine_bwd(o_f, o_b, hg, g_hgrn, d_ya, tm)
    dq_f, dv_f, dz_f, dq_b, dv_b, dz_b, dl_f, dl_b = _gla_bwd(
        hq, hi, (zf, zb), (lbl_f, lbl_b), (save_f, save_b), d_o, nb, seq, group)
    grad_x, h1_b, dproj_b, d_g1 = _inproj_bwd(
        xt, g1, w_in, dx1, [[dq_f, dq_b], [dv_f, dv_b], [dz_f], [dz_b], [d_hg], [d_cq], [d_ckv], [d_kr]], tm)
    d_w_in = _matmul_tn(dproj_b, h1_b, 512, tt, "in")

    early_g = [d_w_in[:D_IN].reshape(N_DEV, D_IN // N_DEV, D_MODEL),
               _unarrange_w_q_t(d_w_q).reshape(N_DEV, 768 // N_DEV, Q_LORA), _cols_to_slots(d_w_kv)]
    d_lb = jnp.stack([jnp.sum(dl_f, axis=0), jnp.sum(dl_b, axis=0)], axis=0)
    small_grads = [d_g1, d_g_hgrn, d_g_qa, d_g_kva, d_g_mla, d_g2, d_g_fin]
    return loss_row, grad_x.reshape(nb, seq, D_MODEL), early_g, late_g, small_grads, d_lb


def kernel(x, norm1_g, w_in, lb_logits, hgrn_norm_g, q_a_norm_g, w_q_b, kv_a_norm_g, w_kv_b, mla_norm_g, w_out, norm2_g, w_gate, w_up, w_down, final_norm_g, loss_target, m_norm1_g, m_w_in, m_lb_logits, m_hgrn_norm_g, m_q_a_norm_g, m_w_q_b, m_kv_a_norm_g, m_w_kv_b, m_mla_norm_g, m_w_out, m_norm2_g, m_w_gate, m_w_up, m_w_down, m_final_norm_g, v_norm1_g, v_w_in, v_lb_logits, v_hgrn_norm_g, v_q_a_norm_g, v_w_q_b, v_kv_a_norm_g, v_w_kv_b, v_mla_norm_g, v_w_out, v_norm2_g, v_w_gate, v_w_up, v_w_down, v_final_norm_g):
    big_w = [w_in, w_q_b, w_kv_b, w_out, w_gate, w_up, w_down]
    big_m = [m_w_in, m_w_q_b, m_w_kv_b, m_w_out, m_w_gate, m_w_up, m_w_down]
    big_v = [v_w_in, v_w_q_b, v_w_kv_b, v_w_out, v_w_gate, v_w_up, v_w_down]
    small_w = [norm1_g, hgrn_norm_g, q_a_norm_g, kv_a_norm_g, mla_norm_g, norm2_g, final_norm_g]
    small_m = [m_norm1_g, m_hgrn_norm_g, m_q_a_norm_g, m_kv_a_norm_g, m_mla_norm_g, m_norm2_g, m_final_norm_g]
    small_v = [v_norm1_g, v_hgrn_norm_g, v_q_a_norm_g, v_kv_a_norm_g, v_mla_norm_g, v_norm2_g, v_final_norm_g]
    seq = x.shape[1]
    my_id = 4 * lax.axis_index("x") + 2 * lax.axis_index("y") + lax.axis_index("c")

    shard = lambda w: w[0].astype(BF16)
    col_t = lambda w: jnp.swapaxes(w, 1, 2)[0]
    shard_t = lambda w: col_t(w).astype(BF16)
    g_in, g_q, g_kv, g_lb = _all_gather_call([shard_t(w_in), shard_t(w_q_b), shard(w_kv_b), lb_logits.reshape(4, 64)])
    early_full = (g_in.reshape(D_IN, D_MODEL), g_q.reshape(768, Q_LORA), _cols_from_slots(g_kv))
    lb_full = g_lb.reshape(N_DEV, 2, 2, 64).transpose(1, 2, 0, 3).reshape(2, 2, 512)

    as_row = lambda a: a.reshape(1, -1)
    loss_row, grad_x, early_g, late_recv, small_g, d_lb = _step_core(
        x, loss_target, [as_row(s) for s in small_w], lb_full, early_full,
        [shard(w_out), shard_t(w_gate), shard_t(w_up), shard(w_down)], seq, min(8, seq // CHUNK),
        (256, 512, min(1024, seq), min(512, seq), min(2048, 2 * seq)), True)

    n_small = len(small_g)
    recv = _exchange_call(early_g + small_g + [d_lb.reshape(4, 512), loss_row], [True] * 3 + [False] * (n_small + 2))
    sums = _sum_slots_call(recv[3:])
    g_small = [g.reshape(s.shape) for g, s in zip(sums[:n_small], small_w)]
    g_lb_own = lax.dynamic_index_in_dim(sums[n_small].reshape(2, 2, N_DEV, 64), my_id, axis=2, keepdims=False)
    loss = sums[n_small + 1][0, 0]

    grads, deltas, new_ms, new_vs = {}, {}, {}, {}
    big_recv = dict(zip(["w_in", "w_q_b", "w_kv_b", "w_out", "w_gate", "w_up", "w_down"], list(recv[:3]) + list(late_recv)))
    for (name, _, _, _), w, m, v in zip(BIG, big_w, big_m, big_v):
        transposed = name in ("w_in", "w_q_b", "w_gate", "w_up")
        view = col_t if transposed else (lambda a: a[0])
        back = (lambda a: jnp.swapaxes(a[None], 1, 2)) if transposed else (lambda a: a[None])
        g, d, nm, nv = _adamw_recv(view(w), big_recv[name], view(m), view(v), name)
        grads[name], deltas[name], new_ms[name], new_vs[name] = back(g), back(d), back(nm), back(nv)
    lb_rows = lambda a: a.reshape(4, 64)
    d_s, nm_s, nv_s = _adamw_small(
        [as_row(a) for a in small_w] + [lb_rows(lb_logits)], [as_row(a) for a in g_small] + [lb_rows(g_lb_own)],
        [as_row(a) for a in small_m] + [lb_rows(m_lb_logits)], [as_row(a) for a in small_v] + [lb_rows(v_lb_logits)])
    for i, (s, (name, _)) in enumerate(zip(small_w + [lb_logits], SMALL + (("lb_logits", 0),))):
        grads[name] = (g_small + [g_lb_own])[i]
        deltas[name], new_ms[name], new_vs[name] = d_s[i].reshape(s.shape), nm_s[i].reshape(s.shape), nv_s[i].reshape(s.shape)

    order = ["norm1_g", "w_in", "lb_logits", "hgrn_norm_g", "q_a_norm_g", "w_q_b", "kv_a_norm_g", "w_kv_b", "mla_norm_g",
             "w_out", "norm2_g", "w_gate", "w_up", "w_down", "final_norm_g"]
    return (loss, grad_x, *[grads[n] for n in order], *[deltas[n] for n in order],
            *[new_ms[n] for n in order], *[new_vs[n] for n in order])
```

```python
import functools
import math

import jax
import jax.numpy as jnp
from jax import lax
from jax.experimental import pallas as pl
from jax.experimental.pallas import tpu as pltpu

F32 = jnp.float32
BF16 = jnp.bfloat16

N_DEV = 8
D_MODEL = 1024
D_FF = 2816
A_WIDTH = 512
HEAD_PAIR = 128
CHUNK = 64
B_HEADS = 4
B_NOPE = 128
B_ROPE = 64
B_V = 128
QK_PAD = 256
Q_LORA = 384
KV_LORA = 256
D_IN = 3264
D_IN_PAD = 3328
IN_WIDTHS = (512, 512, 512, 512, 512, Q_LORA, KV_LORA, 128)
ROPE_THETA = 10000.0
EPS = 1e-6
ATTN_SCALE = (B_NOPE + B_ROPE) ** -0.5
ATTN_SUB = 256
ATTN_SUB_BWD = 256
ROW_SUB = 256
ADAM_LR, ADAM_B1, ADAM_B2, ADAM_EPS, ADAM_WD, ADAM_STEP = 0.001, 0.9, 0.999, 1e-08, 0.01, 10
VMEM_LIMIT = 60 * 1024 * 1024
MESH = pl.DeviceIdType.MESH

BIG = (("w_in", 1024, D_IN, 1), ("w_q_b", Q_LORA, 768, 1), ("w_kv_b", KV_LORA, 1024, 1), ("w_out", 1024, 1024, 0),
       ("w_gate", 1024, D_FF, 1), ("w_up", 1024, D_FF, 1), ("w_down", D_FF, 1024, 0))
SMALL = (("norm1_g", 1024), ("hgrn_norm_g", 512), ("q_a_norm_g", 384), ("kv_a_norm_g", 256), ("mla_norm_g", 512),
         ("norm2_g", 1024), ("final_norm_g", 1024))


def _params(**kw):
    return pltpu.CompilerParams(vmem_limit_bytes=VMEM_LIMIT, **kw)


def _const_spec(shape):
    return pl.BlockSpec(shape, lambda *_: (0,) * len(shape), pipeline_mode=pl.Buffered(1))


def _dot(a, b):
    return jnp.dot(a, b, preferred_element_type=F32)


def _dot_nt(a, b):
    return lax.dot_general(a, b, (((1,), (1,)), ((), ())), preferred_element_type=F32)


def _dot_tn(a, b):
    return lax.dot_general(a, b, (((0,), (0,)), ((), ())), preferred_element_type=F32)


@jax.custom_vjp
def _mm(a, b):
    return _dot(a.astype(BF16), b.astype(BF16))


def _mm_fwd(a, b):
    return _mm(a, b), (a, b)


def _mm_bwd(res, g):
    a, b = res
    gb = g.astype(BF16)
    return _dot_nt(gb, b.astype(BF16)), _dot_tn(a.astype(BF16), gb)


_mm.defvjp(_mm_fwd, _mm_bwd)


@jax.custom_vjp
def _mm_nt(a, b):
    return _dot_nt(a.astype(BF16), b.astype(BF16))


def _mm_nt_fwd(a, b):
    return _mm_nt(a, b), (a, b)


def _mm_nt_bwd(res, g):
    a, b = res
    gb = g.astype(BF16)
    return _dot(gb, b.astype(BF16)), _dot_tn(gb, a.astype(BF16))


_mm_nt.defvjp(_mm_nt_fwd, _mm_nt_bwd)


@jax.custom_vjp
def _mm_tn(a, b):
    return _dot_tn(a.astype(BF16), b.astype(BF16))


def _mm_tn_fwd(a, b):
    return _mm_tn(a, b), (a, b)


def _mm_tn_bwd(res, g):
    a, b = res
    gb = g.astype(BF16)
    return _dot_nt(b.astype(BF16), gb), _dot(a.astype(BF16), gb)


_mm_tn.defvjp(_mm_tn_fwd, _mm_tn_bwd)


def _dot_exact_rhs(a, m):
    hi = a.astype(BF16)
    lo = (a - hi.astype(F32)).astype(BF16)
    return _dot(hi, m) + _dot(lo, m)


@jax.custom_vjp
def _group_mean(a, m):
    return _dot_exact_rhs(a, m)


def _group_mean_fwd(a, m):
    return _group_mean(a, m), m


def _group_mean_bwd(m, g):
    return _dot_exact_rhs(g, m), jnp.zeros_like(m)


_group_mean.defvjp(_group_mean_fwd, _group_mean_bwd)


def _roll_rows(a, shift):
    return pltpu.roll(a, shift, 0)


def _cumsum_rows_raw(a, reverse):
    n = a.shape[0]
    row = lax.broadcasted_iota(jnp.int32, a.shape, 0)
    s = 1
    while s < n:
        if reverse:
            a = a + jnp.where(row < n - s, _roll_rows(a, n - s), 0.0)
        else:
            a = a + jnp.where(row >= s, _roll_rows(a, s), 0.0)
        s *= 2
    return a


@functools.partial(jax.custom_vjp, nondiff_argnums=(1,))
def _cumsum_rows(a, reverse):
    return _cumsum_rows_raw(a, reverse)


def _cumsum_rows_fwd(a, reverse):
    return _cumsum_rows_raw(a, reverse), None


def _cumsum_rows_bwd(reverse, _, g):
    return (_cumsum_rows_raw(g, not reverse),)


_cumsum_rows.defvjp(_cumsum_rows_fwd, _cumsum_rows_bwd)


def _rms(x, g):
    r = lax.rsqrt(jnp.mean(x * x, axis=-1, keepdims=True) + EPS)
    return x * r * g


def _rms_bwd(x, g, dy):
    r = lax.rsqrt(jnp.mean(x * x, axis=-1, keepdims=True) + EPS)
    xh = x * r
    dg = jnp.sum(dy * xh, axis=0, keepdims=True)
    dxh = dy * g
    dx = r * (dxh - xh * jnp.mean(dxh * xh, axis=-1, keepdims=True))
    return dx, dg


def _sigmoid(a):
    return jax.nn.sigmoid(a)


def _mesh_place():
    x, y, c = lax.axis_index("x"), lax.axis_index("y"), lax.axis_index("c")
    return x, y, c


def _dev_index(p):
    return 4 * p[0] + 2 * p[1] + p[2]


def _comm_sems(n):
    return [pltpu.SemaphoreType.DMA((n, 7)), pltpu.SemaphoreType.DMA((n, 7)), pltpu.SemaphoreType.DMA((n,))]


def _gather_protocol(ins, outs, send_sems, recv_sems, local_sems):
    n = len(ins)
    x, y, c = _mesh_place()
    me, sibling = (x, y, c), (x, y, 1 - c)
    chips = [(1 - x, y), (x, 1 - y), (1 - x, 1 - y)]

    def copy(a, k, block, to, src=None):
        slot = outs[a].at[_dev_index(block)]
        return pltpu.make_async_remote_copy(
            src_ref=slot if src is None else src, dst_ref=slot,
            send_sem=send_sems.at[a, k], recv_sem=recv_sems.at[a, k], device_id=to, device_id_type=MESH)

    def mine(a):
        return pltpu.make_async_copy(ins[a], outs[a].at[_dev_index(me)], local_sems.at[a])

    def first(a):
        return [copy(a, 0, me, sibling, src=ins[a])] + [copy(a, 1 + j, me, (*chip, c), src=ins[a]) for j, chip in enumerate(chips)]

    def start():
        for a in range(n):
            mine(a).start()
            for cp in first(a):
                cp.start()

    def forward():
        for a in range(n):
            for j, chip in enumerate(chips):
                copy(a, 1 + j, (*chip, c), me).wait_recv()
                copy(a, 4 + j, (*chip, c), sibling).start()

    def finish():
        for a in range(n):
            copy(a, 0, sibling, me).wait_recv()
            for j, chip in enumerate(chips):
                copy(a, 4 + j, (*chip, 1 - c), me).wait_recv()
        for a in range(n):
            mine(a).wait()
            for cp in first(a):
                cp.wait_send()
            for j, chip in enumerate(chips):
                copy(a, 4 + j, (*chip, c), sibling).wait_send()

    return start, forward, finish


def _exchange_protocol(ins, outs, scatter, send_sems, recv_sems, local_sems):
    n = len(ins)
    x, y, c = _mesh_place()
    me = (x, y, c)
    my_id = _dev_index(me)
    rels = [(dx, dy, dc) for dx in (0, 1) for dy in (0, 1) for dc in (0, 1)][1:]

    def peer_of(rel):
        return tuple(1 - v if d else v for v, d in zip(me, rel))

    def src(a, dev):
        return ins[a].at[dev] if scatter[a] else ins[a]

    def send(a, k):
        peer = peer_of(rels[k])
        return pltpu.make_async_remote_copy(
            src_ref=src(a, _dev_index(peer)), dst_ref=outs[a].at[my_id],
            send_sem=send_sems.at[a, k], recv_sem=recv_sems.at[a, k], device_id=peer, device_id_type=MESH)

    def arrival(a, k):
        peer = peer_of(rels[k])
        return pltpu.make_async_remote_copy(
            src_ref=src(a, my_id), dst_ref=outs[a].at[_dev_index(peer)],
            send_sem=send_sems.at[a, k], recv_sem=recv_sems.at[a, k], device_id=peer, device_id_type=MESH)

    def own(a):
        return pltpu.make_async_copy(src(a, my_id), outs[a].at[my_id], local_sems.at[a])

    def start():
        for a in range(n):
            own(a).start()
            for k in range(7):
                send(a, k).start()

    def finish():
        for a in range(n):
            for k in range(7):
                arrival(a, k).wait_recv()
        for a in range(n):
            for k in range(7):
                send(a, k).wait_send()
            own(a).wait()

    return start, finish


def _slot_shapes(blocks, scatter=None):
    return [jax.ShapeDtypeStruct(b.shape if (scatter and scatter[a]) else (N_DEV,) + b.shape, b.dtype) for a, b in enumerate(blocks)]


def _all_gather_call(blocks):
    n = len(blocks)

    def body(*refs):
        start, forward, finish = _gather_protocol(refs[:n], refs[n:2 * n], *refs[2 * n:])
        start()
        forward()
        finish()

    any_spec = pl.BlockSpec(memory_space=pl.ANY)
    return pl.pallas_call(
        body, name="weights_all_gather", out_shape=_slot_shapes(blocks),
        in_specs=[any_spec] * n, out_specs=[any_spec] * n, scratch_shapes=_comm_sems(n),
    )(*blocks)


def _sum_slots_call(recvs):
    n = len(recvs)

    def body(*refs):
        for in_ref, out_ref in zip(refs[:n], refs[n:]):
            acc = in_ref[0]
            for j in range(1, N_DEV):
                acc = acc + in_ref[j]
            out_ref[...] = acc

    return pl.pallas_call(
        body, name="small_grad_sum", out_shape=[jax.ShapeDtypeStruct(r.shape[1:], F32) for r in recvs],
        compiler_params=_params(),
    )(*recvs)


def _adam_update(w, g, m, v):
    nm = ADAM_B1 * m + (1.0 - ADAM_B1) * g
    nv = ADAM_B2 * v + (1.0 - ADAM_B2) * (g * g)
    bc1 = 1.0 - ADAM_B1 ** ADAM_STEP
    bc2 = 1.0 - ADAM_B2 ** ADAM_STEP
    return -ADAM_LR * ((nm / bc1) / (jnp.sqrt(nv / bc2) + ADAM_EPS) + ADAM_WD * w), nm, nv


def _adamw_recv(w, recv, m, v, tag):
    r, c = w.shape
    tr = r
    for cand in (512, 256, 128):
        if r > cand and r % cand == 0:
            tr = cand
            break

    def body(w_ref, r_ref, m_ref, v_ref, g_ref, d_ref, nm_ref, nv_ref):
        g = r_ref[0].astype(F32)
        for j in range(1, N_DEV):
            g = g + r_ref[j].astype(F32)
        g_ref[...] = g
        d_ref[...], nm_ref[...], nv_ref[...] = _adam_update(w_ref[...], g, m_ref[...], v_ref[...])

    spec = pl.BlockSpec((tr, c), lambda i: (i, 0))
    return pl.pallas_call(
        body, name="adamw_" + tag, out_shape=[jax.ShapeDtypeStruct(w.shape, F32)] * 4, grid=(r // tr,),
        in_specs=[spec, pl.BlockSpec((N_DEV, tr, c), lambda i: (0, i, 0)), spec, spec], out_specs=[spec] * 4,
        compiler_params=_params(),
    )(w, recv, m, v)


def _adamw_recv_hosting(ws, recvs, ms, vs, blocks, scatter):
    n, ne = len(ws), len(blocks)

    def body(*refs):
        ins, ex_in = refs[:4 * n], refs[4 * n:4 * n + ne]
        outs, ex_out = refs[4 * n + ne:8 * n + ne], refs[8 * n + ne:8 * n + 2 * ne]
        start, finish = _exchange_protocol(ex_in, ex_out, scatter, *refs[8 * n + 2 * ne:])
        start()
        for a in range(n):
            r_ref = ins[n + a]
            g = r_ref[0].astype(F32)
            for j in range(1, N_DEV):
                g = g + r_ref[j].astype(F32)
            outs[a][...] = g
            outs[n + a][...], outs[2 * n + a][...], outs[3 * n + a][...] = _adam_update(
                ins[a][...], g, ins[2 * n + a][...], ins[3 * n + a][...])
        finish()

    vmem, any_spec = pl.BlockSpec(memory_space=pltpu.VMEM), pl.BlockSpec(memory_space=pl.ANY)
    out = pl.pallas_call(
        body, name="adamw_late_and_grad_exchange",
        out_shape=[jax.ShapeDtypeStruct(w.shape, F32) for w in ws] * 4 + _slot_shapes(blocks, scatter),
        in_specs=[vmem] * (4 * n) + [any_spec] * ne, out_specs=[vmem] * (4 * n) + [any_spec] * ne,
        scratch_shapes=_comm_sems(ne), compiler_params=_params(),
    )(*ws, *recvs, *ms, *vs, *blocks)
    return out[:n], out[n:2 * n], out[2 * n:3 * n], out[3 * n:4 * n], out[4 * n:]


def _adamw_small(ws, gs, ms, vs):
    n = len(ws)

    def body(*refs):
        ins, outs = refs[:4 * n], refs[4 * n:]
        for a in range(n):
            d, nm, nv = _adam_update(ins[a][...], ins[n + a][...], ins[2 * n + a][...], ins[3 * n + a][...])
            outs[a][...], outs[n + a][...], outs[2 * n + a][...] = d, nm, nv

    out = pl.pallas_call(
        body, name="adamw_small", out_shape=[jax.ShapeDtypeStruct(w.shape, F32) for w in ws] * 3, compiler_params=_params(),
    )(*ws, *gs, *ms, *vs)
    return out[:n], out[n:2 * n], out[2 * n:]


def _tile(t, want):
    return want if t % want == 0 else t


def _inproj(x, g1, w_in, tm):
    t = x.shape[0]

    def body(x_ref, g_ref, w_ref, *outs):
        for j in range(tm // min(tm, ROW_SUB)):
            r = pl.ds(j * min(tm, ROW_SUB), min(tm, ROW_SUB))
            h = _rms(x_ref[r, :], g_ref[...]).astype(BF16)
            off = 0
            for o_ref, wd in zip(outs, IN_WIDTHS):
                o_ref[r, :] = _dot_nt(h, w_ref[off:off + wd, :])
                off += wd

    return pl.pallas_call(
        body, name="inproj_fwd", grid=(t // tm,),
        out_shape=[jax.ShapeDtypeStruct((t, wd), F32) for wd in IN_WIDTHS],
        in_specs=[pl.BlockSpec((tm, D_MODEL), lambda i: (i, 0)), _const_spec((1, D_MODEL)), _const_spec((D_IN_PAD, D_MODEL))],
        out_specs=[pl.BlockSpec((tm, wd), lambda i: (i, 0)) for wd in IN_WIDTHS],
        compiler_params=_params(),
    )(x, g1, w_in)


def _rope_tables(seq):
    inv = 1.0 / (ROPE_THETA ** (jnp.arange(0, B_ROPE, 2, dtype=F32) / B_ROPE))
    ang = jnp.arange(seq, dtype=F32)[:, None] * inv[None, :]
    cos, sin = jnp.cos(ang), jnp.sin(ang)
    z32, z64 = jnp.zeros_like(cos), jnp.zeros((seq, 64), F32)
    cos_t = jnp.concatenate([cos, cos, z64], axis=1)
    sin_a = jnp.concatenate([-sin, z32, z64], axis=1)
    sin_b = jnp.concatenate([z32, sin, z64], axis=1)
    return cos_t, sin_a, sin_b


def _rope(t, cos_t, sin_a, sin_b):
    return t * cos_t + pltpu.roll(t, 96, 1) * sin_a + pltpu.roll(t, 32, 1) * sin_b


def _rope_t(d, cos_t, sin_a, sin_b):
    return d * cos_t + pltpu.roll(d * sin_a, 32, 1) + pltpu.roll(d * sin_b, 96, 1)


def _mla_qkv(cq, ckv, kr, g_qa, g_kva, w_q, w_kv, tables, seq, tm):
    t = cq.shape[0]
    nblk = seq // tm

    def body(cq_ref, ckv_ref, kr_ref, gq_ref, gk_ref, wq_ref, wkv_ref, c_ref, sa_ref, sb_ref, q_out, k_out, v_out):
        cos_t, sin_a, sin_b = c_ref[...], sa_ref[...], sb_ref[...]
        cqn = _rms(cq_ref[...], gq_ref[...]).astype(BF16)
        ckn = _rms(ckv_ref[...], gk_ref[...]).astype(BF16)
        kr_rot = _rope(kr_ref[...], cos_t, sin_a, sin_b).astype(BF16)
        for h in range(B_HEADS):
            lo = h * QK_PAD
            q_out[:, lo:lo + 128] = (_dot_nt(cqn, wq_ref[lo:lo + 128, :]) * ATTN_SCALE).astype(BF16)
            qr = _rope(_dot_nt(cqn, wq_ref[lo + 128:lo + 256, :]), cos_t, sin_a, sin_b)
            q_out[:, lo + 128:lo + 256] = (qr * ATTN_SCALE).astype(BF16)
            k_out[:, lo:lo + 128] = _dot(ckn, wkv_ref[:, lo:lo + 128]).astype(BF16)
            k_out[:, lo + 128:lo + 256] = kr_rot
            v_out[:, h * B_V:(h + 1) * B_V] = _dot(ckn, wkv_ref[:, lo + 128:lo + 256]).astype(BF16)

    tok = lambda wd: pl.BlockSpec((tm, wd), lambda i: (i, 0))
    tab = pl.BlockSpec((tm, 128), lambda i: (i % nblk, 0))
    return pl.pallas_call(
        body, name="mla_qkv_fwd", grid=(t // tm,),
        out_shape=[jax.ShapeDtypeStruct((t, B_HEADS * QK_PAD), BF16), jax.ShapeDtypeStruct((t, B_HEADS * QK_PAD), BF16),
                   jax.ShapeDtypeStruct((t, B_HEADS * B_V), BF16)],
        in_specs=[tok(Q_LORA), tok(KV_LORA), tok(128), _const_spec((1, Q_LORA)), _const_spec((1, KV_LORA)),
                  _const_spec((B_HEADS * QK_PAD, Q_LORA)), _const_spec((KV_LORA, 1024)), tab, tab, tab],
        out_specs=[tok(B_HEADS * QK_PAD), tok(B_HEADS * QK_PAD), tok(B_HEADS * B_V)],
        compiler_params=_params(),
    )(cq, ckv, kr, g_qa, g_kva, w_q, w_kv, *tables)


def _step_index(nq):
    return (pl.program_id(0) * B_HEADS + pl.program_id(1)) * nq + pl.program_id(2)


def _attn_fwd(qcat, kcat, v, nb, seq, tq, gather=()):
    t = qcat.shape[0]
    nq = seq // tq
    ng = len(gather)
    steps = nb * B_HEADS * nq

    def body(q_ref, k_ref, v_ref, *rest):
        o_ref, lse_ref = rest[ng:ng + 2]
        if ng:
            start, forward, finish = _gather_protocol(rest[:ng], rest[ng + 2:2 * ng + 2], *rest[2 * ng + 2:])
            pl.when(_step_index(nq) == 0)(start)
            pl.when(_step_index(nq) == (3 * steps) // 4)(forward)
        for j in range(tq // ATTN_SUB):
            r = pl.ds(j * ATTN_SUB, ATTN_SUB)
            s = _dot_nt(q_ref[r, :], k_ref[...])
            m = jnp.max(s, axis=-1, keepdims=True)
            p = jnp.exp(s - m)
            l = jnp.sum(p, axis=-1, keepdims=True)
            o_ref[r, :] = _dot(p.astype(BF16), v_ref[...]) / l
            lse_ref[0, r, :] = m + jnp.log(l)
        if ng:
            pl.when(_step_index(nq) == steps - 1)(finish)

    any_spec = pl.BlockSpec(memory_space=pl.ANY)
    return pl.pallas_call(
        body, name="attn_fwd", grid=(nb, B_HEADS, nq),
        out_shape=[jax.ShapeDtypeStruct((t, B_HEADS * B_V), F32), jax.ShapeDtypeStruct((B_HEADS, t, 1), F32)] + _slot_shapes(gather),
        in_specs=[pl.BlockSpec((tq, QK_PAD), lambda b, h, i: (b * nq + i, h)),
                  pl.BlockSpec((seq, QK_PAD), lambda b, h, i: (b, h)),
                  pl.BlockSpec((seq, B_V), lambda b, h, i: (b, h))] + [any_spec] * ng,
        out_specs=[pl.BlockSpec((tq, B_V), lambda b, h, i: (b * nq + i, h)),
                   pl.BlockSpec((1, tq, 1), lambda b, h, i: (h, b * nq + i, 0))] + [any_spec] * ng,
        scratch_shapes=_comm_sems(ng) if ng else [],
        compiler_params=_params(),
    )(qcat, kcat, v, *gather)


def _attn_bwd(qcat, kcat, v, o, lse, do, nb, seq, tq, exchange=()):
    t = qcat.shape[0]
    nq = seq // tq
    ne = len(exchange)
    steps = nb * B_HEADS * nq

    def body(q_ref, k_ref, v_ref, o_ref, lse_ref, do_ref, *rest):
        dq_ref, dk_ref, dv_ref = rest[ne:ne + 3]
        if ne:
            start, finish = _exchange_protocol(rest[:ne], rest[ne + 3:2 * ne + 3], [True] * ne, *rest[2 * ne + 3:])
            pl.when(_step_index(nq) == 0)(start)

        @pl.when(pl.program_id(2) == 0)
        def _():
            dv_ref[...] = jnp.zeros_like(dv_ref)
            dk_ref[...] = jnp.zeros_like(dk_ref)

        for j in range(tq // ATTN_SUB_BWD):
            r = pl.ds(j * ATTN_SUB_BWD, ATTN_SUB_BWD)
            q, k = q_ref[r, :], k_ref[...]
            do_f = do_ref[r, :]
            delta = jnp.sum(do_f * o_ref[r, :], axis=-1, keepdims=True)
            dob = do_f.astype(BF16)
            p = jnp.exp(_dot_nt(q, k) - lse_ref[0, r, :])
            ds = (p * (_dot_nt(dob, v_ref[...]) - delta)).astype(BF16)
            dq_ref[r, :] = _dot(ds, k).astype(dq_ref.dtype)
            dv_ref[...] += _dot_tn(p.astype(BF16), dob)
            dk_ref[...] += _dot_tn(ds, q)
        if ne:
            pl.when(_step_index(nq) == steps - 1)(finish)

    qspec = lambda wd: pl.BlockSpec((tq, wd), lambda b, h, i: (b * nq + i, h))
    kspec = lambda wd: pl.BlockSpec((seq, wd), lambda b, h, i: (b, h))
    any_spec = pl.BlockSpec(memory_space=pl.ANY)
    return pl.pallas_call(
        body, name="attn_bwd", grid=(nb, B_HEADS, nq),
        out_shape=[jax.ShapeDtypeStruct((t, B_HEADS * QK_PAD), BF16), jax.ShapeDtypeStruct((t, B_HEADS * QK_PAD), F32),
                   jax.ShapeDtypeStruct((t, B_HEADS * B_V), F32)] + _slot_shapes(exchange, [True] * ne),
        in_specs=[qspec(QK_PAD), kspec(QK_PAD), kspec(B_V), qspec(B_V),
                  pl.BlockSpec((1, tq, 1), lambda b, h, i: (h, b * nq + i, 0)), qspec(B_V)] + [any_spec] * ne,
        out_specs=[qspec(QK_PAD), kspec(QK_PAD), kspec(B_V)] + [any_spec] * ne,
        scratch_shapes=_comm_sems(ne) if ne else [],
        compiler_params=_params(),
    )(qcat, kcat, v, o, lse, do, *exchange)


def _gla_consts(reverse):
    row = lax.broadcasted_iota(jnp.int32, (CHUNK, CHUNK), 0)
    col = lax.broadcasted_iota(jnp.int32, (CHUNK, CHUNK), 1)
    causal = (row <= col) if reverse else (row >= col)
    lane = lax.broadcasted_iota(jnp.int32, (1, HEAD_PAIR), 1)
    m0 = (lane < 64).astype(F32)
    m1 = 1.0 - m0
    r2 = lax.broadcasted_iota(jnp.int32, (HEAD_PAIR, HEAD_PAIR), 0)
    c2 = lax.broadcasted_iota(jnp.int32, (HEAD_PAIR, HEAD_PAIR), 1)
    same_head = ((r2 < 64) == (c2 < 64)).astype(F32)
    return causal, m0, m1, same_head


def _gla_chunk(hq, hi, z, l0, l1, st, consts, reverse):
    q_dec, k_inv, k_end, decay = _gla_gates(hq, z, l0, l1, reverse)
    o, st_new = _gla_state(q_dec, st, decay, _gla_increment(hi, k_end, consts))
    return o + _gla_intra(q_dec, k_inv, hi, consts), st_new


def _gla_gates(hq, z, l0, l1, reverse):
    mx = jnp.maximum(l0, l1)
    e0, e1 = jnp.exp(l0 - mx), jnp.exp(l1 - mx)
    lb = e0 / (e0 + e1)
    q = hq * _sigmoid(hq)
    sz = _sigmoid(z)
    log_f = jnp.log(lb + (1.0 - lb) * sz)
    k = (1.0 - lb) * (1.0 - sz)
    cum = _cumsum_rows(log_f, reverse)
    decay = jnp.exp(jnp.sum(log_f, axis=0, keepdims=True))
    k_inv = k * jnp.exp(-cum)
    return q * jnp.exp(cum), k_inv, k_inv * decay, decay


def _gla_intra(q_dec, k_inv, hi, consts):
    causal, m0, m1, _ = consts
    o = None
    for mh in (m0, m1):
        s = jnp.where(causal, _mm_nt(q_dec * mh, k_inv), 0.0)
        part = _mm(s, hi) * mh
        o = part if o is None else o + part
    return o


def _gla_increment(hi, k_end, consts):
    return _mm_tn(hi, k_end) * consts[3]


def _gla_state(q_dec, st, decay, inc):
    return _mm_nt(q_dec, st), st * decay + inc


GLA_DIRS = (False, True)
GLA_BATCH_FWD = 8
GLA_BATCH_BWD = 4


def _gla_fwd(hq, hi, zs, lbls, nb, seq, group):
    t = hq.shape[0]
    rows = group * CHUNK
    nblk = seq // rows
    n_chunks = seq // CHUNK
    nd = len(GLA_DIRS)

    def body(*refs):
        ins, outs, st_refs = refs[:4 * nd], refs[4 * nd:6 * nd], refs[6 * nd:]
        @pl.when(pl.program_id(2) == 0)
        def _():
            for st_ref in st_refs:
                st_ref[...] = jnp.zeros_like(st_ref)

        consts = [_gla_consts(rev) for rev in GLA_DIRS]
        work = [(d, rev, group - 1 - cc if rev else cc) for cc in range(group) for d, rev in enumerate(GLA_DIRS)]
        rows_of = lambda c: pl.ds(c * CHUNK, CHUNK)
        sts = [st_ref[...] for st_ref in st_refs]
        for w0 in range(0, len(work), GLA_BATCH_FWD):
            batch = work[w0:w0 + GLA_BATCH_FWD]
            gates, intra, incs = {}, {}, {}
            for d, rev, c in batch:
                hq_ref, _, z_ref, lbl_ref = ins[4 * d:4 * d + 4]
                gates[d, c] = _gla_gates(hq_ref[rows_of(c), :], z_ref[rows_of(c), :], lbl_ref[0:1, :], lbl_ref[1:2, :], rev)
            for d, rev, c in batch:
                hi_c = ins[4 * d + 1][rows_of(c), :]
                intra[d, c] = _gla_intra(gates[d, c][0], gates[d, c][1], hi_c, consts[d])
                incs[d, c] = _gla_increment(hi_c, gates[d, c][2], consts[d])
            for d, rev, c in batch:
                outs[nd + d][0, 0, c] = sts[d]
                o_state, sts[d] = _gla_state(gates[d, c][0], sts[d], gates[d, c][3], incs[d, c])
                outs[d][rows_of(c), :] = intra[d, c] + o_state
        for st_ref, st in zip(st_refs, sts):
            st_ref[...] = st

    def tb(rev):
        return (lambda i: nblk - 1 - i) if rev else (lambda i: i)

    tok = lambda rev: pl.BlockSpec((rows, HEAD_PAIR), lambda b, p, i: (b * nblk + tb(rev)(i), p))
    lspec = pl.BlockSpec((2, HEAD_PAIR), lambda b, p, i: (0, p))
    sspec = lambda rev: pl.BlockSpec((1, 1, group, HEAD_PAIR, HEAD_PAIR), lambda b, p, i: (b, p, tb(rev)(i), 0, 0))
    args, in_specs = [], []
    for d, rev in enumerate(GLA_DIRS):
        args += [hq, hi, zs[d], lbls[d]]
        in_specs += [tok(rev), tok(rev), tok(rev), lspec]
    return pl.pallas_call(
        body, name="gla_fwd", grid=(nb, 4, nblk),
        out_shape=[jax.ShapeDtypeStruct((t, A_WIDTH), F32)] * nd
        + [jax.ShapeDtypeStruct((nb, 4, n_chunks, HEAD_PAIR, HEAD_PAIR), F32)] * nd,
        in_specs=in_specs, out_specs=[tok(rev) for rev in GLA_DIRS] + [sspec(rev) for rev in GLA_DIRS],
        scratch_shapes=[pltpu.VMEM((HEAD_PAIR, HEAD_PAIR), F32)] * nd,
        compiler_params=_params(),
    )(*args)


def _gla_bwd(hq, hi, zs, lbls, saved, do, nb, seq, group):
    t = hq.shape[0]
    rows = group * CHUNK
    nblk = seq // rows
    nd = len(GLA_DIRS)

    def body(*refs):
        ins, outs, dst_refs = refs[:6 * nd], refs[6 * nd:10 * nd], refs[10 * nd:]
        dl_refs = outs[3 * nd:]

        @pl.when(pl.program_id(2) == 0)
        def _():
            for dst_ref, dl_ref in zip(dst_refs, dl_refs):
                dst_ref[...] = jnp.zeros_like(dst_ref)
                dl_ref[...] = jnp.zeros_like(dl_ref)

        consts = [_gla_consts(rev) for rev in GLA_DIRS]
        dsts = [dst_ref[...] for dst_ref in dst_refs]
        dls = [[jnp.zeros((1, HEAD_PAIR), F32), jnp.zeros((1, HEAD_PAIR), F32)] for _ in GLA_DIRS]
        work = [(d, rev, cc if rev else group - 1 - cc) for cc in range(group) for d, rev in enumerate(GLA_DIRS)]
        for w0 in range(0, len(work), GLA_BATCH_BWD):
            vjps = {}
            for d, rev, c in work[w0:w0 + GLA_BATCH_BWD]:
                hq_ref, hi_ref, z_ref, lbl_ref, save_ref, _ = ins[6 * d:6 * d + 6]
                r = pl.ds(c * CHUNK, CHUNK)
                fn = functools.partial(_gla_chunk, consts=consts[d], reverse=rev)
                _, vjps[d, c] = jax.vjp(fn, hq_ref[r, :], hi_ref[r, :], z_ref[r, :], lbl_ref[0:1, :], lbl_ref[1:2, :], save_ref[0, 0, c])
            for d, rev, c in work[w0:w0 + GLA_BATCH_BWD]:
                dq_ref, dv_ref, dz_ref = outs[3 * d:3 * d + 3]
                r = pl.ds(c * CHUNK, CHUNK)
                d_hq, d_hi, d_z, d_l0, d_l1, dsts[d] = vjps[d, c]((ins[6 * d + 5][r, :], dsts[d]))
                dq_ref[r, :] = d_hq.astype(dq_ref.dtype)
                dv_ref[r, :] = d_hi.astype(dv_ref.dtype)
                dz_ref[r, :] = d_z.astype(dz_ref.dtype)
                dls[d] = [dls[d][0] + d_l0, dls[d][1] + d_l1]
        for d in range(nd):
            dst_refs[d][...] = dsts[d]
            dl_refs[d][0, 0:1, :] += dls[d][0]
            dl_refs[d][0, 1:2, :] += dls[d][1]

    def tb(rev):
        return (lambda i: i) if rev else (lambda i: nblk - 1 - i)

    tok = lambda rev: pl.BlockSpec((rows, HEAD_PAIR), lambda b, p, i: (b * nblk + tb(rev)(i), p))
    lspec = pl.BlockSpec((2, HEAD_PAIR), lambda b, p, i: (0, p))
    sspec = lambda rev: pl.BlockSpec((1, 1, group, HEAD_PAIR, HEAD_PAIR), lambda b, p, i: (b, p, tb(rev)(i), 0, 0))
    args, in_specs, out_specs = [], [], []
    for d, rev in enumerate(GLA_DIRS):
        args += [hq, hi, zs[d], lbls[d], saved[d], do]
        in_specs += [tok(rev), tok(rev), tok(rev), lspec, sspec(rev), tok(rev)]
        out_specs += [tok(rev)] * 3
    out_specs += [pl.BlockSpec((1, 2, HEAD_PAIR), lambda b, p, i: (b, 0, p))] * nd
    return pl.pallas_call(
        body, name="gla_bwd", grid=(nb, 4, nblk),
        out_shape=[jax.ShapeDtypeStruct((t, A_WIDTH), BF16)] * (3 * nd) + [jax.ShapeDtypeStruct((nb, 2, A_WIDTH), F32)] * nd,
        in_specs=in_specs, out_specs=out_specs,
        scratch_shapes=[pltpu.VMEM((HEAD_PAIR, HEAD_PAIR), F32)] * nd,
        compiler_params=_params(),
    )(*args)


def _head_mean_matrix():
    r = lax.broadcasted_iota(jnp.int32, (A_WIDTH, A_WIDTH), 0) // 64
    c = lax.broadcasted_iota(jnp.int32, (A_WIDTH, A_WIDTH), 1) // 64
    return jnp.where(r == c, 1.0 / 64.0, 0.0).astype(BF16)


def _gla_out(o_f, o_b, hg, g, mean_mat):
    o = o_f + o_b
    ms = _group_mean(o * o, mean_mat)
    return o * lax.rsqrt(ms + EPS) * g * (hg * _sigmoid(hg))


def _gla_combine(o_f, o_b, hg, g, tm):
    t = o_f.shape[0]

    def body(of_ref, ob_ref, hg_ref, g_ref, y_ref):
        y_ref[...] = _gla_out(of_ref[...], ob_ref[...], hg_ref[...], g_ref[...], _head_mean_matrix())

    tok = pl.BlockSpec((tm, A_WIDTH), lambda i: (i, 0))
    return pl.pallas_call(
        body, name="gla_combine_fwd", grid=(t // tm,), out_shape=jax.ShapeDtypeStruct((t, A_WIDTH), F32),
        in_specs=[tok, tok, tok, _const_spec((1, A_WIDTH))], out_specs=tok, compiler_params=_params(),
    )(o_f, o_b, hg, g)


def _gla_combine_bwd(o_f, o_b, hg, g, dy, tm):
    t = o_f.shape[0]

    def body(of_ref, ob_ref, hg_ref, g_ref, dy_ref, do_ref, dhg_ref, dg_ref):
        mean_mat = _head_mean_matrix()
        fn = lambda o, hgv, gv: _gla_out(o, jnp.zeros_like(o), hgv, gv, mean_mat)
        _, vjp = jax.vjp(fn, of_ref[...] + ob_ref[...], hg_ref[...], g_ref[...])
        d_o, d_hg, d_g = vjp(dy_ref[...])
        do_ref[...] = d_o
        dhg_ref[...] = d_hg.astype(dhg_ref.dtype)

        @pl.when(pl.program_id(0) == 0)
        def _():
            dg_ref[...] = jnp.zeros_like(dg_ref)

        dg_ref[...] += d_g

    tok = pl.BlockSpec((tm, A_WIDTH), lambda i: (i, 0))
    vec = pl.BlockSpec((1, A_WIDTH), lambda i: (0, 0))
    return pl.pallas_call(
        body, name="gla_combine_bwd", grid=(t // tm,),
        out_shape=[jax.ShapeDtypeStruct((t, A_WIDTH), F32), jax.ShapeDtypeStruct((t, A_WIDTH), BF16),
                   jax.ShapeDtypeStruct((1, A_WIDTH), F32)],
        in_specs=[tok, tok, tok, _const_spec((1, A_WIDTH)), tok], out_specs=[tok, tok, vec], compiler_params=_params(),
    )(o_f, o_b, hg, g, dy)


def _post_fwd(x, ya, oattn, tgt, g_mla, w_out, g2, w_gate, w_up, w_down, g_fin, tm):
    t = x.shape[0]

    def body(x_ref, ya_ref, oa_ref, tgt_ref, gm_ref, wo_ref, g2_ref, wg_ref, wu_ref, wd_ref, gf_ref,
             x1_ref, x2_ref, gate_ref, up_ref, loss_ref):
        part = jnp.zeros((1, 1), F32)
        for j in range(tm // min(tm, ROW_SUB)):
            r = pl.ds(j * min(tm, ROW_SUB), min(tm, ROW_SUB))
            yb = _rms(oa_ref[r, :], gm_ref[...])
            x1 = x_ref[r, :] + _dot(ya_ref[r, :].astype(BF16), wo_ref[0:A_WIDTH, :]) + _dot(yb.astype(BF16), wo_ref[A_WIDTH:, :])
            x1_ref[r, :] = x1
            h2 = _rms(x1, g2_ref[...]).astype(BF16)
            gate, up = _dot_nt(h2, wg_ref[...]), _dot_nt(h2, wu_ref[...])
            gate_ref[r, :] = gate.astype(BF16)
            up_ref[r, :] = up.astype(BF16)
            act = (gate * _sigmoid(gate) * up).astype(BF16)
            x2 = x1 + _dot(act, wd_ref[...])
            x2_ref[r, :] = x2
            err = _rms(x2, gf_ref[...]) - tgt_ref[r, :]
            part = part + 0.5 * jnp.sum(jnp.mean(err * err, axis=-1, keepdims=True), axis=0, keepdims=True)

        @pl.when(pl.program_id(0) == 0)
        def _():
            loss_ref[...] = jnp.zeros_like(loss_ref)

        loss_ref[...] += jnp.broadcast_to(part, loss_ref.shape)

    tok = lambda wd: pl.BlockSpec((tm, wd), lambda i: (i, 0))
    return pl.pallas_call(
        body, name="post_fwd", grid=(t // tm,),
        out_shape=[jax.ShapeDtypeStruct((t, D_MODEL), F32)] * 2 + [jax.ShapeDtypeStruct((t, D_FF), BF16)] * 2
        + [jax.ShapeDtypeStruct((1, 128), F32)],
        in_specs=[tok(D_MODEL), tok(A_WIDTH), tok(512), tok(D_MODEL), _const_spec((1, 512)), _const_spec((D_MODEL, D_MODEL)),
                  _const_spec((1, D_MODEL)), _const_spec((D_FF, D_MODEL)), _const_spec((D_FF, D_MODEL)),
                  _const_spec((D_FF, D_MODEL)), _const_spec((1, D_MODEL))],
        out_specs=[tok(D_MODEL), tok(D_MODEL), tok(D_FF), tok(D_FF), pl.BlockSpec((1, 128), lambda i: (0, 0))],
        compiler_params=_params(),
    )(x, ya, oattn, tgt, g_mla, w_out, g2, w_gate, w_up, w_down, g_fin)


def _post_bwd(x1, x2, gate_b, up_b, ya, oattn, tgt, g_mla, w_out, g2, w_gate, w_up, w_down, g_fin, tm):
    t = x1.shape[0]

    def body(x1_ref, x2_ref, gate_ref, up_ref, ya_ref, oa_ref, tgt_ref, gm_ref, wo_ref, g2_ref, wg_ref, wu_ref, wd_ref, gf_ref,
             dx1_ref, dya_ref, doa_ref, ycat_ref, dx1b_ref, h2_ref, dgate_ref, dup_ref, act_ref, dx2b_ref,
             dgm_ref, dg2_ref, dgf_ref):
        x1, x2 = x1_ref[...], x2_ref[...]
        dy = (_rms(x2, gf_ref[...]) - tgt_ref[...]) * (1.0 / D_MODEL)
        dx2, dgf = _rms_bwd(x2, gf_ref[...], dy)
        dx2b = dx2.astype(BF16)
        dx2b_ref[...] = dx2b
        h2_ref[...] = _rms(x1, g2_ref[...]).astype(BF16)
        gate, up = gate_ref[...].astype(F32), up_ref[...].astype(F32)
        sg = _sigmoid(gate)
        sl = gate * sg
        act_ref[...] = (sl * up).astype(BF16)
        dact = _dot_nt(dx2b, wd_ref[...])
        dup = (dact * sl).astype(BF16)
        dgate = (dact * up * (sg * (1.0 + gate * (1.0 - sg)))).astype(BF16)
        dup_ref[...] = dup
        dgate_ref[...] = dgate
        dh2 = _dot(dgate, wg_ref[...]) + _dot(dup, wu_ref[...])
        dx1n, dg2 = _rms_bwd(x1, g2_ref[...], dh2)
        dx1 = dx2 + dx1n
        dx1_ref[...] = dx1
        dx1b = dx1.astype(BF16)
        dx1b_ref[...] = dx1b
        oa = oa_ref[...]
        ycat_ref[:, 0:A_WIDTH] = ya_ref[...].astype(BF16)
        ycat_ref[:, A_WIDTH:] = _rms(oa, gm_ref[...]).astype(BF16)
        dya_ref[...] = _dot_nt(dx1b, wo_ref[0:A_WIDTH, :])
        doa, dgm = _rms_bwd(oa, gm_ref[...], _dot_nt(dx1b, wo_ref[A_WIDTH:, :]))
        doa_ref[...] = doa

        @pl.when(pl.program_id(0) == 0)
        def _():
            dgm_ref[...] = jnp.zeros_like(dgm_ref)
            dg2_ref[...] = jnp.zeros_like(dg2_ref)
            dgf_ref[...] = jnp.zeros_like(dgf_ref)

        dgm_ref[...] += dgm
        dg2_ref[...] += dg2
        dgf_ref[...] += dgf

    tok = lambda wd: pl.BlockSpec((tm, wd), lambda i: (i, 0))
    vec = lambda wd: pl.BlockSpec((1, wd), lambda i: (0, 0))
    sds = lambda wd, dt: jax.ShapeDtypeStruct((t, wd), dt)
    return pl.pallas_call(
        body, name="post_bwd", grid=(t // tm,),
        out_shape=[sds(D_MODEL, F32), sds(512, F32), sds(512, F32), sds(D_MODEL, BF16), sds(D_MODEL, BF16), sds(D_MODEL, BF16),
                   sds(D_FF, BF16), sds(D_FF, BF16), sds(D_FF, BF16), sds(D_MODEL, BF16),
                   jax.ShapeDtypeStruct((1, 512), F32), jax.ShapeDtypeStruct((1, D_MODEL), F32), jax.ShapeDtypeStruct((1, D_MODEL), F32)],
        in_specs=[tok(D_MODEL), tok(D_MODEL), tok(D_FF), tok(D_FF), tok(512), tok(512), tok(D_MODEL), _const_spec((1, 512)),
                  _const_spec((D_MODEL, D_MODEL)), _const_spec((1, D_MODEL)), _const_spec((D_FF, D_MODEL)),
                  _const_spec((D_FF, D_MODEL)), _const_spec((D_FF, D_MODEL)), _const_spec((1, D_MODEL))],
        out_specs=[tok(D_MODEL), tok(512), tok(512), tok(D_MODEL), tok(D_MODEL), tok(D_MODEL), tok(D_FF), tok(D_FF), tok(D_FF),
                   tok(D_MODEL), vec(512), vec(D_MODEL), vec(D_MODEL)],
        compiler_params=_params(),
    )(x1, x2, gate_b, up_b, ya, oattn, tgt, g_mla, w_out, g2, w_gate, w_up, w_down, g_fin)


def _matmul_tn(a, b, tn, tt, tag):
    t, k = a.shape
    n = b.shape[1]
    last = t // tt - 1

    def body(a_ref, b_ref, o_ref, acc_ref):
        part = _dot_tn(a_ref[...], b_ref[...])

        @pl.when(pl.program_id(1) == 0)
        def _():
            acc_ref[...] = part

        @pl.when(pl.program_id(1) > 0)
        def _():
            acc_ref[...] += part

        @pl.when(pl.program_id(1) == last)
        def _():
            o_ref[...] = acc_ref[...].astype(o_ref.dtype)

    return pl.pallas_call(
        body, name="wgrad_" + tag, grid=(n // tn, t // tt), out_shape=jax.ShapeDtypeStruct((k, n), BF16),
        in_specs=[pl.BlockSpec((tt, k), lambda j, i: (i, 0)), pl.BlockSpec((tt, tn), lambda j, i: (i, j))],
        out_specs=pl.BlockSpec((k, tn), lambda j, i: (0, j)), scratch_shapes=[pltpu.VMEM((k, tn), F32)],
        compiler_params=_params(),
    )(a, b)


def _mla_qkv_bwd(cq, ckv, g_qa, g_kva, w_q, w_kv, tables, dq, dk, dv, seq, tm):
    t = cq.shape[0]
    nblk = seq // tm

    def body(cq_ref, ckv_ref, gq_ref, gk_ref, wq_ref, wkv_ref, c_ref, sa_ref, sb_ref, dq_ref, dk_ref, dv_ref,
             dcq_ref, dckv_ref, dkr_ref, cqn_ref, dqf_ref, ckn_ref, dkv_ref, dgq_ref, dgk_ref):
        cos_t, sin_a, sin_b = c_ref[...], sa_ref[...], sb_ref[...]
        cqn_ref[...] = _rms(cq_ref[...], gq_ref[...]).astype(BF16)
        ckn_ref[...] = _rms(ckv_ref[...], gk_ref[...]).astype(BF16)
        dkr = jnp.zeros((tm, 128), F32)
        for h in range(B_HEADS):
            lo = h * QK_PAD
            dqf_ref[:, lo:lo + 128] = (dq_ref[:, lo:lo + 128].astype(F32) * ATTN_SCALE).astype(BF16)
            dq_rope = dq_ref[:, lo + 128:lo + 256].astype(F32) * ATTN_SCALE
            dqf_ref[:, lo + 128:lo + 256] = _rope_t(dq_rope, cos_t, sin_a, sin_b).astype(BF16)
            dkv_ref[:, lo:lo + 128] = dk_ref[:, lo:lo + 128].astype(BF16)
            dkv_ref[:, lo + 128:lo + 256] = dv_ref[:, h * B_V:(h + 1) * B_V].astype(BF16)
            dkr = dkr + dk_ref[:, lo + 128:lo + 256]
        dkr_ref[...] = _rope_t(dkr, cos_t, sin_a, sin_b).astype(dkr_ref.dtype)
        dcq, dgq = _rms_bwd(cq_ref[...], gq_ref[...], _dot(dqf_ref[...], wq_ref[...]))
        dckv, dgk = _rms_bwd(ckv_ref[...], gk_ref[...], _dot_nt(dkv_ref[...], wkv_ref[...]))
        dcq_ref[...] = dcq.astype(dcq_ref.dtype)
        dckv_ref[...] = dckv.astype(dckv_ref.dtype)

        @pl.when(pl.program_id(0) == 0)
        def _():
            dgq_ref[...] = jnp.zeros_like(dgq_ref)
            dgk_ref[...] = jnp.zeros_like(dgk_ref)

        dgq_ref[...] += dgq
        dgk_ref[...] += dgk

    tok = lambda wd: pl.BlockSpec((tm, wd), lambda i: (i, 0))
    vec = lambda wd: pl.BlockSpec((1, wd), lambda i: (0, 0))
    tab = pl.BlockSpec((tm, 128), lambda i: (i % nblk, 0))
    sds = lambda wd, dt: jax.ShapeDtypeStruct((t, wd), dt)
    return pl.pallas_call(
        body, name="mla_qkv_bwd", grid=(t // tm,),
        out_shape=[sds(Q_LORA, BF16), sds(KV_LORA, BF16), sds(128, BF16), sds(Q_LORA, BF16), sds(1024, BF16), sds(KV_LORA, BF16),
                   sds(1024, BF16), jax.ShapeDtypeStruct((1, Q_LORA), F32), jax.ShapeDtypeStruct((1, KV_LORA), F32)],
        in_specs=[tok(Q_LORA), tok(KV_LORA), _const_spec((1, Q_LORA)), _const_spec((1, KV_LORA)),
                  _const_spec((1024, Q_LORA)), _const_spec((KV_LORA, 1024)), tab, tab, tab,
                  tok(1024), tok(1024), tok(512)],
        out_specs=[tok(Q_LORA), tok(KV_LORA), tok(128), tok(Q_LORA), tok(1024), tok(KV_LORA), tok(1024),
                   vec(Q_LORA), vec(KV_LORA)],
        compiler_params=_params(),
    )(cq, ckv, g_qa, g_kva, w_q, w_kv, *tables, dq, dk, dv)


def _inproj_bwd(x, g1, w_in, dx1, pieces, tm):
    t = x.shape[0]
    counts = [len(p) for p in pieces]
    flat = [a for p in pieces for a in p]
    widths = [wd for wd, p in zip(IN_WIDTHS, pieces) for _ in p]

    def body(x_ref, g_ref, w_ref, dx1_ref, *refs):
        ins = refs[:len(flat)]
        dx_ref, h_ref, dp_ref, dg_ref = refs[len(flat):]
        xv = x_ref[...]
        h_ref[...] = _rms(xv, g_ref[...]).astype(BF16)
        off, j = 0, 0
        for wd, cnt in zip(IN_WIDTHS, counts):
            acc = ins[j][...].astype(F32)
            for jj in range(1, cnt):
                acc = acc + ins[j + jj][...].astype(F32)
            dp_ref[:, off:off + wd] = acc.astype(BF16)
            off += wd
            j += cnt
        dxn, dg = _rms_bwd(xv, g_ref[...], _dot(dp_ref[...], w_ref[...]))
        dx_ref[...] = dx1_ref[...] + dxn

        @pl.when(pl.program_id(0) == 0)
        def _():
            dg_ref[...] = jnp.zeros_like(dg_ref)

        dg_ref[...] += dg

    tok = lambda wd: pl.BlockSpec((tm, wd), lambda i: (i, 0))
    return pl.pallas_call(
        body, name="inproj_bwd", grid=(t // tm,),
        out_shape=[jax.ShapeDtypeStruct((t, D_MODEL), F32), jax.ShapeDtypeStruct((t, D_MODEL), BF16),
                   jax.ShapeDtypeStruct((t, D_IN_PAD), BF16), jax.ShapeDtypeStruct((1, D_MODEL), F32)],
        in_specs=[tok(D_MODEL), _const_spec((1, D_MODEL)), _const_spec((D_IN_PAD, D_MODEL)), tok(D_MODEL)] + [tok(wd) for wd in widths],
        out_specs=[tok(D_MODEL), tok(D_MODEL), tok(D_IN_PAD), pl.BlockSpec((1, D_MODEL), lambda i: (0, 0))],
        compiler_params=_params(),
    )(x, g1, w_in, dx1, *flat)


def _cols_from_slots(g):
    n, r, cs = g.shape
    return g.transpose(1, 0, 2).reshape(r, n * cs)


def _cols_to_slots(full):
    r, c = full.shape
    return full.reshape(r, N_DEV, c // N_DEV).transpose(1, 0, 2)


def _arrange_w_in_t(w_in_t):
    return jnp.concatenate([w_in_t, jnp.zeros((D_IN_PAD - D_IN, D_MODEL), w_in_t.dtype)], axis=0)


def _arrange_w_q_t(w_q_t):
    q3 = w_q_t.reshape(B_HEADS, B_NOPE + B_ROPE, Q_LORA)
    pad = jnp.zeros((B_HEADS, QK_PAD - B_NOPE - B_ROPE, Q_LORA), w_q_t.dtype)
    return jnp.concatenate([q3, pad], axis=1).reshape(B_HEADS * QK_PAD, Q_LORA)


def _unarrange_w_q_t(d_q_t):
    return d_q_t.reshape(B_HEADS, QK_PAD, Q_LORA)[:, :B_NOPE + B_ROPE].reshape(B_HEADS * (B_NOPE + B_ROPE), Q_LORA)


def _step_core(x, loss_target, small_w, lb_full, early_full, late, seq, group, tiles, distributed):
    g1, g_hgrn, g_qa, g_kva, g_mla, g2, g_fin = small_w
    w_in, w_q, w_kv = _arrange_w_in_t(early_full[0]), _arrange_w_q_t(early_full[1]), early_full[2]
    nb = x.shape[0]
    t = nb * seq
    tm, tm_fwd, tq_f, tq_b, tt = tiles
    xt = x.reshape(t, D_MODEL)
    tgt = loss_target.reshape(t, D_MODEL)
    tables = _rope_tables(seq)

    hq, hi, zf, zb, hg, cq, ckv, kr = _inproj(xt, g1, w_in, tm_fwd)
    qcat, kcat, vv = _mla_qkv(cq, ckv, kr, g_qa, g_kva, w_q, w_kv, tables, seq, tm)
    if distributed:
        oattn, lse, *late_slots = _attn_fwd(qcat, kcat, vv, nb, seq, tq_f, gather=tuple(late))
    else:
        oattn, lse = _attn_fwd(qcat, kcat, vv, nb, seq, tq_f)
        late_slots = late
    w_out = late_slots[0].reshape(D_MODEL, D_MODEL)
    w_gate, w_up = late_slots[1].reshape(D_FF, D_MODEL), late_slots[2].reshape(D_FF, D_MODEL)
    w_down = late_slots[3].reshape(D_FF, D_MODEL)
    lbl_f, lbl_b = lb_full[0], lb_full[1]
    o_f, o_b, save_f, save_b = _gla_fwd(hq, hi, (zf, zb), (lbl_f, lbl_b), nb, seq, group)
    ya = _gla_combine(o_f, o_b, hg, g_hgrn, tm)
    x1, x2, gate_b, up_b, loss_row = _post_fwd(xt, ya, oattn, tgt, g_mla, w_out, g2, w_gate, w_up, w_down, g_fin, tm_fwd)

    (dx1, d_ya, d_oattn, ycat_b, dx1_b, h2_b, dgate_b, dup_b, act_b, dx2_b, d_g_mla, d_g2, d_g_fin) = _post_bwd(
        x1, x2, gate_b, up_b, ya, oattn, tgt, g_mla, w_out, g2, w_gate, w_up, w_down, g_fin, tm)
    d_w_gate = _matmul_tn(dgate_b, h2_b, 512, tt, "gate")
    d_w_up = _matmul_tn(dup_b, h2_b, 512, tt, "up")
    d_w_down = _matmul_tn(act_b, dx2_b, 512, tt, "down")
    d_w_out = _matmul_tn(ycat_b, dx1_b, D_MODEL, tt, "out")
    late_g = [d_w_out.reshape(N_DEV, D_MODEL // N_DEV, D_MODEL)] + [
        g.reshape(N_DEV, D_FF // N_DEV, D_MODEL) for g in (d_w_gate, d_w_up, d_w_down)]
    if distributed:
        dq, dk, dv, *late_g = _attn_bwd(qcat, kcat, vv, oattn, lse, d_oattn, nb, seq, tq_b, exchange=tuple(late_g))
    else:
        dq, dk, dv = _attn_bwd(qcat, kcat, vv, oattn, lse, d_oattn, nb, seq, tq_b)
    (d_cq, d_ckv, d_kr, cqn_b, dqf_b, ckn_b, dkv_b, d_g_qa, d_g_kva) = _mla_qkv_bwd(
        cq, ckv, g_qa, g_kva, w_q, w_kv, tables, dq, dk, dv, seq, tm)
    d_w_q = _matmul_tn(dqf_b, cqn_b, Q_LORA, tt, "q_b")
    d_w_kv = _matmul_tn(ckn_b, dkv_b, B_HEADS * (B_NOPE + B_V), tt, "kv_b")
    d_o, d_hg, d_g_hgrn = _gla_combine_bwd(o_f, o_b, hg, g_hgrn, d_ya, tm)
    dq_f, dv_f, dz_f, dq_b, dv_b, dz_b, dl_f, dl_b = _gla_bwd(
        hq, hi, (zf, zb), (lbl_f, lbl_b), (save_f, save_b), d_o, nb, seq, group)
    grad_x, h1_b, dproj_b, d_g1 = _inproj_bwd(
        xt, g1, w_in, dx1, [[dq_f, dq_b], [dv_f, dv_b], [dz_f], [dz_b], [d_hg], [d_cq], [d_ckv], [d_kr]], tm)
    d_w_in = _matmul_tn(dproj_b, h1_b, 512, tt, "in")

    early_g = [d_w_in[:D_IN].reshape(N_DEV, D_IN // N_DEV, D_MODEL),
               _unarrange_w_q_t(d_w_q).reshape(N_DEV, 768 // N_DEV, Q_LORA), _cols_to_slots(d_w_kv)]
    d_lb = jnp.stack([jnp.sum(dl_f, axis=0), jnp.sum(dl_b, axis=0)], axis=0)
    small_grads = [d_g1, d_g_hgrn, d_g_qa, d_g_kva, d_g_mla, d_g2, d_g_fin]
    return loss_row, grad_x.reshape(nb, seq, D_MODEL), early_g, late_g, small_grads, d_lb


def kernel(x, norm1_g, w_in, lb_logits, hgrn_norm_g, q_a_norm_g, w_q_b, kv_a_norm_g, w_kv_b, mla_norm_g, w_out, norm2_g, w_gate, w_up, w_down, final_norm_g, loss_target, m_norm1_g, m_w_in, m_lb_logits, m_hgrn_norm_g, m_q_a_norm_g, m_w_q_b, m_kv_a_norm_g, m_w_kv_b, m_mla_norm_g, m_w_out, m_norm2_g, m_w_gate, m_w_up, m_w_down, m_final_norm_g, v_norm1_g, v_w_in, v_lb_logits, v_hgrn_norm_g, v_q_a_norm_g, v_w_q_b, v_kv_a_norm_g, v_w_kv_b, v_mla_norm_g, v_w_out, v_norm2_g, v_w_gate, v_w_up, v_w_down, v_final_norm_g):
    big_w = [w_in, w_q_b, w_kv_b, w_out, w_gate, w_up, w_down]
    big_m = [m_w_in, m_w_q_b, m_w_kv_b, m_w_out, m_w_gate, m_w_up, m_w_down]
    big_v = [v_w_in, v_w_q_b, v_w_kv_b, v_w_out, v_w_gate, v_w_up, v_w_down]
    small_w = [norm1_g, hgrn_norm_g, q_a_norm_g, kv_a_norm_g, mla_norm_g, norm2_g, final_norm_g]
    small_m = [m_norm1_g, m_hgrn_norm_g, m_q_a_norm_g, m_kv_a_norm_g, m_mla_norm_g, m_norm2_g, m_final_norm_g]
    small_v = [v_norm1_g, v_hgrn_norm_g, v_q_a_norm_g, v_kv_a_norm_g, v_mla_norm_g, v_norm2_g, v_final_norm_g]
    seq = x.shape[1]
    my_id = 4 * lax.axis_index("x") + 2 * lax.axis_index("y") + lax.axis_index("c")

    shard = lambda w: w[0].astype(BF16)
    col_t = lambda w: jnp.swapaxes(w, 1, 2)[0]
    shard_t = lambda w: col_t(w).astype(BF16)
    g_in, g_q, g_kv, g_lb = _all_gather_call([shard_t(w_in), shard_t(w_q_b), shard(w_kv_b), lb_logits.reshape(4, 64)])
    early_full = (g_in.reshape(D_IN, D_MODEL), g_q.reshape(768, Q_LORA), _cols_from_slots(g_kv))
    lb_full = g_lb.reshape(N_DEV, 2, 2, 64).transpose(1, 2, 0, 3).reshape(2, 2, 512)

    as_row = lambda a: a.reshape(1, -1)
    loss_row, grad_x, early_g, late_recv, small_g, d_lb = _step_core(
        x, loss_target, [as_row(s) for s in small_w], lb_full, early_full,
        [shard(w_out), shard_t(w_gate), shard_t(w_up), shard(w_down)], seq, min(8, seq // CHUNK),
        (256, 512, min(1024, seq), min(512, seq), min(2048, 2 * seq)), True)

    grads, deltas, new_ms, new_vs = {}, {}, {}, {}
    views = {name: (col_t if name in ("w_in", "w_q_b", "w_gate", "w_up") else (lambda a: a[0])) for name, _, _, _ in BIG}
    backs = {name: ((lambda a: jnp.swapaxes(a[None], 1, 2)) if name in ("w_in", "w_q_b", "w_gate", "w_up") else (lambda a: a[None]))
             for name, _, _, _ in BIG}
    by_name = {name: (w, m, v) for (name, _, _, _), w, m, v in zip(BIG, big_w, big_m, big_v)}
    late_names = ["w_out", "w_gate", "w_up", "w_down"]
    n_small = len(small_g)
    g_l, d_l, nm_l, nv_l, recv = _adamw_recv_hosting(
        [views[n](by_name[n][0]) for n in late_names], list(late_recv), [views[n](by_name[n][1]) for n in late_names],
        [views[n](by_name[n][2]) for n in late_names],
        early_g + small_g + [d_lb.reshape(4, 512), loss_row], [True] * 3 + [False] * (n_small + 2))
    for i, name in enumerate(late_names):
        grads[name], deltas[name], new_ms[name], new_vs[name] = (backs[name](a[i]) for a in (g_l, d_l, nm_l, nv_l))
    sums = _sum_slots_call(recv[3:])
    g_small = [g.reshape(s.shape) for g, s in zip(sums[:n_small], small_w)]
    g_lb_own = lax.dynamic_index_in_dim(sums[n_small].reshape(2, 2, N_DEV, 64), my_id, axis=2, keepdims=False)
    loss = sums[n_small + 1][0, 0]

    for name, r in zip(["w_in", "w_q_b", "w_kv_b"], recv[:3]):
        w, m, v = by_name[name]
        g, d, nm, nv = _adamw_recv(views[name](w), r, views[name](m), views[name](v), name)
        grads[name], deltas[name], new_ms[name], new_vs[name] = (backs[name](a) for a in (g, d, nm, nv))
    lb_rows = lambda a: a.reshape(4, 64)
    d_s, nm_s, nv_s = _adamw_small(
        [as_row(a) for a in small_w] + [lb_rows(lb_logits)], [as_row(a) for a in g_small] + [lb_rows(g_lb_own)],
        [as_row(a) for a in small_m] + [lb_rows(m_lb_logits)], [as_row(a) for a in small_v] + [lb_rows(v_lb_logits)])
    for i, (s, (name, _)) in enumerate(zip(small_w + [lb_logits], SMALL + (("lb_logits", 0),))):
        grads[name] = (g_small + [g_lb_own])[i]
        deltas[name], new_ms[name], new_vs[name] = d_s[i].reshape(s.shape), nm_s[i].reshape(s.shape), nv_s[i].reshape(s.shape)

    order = ["norm1_g", "w_in", "lb_logits", "hgrn_norm_g", "q_a_norm_g", "w_q_b", "kv_a_norm_g", "w_kv_b", "mla_norm_g",
             "w_out", "norm2_g", "w_gate", "w_up", "w_down", "final_norm_g"]
    return (loss, grad_x, *[grads[n] for n in order], *[deltas[n] for n in order],
            *[new_ms[n] for n in order], *[new_vs[n] for n in order])
```

```python
import functools
import math

import jax
import jax.numpy as jnp
from jax import lax
from jax.experimental import pallas as pl
from jax.experimental.pallas import tpu as pltpu

F32 = jnp.float32
BF16 = jnp.bfloat16

N_DEV = 8
D_MODEL = 1024
D_FF = 2816
A_WIDTH = 512
HEAD_PAIR = 128
CHUNK = 64
B_HEADS = 4
B_NOPE = 128
B_ROPE = 64
B_V = 128
QK_PAD = 256
Q_LORA = 384
KV_LORA = 256
D_IN = 3264
D_IN_PAD = 3328
IN_WIDTHS = (512, 512, 512, 512, 512, Q_LORA, KV_LORA, 128)
ROPE_THETA = 10000.0
EPS = 1e-6
ATTN_SCALE = (B_NOPE + B_ROPE) ** -0.5
ATTN_SUB = 256
ATTN_SUB_BWD = 256
ROW_SUB = 256
ADAM_LR, ADAM_B1, ADAM_B2, ADAM_EPS, ADAM_WD, ADAM_STEP = 0.001, 0.9, 0.999, 1e-08, 0.01, 10
VMEM_LIMIT = 60 * 1024 * 1024
MESH = pl.DeviceIdType.MESH

BIG = (("w_in", 1024, D_IN, 1), ("w_q_b", Q_LORA, 768, 1), ("w_kv_b", KV_LORA, 1024, 1), ("w_out", 1024, 1024, 0),
       ("w_gate", 1024, D_FF, 1), ("w_up", 1024, D_FF, 1), ("w_down", D_FF, 1024, 0))
SMALL = (("norm1_g", 1024), ("hgrn_norm_g", 512), ("q_a_norm_g", 384), ("kv_a_norm_g", 256), ("mla_norm_g", 512),
         ("norm2_g", 1024), ("final_norm_g", 1024))


def _params(**kw):
    return pltpu.CompilerParams(vmem_limit_bytes=VMEM_LIMIT, **kw)


def _const_spec(shape):
    return pl.BlockSpec(shape, lambda *_: (0,) * len(shape), pipeline_mode=pl.Buffered(1))


def _dot(a, b):
    return jnp.dot(a, b, preferred_element_type=F32)


def _dot_nt(a, b):
    return lax.dot_general(a, b, (((1,), (1,)), ((), ())), preferred_element_type=F32)


def _dot_tn(a, b):
    return lax.dot_general(a, b, (((0,), (0,)), ((), ())), preferred_element_type=F32)


@jax.custom_vjp
def _mm(a, b):
    return _dot(a.astype(BF16), b.astype(BF16))


def _mm_fwd(a, b):
    return _mm(a, b), (a, b)


def _mm_bwd(res, g):
    a, b = res
    gb = g.astype(BF16)
    return _dot_nt(gb, b.astype(BF16)), _dot_tn(a.astype(BF16), gb)


_mm.defvjp(_mm_fwd, _mm_bwd)


@jax.custom_vjp
def _mm_nt(a, b):
    return _dot_nt(a.astype(BF16), b.astype(BF16))


def _mm_nt_fwd(a, b):
    return _mm_nt(a, b), (a, b)


def _mm_nt_bwd(res, g):
    a, b = res
    gb = g.astype(BF16)
    return _dot(gb, b.astype(BF16)), _dot_tn(gb, a.astype(BF16))


_mm_nt.defvjp(_mm_nt_fwd, _mm_nt_bwd)


@jax.custom_vjp
def _mm_tn(a, b):
    return _dot_tn(a.astype(BF16), b.astype(BF16))


def _mm_tn_fwd(a, b):
    return _mm_tn(a, b), (a, b)


def _mm_tn_bwd(res, g):
    a, b = res
    gb = g.astype(BF16)
    return _dot_nt(b.astype(BF16), gb), _dot(a.astype(BF16), gb)


_mm_tn.defvjp(_mm_tn_fwd, _mm_tn_bwd)


def _dot_exact_rhs(a, m):
    hi = a.astype(BF16)
    lo = (a - hi.astype(F32)).astype(BF16)
    return _dot(hi, m) + _dot(lo, m)


@jax.custom_vjp
def _group_mean(a, m):
    return _dot_exact_rhs(a, m)


def _group_mean_fwd(a, m):
    return _group_mean(a, m), m


def _group_mean_bwd(m, g):
    return _dot_exact_rhs(g, m), jnp.zeros_like(m)


_group_mean.defvjp(_group_mean_fwd, _group_mean_bwd)


def _roll_rows(a, shift):
    return pltpu.roll(a, shift, 0)


def _cumsum_rows_raw(a, reverse):
    n = a.shape[0]
    row = lax.broadcasted_iota(jnp.int32, a.shape, 0)
    s = 1
    while s < n:
        if reverse:
            a = a + jnp.where(row < n - s, _roll_rows(a, n - s), 0.0)
        else:
            a = a + jnp.where(row >= s, _roll_rows(a, s), 0.0)
        s *= 2
    return a


@functools.partial(jax.custom_vjp, nondiff_argnums=(1,))
def _cumsum_rows(a, reverse):
    return _cumsum_rows_raw(a, reverse)


def _cumsum_rows_fwd(a, reverse):
    return _cumsum_rows_raw(a, reverse), None


def _cumsum_rows_bwd(reverse, _, g):
    return (_cumsum_rows_raw(g, not reverse),)


_cumsum_rows.defvjp(_cumsum_rows_fwd, _cumsum_rows_bwd)


def _rms(x, g):
    r = lax.rsqrt(jnp.mean(x * x, axis=-1, keepdims=True) + EPS)
    return x * r * g


def _rms_bwd(x, g, dy):
    r = lax.rsqrt(jnp.mean(x * x, axis=-1, keepdims=True) + EPS)
    xh = x * r
    dg = jnp.sum(dy * xh, axis=0, keepdims=True)
    dxh = dy * g
    dx = r * (dxh - xh * jnp.mean(dxh * xh, axis=-1, keepdims=True))
    return dx, dg


def _sigmoid(a):
    return jax.nn.sigmoid(a)


def _mesh_place():
    x, y, c = lax.axis_index("x"), lax.axis_index("y"), lax.axis_index("c")
    return x, y, c


def _dev_index(p):
    return 4 * p[0] + 2 * p[1] + p[2]


def _comm_sems(n):
    return [pltpu.SemaphoreType.DMA((n, 7)), pltpu.SemaphoreType.DMA((n, 7)), pltpu.SemaphoreType.DMA((n,))]


def _gather_protocol(ins, outs, send_sems, recv_sems, local_sems):
    n = len(ins)
    x, y, c = _mesh_place()
    me, sibling = (x, y, c), (x, y, 1 - c)
    chips = [(1 - x, y), (x, 1 - y), (1 - x, 1 - y)]

    def copy(a, k, block, to, src=None):
        slot = outs[a].at[_dev_index(block)]
        return pltpu.make_async_remote_copy(
            src_ref=slot if src is None else src, dst_ref=slot,
            send_sem=send_sems.at[a, k], recv_sem=recv_sems.at[a, k], device_id=to, device_id_type=MESH)

    def mine(a):
        return pltpu.make_async_copy(ins[a], outs[a].at[_dev_index(me)], local_sems.at[a])

    def first(a):
        return [copy(a, 0, me, sibling, src=ins[a])] + [copy(a, 1 + j, me, (*chip, c), src=ins[a]) for j, chip in enumerate(chips)]

    def start():
        for a in range(n):
            mine(a).start()
            for cp in first(a):
                cp.start()

    def forward():
        for a in range(n):
            for j, chip in enumerate(chips):
                copy(a, 1 + j, (*chip, c), me).wait_recv()
                copy(a, 4 + j, (*chip, c), sibling).start()

    def finish():
        for a in range(n):
            copy(a, 0, sibling, me).wait_recv()
            for j, chip in enumerate(chips):
                copy(a, 4 + j, (*chip, 1 - c), me).wait_recv()
        for a in range(n):
            mine(a).wait()
            for cp in first(a):
                cp.wait_send()
            for j, chip in enumerate(chips):
                copy(a, 4 + j, (*chip, c), sibling).wait_send()

    return start, forward, finish


def _exchange_protocol(ins, outs, scatter, send_sems, recv_sems, local_sems):
    n = len(ins)
    x, y, c = _mesh_place()
    me = (x, y, c)
    my_id = _dev_index(me)
    rels = [(dx, dy, dc) for dx in (0, 1) for dy in (0, 1) for dc in (0, 1)][1:]

    def peer_of(rel):
        return tuple(1 - v if d else v for v, d in zip(me, rel))

    def src(a, dev):
        return ins[a].at[dev] if scatter[a] else ins[a]

    def send(a, k):
        peer = peer_of(rels[k])
        return pltpu.make_async_remote_copy(
            src_ref=src(a, _dev_index(peer)), dst_ref=outs[a].at[my_id],
            send_sem=send_sems.at[a, k], recv_sem=recv_sems.at[a, k], device_id=peer, device_id_type=MESH)

    def arrival(a, k):
        peer = peer_of(rels[k])
        return pltpu.make_async_remote_copy(
            src_ref=src(a, my_id), dst_ref=outs[a].at[_dev_index(peer)],
            send_sem=send_sems.at[a, k], recv_sem=recv_sems.at[a, k], device_id=peer, device_id_type=MESH)

    def own(a):
        return pltpu.make_async_copy(src(a, my_id), outs[a].at[my_id], local_sems.at[a])

    def start():
        for a in range(n):
            own(a).start()
            for k in range(7):
                send(a, k).start()

    def finish():
        for a in range(n):
            for k in range(7):
                arrival(a, k).wait_recv()
        for a in range(n):
            for k in range(7):
                send(a, k).wait_send()
            own(a).wait()

    return start, finish


def _slot_shapes(blocks, scatter=None):
    return [jax.ShapeDtypeStruct(b.shape if (scatter and scatter[a]) else (N_DEV,) + b.shape, b.dtype) for a, b in enumerate(blocks)]


def _all_gather_call(blocks):
    n = len(blocks)

    def body(*refs):
        start, forward, finish = _gather_protocol(refs[:n], refs[n:2 * n], *refs[2 * n:])
        start()
        forward()
        finish()

    any_spec = pl.BlockSpec(memory_space=pl.ANY)
    return pl.pallas_call(
        body, name="weights_all_gather", out_shape=_slot_shapes(blocks),
        in_specs=[any_spec] * n, out_specs=[any_spec] * n, scratch_shapes=_comm_sems(n),
    )(*blocks)


def _sum_slots_call(recvs):
    n = len(recvs)

    def body(*refs):
        for in_ref, out_ref in zip(refs[:n], refs[n:]):
            acc = in_ref[0]
            for j in range(1, N_DEV):
                acc = acc + in_ref[j]
            out_ref[...] = acc

    return pl.pallas_call(
        body, name="small_grad_sum", out_shape=[jax.ShapeDtypeStruct(r.shape[1:], F32) for r in recvs],
        compiler_params=_params(),
    )(*recvs)


def _adam_update(w, g, m, v):
    nm = ADAM_B1 * m + (1.0 - ADAM_B1) * g
    nv = ADAM_B2 * v + (1.0 - ADAM_B2) * (g * g)
    bc1 = 1.0 - ADAM_B1 ** ADAM_STEP
    bc2 = 1.0 - ADAM_B2 ** ADAM_STEP
    return -ADAM_LR * ((nm / bc1) / (jnp.sqrt(nv / bc2) + ADAM_EPS) + ADAM_WD * w), nm, nv


def _adamw_recv(w, recv, m, v, tag):
    r, c = w.shape
    tr = r
    for cand in (512, 256, 128):
        if r > cand and r % cand == 0:
            tr = cand
            break

    def body(w_ref, r_ref, m_ref, v_ref, g_ref, d_ref, nm_ref, nv_ref):
        g = r_ref[0].astype(F32)
        for j in range(1, N_DEV):
            g = g + r_ref[j].astype(F32)
        g_ref[...] = g
        d_ref[...], nm_ref[...], nv_ref[...] = _adam_update(w_ref[...], g, m_ref[...], v_ref[...])

    spec = pl.BlockSpec((tr, c), lambda i: (i, 0))
    return pl.pallas_call(
        body, name="adamw_" + tag, out_shape=[jax.ShapeDtypeStruct(w.shape, F32)] * 4, grid=(r // tr,),
        in_specs=[spec, pl.BlockSpec((N_DEV, tr, c), lambda i: (0, i, 0)), spec, spec], out_specs=[spec] * 4,
        compiler_params=_params(),
    )(w, recv, m, v)


def _adamw_recv_hosting(ws, recvs, ms, vs, blocks, scatter):
    n, ne = len(ws), len(blocks)

    def body(*refs):
        ins, ex_in = refs[:4 * n], refs[4 * n:4 * n + ne]
        outs, ex_out = refs[4 * n + ne:8 * n + ne], refs[8 * n + ne:8 * n + 2 * ne]
        start, finish = _exchange_protocol(ex_in, ex_out, scatter, *refs[8 * n + 2 * ne:])
        start()
        for a in range(n):
            r_ref = ins[n + a]
            g = r_ref[0].astype(F32)
            for j in range(1, N_DEV):
                g = g + r_ref[j].astype(F32)
            outs[a][...] = g
            outs[n + a][...], outs[2 * n + a][...], outs[3 * n + a][...] = _adam_update(
                ins[a][...], g, ins[2 * n + a][...], ins[3 * n + a][...])
        finish()

    vmem, any_spec = pl.BlockSpec(memory_space=pltpu.VMEM), pl.BlockSpec(memory_space=pl.ANY)
    out = pl.pallas_call(
        body, name="adamw_late_and_grad_exchange",
        out_shape=[jax.ShapeDtypeStruct(w.shape, F32) for w in ws] * 4 + _slot_shapes(blocks, scatter),
        in_specs=[vmem] * (4 * n) + [any_spec] * ne, out_specs=[vmem] * (4 * n) + [any_spec] * ne,
        scratch_shapes=_comm_sems(ne), compiler_params=_params(),
    )(*ws, *recvs, *ms, *vs, *blocks)
    return out[:n], out[n:2 * n], out[2 * n:3 * n], out[3 * n:4 * n], out[4 * n:]


def _adamw_small(ws, gs, ms, vs):
    n = len(ws)

    def body(*refs):
        ins, outs = refs[:4 * n], refs[4 * n:]
        for a in range(n):
            d, nm, nv = _adam_update(ins[a][...], ins[n + a][...], ins[2 * n + a][...], ins[3 * n + a][...])
            outs[a][...], outs[n + a][...], outs[2 * n + a][...] = d, nm, nv

    out = pl.pallas_call(
        body, name="adamw_small", out_shape=[jax.ShapeDtypeStruct(w.shape, F32) for w in ws] * 3, compiler_params=_params(),
    )(*ws, *gs, *ms, *vs)
    return out[:n], out[n:2 * n], out[2 * n:]


def _tile(t, want):
    return want if t % want == 0 else t


def _inproj(x, g1, w_in, tm):
    t = x.shape[0]

    def body(x_ref, g_ref, w_ref, *outs):
        for j in range(tm // min(tm, ROW_SUB)):
            r = pl.ds(j * min(tm, ROW_SUB), min(tm, ROW_SUB))
            h = _rms(x_ref[r, :], g_ref[...]).astype(BF16)
            off = 0
            for o_ref, wd in zip(outs, IN_WIDTHS):
                o_ref[r, :] = _dot_nt(h, w_ref[off:off + wd, :])
                off += wd

    return pl.pallas_call(
        body, name="inproj_fwd", grid=(t // tm,),
        out_shape=[jax.ShapeDtypeStruct((t, wd), F32) for wd in IN_WIDTHS],
        in_specs=[pl.BlockSpec((tm, D_MODEL), lambda i: (i, 0)), _const_spec((1, D_MODEL)), _const_spec((D_IN_PAD, D_MODEL))],
        out_specs=[pl.BlockSpec((tm, wd), lambda i: (i, 0)) for wd in IN_WIDTHS],
        compiler_params=_params(),
    )(x, g1, w_in)


def _rope_tables(seq):
    inv = 1.0 / (ROPE_THETA ** (jnp.arange(0, B_ROPE, 2, dtype=F32) / B_ROPE))
    ang = jnp.arange(seq, dtype=F32)[:, None] * inv[None, :]
    cos, sin = jnp.cos(ang), jnp.sin(ang)
    z32, z64 = jnp.zeros_like(cos), jnp.zeros((seq, 64), F32)
    cos_t = jnp.concatenate([cos, cos, z64], axis=1)
    sin_a = jnp.concatenate([-sin, z32, z64], axis=1)
    sin_b = jnp.concatenate([z32, sin, z64], axis=1)
    return cos_t, sin_a, sin_b


def _rope(t, cos_t, sin_a, sin_b):
    return t * cos_t + pltpu.roll(t, 96, 1) * sin_a + pltpu.roll(t, 32, 1) * sin_b


def _rope_t(d, cos_t, sin_a, sin_b):
    return d * cos_t + pltpu.roll(d * sin_a, 32, 1) + pltpu.roll(d * sin_b, 96, 1)


def _mla_qkv(cq, ckv, kr, g_qa, g_kva, w_q, w_kv, tables, seq, tm):
    t = cq.shape[0]
    nblk = seq // tm

    def body(cq_ref, ckv_ref, kr_ref, gq_ref, gk_ref, wq_ref, wkv_ref, c_ref, sa_ref, sb_ref, q_out, k_out, v_out):
        cos_t, sin_a, sin_b = c_ref[...], sa_ref[...], sb_ref[...]
        cqn = _rms(cq_ref[...], gq_ref[...]).astype(BF16)
        ckn = _rms(ckv_ref[...], gk_ref[...]).astype(BF16)
        kr_rot = _rope(kr_ref[...], cos_t, sin_a, sin_b).astype(BF16)
        for h in range(B_HEADS):
            lo = h * QK_PAD
            q_out[:, lo:lo + 128] = (_dot_nt(cqn, wq_ref[lo:lo + 128, :]) * ATTN_SCALE).astype(BF16)
            qr = _rope(_dot_nt(cqn, wq_ref[lo + 128:lo + 256, :]), cos_t, sin_a, sin_b)
            q_out[:, lo + 128:lo + 256] = (qr * ATTN_SCALE).astype(BF16)
            k_out[:, lo:lo + 128] = _dot(ckn, wkv_ref[:, lo:lo + 128]).astype(BF16)
            k_out[:, lo + 128:lo + 256] = kr_rot
            v_out[:, h * B_V:(h + 1) * B_V] = _dot(ckn, wkv_ref[:, lo + 128:lo + 256]).astype(BF16)

    tok = lambda wd: pl.BlockSpec((tm, wd), lambda i: (i, 0))
    tab = pl.BlockSpec((tm, 128), lambda i: (i % nblk, 0))
    return pl.pallas_call(
        body, name="mla_qkv_fwd", grid=(t // tm,),
        out_shape=[jax.ShapeDtypeStruct((t, B_HEADS * QK_PAD), BF16), jax.ShapeDtypeStruct((t, B_HEADS * QK_PAD), BF16),
                   jax.ShapeDtypeStruct((t, B_HEADS * B_V), BF16)],
        in_specs=[tok(Q_LORA), tok(KV_LORA), tok(128), _const_spec((1, Q_LORA)), _const_spec((1, KV_LORA)),
                  _const_spec((B_HEADS * QK_PAD, Q_LORA)), _const_spec((KV_LORA, 1024)), tab, tab, tab],
        out_specs=[tok(B_HEADS * QK_PAD), tok(B_HEADS * QK_PAD), tok(B_HEADS * B_V)],
        compiler_params=_params(),
    )(cq, ckv, kr, g_qa, g_kva, w_q, w_kv, *tables)


def _step_index(nq):
    return (pl.program_id(0) * B_HEADS + pl.program_id(1)) * nq + pl.program_id(2)


def _attn_fwd(qcat, kcat, v, nb, seq, tq, gather=()):
    t = qcat.shape[0]
    nq = seq // tq
    ng = len(gather)
    steps = nb * B_HEADS * nq

    def body(q_ref, k_ref, v_ref, *rest):
        o_ref, lse_ref = rest[ng:ng + 2]
        if ng:
            start, forward, finish = _gather_protocol(rest[:ng], rest[ng + 2:2 * ng + 2], *rest[2 * ng + 2:])
            pl.when(_step_index(nq) == 0)(start)
            pl.when(_step_index(nq) == (3 * steps) // 4)(forward)
        for j in range(tq // ATTN_SUB):
            r = pl.ds(j * ATTN_SUB, ATTN_SUB)
            s = _dot_nt(q_ref[r, :], k_ref[...])
            m = jnp.max(s, axis=-1, keepdims=True)
            p = jnp.exp(s - m)
            l = jnp.sum(p, axis=-1, keepdims=True)
            o_ref[r, :] = _dot(p.astype(BF16), v_ref[...]) / l
            lse_ref[0, r, :] = m + jnp.log(l)
        if ng:
            pl.when(_step_index(nq) == steps - 1)(finish)

    any_spec = pl.BlockSpec(memory_space=pl.ANY)
    return pl.pallas_call(
        body, name="attn_fwd", grid=(nb, B_HEADS, nq),
        out_shape=[jax.ShapeDtypeStruct((t, B_HEADS * B_V), F32), jax.ShapeDtypeStruct((B_HEADS, t, 1), F32)] + _slot_shapes(gather),
        in_specs=[pl.BlockSpec((tq, QK_PAD), lambda b, h, i: (b * nq + i, h)),
                  pl.BlockSpec((seq, QK_PAD), lambda b, h, i: (b, h)),
                  pl.BlockSpec((seq, B_V), lambda b, h, i: (b, h))] + [any_spec] * ng,
        out_specs=[pl.BlockSpec((tq, B_V), lambda b, h, i: (b * nq + i, h)),
                   pl.BlockSpec((1, tq, 1), lambda b, h, i: (h, b * nq + i, 0))] + [any_spec] * ng,
        scratch_shapes=_comm_sems(ng) if ng else [],
        compiler_params=_params(),
    )(qcat, kcat, v, *gather)


def _attn_bwd(qcat, kcat, v, o, lse, do, nb, seq, tq, exchange=()):
    t = qcat.shape[0]
    nq = seq // tq
    ne = len(exchange)
    steps = nb * B_HEADS * nq

    def body(q_ref, k_ref, v_ref, o_ref, lse_ref, do_ref, *rest):
        dq_ref, dk_ref, dv_ref = rest[ne:ne + 3]
        if ne:
            start, finish = _exchange_protocol(rest[:ne], rest[ne + 3:2 * ne + 3], [True] * ne, *rest[2 * ne + 3:])
            pl.when(_step_index(nq) == 0)(start)

        @pl.when(pl.program_id(2) == 0)
        def _():
            dv_ref[...] = jnp.zeros_like(dv_ref)
            dk_ref[...] = jnp.zeros_like(dk_ref)

        for j in range(tq // ATTN_SUB_BWD):
            r = pl.ds(j * ATTN_SUB_BWD, ATTN_SUB_BWD)
            q, k = q_ref[r, :], k_ref[...]
            do_f = do_ref[r, :]
            delta = jnp.sum(do_f * o_ref[r, :], axis=-1, keepdims=True)
            dob = do_f.astype(BF16)
            p = jnp.exp(_dot_nt(q, k) - lse_ref[0, r, :])
            ds = (p * (_dot_nt(dob, v_ref[...]) - delta)).astype(BF16)
            dq_ref[r, :] = _dot(ds, k).astype(dq_ref.dtype)
            dv_ref[...] += _dot_tn(p.astype(BF16), dob)
            dk_ref[...] += _dot_tn(ds, q)
        if ne:
            pl.when(_step_index(nq) == steps - 1)(finish)

    qspec = lambda wd: pl.BlockSpec((tq, wd), lambda b, h, i: (b * nq + i, h))
    kspec = lambda wd: pl.BlockSpec((seq, wd), lambda b, h, i: (b, h))
    any_spec = pl.BlockSpec(memory_space=pl.ANY)
    return pl.pallas_call(
        body, name="attn_bwd", grid=(nb, B_HEADS, nq),
        out_shape=[jax.ShapeDtypeStruct((t, B_HEADS * QK_PAD), BF16), jax.ShapeDtypeStruct((t, B_HEADS * QK_PAD), F32),
                   jax.ShapeDtypeStruct((t, B_HEADS * B_V), F32)] + _slot_shapes(exchange, [True] * ne),
        in_specs=[qspec(QK_PAD), kspec(QK_PAD), kspec(B_V), qspec(B_V),
                  pl.BlockSpec((1, tq, 1), lambda b, h, i: (h, b * nq + i, 0)), qspec(B_V)] + [any_spec] * ne,
        out_specs=[qspec(QK_PAD), kspec(QK_PAD), kspec(B_V)] + [any_spec] * ne,
        scratch_shapes=_comm_sems(ne) if ne else [],
        compiler_params=_params(),
    )(qcat, kcat, v, o, lse, do, *exchange)


def _gla_consts(reverse):
    row = lax.broadcasted_iota(jnp.int32, (CHUNK, CHUNK), 0)
    col = lax.broadcasted_iota(jnp.int32, (CHUNK, CHUNK), 1)
    causal = (row <= col) if reverse else (row >= col)
    lane = lax.broadcasted_iota(jnp.int32, (1, HEAD_PAIR), 1)
    m0 = (lane < 64).astype(F32)
    m1 = 1.0 - m0
    r2 = lax.broadcasted_iota(jnp.int32, (HEAD_PAIR, HEAD_PAIR), 0)
    c2 = lax.broadcasted_iota(jnp.int32, (HEAD_PAIR, HEAD_PAIR), 1)
    same_head = ((r2 < 64) == (c2 < 64)).astype(F32)
    return causal, m0, m1, same_head


def _gla_chunk(hq, hi, z, l0, l1, st, consts, reverse):
    q_dec, k_inv, k_end, decay = _gla_gates(hq, z, l0, l1, reverse)
    o, st_new = _gla_state(q_dec, st, decay, _gla_increment(hi, k_end, consts))
    return o + _gla_intra(q_dec, k_inv, hi, consts), st_new


def _gla_gates(hq, z, l0, l1, reverse):
    mx = jnp.maximum(l0, l1)
    e0, e1 = jnp.exp(l0 - mx), jnp.exp(l1 - mx)
    lb = e0 / (e0 + e1)
    q = hq * _sigmoid(hq)
    sz = _sigmoid(z)
    log_f = jnp.log(lb + (1.0 - lb) * sz)
    k = (1.0 - lb) * (1.0 - sz)
    cum = _cumsum_rows(log_f, reverse)
    decay = jnp.exp(jnp.sum(log_f, axis=0, keepdims=True))
    k_inv = k * jnp.exp(-cum)
    return q * jnp.exp(cum), k_inv, k_inv * decay, decay


def _gla_intra(q_dec, k_inv, hi, consts):
    causal, m0, m1, _ = consts
    o = None
    for mh in (m0, m1):
        s = jnp.where(causal, _mm_nt(q_dec * mh, k_inv), 0.0)
        part = _mm(s, hi) * mh
        o = part if o is None else o + part
    return o


def _gla_increment(hi, k_end, consts):
    return _mm_tn(hi, k_end) * consts[3]


def _gla_state(q_dec, st, decay, inc):
    return _mm_nt(q_dec, st), st * decay + inc


GLA_DIRS = (False, True)
GLA_BATCH_FWD = 8
GLA_BATCH_BWD = 4


def _gla_fwd(hq, hi, zs, lbls, nb, seq, group):
    t = hq.shape[0]
    rows = group * CHUNK
    nblk = seq // rows
    n_chunks = seq // CHUNK
    nd = len(GLA_DIRS)

    def body(*refs):
        ins, outs, st_refs = refs[:4 * nd], refs[4 * nd:6 * nd], refs[6 * nd:]
        @pl.when(pl.program_id(2) == 0)
        def _():
            for st_ref in st_refs:
                st_ref[...] = jnp.zeros_like(st_ref)

        consts = [_gla_consts(rev) for rev in GLA_DIRS]
        work = [(d, rev, group - 1 - cc if rev else cc) for cc in range(group) for d, rev in enumerate(GLA_DIRS)]
        rows_of = lambda c: pl.ds(c * CHUNK, CHUNK)
        sts = [st_ref[...] for st_ref in st_refs]
        for w0 in range(0, len(work), GLA_BATCH_FWD):
            batch = work[w0:w0 + GLA_BATCH_FWD]
            gates, intra, incs = {}, {}, {}
            for d, rev, c in batch:
                hq_ref, _, z_ref, lbl_ref = ins[4 * d:4 * d + 4]
                gates[d, c] = _gla_gates(hq_ref[rows_of(c), :], z_ref[rows_of(c), :], lbl_ref[0:1, :], lbl_ref[1:2, :], rev)
            for d, rev, c in batch:
                hi_c = ins[4 * d + 1][rows_of(c), :]
                intra[d, c] = _gla_intra(gates[d, c][0], gates[d, c][1], hi_c, consts[d])
                incs[d, c] = _gla_increment(hi_c, gates[d, c][2], consts[d])
            for d, rev, c in batch:
                outs[nd + d][0, 0, c] = sts[d]
                o_state, sts[d] = _gla_state(gates[d, c][0], sts[d], gates[d, c][3], incs[d, c])
                outs[d][rows_of(c), :] = intra[d, c] + o_state
        for st_ref, st in zip(st_refs, sts):
            st_ref[...] = st

    def tb(rev):
        return (lambda i: nblk - 1 - i) if rev else (lambda i: i)

    tok = lambda rev: pl.BlockSpec((rows, HEAD_PAIR), lambda b, p, i: (b * nblk + tb(rev)(i), p))
    lspec = pl.BlockSpec((2, HEAD_PAIR), lambda b, p, i: (0, p))
    sspec = lambda rev: pl.BlockSpec((1, 1, group, HEAD_PAIR, HEAD_PAIR), lambda b, p, i: (b, p, tb(rev)(i), 0, 0))
    args, in_specs = [], []
    for d, rev in enumerate(GLA_DIRS):
        args += [hq, hi, zs[d], lbls[d]]
        in_specs += [tok(rev), tok(rev), tok(rev), lspec]
    return pl.pallas_call(
        body, name="gla_fwd", grid=(nb, 4, nblk),
        out_shape=[jax.ShapeDtypeStruct((t, A_WIDTH), F32)] * nd
        + [jax.ShapeDtypeStruct((nb, 4, n_chunks, HEAD_PAIR, HEAD_PAIR), F32)] * nd,
        in_specs=in_specs, out_specs=[tok(rev) for rev in GLA_DIRS] + [sspec(rev) for rev in GLA_DIRS],
        scratch_shapes=[pltpu.VMEM((HEAD_PAIR, HEAD_PAIR), F32)] * nd,
        compiler_params=_params(),
    )(*args)


def _gla_bwd(hq, hi, zs, lbls, saved, do, nb, seq, group):
    t = hq.shape[0]
    rows = group * CHUNK
    nblk = seq // rows
    nd = len(GLA_DIRS)

    def body(*refs):
        ins, outs, dst_refs = refs[:6 * nd], refs[6 * nd:10 * nd], refs[10 * nd:]
        dl_refs = outs[3 * nd:]

        @pl.when(pl.program_id(2) == 0)
        def _():
            for dst_ref, dl_ref in zip(dst_refs, dl_refs):
                dst_ref[...] = jnp.zeros_like(dst_ref)
                dl_ref[...] = jnp.zeros_like(dl_ref)

        consts = [_gla_consts(rev) for rev in GLA_DIRS]
        dsts = [dst_ref[...] for dst_ref in dst_refs]
        dls = [[jnp.zeros((1, HEAD_PAIR), F32), jnp.zeros((1, HEAD_PAIR), F32)] for _ in GLA_DIRS]
        work = [(d, rev, cc if rev else group - 1 - cc) for cc in range(group) for d, rev in enumerate(GLA_DIRS)]
        for w0 in range(0, len(work), GLA_BATCH_BWD):
            vjps = {}
            for d, rev, c in work[w0:w0 + GLA_BATCH_BWD]:
                hq_ref, hi_ref, z_ref, lbl_ref, save_ref, _ = ins[6 * d:6 * d + 6]
                r = pl.ds(c * CHUNK, CHUNK)
                fn = functools.partial(_gla_chunk, consts=consts[d], reverse=rev)
                _, vjps[d, c] = jax.vjp(fn, hq_ref[r, :], hi_ref[r, :], z_ref[r, :], lbl_ref[0:1, :], lbl_ref[1:2, :], save_ref[0, 0, c])
            for d, rev, c in work[w0:w0 + GLA_BATCH_BWD]:
                dq_ref, dv_ref, dz_ref = outs[3 * d:3 * d + 3]
                r = pl.ds(c * CHUNK, CHUNK)
                d_hq, d_hi, d_z, d_l0, d_l1, dsts[d] = vjps[d, c]((ins[6 * d + 5][r, :], dsts[d]))
                dq_ref[r, :] = d_hq.astype(dq_ref.dtype)
                dv_ref[r, :] = d_hi.astype(dv_ref.dtype)
                dz_ref[r, :] = d_z.astype(dz_ref.dtype)
                dls[d] = [dls[d][0] + d_l0, dls[d][1] + d_l1]
        for d in range(nd):
            dst_refs[d][...] = dsts[d]
            dl_refs[d][0, 0:1, :] += dls[d][0]
            dl_refs[d][0, 1:2, :] += dls[d][1]

    def tb(rev):
        return (lambda i: i) if rev else (lambda i: nblk - 1 - i)

    tok = lambda rev: pl.BlockSpec((rows, HEAD_PAIR), lambda b, p, i: (b * nblk + tb(rev)(i), p))
    lspec = pl.BlockSpec((2, HEAD_PAIR), lambda b, p, i: (0, p))
    sspec = lambda rev: pl.BlockSpec((1, 1, group, HEAD_PAIR, HEAD_PAIR), lambda b, p, i: (b, p, tb(rev)(i), 0, 0))
    args, in_specs, out_specs = [], [], []
    for d, rev in enumerate(GLA_DIRS):
        args += [hq, hi, zs[d], lbls[d], saved[d], do]
        in_specs += [tok(rev), tok(rev), tok(rev), lspec, sspec(rev), tok(rev)]
        out_specs += [tok(rev)] * 3
    out_specs += [pl.BlockSpec((1, 2, HEAD_PAIR), lambda b, p, i: (b, 0, p))] * nd
    return pl.pallas_call(
        body, name="gla_bwd", grid=(nb, 4, nblk),
        out_shape=[jax.ShapeDtypeStruct((t, A_WIDTH), BF16)] * (3 * nd) + [jax.ShapeDtypeStruct((nb, 2, A_WIDTH), F32)] * nd,
        in_specs=in_specs, out_specs=out_specs,
        scratch_shapes=[pltpu.VMEM((HEAD_PAIR, HEAD_PAIR), F32)] * nd,
        compiler_params=_params(),
    )(*args)


def _head_mean_matrix():
    r = lax.broadcasted_iota(jnp.int32, (A_WIDTH, A_WIDTH), 0) // 64
    c = lax.broadcasted_iota(jnp.int32, (A_WIDTH, A_WIDTH), 1) // 64
    return jnp.where(r == c, 1.0 / 64.0, 0.0).astype(BF16)


def _gla_out(o_f, o_b, hg, g, mean_mat):
    o = o_f + o_b
    ms = _group_mean(o * o, mean_mat)
    return o * lax.rsqrt(ms + EPS) * g * (hg * _sigmoid(hg))


def _gla_combine(o_f, o_b, hg, g, tm):
    t = o_f.shape[0]

    def body(of_ref, ob_ref, hg_ref, g_ref, y_ref):
        y_ref[...] = _gla_out(of_ref[...], ob_ref[...], hg_ref[...], g_ref[...], _head_mean_matrix())

    tok = pl.BlockSpec((tm, A_WIDTH), lambda i: (i, 0))
    return pl.pallas_call(
        body, name="gla_combine_fwd", grid=(t // tm,), out_shape=jax.ShapeDtypeStruct((t, A_WIDTH), F32),
        in_specs=[tok, tok, tok, _const_spec((1, A_WIDTH))], out_specs=tok, compiler_params=_params(),
    )(o_f, o_b, hg, g)


def _gla_combine_bwd(o_f, o_b, hg, g, dy, tm):
    t = o_f.shape[0]

    def body(of_ref, ob_ref, hg_ref, g_ref, dy_ref, do_ref, dhg_ref, dg_ref):
        mean_mat = _head_mean_matrix()
        fn = lambda o, hgv, gv: _gla_out(o, jnp.zeros_like(o), hgv, gv, mean_mat)
        _, vjp = jax.vjp(fn, of_ref[...] + ob_ref[...], hg_ref[...], g_ref[...])
        d_o, d_hg, d_g = vjp(dy_ref[...])
        do_ref[...] = d_o
        dhg_ref[...] = d_hg.astype(dhg_ref.dtype)

        @pl.when(pl.program_id(0) == 0)
        def _():
            dg_ref[...] = jnp.zeros_like(dg_ref)

        dg_ref[...] += d_g

    tok = pl.BlockSpec((tm, A_WIDTH), lambda i: (i, 0))
    vec = pl.BlockSpec((1, A_WIDTH), lambda i: (0, 0))
    return pl.pallas_call(
        body, name="gla_combine_bwd", grid=(t // tm,),
        out_shape=[jax.ShapeDtypeStruct((t, A_WIDTH), F32), jax.ShapeDtypeStruct((t, A_WIDTH), BF16),
                   jax.ShapeDtypeStruct((1, A_WIDTH), F32)],
        in_specs=[tok, tok, tok, _const_spec((1, A_WIDTH)), tok], out_specs=[tok, tok, vec], compiler_params=_params(),
    )(o_f, o_b, hg, g, dy)


def _post_fwd(x, ya, oattn, tgt, g_mla, w_out, g2, w_gate, w_up, w_down, g_fin, tm):
    t = x.shape[0]

    def body(x_ref, ya_ref, oa_ref, tgt_ref, gm_ref, wo_ref, g2_ref, wg_ref, wu_ref, wd_ref, gf_ref,
             x1_ref, x2_ref, gate_ref, up_ref, loss_ref):
        part = jnp.zeros((1, 1), F32)
        for j in range(tm // min(tm, ROW_SUB)):
            r = pl.ds(j * min(tm, ROW_SUB), min(tm, ROW_SUB))
            yb = _rms(oa_ref[r, :], gm_ref[...])
            x1 = x_ref[r, :] + _dot(ya_ref[r, :].astype(BF16), wo_ref[0:A_WIDTH, :]) + _dot(yb.astype(BF16), wo_ref[A_WIDTH:, :])
            x1_ref[r, :] = x1
            h2 = _rms(x1, g2_ref[...]).astype(BF16)
            gate, up = _dot_nt(h2, wg_ref[...]), _dot_nt(h2, wu_ref[...])
            gate_ref[r, :] = gate.astype(BF16)
            up_ref[r, :] = up.astype(BF16)
            act = (gate * _sigmoid(gate) * up).astype(BF16)
            x2 = x1 + _dot(act, wd_ref[...])
            x2_ref[r, :] = x2
            err = _rms(x2, gf_ref[...]) - tgt_ref[r, :]
            part = part + 0.5 * jnp.sum(jnp.mean(err * err, axis=-1, keepdims=True), axis=0, keepdims=True)

        @pl.when(pl.program_id(0) == 0)
        def _():
            loss_ref[...] = jnp.zeros_like(loss_ref)

        loss_ref[...] += jnp.broadcast_to(part, loss_ref.shape)

    tok = lambda wd: pl.BlockSpec((tm, wd), lambda i: (i, 0))
    return pl.pallas_call(
        body, name="post_fwd", grid=(t // tm,),
        out_shape=[jax.ShapeDtypeStruct((t, D_MODEL), F32)] * 2 + [jax.ShapeDtypeStruct((t, D_FF), BF16)] * 2
        + [jax.ShapeDtypeStruct((1, 128), F32)],
        in_specs=[tok(D_MODEL), tok(A_WIDTH), tok(512), tok(D_MODEL), _const_spec((1, 512)), _const_spec((D_MODEL, D_MODEL)),
                  _const_spec((1, D_MODEL)), _const_spec((D_FF, D_MODEL)), _const_spec((D_FF, D_MODEL)),
                  _const_spec((D_FF, D_MODEL)), _const_spec((1, D_MODEL))],
        out_specs=[tok(D_MODEL), tok(D_MODEL), tok(D_FF), tok(D_FF), pl.BlockSpec((1, 128), lambda i: (0, 0))],
        compiler_params=_params(),
    )(x, ya, oattn, tgt, g_mla, w_out, g2, w_gate, w_up, w_down, g_fin)


def _post_bwd(x1, x2, gate_b, up_b, ya, oattn, tgt, g_mla, w_out, g2, w_gate, w_up, w_down, g_fin, tm):
    t = x1.shape[0]

    def body(x1_ref, x2_ref, gate_ref, up_ref, ya_ref, oa_ref, tgt_ref, gm_ref, wo_ref, g2_ref, wg_ref, wu_ref, wd_ref, gf_ref,
             dx1_ref, dya_ref, doa_ref, ycat_ref, dx1b_ref, h2_ref, dgate_ref, dup_ref, act_ref, dx2b_ref,
             dgm_ref, dg2_ref, dgf_ref):
        x1, x2 = x1_ref[...], x2_ref[...]
        dy = (_rms(x2, gf_ref[...]) - tgt_ref[...]) * (1.0 / D_MODEL)
        dx2, dgf = _rms_bwd(x2, gf_ref[...], dy)
        dx2b = dx2.astype(BF16)
        dx2b_ref[...] = dx2b
        h2_ref[...] = _rms(x1, g2_ref[...]).astype(BF16)
        gate, up = gate_ref[...].astype(F32), up_ref[...].astype(F32)
        sg = _sigmoid(gate)
        sl = gate * sg
        act_ref[...] = (sl * up).astype(BF16)
        dact = _dot_nt(dx2b, wd_ref[...])
        dup = (dact * sl).astype(BF16)
        dgate = (dact * up * (sg * (1.0 + gate * (1.0 - sg)))).astype(BF16)
        dup_ref[...] = dup
        dgate_ref[...] = dgate
        dh2 = _dot(dgate, wg_ref[...]) + _dot(dup, wu_ref[...])
        dx1n, dg2 = _rms_bwd(x1, g2_ref[...], dh2)
        dx1 = dx2 + dx1n
        dx1_ref[...] = dx1
        dx1b = dx1.astype(BF16)
        dx1b_ref[...] = dx1b
        oa = oa_ref[...]
        ycat_ref[:, 0:A_WIDTH] = ya_ref[...].astype(BF16)
        ycat_ref[:, A_WIDTH:] = _rms(oa, gm_ref[...]).astype(BF16)
        dya_ref[...] = _dot_nt(dx1b, wo_ref[0:A_WIDTH, :])
        doa, dgm = _rms_bwd(oa, gm_ref[...], _dot_nt(dx1b, wo_ref[A_WIDTH:, :]))
        doa_ref[...] = doa

        @pl.when(pl.program_id(0) == 0)
        def _():
            dgm_ref[...] = jnp.zeros_like(dgm_ref)
            dg2_ref[...] = jnp.zeros_like(dg2_ref)
            dgf_ref[...] = jnp.zeros_like(dgf_ref)

        dgm_ref[...] += dgm
        dg2_ref[...] += dg2
        dgf_ref[...] += dgf

    tok = lambda wd: pl.BlockSpec((tm, wd), lambda i: (i, 0))
    vec = lambda wd: pl.BlockSpec((1, wd), lambda i: (0, 0))
    sds = lambda wd, dt: jax.ShapeDtypeStruct((t, wd), dt)
    return pl.pallas_call(
        body, name="post_bwd", grid=(t // tm,),
        out_shape=[sds(D_MODEL, F32), sds(512, F32), sds(512, F32), sds(D_MODEL, BF16), sds(D_MODEL, BF16), sds(D_MODEL, BF16),
                   sds(D_FF, BF16), sds(D_FF, BF16), sds(D_FF, BF16), sds(D_MODEL, BF16),
                   jax.ShapeDtypeStruct((1, 512), F32), jax.ShapeDtypeStruct((1, D_MODEL), F32), jax.ShapeDtypeStruct((1, D_MODEL), F32)],
        in_specs=[tok(D_MODEL), tok(D_MODEL), tok(D_FF), tok(D_FF), tok(512), tok(512), tok(D_MODEL), _const_spec((1, 512)),
                  _const_spec((D_MODEL, D_MODEL)), _const_spec((1, D_MODEL)), _const_spec((D_FF, D_MODEL)),
                  _const_spec((D_FF, D_MODEL)), _const_spec((D_FF, D_MODEL)), _const_spec((1, D_MODEL))],
        out_specs=[tok(D_MODEL), tok(512), tok(512), tok(D_MODEL), tok(D_MODEL), tok(D_MODEL), tok(D_FF), tok(D_FF), tok(D_FF),
                   tok(D_MODEL), vec(512), vec(D_MODEL), vec(D_MODEL)],
        compiler_params=_params(),
    )(x1, x2, gate_b, up_b, ya, oattn, tgt, g_mla, w_out, g2, w_gate, w_up, w_down, g_fin)


def _matmul_tn(a, b, tn, tt, tag):
    t, k = a.shape
    n = b.shape[1]
    last = t // tt - 1

    def body(a_ref, b_ref, o_ref, acc_ref):
        part = _dot_tn(a_ref[...], b_ref[...])

        @pl.when(pl.program_id(1) == 0)
        def _():
            acc_ref[...] = part

        @pl.when(pl.program_id(1) > 0)
        def _():
            acc_ref[...] += part

        @pl.when(pl.program_id(1) == last)
        def _():
            o_ref[...] = acc_ref[...].astype(o_ref.dtype)

    return pl.pallas_call(
        body, name="wgrad_" + tag, grid=(n // tn, t // tt), out_shape=jax.ShapeDtypeStruct((k, n), BF16),
        in_specs=[pl.BlockSpec((tt, k), lambda j, i: (i, 0)), pl.BlockSpec((tt, tn), lambda j, i: (i, j))],
        out_specs=pl.BlockSpec((k, tn), lambda j, i: (0, j)), scratch_shapes=[pltpu.VMEM((k, tn), F32)],
        compiler_params=_params(),
    )(a, b)


def _mla_qkv_bwd(cq, ckv, g_qa, g_kva, w_q, w_kv, tables, dq, dk, dv, seq, tm):
    t = cq.shape[0]
    nblk = seq // tm

    def body(cq_ref, ckv_ref, gq_ref, gk_ref, wq_ref, wkv_ref, c_ref, sa_ref, sb_ref, dq_ref, dk_ref, dv_ref,
             dcq_ref, dckv_ref, dkr_ref, cqn_ref, dqf_ref, ckn_ref, dkv_ref, dgq_ref, dgk_ref):
        cos_t, sin_a, sin_b = c_ref[...], sa_ref[...], sb_ref[...]
        cqn_ref[...] = _rms(cq_ref[...], gq_ref[...]).astype(BF16)
        ckn_ref[...] = _rms(ckv_ref[...], gk_ref[...]).astype(BF16)
        dkr = jnp.zeros((tm, 128), F32)
        for h in range(B_HEADS):
            lo = h * QK_PAD
            dqf_ref[:, lo:lo + 128] = (dq_ref[:, lo:lo + 128].astype(F32) * ATTN_SCALE).astype(BF16)
            dq_rope = dq_ref[:, lo + 128:lo + 256].astype(F32) * ATTN_SCALE
            dqf_ref[:, lo + 128:lo + 256] = _rope_t(dq_rope, cos_t, sin_a, sin_b).astype(BF16)
            dkv_ref[:, lo:lo + 128] = dk_ref[:, lo:lo + 128].astype(BF16)
            dkv_ref[:, lo + 128:lo + 256] = dv_ref[:, h * B_V:(h + 1) * B_V].astype(BF16)
            dkr = dkr + dk_ref[:, lo + 128:lo + 256]
        dkr_ref[...] = _rope_t(dkr, cos_t, sin_a, sin_b).astype(dkr_ref.dtype)
        dcq, dgq = _rms_bwd(cq_ref[...], gq_ref[...], _dot(dqf_ref[...], wq_ref[...]))
        dckv, dgk = _rms_bwd(ckv_ref[...], gk_ref[...], _dot_nt(dkv_ref[...], wkv_ref[...]))
        dcq_ref[...] = dcq.astype(dcq_ref.dtype)
        dckv_ref[...] = dckv.astype(dckv_ref.dtype)

        @pl.when(pl.program_id(0) == 0)
        def _():
            dgq_ref[...] = jnp.zeros_like(dgq_ref)
            dgk_ref[...] = jnp.zeros_like(dgk_ref)

        dgq_ref[...] += dgq
        dgk_ref[...] += dgk

    tok = lambda wd: pl.BlockSpec((tm, wd), lambda i: (i, 0))
    vec = lambda wd: pl.BlockSpec((1, wd), lambda i: (0, 0))
    tab = pl.BlockSpec((tm, 128), lambda i: (i % nblk, 0))
    sds = lambda wd, dt: jax.ShapeDtypeStruct((t, wd), dt)
    return pl.pallas_call(
        body, name="mla_qkv_bwd", grid=(t // tm,),
        out_shape=[sds(Q_LORA, BF16), sds(KV_LORA, BF16), sds(128, BF16), sds(Q_LORA, BF16), sds(1024, BF16), sds(KV_LORA, BF16),
                   sds(1024, BF16), jax.ShapeDtypeStruct((1, Q_LORA), F32), jax.ShapeDtypeStruct((1, KV_LORA), F32)],
        in_specs=[tok(Q_LORA), tok(KV_LORA), _const_spec((1, Q_LORA)), _const_spec((1, KV_LORA)),
                  _const_spec((1024, Q_LORA)), _const_spec((KV_LORA, 1024)), tab, tab, tab,
                  tok(1024), tok(1024), tok(512)],
        out_specs=[tok(Q_LORA), tok(KV_LORA), tok(128), tok(Q_LORA), tok(1024), tok(KV_LORA), tok(1024),
                   vec(Q_LORA), vec(KV_LORA)],
        compiler_params=_params(),
    )(cq, ckv, g_qa, g_kva, w_q, w_kv, *tables, dq, dk, dv)


def _inproj_bwd(x, g1, w_in, dx1, pieces, tm):
    t = x.shape[0]
    counts = [len(p) for p in pieces]
    flat = [a for p in pieces for a in p]
    widths = [wd for wd, p in zip(IN_WIDTHS, pieces) for _ in p]

    def body(x_ref, g_ref, w_ref, dx1_ref, *refs):
        ins = refs[:len(flat)]
        dx_ref, h_ref, dp_ref, dg_ref = refs[len(flat):]
        xv = x_ref[...]
        h_ref[...] = _rms(xv, g_ref[...]).astype(BF16)
        off, j = 0, 0
        for wd, cnt in zip(IN_WIDTHS, counts):
            acc = ins[j][...].astype(F32)
            for jj in range(1, cnt):
                acc = acc + ins[j + jj][...].astype(F32)
            dp_ref[:, off:off + wd] = acc.astype(BF16)
            off += wd
            j += cnt
        dxn, dg = _rms_bwd(xv, g_ref[...], _dot(dp_ref[...], w_ref[...]))
        dx_ref[...] = dx1_ref[...] + dxn

        @pl.when(pl.program_id(0) == 0)
        def _():
            dg_ref[...] = jnp.zeros_like(dg_ref)

        dg_ref[...] += dg

    tok = lambda wd: pl.BlockSpec((tm, wd), lambda i: (i, 0))
    return pl.pallas_call(
        body, name="inproj_bwd", grid=(t // tm,),
        out_shape=[jax.ShapeDtypeStruct((t, D_MODEL), F32), jax.ShapeDtypeStruct((t, D_MODEL), BF16),
                   jax.ShapeDtypeStruct((t, D_IN_PAD), BF16), jax.ShapeDtypeStruct((1, D_MODEL), F32)],
        in_specs=[tok(D_MODEL), _const_spec((1, D_MODEL)), _const_spec((D_IN_PAD, D_MODEL)), tok(D_MODEL)] + [tok(wd) for wd in widths],
        out_specs=[tok(D_MODEL), tok(D_MODEL), tok(D_IN_PAD), pl.BlockSpec((1, D_MODEL), lambda i: (0, 0))],
        compiler_params=_params(),
    )(x, g1, w_in, dx1, *flat)


def _cols_from_slots(g):
    n, r, cs = g.shape
    return g.transpose(1, 0, 2).reshape(r, n * cs)


def _cols_to_slots(full):
    r, c = full.shape
    return full.reshape(r, N_DEV, c // N_DEV).transpose(1, 0, 2)


def _arrange_w_in_t(w_in_t):
    return jnp.concatenate([w_in_t, jnp.zeros((D_IN_PAD - D_IN, D_MODEL), w_in_t.dtype)], axis=0)


def _arrange_w_q_t(w_q_t):
    q3 = w_q_t.reshape(B_HEADS, B_NOPE + B_ROPE, Q_LORA)
    pad = jnp.zeros((B_HEADS, QK_PAD - B_NOPE - B_ROPE, Q_LORA), w_q_t.dtype)
    return jnp.concatenate([q3, pad], axis=1).reshape(B_HEADS * QK_PAD, Q_LORA)


def _unarrange_w_q_t(d_q_t):
    return d_q_t.reshape(B_HEADS, QK_PAD, Q_LORA)[:, :B_NOPE + B_ROPE].reshape(B_HEADS * (B_NOPE + B_ROPE), Q_LORA)


def _step_core(x, loss_target, small_w, lb_full, early_full, late, seq, group, tiles, distributed):
    g1, g_hgrn, g_qa, g_kva, g_mla, g2, g_fin = small_w
    w_in, w_q, w_kv = _arrange_w_in_t(early_full[0]), _arrange_w_q_t(early_full[1]), early_full[2]
    nb = x.shape[0]
    t = nb * seq
    tm, tm_fwd, tq_f, tq_b, tt = tiles
    xt = x.reshape(t, D_MODEL)
    tgt = loss_target.reshape(t, D_MODEL)
    tables = _rope_tables(seq)

    hq, hi, zf, zb, hg, cq, ckv, kr = _inproj(xt, g1, w_in, tm_fwd)
    qcat, kcat, vv = _mla_qkv(cq, ckv, kr, g_qa, g_kva, w_q, w_kv, tables, seq, tm_fwd)
    if distributed:
        oattn, lse, *late_slots = _attn_fwd(qcat, kcat, vv, nb, seq, tq_f, gather=tuple(late))
    else:
        oattn, lse = _attn_fwd(qcat, kcat, vv, nb, seq, tq_f)
        late_slots = late
    w_out = late_slots[0].reshape(D_MODEL, D_MODEL)
    w_gate, w_up = late_slots[1].reshape(D_FF, D_MODEL), late_slots[2].reshape(D_FF, D_MODEL)
    w_down = late_slots[3].reshape(D_FF, D_MODEL)
    lbl_f, lbl_b = lb_full[0], lb_full[1]
    o_f, o_b, save_f, save_b = _gla_fwd(hq, hi, (zf, zb), (lbl_f, lbl_b), nb, seq, group)
    ya = _gla_combine(o_f, o_b, hg, g_hgrn, tm_fwd)
    x1, x2, gate_b, up_b, loss_row = _post_fwd(xt, ya, oattn, tgt, g_mla, w_out, g2, w_gate, w_up, w_down, g_fin, tm_fwd)

    (dx1, d_ya, d_oattn, ycat_b, dx1_b, h2_b, dgate_b, dup_b, act_b, dx2_b, d_g_mla, d_g2, d_g_fin) = _post_bwd(
        x1, x2, gate_b, up_b, ya, oattn, tgt, g_mla, w_out, g2, w_gate, w_up, w_down, g_fin, tm)
    d_w_gate = _matmul_tn(dgate_b, h2_b, 512, tt, "gate")
    d_w_up = _matmul_tn(dup_b, h2_b, 512, tt, "up")
    d_w_down = _matmul_tn(act_b, dx2_b, 512, tt, "down")
    d_w_out = _matmul_tn(ycat_b, dx1_b, D_MODEL, tt, "out")
    late_g = [d_w_out.reshape(N_DEV, D_MODEL // N_DEV, D_MODEL)] + [
        g.reshape(N_DEV, D_FF // N_DEV, D_MODEL) for g in (d_w_gate, d_w_up, d_w_down)]
    if distributed:
        dq, dk, dv, *late_g = _attn_bwd(qcat, kcat, vv, oattn, lse, d_oattn, nb, seq, tq_b, exchange=tuple(late_g))
    else:
        dq, dk, dv = _attn_bwd(qcat, kcat, vv, oattn, lse, d_oattn, nb, seq, tq_b)
    (d_cq, d_ckv, d_kr, cqn_b, dqf_b, ckn_b, dkv_b, d_g_qa, d_g_kva) = _mla_qkv_bwd(
        cq, ckv, g_qa, g_kva, w_q, w_kv, tables, dq, dk, dv, seq, tm_fwd)
    d_w_q = _matmul_tn(dqf_b, cqn_b, Q_LORA, tt, "q_b")
    d_w_kv = _matmul_tn(ckn_b, dkv_b, B_HEADS * (B_NOPE + B_V), tt, "kv_b")
    d_o, d_hg, d_g_hgrn = _gla_combine_bwd(o_f, o_b, hg, g_hgrn, d_ya, tm_fwd)
    dq_f, dv_f, dz_f, dq_b, dv_b, dz_b, dl_f, dl_b = _gla_bwd(
        hq, hi, (zf, zb), (lbl_f, lbl_b), (save_f, save_b), d_o, nb, seq, group)
    grad_x, h1_b, dproj_b, d_g1 = _inproj_bwd(
        xt, g1, w_in, dx1, [[dq_f, dq_b], [dv_f, dv_b], [dz_f], [dz_b], [d_hg], [d_cq], [d_ckv], [d_kr]], tm)
    d_w_in = _matmul_tn(dproj_b, h1_b, 512, tt, "in")

    early_g = [d_w_in[:D_IN].reshape(N_DEV, D_IN // N_DEV, D_MODEL),
               _unarrange_w_q_t(d_w_q).reshape(N_DEV, 768 // N_DEV, Q_LORA), _cols_to_slots(d_w_kv)]
    d_lb = jnp.stack([jnp.sum(dl_f, axis=0), jnp.sum(dl_b, axis=0)], axis=0)
    small_grads = [d_g1, d_g_hgrn, d_g_qa, d_g_kva, d_g_mla, d_g2, d_g_fin]
    return loss_row, grad_x.reshape(nb, seq, D_MODEL), early_g, late_g, small_grads, d_lb


def kernel(x, norm1_g, w_in, lb_logits, hgrn_norm_g, q_a_norm_g, w_q_b, kv_a_norm_g, w_kv_b, mla_norm_g, w_out, norm2_g, w_gate, w_up, w_down, final_norm_g, loss_target, m_norm1_g, m_w_in, m_lb_logits, m_hgrn_norm_g, m_q_a_norm_g, m_w_q_b, m_kv_a_norm_g, m_w_kv_b, m_mla_norm_g, m_w_out, m_norm2_g, m_w_gate, m_w_up, m_w_down, m_final_norm_g, v_norm1_g, v_w_in, v_lb_logits, v_hgrn_norm_g, v_q_a_norm_g, v_w_q_b, v_kv_a_norm_g, v_w_kv_b, v_mla_norm_g, v_w_out, v_norm2_g, v_w_gate, v_w_up, v_w_down, v_final_norm_g):
    big_w = [w_in, w_q_b, w_kv_b, w_out, w_gate, w_up, w_down]
    big_m = [m_w_in, m_w_q_b, m_w_kv_b, m_w_out, m_w_gate, m_w_up, m_w_down]
    big_v = [v_w_in, v_w_q_b, v_w_kv_b, v_w_out, v_w_gate, v_w_up, v_w_down]
    small_w = [norm1_g, hgrn_norm_g, q_a_norm_g, kv_a_norm_g, mla_norm_g, norm2_g, final_norm_g]
    small_m = [m_norm1_g, m_hgrn_norm_g, m_q_a_norm_g, m_kv_a_norm_g, m_mla_norm_g, m_norm2_g, m_final_norm_g]
    small_v = [v_norm1_g, v_hgrn_norm_g, v_q_a_norm_g, v_kv_a_norm_g, v_mla_norm_g, v_norm2_g, v_final_norm_g]
    seq = x.shape[1]
    my_id = 4 * lax.axis_index("x") + 2 * lax.axis_index("y") + lax.axis_index("c")

    shard = lambda w: w[0].astype(BF16)
    col_t = lambda w: jnp.swapaxes(w, 1, 2)[0]
    shard_t = lambda w: col_t(w).astype(BF16)
    g_in, g_q, g_kv, g_lb = _all_gather_call([shard_t(w_in), shard_t(w_q_b), shard(w_kv_b), lb_logits.reshape(4, 64)])
    early_full = (g_in.reshape(D_IN, D_MODEL), g_q.reshape(768, Q_LORA), _cols_from_slots(g_kv))
    lb_full = g_lb.reshape(N_DEV, 2, 2, 64).transpose(1, 2, 0, 3).reshape(2, 2, 512)

    as_row = lambda a: a.reshape(1, -1)
    loss_row, grad_x, early_g, late_recv, small_g, d_lb = _step_core(
        x, loss_target, [as_row(s) for s in small_w], lb_full, early_full,
        [shard(w_out), shard_t(w_gate), shard_t(w_up), shard(w_down)], seq, min(16, seq // CHUNK),
        (256, 512, min(1024, seq), min(1024, seq), min(2048, 2 * seq)), True)

    grads, deltas, new_ms, new_vs = {}, {}, {}, {}
    views = {name: (col_t if name in ("w_in", "w_q_b", "w_gate", "w_up") else (lambda a: a[0])) for name, _, _, _ in BIG}
    backs = {name: ((lambda a: jnp.swapaxes(a[None], 1, 2)) if name in ("w_in", "w_q_b", "w_gate", "w_up") else (lambda a: a[None]))
             for name, _, _, _ in BIG}
    by_name = {name: (w, m, v) for (name, _, _, _), w, m, v in zip(BIG, big_w, big_m, big_v)}
    late_names = ["w_out", "w_gate", "w_up", "w_down"]
    n_small = len(small_g)
    g_l, d_l, nm_l, nv_l, recv = _adamw_recv_hosting(
        [views[n](by_name[n][0]) for n in late_names], list(late_recv), [views[n](by_name[n][1]) for n in late_names],
        [views[n](by_name[n][2]) for n in late_names],
        early_g + small_g + [d_lb.reshape(4, 512), loss_row], [True] * 3 + [False] * (n_small + 2))
    for i, name in enumerate(late_names):
        grads[name], deltas[name], new_ms[name], new_vs[name] = (backs[name](a[i]) for a in (g_l, d_l, nm_l, nv_l))
    sums = _sum_slots_call(recv[3:])
    g_small = [g.reshape(s.shape) for g, s in zip(sums[:n_small], small_w)]
    g_lb_own = lax.dynamic_index_in_dim(sums[n_small].reshape(2, 2, N_DEV, 64), my_id, axis=2, keepdims=False)
    loss = sums[n_small + 1][0, 0]

    for name, r in zip(["w_in", "w_q_b", "w_kv_b"], recv[:3]):
        w, m, v = by_name[name]
        g, d, nm, nv = _adamw_recv(views[name](w), r, views[name](m), views[name](v), name)
        grads[name], deltas[name], new_ms[name], new_vs[name] = (backs[name](a) for a in (g, d, nm, nv))
    lb_rows = lambda a: a.reshape(4, 64)
    d_s, nm_s, nv_s = _adamw_small(
        [as_row(a) for a in small_w] + [lb_rows(lb_logits)], [as_row(a) for a in g_small] + [lb_rows(g_lb_own)],
        [as_row(a) for a in small_m] + [lb_rows(m_lb_logits)], [as_row(a) for a in small_v] + [lb_rows(v_lb_logits)])
    for i, (s, (name, _)) in enumerate(zip(small_w + [lb_logits], SMALL + (("lb_logits", 0),))):
        grads[name] = (g_small + [g_lb_own])[i]
        deltas[name], new_ms[name], new_vs[name] = d_s[i].reshape(s.shape), nm_s[i].reshape(s.shape), nv_s[i].reshape(s.shape)

    order = ["norm1_g", "w_in", "lb_logits", "hgrn_norm_g", "q_a_norm_g", "w_q_b", "kv_a_norm_g", "w_kv_b", "mla_norm_g",
             "w_out", "norm2_g", "w_gate", "w_up", "w_down", "final_norm_g"]
    return (loss, grad_x, *[grads[n] for n in order], *[deltas[n] for n in order],
            *[new_ms[n] for n in order], *[new_vs[n] for n in order])
```

```python
import functools
import math

import jax
import jax.numpy as jnp
from jax import lax
from jax.experimental import pallas as pl
from jax.experimental.pallas import tpu as pltpu

F32 = jnp.float32
BF16 = jnp.bfloat16

N_DEV = 8
D_MODEL = 1024
D_FF = 2816
A_WIDTH = 512
HEAD_PAIR = 128
CHUNK = 64
B_HEADS = 4
B_NOPE = 128
B_ROPE = 64
B_V = 128
QK_PAD = 256
Q_LORA = 384
KV_LORA = 256
D_IN = 3264
D_IN_PAD = 3328
IN_WIDTHS = (512, 512, 512, 512, 512, Q_LORA, KV_LORA, 128)
ROPE_THETA = 10000.0
EPS = 1e-6
ATTN_SCALE = (B_NOPE + B_ROPE) ** -0.5
ATTN_SUB = 256
ATTN_SUB_BWD = 256
ROW_SUB = 256
ADAM_LR, ADAM_B1, ADAM_B2, ADAM_EPS, ADAM_WD, ADAM_STEP = 0.001, 0.9, 0.999, 1e-08, 0.01, 10
VMEM_LIMIT = 60 * 1024 * 1024
MESH = pl.DeviceIdType.MESH

BIG = (("w_in", 1024, D_IN, 1), ("w_q_b", Q_LORA, 768, 1), ("w_kv_b", KV_LORA, 1024, 1), ("w_out", 1024, 1024, 0),
       ("w_gate", 1024, D_FF, 1), ("w_up", 1024, D_FF, 1), ("w_down", D_FF, 1024, 0))
SMALL = (("norm1_g", 1024), ("hgrn_norm_g", 512), ("q_a_norm_g", 384), ("kv_a_norm_g", 256), ("mla_norm_g", 512),
         ("norm2_g", 1024), ("final_norm_g", 1024))


def _params(**kw):
    return pltpu.CompilerParams(vmem_limit_bytes=VMEM_LIMIT, **kw)


def _const_spec(shape):
    return pl.BlockSpec(shape, lambda *_: (0,) * len(shape), pipeline_mode=pl.Buffered(1))


def _dot(a, b):
    return jnp.dot(a, b, preferred_element_type=F32)


def _dot_nt(a, b):
    return lax.dot_general(a, b, (((1,), (1,)), ((), ())), preferred_element_type=F32)


def _dot_tn(a, b):
    return lax.dot_general(a, b, (((0,), (0,)), ((), ())), preferred_element_type=F32)


@jax.custom_vjp
def _mm(a, b):
    return _dot(a.astype(BF16), b.astype(BF16))


def _mm_fwd(a, b):
    return _mm(a, b), (a, b)


def _mm_bwd(res, g):
    a, b = res
    gb = g.astype(BF16)
    return _dot_nt(gb, b.astype(BF16)), _dot_tn(a.astype(BF16), gb)


_mm.defvjp(_mm_fwd, _mm_bwd)


@jax.custom_vjp
def _mm_nt(a, b):
    return _dot_nt(a.astype(BF16), b.astype(BF16))


def _mm_nt_fwd(a, b):
    return _mm_nt(a, b), (a, b)


def _mm_nt_bwd(res, g):
    a, b = res
    gb = g.astype(BF16)
    return _dot(gb, b.astype(BF16)), _dot_tn(gb, a.astype(BF16))


_mm_nt.defvjp(_mm_nt_fwd, _mm_nt_bwd)


@jax.custom_vjp
def _mm_tn(a, b):
    return _dot_tn(a.astype(BF16), b.astype(BF16))


def _mm_tn_fwd(a, b):
    return _mm_tn(a, b), (a, b)


def _mm_tn_bwd(res, g):
    a, b = res
    gb = g.astype(BF16)
    return _dot_nt(b.astype(BF16), gb), _dot(a.astype(BF16), gb)


_mm_tn.defvjp(_mm_tn_fwd, _mm_tn_bwd)


def _dot_exact_rhs(a, m):
    hi = a.astype(BF16)
    lo = (a - hi.astype(F32)).astype(BF16)
    return _dot(hi, m) + _dot(lo, m)


@jax.custom_vjp
def _group_mean(a, m):
    return _dot_exact_rhs(a, m)


def _group_mean_fwd(a, m):
    return _group_mean(a, m), m


def _group_mean_bwd(m, g):
    return _dot_exact_rhs(g, m), jnp.zeros_like(m)


_group_mean.defvjp(_group_mean_fwd, _group_mean_bwd)


def _roll_rows(a, shift):
    return pltpu.roll(a, shift, 0)


def _cumsum_rows_raw(a, reverse):
    n = a.shape[0]
    row = lax.broadcasted_iota(jnp.int32, a.shape, 0)
    s = 1
    while s < n:
        if reverse:
            a = a + jnp.where(row < n - s, _roll_rows(a, n - s), 0.0)
        else:
            a = a + jnp.where(row >= s, _roll_rows(a, s), 0.0)
        s *= 2
    return a


@functools.partial(jax.custom_vjp, nondiff_argnums=(1,))
def _cumsum_rows(a, reverse):
    return _cumsum_rows_raw(a, reverse)


def _cumsum_rows_fwd(a, reverse):
    return _cumsum_rows_raw(a, reverse), None


def _cumsum_rows_bwd(reverse, _, g):
    return (_cumsum_rows_raw(g, not reverse),)


_cumsum_rows.defvjp(_cumsum_rows_fwd, _cumsum_rows_bwd)


def _rms(x, g):
    r = lax.rsqrt(jnp.mean(x * x, axis=-1, keepdims=True) + EPS)
    return x * r * g


def _rms_bwd(x, g, dy):
    r = lax.rsqrt(jnp.mean(x * x, axis=-1, keepdims=True) + EPS)
    xh = x * r
    dg = jnp.sum(dy * xh, axis=0, keepdims=True)
    dxh = dy * g
    dx = r * (dxh - xh * jnp.mean(dxh * xh, axis=-1, keepdims=True))
    return dx, dg


def _sigmoid(a):
    return jax.nn.sigmoid(a)


def _mesh_place():
    x, y, c = lax.axis_index("x"), lax.axis_index("y"), lax.axis_index("c")
    return x, y, c


def _dev_index(p):
    return 4 * p[0] + 2 * p[1] + p[2]


def _comm_sems(n):
    return [pltpu.SemaphoreType.DMA((n, 7)), pltpu.SemaphoreType.DMA((n, 7)), pltpu.SemaphoreType.DMA((n,))]


def _gather_protocol(ins, outs, send_sems, recv_sems, local_sems):
    n = len(ins)
    x, y, c = _mesh_place()
    me, sibling = (x, y, c), (x, y, 1 - c)
    chips = [(1 - x, y), (x, 1 - y), (1 - x, 1 - y)]

    def copy(a, k, block, to, src=None):
        slot = outs[a].at[_dev_index(block)]
        return pltpu.make_async_remote_copy(
            src_ref=slot if src is None else src, dst_ref=slot,
            send_sem=send_sems.at[a, k], recv_sem=recv_sems.at[a, k], device_id=to, device_id_type=MESH)

    def mine(a):
        return pltpu.make_async_copy(ins[a], outs[a].at[_dev_index(me)], local_sems.at[a])

    def first(a):
        return [copy(a, 0, me, sibling, src=ins[a])] + [copy(a, 1 + j, me, (*chip, c), src=ins[a]) for j, chip in enumerate(chips)]

    def start():
        for a in range(n):
            mine(a).start()
            for cp in first(a):
                cp.start()

    def forward():
        for a in range(n):
            for j, chip in enumerate(chips):
                copy(a, 1 + j, (*chip, c), me).wait_recv()
                copy(a, 4 + j, (*chip, c), sibling).start()

    def finish():
        for a in range(n):
            copy(a, 0, sibling, me).wait_recv()
            for j, chip in enumerate(chips):
                copy(a, 4 + j, (*chip, 1 - c), me).wait_recv()
        for a in range(n):
            mine(a).wait()
            for cp in first(a):
                cp.wait_send()
            for j, chip in enumerate(chips):
                copy(a, 4 + j, (*chip, c), sibling).wait_send()

    return start, forward, finish


def _exchange_protocol(ins, outs, scatter, send_sems, recv_sems, local_sems):
    n = len(ins)
    x, y, c = _mesh_place()
    me = (x, y, c)
    my_id = _dev_index(me)
    rels = [(dx, dy, dc) for dx in (0, 1) for dy in (0, 1) for dc in (0, 1)][1:]

    def peer_of(rel):
        return tuple(1 - v if d else v for v, d in zip(me, rel))

    def src(a, dev):
        return ins[a].at[dev] if scatter[a] else ins[a]

    def send(a, k):
        peer = peer_of(rels[k])
        return pltpu.make_async_remote_copy(
            src_ref=src(a, _dev_index(peer)), dst_ref=outs[a].at[my_id],
            send_sem=send_sems.at[a, k], recv_sem=recv_sems.at[a, k], device_id=peer, device_id_type=MESH)

    def arrival(a, k):
        peer = peer_of(rels[k])
        return pltpu.make_async_remote_copy(
            src_ref=src(a, my_id), dst_ref=outs[a].at[_dev_index(peer)],
            send_sem=send_sems.at[a, k], recv_sem=recv_sems.at[a, k], device_id=peer, device_id_type=MESH)

    def own(a):
        return pltpu.make_async_copy(src(a, my_id), outs[a].at[my_id], local_sems.at[a])

    def start():
        for a in range(n):
            own(a).start()
            for k in range(7):
                send(a, k).start()

    def finish():
        for a in range(n):
            for k in range(7):
                arrival(a, k).wait_recv()
        for a in range(n):
            for k in range(7):
                send(a, k).wait_send()
            own(a).wait()

    return start, finish


def _slot_shapes(blocks, scatter=None):
    return [jax.ShapeDtypeStruct(b.shape if (scatter and scatter[a]) else (N_DEV,) + b.shape, b.dtype) for a, b in enumerate(blocks)]


def _all_gather_call(blocks):
    n = len(blocks)

    def body(*refs):
        start, forward, finish = _gather_protocol(refs[:n], refs[n:2 * n], *refs[2 * n:])
        start()
        forward()
        finish()

    any_spec = pl.BlockSpec(memory_space=pl.ANY)
    return pl.pallas_call(
        body, name="weights_all_gather", out_shape=_slot_shapes(blocks),
        in_specs=[any_spec] * n, out_specs=[any_spec] * n, scratch_shapes=_comm_sems(n),
    )(*blocks)


def _sum_slots_call(recvs):
    n = len(recvs)

    def body(*refs):
        for in_ref, out_ref in zip(refs[:n], refs[n:]):
            acc = in_ref[0]
            for j in range(1, N_DEV):
                acc = acc + in_ref[j]
            out_ref[...] = acc

    return pl.pallas_call(
        body, name="small_grad_sum", out_shape=[jax.ShapeDtypeStruct(r.shape[1:], F32) for r in recvs],
        compiler_params=_params(),
    )(*recvs)


def _adam_update(w, g, m, v):
    nm = ADAM_B1 * m + (1.0 - ADAM_B1) * g
    nv = ADAM_B2 * v + (1.0 - ADAM_B2) * (g * g)
    bc1 = 1.0 - ADAM_B1 ** ADAM_STEP
    bc2 = 1.0 - ADAM_B2 ** ADAM_STEP
    return -ADAM_LR * ((nm / bc1) / (jnp.sqrt(nv / bc2) + ADAM_EPS) + ADAM_WD * w), nm, nv


def _adamw_recv(w, recv, m, v, tag):
    r, c = w.shape
    tr = r
    for cand in (512, 256, 128):
        if r > cand and r % cand == 0:
            tr = cand
            break

    def body(w_ref, r_ref, m_ref, v_ref, g_ref, d_ref, nm_ref, nv_ref):
        g = r_ref[0].astype(F32)
        for j in range(1, N_DEV):
            g = g + r_ref[j].astype(F32)
        g_ref[...] = g
        d_ref[...], nm_ref[...], nv_ref[...] = _adam_update(w_ref[...], g, m_ref[...], v_ref[...])

    spec = pl.BlockSpec((tr, c), lambda i: (i, 0))
    return pl.pallas_call(
        body, name="adamw_" + tag, out_shape=[jax.ShapeDtypeStruct(w.shape, F32)] * 4, grid=(r // tr,),
        in_specs=[spec, pl.BlockSpec((N_DEV, tr, c), lambda i: (0, i, 0)), spec, spec], out_specs=[spec] * 4,
        compiler_params=_params(),
    )(w, recv, m, v)


def _adamw_recv_halves(w, recv_halves, m, v, tag):
    r, c = w.shape
    half = c // 2

    def body(w_ref, ra_ref, rb_ref, m_ref, v_ref, g_ref, d_ref, nm_ref, nv_ref):
        def update(r_ref):
            g = r_ref[0].astype(F32)
            for j in range(1, N_DEV):
                g = g + r_ref[j].astype(F32)
            g_ref[...] = g
            d_ref[...], nm_ref[...], nv_ref[...] = _adam_update(w_ref[...], g, m_ref[...], v_ref[...])

        pl.when(pl.program_id(0) == 0)(lambda: update(ra_ref))
        pl.when(pl.program_id(0) == 1)(lambda: update(rb_ref))

    spec = pl.BlockSpec((r, half), lambda j: (0, j))
    whole = pl.BlockSpec((N_DEV, r, half), lambda j: (0, 0, 0))
    return pl.pallas_call(
        body, name="adamw_" + tag, out_shape=[jax.ShapeDtypeStruct(w.shape, F32)] * 4, grid=(2,),
        in_specs=[spec, whole, whole, spec, spec], out_specs=[spec] * 4, compiler_params=_params(),
    )(w, *recv_halves, m, v)


def _adamw_recv_hosting(ws, recvs, ms, vs, blocks, scatter):
    n, ne = len(ws), len(blocks)

    def body(*refs):
        ins, ex_in = refs[:4 * n], refs[4 * n:4 * n + ne]
        outs, ex_out = refs[4 * n + ne:8 * n + ne], refs[8 * n + ne:8 * n + 2 * ne]
        start, finish = _exchange_protocol(ex_in, ex_out, scatter, *refs[8 * n + 2 * ne:])
        start()
        for a in range(n):
            r_ref = ins[n + a]
            g = r_ref[0].astype(F32)
            for j in range(1, N_DEV):
                g = g + r_ref[j].astype(F32)
            outs[a][...] = g
            outs[n + a][...], outs[2 * n + a][...], outs[3 * n + a][...] = _adam_update(
                ins[a][...], g, ins[2 * n + a][...], ins[3 * n + a][...])
        finish()

    vmem, any_spec = pl.BlockSpec(memory_space=pltpu.VMEM), pl.BlockSpec(memory_space=pl.ANY)
    out = pl.pallas_call(
        body, name="adamw_late_and_grad_exchange",
        out_shape=[jax.ShapeDtypeStruct(w.shape, F32) for w in ws] * 4 + _slot_shapes(blocks, scatter),
        in_specs=[vmem] * (4 * n) + [any_spec] * ne, out_specs=[vmem] * (4 * n) + [any_spec] * ne,
        scratch_shapes=_comm_sems(ne), compiler_params=_params(),
    )(*ws, *recvs, *ms, *vs, *blocks)
    return out[:n], out[n:2 * n], out[2 * n:3 * n], out[3 * n:4 * n], out[4 * n:]


def _adamw_small(ws, gs, ms, vs):
    n = len(ws)

    def body(*refs):
        ins, outs = refs[:4 * n], refs[4 * n:]
        for a in range(n):
            d, nm, nv = _adam_update(ins[a][...], ins[n + a][...], ins[2 * n + a][...], ins[3 * n + a][...])
            outs[a][...], outs[n + a][...], outs[2 * n + a][...] = d, nm, nv

    out = pl.pallas_call(
        body, name="adamw_small", out_shape=[jax.ShapeDtypeStruct(w.shape, F32) for w in ws] * 3, compiler_params=_params(),
    )(*ws, *gs, *ms, *vs)
    return out[:n], out[n:2 * n], out[2 * n:]


def _tile(t, want):
    return want if t % want == 0 else t


def _inproj(x, g1, w_in, tm):
    t = x.shape[0]

    def body(x_ref, g_ref, w_ref, *outs):
        for j in range(tm // min(tm, ROW_SUB)):
            r = pl.ds(j * min(tm, ROW_SUB), min(tm, ROW_SUB))
            h = _rms(x_ref[r, :], g_ref[...]).astype(BF16)
            off = 0
            for o_ref, wd in zip(outs, IN_WIDTHS):
                o_ref[r, :] = _dot_nt(h, w_ref[off:off + wd, :])
                off += wd

    return pl.pallas_call(
        body, name="inproj_fwd", grid=(t // tm,),
        out_shape=[jax.ShapeDtypeStruct((t, wd), F32) for wd in IN_WIDTHS],
        in_specs=[pl.BlockSpec((tm, D_MODEL), lambda i: (i, 0)), _const_spec((1, D_MODEL)), _const_spec((D_IN_PAD, D_MODEL))],
        out_specs=[pl.BlockSpec((tm, wd), lambda i: (i, 0)) for wd in IN_WIDTHS],
        compiler_params=_params(),
    )(x, g1, w_in)


def _rope_tables(seq):
    inv = 1.0 / (ROPE_THETA ** (jnp.arange(0, B_ROPE, 2, dtype=F32) / B_ROPE))
    ang = jnp.arange(seq, dtype=F32)[:, None] * inv[None, :]
    cos, sin = jnp.cos(ang), jnp.sin(ang)
    z32, z64 = jnp.zeros_like(cos), jnp.zeros((seq, 64), F32)
    cos_t = jnp.concatenate([cos, cos, z64], axis=1)
    sin_a = jnp.concatenate([-sin, z32, z64], axis=1)
    sin_b = jnp.concatenate([z32, sin, z64], axis=1)
    return cos_t, sin_a, sin_b


def _rope(t, cos_t, sin_a, sin_b):
    return t * cos_t + pltpu.roll(t, 96, 1) * sin_a + pltpu.roll(t, 32, 1) * sin_b


def _rope_t(d, cos_t, sin_a, sin_b):
    return d * cos_t + pltpu.roll(d * sin_a, 32, 1) + pltpu.roll(d * sin_b, 96, 1)


def _mla_qkv(cq, ckv, kr, g_qa, g_kva, w_q, w_kv, tables, seq, tm):
    t = cq.shape[0]
    nblk = seq // tm

    def body(cq_ref, ckv_ref, kr_ref, gq_ref, gk_ref, wq_ref, wkv_ref, c_ref, sa_ref, sb_ref, q_out, k_out, v_out):
        cos_t, sin_a, sin_b = c_ref[...], sa_ref[...], sb_ref[...]
        cqn = _rms(cq_ref[...], gq_ref[...]).astype(BF16)
        ckn = _rms(ckv_ref[...], gk_ref[...]).astype(BF16)
        kr_rot = _rope(kr_ref[...], cos_t, sin_a, sin_b).astype(BF16)
        for h in range(B_HEADS):
            lo = h * QK_PAD
            q_out[:, lo:lo + 128] = (_dot_nt(cqn, wq_ref[lo:lo + 128, :]) * ATTN_SCALE).astype(BF16)
            qr = _rope(_dot_nt(cqn, wq_ref[lo + 128:lo + 256, :]), cos_t, sin_a, sin_b)
            q_out[:, lo + 128:lo + 256] = (qr * ATTN_SCALE).astype(BF16)
            k_out[:, lo:lo + 128] = _dot(ckn, wkv_ref[:, lo:lo + 128]).astype(BF16)
            k_out[:, lo + 128:lo + 256] = kr_rot
            v_out[:, h * B_V:(h + 1) * B_V] = _dot(ckn, wkv_ref[:, lo + 128:lo + 256]).astype(BF16)

    tok = lambda wd: pl.BlockSpec((tm, wd), lambda i: (i, 0))
    tab = pl.BlockSpec((tm, 128), lambda i: (i % nblk, 0))
    return pl.pallas_call(
        body, name="mla_qkv_fwd", grid=(t // tm,),
        out_shape=[jax.ShapeDtypeStruct((t, B_HEADS * QK_PAD), BF16), jax.ShapeDtypeStruct((t, B_HEADS * QK_PAD), BF16),
                   jax.ShapeDtypeStruct((t, B_HEADS * B_V), BF16)],
        in_specs=[tok(Q_LORA), tok(KV_LORA), tok(128), _const_spec((1, Q_LORA)), _const_spec((1, KV_LORA)),
                  _const_spec((B_HEADS * QK_PAD, Q_LORA)), _const_spec((KV_LORA, 1024)), tab, tab, tab],
        out_specs=[tok(B_HEADS * QK_PAD), tok(B_HEADS * QK_PAD), tok(B_HEADS * B_V)],
        compiler_params=_params(),
    )(cq, ckv, kr, g_qa, g_kva, w_q, w_kv, *tables)


def _step_index(nq):
    return (pl.program_id(0) * B_HEADS + pl.program_id(1)) * nq + pl.program_id(2)


def _attn_fwd(qcat, kcat, v, nb, seq, tq, gather=()):
    t = qcat.shape[0]
    nq = seq // tq
    ng = len(gather)
    steps = nb * B_HEADS * nq

    def body(q_ref, k_ref, v_ref, *rest):
        o_ref, lse_ref = rest[ng:ng + 2]
        if ng:
            start, forward, finish = _gather_protocol(rest[:ng], rest[ng + 2:2 * ng + 2], *rest[2 * ng + 2:])
            pl.when(_step_index(nq) == 0)(start)
            pl.when(_step_index(nq) == (3 * steps) // 4)(forward)
        for j in range(tq // ATTN_SUB):
            r = pl.ds(j * ATTN_SUB, ATTN_SUB)
            s = _dot_nt(q_ref[r, :], k_ref[...])
            m = jnp.max(s, axis=-1, keepdims=True)
            p = jnp.exp(s - m)
            l = jnp.sum(p, axis=-1, keepdims=True)
            o_ref[r, :] = _dot(p.astype(BF16), v_ref[...]) / l
            lse_ref[0, r, :] = m + jnp.log(l)
        if ng:
            pl.when(_step_index(nq) == steps - 1)(finish)

    any_spec = pl.BlockSpec(memory_space=pl.ANY)
    return pl.pallas_call(
        body, name="attn_fwd", grid=(nb, B_HEADS, nq),
        out_shape=[jax.ShapeDtypeStruct((t, B_HEADS * B_V), F32), jax.ShapeDtypeStruct((B_HEADS, t, 1), F32)] + _slot_shapes(gather),
        in_specs=[pl.BlockSpec((tq, QK_PAD), lambda b, h, i: (b * nq + i, h)),
                  pl.BlockSpec((seq, QK_PAD), lambda b, h, i: (b, h)),
                  pl.BlockSpec((seq, B_V), lambda b, h, i: (b, h))] + [any_spec] * ng,
        out_specs=[pl.BlockSpec((tq, B_V), lambda b, h, i: (b * nq + i, h)),
                   pl.BlockSpec((1, tq, 1), lambda b, h, i: (h, b * nq + i, 0))] + [any_spec] * ng,
        scratch_shapes=_comm_sems(ng) if ng else [],
        compiler_params=_params(),
    )(qcat, kcat, v, *gather)


def _attn_bwd(qcat, kcat, v, o, lse, do, nb, seq, tq, exchange=()):
    t = qcat.shape[0]
    nq = seq // tq
    ne = len(exchange)
    steps = nb * B_HEADS * nq

    def body(q_ref, k_ref, v_ref, o_ref, lse_ref, do_ref, *rest):
        dq_ref, dk_ref, dv_ref = rest[ne:ne + 3]
        if ne:
            start, finish = _exchange_protocol(rest[:ne], rest[ne + 3:2 * ne + 3], [True] * ne, *rest[2 * ne + 3:])
            pl.when(_step_index(nq) == 0)(start)

        @pl.when(pl.program_id(2) == 0)
        def _():
            dv_ref[...] = jnp.zeros_like(dv_ref)
            dk_ref[...] = jnp.zeros_like(dk_ref)

        for j in range(tq // ATTN_SUB_BWD):
            r = pl.ds(j * ATTN_SUB_BWD, ATTN_SUB_BWD)
            q, k = q_ref[r, :], k_ref[...]
            do_f = do_ref[r, :]
            delta = jnp.sum(do_f * o_ref[r, :], axis=-1, keepdims=True)
            dob = do_f.astype(BF16)
            p = jnp.exp(_dot_nt(q, k) - lse_ref[0, r, :])
            ds = (p * (_dot_nt(dob, v_ref[...]) - delta)).astype(BF16)
            dq_ref[r, :] = _dot(ds, k).astype(dq_ref.dtype)
            dv_ref[...] += _dot_tn(p.astype(BF16), dob)
            dk_ref[...] += _dot_tn(ds, q)
        if ne:
            pl.when(_step_index(nq) == steps - 1)(finish)

    qspec = lambda wd: pl.BlockSpec((tq, wd), lambda b, h, i: (b * nq + i, h))
    kspec = lambda wd: pl.BlockSpec((seq, wd), lambda b, h, i: (b, h))
    any_spec = pl.BlockSpec(memory_space=pl.ANY)
    return pl.pallas_call(
        body, name="attn_bwd", grid=(nb, B_HEADS, nq),
        out_shape=[jax.ShapeDtypeStruct((t, B_HEADS * QK_PAD), BF16), jax.ShapeDtypeStruct((t, B_HEADS * QK_PAD), F32),
                   jax.ShapeDtypeStruct((t, B_HEADS * B_V), F32)] + _slot_shapes(exchange, [True] * ne),
        in_specs=[qspec(QK_PAD), kspec(QK_PAD), kspec(B_V), qspec(B_V),
                  pl.BlockSpec((1, tq, 1), lambda b, h, i: (h, b * nq + i, 0)), qspec(B_V)] + [any_spec] * ne,
        out_specs=[qspec(QK_PAD), kspec(QK_PAD), kspec(B_V)] + [any_spec] * ne,
        scratch_shapes=_comm_sems(ne) if ne else [],
        compiler_params=_params(),
    )(qcat, kcat, v, o, lse, do, *exchange)


def _gla_consts(reverse):
    row = lax.broadcasted_iota(jnp.int32, (CHUNK, CHUNK), 0)
    col = lax.broadcasted_iota(jnp.int32, (CHUNK, CHUNK), 1)
    causal = (row <= col) if reverse else (row >= col)
    lane = lax.broadcasted_iota(jnp.int32, (1, HEAD_PAIR), 1)
    m0 = (lane < 64).astype(F32)
    m1 = 1.0 - m0
    r2 = lax.broadcasted_iota(jnp.int32, (HEAD_PAIR, HEAD_PAIR), 0)
    c2 = lax.broadcasted_iota(jnp.int32, (HEAD_PAIR, HEAD_PAIR), 1)
    same_head = ((r2 < 64) == (c2 < 64)).astype(F32)
    return causal, m0, m1, same_head


def _gla_chunk(hq, hi, z, l0, l1, st, consts, reverse):
    q_dec, k_inv, k_end, decay = _gla_gates(hq, z, l0, l1, reverse)
    o, st_new = _gla_state(q_dec, st, decay, _gla_increment(hi, k_end, consts))
    return o + _gla_intra(q_dec, k_inv, hi, consts), st_new


def _gla_gates(hq, z, l0, l1, reverse):
    mx = jnp.maximum(l0, l1)
    e0, e1 = jnp.exp(l0 - mx), jnp.exp(l1 - mx)
    lb = e0 / (e0 + e1)
    q = hq * _sigmoid(hq)
    sz = _sigmoid(z)
    log_f = jnp.log(lb + (1.0 - lb) * sz)
    k = (1.0 - lb) * (1.0 - sz)
    cum = _cumsum_rows(log_f, reverse)
    decay = jnp.exp(jnp.sum(log_f, axis=0, keepdims=True))
    k_inv = k * jnp.exp(-cum)
    return q * jnp.exp(cum), k_inv, k_inv * decay, decay


def _gla_intra(q_dec, k_inv, hi, consts):
    causal, m0, m1, _ = consts
    o = None
    for mh in (m0, m1):
        s = jnp.where(causal, _mm_nt(q_dec * mh, k_inv), 0.0)
        part = _mm(s, hi) * mh
        o = part if o is None else o + part
    return o


def _gla_increment(hi, k_end, consts):
    return _mm_tn(hi, k_end) * consts[3]


def _gla_state(q_dec, st, decay, inc):
    return _mm_nt(q_dec, st), st * decay + inc


GLA_DIRS = (False, True)
GLA_BATCH_FWD = 8
GLA_BATCH_BWD = 4


def _gla_fwd(hq, hi, zs, lbls, nb, seq, group):
    t = hq.shape[0]
    rows = group * CHUNK
    nblk = seq // rows
    n_chunks = seq // CHUNK
    nd = len(GLA_DIRS)

    def body(*refs):
        ins, outs, st_refs = refs[:4 * nd], refs[4 * nd:6 * nd], refs[6 * nd:]
        @pl.when(pl.program_id(2) == 0)
        def _():
            for st_ref in st_refs:
                st_ref[...] = jnp.zeros_like(st_ref)

        consts = [_gla_consts(rev) for rev in GLA_DIRS]
        work = [(d, rev, group - 1 - cc if rev else cc) for cc in range(group) for d, rev in enumerate(GLA_DIRS)]
        rows_of = lambda c: pl.ds(c * CHUNK, CHUNK)
        sts = [st_ref[...] for st_ref in st_refs]
        for w0 in range(0, len(work), GLA_BATCH_FWD):
            batch = work[w0:w0 + GLA_BATCH_FWD]
            gates, intra, incs = {}, {}, {}
            for d, rev, c in batch:
                hq_ref, _, z_ref, lbl_ref = ins[4 * d:4 * d + 4]
                gates[d, c] = _gla_gates(hq_ref[rows_of(c), :], z_ref[rows_of(c), :], lbl_ref[0:1, :], lbl_ref[1:2, :], rev)
            for d, rev, c in batch:
                hi_c = ins[4 * d + 1][rows_of(c), :]
                intra[d, c] = _gla_intra(gates[d, c][0], gates[d, c][1], hi_c, consts[d])
                incs[d, c] = _gla_increment(hi_c, gates[d, c][2], consts[d])
            for d, rev, c in batch:
                outs[nd + d][0, 0, c] = sts[d]
                o_state, sts[d] = _gla_state(gates[d, c][0], sts[d], gates[d, c][3], incs[d, c])
                outs[d][rows_of(c), :] = intra[d, c] + o_state
        for st_ref, st in zip(st_refs, sts):
            st_ref[...] = st

    def tb(rev):
        return (lambda i: nblk - 1 - i) if rev else (lambda i: i)

    tok = lambda rev: pl.BlockSpec((rows, HEAD_PAIR), lambda b, p, i: (b * nblk + tb(rev)(i), p))
    lspec = pl.BlockSpec((2, HEAD_PAIR), lambda b, p, i: (0, p))
    sspec = lambda rev: pl.BlockSpec((1, 1, group, HEAD_PAIR, HEAD_PAIR), lambda b, p, i: (b, p, tb(rev)(i), 0, 0))
    args, in_specs = [], []
    for d, rev in enumerate(GLA_DIRS):
        args += [hq, hi, zs[d], lbls[d]]
        in_specs += [tok(rev), tok(rev), tok(rev), lspec]
    return pl.pallas_call(
        body, name="gla_fwd", grid=(nb, 4, nblk),
        out_shape=[jax.ShapeDtypeStruct((t, A_WIDTH), F32)] * nd
        + [jax.ShapeDtypeStruct((nb, 4, n_chunks, HEAD_PAIR, HEAD_PAIR), F32)] * nd,
        in_specs=in_specs, out_specs=[tok(rev) for rev in GLA_DIRS] + [sspec(rev) for rev in GLA_DIRS],
        scratch_shapes=[pltpu.VMEM((HEAD_PAIR, HEAD_PAIR), F32)] * nd,
        compiler_params=_params(),
    )(*args)


def _gla_bwd(hq, hi, zs, lbls, saved, do, nb, seq, group):
    t = hq.shape[0]
    rows = group * CHUNK
    nblk = seq // rows
    nd = len(GLA_DIRS)

    def body(*refs):
        ins, outs, dst_refs = refs[:6 * nd], refs[6 * nd:10 * nd], refs[10 * nd:]
        dl_refs = outs[3 * nd:]

        @pl.when(pl.program_id(2) == 0)
        def _():
            for dst_ref, dl_ref in zip(dst_refs, dl_refs):
                dst_ref[...] = jnp.zeros_like(dst_ref)
                dl_ref[...] = jnp.zeros_like(dl_ref)

        consts = [_gla_consts(rev) for rev in GLA_DIRS]
        dsts = [dst_ref[...] for dst_ref in dst_refs]
        dls = [[jnp.zeros((1, HEAD_PAIR), F32), jnp.zeros((1, HEAD_PAIR), F32)] for _ in GLA_DIRS]
        work = [(d, rev, cc if rev else group - 1 - cc) for cc in range(group) for d, rev in enumerate(GLA_DIRS)]
        for w0 in range(0, len(work), GLA_BATCH_BWD):
            vjps = {}
            for d, rev, c in work[w0:w0 + GLA_BATCH_BWD]:
                hq_ref, hi_ref, z_ref, lbl_ref, save_ref, _ = ins[6 * d:6 * d + 6]
                r = pl.ds(c * CHUNK, CHUNK)
                fn = functools.partial(_gla_chunk, consts=consts[d], reverse=rev)
                _, vjps[d, c] = jax.vjp(fn, hq_ref[r, :], hi_ref[r, :], z_ref[r, :], lbl_ref[0:1, :], lbl_ref[1:2, :], save_ref[0, 0, c])
            for d, rev, c in work[w0:w0 + GLA_BATCH_BWD]:
                dq_ref, dv_ref, dz_ref = outs[3 * d:3 * d + 3]
                r = pl.ds(c * CHUNK, CHUNK)
                d_hq, d_hi, d_z, d_l0, d_l1, dsts[d] = vjps[d, c]((ins[6 * d + 5][r, :], dsts[d]))
                dq_ref[r, :] = d_hq.astype(dq_ref.dtype)
                dv_ref[r, :] = d_hi.astype(dv_ref.dtype)
                dz_ref[r, :] = d_z.astype(dz_ref.dtype)
                dls[d] = [dls[d][0] + d_l0, dls[d][1] + d_l1]
        for d in range(nd):
            dst_refs[d][...] = dsts[d]
            dl_refs[d][0, 0:1, :] += dls[d][0]
            dl_refs[d][0, 1:2, :] += dls[d][1]

    def tb(rev):
        return (lambda i: i) if rev else (lambda i: nblk - 1 - i)

    tok = lambda rev: pl.BlockSpec((rows, HEAD_PAIR), lambda b, p, i: (b * nblk + tb(rev)(i), p))
    lspec = pl.BlockSpec((2, HEAD_PAIR), lambda b, p, i: (0, p))
    sspec = lambda rev: pl.BlockSpec((1, 1, group, HEAD_PAIR, HEAD_PAIR), lambda b, p, i: (b, p, tb(rev)(i), 0, 0))
    args, in_specs, out_specs = [], [], []
    for d, rev in enumerate(GLA_DIRS):
        args += [hq, hi, zs[d], lbls[d], saved[d], do]
        in_specs += [tok(rev), tok(rev), tok(rev), lspec, sspec(rev), tok(rev)]
        out_specs += [tok(rev)] * 3
    out_specs += [pl.BlockSpec((1, 2, HEAD_PAIR), lambda b, p, i: (b, 0, p))] * nd
    return pl.pallas_call(
        body, name="gla_bwd", grid=(nb, 4, nblk),
        out_shape=[jax.ShapeDtypeStruct((t, A_WIDTH), BF16)] * (3 * nd) + [jax.ShapeDtypeStruct((nb, 2, A_WIDTH), F32)] * nd,
        in_specs=in_specs, out_specs=out_specs,
        scratch_shapes=[pltpu.VMEM((HEAD_PAIR, HEAD_PAIR), F32)] * nd,
        compiler_params=_params(),
    )(*args)


def _head_mean_matrix():
    r = lax.broadcasted_iota(jnp.int32, (A_WIDTH, A_WIDTH), 0) // 64
    c = lax.broadcasted_iota(jnp.int32, (A_WIDTH, A_WIDTH), 1) // 64
    return jnp.where(r == c, 1.0 / 64.0, 0.0).astype(BF16)


def _gla_out(o_f, o_b, hg, g, mean_mat):
    o = o_f + o_b
    ms = _group_mean(o * o, mean_mat)
    return o * lax.rsqrt(ms + EPS) * g * (hg * _sigmoid(hg))


def _gla_combine(o_f, o_b, hg, g, tm):
    t = o_f.shape[0]

    def body(of_ref, ob_ref, hg_ref, g_ref, y_ref):
        y_ref[...] = _gla_out(of_ref[...], ob_ref[...], hg_ref[...], g_ref[...], _head_mean_matrix())

    tok = pl.BlockSpec((tm, A_WIDTH), lambda i: (i, 0))
    return pl.pallas_call(
        body, name="gla_combine_fwd", grid=(t // tm,), out_shape=jax.ShapeDtypeStruct((t, A_WIDTH), F32),
        in_specs=[tok, tok, tok, _const_spec((1, A_WIDTH))], out_specs=tok, compiler_params=_params(),
    )(o_f, o_b, hg, g)


def _gla_combine_bwd(o_f, o_b, hg, g, dy, tm):
    t = o_f.shape[0]

    def body(of_ref, ob_ref, hg_ref, g_ref, dy_ref, do_ref, dhg_ref, dg_ref):
        mean_mat = _head_mean_matrix()
        fn = lambda o, hgv, gv: _gla_out(o, jnp.zeros_like(o), hgv, gv, mean_mat)
        _, vjp = jax.vjp(fn, of_ref[...] + ob_ref[...], hg_ref[...], g_ref[...])
        d_o, d_hg, d_g = vjp(dy_ref[...])
        do_ref[...] = d_o
        dhg_ref[...] = d_hg.astype(dhg_ref.dtype)

        @pl.when(pl.program_id(0) == 0)
        def _():
            dg_ref[...] = jnp.zeros_like(dg_ref)

        dg_ref[...] += d_g

    tok = pl.BlockSpec((tm, A_WIDTH), lambda i: (i, 0))
    vec = pl.BlockSpec((1, A_WIDTH), lambda i: (0, 0))
    return pl.pallas_call(
        body, name="gla_combine_bwd", grid=(t // tm,),
        out_shape=[jax.ShapeDtypeStruct((t, A_WIDTH), F32), jax.ShapeDtypeStruct((t, A_WIDTH), BF16),
                   jax.ShapeDtypeStruct((1, A_WIDTH), F32)],
        in_specs=[tok, tok, tok, _const_spec((1, A_WIDTH)), tok], out_specs=[tok, tok, vec], compiler_params=_params(),
    )(o_f, o_b, hg, g, dy)


def _post_fwd(x, ya, oattn, tgt, g_mla, w_out, g2, w_gate, w_up, w_down, g_fin, tm):
    t = x.shape[0]

    def body(x_ref, ya_ref, oa_ref, tgt_ref, gm_ref, wo_ref, g2_ref, wg_ref, wu_ref, wd_ref, gf_ref,
             x1_ref, x2_ref, gate_ref, up_ref, loss_ref):
        part = jnp.zeros((1, 1), F32)
        for j in range(tm // min(tm, ROW_SUB)):
            r = pl.ds(j * min(tm, ROW_SUB), min(tm, ROW_SUB))
            yb = _rms(oa_ref[r, :], gm_ref[...])
            x1 = x_ref[r, :] + _dot(ya_ref[r, :].astype(BF16), wo_ref[0:A_WIDTH, :]) + _dot(yb.astype(BF16), wo_ref[A_WIDTH:, :])
            x1_ref[r, :] = x1
            h2 = _rms(x1, g2_ref[...]).astype(BF16)
            gate, up = _dot_nt(h2, wg_ref[...]), _dot_nt(h2, wu_ref[...])
            gate_ref[r, :] = gate.astype(BF16)
            up_ref[r, :] = up.astype(BF16)
            act = (gate * _sigmoid(gate) * up).astype(BF16)
            x2 = x1 + _dot(act, wd_ref[...])
            x2_ref[r, :] = x2
            err = _rms(x2, gf_ref[...]) - tgt_ref[r, :]
            part = part + 0.5 * jnp.sum(jnp.mean(err * err, axis=-1, keepdims=True), axis=0, keepdims=True)

        @pl.when(pl.program_id(0) == 0)
        def _():
            loss_ref[...] = jnp.zeros_like(loss_ref)

        loss_ref[...] += jnp.broadcast_to(part, loss_ref.shape)

    tok = lambda wd: pl.BlockSpec((tm, wd), lambda i: (i, 0))
    return pl.pallas_call(
        body, name="post_fwd", grid=(t // tm,),
        out_shape=[jax.ShapeDtypeStruct((t, D_MODEL), F32)] * 2 + [jax.ShapeDtypeStruct((t, D_FF), BF16)] * 2
        + [jax.ShapeDtypeStruct((1, 128), F32)],
        in_specs=[tok(D_MODEL), tok(A_WIDTH), tok(512), tok(D_MODEL), _const_spec((1, 512)), _const_spec((D_MODEL, D_MODEL)),
                  _const_spec((1, D_MODEL)), _const_spec((D_FF, D_MODEL)), _const_spec((D_FF, D_MODEL)),
                  _const_spec((D_FF, D_MODEL)), _const_spec((1, D_MODEL))],
        out_specs=[tok(D_MODEL), tok(D_MODEL), tok(D_FF), tok(D_FF), pl.BlockSpec((1, 128), lambda i: (0, 0))],
        compiler_params=_params(),
    )(x, ya, oattn, tgt, g_mla, w_out, g2, w_gate, w_up, w_down, g_fin)


def _post_bwd(x1, x2, gate_b, up_b, ya, oattn, tgt, g_mla, w_out, g2, w_gate, w_up, w_down, g_fin, tm):
    t = x1.shape[0]

    def body(x1_ref, x2_ref, gate_ref, up_ref, ya_ref, oa_ref, tgt_ref, gm_ref, wo_ref, g2_ref, wg_ref, wu_ref, wd_ref, gf_ref,
             dx1_ref, dya_ref, doa_ref, ycat_ref, dx1b_ref, h2_ref, dgate_ref, dup_ref, act_ref, dx2b_ref,
             dgm_ref, dg2_ref, dgf_ref):
        x1, x2 = x1_ref[...], x2_ref[...]
        dy = (_rms(x2, gf_ref[...]) - tgt_ref[...]) * (1.0 / D_MODEL)
        dx2, dgf = _rms_bwd(x2, gf_ref[...], dy)
        dx2b = dx2.astype(BF16)
        dx2b_ref[...] = dx2b
        h2_ref[...] = _rms(x1, g2_ref[...]).astype(BF16)
        gate, up = gate_ref[...].astype(F32), up_ref[...].astype(F32)
        sg = _sigmoid(gate)
        sl = gate * sg
        act_ref[...] = (sl * up).astype(BF16)
        dact = _dot_nt(dx2b, wd_ref[...])
        dup = (dact * sl).astype(BF16)
        dgate = (dact * up * (sg * (1.0 + gate * (1.0 - sg)))).astype(BF16)
        dup_ref[...] = dup
        dgate_ref[...] = dgate
        dh2 = _dot(dgate, wg_ref[...]) + _dot(dup, wu_ref[...])
        dx1n, dg2 = _rms_bwd(x1, g2_ref[...], dh2)
        dx1 = dx2 + dx1n
        dx1_ref[...] = dx1
        dx1b = dx1.astype(BF16)
        dx1b_ref[...] = dx1b
        oa = oa_ref[...]
        ycat_ref[:, 0:A_WIDTH] = ya_ref[...].astype(BF16)
        ycat_ref[:, A_WIDTH:] = _rms(oa, gm_ref[...]).astype(BF16)
        dya_ref[...] = _dot_nt(dx1b, wo_ref[0:A_WIDTH, :])
        doa, dgm = _rms_bwd(oa, gm_ref[...], _dot_nt(dx1b, wo_ref[A_WIDTH:, :]))
        doa_ref[...] = doa

        @pl.when(pl.program_id(0) == 0)
        def _():
            dgm_ref[...] = jnp.zeros_like(dgm_ref)
            dg2_ref[...] = jnp.zeros_like(dg2_ref)
            dgf_ref[...] = jnp.zeros_like(dgf_ref)

        dgm_ref[...] += dgm
        dg2_ref[...] += dg2
        dgf_ref[...] += dgf

    tok = lambda wd: pl.BlockSpec((tm, wd), lambda i: (i, 0))
    vec = lambda wd: pl.BlockSpec((1, wd), lambda i: (0, 0))
    sds = lambda wd, dt: jax.ShapeDtypeStruct((t, wd), dt)
    return pl.pallas_call(
        body, name="post_bwd", grid=(t // tm,),
        out_shape=[sds(D_MODEL, F32), sds(512, F32), sds(512, F32), sds(D_MODEL, BF16), sds(D_MODEL, BF16), sds(D_MODEL, BF16),
                   sds(D_FF, BF16), sds(D_FF, BF16), sds(D_FF, BF16), sds(D_MODEL, BF16),
                   jax.ShapeDtypeStruct((1, 512), F32), jax.ShapeDtypeStruct((1, D_MODEL), F32), jax.ShapeDtypeStruct((1, D_MODEL), F32)],
        in_specs=[tok(D_MODEL), tok(D_MODEL), tok(D_FF), tok(D_FF), tok(512), tok(512), tok(D_MODEL), _const_spec((1, 512)),
                  _const_spec((D_MODEL, D_MODEL)), _const_spec((1, D_MODEL)), _const_spec((D_FF, D_MODEL)),
                  _const_spec((D_FF, D_MODEL)), _const_spec((D_FF, D_MODEL)), _const_spec((1, D_MODEL))],
        out_specs=[tok(D_MODEL), tok(512), tok(512), tok(D_MODEL), tok(D_MODEL), tok(D_MODEL), tok(D_FF), tok(D_FF), tok(D_FF),
                   tok(D_MODEL), vec(512), vec(D_MODEL), vec(D_MODEL)],
        compiler_params=_params(),
    )(x1, x2, gate_b, up_b, ya, oattn, tgt, g_mla, w_out, g2, w_gate, w_up, w_down, g_fin)


def _matmul_tn(a, b, tn, tt, tag, b_cols=None, k_out=None, exchange=()):
    t, k = a.shape
    c0, n = (0, b.shape[1]) if b_cols is None else b_cols
    k_out = k if k_out is None else k_out
    last = t // tt - 1
    ne = len(exchange)
    n_j = n // tn

    def body(a_ref, b_ref, *rest):
        o_ref, acc_ref = rest[ne], rest[2 * ne + 1]
        if ne:
            start, finish = _exchange_protocol(rest[:ne], rest[ne + 1:2 * ne + 1], [True] * ne, *rest[2 * ne + 2:])
            pl.when((pl.program_id(0) == 0) & (pl.program_id(1) == 0))(start)
        part = _dot_tn(a_ref[...], b_ref[...])

        @pl.when(pl.program_id(1) == 0)
        def _():
            acc_ref[...] = part

        @pl.when(pl.program_id(1) > 0)
        def _():
            acc_ref[...] += part

        @pl.when(pl.program_id(1) == last)
        def _():
            o_ref[...] = acc_ref[0:k_out, :].astype(o_ref.dtype)

        if ne:
            pl.when((pl.program_id(0) == n_j - 1) & (pl.program_id(1) == last))(finish)

    any_spec = pl.BlockSpec(memory_space=pl.ANY)
    out = pl.pallas_call(
        body, name="wgrad_" + tag, grid=(n_j, t // tt),
        out_shape=[jax.ShapeDtypeStruct((k_out, n), BF16)] + _slot_shapes(exchange, [True] * ne),
        in_specs=[pl.BlockSpec((tt, k), lambda j, i: (i, 0)), pl.BlockSpec((tt, tn), lambda j, i: (i, j + c0 // tn))]
        + [any_spec] * ne,
        out_specs=[pl.BlockSpec((k_out, tn), lambda j, i: (0, j))] + [any_spec] * ne,
        scratch_shapes=[pltpu.VMEM((k, tn), F32)] + (_comm_sems(ne) if ne else []),
        compiler_params=_params(),
    )(a, b, *exchange)
    return out if ne else out[0]


def _mla_qkv_bwd(cq, ckv, g_qa, g_kva, w_q, w_kv, tables, dq, dk, dv, seq, tm):
    t = cq.shape[0]
    nblk = seq // tm

    def body(cq_ref, ckv_ref, gq_ref, gk_ref, wq_ref, wkv_ref, c_ref, sa_ref, sb_ref, dq_ref, dk_ref, dv_ref,
             dcq_ref, dckv_ref, dkr_ref, cqn_ref, dqf_ref, ckn_ref, dkv_ref, dgq_ref, dgk_ref):
        cos_t, sin_a, sin_b = c_ref[...], sa_ref[...], sb_ref[...]
        cqn_ref[...] = _rms(cq_ref[...], gq_ref[...]).astype(BF16)
        ckn_ref[...] = _rms(ckv_ref[...], gk_ref[...]).astype(BF16)
        dkr = jnp.zeros((tm, 128), F32)
        for h in range(B_HEADS):
            lo = h * QK_PAD
            dqf_ref[:, lo:lo + 128] = (dq_ref[:, lo:lo + 128].astype(F32) * ATTN_SCALE).astype(BF16)
            dq_rope = dq_ref[:, lo + 128:lo + 256].astype(F32) * ATTN_SCALE
            dqf_ref[:, lo + 128:lo + 256] = _rope_t(dq_rope, cos_t, sin_a, sin_b).astype(BF16)
            dkv_ref[:, lo:lo + 128] = dk_ref[:, lo:lo + 128].astype(BF16)
            dkv_ref[:, lo + 128:lo + 256] = dv_ref[:, h * B_V:(h + 1) * B_V].astype(BF16)
            dkr = dkr + dk_ref[:, lo + 128:lo + 256]
        dkr_ref[...] = _rope_t(dkr, cos_t, sin_a, sin_b).astype(dkr_ref.dtype)
        dcq, dgq = _rms_bwd(cq_ref[...], gq_ref[...], _dot(dqf_ref[...], wq_ref[...]))
        dckv, dgk = _rms_bwd(ckv_ref[...], gk_ref[...], _dot_nt(dkv_ref[...], wkv_ref[...]))
        dcq_ref[...] = dcq.astype(dcq_ref.dtype)
        dckv_ref[...] = dckv.astype(dckv_ref.dtype)

        @pl.when(pl.program_id(0) == 0)
        def _():
            dgq_ref[...] = jnp.zeros_like(dgq_ref)
            dgk_ref[...] = jnp.zeros_like(dgk_ref)

        dgq_ref[...] += dgq
        dgk_ref[...] += dgk

    tok = lambda wd: pl.BlockSpec((tm, wd), lambda i: (i, 0))
    vec = lambda wd: pl.BlockSpec((1, wd), lambda i: (0, 0))
    tab = pl.BlockSpec((tm, 128), lambda i: (i % nblk, 0))
    sds = lambda wd, dt: jax.ShapeDtypeStruct((t, wd), dt)
    return pl.pallas_call(
        body, name="mla_qkv_bwd", grid=(t // tm,),
        out_shape=[sds(Q_LORA, BF16), sds(KV_LORA, BF16), sds(128, BF16), sds(Q_LORA, BF16), sds(1024, BF16), sds(KV_LORA, BF16),
                   sds(1024, BF16), jax.ShapeDtypeStruct((1, Q_LORA), F32), jax.ShapeDtypeStruct((1, KV_LORA), F32)],
        in_specs=[tok(Q_LORA), tok(KV_LORA), _const_spec((1, Q_LORA)), _const_spec((1, KV_LORA)),
                  _const_spec((1024, Q_LORA)), _const_spec((KV_LORA, 1024)), tab, tab, tab,
                  tok(1024), tok(1024), tok(512)],
        out_specs=[tok(Q_LORA), tok(KV_LORA), tok(128), tok(Q_LORA), tok(1024), tok(KV_LORA), tok(1024),
                   vec(Q_LORA), vec(KV_LORA)],
        compiler_params=_params(),
    )(cq, ckv, g_qa, g_kva, w_q, w_kv, *tables, dq, dk, dv)


def _inproj_bwd(x, g1, w_in, dx1, pieces, tm):
    t = x.shape[0]
    counts = [len(p) for p in pieces]
    flat = [a for p in pieces for a in p]
    widths = [wd for wd, p in zip(IN_WIDTHS, pieces) for _ in p]

    def body(x_ref, g_ref, w_ref, dx1_ref, *refs):
        ins = refs[:len(flat)]
        dx_ref, h_ref, dp_ref, dg_ref = refs[len(flat):]
        xv = x_ref[...]
        h_ref[...] = _rms(xv, g_ref[...]).astype(BF16)
        off, j = 0, 0
        for wd, cnt in zip(IN_WIDTHS, counts):
            acc = ins[j][...].astype(F32)
            for jj in range(1, cnt):
                acc = acc + ins[j + jj][...].astype(F32)
            dp_ref[:, off:off + wd] = acc.astype(BF16)
            off += wd
            j += cnt
        dxn, dg = _rms_bwd(xv, g_ref[...], _dot(dp_ref[...], w_ref[...]))
        dx_ref[...] = dx1_ref[...] + dxn

        @pl.when(pl.program_id(0) == 0)
        def _():
            dg_ref[...] = jnp.zeros_like(dg_ref)

        dg_ref[...] += dg

    tok = lambda wd: pl.BlockSpec((tm, wd), lambda i: (i, 0))
    return pl.pallas_call(
        body, name="inproj_bwd", grid=(t // tm,),
        out_shape=[jax.ShapeDtypeStruct((t, D_MODEL), F32), jax.ShapeDtypeStruct((t, D_MODEL), BF16),
                   jax.ShapeDtypeStruct((t, D_IN_PAD), BF16), jax.ShapeDtypeStruct((1, D_MODEL), F32)],
        in_specs=[tok(D_MODEL), _const_spec((1, D_MODEL)), _const_spec((D_IN_PAD, D_MODEL)), tok(D_MODEL)] + [tok(wd) for wd in widths],
        out_specs=[tok(D_MODEL), tok(D_MODEL), tok(D_IN_PAD), pl.BlockSpec((1, D_MODEL), lambda i: (0, 0))],
        compiler_params=_params(),
    )(x, g1, w_in, dx1, *flat)


def _cols_from_slots(g):
    n, r, cs = g.shape
    return g.transpose(1, 0, 2).reshape(r, n * cs)


def _cols_to_slots(full):
    r, c = full.shape
    return full.reshape(r, N_DEV, c // N_DEV).transpose(1, 0, 2)


def _arrange_w_in_t(w_in_t):
    return jnp.concatenate([w_in_t, jnp.zeros((D_IN_PAD - D_IN, D_MODEL), w_in_t.dtype)], axis=0)


def _arrange_w_q_t(w_q_t):
    q3 = w_q_t.reshape(B_HEADS, B_NOPE + B_ROPE, Q_LORA)
    pad = jnp.zeros((B_HEADS, QK_PAD - B_NOPE - B_ROPE, Q_LORA), w_q_t.dtype)
    return jnp.concatenate([q3, pad], axis=1).reshape(B_HEADS * QK_PAD, Q_LORA)


def _unarrange_w_q_t(d_q_t):
    return d_q_t.reshape(B_HEADS, QK_PAD, Q_LORA)[:, :B_NOPE + B_ROPE].reshape(B_HEADS * (B_NOPE + B_ROPE), Q_LORA)


def _step_core(x, loss_target, small_w, lb_full, early_full, late, seq, group, tiles, distributed):
    g1, g_hgrn, g_qa, g_kva, g_mla, g2, g_fin = small_w
    w_in, w_q, w_kv = _arrange_w_in_t(early_full[0]), _arrange_w_q_t(early_full[1]), early_full[2]
    nb = x.shape[0]
    t = nb * seq
    tm, tm_fwd, tq_f, tq_b, tt = tiles
    xt = x.reshape(t, D_MODEL)
    tgt = loss_target.reshape(t, D_MODEL)
    tables = _rope_tables(seq)

    hq, hi, zf, zb, hg, cq, ckv, kr = _inproj(xt, g1, w_in, tm_fwd)
    qcat, kcat, vv = _mla_qkv(cq, ckv, kr, g_qa, g_kva, w_q, w_kv, tables, seq, tm_fwd)
    if distributed:
        oattn, lse, *late_slots = _attn_fwd(qcat, kcat, vv, nb, seq, tq_f, gather=tuple(late))
    else:
        oattn, lse = _attn_fwd(qcat, kcat, vv, nb, seq, tq_f)
        late_slots = late
    w_out = late_slots[0].reshape(D_MODEL, D_MODEL)
    w_gate, w_up = late_slots[1].reshape(D_FF, D_MODEL), late_slots[2].reshape(D_FF, D_MODEL)
    w_down = late_slots[3].reshape(D_FF, D_MODEL)
    lbl_f, lbl_b = lb_full[0], lb_full[1]
    o_f, o_b, save_f, save_b = _gla_fwd(hq, hi, (zf, zb), (lbl_f, lbl_b), nb, seq, group)
    ya = _gla_combine(o_f, o_b, hg, g_hgrn, tm_fwd)
    x1, x2, gate_b, up_b, loss_row = _post_fwd(xt, ya, oattn, tgt, g_mla, w_out, g2, w_gate, w_up, w_down, g_fin, tm_fwd)

    (dx1, d_ya, d_oattn, ycat_b, dx1_b, h2_b, dgate_b, dup_b, act_b, dx2_b, d_g_mla, d_g2, d_g_fin) = _post_bwd(
        x1, x2, gate_b, up_b, ya, oattn, tgt, g_mla, w_out, g2, w_gate, w_up, w_down, g_fin, tm)
    d_w_gate = _matmul_tn(dgate_b, h2_b, 512, tt, "gate")
    d_w_up = _matmul_tn(dup_b, h2_b, 512, tt, "up")
    d_w_down = _matmul_tn(act_b, dx2_b, 512, tt, "down")
    d_w_out = _matmul_tn(ycat_b, dx1_b, D_MODEL, tt, "out")
    late_g = [d_w_out.reshape(N_DEV, D_MODEL // N_DEV, D_MODEL)] + [
        g.reshape(N_DEV, D_FF // N_DEV, D_MODEL) for g in (d_w_gate, d_w_up, d_w_down)]
    if distributed:
        dq, dk, dv, *late_g = _attn_bwd(qcat, kcat, vv, oattn, lse, d_oattn, nb, seq, tq_b, exchange=tuple(late_g))
    else:
        dq, dk, dv = _attn_bwd(qcat, kcat, vv, oattn, lse, d_oattn, nb, seq, tq_b)
    (d_cq, d_ckv, d_kr, cqn_b, dqf_b, ckn_b, dkv_b, d_g_qa, d_g_kva) = _mla_qkv_bwd(
        cq, ckv, g_qa, g_kva, w_q, w_kv, tables, dq, dk, dv, seq, tm_fwd)
    d_w_q = _matmul_tn(dqf_b, cqn_b, Q_LORA, tt, "q_b")
    d_w_kv = _matmul_tn(ckn_b, dkv_b, B_HEADS * (B_NOPE + B_V), tt, "kv_b")
    d_o, d_hg, d_g_hgrn = _gla_combine_bwd(o_f, o_b, hg, g_hgrn, d_ya, tm_fwd)
    dq_f, dv_f, dz_f, dq_b, dv_b, dz_b, dl_f, dl_b = _gla_bwd(
        hq, hi, (zf, zb), (lbl_f, lbl_b), (save_f, save_b), d_o, nb, seq, group)
    grad_x, h1_b, dproj_b, d_g1 = _inproj_bwd(
        xt, g1, w_in, dx1, [[dq_f, dq_b], [dv_f, dv_b], [dz_f], [dz_b], [d_hg], [d_cq], [d_ckv], [d_kr]], tm_fwd)
    half = D_MODEL // 2
    in_slots = lambda g: g.reshape(N_DEV, D_IN // N_DEV, half)
    g_in_a = in_slots(_matmul_tn(dproj_b, h1_b, half, tt, "in_a", b_cols=(0, half), k_out=D_IN))
    if distributed:
        d_w_in_b, g_in_a = _matmul_tn(dproj_b, h1_b, half, tt, "in_b", b_cols=(half, half), k_out=D_IN, exchange=(g_in_a,))
    else:
        d_w_in_b = _matmul_tn(dproj_b, h1_b, half, tt, "in_b", b_cols=(half, half), k_out=D_IN)

    early_g = [in_slots(d_w_in_b), _unarrange_w_q_t(d_w_q).reshape(N_DEV, 768 // N_DEV, Q_LORA), _cols_to_slots(d_w_kv)]
    d_lb = jnp.stack([jnp.sum(dl_f, axis=0), jnp.sum(dl_b, axis=0)], axis=0)
    small_grads = [d_g1, d_g_hgrn, d_g_qa, d_g_kva, d_g_mla, d_g2, d_g_fin]
    return loss_row, grad_x.reshape(nb, seq, D_MODEL), g_in_a, early_g, late_g, small_grads, d_lb


def kernel(x, norm1_g, w_in, lb_logits, hgrn_norm_g, q_a_norm_g, w_q_b, kv_a_norm_g, w_kv_b, mla_norm_g, w_out, norm2_g, w_gate, w_up, w_down, final_norm_g, loss_target, m_norm1_g, m_w_in, m_lb_logits, m_hgrn_norm_g, m_q_a_norm_g, m_w_q_b, m_kv_a_norm_g, m_w_kv_b, m_mla_norm_g, m_w_out, m_norm2_g, m_w_gate, m_w_up, m_w_down, m_final_norm_g, v_norm1_g, v_w_in, v_lb_logits, v_hgrn_norm_g, v_q_a_norm_g, v_w_q_b, v_kv_a_norm_g, v_w_kv_b, v_mla_norm_g, v_w_out, v_norm2_g, v_w_gate, v_w_up, v_w_down, v_final_norm_g):
    big_w = [w_in, w_q_b, w_kv_b, w_out, w_gate, w_up, w_down]
    big_m = [m_w_in, m_w_q_b, m_w_kv_b, m_w_out, m_w_gate, m_w_up, m_w_down]
    big_v = [v_w_in, v_w_q_b, v_w_kv_b, v_w_out, v_w_gate, v_w_up, v_w_down]
    small_w = [norm1_g, hgrn_norm_g, q_a_norm_g, kv_a_norm_g, mla_norm_g, norm2_g, final_norm_g]
    small_m = [m_norm1_g, m_hgrn_norm_g, m_q_a_norm_g, m_kv_a_norm_g, m_mla_norm_g, m_norm2_g, m_final_norm_g]
    small_v = [v_norm1_g, v_hgrn_norm_g, v_q_a_norm_g, v_kv_a_norm_g, v_mla_norm_g, v_norm2_g, v_final_norm_g]
    seq = x.shape[1]
    my_id = 4 * lax.axis_index("x") + 2 * lax.axis_index("y") + lax.axis_index("c")

    shard = lambda w: w[0].astype(BF16)
    col_t = lambda w: jnp.swapaxes(w, 1, 2)[0]
    shard_t = lambda w: col_t(w).astype(BF16)
    g_in, g_q, g_kv, g_lb = _all_gather_call([shard_t(w_in), shard_t(w_q_b), shard(w_kv_b), lb_logits.reshape(4, 64)])
    early_full = (g_in.reshape(D_IN, D_MODEL), g_q.reshape(768, Q_LORA), _cols_from_slots(g_kv))
    lb_full = g_lb.reshape(N_DEV, 2, 2, 64).transpose(1, 2, 0, 3).reshape(2, 2, 512)

    as_row = lambda a: a.reshape(1, -1)
    loss_row, grad_x, recv_in_a, early_g, late_recv, small_g, d_lb = _step_core(
        x, loss_target, [as_row(s) for s in small_w], lb_full, early_full,
        [shard(w_out), shard_t(w_gate), shard_t(w_up), shard(w_down)], seq, min(16, seq // CHUNK),
        (256, 512, min(1024, seq), min(1024, seq), min(2048, 2 * seq)), True)

    grads, deltas, new_ms, new_vs = {}, {}, {}, {}
    views = {name: (col_t if name in ("w_in", "w_q_b", "w_gate", "w_up") else (lambda a: a[0])) for name, _, _, _ in BIG}
    backs = {name: ((lambda a: jnp.swapaxes(a[None], 1, 2)) if name in ("w_in", "w_q_b", "w_gate", "w_up") else (lambda a: a[None]))
             for name, _, _, _ in BIG}
    by_name = {name: (w, m, v) for (name, _, _, _), w, m, v in zip(BIG, big_w, big_m, big_v)}
    late_names = ["w_out", "w_gate", "w_up", "w_down"]
    n_small = len(small_g)
    g_l, d_l, nm_l, nv_l, recv = _adamw_recv_hosting(
        [views[n](by_name[n][0]) for n in late_names], list(late_recv), [views[n](by_name[n][1]) for n in late_names],
        [views[n](by_name[n][2]) for n in late_names],
        early_g + small_g + [d_lb.reshape(4, 512), loss_row], [True] * 3 + [False] * (n_small + 2))
    for i, name in enumerate(late_names):
        grads[name], deltas[name], new_ms[name], new_vs[name] = (backs[name](a[i]) for a in (g_l, d_l, nm_l, nv_l))
    sums = _sum_slots_call(recv[3:])
    g_small = [g.reshape(s.shape) for g, s in zip(sums[:n_small], small_w)]
    g_lb_own = lax.dynamic_index_in_dim(sums[n_small].reshape(2, 2, N_DEV, 64), my_id, axis=2, keepdims=False)
    loss = sums[n_small + 1][0, 0]

    for name, r in zip(["w_in", "w_q_b", "w_kv_b"], recv[:3]):
        w, m, v = (views[name](a) for a in by_name[name])
        g, d, nm, nv = _adamw_recv_halves(w, (recv_in_a, r), m, v, name) if name == "w_in" else _adamw_recv(w, r, m, v, name)
        grads[name], deltas[name], new_ms[name], new_vs[name] = (backs[name](a) for a in (g, d, nm, nv))
    lb_rows = lambda a: a.reshape(4, 64)
    d_s, nm_s, nv_s = _adamw_small(
        [as_row(a) for a in small_w] + [lb_rows(lb_logits)], [as_row(a) for a in g_small] + [lb_rows(g_lb_own)],
        [as_row(a) for a in small_m] + [lb_rows(m_lb_logits)], [as_row(a) for a in small_v] + [lb_rows(v_lb_logits)])
    for i, (s, (name, _)) in enumerate(zip(small_w + [lb_logits], SMALL + (("lb_logits", 0),))):
        grads[name] = (g_small + [g_lb_own])[i]
        deltas[name], new_ms[name], new_vs[name] = d_s[i].reshape(s.shape), nm_s[i].reshape(s.shape), nv_s[i].reshape(s.shape)

    order = ["norm1_g", "w_in", "lb_logits", "hgrn_norm_g", "q_a_norm_g", "w_q_b", "kv_a_norm_g", "w_kv_b", "mla_norm_g",
             "w_out", "norm2_g", "w_gate", "w_up", "w_down", "final_norm_g"]
    return (loss, grad_x, *[grads[n] for n in order], *[deltas[n] for n in order],
            *[new_ms[n] for n in order], *[new_vs[n] for n in order])
```

```python
import functools
import math

import jax
import jax.numpy as jnp
from jax import lax
from jax.experimental import pallas as pl
from jax.experimental.pallas import tpu as pltpu

F32 = jnp.float32
BF16 = jnp.bfloat16

N_DEV = 8
D_MODEL = 1024
D_FF = 2816
A_WIDTH = 512
HEAD_PAIR = 128
CHUNK = 64
B_HEADS = 4
B_NOPE = 128
B_ROPE = 64
B_V = 128
QK_PAD = 256
Q_LORA = 384
KV_LORA = 256
D_IN = 3264
D_IN_PAD = 3328
IN_WIDTHS = (512, 512, 512, 512, 512, Q_LORA, KV_LORA, 128)
ROPE_THETA = 10000.0
EPS = 1e-6
ATTN_SCALE = (B_NOPE + B_ROPE) ** -0.5
ATTN_SUB = 256
ATTN_SUB_BWD = 256
ROW_SUB = 256
ADAM_LR, ADAM_B1, ADAM_B2, ADAM_EPS, ADAM_WD, ADAM_STEP = 0.001, 0.9, 0.999, 1e-08, 0.01, 10
VMEM_LIMIT = 60 * 1024 * 1024
MESH = pl.DeviceIdType.MESH

BIG = (("w_in", 1024, D_IN, 1), ("w_q_b", Q_LORA, 768, 1), ("w_kv_b", KV_LORA, 1024, 1), ("w_out", 1024, 1024, 0),
       ("w_gate", 1024, D_FF, 1), ("w_up", 1024, D_FF, 1), ("w_down", D_FF, 1024, 0))
SMALL = (("norm1_g", 1024), ("hgrn_norm_g", 512), ("q_a_norm_g", 384), ("kv_a_norm_g", 256), ("mla_norm_g", 512),
         ("norm2_g", 1024), ("final_norm_g", 1024))


def _params(**kw):
    return pltpu.CompilerParams(vmem_limit_bytes=VMEM_LIMIT, **kw)


def _const_spec(shape):
    return pl.BlockSpec(shape, lambda *_: (0,) * len(shape), pipeline_mode=pl.Buffered(1))


def _dot(a, b):
    return jnp.dot(a, b, preferred_element_type=F32)


def _dot_nt(a, b):
    return lax.dot_general(a, b, (((1,), (1,)), ((), ())), preferred_element_type=F32)


def _dot_tn(a, b):
    return lax.dot_general(a, b, (((0,), (0,)), ((), ())), preferred_element_type=F32)


@jax.custom_vjp
def _mm(a, b):
    return _dot(a.astype(BF16), b.astype(BF16))


def _mm_fwd(a, b):
    return _mm(a, b), (a, b)


def _mm_bwd(res, g):
    a, b = res
    gb = g.astype(BF16)
    return _dot_nt(gb, b.astype(BF16)), _dot_tn(a.astype(BF16), gb)


_mm.defvjp(_mm_fwd, _mm_bwd)


@jax.custom_vjp
def _mm_nt(a, b):
    return _dot_nt(a.astype(BF16), b.astype(BF16))


def _mm_nt_fwd(a, b):
    return _mm_nt(a, b), (a, b)


def _mm_nt_bwd(res, g):
    a, b = res
    gb = g.astype(BF16)
    return _dot(gb, b.astype(BF16)), _dot_tn(gb, a.astype(BF16))


_mm_nt.defvjp(_mm_nt_fwd, _mm_nt_bwd)


@jax.custom_vjp
def _mm_tn(a, b):
    return _dot_tn(a.astype(BF16), b.astype(BF16))


def _mm_tn_fwd(a, b):
    return _mm_tn(a, b), (a, b)


def _mm_tn_bwd(res, g):
    a, b = res
    gb = g.astype(BF16)
    return _dot_nt(b.astype(BF16), gb), _dot(a.astype(BF16), gb)


_mm_tn.defvjp(_mm_tn_fwd, _mm_tn_bwd)


def _dot_exact_rhs(a, m):
    hi = a.astype(BF16)
    lo = (a - hi.astype(F32)).astype(BF16)
    return _dot(hi, m) + _dot(lo, m)


@jax.custom_vjp
def _group_mean(a, m):
    return _dot_exact_rhs(a, m)


def _group_mean_fwd(a, m):
    return _group_mean(a, m), m


def _group_mean_bwd(m, g):
    return _dot_exact_rhs(g, m), jnp.zeros_like(m)


_group_mean.defvjp(_group_mean_fwd, _group_mean_bwd)


def _roll_rows(a, shift):
    return pltpu.roll(a, shift, 0)


def _cumsum_rows_raw(a, reverse):
    n = a.shape[0]
    row = lax.broadcasted_iota(jnp.int32, a.shape, 0)
    s = 1
    while s < n:
        if reverse:
            a = a + jnp.where(row < n - s, _roll_rows(a, n - s), 0.0)
        else:
            a = a + jnp.where(row >= s, _roll_rows(a, s), 0.0)
        s *= 2
    return a


@functools.partial(jax.custom_vjp, nondiff_argnums=(1,))
def _cumsum_rows(a, reverse):
    return _cumsum_rows_raw(a, reverse)


def _cumsum_rows_fwd(a, reverse):
    return _cumsum_rows_raw(a, reverse), None


def _cumsum_rows_bwd(reverse, _, g):
    return (_cumsum_rows_raw(g, not reverse),)


_cumsum_rows.defvjp(_cumsum_rows_fwd, _cumsum_rows_bwd)


def _rms(x, g):
    r = lax.rsqrt(jnp.mean(x * x, axis=-1, keepdims=True) + EPS)
    return x * r * g


def _rms_bwd(x, g, dy):
    r = lax.rsqrt(jnp.mean(x * x, axis=-1, keepdims=True) + EPS)
    xh = x * r
    dg = jnp.sum(dy * xh, axis=0, keepdims=True)
    dxh = dy * g
    dx = r * (dxh - xh * jnp.mean(dxh * xh, axis=-1, keepdims=True))
    return dx, dg


def _sigmoid(a):
    return jax.nn.sigmoid(a)


def _mesh_place():
    x, y, c = lax.axis_index("x"), lax.axis_index("y"), lax.axis_index("c")
    return x, y, c


def _dev_index(p):
    return 4 * p[0] + 2 * p[1] + p[2]


def _comm_sems(n):
    return [pltpu.SemaphoreType.DMA((n, 7)), pltpu.SemaphoreType.DMA((n, 7)), pltpu.SemaphoreType.DMA((n,))]


def _gather_protocol(ins, outs, send_sems, recv_sems, local_sems):
    n = len(ins)
    x, y, c = _mesh_place()
    me, sibling = (x, y, c), (x, y, 1 - c)
    chips = [(1 - x, y), (x, 1 - y), (1 - x, 1 - y)]

    def copy(a, k, block, to, src=None):
        slot = outs[a].at[_dev_index(block)]
        return pltpu.make_async_remote_copy(
            src_ref=slot if src is None else src, dst_ref=slot,
            send_sem=send_sems.at[a, k], recv_sem=recv_sems.at[a, k], device_id=to, device_id_type=MESH)

    def mine(a):
        return pltpu.make_async_copy(ins[a], outs[a].at[_dev_index(me)], local_sems.at[a])

    def first(a):
        return [copy(a, 0, me, sibling, src=ins[a])] + [copy(a, 1 + j, me, (*chip, c), src=ins[a]) for j, chip in enumerate(chips)]

    def start():
        for a in range(n):
            mine(a).start()
            for cp in first(a):
                cp.start()

    def forward():
        for a in range(n):
            for j, chip in enumerate(chips):
                copy(a, 1 + j, (*chip, c), me).wait_recv()
                copy(a, 4 + j, (*chip, c), sibling).start()

    def finish():
        for a in range(n):
            copy(a, 0, sibling, me).wait_recv()
            for j, chip in enumerate(chips):
                copy(a, 4 + j, (*chip, 1 - c), me).wait_recv()
        for a in range(n):
            mine(a).wait()
            for cp in first(a):
                cp.wait_send()
            for j, chip in enumerate(chips):
                copy(a, 4 + j, (*chip, c), sibling).wait_send()

    return start, forward, finish


def _exchange_protocol(ins, outs, scatter, send_sems, recv_sems, local_sems):
    n = len(ins)
    x, y, c = _mesh_place()
    me = (x, y, c)
    my_id = _dev_index(me)
    rels = [(dx, dy, dc) for dx in (0, 1) for dy in (0, 1) for dc in (0, 1)][1:]

    def peer_of(rel):
        return tuple(1 - v if d else v for v, d in zip(me, rel))

    def src(a, dev):
        return ins[a].at[dev] if scatter[a] else ins[a]

    def send(a, k):
        peer = peer_of(rels[k])
        return pltpu.make_async_remote_copy(
            src_ref=src(a, _dev_index(peer)), dst_ref=outs[a].at[my_id],
            send_sem=send_sems.at[a, k], recv_sem=recv_sems.at[a, k], device_id=peer, device_id_type=MESH)

    def arrival(a, k):
        peer = peer_of(rels[k])
        return pltpu.make_async_remote_copy(
            src_ref=src(a, my_id), dst_ref=outs[a].at[_dev_index(peer)],
            send_sem=send_sems.at[a, k], recv_sem=recv_sems.at[a, k], device_id=peer, device_id_type=MESH)

    def own(a):
        return pltpu.make_async_copy(src(a, my_id), outs[a].at[my_id], local_sems.at[a])

    def start():
        for a in range(n):
            own(a).start()
            for k in range(7):
                send(a, k).start()

    def finish():
        for a in range(n):
            for k in range(7):
                arrival(a, k).wait_recv()
        for a in range(n):
            for k in range(7):
                send(a, k).wait_send()
            own(a).wait()

    return start, finish


def _slot_shapes(blocks, scatter=None):
    return [jax.ShapeDtypeStruct(b.shape if (scatter and scatter[a]) else (N_DEV,) + b.shape, b.dtype) for a, b in enumerate(blocks)]


def _all_gather_call(blocks):
    n = len(blocks)

    def body(*refs):
        start, forward, finish = _gather_protocol(refs[:n], refs[n:2 * n], *refs[2 * n:])
        start()
        forward()
        finish()

    any_spec = pl.BlockSpec(memory_space=pl.ANY)
    return pl.pallas_call(
        body, name="weights_all_gather", out_shape=_slot_shapes(blocks),
        in_specs=[any_spec] * n, out_specs=[any_spec] * n, scratch_shapes=_comm_sems(n),
    )(*blocks)


def _sum_slots_call(recvs):
    n = len(recvs)

    def body(*refs):
        for in_ref, out_ref in zip(refs[:n], refs[n:]):
            acc = in_ref[0]
            for j in range(1, N_DEV):
                acc = acc + in_ref[j]
            out_ref[...] = acc

    return pl.pallas_call(
        body, name="small_grad_sum", out_shape=[jax.ShapeDtypeStruct(r.shape[1:], F32) for r in recvs],
        compiler_params=_params(),
    )(*recvs)


def _adam_update(w, g, m, v):
    nm = ADAM_B1 * m + (1.0 - ADAM_B1) * g
    nv = ADAM_B2 * v + (1.0 - ADAM_B2) * (g * g)
    bc1 = 1.0 - ADAM_B1 ** ADAM_STEP
    bc2 = 1.0 - ADAM_B2 ** ADAM_STEP
    return -ADAM_LR * ((nm / bc1) / (jnp.sqrt(nv / bc2) + ADAM_EPS) + ADAM_WD * w), nm, nv


def _adamw_recv(w, recv, m, v, tag):
    r, c = w.shape
    tr = r
    for cand in (512, 256, 128):
        if r > cand and r % cand == 0:
            tr = cand
            break

    def body(w_ref, r_ref, m_ref, v_ref, g_ref, d_ref, nm_ref, nv_ref):
        g = r_ref[0].astype(F32)
        for j in range(1, N_DEV):
            g = g + r_ref[j].astype(F32)
        g_ref[...] = g
        d_ref[...], nm_ref[...], nv_ref[...] = _adam_update(w_ref[...], g, m_ref[...], v_ref[...])

    spec = pl.BlockSpec((tr, c), lambda i: (i, 0))
    return pl.pallas_call(
        body, name="adamw_" + tag, out_shape=[jax.ShapeDtypeStruct(w.shape, F32)] * 4, grid=(r // tr,),
        in_specs=[spec, pl.BlockSpec((N_DEV, tr, c), lambda i: (0, i, 0)), spec, spec], out_specs=[spec] * 4,
        compiler_params=_params(),
    )(w, recv, m, v)


def _adamw_recv_halves(w, recv_halves, m, v, tag):
    r, c = w.shape
    half = c // 2

    def body(w_ref, ra_ref, rb_ref, m_ref, v_ref, g_ref, d_ref, nm_ref, nv_ref):
        def update(r_ref):
            g = r_ref[0].astype(F32)
            for j in range(1, N_DEV):
                g = g + r_ref[j].astype(F32)
            g_ref[...] = g
            d_ref[...], nm_ref[...], nv_ref[...] = _adam_update(w_ref[...], g, m_ref[...], v_ref[...])

        pl.when(pl.program_id(0) == 0)(lambda: update(ra_ref))
        pl.when(pl.program_id(0) == 1)(lambda: update(rb_ref))

    spec = pl.BlockSpec((r, half), lambda j: (0, j))
    whole = pl.BlockSpec((N_DEV, r, half), lambda j: (0, 0, 0))
    return pl.pallas_call(
        body, name="adamw_" + tag, out_shape=[jax.ShapeDtypeStruct(w.shape, F32)] * 4, grid=(2,),
        in_specs=[spec, whole, whole, spec, spec], out_specs=[spec] * 4, compiler_params=_params(),
    )(w, *recv_halves, m, v)


def _adamw_recv_hosting(ws, recvs, ms, vs, blocks, scatter):
    n, ne = len(ws), len(blocks)

    def body(*refs):
        ins, ex_in = refs[:4 * n], refs[4 * n:4 * n + ne]
        outs, ex_out = refs[4 * n + ne:8 * n + ne], refs[8 * n + ne:8 * n + 2 * ne]
        start, finish = _exchange_protocol(ex_in, ex_out, scatter, *refs[8 * n + 2 * ne:])
        start()
        for a in range(n):
            r_ref = ins[n + a]
            g = r_ref[0].astype(F32)
            for j in range(1, N_DEV):
                g = g + r_ref[j].astype(F32)
            outs[a][...] = g
            outs[n + a][...], outs[2 * n + a][...], outs[3 * n + a][...] = _adam_update(
                ins[a][...], g, ins[2 * n + a][...], ins[3 * n + a][...])
        finish()

    vmem, any_spec = pl.BlockSpec(memory_space=pltpu.VMEM), pl.BlockSpec(memory_space=pl.ANY)
    out = pl.pallas_call(
        body, name="adamw_late_and_grad_exchange",
        out_shape=[jax.ShapeDtypeStruct(w.shape, F32) for w in ws] * 4 + _slot_shapes(blocks, scatter),
        in_specs=[vmem] * (4 * n) + [any_spec] * ne, out_specs=[vmem] * (4 * n) + [any_spec] * ne,
        scratch_shapes=_comm_sems(ne), compiler_params=_params(),
    )(*ws, *recvs, *ms, *vs, *blocks)
    return out[:n], out[n:2 * n], out[2 * n:3 * n], out[3 * n:4 * n], out[4 * n:]


def _adamw_small(ws, gs, ms, vs):
    n = len(ws)

    def body(*refs):
        ins, outs = refs[:4 * n], refs[4 * n:]
        for a in range(n):
            d, nm, nv = _adam_update(ins[a][...], ins[n + a][...], ins[2 * n + a][...], ins[3 * n + a][...])
            outs[a][...], outs[n + a][...], outs[2 * n + a][...] = d, nm, nv

    out = pl.pallas_call(
        body, name="adamw_small", out_shape=[jax.ShapeDtypeStruct(w.shape, F32) for w in ws] * 3, compiler_params=_params(),
    )(*ws, *gs, *ms, *vs)
    return out[:n], out[n:2 * n], out[2 * n:]


def _tile(t, want):
    return want if t % want == 0 else t


def _inproj(x, g1, w_in, tm):
    t = x.shape[0]

    def body(x_ref, g_ref, w_ref, *outs):
        for j in range(tm // min(tm, ROW_SUB)):
            r = pl.ds(j * min(tm, ROW_SUB), min(tm, ROW_SUB))
            h = _rms(x_ref[r, :], g_ref[...]).astype(BF16)
            off = 0
            for o_ref, wd in zip(outs, IN_WIDTHS):
                o_ref[r, :] = _dot_nt(h, w_ref[off:off + wd, :])
                off += wd

    return pl.pallas_call(
        body, name="inproj_fwd", grid=(t // tm,),
        out_shape=[jax.ShapeDtypeStruct((t, wd), F32) for wd in IN_WIDTHS],
        in_specs=[pl.BlockSpec((tm, D_MODEL), lambda i: (i, 0)), _const_spec((1, D_MODEL)), _const_spec((D_IN_PAD, D_MODEL))],
        out_specs=[pl.BlockSpec((tm, wd), lambda i: (i, 0)) for wd in IN_WIDTHS],
        compiler_params=_params(),
    )(x, g1, w_in)


def _rope_tables(seq):
    inv = 1.0 / (ROPE_THETA ** (jnp.arange(0, B_ROPE, 2, dtype=F32) / B_ROPE))
    ang = jnp.arange(seq, dtype=F32)[:, None] * inv[None, :]
    cos, sin = jnp.cos(ang), jnp.sin(ang)
    z32, z64 = jnp.zeros_like(cos), jnp.zeros((seq, 64), F32)
    cos_t = jnp.concatenate([cos, cos, z64], axis=1)
    sin_a = jnp.concatenate([-sin, z32, z64], axis=1)
    sin_b = jnp.concatenate([z32, sin, z64], axis=1)
    return cos_t, sin_a, sin_b


def _rope(t, cos_t, sin_a, sin_b):
    return t * cos_t + pltpu.roll(t, 96, 1) * sin_a + pltpu.roll(t, 32, 1) * sin_b


def _rope_t(d, cos_t, sin_a, sin_b):
    return d * cos_t + pltpu.roll(d * sin_a, 32, 1) + pltpu.roll(d * sin_b, 96, 1)


def _mla_qkv(cq, ckv, kr, g_qa, g_kva, w_q, w_kv, tables, seq, tm):
    t = cq.shape[0]
    nblk = seq // tm

    def body(cq_ref, ckv_ref, kr_ref, gq_ref, gk_ref, wq_ref, wkv_ref, c_ref, sa_ref, sb_ref, q_out, k_out, v_out):
        cos_t, sin_a, sin_b = c_ref[...], sa_ref[...], sb_ref[...]
        cqn = _rms(cq_ref[...], gq_ref[...]).astype(BF16)
        ckn = _rms(ckv_ref[...], gk_ref[...]).astype(BF16)
        kr_rot = _rope(kr_ref[...], cos_t, sin_a, sin_b).astype(BF16)
        for h in range(B_HEADS):
            lo = h * QK_PAD
            q_out[:, lo:lo + 128] = (_dot_nt(cqn, wq_ref[lo:lo + 128, :]) * ATTN_SCALE).astype(BF16)
            qr = _rope(_dot_nt(cqn, wq_ref[lo + 128:lo + 256, :]), cos_t, sin_a, sin_b)
            q_out[:, lo + 128:lo + 256] = (qr * ATTN_SCALE).astype(BF16)
            k_out[:, lo:lo + 128] = _dot(ckn, wkv_ref[:, lo:lo + 128]).astype(BF16)
            k_out[:, lo + 128:lo + 256] = kr_rot
            v_out[:, h * B_V:(h + 1) * B_V] = _dot(ckn, wkv_ref[:, lo + 128:lo + 256]).astype(BF16)

    tok = lambda wd: pl.BlockSpec((tm, wd), lambda i: (i, 0))
    tab = pl.BlockSpec((tm, 128), lambda i: (i % nblk, 0))
    return pl.pallas_call(
        body, name="mla_qkv_fwd", grid=(t // tm,),
        out_shape=[jax.ShapeDtypeStruct((t, B_HEADS * QK_PAD), BF16), jax.ShapeDtypeStruct((t, B_HEADS * QK_PAD), BF16),
                   jax.ShapeDtypeStruct((t, B_HEADS * B_V), BF16)],
        in_specs=[tok(Q_LORA), tok(KV_LORA), tok(128), _const_spec((1, Q_LORA)), _const_spec((1, KV_LORA)),
                  _const_spec((B_HEADS * QK_PAD, Q_LORA)), _const_spec((KV_LORA, 1024)), tab, tab, tab],
        out_specs=[tok(B_HEADS * QK_PAD), tok(B_HEADS * QK_PAD), tok(B_HEADS * B_V)],
        compiler_params=_params(),
    )(cq, ckv, kr, g_qa, g_kva, w_q, w_kv, *tables)


def _step_index(nq):
    return (pl.program_id(0) * B_HEADS + pl.program_id(1)) * nq + pl.program_id(2)


def _attn_fwd(qcat, kcat, v, nb, seq, tq, gather=()):
    t = qcat.shape[0]
    nq = seq // tq
    ng = len(gather)
    steps = nb * B_HEADS * nq

    def body(q_ref, k_ref, v_ref, *rest):
        o_ref, lse_ref = rest[ng:ng + 2]
        if ng:
            start, forward, finish = _gather_protocol(rest[:ng], rest[ng + 2:2 * ng + 2], *rest[2 * ng + 2:])
            pl.when(_step_index(nq) == 0)(start)
            pl.when(_step_index(nq) == (3 * steps) // 4)(forward)
        for j in range(tq // ATTN_SUB):
            r = pl.ds(j * ATTN_SUB, ATTN_SUB)
            s = _dot_nt(q_ref[r, :], k_ref[...])
            m = jnp.max(s, axis=-1, keepdims=True)
            p = jnp.exp(s - m)
            l = jnp.sum(p, axis=-1, keepdims=True)
            o_ref[r, :] = _dot(p.astype(BF16), v_ref[...]) / l
            lse_ref[0, r, :] = m + jnp.log(l)
        if ng:
            pl.when(_step_index(nq) == steps - 1)(finish)

    any_spec = pl.BlockSpec(memory_space=pl.ANY)
    return pl.pallas_call(
        body, name="attn_fwd", grid=(nb, B_HEADS, nq),
        out_shape=[jax.ShapeDtypeStruct((t, B_HEADS * B_V), F32), jax.ShapeDtypeStruct((B_HEADS, t, 1), F32)] + _slot_shapes(gather),
        in_specs=[pl.BlockSpec((tq, QK_PAD), lambda b, h, i: (b * nq + i, h)),
                  pl.BlockSpec((seq, QK_PAD), lambda b, h, i: (b, h)),
                  pl.BlockSpec((seq, B_V), lambda b, h, i: (b, h))] + [any_spec] * ng,
        out_specs=[pl.BlockSpec((tq, B_V), lambda b, h, i: (b * nq + i, h)),
                   pl.BlockSpec((1, tq, 1), lambda b, h, i: (h, b * nq + i, 0))] + [any_spec] * ng,
        scratch_shapes=_comm_sems(ng) if ng else [],
        compiler_params=_params(),
    )(qcat, kcat, v, *gather)


def _attn_bwd(qcat, kcat, v, o, lse, do, nb, seq, tq, exchange=()):
    t = qcat.shape[0]
    nq = seq // tq
    ne = len(exchange)
    steps = nb * B_HEADS * nq

    def body(q_ref, k_ref, v_ref, o_ref, lse_ref, do_ref, *rest):
        dq_ref, dk_ref, dv_ref = rest[ne:ne + 3]
        if ne:
            start, finish = _exchange_protocol(rest[:ne], rest[ne + 3:2 * ne + 3], [True] * ne, *rest[2 * ne + 3:])
            pl.when(_step_index(nq) == 0)(start)

        @pl.when(pl.program_id(2) == 0)
        def _():
            dv_ref[...] = jnp.zeros_like(dv_ref)
            dk_ref[...] = jnp.zeros_like(dk_ref)

        for j in range(tq // ATTN_SUB_BWD):
            r = pl.ds(j * ATTN_SUB_BWD, ATTN_SUB_BWD)
            q, k = q_ref[r, :], k_ref[...]
            do_f = do_ref[r, :]
            delta = jnp.sum(do_f * o_ref[r, :], axis=-1, keepdims=True)
            dob = do_f.astype(BF16)
            p = jnp.exp(_dot_nt(q, k) - lse_ref[0, r, :])
            ds = (p * (_dot_nt(dob, v_ref[...]) - delta)).astype(BF16)
            dq_ref[r, :] = _dot(ds, k).astype(dq_ref.dtype)
            dv_ref[...] += _dot_tn(p.astype(BF16), dob)
            dk_ref[...] += _dot_tn(ds, q)
        if ne:
            pl.when(_step_index(nq) == steps - 1)(finish)

    qspec = lambda wd: pl.BlockSpec((tq, wd), lambda b, h, i: (b * nq + i, h))
    kspec = lambda wd: pl.BlockSpec((seq, wd), lambda b, h, i: (b, h))
    any_spec = pl.BlockSpec(memory_space=pl.ANY)
    return pl.pallas_call(
        body, name="attn_bwd", grid=(nb, B_HEADS, nq),
        out_shape=[jax.ShapeDtypeStruct((t, B_HEADS * QK_PAD), BF16), jax.ShapeDtypeStruct((t, B_HEADS * QK_PAD), F32),
                   jax.ShapeDtypeStruct((t, B_HEADS * B_V), F32)] + _slot_shapes(exchange, [True] * ne),
        in_specs=[qspec(QK_PAD), kspec(QK_PAD), kspec(B_V), qspec(B_V),
                  pl.BlockSpec((1, tq, 1), lambda b, h, i: (h, b * nq + i, 0)), qspec(B_V)] + [any_spec] * ne,
        out_specs=[qspec(QK_PAD), kspec(QK_PAD), kspec(B_V)] + [any_spec] * ne,
        scratch_shapes=_comm_sems(ne) if ne else [],
        compiler_params=_params(),
    )(qcat, kcat, v, o, lse, do, *exchange)


def _gla_consts(reverse):
    row = lax.broadcasted_iota(jnp.int32, (CHUNK, CHUNK), 0)
    col = lax.broadcasted_iota(jnp.int32, (CHUNK, CHUNK), 1)
    causal = (row <= col) if reverse else (row >= col)
    lane = lax.broadcasted_iota(jnp.int32, (1, HEAD_PAIR), 1)
    m0 = (lane < 64).astype(F32)
    m1 = 1.0 - m0
    r2 = lax.broadcasted_iota(jnp.int32, (HEAD_PAIR, HEAD_PAIR), 0)
    c2 = lax.broadcasted_iota(jnp.int32, (HEAD_PAIR, HEAD_PAIR), 1)
    same_head = ((r2 < 64) == (c2 < 64)).astype(F32)
    return causal, m0, m1, same_head


def _gla_chunk(hq, hi, z, l0, l1, st, consts, reverse):
    q_dec, k_inv, k_end, decay = _gla_gates(hq, z, l0, l1, reverse)
    o, st_new = _gla_state(q_dec, st, decay, _gla_increment(hi, k_end, consts))
    return o + _gla_intra(q_dec, k_inv, hi, consts), st_new


def _gla_gates(hq, z, l0, l1, reverse):
    mx = jnp.maximum(l0, l1)
    e0, e1 = jnp.exp(l0 - mx), jnp.exp(l1 - mx)
    lb = e0 / (e0 + e1)
    q = hq * _sigmoid(hq)
    sz = _sigmoid(z)
    log_f = jnp.log(lb + (1.0 - lb) * sz)
    k = (1.0 - lb) * (1.0 - sz)
    cum = _cumsum_rows(log_f, reverse)
    decay = jnp.exp(jnp.sum(log_f, axis=0, keepdims=True))
    k_inv = k * jnp.exp(-cum)
    return q * jnp.exp(cum), k_inv, k_inv * decay, decay


def _gla_intra(q_dec, k_inv, hi, consts):
    causal, m0, m1, _ = consts
    o = None
    for mh in (m0, m1):
        s = jnp.where(causal, _mm_nt(q_dec * mh, k_inv), 0.0)
        part = _mm(s, hi) * mh
        o = part if o is None else o + part
    return o


def _gla_increment(hi, k_end, consts):
    return _mm_tn(hi, k_end) * consts[3]


def _gla_state(q_dec, st, decay, inc):
    return _mm_nt(q_dec, st), st * decay + inc


GLA_DIRS = (False, True)
GLA_BATCH_FWD = 8
GLA_BATCH_BWD = 4


def _gla_fwd(hq, hi, zs, lbls, nb, seq, group):
    t = hq.shape[0]
    rows = group * CHUNK
    nblk = seq // rows
    n_chunks = seq // CHUNK
    nd = len(GLA_DIRS)

    def body(*refs):
        ins, outs, st_refs = refs[:4 * nd], refs[4 * nd:6 * nd], refs[6 * nd:]
        @pl.when(pl.program_id(2) == 0)
        def _():
            for st_ref in st_refs:
                st_ref[...] = jnp.zeros_like(st_ref)

        consts = [_gla_consts(rev) for rev in GLA_DIRS]
        work = [(d, rev, group - 1 - cc if rev else cc) for cc in range(group) for d, rev in enumerate(GLA_DIRS)]
        rows_of = lambda c: pl.ds(c * CHUNK, CHUNK)
        sts = [st_ref[...] for st_ref in st_refs]
        for w0 in range(0, len(work), GLA_BATCH_FWD):
            batch = work[w0:w0 + GLA_BATCH_FWD]
            gates, intra, incs = {}, {}, {}
            for d, rev, c in batch:
                hq_ref, _, z_ref, lbl_ref = ins[4 * d:4 * d + 4]
                gates[d, c] = _gla_gates(hq_ref[rows_of(c), :], z_ref[rows_of(c), :], lbl_ref[0:1, :], lbl_ref[1:2, :], rev)
            for d, rev, c in batch:
                hi_c = ins[4 * d + 1][rows_of(c), :]
                intra[d, c] = _gla_intra(gates[d, c][0], gates[d, c][1], hi_c, consts[d])
                incs[d, c] = _gla_increment(hi_c, gates[d, c][2], consts[d])
            for d, rev, c in batch:
                outs[nd + d][0, 0, c] = sts[d]
                o_state, sts[d] = _gla_state(gates[d, c][0], sts[d], gates[d, c][3], incs[d, c])
                outs[d][rows_of(c), :] = (intra[d, c] + o_state).astype(outs[d].dtype)
        for st_ref, st in zip(st_refs, sts):
            st_ref[...] = st

    def tb(rev):
        return (lambda i: nblk - 1 - i) if rev else (lambda i: i)

    tok = lambda rev: pl.BlockSpec((rows, HEAD_PAIR), lambda b, p, i: (b * nblk + tb(rev)(i), p))
    lspec = pl.BlockSpec((2, HEAD_PAIR), lambda b, p, i: (0, p))
    sspec = lambda rev: pl.BlockSpec((1, 1, group, HEAD_PAIR, HEAD_PAIR), lambda b, p, i: (b, p, tb(rev)(i), 0, 0))
    args, in_specs = [], []
    for d, rev in enumerate(GLA_DIRS):
        args += [hq, hi, zs[d], lbls[d]]
        in_specs += [tok(rev), tok(rev), tok(rev), lspec]
    return pl.pallas_call(
        body, name="gla_fwd", grid=(nb, 4, nblk),
        out_shape=[jax.ShapeDtypeStruct((t, A_WIDTH), BF16)] * nd
        + [jax.ShapeDtypeStruct((nb, 4, n_chunks, HEAD_PAIR, HEAD_PAIR), F32)] * nd,
        in_specs=in_specs, out_specs=[tok(rev) for rev in GLA_DIRS] + [sspec(rev) for rev in GLA_DIRS],
        scratch_shapes=[pltpu.VMEM((HEAD_PAIR, HEAD_PAIR), F32)] * nd,
        compiler_params=_params(),
    )(*args)


def _gla_bwd(hq, hi, zs, lbls, saved, do, nb, seq, group):
    t = hq.shape[0]
    rows = group * CHUNK
    nblk = seq // rows
    nd = len(GLA_DIRS)

    def body(*refs):
        ins, outs, dst_refs = refs[:6 * nd], refs[6 * nd:10 * nd], refs[10 * nd:]
        dl_refs = outs[3 * nd:]

        @pl.when(pl.program_id(2) == 0)
        def _():
            for dst_ref, dl_ref in zip(dst_refs, dl_refs):
                dst_ref[...] = jnp.zeros_like(dst_ref)
                dl_ref[...] = jnp.zeros_like(dl_ref)

        consts = [_gla_consts(rev) for rev in GLA_DIRS]
        dsts = [dst_ref[...] for dst_ref in dst_refs]
        dls = [[jnp.zeros((1, HEAD_PAIR), F32), jnp.zeros((1, HEAD_PAIR), F32)] for _ in GLA_DIRS]
        work = [(d, rev, cc if rev else group - 1 - cc) for cc in range(group) for d, rev in enumerate(GLA_DIRS)]
        for w0 in range(0, len(work), GLA_BATCH_BWD):
            vjps = {}
            for d, rev, c in work[w0:w0 + GLA_BATCH_BWD]:
                hq_ref, hi_ref, z_ref, lbl_ref, save_ref, _ = ins[6 * d:6 * d + 6]
                r = pl.ds(c * CHUNK, CHUNK)
                fn = functools.partial(_gla_chunk, consts=consts[d], reverse=rev)
                _, vjps[d, c] = jax.vjp(fn, hq_ref[r, :], hi_ref[r, :], z_ref[r, :], lbl_ref[0:1, :], lbl_ref[1:2, :], save_ref[0, 0, c])
            for d, rev, c in work[w0:w0 + GLA_BATCH_BWD]:
                dq_ref, dv_ref, dz_ref = outs[3 * d:3 * d + 3]
                r = pl.ds(c * CHUNK, CHUNK)
                d_hq, d_hi, d_z, d_l0, d_l1, dsts[d] = vjps[d, c]((ins[6 * d + 5][r, :].astype(F32), dsts[d]))
                dq_ref[r, :] = d_hq.astype(dq_ref.dtype)
                dv_ref[r, :] = d_hi.astype(dv_ref.dtype)
                dz_ref[r, :] = d_z.astype(dz_ref.dtype)
                dls[d] = [dls[d][0] + d_l0, dls[d][1] + d_l1]
        for d in range(nd):
            dst_refs[d][...] = dsts[d]
            dl_refs[d][0, 0:1, :] += dls[d][0]
            dl_refs[d][0, 1:2, :] += dls[d][1]

    def tb(rev):
        return (lambda i: i) if rev else (lambda i: nblk - 1 - i)

    tok = lambda rev: pl.BlockSpec((rows, HEAD_PAIR), lambda b, p, i: (b * nblk + tb(rev)(i), p))
    lspec = pl.BlockSpec((2, HEAD_PAIR), lambda b, p, i: (0, p))
    sspec = lambda rev: pl.BlockSpec((1, 1, group, HEAD_PAIR, HEAD_PAIR), lambda b, p, i: (b, p, tb(rev)(i), 0, 0))
    args, in_specs, out_specs = [], [], []
    for d, rev in enumerate(GLA_DIRS):
        args += [hq, hi, zs[d], lbls[d], saved[d], do]
        in_specs += [tok(rev), tok(rev), tok(rev), lspec, sspec(rev), tok(rev)]
        out_specs += [tok(rev)] * 3
    out_specs += [pl.BlockSpec((1, 2, HEAD_PAIR), lambda b, p, i: (b, 0, p))] * nd
    return pl.pallas_call(
        body, name="gla_bwd", grid=(nb, 4, nblk),
        out_shape=[jax.ShapeDtypeStruct((t, A_WIDTH), BF16)] * (3 * nd) + [jax.ShapeDtypeStruct((nb, 2, A_WIDTH), F32)] * nd,
        in_specs=in_specs, out_specs=out_specs,
        scratch_shapes=[pltpu.VMEM((HEAD_PAIR, HEAD_PAIR), F32)] * nd,
        compiler_params=_params(),
    )(*args)


def _head_mean_matrix():
    r = lax.broadcasted_iota(jnp.int32, (A_WIDTH, A_WIDTH), 0) // 64
    c = lax.broadcasted_iota(jnp.int32, (A_WIDTH, A_WIDTH), 1) // 64
    return jnp.where(r == c, 1.0 / 64.0, 0.0).astype(BF16)


def _gla_out(o_f, o_b, hg, g, mean_mat):
    o = o_f + o_b
    ms = _group_mean(o * o, mean_mat)
    return o * lax.rsqrt(ms + EPS) * g * (hg * _sigmoid(hg))


def _gla_combine(o_f, o_b, hg, g, tm):
    t = o_f.shape[0]

    def body(of_ref, ob_ref, hg_ref, g_ref, y_ref):
        y_ref[...] = _gla_out(of_ref[...].astype(F32), ob_ref[...].astype(F32), hg_ref[...], g_ref[...], _head_mean_matrix())

    tok = pl.BlockSpec((tm, A_WIDTH), lambda i: (i, 0))
    return pl.pallas_call(
        body, name="gla_combine_fwd", grid=(t // tm,), out_shape=jax.ShapeDtypeStruct((t, A_WIDTH), F32),
        in_specs=[tok, tok, tok, _const_spec((1, A_WIDTH))], out_specs=tok, compiler_params=_params(),
    )(o_f, o_b, hg, g)


def _gla_combine_bwd(o_f, o_b, hg, g, dy, tm):
    t = o_f.shape[0]

    def body(of_ref, ob_ref, hg_ref, g_ref, dy_ref, do_ref, dhg_ref, dg_ref):
        mean_mat = _head_mean_matrix()
        fn = lambda o, hgv, gv: _gla_out(o, jnp.zeros_like(o), hgv, gv, mean_mat)
        _, vjp = jax.vjp(fn, of_ref[...].astype(F32) + ob_ref[...].astype(F32), hg_ref[...], g_ref[...])
        d_o, d_hg, d_g = vjp(dy_ref[...])
        do_ref[...] = d_o.astype(do_ref.dtype)
        dhg_ref[...] = d_hg.astype(dhg_ref.dtype)

        @pl.when(pl.program_id(0) == 0)
        def _():
            dg_ref[...] = jnp.zeros_like(dg_ref)

        dg_ref[...] += d_g

    tok = pl.BlockSpec((tm, A_WIDTH), lambda i: (i, 0))
    vec = pl.BlockSpec((1, A_WIDTH), lambda i: (0, 0))
    return pl.pallas_call(
        body, name="gla_combine_bwd", grid=(t // tm,),
        out_shape=[jax.ShapeDtypeStruct((t, A_WIDTH), BF16), jax.ShapeDtypeStruct((t, A_WIDTH), BF16),
                   jax.ShapeDtypeStruct((1, A_WIDTH), F32)],
        in_specs=[tok, tok, tok, _const_spec((1, A_WIDTH)), tok], out_specs=[tok, tok, vec], compiler_params=_params(),
    )(o_f, o_b, hg, g, dy)


def _post_fwd(x, ya, oattn, tgt, g_mla, w_out, g2, w_gate, w_up, w_down, g_fin, tm):
    t = x.shape[0]

    def body(x_ref, ya_ref, oa_ref, tgt_ref, gm_ref, wo_ref, g2_ref, wg_ref, wu_ref, wd_ref, gf_ref,
             x1_ref, x2_ref, gate_ref, up_ref, loss_ref):
        part = jnp.zeros((1, 1), F32)
        for j in range(tm // min(tm, ROW_SUB)):
            r = pl.ds(j * min(tm, ROW_SUB), min(tm, ROW_SUB))
            yb = _rms(oa_ref[r, :], gm_ref[...])
            x1 = x_ref[r, :] + _dot(ya_ref[r, :].astype(BF16), wo_ref[0:A_WIDTH, :]) + _dot(yb.astype(BF16), wo_ref[A_WIDTH:, :])
            x1_ref[r, :] = x1
            h2 = _rms(x1, g2_ref[...]).astype(BF16)
            gate, up = _dot_nt(h2, wg_ref[...]), _dot_nt(h2, wu_ref[...])
            gate_ref[r, :] = gate.astype(BF16)
            up_ref[r, :] = up.astype(BF16)
            act = (gate * _sigmoid(gate) * up).astype(BF16)
            x2 = x1 + _dot(act, wd_ref[...])
            x2_ref[r, :] = x2
            err = _rms(x2, gf_ref[...]) - tgt_ref[r, :]
            part = part + 0.5 * jnp.sum(jnp.mean(err * err, axis=-1, keepdims=True), axis=0, keepdims=True)

        @pl.when(pl.program_id(0) == 0)
        def _():
            loss_ref[...] = jnp.zeros_like(loss_ref)

        loss_ref[...] += jnp.broadcast_to(part, loss_ref.shape)

    tok = lambda wd: pl.BlockSpec((tm, wd), lambda i: (i, 0))
    return pl.pallas_call(
        body, name="post_fwd", grid=(t // tm,),
        out_shape=[jax.ShapeDtypeStruct((t, D_MODEL), F32)] * 2 + [jax.ShapeDtypeStruct((t, D_FF), BF16)] * 2
        + [jax.ShapeDtypeStruct((1, 128), F32)],
        in_specs=[tok(D_MODEL), tok(A_WIDTH), tok(512), tok(D_MODEL), _const_spec((1, 512)), _const_spec((D_MODEL, D_MODEL)),
                  _const_spec((1, D_MODEL)), _const_spec((D_FF, D_MODEL)), _const_spec((D_FF, D_MODEL)),
                  _const_spec((D_FF, D_MODEL)), _const_spec((1, D_MODEL))],
        out_specs=[tok(D_MODEL), tok(D_MODEL), tok(D_FF), tok(D_FF), pl.BlockSpec((1, 128), lambda i: (0, 0))],
        compiler_params=_params(),
    )(x, ya, oattn, tgt, g_mla, w_out, g2, w_gate, w_up, w_down, g_fin)


def _post_bwd(x1, x2, gate_b, up_b, ya, oattn, tgt, g_mla, w_out, g2, w_gate, w_up, w_down, g_fin, tm):
    t = x1.shape[0]

    def body(x1_ref, x2_ref, gate_ref, up_ref, ya_ref, oa_ref, tgt_ref, gm_ref, wo_ref, g2_ref, wg_ref, wu_ref, wd_ref, gf_ref,
             dx1_ref, dya_ref, doa_ref, ycat_ref, dx1b_ref, h2_ref, dgate_ref, dup_ref, act_ref, dx2b_ref,
             dgm_ref, dg2_ref, dgf_ref):
        x1, x2 = x1_ref[...], x2_ref[...]
        dy = (_rms(x2, gf_ref[...]) - tgt_ref[...]) * (1.0 / D_MODEL)
        dx2, dgf = _rms_bwd(x2, gf_ref[...], dy)
        dx2b = dx2.astype(BF16)
        dx2b_ref[...] = dx2b
        h2_ref[...] = _rms(x1, g2_ref[...]).astype(BF16)
        gate, up = gate_ref[...].astype(F32), up_ref[...].astype(F32)
        sg = _sigmoid(gate)
        sl = gate * sg
        act_ref[...] = (sl * up).astype(BF16)
        dact = _dot_nt(dx2b, wd_ref[...])
        dup = (dact * sl).astype(BF16)
        dgate = (dact * up * (sg * (1.0 + gate * (1.0 - sg)))).astype(BF16)
        dup_ref[...] = dup
        dgate_ref[...] = dgate
        dh2 = _dot(dgate, wg_ref[...]) + _dot(dup, wu_ref[...])
        dx1n, dg2 = _rms_bwd(x1, g2_ref[...], dh2)
        dx1 = dx2 + dx1n
        dx1_ref[...] = dx1
        dx1b = dx1.astype(BF16)
        dx1b_ref[...] = dx1b
        oa = oa_ref[...]
        ycat_ref[:, 0:A_WIDTH] = ya_ref[...].astype(BF16)
        ycat_ref[:, A_WIDTH:] = _rms(oa, gm_ref[...]).astype(BF16)
        dya_ref[...] = _dot_nt(dx1b, wo_ref[0:A_WIDTH, :])
        doa, dgm = _rms_bwd(oa, gm_ref[...], _dot_nt(dx1b, wo_ref[A_WIDTH:, :]))
        doa_ref[...] = doa

        @pl.when(pl.program_id(0) == 0)
        def _():
            dgm_ref[...] = jnp.zeros_like(dgm_ref)
            dg2_ref[...] = jnp.zeros_like(dg2_ref)
            dgf_ref[...] = jnp.zeros_like(dgf_ref)

        dgm_ref[...] += dgm
        dg2_ref[...] += dg2
        dgf_ref[...] += dgf

    tok = lambda wd: pl.BlockSpec((tm, wd), lambda i: (i, 0))
    vec = lambda wd: pl.BlockSpec((1, wd), lambda i: (0, 0))
    sds = lambda wd, dt: jax.ShapeDtypeStruct((t, wd), dt)
    return pl.pallas_call(
        body, name="post_bwd", grid=(t // tm,),
        out_shape=[sds(D_MODEL, F32), sds(512, F32), sds(512, F32), sds(D_MODEL, BF16), sds(D_MODEL, BF16), sds(D_MODEL, BF16),
                   sds(D_FF, BF16), sds(D_FF, BF16), sds(D_FF, BF16), sds(D_MODEL, BF16),
                   jax.ShapeDtypeStruct((1, 512), F32), jax.ShapeDtypeStruct((1, D_MODEL), F32), jax.ShapeDtypeStruct((1, D_MODEL), F32)],
        in_specs=[tok(D_MODEL), tok(D_MODEL), tok(D_FF), tok(D_FF), tok(512), tok(512), tok(D_MODEL), _const_spec((1, 512)),
                  _const_spec((D_MODEL, D_MODEL)), _const_spec((1, D_MODEL)), _const_spec((D_FF, D_MODEL)),
                  _const_spec((D_FF, D_MODEL)), _const_spec((D_FF, D_MODEL)), _const_spec((1, D_MODEL))],
        out_specs=[tok(D_MODEL), tok(512), tok(512), tok(D_MODEL), tok(D_MODEL), tok(D_MODEL), tok(D_FF), tok(D_FF), tok(D_FF),
                   tok(D_MODEL), vec(512), vec(D_MODEL), vec(D_MODEL)],
        compiler_params=_params(),
    )(x1, x2, gate_b, up_b, ya, oattn, tgt, g_mla, w_out, g2, w_gate, w_up, w_down, g_fin)


def _matmul_tn(a, b, tn, tt, tag, b_cols=None, k_out=None, exchange=()):
    t, k = a.shape
    c0, n = (0, b.shape[1]) if b_cols is None else b_cols
    k_out = k if k_out is None else k_out
    last = t // tt - 1
    ne = len(exchange)
    n_j = n // tn

    def body(a_ref, b_ref, *rest):
        o_ref, acc_ref = rest[ne], rest[2 * ne + 1]
        if ne:
            start, finish = _exchange_protocol(rest[:ne], rest[ne + 1:2 * ne + 1], [True] * ne, *rest[2 * ne + 2:])
            pl.when((pl.program_id(0) == 0) & (pl.program_id(1) == 0))(start)
        part = _dot_tn(a_ref[...], b_ref[...])

        @pl.when(pl.program_id(1) == 0)
        def _():
            acc_ref[...] = part

        @pl.when(pl.program_id(1) > 0)
        def _():
            acc_ref[...] += part

        @pl.when(pl.program_id(1) == last)
        def _():
            o_ref[...] = acc_ref[0:k_out, :].astype(o_ref.dtype)

        if ne:
            pl.when((pl.program_id(0) == n_j - 1) & (pl.program_id(1) == last))(finish)

    any_spec = pl.BlockSpec(memory_space=pl.ANY)
    out = pl.pallas_call(
        body, name="wgrad_" + tag, grid=(n_j, t // tt),
        out_shape=[jax.ShapeDtypeStruct((k_out, n), BF16)] + _slot_shapes(exchange, [True] * ne),
        in_specs=[pl.BlockSpec((tt, k), lambda j, i: (i, 0)), pl.BlockSpec((tt, tn), lambda j, i: (i, j + c0 // tn))]
        + [any_spec] * ne,
        out_specs=[pl.BlockSpec((k_out, tn), lambda j, i: (0, j))] + [any_spec] * ne,
        scratch_shapes=[pltpu.VMEM((k, tn), F32)] + (_comm_sems(ne) if ne else []),
        compiler_params=_params(),
    )(a, b, *exchange)
    return out if ne else out[0]


def _mla_qkv_bwd(cq, ckv, g_qa, g_kva, w_q, w_kv, tables, dq, dk, dv, seq, tm):
    t = cq.shape[0]
    nblk = seq // tm

    def body(cq_ref, ckv_ref, gq_ref, gk_ref, wq_ref, wkv_ref, c_ref, sa_ref, sb_ref, dq_ref, dk_ref, dv_ref,
             dcq_ref, dckv_ref, dkr_ref, cqn_ref, dqf_ref, ckn_ref, dkv_ref, dgq_ref, dgk_ref):
        cos_t, sin_a, sin_b = c_ref[...], sa_ref[...], sb_ref[...]
        cqn_ref[...] = _rms(cq_ref[...], gq_ref[...]).astype(BF16)
        ckn_ref[...] = _rms(ckv_ref[...], gk_ref[...]).astype(BF16)
        dkr = jnp.zeros((tm, 128), F32)
        for h in range(B_HEADS):
            lo = h * QK_PAD
            dqf_ref[:, lo:lo + 128] = (dq_ref[:, lo:lo + 128].astype(F32) * ATTN_SCALE).astype(BF16)
            dq_rope = dq_ref[:, lo + 128:lo + 256].astype(F32) * ATTN_SCALE
            dqf_ref[:, lo + 128:lo + 256] = _rope_t(dq_rope, cos_t, sin_a, sin_b).astype(BF16)
            dkv_ref[:, lo:lo + 128] = dk_ref[:, lo:lo + 128].astype(BF16)
            dkv_ref[:, lo + 128:lo + 256] = dv_ref[:, h * B_V:(h + 1) * B_V].astype(BF16)
            dkr = dkr + dk_ref[:, lo + 128:lo + 256]
        dkr_ref[...] = _rope_t(dkr, cos_t, sin_a, sin_b).astype(dkr_ref.dtype)
        dcq, dgq = _rms_bwd(cq_ref[...], gq_ref[...], _dot(dqf_ref[...], wq_ref[...]))
        dckv, dgk = _rms_bwd(ckv_ref[...], gk_ref[...], _dot_nt(dkv_ref[...], wkv_ref[...]))
        dcq_ref[...] = dcq.astype(dcq_ref.dtype)
        dckv_ref[...] = dckv.astype(dckv_ref.dtype)

        @pl.when(pl.program_id(0) == 0)
        def _():
            dgq_ref[...] = jnp.zeros_like(dgq_ref)
            dgk_ref[...] = jnp.zeros_like(dgk_ref)

        dgq_ref[...] += dgq
        dgk_ref[...] += dgk

    tok = lambda wd: pl.BlockSpec((tm, wd), lambda i: (i, 0))
    vec = lambda wd: pl.BlockSpec((1, wd), lambda i: (0, 0))
    tab = pl.BlockSpec((tm, 128), lambda i: (i % nblk, 0))
    sds = lambda wd, dt: jax.ShapeDtypeStruct((t, wd), dt)
    return pl.pallas_call(
        body, name="mla_qkv_bwd", grid=(t // tm,),
        out_shape=[sds(Q_LORA, BF16), sds(KV_LORA, BF16), sds(128, BF16), sds(Q_LORA, BF16), sds(1024, BF16), sds(KV_LORA, BF16),
                   sds(1024, BF16), jax.ShapeDtypeStruct((1, Q_LORA), F32), jax.ShapeDtypeStruct((1, KV_LORA), F32)],
        in_specs=[tok(Q_LORA), tok(KV_LORA), _const_spec((1, Q_LORA)), _const_spec((1, KV_LORA)),
                  _const_spec((1024, Q_LORA)), _const_spec((KV_LORA, 1024)), tab, tab, tab,
                  tok(1024), tok(1024), tok(512)],
        out_specs=[tok(Q_LORA), tok(KV_LORA), tok(128), tok(Q_LORA), tok(1024), tok(KV_LORA), tok(1024),
                   vec(Q_LORA), vec(KV_LORA)],
        compiler_params=_params(),
    )(cq, ckv, g_qa, g_kva, w_q, w_kv, *tables, dq, dk, dv)


def _inproj_bwd(x, g1, w_in, dx1, pieces, tm):
    t = x.shape[0]
    counts = [len(p) for p in pieces]
    flat = [a for p in pieces for a in p]
    widths = [wd for wd, p in zip(IN_WIDTHS, pieces) for _ in p]

    def body(x_ref, g_ref, w_ref, dx1_ref, *refs):
        ins = refs[:len(flat)]
        dx_ref, h_ref, dp_ref, dg_ref = refs[len(flat):]
        xv = x_ref[...]
        h_ref[...] = _rms(xv, g_ref[...]).astype(BF16)
        off, j = 0, 0
        for wd, cnt in zip(IN_WIDTHS, counts):
            acc = ins[j][...].astype(F32)
            for jj in range(1, cnt):
                acc = acc + ins[j + jj][...].astype(F32)
            dp_ref[:, off:off + wd] = acc.astype(BF16)
            off += wd
            j += cnt
        dxn, dg = _rms_bwd(xv, g_ref[...], _dot(dp_ref[...], w_ref[...]))
        dx_ref[...] = dx1_ref[...] + dxn

        @pl.when(pl.program_id(0) == 0)
        def _():
            dg_ref[...] = jnp.zeros_like(dg_ref)

        dg_ref[...] += dg

    tok = lambda wd: pl.BlockSpec((tm, wd), lambda i: (i, 0))
    return pl.pallas_call(
        body, name="inproj_bwd", grid=(t // tm,),
        out_shape=[jax.ShapeDtypeStruct((t, D_MODEL), F32), jax.ShapeDtypeStruct((t, D_MODEL), BF16),
                   jax.ShapeDtypeStruct((t, D_IN_PAD), BF16), jax.ShapeDtypeStruct((1, D_MODEL), F32)],
        in_specs=[tok(D_MODEL), _const_spec((1, D_MODEL)), _const_spec((D_IN_PAD, D_MODEL)), tok(D_MODEL)] + [tok(wd) for wd in widths],
        out_specs=[tok(D_MODEL), tok(D_MODEL), tok(D_IN_PAD), pl.BlockSpec((1, D_MODEL), lambda i: (0, 0))],
        compiler_params=_params(),
    )(x, g1, w_in, dx1, *flat)


def _cols_from_slots(g):
    n, r, cs = g.shape
    return g.transpose(1, 0, 2).reshape(r, n * cs)


def _cols_to_slots(full):
    r, c = full.shape
    return full.reshape(r, N_DEV, c // N_DEV).transpose(1, 0, 2)


def _arrange_w_in_t(w_in_t):
    return jnp.concatenate([w_in_t, jnp.zeros((D_IN_PAD - D_IN, D_MODEL), w_in_t.dtype)], axis=0)


def _arrange_w_q_t(w_q_t):
    q3 = w_q_t.reshape(B_HEADS, B_NOPE + B_ROPE, Q_LORA)
    pad = jnp.zeros((B_HEADS, QK_PAD - B_NOPE - B_ROPE, Q_LORA), w_q_t.dtype)
    return jnp.concatenate([q3, pad], axis=1).reshape(B_HEADS * QK_PAD, Q_LORA)


def _unarrange_w_q_t(d_q_t):
    return d_q_t.reshape(B_HEADS, QK_PAD, Q_LORA)[:, :B_NOPE + B_ROPE].reshape(B_HEADS * (B_NOPE + B_ROPE), Q_LORA)


def _step_core(x, loss_target, small_w, lb_full, early_full, late, seq, group, tiles, distributed):
    g1, g_hgrn, g_qa, g_kva, g_mla, g2, g_fin = small_w
    w_in, w_q, w_kv = _arrange_w_in_t(early_full[0]), _arrange_w_q_t(early_full[1]), early_full[2]
    nb = x.shape[0]
    t = nb * seq
    tm, tm_fwd, tq_f, tq_b, tt = tiles
    xt = x.reshape(t, D_MODEL)
    tgt = loss_target.reshape(t, D_MODEL)
    tables = _rope_tables(seq)

    hq, hi, zf, zb, hg, cq, ckv, kr = _inproj(xt, g1, w_in, tm_fwd)
    qcat, kcat, vv = _mla_qkv(cq, ckv, kr, g_qa, g_kva, w_q, w_kv, tables, seq, tm_fwd)
    if distributed:
        oattn, lse, *late_slots = _attn_fwd(qcat, kcat, vv, nb, seq, tq_f, gather=tuple(late))
    else:
        oattn, lse = _attn_fwd(qcat, kcat, vv, nb, seq, tq_f)
        late_slots = late
    w_out = late_slots[0].reshape(D_MODEL, D_MODEL)
    w_gate, w_up = late_slots[1].reshape(D_FF, D_MODEL), late_slots[2].reshape(D_FF, D_MODEL)
    w_down = late_slots[3].reshape(D_FF, D_MODEL)
    lbl_f, lbl_b = lb_full[0], lb_full[1]
    o_f, o_b, save_f, save_b = _gla_fwd(hq, hi, (zf, zb), (lbl_f, lbl_b), nb, seq, group)
    ya = _gla_combine(o_f, o_b, hg, g_hgrn, tm_fwd)
    x1, x2, gate_b, up_b, loss_row = _post_fwd(xt, ya, oattn, tgt, g_mla, w_out, g2, w_gate, w_up, w_down, g_fin, tm_fwd)

    (dx1, d_ya, d_oattn, ycat_b, dx1_b, h2_b, dgate_b, dup_b, act_b, dx2_b, d_g_mla, d_g2, d_g_fin) = _post_bwd(
        x1, x2, gate_b, up_b, ya, oattn, tgt, g_mla, w_out, g2, w_gate, w_up, w_down, g_fin, tm)
    d_w_gate = _matmul_tn(dgate_b, h2_b, 512, tt, "gate")
    d_w_up = _matmul_tn(dup_b, h2_b, 512, tt, "up")
    d_w_down = _matmul_tn(act_b, dx2_b, 512, tt, "down")
    d_w_out = _matmul_tn(ycat_b, dx1_b, D_MODEL, tt, "out")
    late_g = [d_w_out.reshape(N_DEV, D_MODEL // N_DEV, D_MODEL)] + [
        g.reshape(N_DEV, D_FF // N_DEV, D_MODEL) for g in (d_w_gate, d_w_up, d_w_down)]
    if distributed:
        dq, dk, dv, *late_g = _attn_bwd(qcat, kcat, vv, oattn, lse, d_oattn, nb, seq, tq_b, exchange=tuple(late_g))
    else:
        dq, dk, dv = _attn_bwd(qcat, kcat, vv, oattn, lse, d_oattn, nb, seq, tq_b)
    (d_cq, d_ckv, d_kr, cqn_b, dqf_b, ckn_b, dkv_b, d_g_qa, d_g_kva) = _mla_qkv_bwd(
        cq, ckv, g_qa, g_kva, w_q, w_kv, tables, dq, dk, dv, seq, tm_fwd)
    d_w_q = _matmul_tn(dqf_b, cqn_b, Q_LORA, tt, "q_b")
    d_w_kv = _matmul_tn(ckn_b, dkv_b, B_HEADS * (B_NOPE + B_V), tt, "kv_b")
    d_o, d_hg, d_g_hgrn = _gla_combine_bwd(o_f, o_b, hg, g_hgrn, d_ya, tm_fwd)
    dq_f, dv_f, dz_f, dq_b, dv_b, dz_b, dl_f, dl_b = _gla_bwd(
        hq, hi, (zf, zb), (lbl_f, lbl_b), (save_f, save_b), d_o, nb, seq, group)
    grad_x, h1_b, dproj_b, d_g1 = _inproj_bwd(
        xt, g1, w_in, dx1, [[dq_f, dq_b], [dv_f, dv_b], [dz_f], [dz_b], [d_hg], [d_cq], [d_ckv], [d_kr]], tm_fwd)
    half = D_MODEL // 2
    in_slots = lambda g: g.reshape(N_DEV, D_IN // N_DEV, half)
    g_in_a = in_slots(_matmul_tn(dproj_b, h1_b, half, tt, "in_a", b_cols=(0, half), k_out=D_IN))
    if distributed:
        d_w_in_b, g_in_a = _matmul_tn(dproj_b, h1_b, half, tt, "in_b", b_cols=(half, half), k_out=D_IN, exchange=(g_in_a,))
    else:
        d_w_in_b = _matmul_tn(dproj_b, h1_b, half, tt, "in_b", b_cols=(half, half), k_out=D_IN)

    early_g = [in_slots(d_w_in_b), _unarrange_w_q_t(d_w_q).reshape(N_DEV, 768 // N_DEV, Q_LORA), _cols_to_slots(d_w_kv)]
    d_lb = jnp.stack([jnp.sum(dl_f, axis=0), jnp.sum(dl_b, axis=0)], axis=0)
    small_grads = [d_g1, d_g_hgrn, d_g_qa, d_g_kva, d_g_mla, d_g2, d_g_fin]
    return loss_row, grad_x.reshape(nb, seq, D_MODEL), g_in_a, early_g, late_g, small_grads, d_lb


def kernel(x, norm1_g, w_in, lb_logits, hgrn_norm_g, q_a_norm_g, w_q_b, kv_a_norm_g, w_kv_b, mla_norm_g, w_out, norm2_g, w_gate, w_up, w_down, final_norm_g, loss_target, m_norm1_g, m_w_in, m_lb_logits, m_hgrn_norm_g, m_q_a_norm_g, m_w_q_b, m_kv_a_norm_g, m_w_kv_b, m_mla_norm_g, m_w_out, m_norm2_g, m_w_gate, m_w_up, m_w_down, m_final_norm_g, v_norm1_g, v_w_in, v_lb_logits, v_hgrn_norm_g, v_q_a_norm_g, v_w_q_b, v_kv_a_norm_g, v_w_kv_b, v_mla_norm_g, v_w_out, v_norm2_g, v_w_gate, v_w_up, v_w_down, v_final_norm_g):
    big_w = [w_in, w_q_b, w_kv_b, w_out, w_gate, w_up, w_down]
    big_m = [m_w_in, m_w_q_b, m_w_kv_b, m_w_out, m_w_gate, m_w_up, m_w_down]
    big_v = [v_w_in, v_w_q_b, v_w_kv_b, v_w_out, v_w_gate, v_w_up, v_w_down]
    small_w = [norm1_g, hgrn_norm_g, q_a_norm_g, kv_a_norm_g, mla_norm_g, norm2_g, final_norm_g]
    small_m = [m_norm1_g, m_hgrn_norm_g, m_q_a_norm_g, m_kv_a_norm_g, m_mla_norm_g, m_norm2_g, m_final_norm_g]
    small_v = [v_norm1_g, v_hgrn_norm_g, v_q_a_norm_g, v_kv_a_norm_g, v_mla_norm_g, v_norm2_g, v_final_norm_g]
    seq = x.shape[1]
    my_id = 4 * lax.axis_index("x") + 2 * lax.axis_index("y") + lax.axis_index("c")

    shard = lambda w: w[0].astype(BF16)
    col_t = lambda w: jnp.swapaxes(w, 1, 2)[0]
    shard_t = lambda w: col_t(w).astype(BF16)
    g_in, g_q, g_kv, g_lb = _all_gather_call([shard_t(w_in), shard_t(w_q_b), shard(w_kv_b), lb_logits.reshape(4, 64)])
    early_full = (g_in.reshape(D_IN, D_MODEL), g_q.reshape(768, Q_LORA), _cols_from_slots(g_kv))
    lb_full = g_lb.reshape(N_DEV, 2, 2, 64).transpose(1, 2, 0, 3).reshape(2, 2, 512)

    as_row = lambda a: a.reshape(1, -1)
    loss_row, grad_x, recv_in_a, early_g, late_recv, small_g, d_lb = _step_core(
        x, loss_target, [as_row(s) for s in small_w], lb_full, early_full,
        [shard(w_out), shard_t(w_gate), shard_t(w_up), shard(w_down)], seq, min(16, seq // CHUNK),
        (256, 512, min(1024, seq), min(1024, seq), min(2048, 2 * seq)), True)

    grads, deltas, new_ms, new_vs = {}, {}, {}, {}
    views = {name: (col_t if name in ("w_in", "w_q_b", "w_gate", "w_up") else (lambda a: a[0])) for name, _, _, _ in BIG}
    backs = {name: ((lambda a: jnp.swapaxes(a[None], 1, 2)) if name in ("w_in", "w_q_b", "w_gate", "w_up") else (lambda a: a[None]))
             for name, _, _, _ in BIG}
    by_name = {name: (w, m, v) for (name, _, _, _), w, m, v in zip(BIG, big_w, big_m, big_v)}
    late_names = ["w_out", "w_gate", "w_up", "w_down"]
    n_small = len(small_g)
    g_l, d_l, nm_l, nv_l, recv = _adamw_recv_hosting(
        [views[n](by_name[n][0]) for n in late_names], list(late_recv), [views[n](by_name[n][1]) for n in late_names],
        [views[n](by_name[n][2]) for n in late_names],
        early_g + small_g + [d_lb.reshape(4, 512), loss_row], [True] * 3 + [False] * (n_small + 2))
    for i, name in enumerate(late_names):
        grads[name], deltas[name], new_ms[name], new_vs[name] = (backs[name](a[i]) for a in (g_l, d_l, nm_l, nv_l))
    sums = _sum_slots_call(recv[3:])
    g_small = [g.reshape(s.shape) for g, s in zip(sums[:n_small], small_w)]
    g_lb_own = lax.dynamic_index_in_dim(sums[n_small].reshape(2, 2, N_DEV, 64), my_id, axis=2, keepdims=False)
    loss = sums[n_small + 1][0, 0]

    for name, r in zip(["w_in", "w_q_b", "w_kv_b"], recv[:3]):
        w, m, v = (views[name](a) for a in by_name[name])
        g, d, nm, nv = _adamw_recv_halves(w, (recv_in_a, r), m, v, name) if name == "w_in" else _adamw_recv(w, r, m, v, name)
        grads[name], deltas[name], new_ms[name], new_vs[name] = (backs[name](a) for a in (g, d, nm, nv))
    lb_rows = lambda a: a.reshape(4, 64)
    d_s, nm_s, nv_s = _adamw_small(
        [as_row(a) for a in small_w] + [lb_rows(lb_logits)], [as_row(a) for a in g_small] + [lb_rows(g_lb_own)],
        [as_row(a) for a in small_m] + [lb_rows(m_lb_logits)], [as_row(a) for a in small_v] + [lb_rows(v_lb_logits)])
    for i, (s, (name, _)) in enumerate(zip(small_w + [lb_logits], SMALL + (("lb_logits", 0),))):
        grads[name] = (g_small + [g_lb_own])[i]
        deltas[name], new_ms[name], new_vs[name] = d_s[i].reshape(s.shape), nm_s[i].reshape(s.shape), nv_s[i].reshape(s.shape)

    order = ["norm1_g", "w_in", "lb_logits", "hgrn_norm_g", "q_a_norm_g", "w_q_b", "kv_a_norm_g", "w_kv_b", "mla_norm_g",
             "w_out", "norm2_g", "w_gate", "w_up", "w_down", "final_norm_g"]
    return (loss, grad_x, *[grads[n] for n in order], *[deltas[n] for n in order],
            *[new_ms[n] for n in order], *[new_vs[n] for n in order])
```

```python
import functools
import math

import jax
import jax.numpy as jnp
from jax import lax
from jax.experimental import pallas as pl
from jax.experimental.pallas import tpu as pltpu

F32 = jnp.float32
BF16 = jnp.bfloat16

N_DEV = 8
D_MODEL = 1024
D_FF = 2816
A_WIDTH = 512
HEAD_PAIR = 128
CHUNK = 64
B_HEADS = 4
B_NOPE = 128
B_ROPE = 64
B_V = 128
QK_PAD = 256
Q_LORA = 384
KV_LORA = 256
D_IN = 3264
D_IN_PAD = 3328
IN_WIDTHS = (512, 512, 512, 512, 512, Q_LORA, KV_LORA, 128)
ROPE_THETA = 10000.0
EPS = 1e-6
ATTN_SCALE = (B_NOPE + B_ROPE) ** -0.5
ATTN_SUB = 256
ATTN_SUB_BWD = 256
ROW_SUB = 256
ADAM_LR, ADAM_B1, ADAM_B2, ADAM_EPS, ADAM_WD, ADAM_STEP = 0.001, 0.9, 0.999, 1e-08, 0.01, 10
VMEM_LIMIT = 60 * 1024 * 1024
MESH = pl.DeviceIdType.MESH

BIG = (("w_in", 1024, D_IN, 1), ("w_q_b", Q_LORA, 768, 1), ("w_kv_b", KV_LORA, 1024, 1), ("w_out", 1024, 1024, 0),
       ("w_gate", 1024, D_FF, 1), ("w_up", 1024, D_FF, 1), ("w_down", D_FF, 1024, 0))
SMALL = (("norm1_g", 1024), ("hgrn_norm_g", 512), ("q_a_norm_g", 384), ("kv_a_norm_g", 256), ("mla_norm_g", 512),
         ("norm2_g", 1024), ("final_norm_g", 1024))


def _params(**kw):
    return pltpu.CompilerParams(vmem_limit_bytes=VMEM_LIMIT, **kw)


def _const_spec(shape):
    return pl.BlockSpec(shape, lambda *_: (0,) * len(shape), pipeline_mode=pl.Buffered(1))


def _dot(a, b):
    return jnp.dot(a, b, preferred_element_type=F32)


def _dot_nt(a, b):
    return lax.dot_general(a, b, (((1,), (1,)), ((), ())), preferred_element_type=F32)


def _dot_tn(a, b):
    return lax.dot_general(a, b, (((0,), (0,)), ((), ())), preferred_element_type=F32)


@jax.custom_vjp
def _mm(a, b):
    return _dot(a.astype(BF16), b.astype(BF16))


def _mm_fwd(a, b):
    return _mm(a, b), (a, b)


def _mm_bwd(res, g):
    a, b = res
    gb = g.astype(BF16)
    return _dot_nt(gb, b.astype(BF16)), _dot_tn(a.astype(BF16), gb)


_mm.defvjp(_mm_fwd, _mm_bwd)


@jax.custom_vjp
def _mm_nt(a, b):
    return _dot_nt(a.astype(BF16), b.astype(BF16))


def _mm_nt_fwd(a, b):
    return _mm_nt(a, b), (a, b)


def _mm_nt_bwd(res, g):
    a, b = res
    gb = g.astype(BF16)
    return _dot(gb, b.astype(BF16)), _dot_tn(gb, a.astype(BF16))


_mm_nt.defvjp(_mm_nt_fwd, _mm_nt_bwd)


@jax.custom_vjp
def _mm_tn(a, b):
    return _dot_tn(a.astype(BF16), b.astype(BF16))


def _mm_tn_fwd(a, b):
    return _mm_tn(a, b), (a, b)


def _mm_tn_bwd(res, g):
    a, b = res
    gb = g.astype(BF16)
    return _dot_nt(b.astype(BF16), gb), _dot(a.astype(BF16), gb)


_mm_tn.defvjp(_mm_tn_fwd, _mm_tn_bwd)


def _dot_exact_rhs(a, m):
    hi = a.astype(BF16)
    lo = (a - hi.astype(F32)).astype(BF16)
    return _dot(hi, m) + _dot(lo, m)


@jax.custom_vjp
def _group_mean(a, m):
    return _dot_exact_rhs(a, m)


def _group_mean_fwd(a, m):
    return _group_mean(a, m), m


def _group_mean_bwd(m, g):
    return _dot_exact_rhs(g, m), jnp.zeros_like(m)


_group_mean.defvjp(_group_mean_fwd, _group_mean_bwd)


def _roll_rows(a, shift):
    return pltpu.roll(a, shift, 0)


def _cumsum_rows_raw(a, reverse):
    n = a.shape[0]
    row = lax.broadcasted_iota(jnp.int32, a.shape, 0)
    s = 1
    while s < n:
        if reverse:
            a = a + jnp.where(row < n - s, _roll_rows(a, n - s), 0.0)
        else:
            a = a + jnp.where(row >= s, _roll_rows(a, s), 0.0)
        s *= 2
    return a


@functools.partial(jax.custom_vjp, nondiff_argnums=(1,))
def _cumsum_rows(a, reverse):
    return _cumsum_rows_raw(a, reverse)


def _cumsum_rows_fwd(a, reverse):
    return _cumsum_rows_raw(a, reverse), None


def _cumsum_rows_bwd(reverse, _, g):
    return (_cumsum_rows_raw(g, not reverse),)


_cumsum_rows.defvjp(_cumsum_rows_fwd, _cumsum_rows_bwd)


def _rms(x, g):
    r = lax.rsqrt(jnp.mean(x * x, axis=-1, keepdims=True) + EPS)
    return x * r * g


def _rms_bwd(x, g, dy):
    r = lax.rsqrt(jnp.mean(x * x, axis=-1, keepdims=True) + EPS)
    xh = x * r
    dg = jnp.sum(dy * xh, axis=0, keepdims=True)
    dxh = dy * g
    dx = r * (dxh - xh * jnp.mean(dxh * xh, axis=-1, keepdims=True))
    return dx, dg


def _sigmoid(a):
    return jax.nn.sigmoid(a)


def _mesh_place():
    x, y, c = lax.axis_index("x"), lax.axis_index("y"), lax.axis_index("c")
    return x, y, c


def _dev_index(p):
    return 4 * p[0] + 2 * p[1] + p[2]


def _comm_sems(n):
    return [pltpu.SemaphoreType.DMA((n, 7)), pltpu.SemaphoreType.DMA((n, 7)), pltpu.SemaphoreType.DMA((n,))]


def _gather_protocol(ins, outs, send_sems, recv_sems, local_sems):
    n = len(ins)
    x, y, c = _mesh_place()
    me, sibling = (x, y, c), (x, y, 1 - c)
    chips = [(1 - x, y), (x, 1 - y), (1 - x, 1 - y)]

    def copy(a, k, block, to, src=None):
        slot = outs[a].at[_dev_index(block)]
        return pltpu.make_async_remote_copy(
            src_ref=slot if src is None else src, dst_ref=slot,
            send_sem=send_sems.at[a, k], recv_sem=recv_sems.at[a, k], device_id=to, device_id_type=MESH)

    def mine(a):
        return pltpu.make_async_copy(ins[a], outs[a].at[_dev_index(me)], local_sems.at[a])

    def first(a):
        return [copy(a, 0, me, sibling, src=ins[a])] + [copy(a, 1 + j, me, (*chip, c), src=ins[a]) for j, chip in enumerate(chips)]

    def start():
        for a in range(n):
            mine(a).start()
            for cp in first(a):
                cp.start()

    def forward():
        for a in range(n):
            for j, chip in enumerate(chips):
                copy(a, 1 + j, (*chip, c), me).wait_recv()
                copy(a, 4 + j, (*chip, c), sibling).start()

    def finish():
        for a in range(n):
            copy(a, 0, sibling, me).wait_recv()
            for j, chip in enumerate(chips):
                copy(a, 4 + j, (*chip, 1 - c), me).wait_recv()
        for a in range(n):
            mine(a).wait()
            for cp in first(a):
                cp.wait_send()
            for j, chip in enumerate(chips):
                copy(a, 4 + j, (*chip, c), sibling).wait_send()

    return start, forward, finish


def _exchange_protocol(ins, outs, scatter, send_sems, recv_sems, local_sems):
    n = len(ins)
    x, y, c = _mesh_place()
    me = (x, y, c)
    my_id = _dev_index(me)
    rels = [(dx, dy, dc) for dx in (0, 1) for dy in (0, 1) for dc in (0, 1)][1:]

    def peer_of(rel):
        return tuple(1 - v if d else v for v, d in zip(me, rel))

    def src(a, dev):
        return ins[a].at[dev] if scatter[a] else ins[a]

    def send(a, k):
        peer = peer_of(rels[k])
        return pltpu.make_async_remote_copy(
            src_ref=src(a, _dev_index(peer)), dst_ref=outs[a].at[my_id],
            send_sem=send_sems.at[a, k], recv_sem=recv_sems.at[a, k], device_id=peer, device_id_type=MESH)

    def arrival(a, k):
        peer = peer_of(rels[k])
        return pltpu.make_async_remote_copy(
            src_ref=src(a, my_id), dst_ref=outs[a].at[_dev_index(peer)],
            send_sem=send_sems.at[a, k], recv_sem=recv_sems.at[a, k], device_id=peer, device_id_type=MESH)

    def own(a):
        return pltpu.make_async_copy(src(a, my_id), outs[a].at[my_id], local_sems.at[a])

    def start():
        for a in range(n):
            own(a).start()
            for k in range(7):
                send(a, k).start()

    def finish():
        for a in range(n):
            for k in range(7):
                arrival(a, k).wait_recv()
        for a in range(n):
            for k in range(7):
                send(a, k).wait_send()
            own(a).wait()

    return start, finish


def _slot_shapes(blocks, scatter=None):
    return [jax.ShapeDtypeStruct(b.shape if (scatter and scatter[a]) else (N_DEV,) + b.shape, b.dtype) for a, b in enumerate(blocks)]


def _all_gather_call(blocks):
    n = len(blocks)

    def body(*refs):
        start, forward, finish = _gather_protocol(refs[:n], refs[n:2 * n], *refs[2 * n:])
        start()
        forward()
        finish()

    any_spec = pl.BlockSpec(memory_space=pl.ANY)
    return pl.pallas_call(
        body, name="weights_all_gather", out_shape=_slot_shapes(blocks),
        in_specs=[any_spec] * n, out_specs=[any_spec] * n, scratch_shapes=_comm_sems(n),
    )(*blocks)


def _sum_slots_call(recvs):
    n = len(recvs)

    def body(*refs):
        for in_ref, out_ref in zip(refs[:n], refs[n:]):
            acc = in_ref[0]
            for j in range(1, N_DEV):
                acc = acc + in_ref[j]
            out_ref[...] = acc

    return pl.pallas_call(
        body, name="small_grad_sum", out_shape=[jax.ShapeDtypeStruct(r.shape[1:], F32) for r in recvs],
        compiler_params=_params(),
    )(*recvs)


def _adam_update(w, g, m, v):
    nm = ADAM_B1 * m + (1.0 - ADAM_B1) * g
    nv = ADAM_B2 * v + (1.0 - ADAM_B2) * (g * g)
    bc1 = 1.0 - ADAM_B1 ** ADAM_STEP
    bc2 = 1.0 - ADAM_B2 ** ADAM_STEP
    return -ADAM_LR * ((nm / bc1) / (jnp.sqrt(nv / bc2) + ADAM_EPS) + ADAM_WD * w), nm, nv


def _adamw_recv(w, recv, m, v, tag):
    r, c = w.shape
    tr = r
    for cand in (512, 256, 128):
        if r > cand and r % cand == 0:
            tr = cand
            break

    def body(w_ref, r_ref, m_ref, v_ref, g_ref, d_ref, nm_ref, nv_ref):
        g = r_ref[0].astype(F32)
        for j in range(1, N_DEV):
            g = g + r_ref[j].astype(F32)
        g_ref[...] = g
        d_ref[...], nm_ref[...], nv_ref[...] = _adam_update(w_ref[...], g, m_ref[...], v_ref[...])

    spec = pl.BlockSpec((tr, c), lambda i: (i, 0))
    return pl.pallas_call(
        body, name="adamw_" + tag, out_shape=[jax.ShapeDtypeStruct(w.shape, F32)] * 4, grid=(r // tr,),
        in_specs=[spec, pl.BlockSpec((N_DEV, tr, c), lambda i: (0, i, 0)), spec, spec], out_specs=[spec] * 4,
        compiler_params=_params(),
    )(w, recv, m, v)


def _adamw_recv_halves(w, recv_halves, m, v, tag):
    r, c = w.shape
    half = c // 2

    def body(w_ref, ra_ref, rb_ref, m_ref, v_ref, g_ref, d_ref, nm_ref, nv_ref):
        def update(r_ref):
            g = r_ref[0].astype(F32)
            for j in range(1, N_DEV):
                g = g + r_ref[j].astype(F32)
            g_ref[...] = g
            d_ref[...], nm_ref[...], nv_ref[...] = _adam_update(w_ref[...], g, m_ref[...], v_ref[...])

        pl.when(pl.program_id(0) == 0)(lambda: update(ra_ref))
        pl.when(pl.program_id(0) == 1)(lambda: update(rb_ref))

    spec = pl.BlockSpec((r, half), lambda j: (0, j))
    whole = pl.BlockSpec((N_DEV, r, half), lambda j: (0, 0, 0))
    return pl.pallas_call(
        body, name="adamw_" + tag, out_shape=[jax.ShapeDtypeStruct(w.shape, F32)] * 4, grid=(2,),
        in_specs=[spec, whole, whole, spec, spec], out_specs=[spec] * 4, compiler_params=_params(),
    )(w, *recv_halves, m, v)


def _adamw_recv_hosting(ws, recvs, ms, vs, blocks, scatter):
    n, ne = len(ws), len(blocks)
    rows = max(w.shape[0] for w in ws)
    cols = ws[0].shape[1]
    assert all(w.shape[1] == cols for w in ws)

    def body(*refs):
        ins, ex_in = refs[:4 * n], refs[4 * n:4 * n + ne]
        outs, ex_out = refs[4 * n + ne:8 * n + ne], refs[8 * n + ne:8 * n + 2 * ne]
        in_buf, recv_buf, out_buf, in_sems, out_sems = refs[8 * n + 2 * ne:8 * n + 2 * ne + 5]
        start, finish = _exchange_protocol(ex_in, ex_out, scatter, *refs[8 * n + 2 * ne + 5:])
        start()
        for a in range(n):
            r = pl.ds(0, ws[a].shape[0])
            loads = [pltpu.make_async_copy(ins[k * n + a], in_buf.at[j, r], in_sems.at[j]) for j, k in enumerate((0, 2, 3))]
            loads.append(pltpu.make_async_copy(ins[n + a], recv_buf.at[:, r], in_sems.at[3]))
            for cp in loads:
                cp.start()
            for cp in loads:
                cp.wait()
            g = recv_buf[0, r].astype(F32)
            for j in range(1, N_DEV):
                g = g + recv_buf[j, r].astype(F32)
            out_buf[0, r] = g
            out_buf[1, r], out_buf[2, r], out_buf[3, r] = _adam_update(in_buf[0, r], g, in_buf[1, r], in_buf[2, r])
            stores = [pltpu.make_async_copy(out_buf.at[k, r], outs[k * n + a], out_sems.at[k]) for k in range(4)]
            for cp in stores:
                cp.start()
            for cp in stores:
                cp.wait()
        finish()

    any_spec = pl.BlockSpec(memory_space=pl.ANY)
    out = pl.pallas_call(
        body, name="adamw_late_and_grad_exchange",
        out_shape=[jax.ShapeDtypeStruct(w.shape, F32) for w in ws] * 4 + _slot_shapes(blocks, scatter),
        in_specs=[any_spec] * (4 * n + ne), out_specs=[any_spec] * (4 * n + ne),
        scratch_shapes=[pltpu.VMEM((3, rows, cols), F32), pltpu.VMEM((N_DEV, rows, cols), BF16), pltpu.VMEM((4, rows, cols), F32),
                        pltpu.SemaphoreType.DMA((4,)), pltpu.SemaphoreType.DMA((4,))] + _comm_sems(ne),
        compiler_params=_params(),
    )(*ws, *recvs, *ms, *vs, *blocks)
    return out[:n], out[n:2 * n], out[2 * n:3 * n], out[3 * n:4 * n], out[4 * n:]


def _adamw_small(ws, gs, ms, vs):
    n = len(ws)

    def body(*refs):
        ins, outs = refs[:4 * n], refs[4 * n:]
        for a in range(n):
            d, nm, nv = _adam_update(ins[a][...], ins[n + a][...], ins[2 * n + a][...], ins[3 * n + a][...])
            outs[a][...], outs[n + a][...], outs[2 * n + a][...] = d, nm, nv

    out = pl.pallas_call(
        body, name="adamw_small", out_shape=[jax.ShapeDtypeStruct(w.shape, F32) for w in ws] * 3, compiler_params=_params(),
    )(*ws, *gs, *ms, *vs)
    return out[:n], out[n:2 * n], out[2 * n:]


def _tile(t, want):
    return want if t % want == 0 else t


def _inproj(x, g1, w_in, tm):
    t = x.shape[0]

    def body(x_ref, g_ref, w_ref, *outs):
        for j in range(tm // min(tm, ROW_SUB)):
            r = pl.ds(j * min(tm, ROW_SUB), min(tm, ROW_SUB))
            h = _rms(x_ref[r, :], g_ref[...]).astype(BF16)
            off = 0
            for o_ref, wd in zip(outs, IN_WIDTHS):
                o_ref[r, :] = _dot_nt(h, w_ref[off:off + wd, :])
                off += wd

    return pl.pallas_call(
        body, name="inproj_fwd", grid=(t // tm,),
        out_shape=[jax.ShapeDtypeStruct((t, wd), F32) for wd in IN_WIDTHS],
        in_specs=[pl.BlockSpec((tm, D_MODEL), lambda i: (i, 0)), _const_spec((1, D_MODEL)), _const_spec((D_IN_PAD, D_MODEL))],
        out_specs=[pl.BlockSpec((tm, wd), lambda i: (i, 0)) for wd in IN_WIDTHS],
        compiler_params=_params(),
    )(x, g1, w_in)


def _rope_tables(seq):
    inv = 1.0 / (ROPE_THETA ** (jnp.arange(0, B_ROPE, 2, dtype=F32) / B_ROPE))
    ang = jnp.arange(seq, dtype=F32)[:, None] * inv[None, :]
    cos, sin = jnp.cos(ang), jnp.sin(ang)
    z32, z64 = jnp.zeros_like(cos), jnp.zeros((seq, 64), F32)
    cos_t = jnp.concatenate([cos, cos, z64], axis=1)
    sin_a = jnp.concatenate([-sin, z32, z64], axis=1)
    sin_b = jnp.concatenate([z32, sin, z64], axis=1)
    return cos_t, sin_a, sin_b


def _rope(t, cos_t, sin_a, sin_b):
    return t * cos_t + pltpu.roll(t, 96, 1) * sin_a + pltpu.roll(t, 32, 1) * sin_b


def _rope_t(d, cos_t, sin_a, sin_b):
    return d * cos_t + pltpu.roll(d * sin_a, 32, 1) + pltpu.roll(d * sin_b, 96, 1)


def _mla_qkv(cq, ckv, kr, g_qa, g_kva, w_q, w_kv, tables, seq, tm):
    t = cq.shape[0]
    nblk = seq // tm

    def body(cq_ref, ckv_ref, kr_ref, gq_ref, gk_ref, wq_ref, wkv_ref, c_ref, sa_ref, sb_ref, q_out, k_out, v_out):
        cos_t, sin_a, sin_b = c_ref[...], sa_ref[...], sb_ref[...]
        cqn = _rms(cq_ref[...], gq_ref[...]).astype(BF16)
        ckn = _rms(ckv_ref[...], gk_ref[...]).astype(BF16)
        kr_rot = _rope(kr_ref[...], cos_t, sin_a, sin_b).astype(BF16)
        for h in range(B_HEADS):
            lo = h * QK_PAD
            q_out[:, lo:lo + 128] = (_dot_nt(cqn, wq_ref[lo:lo + 128, :]) * ATTN_SCALE).astype(BF16)
            qr = _rope(_dot_nt(cqn, wq_ref[lo + 128:lo + 256, :]), cos_t, sin_a, sin_b)
            q_out[:, lo + 128:lo + 256] = (qr * ATTN_SCALE).astype(BF16)
            k_out[:, lo:lo + 128] = _dot(ckn, wkv_ref[:, lo:lo + 128]).astype(BF16)
            k_out[:, lo + 128:lo + 256] = kr_rot
            v_out[:, h * B_V:(h + 1) * B_V] = _dot(ckn, wkv_ref[:, lo + 128:lo + 256]).astype(BF16)

    tok = lambda wd: pl.BlockSpec((tm, wd), lambda i: (i, 0))
    tab = pl.BlockSpec((tm, 128), lambda i: (i % nblk, 0))
    return pl.pallas_call(
        body, name="mla_qkv_fwd", grid=(t // tm,),
        out_shape=[jax.ShapeDtypeStruct((t, B_HEADS * QK_PAD), BF16), jax.ShapeDtypeStruct((t, B_HEADS * QK_PAD), BF16),
                   jax.ShapeDtypeStruct((t, B_HEADS * B_V), BF16)],
        in_specs=[tok(Q_LORA), tok(KV_LORA), tok(128), _const_spec((1, Q_LORA)), _const_spec((1, KV_LORA)),
                  _const_spec((B_HEADS * QK_PAD, Q_LORA)), _const_spec((KV_LORA, 1024)), tab, tab, tab],
        out_specs=[tok(B_HEADS * QK_PAD), tok(B_HEADS * QK_PAD), tok(B_HEADS * B_V)],
        compiler_params=_params(),
    )(cq, ckv, kr, g_qa, g_kva, w_q, w_kv, *tables)


def _step_index(nq):
    return (pl.program_id(0) * B_HEADS + pl.program_id(1)) * nq + pl.program_id(2)


def _attn_fwd(qcat, kcat, v, nb, seq, tq, gather=()):
    t = qcat.shape[0]
    nq = seq // tq
    ng = len(gather)
    steps = nb * B_HEADS * nq

    def body(q_ref, k_ref, v_ref, *rest):
        o_ref, lse_ref = rest[ng:ng + 2]
        if ng:
            start, forward, finish = _gather_protocol(rest[:ng], rest[ng + 2:2 * ng + 2], *rest[2 * ng + 2:])
            pl.when(_step_index(nq) == 0)(start)
            pl.when(_step_index(nq) == (3 * steps) // 4)(forward)
        for j in range(tq // ATTN_SUB):
            r = pl.ds(j * ATTN_SUB, ATTN_SUB)
            s = _dot_nt(q_ref[r, :], k_ref[...])
            m = jnp.max(s, axis=-1, keepdims=True)
            p = jnp.exp(s - m)
            l = jnp.sum(p, axis=-1, keepdims=True)
            o_ref[r, :] = _dot(p.astype(BF16), v_ref[...]) / l
            lse_ref[0, r, :] = m + jnp.log(l)
        if ng:
            pl.when(_step_index(nq) == steps - 1)(finish)

    any_spec = pl.BlockSpec(memory_space=pl.ANY)
    return pl.pallas_call(
        body, name="attn_fwd", grid=(nb, B_HEADS, nq),
        out_shape=[jax.ShapeDtypeStruct((t, B_HEADS * B_V), F32), jax.ShapeDtypeStruct((B_HEADS, t, 1), F32)] + _slot_shapes(gather),
        in_specs=[pl.BlockSpec((tq, QK_PAD), lambda b, h, i: (b * nq + i, h)),
                  pl.BlockSpec((seq, QK_PAD), lambda b, h, i: (b, h)),
                  pl.BlockSpec((seq, B_V), lambda b, h, i: (b, h))] + [any_spec] * ng,
        out_specs=[pl.BlockSpec((tq, B_V), lambda b, h, i: (b * nq + i, h)),
                   pl.BlockSpec((1, tq, 1), lambda b, h, i: (h, b * nq + i, 0))] + [any_spec] * ng,
        scratch_shapes=_comm_sems(ng) if ng else [],
        compiler_params=_params(),
    )(qcat, kcat, v, *gather)


def _attn_bwd(qcat, kcat, v, o, lse, do, nb, seq, tq, exchange=()):
    t = qcat.shape[0]
    nq = seq // tq
    ne = len(exchange)
    steps = nb * B_HEADS * nq

    def body(q_ref, k_ref, v_ref, o_ref, lse_ref, do_ref, *rest):
        dq_ref, dk_ref, dv_ref = rest[ne:ne + 3]
        if ne:
            start, finish = _exchange_protocol(rest[:ne], rest[ne + 3:2 * ne + 3], [True] * ne, *rest[2 * ne + 3:])
            pl.when(_step_index(nq) == 0)(start)

        @pl.when(pl.program_id(2) == 0)
        def _():
            dv_ref[...] = jnp.zeros_like(dv_ref)
            dk_ref[...] = jnp.zeros_like(dk_ref)

        for j in range(tq // ATTN_SUB_BWD):
            r = pl.ds(j * ATTN_SUB_BWD, ATTN_SUB_BWD)
            q, k = q_ref[r, :], k_ref[...]
            do_f = do_ref[r, :]
            delta = jnp.sum(do_f * o_ref[r, :], axis=-1, keepdims=True)
            dob = do_f.astype(BF16)
            p = jnp.exp(_dot_nt(q, k) - lse_ref[0, r, :])
            ds = (p * (_dot_nt(dob, v_ref[...]) - delta)).astype(BF16)
            dq_ref[r, :] = _dot(ds, k).astype(dq_ref.dtype)
            dv_ref[...] += _dot_tn(p.astype(BF16), dob)
            dk_ref[...] += _dot_tn(ds, q)
        if ne:
            pl.when(_step_index(nq) == steps - 1)(finish)

    qspec = lambda wd: pl.BlockSpec((tq, wd), lambda b, h, i: (b * nq + i, h))
    kspec = lambda wd: pl.BlockSpec((seq, wd), lambda b, h, i: (b, h))
    any_spec = pl.BlockSpec(memory_space=pl.ANY)
    return pl.pallas_call(
        body, name="attn_bwd", grid=(nb, B_HEADS, nq),
        out_shape=[jax.ShapeDtypeStruct((t, B_HEADS * QK_PAD), BF16), jax.ShapeDtypeStruct((t, B_HEADS * QK_PAD), F32),
                   jax.ShapeDtypeStruct((t, B_HEADS * B_V), F32)] + _slot_shapes(exchange, [True] * ne),
        in_specs=[qspec(QK_PAD), kspec(QK_PAD), kspec(B_V), qspec(B_V),
                  pl.BlockSpec((1, tq, 1), lambda b, h, i: (h, b * nq + i, 0)), qspec(B_V)] + [any_spec] * ne,
        out_specs=[qspec(QK_PAD), kspec(QK_PAD), kspec(B_V)] + [any_spec] * ne,
        scratch_shapes=_comm_sems(ne) if ne else [],
        compiler_params=_params(),
    )(qcat, kcat, v, o, lse, do, *exchange)


def _gla_consts(reverse):
    row = lax.broadcasted_iota(jnp.int32, (CHUNK, CHUNK), 0)
    col = lax.broadcasted_iota(jnp.int32, (CHUNK, CHUNK), 1)
    causal = (row <= col) if reverse else (row >= col)
    lane = lax.broadcasted_iota(jnp.int32, (1, HEAD_PAIR), 1)
    m0 = (lane < 64).astype(F32)
    m1 = 1.0 - m0
    r2 = lax.broadcasted_iota(jnp.int32, (HEAD_PAIR, HEAD_PAIR), 0)
    c2 = lax.broadcasted_iota(jnp.int32, (HEAD_PAIR, HEAD_PAIR), 1)
    same_head = ((r2 < 64) == (c2 < 64)).astype(F32)
    return causal, m0, m1, same_head


def _gla_chunk(hq, hi, z, l0, l1, st, consts, reverse):
    q_dec, k_inv, k_end, decay = _gla_gates(hq, z, l0, l1, reverse)
    o, st_new = _gla_state(q_dec, st, decay, _gla_increment(hi, k_end, consts))
    return o + _gla_intra(q_dec, k_inv, hi, consts), st_new


def _gla_gates(hq, z, l0, l1, reverse):
    mx = jnp.maximum(l0, l1)
    e0, e1 = jnp.exp(l0 - mx), jnp.exp(l1 - mx)
    lb = e0 / (e0 + e1)
    q = hq * _sigmoid(hq)
    sz = _sigmoid(z)
    log_f = jnp.log(lb + (1.0 - lb) * sz)
    k = (1.0 - lb) * (1.0 - sz)
    cum = _cumsum_rows(log_f, reverse)
    decay = jnp.exp(jnp.sum(log_f, axis=0, keepdims=True))
    k_inv = k * jnp.exp(-cum)
    return q * jnp.exp(cum), k_inv, k_inv * decay, decay


def _gla_intra(q_dec, k_inv, hi, consts):
    causal, m0, m1, _ = consts
    o = None
    for mh in (m0, m1):
        s = jnp.where(causal, _mm_nt(q_dec * mh, k_inv), 0.0)
        part = _mm(s, hi) * mh
        o = part if o is None else o + part
    return o


def _gla_increment(hi, k_end, consts):
    return _mm_tn(hi, k_end) * consts[3]


def _gla_state(q_dec, st, decay, inc):
    return _mm_nt(q_dec, st), st * decay + inc


GLA_DIRS = (False, True)
GLA_BATCH_FWD = 8
GLA_BATCH_BWD = 4


def _gla_fwd(hq, hi, zs, lbls, nb, seq, group):
    t = hq.shape[0]
    rows = group * CHUNK
    nblk = seq // rows
    n_chunks = seq // CHUNK
    nd = len(GLA_DIRS)

    def body(*refs):
        ins, outs, st_refs = refs[:4 * nd], refs[4 * nd:6 * nd], refs[6 * nd:]
        @pl.when(pl.program_id(2) == 0)
        def _():
            for st_ref in st_refs:
                st_ref[...] = jnp.zeros_like(st_ref)

        consts = [_gla_consts(rev) for rev in GLA_DIRS]
        work = [(d, rev, group - 1 - cc if rev else cc) for cc in range(group) for d, rev in enumerate(GLA_DIRS)]
        rows_of = lambda c: pl.ds(c * CHUNK, CHUNK)
        sts = [st_ref[...] for st_ref in st_refs]
        for w0 in range(0, len(work), GLA_BATCH_FWD):
            batch = work[w0:w0 + GLA_BATCH_FWD]
            gates, intra, incs = {}, {}, {}
            for d, rev, c in batch:
                hq_ref, _, z_ref, lbl_ref = ins[4 * d:4 * d + 4]
                gates[d, c] = _gla_gates(hq_ref[rows_of(c), :], z_ref[rows_of(c), :], lbl_ref[0:1, :], lbl_ref[1:2, :], rev)
            for d, rev, c in batch:
                hi_c = ins[4 * d + 1][rows_of(c), :]
                intra[d, c] = _gla_intra(gates[d, c][0], gates[d, c][1], hi_c, consts[d])
                incs[d, c] = _gla_increment(hi_c, gates[d, c][2], consts[d])
            for d, rev, c in batch:
                outs[nd + d][0, 0, c] = sts[d]
                o_state, sts[d] = _gla_state(gates[d, c][0], sts[d], gates[d, c][3], incs[d, c])
                outs[d][rows_of(c), :] = (intra[d, c] + o_state).astype(outs[d].dtype)
        for st_ref, st in zip(st_refs, sts):
            st_ref[...] = st

    def tb(rev):
        return (lambda i: nblk - 1 - i) if rev else (lambda i: i)

    tok = lambda rev: pl.BlockSpec((rows, HEAD_PAIR), lambda b, p, i: (b * nblk + tb(rev)(i), p))
    lspec = pl.BlockSpec((2, HEAD_PAIR), lambda b, p, i: (0, p))
    sspec = lambda rev: pl.BlockSpec((1, 1, group, HEAD_PAIR, HEAD_PAIR), lambda b, p, i: (b, p, tb(rev)(i), 0, 0))
    args, in_specs = [], []
    for d, rev in enumerate(GLA_DIRS):
        args += [hq, hi, zs[d], lbls[d]]
        in_specs += [tok(rev), tok(rev), tok(rev), lspec]
    return pl.pallas_call(
        body, name="gla_fwd", grid=(nb, 4, nblk),
        out_shape=[jax.ShapeDtypeStruct((t, A_WIDTH), BF16)] * nd
        + [jax.ShapeDtypeStruct((nb, 4, n_chunks, HEAD_PAIR, HEAD_PAIR), F32)] * nd,
        in_specs=in_specs, out_specs=[tok(rev) for rev in GLA_DIRS] + [sspec(rev) for rev in GLA_DIRS],
        scratch_shapes=[pltpu.VMEM((HEAD_PAIR, HEAD_PAIR), F32)] * nd,
        compiler_params=_params(),
    )(*args)


def _gla_bwd(hq, hi, zs, lbls, saved, do, nb, seq, group):
    t = hq.shape[0]
    rows = group * CHUNK
    nblk = seq // rows
    nd = len(GLA_DIRS)

    def body(*refs):
        ins, outs, dst_refs = refs[:6 * nd], refs[6 * nd:10 * nd], refs[10 * nd:]
        dl_refs = outs[3 * nd:]

        @pl.when(pl.program_id(2) == 0)
        def _():
            for dst_ref, dl_ref in zip(dst_refs, dl_refs):
                dst_ref[...] = jnp.zeros_like(dst_ref)
                dl_ref[...] = jnp.zeros_like(dl_ref)

        consts = [_gla_consts(rev) for rev in GLA_DIRS]
        dsts = [dst_ref[...] for dst_ref in dst_refs]
        dls = [[jnp.zeros((1, HEAD_PAIR), F32), jnp.zeros((1, HEAD_PAIR), F32)] for _ in GLA_DIRS]
        work = [(d, rev, cc if rev else group - 1 - cc) for cc in range(group) for d, rev in enumerate(GLA_DIRS)]
        for w0 in range(0, len(work), GLA_BATCH_BWD):
            vjps = {}
            for d, rev, c in work[w0:w0 + GLA_BATCH_BWD]:
                hq_ref, hi_ref, z_ref, lbl_ref, save_ref, _ = ins[6 * d:6 * d + 6]
                r = pl.ds(c * CHUNK, CHUNK)
                fn = functools.partial(_gla_chunk, consts=consts[d], reverse=rev)
                _, vjps[d, c] = jax.vjp(fn, hq_ref[r, :], hi_ref[r, :], z_ref[r, :], lbl_ref[0:1, :], lbl_ref[1:2, :], save_ref[0, 0, c])
            for d, rev, c in work[w0:w0 + GLA_BATCH_BWD]:
                dq_ref, dv_ref, dz_ref = outs[3 * d:3 * d + 3]
                r = pl.ds(c * CHUNK, CHUNK)
                d_hq, d_hi, d_z, d_l0, d_l1, dsts[d] = vjps[d, c]((ins[6 * d + 5][r, :].astype(F32), dsts[d]))
                dq_ref[r, :] = d_hq.astype(dq_ref.dtype)
                dv_ref[r, :] = d_hi.astype(dv_ref.dtype)
                dz_ref[r, :] = d_z.astype(dz_ref.dtype)
                dls[d] = [dls[d][0] + d_l0, dls[d][1] + d_l1]
        for d in range(nd):
            dst_refs[d][...] = dsts[d]
            dl_refs[d][0, 0:1, :] += dls[d][0]
            dl_refs[d][0, 1:2, :] += dls[d][1]

    def tb(rev):
        return (lambda i: i) if rev else (lambda i: nblk - 1 - i)

    tok = lambda rev: pl.BlockSpec((rows, HEAD_PAIR), lambda b, p, i: (b * nblk + tb(rev)(i), p))
    lspec = pl.BlockSpec((2, HEAD_PAIR), lambda b, p, i: (0, p))
    sspec = lambda rev: pl.BlockSpec((1, 1, group, HEAD_PAIR, HEAD_PAIR), lambda b, p, i: (b, p, tb(rev)(i), 0, 0))
    args, in_specs, out_specs = [], [], []
    for d, rev in enumerate(GLA_DIRS):
        args += [hq, hi, zs[d], lbls[d], saved[d], do]
        in_specs += [tok(rev), tok(rev), tok(rev), lspec, sspec(rev), tok(rev)]
        out_specs += [tok(rev)] * 3
    out_specs += [pl.BlockSpec((1, 2, HEAD_PAIR), lambda b, p, i: (b, 0, p))] * nd
    return pl.pallas_call(
        body, name="gla_bwd", grid=(nb, 4, nblk),
        out_shape=[jax.ShapeDtypeStruct((t, A_WIDTH), BF16)] * (3 * nd) + [jax.ShapeDtypeStruct((nb, 2, A_WIDTH), F32)] * nd,
        in_specs=in_specs, out_specs=out_specs,
        scratch_shapes=[pltpu.VMEM((HEAD_PAIR, HEAD_PAIR), F32)] * nd,
        compiler_params=_params(),
    )(*args)


def _head_mean_matrix():
    r = lax.broadcasted_iota(jnp.int32, (A_WIDTH, A_WIDTH), 0) // 64
    c = lax.broadcasted_iota(jnp.int32, (A_WIDTH, A_WIDTH), 1) // 64
    return jnp.where(r == c, 1.0 / 64.0, 0.0).astype(BF16)


def _gla_out(o_f, o_b, hg, g, mean_mat):
    o = o_f + o_b
    ms = _group_mean(o * o, mean_mat)
    return o * lax.rsqrt(ms + EPS) * g * (hg * _sigmoid(hg))


def _gla_combine(o_f, o_b, hg, g, tm):
    t = o_f.shape[0]

    def body(of_ref, ob_ref, hg_ref, g_ref, y_ref):
        y_ref[...] = _gla_out(of_ref[...].astype(F32), ob_ref[...].astype(F32), hg_ref[...], g_ref[...], _head_mean_matrix())

    tok = pl.BlockSpec((tm, A_WIDTH), lambda i: (i, 0))
    return pl.pallas_call(
        body, name="gla_combine_fwd", grid=(t // tm,), out_shape=jax.ShapeDtypeStruct((t, A_WIDTH), F32),
        in_specs=[tok, tok, tok, _const_spec((1, A_WIDTH))], out_specs=tok, compiler_params=_params(),
    )(o_f, o_b, hg, g)


def _gla_combine_bwd(o_f, o_b, hg, g, dy, tm):
    t = o_f.shape[0]

    def body(of_ref, ob_ref, hg_ref, g_ref, dy_ref, do_ref, dhg_ref, dg_ref):
        mean_mat = _head_mean_matrix()
        fn = lambda o, hgv, gv: _gla_out(o, jnp.zeros_like(o), hgv, gv, mean_mat)
        _, vjp = jax.vjp(fn, of_ref[...].astype(F32) + ob_ref[...].astype(F32), hg_ref[...], g_ref[...])
        d_o, d_hg, d_g = vjp(dy_ref[...])
        do_ref[...] = d_o.astype(do_ref.dtype)
        dhg_ref[...] = d_hg.astype(dhg_ref.dtype)

        @pl.when(pl.program_id(0) == 0)
        def _():
            dg_ref[...] = jnp.zeros_like(dg_ref)

        dg_ref[...] += d_g

    tok = pl.BlockSpec((tm, A_WIDTH), lambda i: (i, 0))
    vec = pl.BlockSpec((1, A_WIDTH), lambda i: (0, 0))
    return pl.pallas_call(
        body, name="gla_combine_bwd", grid=(t // tm,),
        out_shape=[jax.ShapeDtypeStruct((t, A_WIDTH), BF16), jax.ShapeDtypeStruct((t, A_WIDTH), BF16),
                   jax.ShapeDtypeStruct((1, A_WIDTH), F32)],
        in_specs=[tok, tok, tok, _const_spec((1, A_WIDTH)), tok], out_specs=[tok, tok, vec], compiler_params=_params(),
    )(o_f, o_b, hg, g, dy)


def _post_fwd(x, ya, oattn, tgt, g_mla, w_out, g2, w_gate, w_up, w_down, g_fin, tm):
    t = x.shape[0]

    def body(x_ref, ya_ref, oa_ref, tgt_ref, gm_ref, wo_ref, g2_ref, wg_ref, wu_ref, wd_ref, gf_ref,
             x1_ref, x2_ref, gate_ref, up_ref, loss_ref):
        part = jnp.zeros((1, 1), F32)
        for j in range(tm // min(tm, ROW_SUB)):
            r = pl.ds(j * min(tm, ROW_SUB), min(tm, ROW_SUB))
            yb = _rms(oa_ref[r, :], gm_ref[...])
            x1 = x_ref[r, :] + _dot(ya_ref[r, :].astype(BF16), wo_ref[0:A_WIDTH, :]) + _dot(yb.astype(BF16), wo_ref[A_WIDTH:, :])
            x1_ref[r, :] = x1
            h2 = _rms(x1, g2_ref[...]).astype(BF16)
            gate, up = _dot_nt(h2, wg_ref[...]), _dot_nt(h2, wu_ref[...])
            gate_ref[r, :] = gate.astype(BF16)
            up_ref[r, :] = up.astype(BF16)
            act = (gate * _sigmoid(gate) * up).astype(BF16)
            x2 = x1 + _dot(act, wd_ref[...])
            x2_ref[r, :] = x2
            err = _rms(x2, gf_ref[...]) - tgt_ref[r, :]
            part = part + 0.5 * jnp.sum(jnp.mean(err * err, axis=-1, keepdims=True), axis=0, keepdims=True)

        @pl.when(pl.program_id(0) == 0)
        def _():
            loss_ref[...] = jnp.zeros_like(loss_ref)

        loss_ref[...] += jnp.broadcast_to(part, loss_ref.shape)

    tok = lambda wd: pl.BlockSpec((tm, wd), lambda i: (i, 0))
    return pl.pallas_call(
        body, name="post_fwd", grid=(t // tm,),
        out_shape=[jax.ShapeDtypeStruct((t, D_MODEL), F32)] * 2 + [jax.ShapeDtypeStruct((t, D_FF), BF16)] * 2
        + [jax.ShapeDtypeStruct((1, 128), F32)],
        in_specs=[tok(D_MODEL), tok(A_WIDTH), tok(512), tok(D_MODEL), _const_spec((1, 512)), _const_spec((D_MODEL, D_MODEL)),
                  _const_spec((1, D_MODEL)), _const_spec((D_FF, D_MODEL)), _const_spec((D_FF, D_MODEL)),
                  _const_spec((D_FF, D_MODEL)), _const_spec((1, D_MODEL))],
        out_specs=[tok(D_MODEL), tok(D_MODEL), tok(D_FF), tok(D_FF), pl.BlockSpec((1, 128), lambda i: (0, 0))],
        compiler_params=_params(),
    )(x, ya, oattn, tgt, g_mla, w_out, g2, w_gate, w_up, w_down, g_fin)


def _post_bwd(x1, x2, gate_b, up_b, ya, oattn, tgt, g_mla, w_out, g2, w_gate, w_up, w_down, g_fin, tm):
    t = x1.shape[0]

    def body(x1_ref, x2_ref, gate_ref, up_ref, ya_ref, oa_ref, tgt_ref, gm_ref, wo_ref, g2_ref, wg_ref, wu_ref, wd_ref, gf_ref,
             dx1_ref, dya_ref, doa_ref, ycat_ref, dx1b_ref, h2_ref, dgate_ref, dup_ref, act_ref, dx2b_ref,
             dgm_ref, dg2_ref, dgf_ref):
        x1, x2 = x1_ref[...], x2_ref[...]
        dy = (_rms(x2, gf_ref[...]) - tgt_ref[...]) * (1.0 / D_MODEL)
        dx2, dgf = _rms_bwd(x2, gf_ref[...], dy)
        dx2b = dx2.astype(BF16)
        dx2b_ref[...] = dx2b
        h2_ref[...] = _rms(x1, g2_ref[...]).astype(BF16)
        gate, up = gate_ref[...].astype(F32), up_ref[...].astype(F32)
        sg = _sigmoid(gate)
        sl = gate * sg
        act_ref[...] = (sl * up).astype(BF16)
        dact = _dot_nt(dx2b, wd_ref[...])
        dup = (dact * sl).astype(BF16)
        dgate = (dact * up * (sg * (1.0 + gate * (1.0 - sg)))).astype(BF16)
        dup_ref[...] = dup
        dgate_ref[...] = dgate
        dh2 = _dot(dgate, wg_ref[...]) + _dot(dup, wu_ref[...])
        dx1n, dg2 = _rms_bwd(x1, g2_ref[...], dh2)
        dx1 = dx2 + dx1n
        dx1_ref[...] = dx1
        dx1b = dx1.astype(BF16)
        dx1b_ref[...] = dx1b
        oa = oa_ref[...]
        ycat_ref[:, 0:A_WIDTH] = ya_ref[...].astype(BF16)
        ycat_ref[:, A_WIDTH:] = _rms(oa, gm_ref[...]).astype(BF16)
        dya_ref[...] = _dot_nt(dx1b, wo_ref[0:A_WIDTH, :])
        doa, dgm = _rms_bwd(oa, gm_ref[...], _dot_nt(dx1b, wo_ref[A_WIDTH:, :]))
        doa_ref[...] = doa

        @pl.when(pl.program_id(0) == 0)
        def _():
            dgm_ref[...] = jnp.zeros_like(dgm_ref)
            dg2_ref[...] = jnp.zeros_like(dg2_ref)
            dgf_ref[...] = jnp.zeros_like(dgf_ref)

        dgm_ref[...] += dgm
        dg2_ref[...] += dg2
        dgf_ref[...] += dgf

    tok = lambda wd: pl.BlockSpec((tm, wd), lambda i: (i, 0))
    vec = lambda wd: pl.BlockSpec((1, wd), lambda i: (0, 0))
    sds = lambda wd, dt: jax.ShapeDtypeStruct((t, wd), dt)
    return pl.pallas_call(
        body, name="post_bwd", grid=(t // tm,),
        out_shape=[sds(D_MODEL, F32), sds(512, F32), sds(512, F32), sds(D_MODEL, BF16), sds(D_MODEL, BF16), sds(D_MODEL, BF16),
                   sds(D_FF, BF16), sds(D_FF, BF16), sds(D_FF, BF16), sds(D_MODEL, BF16),
                   jax.ShapeDtypeStruct((1, 512), F32), jax.ShapeDtypeStruct((1, D_MODEL), F32), jax.ShapeDtypeStruct((1, D_MODEL), F32)],
        in_specs=[tok(D_MODEL), tok(D_MODEL), tok(D_FF), tok(D_FF), tok(512), tok(512), tok(D_MODEL), _const_spec((1, 512)),
                  _const_spec((D_MODEL, D_MODEL)), _const_spec((1, D_MODEL)), _const_spec((D_FF, D_MODEL)),
                  _const_spec((D_FF, D_MODEL)), _const_spec((D_FF, D_MODEL)), _const_spec((1, D_MODEL))],
        out_specs=[tok(D_MODEL), tok(512), tok(512), tok(D_MODEL), tok(D_MODEL), tok(D_MODEL), tok(D_FF), tok(D_FF), tok(D_FF),
                   tok(D_MODEL), vec(512), vec(D_MODEL), vec(D_MODEL)],
        compiler_params=_params(),
    )(x1, x2, gate_b, up_b, ya, oattn, tgt, g_mla, w_out, g2, w_gate, w_up, w_down, g_fin)


def _matmul_tn(a, b, tn, tt, tag, b_cols=None, k_out=None, exchange=()):
    t, k = a.shape
    c0, n = (0, b.shape[1]) if b_cols is None else b_cols
    k_out = k if k_out is None else k_out
    last = t // tt - 1
    ne = len(exchange)
    n_j = n // tn

    def body(a_ref, b_ref, *rest):
        o_ref, acc_ref = rest[ne], rest[2 * ne + 1]
        if ne:
            start, finish = _exchange_protocol(rest[:ne], rest[ne + 1:2 * ne + 1], [True] * ne, *rest[2 * ne + 2:])
            pl.when((pl.program_id(0) == 0) & (pl.program_id(1) == 0))(start)
        part = _dot_tn(a_ref[...], b_ref[...])

        @pl.when(pl.program_id(1) == 0)
        def _():
            acc_ref[...] = part

        @pl.when(pl.program_id(1) > 0)
        def _():
            acc_ref[...] += part

        @pl.when(pl.program_id(1) == last)
        def _():
            o_ref[...] = acc_ref[0:k_out, :].astype(o_ref.dtype)

        if ne:
            pl.when((pl.program_id(0) == n_j - 1) & (pl.program_id(1) == last))(finish)

    any_spec = pl.BlockSpec(memory_space=pl.ANY)
    out = pl.pallas_call(
        body, name="wgrad_" + tag, grid=(n_j, t // tt),
        out_shape=[jax.ShapeDtypeStruct((k_out, n), BF16)] + _slot_shapes(exchange, [True] * ne),
        in_specs=[pl.BlockSpec((tt, k), lambda j, i: (i, 0)), pl.BlockSpec((tt, tn), lambda j, i: (i, j + c0 // tn))]
        + [any_spec] * ne,
        out_specs=[pl.BlockSpec((k_out, tn), lambda j, i: (0, j))] + [any_spec] * ne,
        scratch_shapes=[pltpu.VMEM((k, tn), F32)] + (_comm_sems(ne) if ne else []),
        compiler_params=_params(),
    )(a, b, *exchange)
    return out if ne else out[0]


def _mla_qkv_bwd(cq, ckv, g_qa, g_kva, w_q, w_kv, tables, dq, dk, dv, seq, tm):
    t = cq.shape[0]
    nblk = seq // tm

    def body(cq_ref, ckv_ref, gq_ref, gk_ref, wq_ref, wkv_ref, c_ref, sa_ref, sb_ref, dq_ref, dk_ref, dv_ref,
             dcq_ref, dckv_ref, dkr_ref, cqn_ref, dqf_ref, ckn_ref, dkv_ref, dgq_ref, dgk_ref):
        cos_t, sin_a, sin_b = c_ref[...], sa_ref[...], sb_ref[...]
        cqn_ref[...] = _rms(cq_ref[...], gq_ref[...]).astype(BF16)
        ckn_ref[...] = _rms(ckv_ref[...], gk_ref[...]).astype(BF16)
        dkr = jnp.zeros((tm, 128), F32)
        for h in range(B_HEADS):
            lo = h * QK_PAD
            dqf_ref[:, lo:lo + 128] = (dq_ref[:, lo:lo + 128].astype(F32) * ATTN_SCALE).astype(BF16)
            dq_rope = dq_ref[:, lo + 128:lo + 256].astype(F32) * ATTN_SCALE
            dqf_ref[:, lo + 128:lo + 256] = _rope_t(dq_rope, cos_t, sin_a, sin_b).astype(BF16)
            dkv_ref[:, lo:lo + 128] = dk_ref[:, lo:lo + 128].astype(BF16)
            dkv_ref[:, lo + 128:lo + 256] = dv_ref[:, h * B_V:(h + 1) * B_V].astype(BF16)
            dkr = dkr + dk_ref[:, lo + 128:lo + 256]
        dkr_ref[...] = _rope_t(dkr, cos_t, sin_a, sin_b).astype(dkr_ref.dtype)
        dcq, dgq = _rms_bwd(cq_ref[...], gq_ref[...], _dot(dqf_ref[...], wq_ref[...]))
        dckv, dgk = _rms_bwd(ckv_ref[...], gk_ref[...], _dot_nt(dkv_ref[...], wkv_ref[...]))
        dcq_ref[...] = dcq.astype(dcq_ref.dtype)
        dckv_ref[...] = dckv.astype(dckv_ref.dtype)

        @pl.when(pl.program_id(0) == 0)
        def _():
            dgq_ref[...] = jnp.zeros_like(dgq_ref)
            dgk_ref[...] = jnp.zeros_like(dgk_ref)

        dgq_ref[...] += dgq
        dgk_ref[...] += dgk

    tok = lambda wd: pl.BlockSpec((tm, wd), lambda i: (i, 0))
    vec = lambda wd: pl.BlockSpec((1, wd), lambda i: (0, 0))
    tab = pl.BlockSpec((tm, 128), lambda i: (i % nblk, 0))
    sds = lambda wd, dt: jax.ShapeDtypeStruct((t, wd), dt)
    return pl.pallas_call(
        body, name="mla_qkv_bwd", grid=(t // tm,),
        out_shape=[sds(Q_LORA, BF16), sds(KV_LORA, BF16), sds(128, BF16), sds(Q_LORA, BF16), sds(1024, BF16), sds(KV_LORA, BF16),
                   sds(1024, BF16), jax.ShapeDtypeStruct((1, Q_LORA), F32), jax.ShapeDtypeStruct((1, KV_LORA), F32)],
        in_specs=[tok(Q_LORA), tok(KV_LORA), _const_spec((1, Q_LORA)), _const_spec((1, KV_LORA)),
                  _const_spec((1024, Q_LORA)), _const_spec((KV_LORA, 1024)), tab, tab, tab,
                  tok(1024), tok(1024), tok(512)],
        out_specs=[tok(Q_LORA), tok(KV_LORA), tok(128), tok(Q_LORA), tok(1024), tok(KV_LORA), tok(1024),
                   vec(Q_LORA), vec(KV_LORA)],
        compiler_params=_params(),
    )(cq, ckv, g_qa, g_kva, w_q, w_kv, *tables, dq, dk, dv)


def _inproj_bwd(x, g1, w_in, dx1, pieces, tm):
    t = x.shape[0]
    counts = [len(p) for p in pieces]
    flat = [a for p in pieces for a in p]
    widths = [wd for wd, p in zip(IN_WIDTHS, pieces) for _ in p]

    def body(x_ref, g_ref, w_ref, dx1_ref, *refs):
        ins = refs[:len(flat)]
        dx_ref, h_ref, dp_ref, dg_ref = refs[len(flat):]
        xv = x_ref[...]
        h_ref[...] = _rms(xv, g_ref[...]).astype(BF16)
        off, j = 0, 0
        for wd, cnt in zip(IN_WIDTHS, counts):
            acc = ins[j][...].astype(F32)
            for jj in range(1, cnt):
                acc = acc + ins[j + jj][...].astype(F32)
            dp_ref[:, off:off + wd] = acc.astype(BF16)
            off += wd
            j += cnt
        dxn, dg = _rms_bwd(xv, g_ref[...], _dot(dp_ref[...], w_ref[...]))
        dx_ref[...] = dx1_ref[...] + dxn

        @pl.when(pl.program_id(0) == 0)
        def _():
            dg_ref[...] = jnp.zeros_like(dg_ref)

        dg_ref[...] += dg

    tok = lambda wd: pl.BlockSpec((tm, wd), lambda i: (i, 0))
    return pl.pallas_call(
        body, name="inproj_bwd", grid=(t // tm,),
        out_shape=[jax.ShapeDtypeStruct((t, D_MODEL), F32), jax.ShapeDtypeStruct((t, D_MODEL), BF16),
                   jax.ShapeDtypeStruct((t, D_IN_PAD), BF16), jax.ShapeDtypeStruct((1, D_MODEL), F32)],
        in_specs=[tok(D_MODEL), _const_spec((1, D_MODEL)), _const_spec((D_IN_PAD, D_MODEL)), tok(D_MODEL)] + [tok(wd) for wd in widths],
        out_specs=[tok(D_MODEL), tok(D_MODEL), tok(D_IN_PAD), pl.BlockSpec((1, D_MODEL), lambda i: (0, 0))],
        compiler_params=_params(),
    )(x, g1, w_in, dx1, *flat)


def _cols_from_slots(g):
    n, r, cs = g.shape
    return g.transpose(1, 0, 2).reshape(r, n * cs)


def _cols_to_slots(full):
    r, c = full.shape
    return full.reshape(r, N_DEV, c // N_DEV).transpose(1, 0, 2)


def _arrange_w_in_t(w_in_t):
    return jnp.concatenate([w_in_t, jnp.zeros((D_IN_PAD - D_IN, D_MODEL), w_in_t.dtype)], axis=0)


def _arrange_w_q_t(w_q_t):
    q3 = w_q_t.reshape(B_HEADS, B_NOPE + B_ROPE, Q_LORA)
    pad = jnp.zeros((B_HEADS, QK_PAD - B_NOPE - B_ROPE, Q_LORA), w_q_t.dtype)
    return jnp.concatenate([q3, pad], axis=1).reshape(B_HEADS * QK_PAD, Q_LORA)


def _unarrange_w_q_t(d_q_t):
    return d_q_t.reshape(B_HEADS, QK_PAD, Q_LORA)[:, :B_NOPE + B_ROPE].reshape(B_HEADS * (B_NOPE + B_ROPE), Q_LORA)


def _step_core(x, loss_target, small_w, lb_full, early_full, late, seq, group, tiles, distributed):
    g1, g_hgrn, g_qa, g_kva, g_mla, g2, g_fin = small_w
    w_in, w_q, w_kv = _arrange_w_in_t(early_full[0]), _arrange_w_q_t(early_full[1]), early_full[2]
    nb = x.shape[0]
    t = nb * seq
    tm, tm_fwd, tq_f, tq_b, tt = tiles
    xt = x.reshape(t, D_MODEL)
    tgt = loss_target.reshape(t, D_MODEL)
    tables = _rope_tables(seq)

    hq, hi, zf, zb, hg, cq, ckv, kr = _inproj(xt, g1, w_in, tm_fwd)
    qcat, kcat, vv = _mla_qkv(cq, ckv, kr, g_qa, g_kva, w_q, w_kv, tables, seq, tm_fwd)
    if distributed:
        oattn, lse, *late_slots = _attn_fwd(qcat, kcat, vv, nb, seq, tq_f, gather=tuple(late))
    else:
        oattn, lse = _attn_fwd(qcat, kcat, vv, nb, seq, tq_f)
        late_slots = late
    w_out = late_slots[0].reshape(D_MODEL, D_MODEL)
    w_gate, w_up = late_slots[1].reshape(D_FF, D_MODEL), late_slots[2].reshape(D_FF, D_MODEL)
    w_down = late_slots[3].reshape(D_FF, D_MODEL)
    lbl_f, lbl_b = lb_full[0], lb_full[1]
    o_f, o_b, save_f, save_b = _gla_fwd(hq, hi, (zf, zb), (lbl_f, lbl_b), nb, seq, group)
    ya = _gla_combine(o_f, o_b, hg, g_hgrn, tm_fwd)
    x1, x2, gate_b, up_b, loss_row = _post_fwd(xt, ya, oattn, tgt, g_mla, w_out, g2, w_gate, w_up, w_down, g_fin, tm_fwd)

    (dx1, d_ya, d_oattn, ycat_b, dx1_b, h2_b, dgate_b, dup_b, act_b, dx2_b, d_g_mla, d_g2, d_g_fin) = _post_bwd(
        x1, x2, gate_b, up_b, ya, oattn, tgt, g_mla, w_out, g2, w_gate, w_up, w_down, g_fin, tm)
    d_w_gate = _matmul_tn(dgate_b, h2_b, 512, tt, "gate")
    d_w_up = _matmul_tn(dup_b, h2_b, 512, tt, "up")
    d_w_down = _matmul_tn(act_b, dx2_b, 512, tt, "down")
    d_w_out = _matmul_tn(ycat_b, dx1_b, D_MODEL, tt, "out")
    late_g = [d_w_out.reshape(N_DEV, D_MODEL // N_DEV, D_MODEL)] + [
        g.reshape(N_DEV, D_FF // N_DEV, D_MODEL) for g in (d_w_gate, d_w_up, d_w_down)]
    if distributed:
        dq, dk, dv, *late_g = _attn_bwd(qcat, kcat, vv, oattn, lse, d_oattn, nb, seq, tq_b, exchange=tuple(late_g))
    else:
        dq, dk, dv = _attn_bwd(qcat, kcat, vv, oattn, lse, d_oattn, nb, seq, tq_b)
    (d_cq, d_ckv, d_kr, cqn_b, dqf_b, ckn_b, dkv_b, d_g_qa, d_g_kva) = _mla_qkv_bwd(
        cq, ckv, g_qa, g_kva, w_q, w_kv, tables, dq, dk, dv, seq, tm_fwd)
    d_w_q = _matmul_tn(dqf_b, cqn_b, Q_LORA, tt, "q_b")
    d_w_kv = _matmul_tn(ckn_b, dkv_b, B_HEADS * (B_NOPE + B_V), tt, "kv_b")
    d_o, d_hg, d_g_hgrn = _gla_combine_bwd(o_f, o_b, hg, g_hgrn, d_ya, tm_fwd)
    dq_f, dv_f, dz_f, dq_b, dv_b, dz_b, dl_f, dl_b = _gla_bwd(
        hq, hi, (zf, zb), (lbl_f, lbl_b), (save_f, save_b), d_o, nb, seq, group)
    grad_x, h1_b, dproj_b, d_g1 = _inproj_bwd(
        xt, g1, w_in, dx1, [[dq_f, dq_b], [dv_f, dv_b], [dz_f], [dz_b], [d_hg], [d_cq], [d_ckv], [d_kr]], tm_fwd)
    half = D_MODEL // 2
    in_slots = lambda g: g.reshape(N_DEV, D_IN // N_DEV, half)
    g_in_a = in_slots(_matmul_tn(dproj_b, h1_b, half, tt, "in_a", b_cols=(0, half), k_out=D_IN))
    if distributed:
        d_w_in_b, g_in_a = _matmul_tn(dproj_b, h1_b, half, tt, "in_b", b_cols=(half, half), k_out=D_IN, exchange=(g_in_a,))
    else:
        d_w_in_b = _matmul_tn(dproj_b, h1_b, half, tt, "in_b", b_cols=(half, half), k_out=D_IN)

    early_g = [in_slots(d_w_in_b), _unarrange_w_q_t(d_w_q).reshape(N_DEV, 768 // N_DEV, Q_LORA), _cols_to_slots(d_w_kv)]
    d_lb = jnp.stack([jnp.sum(dl_f, axis=0), jnp.sum(dl_b, axis=0)], axis=0)
    small_grads = [d_g1, d_g_hgrn, d_g_qa, d_g_kva, d_g_mla, d_g2, d_g_fin]
    return loss_row, grad_x.reshape(nb, seq, D_MODEL), g_in_a, early_g, late_g, small_grads, d_lb


def kernel(x, norm1_g, w_in, lb_logits, hgrn_norm_g, q_a_norm_g, w_q_b, kv_a_norm_g, w_kv_b, mla_norm_g, w_out, norm2_g, w_gate, w_up, w_down, final_norm_g, loss_target, m_norm1_g, m_w_in, m_lb_logits, m_hgrn_norm_g, m_q_a_norm_g, m_w_q_b, m_kv_a_norm_g, m_w_kv_b, m_mla_norm_g, m_w_out, m_norm2_g, m_w_gate, m_w_up, m_w_down, m_final_norm_g, v_norm1_g, v_w_in, v_lb_logits, v_hgrn_norm_g, v_q_a_norm_g, v_w_q_b, v_kv_a_norm_g, v_w_kv_b, v_mla_norm_g, v_w_out, v_norm2_g, v_w_gate, v_w_up, v_w_down, v_final_norm_g):
    big_w = [w_in, w_q_b, w_kv_b, w_out, w_gate, w_up, w_down]
    big_m = [m_w_in, m_w_q_b, m_w_kv_b, m_w_out, m_w_gate, m_w_up, m_w_down]
    big_v = [v_w_in, v_w_q_b, v_w_kv_b, v_w_out, v_w_gate, v_w_up, v_w_down]
    small_w = [norm1_g, hgrn_norm_g, q_a_norm_g, kv_a_norm_g, mla_norm_g, norm2_g, final_norm_g]
    small_m = [m_norm1_g, m_hgrn_norm_g, m_q_a_norm_g, m_kv_a_norm_g, m_mla_norm_g, m_norm2_g, m_final_norm_g]
    small_v = [v_norm1_g, v_hgrn_norm_g, v_q_a_norm_g, v_kv_a_norm_g, v_mla_norm_g, v_norm2_g, v_final_norm_g]
    seq = x.shape[1]
    my_id = 4 * lax.axis_index("x") + 2 * lax.axis_index("y") + lax.axis_index("c")

    shard = lambda w: w[0].astype(BF16)
    col_t = lambda w: jnp.swapaxes(w, 1, 2)[0]
    shard_t = lambda w: col_t(w).astype(BF16)
    g_in, g_q, g_kv, g_lb = _all_gather_call([shard_t(w_in), shard_t(w_q_b), shard(w_kv_b), lb_logits.reshape(4, 64)])
    early_full = (g_in.reshape(D_IN, D_MODEL), g_q.reshape(768, Q_LORA), _cols_from_slots(g_kv))
    lb_full = g_lb.reshape(N_DEV, 2, 2, 64).transpose(1, 2, 0, 3).reshape(2, 2, 512)

    as_row = lambda a: a.reshape(1, -1)
    loss_row, grad_x, recv_in_a, early_g, late_recv, small_g, d_lb = _step_core(
        x, loss_target, [as_row(s) for s in small_w], lb_full, early_full,
        [shard(w_out), shard_t(w_gate), shard_t(w_up), shard(w_down)], seq, min(16, seq // CHUNK),
        (256, 512, min(1024, seq), min(1024, seq), min(2048, 2 * seq)), True)

    grads, deltas, new_ms, new_vs = {}, {}, {}, {}
    views = {name: (col_t if name in ("w_in", "w_q_b", "w_gate", "w_up") else (lambda a: a[0])) for name, _, _, _ in BIG}
    backs = {name: ((lambda a: jnp.swapaxes(a[None], 1, 2)) if name in ("w_in", "w_q_b", "w_gate", "w_up") else (lambda a: a[None]))
             for name, _, _, _ in BIG}
    by_name = {name: (w, m, v) for (name, _, _, _), w, m, v in zip(BIG, big_w, big_m, big_v)}
    late_names = ["w_out", "w_gate", "w_up", "w_down"]
    n_small = len(small_g)
    g_l, d_l, nm_l, nv_l, recv = _adamw_recv_hosting(
        [views[n](by_name[n][0]) for n in late_names], list(late_recv), [views[n](by_name[n][1]) for n in late_names],
        [views[n](by_name[n][2]) for n in late_names],
        early_g + small_g + [d_lb.reshape(4, 512), loss_row], [True] * 3 + [False] * (n_small + 2))
    for i, name in enumerate(late_names):
        grads[name], deltas[name], new_ms[name], new_vs[name] = (backs[name](a[i]) for a in (g_l, d_l, nm_l, nv_l))
    sums = _sum_slots_call(recv[3:])
    g_small = [g.reshape(s.shape) for g, s in zip(sums[:n_small], small_w)]
    g_lb_own = lax.dynamic_index_in_dim(sums[n_small].reshape(2, 2, N_DEV, 64), my_id, axis=2, keepdims=False)
    loss = sums[n_small + 1][0, 0]

    for name, r in zip(["w_in", "w_q_b", "w_kv_b"], recv[:3]):
        w, m, v = (views[name](a) for a in by_name[name])
        g, d, nm, nv = _adamw_recv_halves(w, (recv_in_a, r), m, v, name) if name == "w_in" else _adamw_recv(w, r, m, v, name)
        grads[name], deltas[name], new_ms[name], new_vs[name] = (backs[name](a) for a in (g, d, nm, nv))
    lb_rows = lambda a: a.reshape(4, 64)
    d_s, nm_s, nv_s = _adamw_small(
        [as_row(a) for a in small_w] + [lb_rows(lb_logits)], [as_row(a) for a in g_small] + [lb_rows(g_lb_own)],
        [as_row(a) for a in small_m] + [lb_rows(m_lb_logits)], [as_row(a) for a in small_v] + [lb_rows(v_lb_logits)])
    for i, (s, (name, _)) in enumerate(zip(small_w + [lb_logits], SMALL + (("lb_logits", 0),))):
        grads[name] = (g_small + [g_lb_own])[i]
        deltas[name], new_ms[name], new_vs[name] = d_s[i].reshape(s.shape), nm_s[i].reshape(s.shape), nv_s[i].reshape(s.shape)

    order = ["norm1_g", "w_in", "lb_logits", "hgrn_norm_g", "q_a_norm_g", "w_q_b", "kv_a_norm_g", "w_kv_b", "mla_norm_g",
             "w_out", "norm2_g", "w_gate", "w_up", "w_down", "final_norm_g"]
    return (loss, grad_x, *[grads[n] for n in order], *[deltas[n] for n in order],
            *[new_ms[n] for n in order], *[new_vs[n] for n in order])
```

```python
import functools

import jax
import jax.numpy as jnp
from jax import lax
from jax.experimental import pallas as pl
from jax.experimental.pallas import tpu as pltpu

F32 = jnp.float32
BF16 = jnp.bfloat16

N_DEV = 8
D_MODEL = 1024
D_FF = 2816
A_WIDTH = 512
HEAD_PAIR = 128
CHUNK = 64
B_HEADS = 4
B_NOPE = 128
B_ROPE = 64
B_V = 128
QK_PAD = 256
Q_LORA = 384
KV_LORA = 256
D_IN = 3264
D_IN_PAD = 3328
IN_WIDTHS = (512, 512, 512, 512, 512, Q_LORA, KV_LORA, 128)
ROPE_THETA = 10000.0
EPS = 1e-6
ATTN_SCALE = (B_NOPE + B_ROPE) ** -0.5
ATTN_SUB = 256
ATTN_SUB_BWD = 256
ROW_SUB = 256
ADAM_LR, ADAM_B1, ADAM_B2, ADAM_EPS, ADAM_WD, ADAM_STEP = 0.001, 0.9, 0.999, 1e-08, 0.01, 10
VMEM_LIMIT = 60 * 1024 * 1024
MESH = pl.DeviceIdType.MESH

BIG = (("w_in", 1024, D_IN, 1), ("w_q_b", Q_LORA, 768, 1), ("w_kv_b", KV_LORA, 1024, 1), ("w_out", 1024, 1024, 0),
       ("w_gate", 1024, D_FF, 1), ("w_up", 1024, D_FF, 1), ("w_down", D_FF, 1024, 0))
SMALL = (("norm1_g", 1024), ("hgrn_norm_g", 512), ("q_a_norm_g", 384), ("kv_a_norm_g", 256), ("mla_norm_g", 512),
         ("norm2_g", 1024), ("final_norm_g", 1024))


def _params(**kw):
    return pltpu.CompilerParams(vmem_limit_bytes=VMEM_LIMIT, **kw)


def _const_spec(shape):
    return pl.BlockSpec(shape, lambda *_: (0,) * len(shape), pipeline_mode=pl.Buffered(1))


def _dot(a, b):
    return jnp.dot(a, b, preferred_element_type=F32)


def _dot_nt(a, b):
    return lax.dot_general(a, b, (((1,), (1,)), ((), ())), preferred_element_type=F32)


def _dot_tn(a, b):
    return lax.dot_general(a, b, (((0,), (0,)), ((), ())), preferred_element_type=F32)


@jax.custom_vjp
def _mm(a, b):
    return _dot(a.astype(BF16), b.astype(BF16))


def _mm_fwd(a, b):
    return _mm(a, b), (a, b)


def _mm_bwd(res, g):
    a, b = res
    gb = g.astype(BF16)
    return _dot_nt(gb, b.astype(BF16)), _dot_tn(a.astype(BF16), gb)


_mm.defvjp(_mm_fwd, _mm_bwd)


@jax.custom_vjp
def _mm_nt(a, b):
    return _dot_nt(a.astype(BF16), b.astype(BF16))


def _mm_nt_fwd(a, b):
    return _mm_nt(a, b), (a, b)


def _mm_nt_bwd(res, g):
    a, b = res
    gb = g.astype(BF16)
    return _dot(gb, b.astype(BF16)), _dot_tn(gb, a.astype(BF16))


_mm_nt.defvjp(_mm_nt_fwd, _mm_nt_bwd)


@jax.custom_vjp
def _mm_tn(a, b):
    return _dot_tn(a.astype(BF16), b.astype(BF16))


def _mm_tn_fwd(a, b):
    return _mm_tn(a, b), (a, b)


def _mm_tn_bwd(res, g):
    a, b = res
    gb = g.astype(BF16)
    return _dot_nt(b.astype(BF16), gb), _dot(a.astype(BF16), gb)


_mm_tn.defvjp(_mm_tn_fwd, _mm_tn_bwd)


def _dot_exact_rhs(a, m):
    hi = a.astype(BF16)
    lo = (a - hi.astype(F32)).astype(BF16)
    return _dot(hi, m) + _dot(lo, m)


@jax.custom_vjp
def _group_mean(a, m):
    return _dot_exact_rhs(a, m)


def _group_mean_fwd(a, m):
    return _group_mean(a, m), m


def _group_mean_bwd(m, g):
    return _dot_exact_rhs(g, m), jnp.zeros_like(m)


_group_mean.defvjp(_group_mean_fwd, _group_mean_bwd)


def _roll_rows(a, shift):
    return pltpu.roll(a, shift, 0)


def _cumsum_rows_raw(a, reverse):
    n = a.shape[0]
    row = lax.broadcasted_iota(jnp.int32, a.shape, 0)
    s = 1
    while s < n:
        if reverse:
            a = a + jnp.where(row < n - s, _roll_rows(a, n - s), 0.0)
        else:
            a = a + jnp.where(row >= s, _roll_rows(a, s), 0.0)
        s *= 2
    return a


@functools.partial(jax.custom_vjp, nondiff_argnums=(1,))
def _cumsum_rows(a, reverse):
    return _cumsum_rows_raw(a, reverse)


def _cumsum_rows_fwd(a, reverse):
    return _cumsum_rows_raw(a, reverse), None


def _cumsum_rows_bwd(reverse, _, g):
    return (_cumsum_rows_raw(g, not reverse),)


_cumsum_rows.defvjp(_cumsum_rows_fwd, _cumsum_rows_bwd)


def _rms(x, g):
    r = lax.rsqrt(jnp.mean(x * x, axis=-1, keepdims=True) + EPS)
    return x * r * g


def _rms_bwd(x, g, dy):
    r = lax.rsqrt(jnp.mean(x * x, axis=-1, keepdims=True) + EPS)
    xh = x * r
    dg = jnp.sum(dy * xh, axis=0, keepdims=True)
    dxh = dy * g
    dx = r * (dxh - xh * jnp.mean(dxh * xh, axis=-1, keepdims=True))
    return dx, dg


def _sigmoid(a):
    return jax.nn.sigmoid(a)


def _mesh_place():
    x, y, c = lax.axis_index("x"), lax.axis_index("y"), lax.axis_index("c")
    return x, y, c


def _dev_index(p):
    return 4 * p[0] + 2 * p[1] + p[2]


def _comm_sems(n):
    return [pltpu.SemaphoreType.DMA((n, 7)), pltpu.SemaphoreType.DMA((n, 7)), pltpu.SemaphoreType.DMA((n,))]


def _gather_protocol(ins, outs, send_sems, recv_sems, local_sems):
    n = len(ins)
    x, y, c = _mesh_place()
    me, sibling = (x, y, c), (x, y, 1 - c)
    chips = [(1 - x, y), (x, 1 - y), (1 - x, 1 - y)]

    def copy(a, k, block, to, src=None):
        slot = outs[a].at[_dev_index(block)]
        return pltpu.make_async_remote_copy(
            src_ref=slot if src is None else src, dst_ref=slot,
            send_sem=send_sems.at[a, k], recv_sem=recv_sems.at[a, k], device_id=to, device_id_type=MESH)

    def mine(a):
        return pltpu.make_async_copy(ins[a], outs[a].at[_dev_index(me)], local_sems.at[a])

    def first(a):
        return [copy(a, 0, me, sibling, src=ins[a])] + [copy(a, 1 + j, me, (*chip, c), src=ins[a]) for j, chip in enumerate(chips)]

    def start():
        for a in range(n):
            mine(a).start()
            for cp in first(a):
                cp.start()

    def forward():
        for a in range(n):
            for j, chip in enumerate(chips):
                copy(a, 1 + j, (*chip, c), me).wait_recv()
                copy(a, 4 + j, (*chip, c), sibling).start()

    def finish():
        for a in range(n):
            copy(a, 0, sibling, me).wait_recv()
            for j, chip in enumerate(chips):
                copy(a, 4 + j, (*chip, 1 - c), me).wait_recv()
        for a in range(n):
            mine(a).wait()
            for cp in first(a):
                cp.wait_send()
            for j, chip in enumerate(chips):
                copy(a, 4 + j, (*chip, c), sibling).wait_send()

    return start, forward, finish


def _exchange_protocol(ins, outs, scatter, send_sems, recv_sems, local_sems):
    n = len(ins)
    x, y, c = _mesh_place()
    me = (x, y, c)
    my_id = _dev_index(me)
    rels = [(dx, dy, dc) for dx in (0, 1) for dy in (0, 1) for dc in (0, 1)][1:]

    def peer_of(rel):
        return tuple(1 - v if d else v for v, d in zip(me, rel))

    def src(a, dev):
        return ins[a].at[dev] if scatter[a] else ins[a]

    def send(a, k):
        peer = peer_of(rels[k])
        return pltpu.make_async_remote_copy(
            src_ref=src(a, _dev_index(peer)), dst_ref=outs[a].at[my_id],
            send_sem=send_sems.at[a, k], recv_sem=recv_sems.at[a, k], device_id=peer, device_id_type=MESH)

    def arrival(a, k):
        peer = peer_of(rels[k])
        return pltpu.make_async_remote_copy(
            src_ref=src(a, my_id), dst_ref=outs[a].at[_dev_index(peer)],
            send_sem=send_sems.at[a, k], recv_sem=recv_sems.at[a, k], device_id=peer, device_id_type=MESH)

    def own(a):
        return pltpu.make_async_copy(src(a, my_id), outs[a].at[my_id], local_sems.at[a])

    def start():
        for a in range(n):
            own(a).start()
            for k in range(7):
                send(a, k).start()

    def finish():
        for a in range(n):
            for k in range(7):
                arrival(a, k).wait_recv()
        for a in range(n):
            for k in range(7):
                send(a, k).wait_send()
            own(a).wait()

    return start, finish


def _slot_shapes(blocks, scatter=None):
    return [jax.ShapeDtypeStruct(b.shape if (scatter and scatter[a]) else (N_DEV,) + b.shape, b.dtype) for a, b in enumerate(blocks)]


def _all_gather_call(blocks):
    n = len(blocks)

    def body(*refs):
        start, forward, finish = _gather_protocol(refs[:n], refs[n:2 * n], *refs[2 * n:])
        start()
        forward()
        finish()

    any_spec = pl.BlockSpec(memory_space=pl.ANY)
    return pl.pallas_call(
        body, name="weights_all_gather", out_shape=_slot_shapes(blocks),
        in_specs=[any_spec] * n, out_specs=[any_spec] * n, scratch_shapes=_comm_sems(n),
    )(*blocks)


def _sum_slots_call(recvs):
    n = len(recvs)

    def body(*refs):
        for in_ref, out_ref in zip(refs[:n], refs[n:]):
            acc = in_ref[0]
            for j in range(1, N_DEV):
                acc = acc + in_ref[j]
            out_ref[...] = acc

    return pl.pallas_call(
        body, name="small_grad_sum", out_shape=[jax.ShapeDtypeStruct(r.shape[1:], F32) for r in recvs],
        compiler_params=_params(),
    )(*recvs)


def _adam_update(w, g, m, v):
    nm = ADAM_B1 * m + (1.0 - ADAM_B1) * g
    nv = ADAM_B2 * v + (1.0 - ADAM_B2) * (g * g)
    bc1 = 1.0 - ADAM_B1 ** ADAM_STEP
    bc2 = 1.0 - ADAM_B2 ** ADAM_STEP
    return -ADAM_LR * ((nm / bc1) / (jnp.sqrt(nv / bc2) + ADAM_EPS) + ADAM_WD * w), nm, nv


def _adamw_recv(w, recv, m, v, tag):
    r, c = w.shape
    tr = r
    for cand in (512, 256, 128):
        if r > cand and r % cand == 0:
            tr = cand
            break

    def body(w_ref, r_ref, m_ref, v_ref, g_ref, d_ref, nm_ref, nv_ref):
        g = r_ref[0].astype(F32)
        for j in range(1, N_DEV):
            g = g + r_ref[j].astype(F32)
        g_ref[...] = g
        d_ref[...], nm_ref[...], nv_ref[...] = _adam_update(w_ref[...], g, m_ref[...], v_ref[...])

    spec = pl.BlockSpec((tr, c), lambda i: (i, 0))
    return pl.pallas_call(
        body, name="adamw_" + tag, out_shape=[jax.ShapeDtypeStruct(w.shape, F32)] * 4, grid=(r // tr,),
        in_specs=[spec, pl.BlockSpec((N_DEV, tr, c), lambda i: (0, i, 0)), spec, spec], out_specs=[spec] * 4,
        compiler_params=_params(),
    )(w, recv, m, v)


def _adamw_recv_halves(w, recv_halves, m, v, tag):
    r, c = w.shape
    half = c // 2

    def body(w_ref, ra_ref, rb_ref, m_ref, v_ref, g_ref, d_ref, nm_ref, nv_ref):
        def update(r_ref):
            g = r_ref[0].astype(F32)
            for j in range(1, N_DEV):
                g = g + r_ref[j].astype(F32)
            g_ref[...] = g
            d_ref[...], nm_ref[...], nv_ref[...] = _adam_update(w_ref[...], g, m_ref[...], v_ref[...])

        pl.when(pl.program_id(0) == 0)(lambda: update(ra_ref))
        pl.when(pl.program_id(0) == 1)(lambda: update(rb_ref))

    spec = pl.BlockSpec((r, half), lambda j: (0, j))
    whole = pl.BlockSpec((N_DEV, r, half), lambda j: (0, 0, 0))
    return pl.pallas_call(
        body, name="adamw_" + tag, out_shape=[jax.ShapeDtypeStruct(w.shape, F32)] * 4, grid=(2,),
        in_specs=[spec, whole, whole, spec, spec], out_specs=[spec] * 4, compiler_params=_params(),
    )(w, *recv_halves, m, v)


def _adamw_recv_hosting(ws, recvs, ms, vs, blocks, scatter):
    n, ne = len(ws), len(blocks)
    rows = max(w.shape[0] for w in ws)
    cols = ws[0].shape[1]
    assert all(w.shape[1] == cols for w in ws)

    def body(*refs):
        ins, ex_in = refs[:4 * n], refs[4 * n:4 * n + ne]
        outs, ex_out = refs[4 * n + ne:8 * n + ne], refs[8 * n + ne:8 * n + 2 * ne]
        in_buf, recv_buf, out_buf, in_sems, out_sems = refs[8 * n + 2 * ne:8 * n + 2 * ne + 5]
        start, finish = _exchange_protocol(ex_in, ex_out, scatter, *refs[8 * n + 2 * ne + 5:])
        start()
        for a in range(n):
            r = pl.ds(0, ws[a].shape[0])
            loads = [pltpu.make_async_copy(ins[k * n + a], in_buf.at[j, r], in_sems.at[j]) for j, k in enumerate((0, 2, 3))]
            loads.append(pltpu.make_async_copy(ins[n + a], recv_buf.at[:, r], in_sems.at[3]))
            for cp in loads:
                cp.start()
            for cp in loads:
                cp.wait()
            g = recv_buf[0, r].astype(F32)
            for j in range(1, N_DEV):
                g = g + recv_buf[j, r].astype(F32)
            out_buf[0, r] = g
            out_buf[1, r], out_buf[2, r], out_buf[3, r] = _adam_update(in_buf[0, r], g, in_buf[1, r], in_buf[2, r])
            stores = [pltpu.make_async_copy(out_buf.at[k, r], outs[k * n + a], out_sems.at[k]) for k in range(4)]
            for cp in stores:
                cp.start()
            for cp in stores:
                cp.wait()
        finish()

    any_spec = pl.BlockSpec(memory_space=pl.ANY)
    out = pl.pallas_call(
        body, name="adamw_late_and_grad_exchange",
        out_shape=[jax.ShapeDtypeStruct(w.shape, F32) for w in ws] * 4 + _slot_shapes(blocks, scatter),
        in_specs=[any_spec] * (4 * n + ne), out_specs=[any_spec] * (4 * n + ne),
        scratch_shapes=[pltpu.VMEM((3, rows, cols), F32), pltpu.VMEM((N_DEV, rows, cols), BF16), pltpu.VMEM((4, rows, cols), F32),
                        pltpu.SemaphoreType.DMA((4,)), pltpu.SemaphoreType.DMA((4,))] + _comm_sems(ne),
        compiler_params=_params(),
    )(*ws, *recvs, *ms, *vs, *blocks)
    return out[:n], out[n:2 * n], out[2 * n:3 * n], out[3 * n:4 * n], out[4 * n:]


def _adamw_small(ws, gs, ms, vs):
    n = len(ws)

    def body(*refs):
        ins, outs = refs[:4 * n], refs[4 * n:]
        for a in range(n):
            d, nm, nv = _adam_update(ins[a][...], ins[n + a][...], ins[2 * n + a][...], ins[3 * n + a][...])
            outs[a][...], outs[n + a][...], outs[2 * n + a][...] = d, nm, nv

    out = pl.pallas_call(
        body, name="adamw_small", out_shape=[jax.ShapeDtypeStruct(w.shape, F32) for w in ws] * 3, compiler_params=_params(),
    )(*ws, *gs, *ms, *vs)
    return out[:n], out[n:2 * n], out[2 * n:]


def _inproj(x, g1, w_in, tm):
    t = x.shape[0]

    def body(x_ref, g_ref, w_ref, *outs):
        for j in range(tm // min(tm, ROW_SUB)):
            r = pl.ds(j * min(tm, ROW_SUB), min(tm, ROW_SUB))
            h = _rms(x_ref[r, :], g_ref[...]).astype(BF16)
            off = 0
            for o_ref, wd in zip(outs, IN_WIDTHS):
                o_ref[r, :] = _dot_nt(h, w_ref[off:off + wd, :])
                off += wd

    return pl.pallas_call(
        body, name="inproj_fwd", grid=(t // tm,),
        out_shape=[jax.ShapeDtypeStruct((t, wd), F32) for wd in IN_WIDTHS],
        in_specs=[pl.BlockSpec((tm, D_MODEL), lambda i: (i, 0)), _const_spec((1, D_MODEL)), _const_spec((D_IN_PAD, D_MODEL))],
        out_specs=[pl.BlockSpec((tm, wd), lambda i: (i, 0)) for wd in IN_WIDTHS],
        compiler_params=_params(),
    )(x, g1, w_in)


def _rope_tables(seq):
    inv = 1.0 / (ROPE_THETA ** (jnp.arange(0, B_ROPE, 2, dtype=F32) / B_ROPE))
    ang = jnp.arange(seq, dtype=F32)[:, None] * inv[None, :]
    cos, sin = jnp.cos(ang), jnp.sin(ang)
    z32, z64 = jnp.zeros_like(cos), jnp.zeros((seq, 64), F32)
    cos_t = jnp.concatenate([cos, cos, z64], axis=1)
    sin_a = jnp.concatenate([-sin, z32, z64], axis=1)
    sin_b = jnp.concatenate([z32, sin, z64], axis=1)
    return cos_t, sin_a, sin_b


def _rope(t, cos_t, sin_a, sin_b):
    return t * cos_t + pltpu.roll(t, 96, 1) * sin_a + pltpu.roll(t, 32, 1) * sin_b


def _rope_t(d, cos_t, sin_a, sin_b):
    return d * cos_t + pltpu.roll(d * sin_a, 32, 1) + pltpu.roll(d * sin_b, 96, 1)


def _mla_qkv(cq, ckv, kr, g_qa, g_kva, w_q, w_kv, tables, seq, tm):
    t = cq.shape[0]
    nblk = seq // tm

    def body(cq_ref, ckv_ref, kr_ref, gq_ref, gk_ref, wq_ref, wkv_ref, c_ref, sa_ref, sb_ref, q_out, k_out, v_out):
        cos_t, sin_a, sin_b = c_ref[...], sa_ref[...], sb_ref[...]
        cqn = _rms(cq_ref[...], gq_ref[...]).astype(BF16)
        ckn = _rms(ckv_ref[...], gk_ref[...]).astype(BF16)
        kr_rot = _rope(kr_ref[...], cos_t, sin_a, sin_b).astype(BF16)
        for h in range(B_HEADS):
            lo = h * QK_PAD
            q_out[:, lo:lo + 128] = (_dot_nt(cqn, wq_ref[lo:lo + 128, :]) * ATTN_SCALE).astype(BF16)
            qr = _rope(_dot_nt(cqn, wq_ref[lo + 128:lo + 256, :]), cos_t, sin_a, sin_b)
            q_out[:, lo + 128:lo + 256] = (qr * ATTN_SCALE).astype(BF16)
            k_out[:, lo:lo + 128] = _dot(ckn, wkv_ref[:, lo:lo + 128]).astype(BF16)
            k_out[:, lo + 128:lo + 256] = kr_rot
            v_out[:, h * B_V:(h + 1) * B_V] = _dot(ckn, wkv_ref[:, lo + 128:lo + 256]).astype(BF16)

    tok = lambda wd: pl.BlockSpec((tm, wd), lambda i: (i, 0))
    tab = pl.BlockSpec((tm, 128), lambda i: (i % nblk, 0))
    return pl.pallas_call(
        body, name="mla_qkv_fwd", grid=(t // tm,),
        out_shape=[jax.ShapeDtypeStruct((t, B_HEADS * QK_PAD), BF16), jax.ShapeDtypeStruct((t, B_HEADS * QK_PAD), BF16),
                   jax.ShapeDtypeStruct((t, B_HEADS * B_V), BF16)],
        in_specs=[tok(Q_LORA), tok(KV_LORA), tok(128), _const_spec((1, Q_LORA)), _const_spec((1, KV_LORA)),
                  _const_spec((B_HEADS * QK_PAD, Q_LORA)), _const_spec((KV_LORA, 1024)), tab, tab, tab],
        out_specs=[tok(B_HEADS * QK_PAD), tok(B_HEADS * QK_PAD), tok(B_HEADS * B_V)],
        compiler_params=_params(),
    )(cq, ckv, kr, g_qa, g_kva, w_q, w_kv, *tables)


def _step_index(nq):
    return (pl.program_id(0) * B_HEADS + pl.program_id(1)) * nq + pl.program_id(2)


def _attn_fwd(qcat, kcat, v, nb, seq, tq, gather=()):
    t = qcat.shape[0]
    nq = seq // tq
    ng = len(gather)
    steps = nb * B_HEADS * nq

    def body(q_ref, k_ref, v_ref, *rest):
        o_ref, lse_ref = rest[ng:ng + 2]
        if ng:
            start, forward, finish = _gather_protocol(rest[:ng], rest[ng + 2:2 * ng + 2], *rest[2 * ng + 2:])
            pl.when(_step_index(nq) == 0)(start)
            pl.when(_step_index(nq) == (3 * steps) // 4)(forward)
        for j in range(tq // ATTN_SUB):
            r = pl.ds(j * ATTN_SUB, ATTN_SUB)
            s = _dot_nt(q_ref[r, :], k_ref[...])
            m = jnp.max(s, axis=-1, keepdims=True)
            p = jnp.exp(s - m)
            l = jnp.sum(p, axis=-1, keepdims=True)
            o_ref[r, :] = _dot(p.astype(BF16), v_ref[...]) / l
            lse_ref[0, r, :] = m + jnp.log(l)
        if ng:
            pl.when(_step_index(nq) == steps - 1)(finish)

    any_spec = pl.BlockSpec(memory_space=pl.ANY)
    return pl.pallas_call(
        body, name="attn_fwd", grid=(nb, B_HEADS, nq),
        out_shape=[jax.ShapeDtypeStruct((t, B_HEADS * B_V), F32), jax.ShapeDtypeStruct((B_HEADS, t, 1), F32)] + _slot_shapes(gather),
        in_specs=[pl.BlockSpec((tq, QK_PAD), lambda b, h, i: (b * nq + i, h)),
                  pl.BlockSpec((seq, QK_PAD), lambda b, h, i: (b, h)),
                  pl.BlockSpec((seq, B_V), lambda b, h, i: (b, h))] + [any_spec] * ng,
        out_specs=[pl.BlockSpec((tq, B_V), lambda b, h, i: (b * nq + i, h)),
                   pl.BlockSpec((1, tq, 1), lambda b, h, i: (h, b * nq + i, 0))] + [any_spec] * ng,
        scratch_shapes=_comm_sems(ng) if ng else [],
        compiler_params=_params(),
    )(qcat, kcat, v, *gather)


def _attn_bwd(qcat, kcat, v, o, lse, do, nb, seq, tq, exchange=()):
    t = qcat.shape[0]
    nq = seq // tq
    ne = len(exchange)
    steps = nb * B_HEADS * nq

    def body(q_ref, k_ref, v_ref, o_ref, lse_ref, do_ref, *rest):
        dq_ref, dk_ref, dv_ref = rest[ne:ne + 3]
        if ne:
            start, finish = _exchange_protocol(rest[:ne], rest[ne + 3:2 * ne + 3], [True] * ne, *rest[2 * ne + 3:])
            pl.when(_step_index(nq) == 0)(start)

        @pl.when(pl.program_id(2) == 0)
        def _():
            dv_ref[...] = jnp.zeros_like(dv_ref)
            dk_ref[...] = jnp.zeros_like(dk_ref)

        for j in range(tq // ATTN_SUB_BWD):
            r = pl.ds(j * ATTN_SUB_BWD, ATTN_SUB_BWD)
            q, k = q_ref[r, :], k_ref[...]
            do_f = do_ref[r, :].astype(F32)
            delta = jnp.sum(do_f * o_ref[r, :], axis=-1, keepdims=True)
            dob = do_f.astype(BF16)
            p = jnp.exp(_dot_nt(q, k) - lse_ref[0, r, :])
            ds = (p * (_dot_nt(dob, v_ref[...]) - delta)).astype(BF16)
            dq_ref[r, :] = _dot(ds, k).astype(dq_ref.dtype)
            dv_ref[...] += _dot_tn(p.astype(BF16), dob)
            dk_ref[...] += _dot_tn(ds, q)
        if ne:
            pl.when(_step_index(nq) == steps - 1)(finish)

    qspec = lambda wd: pl.BlockSpec((tq, wd), lambda b, h, i: (b * nq + i, h))
    kspec = lambda wd: pl.BlockSpec((seq, wd), lambda b, h, i: (b, h))
    any_spec = pl.BlockSpec(memory_space=pl.ANY)
    return pl.pallas_call(
        body, name="attn_bwd", grid=(nb, B_HEADS, nq),
        out_shape=[jax.ShapeDtypeStruct((t, B_HEADS * QK_PAD), BF16), jax.ShapeDtypeStruct((t, B_HEADS * QK_PAD), F32),
                   jax.ShapeDtypeStruct((t, B_HEADS * B_V), F32)] + _slot_shapes(exchange, [True] * ne),
        in_specs=[qspec(QK_PAD), kspec(QK_PAD), kspec(B_V), qspec(B_V),
                  pl.BlockSpec((1, tq, 1), lambda b, h, i: (h, b * nq + i, 0)), qspec(B_V)] + [any_spec] * ne,
        out_specs=[qspec(QK_PAD), kspec(QK_PAD), kspec(B_V)] + [any_spec] * ne,
        scratch_shapes=_comm_sems(ne) if ne else [],
        compiler_params=_params(),
    )(qcat, kcat, v, o, lse, do, *exchange)


def _gla_consts(reverse):
    row = lax.broadcasted_iota(jnp.int32, (CHUNK, CHUNK), 0)
    col = lax.broadcasted_iota(jnp.int32, (CHUNK, CHUNK), 1)
    causal = (row <= col) if reverse else (row >= col)
    lane = lax.broadcasted_iota(jnp.int32, (1, HEAD_PAIR), 1)
    m0 = (lane < 64).astype(F32)
    m1 = 1.0 - m0
    r2 = lax.broadcasted_iota(jnp.int32, (HEAD_PAIR, HEAD_PAIR), 0)
    c2 = lax.broadcasted_iota(jnp.int32, (HEAD_PAIR, HEAD_PAIR), 1)
    same_head = ((r2 < 64) == (c2 < 64)).astype(F32)
    return causal, m0, m1, same_head


def _gla_chunk(hq, hi, z, l0, l1, st, consts, reverse):
    q_dec, k_inv, k_end, decay = _gla_gates(hq, z, l0, l1, reverse)
    o, st_new = _gla_state(q_dec, st, decay, _gla_increment(hi, k_end, consts))
    return o + _gla_intra(q_dec, k_inv, hi, consts), st_new


def _gla_gates(hq, z, l0, l1, reverse):
    mx = jnp.maximum(l0, l1)
    e0, e1 = jnp.exp(l0 - mx), jnp.exp(l1 - mx)
    lb = e0 / (e0 + e1)
    q = hq * _sigmoid(hq)
    sz = _sigmoid(z)
    log_f = jnp.log(lb + (1.0 - lb) * sz)
    k = (1.0 - lb) * (1.0 - sz)
    cum = _cumsum_rows(log_f, reverse)
    decay = jnp.exp(jnp.sum(log_f, axis=0, keepdims=True))
    k_inv = k * jnp.exp(-cum)
    return q * jnp.exp(cum), k_inv, k_inv * decay, decay


def _gla_intra(q_dec, k_inv, hi, consts):
    causal, m0, m1, _ = consts
    o = None
    for mh in (m0, m1):
        s = jnp.where(causal, _mm_nt(q_dec * mh, k_inv), 0.0)
        part = _mm(s, hi) * mh
        o = part if o is None else o + part
    return o


def _gla_increment(hi, k_end, consts):
    return _mm_tn(hi, k_end) * consts[3]


def _gla_state(q_dec, st, decay, inc):
    return _mm_nt(q_dec, st), st * decay + inc


GLA_DIRS = (False, True)
GLA_BATCH_FWD = 8
GLA_BATCH_BWD = 4


def _gla_fwd(hq, hi, zs, lbls, nb, seq, group):
    t = hq.shape[0]
    rows = group * CHUNK
    nblk = seq // rows
    n_chunks = seq // CHUNK
    nd = len(GLA_DIRS)

    def body(*refs):
        ins, outs, st_refs = refs[:4 * nd], refs[4 * nd:6 * nd], refs[6 * nd:]
        @pl.when(pl.program_id(2) == 0)
        def _():
            for st_ref in st_refs:
                st_ref[...] = jnp.zeros_like(st_ref)

        consts = [_gla_consts(rev) for rev in GLA_DIRS]
        work = [(d, rev, group - 1 - cc if rev else cc) for cc in range(group) for d, rev in enumerate(GLA_DIRS)]
        rows_of = lambda c: pl.ds(c * CHUNK, CHUNK)
        sts = [st_ref[...] for st_ref in st_refs]
        for w0 in range(0, len(work), GLA_BATCH_FWD):
            batch = work[w0:w0 + GLA_BATCH_FWD]
            gates, intra, incs = {}, {}, {}
            for d, rev, c in batch:
                hq_ref, _, z_ref, lbl_ref = ins[4 * d:4 * d + 4]
                gates[d, c] = _gla_gates(hq_ref[rows_of(c), :], z_ref[rows_of(c), :], lbl_ref[0:1, :], lbl_ref[1:2, :], rev)
            for d, rev, c in batch:
                hi_c = ins[4 * d + 1][rows_of(c), :]
                intra[d, c] = _gla_intra(gates[d, c][0], gates[d, c][1], hi_c, consts[d])
                incs[d, c] = _gla_increment(hi_c, gates[d, c][2], consts[d])
            for d, rev, c in batch:
                outs[nd + d][0, 0, c] = sts[d].astype(outs[nd + d].dtype)
                o_state, sts[d] = _gla_state(gates[d, c][0], sts[d], gates[d, c][3], incs[d, c])
                outs[d][rows_of(c), :] = (intra[d, c] + o_state).astype(outs[d].dtype)
        for st_ref, st in zip(st_refs, sts):
            st_ref[...] = st

    def tb(rev):
        return (lambda i: nblk - 1 - i) if rev else (lambda i: i)

    tok = lambda rev: pl.BlockSpec((rows, HEAD_PAIR), lambda b, p, i: (b * nblk + tb(rev)(i), p))
    lspec = pl.BlockSpec((2, HEAD_PAIR), lambda b, p, i: (0, p))
    sspec = lambda rev: pl.BlockSpec((1, 1, group, HEAD_PAIR, HEAD_PAIR), lambda b, p, i: (b, p, tb(rev)(i), 0, 0))
    args, in_specs = [], []
    for d, rev in enumerate(GLA_DIRS):
        args += [hq, hi, zs[d], lbls[d]]
        in_specs += [tok(rev), tok(rev), tok(rev), lspec]
    return pl.pallas_call(
        body, name="gla_fwd", grid=(nb, 4, nblk),
        out_shape=[jax.ShapeDtypeStruct((t, A_WIDTH), BF16)] * nd
        + [jax.ShapeDtypeStruct((nb, 4, n_chunks, HEAD_PAIR, HEAD_PAIR), BF16)] * nd,
        in_specs=in_specs, out_specs=[tok(rev) for rev in GLA_DIRS] + [sspec(rev) for rev in GLA_DIRS],
        scratch_shapes=[pltpu.VMEM((HEAD_PAIR, HEAD_PAIR), F32)] * nd,
        compiler_params=_params(),
    )(*args)


def _gla_bwd(hq, hi, zs, lbls, saved, do, nb, seq, group):
    t = hq.shape[0]
    rows = group * CHUNK
    nblk = seq // rows
    nd = len(GLA_DIRS)

    def body(*refs):
        ins, outs, dst_refs = refs[:6 * nd], refs[6 * nd:10 * nd], refs[10 * nd:]
        dl_refs = outs[3 * nd:]

        @pl.when(pl.program_id(2) == 0)
        def _():
            for dst_ref, dl_ref in zip(dst_refs, dl_refs):
                dst_ref[...] = jnp.zeros_like(dst_ref)
                dl_ref[...] = jnp.zeros_like(dl_ref)

        consts = [_gla_consts(rev) for rev in GLA_DIRS]
        dsts = [dst_ref[...] for dst_ref in dst_refs]
        dls = [[jnp.zeros((1, HEAD_PAIR), F32), jnp.zeros((1, HEAD_PAIR), F32)] for _ in GLA_DIRS]
        work = [(d, rev, cc if rev else group - 1 - cc) for cc in range(group) for d, rev in enumerate(GLA_DIRS)]
        for w0 in range(0, len(work), GLA_BATCH_BWD):
            vjps = {}
            for d, rev, c in work[w0:w0 + GLA_BATCH_BWD]:
                hq_ref, hi_ref, z_ref, lbl_ref, save_ref, _ = ins[6 * d:6 * d + 6]
                r = pl.ds(c * CHUNK, CHUNK)
                fn = functools.partial(_gla_chunk, consts=consts[d], reverse=rev)
                _, vjps[d, c] = jax.vjp(fn, hq_ref[r, :], hi_ref[r, :], z_ref[r, :], lbl_ref[0:1, :], lbl_ref[1:2, :],
                                         save_ref[0, 0, c].astype(F32))
            for d, rev, c in work[w0:w0 + GLA_BATCH_BWD]:
                dq_ref, dv_ref, dz_ref = outs[3 * d:3 * d + 3]
                r = pl.ds(c * CHUNK, CHUNK)
                d_hq, d_hi, d_z, d_l0, d_l1, dsts[d] = vjps[d, c]((ins[6 * d + 5][r, :].astype(F32), dsts[d]))
                dq_ref[r, :] = d_hq.astype(dq_ref.dtype)
                dv_ref[r, :] = d_hi.astype(dv_ref.dtype)
                dz_ref[r, :] = d_z.astype(dz_ref.dtype)
                dls[d] = [dls[d][0] + d_l0, dls[d][1] + d_l1]
        for d in range(nd):
            dst_refs[d][...] = dsts[d]
            dl_refs[d][0, 0:1, :] += dls[d][0]
            dl_refs[d][0, 1:2, :] += dls[d][1]

    def tb(rev):
        return (lambda i: i) if rev else (lambda i: nblk - 1 - i)

    tok = lambda rev: pl.BlockSpec((rows, HEAD_PAIR), lambda b, p, i: (b * nblk + tb(rev)(i), p))
    lspec = pl.BlockSpec((2, HEAD_PAIR), lambda b, p, i: (0, p))
    sspec = lambda rev: pl.BlockSpec((1, 1, group, HEAD_PAIR, HEAD_PAIR), lambda b, p, i: (b, p, tb(rev)(i), 0, 0))
    args, in_specs, out_specs = [], [], []
    for d, rev in enumerate(GLA_DIRS):
        args += [hq, hi, zs[d], lbls[d], saved[d], do]
        in_specs += [tok(rev), tok(rev), tok(rev), lspec, sspec(rev), tok(rev)]
        out_specs += [tok(rev)] * 3
    out_specs += [pl.BlockSpec((1, 2, HEAD_PAIR), lambda b, p, i: (b, 0, p))] * nd
    return pl.pallas_call(
        body, name="gla_bwd", grid=(nb, 4, nblk),
        out_shape=[jax.ShapeDtypeStruct((t, A_WIDTH), BF16)] * (3 * nd) + [jax.ShapeDtypeStruct((nb, 2, A_WIDTH), F32)] * nd,
        in_specs=in_specs, out_specs=out_specs,
        scratch_shapes=[pltpu.VMEM((HEAD_PAIR, HEAD_PAIR), F32)] * nd,
        compiler_params=_params(),
    )(*args)


def _head_mean_matrix():
    r = lax.broadcasted_iota(jnp.int32, (A_WIDTH, A_WIDTH), 0) // 64
    c = lax.broadcasted_iota(jnp.int32, (A_WIDTH, A_WIDTH), 1) // 64
    return jnp.where(r == c, 1.0 / 64.0, 0.0).astype(BF16)


def _gla_out(o_f, o_b, hg, g, mean_mat):
    o = o_f + o_b
    ms = _group_mean(o * o, mean_mat)
    return o * lax.rsqrt(ms + EPS) * g * (hg * _sigmoid(hg))


def _gla_combine(o_f, o_b, hg, g, tm):
    t = o_f.shape[0]

    def body(of_ref, ob_ref, hg_ref, g_ref, y_ref):
        y_ref[...] = _gla_out(of_ref[...].astype(F32), ob_ref[...].astype(F32), hg_ref[...], g_ref[...], _head_mean_matrix())

    tok = pl.BlockSpec((tm, A_WIDTH), lambda i: (i, 0))
    return pl.pallas_call(
        body, name="gla_combine_fwd", grid=(t // tm,), out_shape=jax.ShapeDtypeStruct((t, A_WIDTH), F32),
        in_specs=[tok, tok, tok, _const_spec((1, A_WIDTH))], out_specs=tok, compiler_params=_params(),
    )(o_f, o_b, hg, g)


def _gla_combine_bwd(o_f, o_b, hg, g, dy, tm):
    t = o_f.shape[0]

    def body(of_ref, ob_ref, hg_ref, g_ref, dy_ref, do_ref, dhg_ref, dg_ref):
        mean_mat = _head_mean_matrix()
        fn = lambda o, hgv, gv: _gla_out(o, jnp.zeros_like(o), hgv, gv, mean_mat)
        _, vjp = jax.vjp(fn, of_ref[...].astype(F32) + ob_ref[...].astype(F32), hg_ref[...], g_ref[...])
        d_o, d_hg, d_g = vjp(dy_ref[...].astype(F32))
        do_ref[...] = d_o.astype(do_ref.dtype)
        dhg_ref[...] = d_hg.astype(dhg_ref.dtype)

        @pl.when(pl.program_id(0) == 0)
        def _():
            dg_ref[...] = jnp.zeros_like(dg_ref)

        dg_ref[...] += d_g

    tok = pl.BlockSpec((tm, A_WIDTH), lambda i: (i, 0))
    vec = pl.BlockSpec((1, A_WIDTH), lambda i: (0, 0))
    return pl.pallas_call(
        body, name="gla_combine_bwd", grid=(t // tm,),
        out_shape=[jax.ShapeDtypeStruct((t, A_WIDTH), BF16), jax.ShapeDtypeStruct((t, A_WIDTH), BF16),
                   jax.ShapeDtypeStruct((1, A_WIDTH), F32)],
        in_specs=[tok, tok, tok, _const_spec((1, A_WIDTH)), tok], out_specs=[tok, tok, vec], compiler_params=_params(),
    )(o_f, o_b, hg, g, dy)


def _post_fwd(x, ya, oattn, tgt, g_mla, w_out, g2, w_gate, w_up, w_down, g_fin, tm):
    t = x.shape[0]

    def body(x_ref, ya_ref, oa_ref, tgt_ref, gm_ref, wo_ref, g2_ref, wg_ref, wu_ref, wd_ref, gf_ref,
             x1_ref, x2_ref, gate_ref, up_ref, loss_ref):
        part = jnp.zeros((1, 1), F32)
        for j in range(tm // min(tm, ROW_SUB)):
            r = pl.ds(j * min(tm, ROW_SUB), min(tm, ROW_SUB))
            yb = _rms(oa_ref[r, :], gm_ref[...])
            x1 = x_ref[r, :] + _dot(ya_ref[r, :].astype(BF16), wo_ref[0:A_WIDTH, :]) + _dot(yb.astype(BF16), wo_ref[A_WIDTH:, :])
            x1_ref[r, :] = x1
            h2 = _rms(x1, g2_ref[...]).astype(BF16)
            gate, up = _dot_nt(h2, wg_ref[...]), _dot_nt(h2, wu_ref[...])
            gate_ref[r, :] = gate.astype(BF16)
            up_ref[r, :] = up.astype(BF16)
            act = (gate * _sigmoid(gate) * up).astype(BF16)
            x2 = x1 + _dot(act, wd_ref[...])
            x2_ref[r, :] = x2
            err = _rms(x2, gf_ref[...]) - tgt_ref[r, :]
            part = part + 0.5 * jnp.sum(jnp.mean(err * err, axis=-1, keepdims=True), axis=0, keepdims=True)

        @pl.when(pl.program_id(0) == 0)
        def _():
            loss_ref[...] = jnp.zeros_like(loss_ref)

        loss_ref[...] += jnp.broadcast_to(part, loss_ref.shape)

    tok = lambda wd: pl.BlockSpec((tm, wd), lambda i: (i, 0))
    return pl.pallas_call(
        body, name="post_fwd", grid=(t // tm,),
        out_shape=[jax.ShapeDtypeStruct((t, D_MODEL), F32)] * 2 + [jax.ShapeDtypeStruct((t, D_FF), BF16)] * 2
        + [jax.ShapeDtypeStruct((1, 128), F32)],
        in_specs=[tok(D_MODEL), tok(A_WIDTH), tok(512), tok(D_MODEL), _const_spec((1, 512)), _const_spec((D_MODEL, D_MODEL)),
                  _const_spec((1, D_MODEL)), _const_spec((D_FF, D_MODEL)), _const_spec((D_FF, D_MODEL)),
                  _const_spec((D_FF, D_MODEL)), _const_spec((1, D_MODEL))],
        out_specs=[tok(D_MODEL), tok(D_MODEL), tok(D_FF), tok(D_FF), pl.BlockSpec((1, 128), lambda i: (0, 0))],
        compiler_params=_params(),
    )(x, ya, oattn, tgt, g_mla, w_out, g2, w_gate, w_up, w_down, g_fin)


def _post_bwd(x1, x2, gate_b, up_b, ya, oattn, tgt, g_mla, w_out, g2, w_gate, w_up, w_down, g_fin, tm):
    t = x1.shape[0]

    def body(x1_ref, x2_ref, gate_ref, up_ref, ya_ref, oa_ref, tgt_ref, gm_ref, wo_ref, g2_ref, wg_ref, wu_ref, wd_ref, gf_ref,
             dx1_ref, dya_ref, doa_ref, ycat_ref, dx1b_ref, h2_ref, dgate_ref, dup_ref, act_ref, dx2b_ref,
             dgm_ref, dg2_ref, dgf_ref):
        x1, x2 = x1_ref[...], x2_ref[...]
        dy = (_rms(x2, gf_ref[...]) - tgt_ref[...]) * (1.0 / D_MODEL)
        dx2, dgf = _rms_bwd(x2, gf_ref[...], dy)
        dx2b = dx2.astype(BF16)
        dx2b_ref[...] = dx2b
        h2_ref[...] = _rms(x1, g2_ref[...]).astype(BF16)
        gate, up = gate_ref[...].astype(F32), up_ref[...].astype(F32)
        sg = _sigmoid(gate)
        sl = gate * sg
        act_ref[...] = (sl * up).astype(BF16)
        dact = _dot_nt(dx2b, wd_ref[...])
        dup = (dact * sl).astype(BF16)
        dgate = (dact * up * (sg * (1.0 + gate * (1.0 - sg)))).astype(BF16)
        dup_ref[...] = dup
        dgate_ref[...] = dgate
        dh2 = _dot(dgate, wg_ref[...]) + _dot(dup, wu_ref[...])
        dx1n, dg2 = _rms_bwd(x1, g2_ref[...], dh2)
        dx1 = dx2 + dx1n
        dx1_ref[...] = dx1
        dx1b = dx1.astype(BF16)
        dx1b_ref[...] = dx1b
        oa = oa_ref[...]
        ycat_ref[:, 0:A_WIDTH] = ya_ref[...].astype(BF16)
        ycat_ref[:, A_WIDTH:] = _rms(oa, gm_ref[...]).astype(BF16)
        dya_ref[...] = _dot_nt(dx1b, wo_ref[0:A_WIDTH, :]).astype(dya_ref.dtype)
        doa, dgm = _rms_bwd(oa, gm_ref[...], _dot_nt(dx1b, wo_ref[A_WIDTH:, :]))
        doa_ref[...] = doa.astype(doa_ref.dtype)

        @pl.when(pl.program_id(0) == 0)
        def _():
            dgm_ref[...] = jnp.zeros_like(dgm_ref)
            dg2_ref[...] = jnp.zeros_like(dg2_ref)
            dgf_ref[...] = jnp.zeros_like(dgf_ref)

        dgm_ref[...] += dgm
        dg2_ref[...] += dg2
        dgf_ref[...] += dgf

    tok = lambda wd: pl.BlockSpec((tm, wd), lambda i: (i, 0))
    vec = lambda wd: pl.BlockSpec((1, wd), lambda i: (0, 0))
    sds = lambda wd, dt: jax.ShapeDtypeStruct((t, wd), dt)
    return pl.pallas_call(
        body, name="post_bwd", grid=(t // tm,),
        out_shape=[sds(D_MODEL, F32), sds(512, BF16), sds(512, BF16), sds(D_MODEL, BF16), sds(D_MODEL, BF16), sds(D_MODEL, BF16),
                   sds(D_FF, BF16), sds(D_FF, BF16), sds(D_FF, BF16), sds(D_MODEL, BF16),
                   jax.ShapeDtypeStruct((1, 512), F32), jax.ShapeDtypeStruct((1, D_MODEL), F32), jax.ShapeDtypeStruct((1, D_MODEL), F32)],
        in_specs=[tok(D_MODEL), tok(D_MODEL), tok(D_FF), tok(D_FF), tok(512), tok(512), tok(D_MODEL), _const_spec((1, 512)),
                  _const_spec((D_MODEL, D_MODEL)), _const_spec((1, D_MODEL)), _const_spec((D_FF, D_MODEL)),
                  _const_spec((D_FF, D_MODEL)), _const_spec((D_FF, D_MODEL)), _const_spec((1, D_MODEL))],
        out_specs=[tok(D_MODEL), tok(512), tok(512), tok(D_MODEL), tok(D_MODEL), tok(D_MODEL), tok(D_FF), tok(D_FF), tok(D_FF),
                   tok(D_MODEL), vec(512), vec(D_MODEL), vec(D_MODEL)],
        compiler_params=_params(),
    )(x1, x2, gate_b, up_b, ya, oattn, tgt, g_mla, w_out, g2, w_gate, w_up, w_down, g_fin)


def _matmul_tn(a, b, tn, tt, tag, b_cols=None, k_out=None, exchange=()):
    t, k = a.shape
    c0, n = (0, b.shape[1]) if b_cols is None else b_cols
    k_out = k if k_out is None else k_out
    last = t // tt - 1
    ne = len(exchange)
    n_j = n // tn

    def body(a_ref, b_ref, *rest):
        o_ref, acc_ref = rest[ne], rest[2 * ne + 1]
        if ne:
            start, finish = _exchange_protocol(rest[:ne], rest[ne + 1:2 * ne + 1], [True] * ne, *rest[2 * ne + 2:])
            pl.when((pl.program_id(0) == 0) & (pl.program_id(1) == 0))(start)
        part = _dot_tn(a_ref[...], b_ref[...])

        @pl.when(pl.program_id(1) == 0)
        def _():
            acc_ref[...] = part

        @pl.when(pl.program_id(1) > 0)
        def _():
            acc_ref[...] += part

        @pl.when(pl.program_id(1) == last)
        def _():
            o_ref[...] = acc_ref[0:k_out, :].astype(o_ref.dtype)

        if ne:
            pl.when((pl.program_id(0) == n_j - 1) & (pl.program_id(1) == last))(finish)

    any_spec = pl.BlockSpec(memory_space=pl.ANY)
    out = pl.pallas_call(
        body, name="wgrad_" + tag, grid=(n_j, t // tt),
        out_shape=[jax.ShapeDtypeStruct((k_out, n), BF16)] + _slot_shapes(exchange, [True] * ne),
        in_specs=[pl.BlockSpec((tt, k), lambda j, i: (i, 0)), pl.BlockSpec((tt, tn), lambda j, i: (i, j + c0 // tn))]
        + [any_spec] * ne,
        out_specs=[pl.BlockSpec((k_out, tn), lambda j, i: (0, j))] + [any_spec] * ne,
        scratch_shapes=[pltpu.VMEM((k, tn), F32)] + (_comm_sems(ne) if ne else []),
        compiler_params=_params(),
    )(a, b, *exchange)
    return out if ne else out[0]


def _mla_qkv_bwd(cq, ckv, g_qa, g_kva, w_q, w_kv, tables, dq, dk, dv, seq, tm):
    t = cq.shape[0]
    nblk = seq // tm

    def body(cq_ref, ckv_ref, gq_ref, gk_ref, wq_ref, wkv_ref, c_ref, sa_ref, sb_ref, dq_ref, dk_ref, dv_ref,
             dcq_ref, dckv_ref, dkr_ref, cqn_ref, dqf_ref, ckn_ref, dkv_ref, dgq_ref, dgk_ref):
        cos_t, sin_a, sin_b = c_ref[...], sa_ref[...], sb_ref[...]
        cqn_ref[...] = _rms(cq_ref[...], gq_ref[...]).astype(BF16)
        ckn_ref[...] = _rms(ckv_ref[...], gk_ref[...]).astype(BF16)
        dkr = jnp.zeros((tm, 128), F32)
        for h in range(B_HEADS):
            lo = h * QK_PAD
            dqf_ref[:, lo:lo + 128] = (dq_ref[:, lo:lo + 128].astype(F32) * ATTN_SCALE).astype(BF16)
            dq_rope = dq_ref[:, lo + 128:lo + 256].astype(F32) * ATTN_SCALE
            dqf_ref[:, lo + 128:lo + 256] = _rope_t(dq_rope, cos_t, sin_a, sin_b).astype(BF16)
            dkv_ref[:, lo:lo + 128] = dk_ref[:, lo:lo + 128].astype(BF16)
            dkv_ref[:, lo + 128:lo + 256] = dv_ref[:, h * B_V:(h + 1) * B_V].astype(BF16)
            dkr = dkr + dk_ref[:, lo + 128:lo + 256]
        dkr_ref[...] = _rope_t(dkr, cos_t, sin_a, sin_b).astype(dkr_ref.dtype)
        dcq, dgq = _rms_bwd(cq_ref[...], gq_ref[...], _dot(dqf_ref[...], wq_ref[...]))
        dckv, dgk = _rms_bwd(ckv_ref[...], gk_ref[...], _dot_nt(dkv_ref[...], wkv_ref[...]))
        dcq_ref[...] = dcq.astype(dcq_ref.dtype)
        dckv_ref[...] = dckv.astype(dckv_ref.dtype)

        @pl.when(pl.program_id(0) == 0)
        def _():
            dgq_ref[...] = jnp.zeros_like(dgq_ref)
            dgk_ref[...] = jnp.zeros_like(dgk_ref)

        dgq_ref[...] += dgq
        dgk_ref[...] += dgk

    tok = lambda wd: pl.BlockSpec((tm, wd), lambda i: (i, 0))
    vec = lambda wd: pl.BlockSpec((1, wd), lambda i: (0, 0))
    tab = pl.BlockSpec((tm, 128), lambda i: (i % nblk, 0))
    sds = lambda wd, dt: jax.ShapeDtypeStruct((t, wd), dt)
    return pl.pallas_call(
        body, name="mla_qkv_bwd", grid=(t // tm,),
        out_shape=[sds(Q_LORA, BF16), sds(KV_LORA, BF16), sds(128, BF16), sds(Q_LORA, BF16), sds(1024, BF16), sds(KV_LORA, BF16),
                   sds(1024, BF16), jax.ShapeDtypeStruct((1, Q_LORA), F32), jax.ShapeDtypeStruct((1, KV_LORA), F32)],
        in_specs=[tok(Q_LORA), tok(KV_LORA), _const_spec((1, Q_LORA)), _const_spec((1, KV_LORA)),
                  _const_spec((1024, Q_LORA)), _const_spec((KV_LORA, 1024)), tab, tab, tab,
                  tok(1024), tok(1024), tok(512)],
        out_specs=[tok(Q_LORA), tok(KV_LORA), tok(128), tok(Q_LORA), tok(1024), tok(KV_LORA), tok(1024),
                   vec(Q_LORA), vec(KV_LORA)],
        compiler_params=_params(),
    )(cq, ckv, g_qa, g_kva, w_q, w_kv, *tables, dq, dk, dv)


def _inproj_bwd(x, g1, w_in, dx1, pieces, tm):
    t = x.shape[0]
    counts = [len(p) for p in pieces]
    flat = [a for p in pieces for a in p]
    widths = [wd for wd, p in zip(IN_WIDTHS, pieces) for _ in p]

    def body(x_ref, g_ref, w_ref, dx1_ref, *refs):
        ins = refs[:len(flat)]
        dx_ref, h_ref, dp_ref, dg_ref = refs[len(flat):]
        xv = x_ref[...]
        h_ref[...] = _rms(xv, g_ref[...]).astype(BF16)
        off, j = 0, 0
        for wd, cnt in zip(IN_WIDTHS, counts):
            acc = ins[j][...].astype(F32)
            for jj in range(1, cnt):
                acc = acc + ins[j + jj][...].astype(F32)
            dp_ref[:, off:off + wd] = acc.astype(BF16)
            off += wd
            j += cnt
        dxn, dg = _rms_bwd(xv, g_ref[...], _dot(dp_ref[...], w_ref[...]))
        dx_ref[...] = dx1_ref[...] + dxn

        @pl.when(pl.program_id(0) == 0)
        def _():
            dg_ref[...] = jnp.zeros_like(dg_ref)

        dg_ref[...] += dg

    tok = lambda wd: pl.BlockSpec((tm, wd), lambda i: (i, 0))
    return pl.pallas_call(
        body, name="inproj_bwd", grid=(t // tm,),
        out_shape=[jax.ShapeDtypeStruct((t, D_MODEL), F32), jax.ShapeDtypeStruct((t, D_MODEL), BF16),
                   jax.ShapeDtypeStruct((t, D_IN_PAD), BF16), jax.ShapeDtypeStruct((1, D_MODEL), F32)],
        in_specs=[tok(D_MODEL), _const_spec((1, D_MODEL)), _const_spec((D_IN_PAD, D_MODEL)), tok(D_MODEL)] + [tok(wd) for wd in widths],
        out_specs=[tok(D_MODEL), tok(D_MODEL), tok(D_IN_PAD), pl.BlockSpec((1, D_MODEL), lambda i: (0, 0))],
        compiler_params=_params(),
    )(x, g1, w_in, dx1, *flat)


def _cols_from_slots(g):
    n, r, cs = g.shape
    return g.transpose(1, 0, 2).reshape(r, n * cs)


def _cols_to_slots(full):
    r, c = full.shape
    return full.reshape(r, N_DEV, c // N_DEV).transpose(1, 0, 2)


def _arrange_w_in_t(w_in_t):
    return jnp.concatenate([w_in_t, jnp.zeros((D_IN_PAD - D_IN, D_MODEL), w_in_t.dtype)], axis=0)


def _arrange_w_q_t(w_q_t):
    q3 = w_q_t.reshape(B_HEADS, B_NOPE + B_ROPE, Q_LORA)
    pad = jnp.zeros((B_HEADS, QK_PAD - B_NOPE - B_ROPE, Q_LORA), w_q_t.dtype)
    return jnp.concatenate([q3, pad], axis=1).reshape(B_HEADS * QK_PAD, Q_LORA)


def _unarrange_w_q_t(d_q_t):
    return d_q_t.reshape(B_HEADS, QK_PAD, Q_LORA)[:, :B_NOPE + B_ROPE].reshape(B_HEADS * (B_NOPE + B_ROPE), Q_LORA)


def _step_core(x, loss_target, small_w, lb_full, early_full, late, seq, group, tiles, distributed):
    g1, g_hgrn, g_qa, g_kva, g_mla, g2, g_fin = small_w
    w_in, w_q, w_kv = _arrange_w_in_t(early_full[0]), _arrange_w_q_t(early_full[1]), early_full[2]
    nb = x.shape[0]
    t = nb * seq
    tm, tm_fwd, tq_f, tq_b, tt = tiles
    xt = x.reshape(t, D_MODEL)
    tgt = loss_target.reshape(t, D_MODEL)
    tables = _rope_tables(seq)

    hq, hi, zf, zb, hg, cq, ckv, kr = _inproj(xt, g1, w_in, tm_fwd)
    qcat, kcat, vv = _mla_qkv(cq, ckv, kr, g_qa, g_kva, w_q, w_kv, tables, seq, tm_fwd)
    if distributed:
        oattn, lse, *late_slots = _attn_fwd(qcat, kcat, vv, nb, seq, tq_f, gather=tuple(late))
    else:
        oattn, lse = _attn_fwd(qcat, kcat, vv, nb, seq, tq_f)
        late_slots = late
    w_out = late_slots[0].reshape(D_MODEL, D_MODEL)
    w_gate, w_up = late_slots[1].reshape(D_FF, D_MODEL), late_slots[2].reshape(D_FF, D_MODEL)
    w_down = late_slots[3].reshape(D_FF, D_MODEL)
    lbl_f, lbl_b = lb_full[0], lb_full[1]
    o_f, o_b, save_f, save_b = _gla_fwd(hq, hi, (zf, zb), (lbl_f, lbl_b), nb, seq, group)
    ya = _gla_combine(o_f, o_b, hg, g_hgrn, tm_fwd)
    x1, x2, gate_b, up_b, loss_row = _post_fwd(xt, ya, oattn, tgt, g_mla, w_out, g2, w_gate, w_up, w_down, g_fin, tm_fwd)

    (dx1, d_ya, d_oattn, ycat_b, dx1_b, h2_b, dgate_b, dup_b, act_b, dx2_b, d_g_mla, d_g2, d_g_fin) = _post_bwd(
        x1, x2, gate_b, up_b, ya, oattn, tgt, g_mla, w_out, g2, w_gate, w_up, w_down, g_fin, tm)
    d_w_gate = _matmul_tn(dgate_b, h2_b, 512, tt, "gate")
    d_w_up = _matmul_tn(dup_b, h2_b, 512, tt, "up")
    d_w_down = _matmul_tn(act_b, dx2_b, 512, tt, "down")
    d_w_out = _matmul_tn(ycat_b, dx1_b, D_MODEL, tt, "out")
    late_g = [d_w_out.reshape(N_DEV, D_MODEL // N_DEV, D_MODEL)] + [
        g.reshape(N_DEV, D_FF // N_DEV, D_MODEL) for g in (d_w_gate, d_w_up, d_w_down)]
    if distributed:
        dq, dk, dv, *late_g = _attn_bwd(qcat, kcat, vv, oattn, lse, d_oattn, nb, seq, tq_b, exchange=tuple(late_g))
    else:
        dq, dk, dv = _attn_bwd(qcat, kcat, vv, oattn, lse, d_oattn, nb, seq, tq_b)
    (d_cq, d_ckv, d_kr, cqn_b, dqf_b, ckn_b, dkv_b, d_g_qa, d_g_kva) = _mla_qkv_bwd(
        cq, ckv, g_qa, g_kva, w_q, w_kv, tables, dq, dk, dv, seq, tm_fwd)
    d_w_q = _matmul_tn(dqf_b, cqn_b, Q_LORA, tt, "q_b")
    d_w_kv = _matmul_tn(ckn_b, dkv_b, B_HEADS * (B_NOPE + B_V), tt, "kv_b")
    d_o, d_hg, d_g_hgrn = _gla_combine_bwd(o_f, o_b, hg, g_hgrn, d_ya, tm_fwd)
    dq_f, dv_f, dz_f, dq_b, dv_b, dz_b, dl_f, dl_b = _gla_bwd(
        hq, hi, (zf, zb), (lbl_f, lbl_b), (save_f, save_b), d_o, nb, seq, group)
    grad_x, h1_b, dproj_b, d_g1 = _inproj_bwd(
        xt, g1, w_in, dx1, [[dq_f, dq_b], [dv_f, dv_b], [dz_f], [dz_b], [d_hg], [d_cq], [d_ckv], [d_kr]], tm_fwd)
    half = D_MODEL // 2
    in_slots = lambda g: g.reshape(N_DEV, D_IN // N_DEV, half)
    g_in_a = in_slots(_matmul_tn(dproj_b, h1_b, half, tt, "in_a", b_cols=(0, half), k_out=D_IN))
    if distributed:
        d_w_in_b, g_in_a = _matmul_tn(dproj_b, h1_b, half, tt, "in_b", b_cols=(half, half), k_out=D_IN, exchange=(g_in_a,))
    else:
        d_w_in_b = _matmul_tn(dproj_b, h1_b, half, tt, "in_b", b_cols=(half, half), k_out=D_IN)

    early_g = [in_slots(d_w_in_b), _unarrange_w_q_t(d_w_q).reshape(N_DEV, 768 // N_DEV, Q_LORA), _cols_to_slots(d_w_kv)]
    d_lb = jnp.stack([jnp.sum(dl_f, axis=0), jnp.sum(dl_b, axis=0)], axis=0)
    small_grads = [d_g1, d_g_hgrn, d_g_qa, d_g_kva, d_g_mla, d_g2, d_g_fin]
    return loss_row, grad_x.reshape(nb, seq, D_MODEL), g_in_a, early_g, late_g, small_grads, d_lb


def kernel(x, norm1_g, w_in, lb_logits, hgrn_norm_g, q_a_norm_g, w_q_b, kv_a_norm_g, w_kv_b, mla_norm_g, w_out, norm2_g, w_gate, w_up, w_down, final_norm_g, loss_target, m_norm1_g, m_w_in, m_lb_logits, m_hgrn_norm_g, m_q_a_norm_g, m_w_q_b, m_kv_a_norm_g, m_w_kv_b, m_mla_norm_g, m_w_out, m_norm2_g, m_w_gate, m_w_up, m_w_down, m_final_norm_g, v_norm1_g, v_w_in, v_lb_logits, v_hgrn_norm_g, v_q_a_norm_g, v_w_q_b, v_kv_a_norm_g, v_w_kv_b, v_mla_norm_g, v_w_out, v_norm2_g, v_w_gate, v_w_up, v_w_down, v_final_norm_g):
    big_w = [w_in, w_q_b, w_kv_b, w_out, w_gate, w_up, w_down]
    big_m = [m_w_in, m_w_q_b, m_w_kv_b, m_w_out, m_w_gate, m_w_up, m_w_down]
    big_v = [v_w_in, v_w_q_b, v_w_kv_b, v_w_out, v_w_gate, v_w_up, v_w_down]
    small_w = [norm1_g, hgrn_norm_g, q_a_norm_g, kv_a_norm_g, mla_norm_g, norm2_g, final_norm_g]
    small_m = [m_norm1_g, m_hgrn_norm_g, m_q_a_norm_g, m_kv_a_norm_g, m_mla_norm_g, m_norm2_g, m_final_norm_g]
    small_v = [v_norm1_g, v_hgrn_norm_g, v_q_a_norm_g, v_kv_a_norm_g, v_mla_norm_g, v_norm2_g, v_final_norm_g]
    seq = x.shape[1]
    my_id = 4 * lax.axis_index("x") + 2 * lax.axis_index("y") + lax.axis_index("c")

    shard = lambda w: w[0].astype(BF16)
    col_t = lambda w: jnp.swapaxes(w, 1, 2)[0]
    shard_t = lambda w: col_t(w).astype(BF16)
    g_in, g_q, g_kv, g_lb = _all_gather_call([shard_t(w_in), shard_t(w_q_b), shard(w_kv_b), lb_logits.reshape(4, 64)])
    early_full = (g_in.reshape(D_IN, D_MODEL), g_q.reshape(768, Q_LORA), _cols_from_slots(g_kv))
    lb_full = g_lb.reshape(N_DEV, 2, 2, 64).transpose(1, 2, 0, 3).reshape(2, 2, 512)

    as_row = lambda a: a.reshape(1, -1)
    loss_row, grad_x, recv_in_a, early_g, late_recv, small_g, d_lb = _step_core(
        x, loss_target, [as_row(s) for s in small_w], lb_full, early_full,
        [shard(w_out), shard_t(w_gate), shard_t(w_up), shard(w_down)], seq, min(16, seq // CHUNK),
        (256, 512, min(1024, seq), min(1024, seq), min(2048, 2 * seq)), True)

    grads, deltas, new_ms, new_vs = {}, {}, {}, {}
    views = {name: (col_t if name in ("w_in", "w_q_b", "w_gate", "w_up") else (lambda a: a[0])) for name, _, _, _ in BIG}
    backs = {name: ((lambda a: jnp.swapaxes(a[None], 1, 2)) if name in ("w_in", "w_q_b", "w_gate", "w_up") else (lambda a: a[None]))
             for name, _, _, _ in BIG}
    by_name = {name: (w, m, v) for (name, _, _, _), w, m, v in zip(BIG, big_w, big_m, big_v)}
    late_names = ["w_out", "w_gate", "w_up", "w_down"]
    n_small = len(small_g)
    g_l, d_l, nm_l, nv_l, recv = _adamw_recv_hosting(
        [views[n](by_name[n][0]) for n in late_names], list(late_recv), [views[n](by_name[n][1]) for n in late_names],
        [views[n](by_name[n][2]) for n in late_names],
        early_g + small_g + [d_lb.reshape(4, 512), loss_row], [True] * 3 + [False] * (n_small + 2))
    for i, name in enumerate(late_names):
        grads[name], deltas[name], new_ms[name], new_vs[name] = (backs[name](a[i]) for a in (g_l, d_l, nm_l, nv_l))
    sums = _sum_slots_call(recv[3:])
    g_small = [g.reshape(s.shape) for g, s in zip(sums[:n_small], small_w)]
    g_lb_own = lax.dynamic_index_in_dim(sums[n_small].reshape(2, 2, N_DEV, 64), my_id, axis=2, keepdims=False)
    loss = sums[n_small + 1][0, 0]

    for name, r in zip(["w_in", "w_q_b", "w_kv_b"], recv[:3]):
        w, m, v = (views[name](a) for a in by_name[name])
        g, d, nm, nv = _adamw_recv_halves(w, (recv_in_a, r), m, v, name) if name == "w_in" else _adamw_recv(w, r, m, v, name)
        grads[name], deltas[name], new_ms[name], new_vs[name] = (backs[name](a) for a in (g, d, nm, nv))
    lb_rows = lambda a: a.reshape(4, 64)
    d_s, nm_s, nv_s = _adamw_small(
        [as_row(a) for a in small_w] + [lb_rows(lb_logits)], [as_row(a) for a in g_small] + [lb_rows(g_lb_own)],
        [as_row(a) for a in small_m] + [lb_rows(m_lb_logits)], [as_row(a) for a in small_v] + [lb_rows(v_lb_logits)])
    for i, (s, (name, _)) in enumerate(zip(small_w + [lb_logits], SMALL + (("lb_logits", 0),))):
        grads[name] = (g_small + [g_lb_own])[i]
        deltas[name], new_ms[name], new_vs[name] = d_s[i].reshape(s.shape), nm_s[i].reshape(s.shape), nv_s[i].reshape(s.shape)

    order = ["norm1_g", "w_in", "lb_logits", "hgrn_norm_g", "q_a_norm_g", "w_q_b", "kv_a_norm_g", "w_kv_b", "mla_norm_g",
             "w_out", "norm2_g", "w_gate", "w_up", "w_down", "final_norm_g"]
    return (loss, grad_x, *[grads[n] for n in order], *[deltas[n] for n in order],
            *[new_ms[n] for n in order], *[new_vs[n] for n in order])
```

```python
import functools

import jax
import jax.numpy as jnp
from jax import lax
from jax.experimental import pallas as pl
from jax.experimental.pallas import tpu as pltpu

F32 = jnp.float32
BF16 = jnp.bfloat16

N_DEV = 8
D_MODEL = 1024
D_FF = 2816
A_WIDTH = 512
HEAD_PAIR = 128
CHUNK = 64
B_HEADS = 4
B_NOPE = 128
B_ROPE = 64
B_V = 128
QK_PAD = 256
Q_LORA = 384
KV_LORA = 256
D_IN = 3264
D_IN_PAD = 3328
IN_WIDTHS = (512, 512, 512, 512, 512, Q_LORA, KV_LORA, 128)
ROPE_THETA = 10000.0
EPS = 1e-6
ATTN_SCALE = (B_NOPE + B_ROPE) ** -0.5
ATTN_SUB = 256
ATTN_SUB_BWD = 256
ROW_SUB = 256
ADAM_LR, ADAM_B1, ADAM_B2, ADAM_EPS, ADAM_WD, ADAM_STEP = 0.001, 0.9, 0.999, 1e-08, 0.01, 10
VMEM_LIMIT = 60 * 1024 * 1024
MESH = pl.DeviceIdType.MESH

BIG = (("w_in", 1024, D_IN, 1), ("w_q_b", Q_LORA, 768, 1), ("w_kv_b", KV_LORA, 1024, 1), ("w_out", 1024, 1024, 0),
       ("w_gate", 1024, D_FF, 1), ("w_up", 1024, D_FF, 1), ("w_down", D_FF, 1024, 0))
SMALL = (("norm1_g", 1024), ("hgrn_norm_g", 512), ("q_a_norm_g", 384), ("kv_a_norm_g", 256), ("mla_norm_g", 512),
         ("norm2_g", 1024), ("final_norm_g", 1024))


def _params(**kw):
    return pltpu.CompilerParams(vmem_limit_bytes=VMEM_LIMIT, **kw)


def _const_spec(shape):
    return pl.BlockSpec(shape, lambda *_: (0,) * len(shape), pipeline_mode=pl.Buffered(1))


def _dot(a, b):
    return jnp.dot(a, b, preferred_element_type=F32)


def _dot_nt(a, b):
    return lax.dot_general(a, b, (((1,), (1,)), ((), ())), preferred_element_type=F32)


def _dot_tn(a, b):
    return lax.dot_general(a, b, (((0,), (0,)), ((), ())), preferred_element_type=F32)


@jax.custom_vjp
def _mm(a, b):
    return _dot(a.astype(BF16), b.astype(BF16))


def _mm_fwd(a, b):
    return _mm(a, b), (a, b)


def _mm_bwd(res, g):
    a, b = res
    gb = g.astype(BF16)
    return _dot_nt(gb, b.astype(BF16)), _dot_tn(a.astype(BF16), gb)


_mm.defvjp(_mm_fwd, _mm_bwd)


@jax.custom_vjp
def _mm_nt(a, b):
    return _dot_nt(a.astype(BF16), b.astype(BF16))


def _mm_nt_fwd(a, b):
    return _mm_nt(a, b), (a, b)


def _mm_nt_bwd(res, g):
    a, b = res
    gb = g.astype(BF16)
    return _dot(gb, b.astype(BF16)), _dot_tn(gb, a.astype(BF16))


_mm_nt.defvjp(_mm_nt_fwd, _mm_nt_bwd)


@jax.custom_vjp
def _mm_tn(a, b):
    return _dot_tn(a.astype(BF16), b.astype(BF16))


def _mm_tn_fwd(a, b):
    return _mm_tn(a, b), (a, b)


def _mm_tn_bwd(res, g):
    a, b = res
    gb = g.astype(BF16)
    return _dot_nt(b.astype(BF16), gb), _dot(a.astype(BF16), gb)


_mm_tn.defvjp(_mm_tn_fwd, _mm_tn_bwd)


def _dot_exact_rhs(a, m):
    hi = a.astype(BF16)
    lo = (a - hi.astype(F32)).astype(BF16)
    return _dot(hi, m) + _dot(lo, m)


@jax.custom_vjp
def _group_mean(a, m):
    return _dot_exact_rhs(a, m)


def _group_mean_fwd(a, m):
    return _group_mean(a, m), m


def _group_mean_bwd(m, g):
    return _dot_exact_rhs(g, m), jnp.zeros_like(m)


_group_mean.defvjp(_group_mean_fwd, _group_mean_bwd)


def _roll_rows(a, shift):
    return pltpu.roll(a, shift, 0)


def _cumsum_rows_raw(a, reverse):
    n = a.shape[0]
    row = lax.broadcasted_iota(jnp.int32, a.shape, 0)
    s = 1
    while s < n:
        if reverse:
            a = a + jnp.where(row < n - s, _roll_rows(a, n - s), 0.0)
        else:
            a = a + jnp.where(row >= s, _roll_rows(a, s), 0.0)
        s *= 2
    return a


@functools.partial(jax.custom_vjp, nondiff_argnums=(1,))
def _cumsum_rows(a, reverse):
    return _cumsum_rows_raw(a, reverse)


def _cumsum_rows_fwd(a, reverse):
    return _cumsum_rows_raw(a, reverse), None


def _cumsum_rows_bwd(reverse, _, g):
    return (_cumsum_rows_raw(g, not reverse),)


_cumsum_rows.defvjp(_cumsum_rows_fwd, _cumsum_rows_bwd)


def _rms(x, g):
    r = lax.rsqrt(jnp.mean(x * x, axis=-1, keepdims=True) + EPS)
    return x * r * g


def _rms_bwd(x, g, dy):
    r = lax.rsqrt(jnp.mean(x * x, axis=-1, keepdims=True) + EPS)
    xh = x * r
    dg = jnp.sum(dy * xh, axis=0, keepdims=True)
    dxh = dy * g
    dx = r * (dxh - xh * jnp.mean(dxh * xh, axis=-1, keepdims=True))
    return dx, dg


def _sigmoid(a):
    return jax.nn.sigmoid(a)


def _mesh_place():
    x, y, c = lax.axis_index("x"), lax.axis_index("y"), lax.axis_index("c")
    return x, y, c


def _dev_index(p):
    return 4 * p[0] + 2 * p[1] + p[2]


def _comm_sems(n):
    return [pltpu.SemaphoreType.DMA((n, 7)), pltpu.SemaphoreType.DMA((n, 7)), pltpu.SemaphoreType.DMA((n,))]


def _gather_protocol(ins, outs, send_sems, recv_sems, local_sems):
    n = len(ins)
    x, y, c = _mesh_place()
    me, sibling = (x, y, c), (x, y, 1 - c)
    chips = [(1 - x, y), (x, 1 - y), (1 - x, 1 - y)]

    def copy(a, k, block, to, src=None):
        slot = outs[a].at[_dev_index(block)]
        return pltpu.make_async_remote_copy(
            src_ref=slot if src is None else src, dst_ref=slot,
            send_sem=send_sems.at[a, k], recv_sem=recv_sems.at[a, k], device_id=to, device_id_type=MESH)

    def mine(a):
        return pltpu.make_async_copy(ins[a], outs[a].at[_dev_index(me)], local_sems.at[a])

    def first(a):
        return [copy(a, 0, me, sibling, src=ins[a])] + [copy(a, 1 + j, me, (*chip, c), src=ins[a]) for j, chip in enumerate(chips)]

    def start():
        for a in range(n):
            mine(a).start()
            for cp in first(a):
                cp.start()

    def forward():
        for a in range(n):
            for j, chip in enumerate(chips):
                copy(a, 1 + j, (*chip, c), me).wait_recv()
                copy(a, 4 + j, (*chip, c), sibling).start()

    def finish():
        for a in range(n):
            copy(a, 0, sibling, me).wait_recv()
            for j, chip in enumerate(chips):
                copy(a, 4 + j, (*chip, 1 - c), me).wait_recv()
        for a in range(n):
            mine(a).wait()
            for cp in first(a):
                cp.wait_send()
            for j, chip in enumerate(chips):
                copy(a, 4 + j, (*chip, c), sibling).wait_send()

    return start, forward, finish


def _exchange_protocol(ins, outs, scatter, send_sems, recv_sems, local_sems):
    n = len(ins)
    x, y, c = _mesh_place()
    me = (x, y, c)
    my_id = _dev_index(me)
    rels = [(dx, dy, dc) for dx in (0, 1) for dy in (0, 1) for dc in (0, 1)][1:]

    def peer_of(rel):
        return tuple(1 - v if d else v for v, d in zip(me, rel))

    def src(a, dev):
        return ins[a].at[dev] if scatter[a] else ins[a]

    def send(a, k):
        peer = peer_of(rels[k])
        return pltpu.make_async_remote_copy(
            src_ref=src(a, _dev_index(peer)), dst_ref=outs[a].at[my_id],
            send_sem=send_sems.at[a, k], recv_sem=recv_sems.at[a, k], device_id=peer, device_id_type=MESH)

    def arrival(a, k):
        peer = peer_of(rels[k])
        return pltpu.make_async_remote_copy(
            src_ref=src(a, my_id), dst_ref=outs[a].at[_dev_index(peer)],
            send_sem=send_sems.at[a, k], recv_sem=recv_sems.at[a, k], device_id=peer, device_id_type=MESH)

    def own(a):
        return pltpu.make_async_copy(src(a, my_id), outs[a].at[my_id], local_sems.at[a])

    def start():
        for a in range(n):
            own(a).start()
            for k in range(7):
                send(a, k).start()

    def finish():
        for a in range(n):
            for k in range(7):
                arrival(a, k).wait_recv()
        for a in range(n):
            for k in range(7):
                send(a, k).wait_send()
            own(a).wait()

    return start, finish


def _slot_shapes(blocks, scatter=None):
    return [jax.ShapeDtypeStruct(b.shape if (scatter and scatter[a]) else (N_DEV,) + b.shape, b.dtype) for a, b in enumerate(blocks)]


def _all_gather_call(blocks):
    n = len(blocks)

    def body(*refs):
        start, forward, finish = _gather_protocol(refs[:n], refs[n:2 * n], *refs[2 * n:])
        start()
        forward()
        finish()

    any_spec = pl.BlockSpec(memory_space=pl.ANY)
    return pl.pallas_call(
        body, name="weights_all_gather", out_shape=_slot_shapes(blocks),
        in_specs=[any_spec] * n, out_specs=[any_spec] * n, scratch_shapes=_comm_sems(n),
    )(*blocks)


def _sum_slots_call(recvs):
    n = len(recvs)

    def body(*refs):
        for in_ref, out_ref in zip(refs[:n], refs[n:]):
            acc = in_ref[0]
            for j in range(1, N_DEV):
                acc = acc + in_ref[j]
            out_ref[...] = acc

    return pl.pallas_call(
        body, name="small_grad_sum", out_shape=[jax.ShapeDtypeStruct(r.shape[1:], F32) for r in recvs],
        compiler_params=_params(),
    )(*recvs)


def _adam_update(w, g, m, v):
    nm = ADAM_B1 * m + (1.0 - ADAM_B1) * g
    nv = ADAM_B2 * v + (1.0 - ADAM_B2) * (g * g)
    bc1 = 1.0 - ADAM_B1 ** ADAM_STEP
    bc2 = 1.0 - ADAM_B2 ** ADAM_STEP
    return -ADAM_LR * ((nm / bc1) / (jnp.sqrt(nv / bc2) + ADAM_EPS) + ADAM_WD * w), nm, nv


def _adamw_recv(w, recv, m, v, tag):
    r, c = w.shape
    tr = r
    for cand in (512, 256, 128):
        if r > cand and r % cand == 0:
            tr = cand
            break

    def body(w_ref, r_ref, m_ref, v_ref, g_ref, d_ref, nm_ref, nv_ref):
        g = r_ref[0].astype(F32)
        for j in range(1, N_DEV):
            g = g + r_ref[j].astype(F32)
        g_ref[...] = g
        d_ref[...], nm_ref[...], nv_ref[...] = _adam_update(w_ref[...], g, m_ref[...], v_ref[...])

    spec = pl.BlockSpec((tr, c), lambda i: (i, 0))
    return pl.pallas_call(
        body, name="adamw_" + tag, out_shape=[jax.ShapeDtypeStruct(w.shape, F32)] * 4, grid=(r // tr,),
        in_specs=[spec, pl.BlockSpec((N_DEV, tr, c), lambda i: (0, i, 0)), spec, spec], out_specs=[spec] * 4,
        compiler_params=_params(),
    )(w, recv, m, v)


def _adamw_recv_halves(w, recv_halves, m, v, tag):
    r, c = w.shape
    half = c // 2

    def body(w_ref, ra_ref, rb_ref, m_ref, v_ref, g_ref, d_ref, nm_ref, nv_ref):
        def update(r_ref):
            g = r_ref[0].astype(F32)
            for j in range(1, N_DEV):
                g = g + r_ref[j].astype(F32)
            g_ref[...] = g
            d_ref[...], nm_ref[...], nv_ref[...] = _adam_update(w_ref[...], g, m_ref[...], v_ref[...])

        pl.when(pl.program_id(0) == 0)(lambda: update(ra_ref))
        pl.when(pl.program_id(0) == 1)(lambda: update(rb_ref))

    spec = pl.BlockSpec((r, half), lambda j: (0, j))
    whole = pl.BlockSpec((N_DEV, r, half), lambda j: (0, 0, 0))
    return pl.pallas_call(
        body, name="adamw_" + tag, out_shape=[jax.ShapeDtypeStruct(w.shape, F32)] * 4, grid=(2,),
        in_specs=[spec, whole, whole, spec, spec], out_specs=[spec] * 4, compiler_params=_params(),
    )(w, *recv_halves, m, v)


def _adamw_recv_hosting(ws, recvs, ms, vs, blocks, scatter):
    n, ne = len(ws), len(blocks)
    rows = max(w.shape[0] for w in ws)
    cols = ws[0].shape[1]
    assert all(w.shape[1] == cols for w in ws)

    def body(*refs):
        ins, ex_in = refs[:4 * n], refs[4 * n:4 * n + ne]
        outs, ex_out = refs[4 * n + ne:8 * n + ne], refs[8 * n + ne:8 * n + 2 * ne]
        in_buf, recv_buf, out_buf, in_sems, out_sems = refs[8 * n + 2 * ne:8 * n + 2 * ne + 5]
        start, finish = _exchange_protocol(ex_in, ex_out, scatter, *refs[8 * n + 2 * ne + 5:])
        start()
        for a in range(n):
            r = pl.ds(0, ws[a].shape[0])
            loads = [pltpu.make_async_copy(ins[k * n + a], in_buf.at[j, r], in_sems.at[j]) for j, k in enumerate((0, 2, 3))]
            loads.append(pltpu.make_async_copy(ins[n + a], recv_buf.at[:, r], in_sems.at[3]))
            for cp in loads:
                cp.start()
            for cp in loads:
                cp.wait()
            g = recv_buf[0, r].astype(F32)
            for j in range(1, N_DEV):
                g = g + recv_buf[j, r].astype(F32)
            out_buf[0, r] = g
            out_buf[1, r], out_buf[2, r], out_buf[3, r] = _adam_update(in_buf[0, r], g, in_buf[1, r], in_buf[2, r])
            stores = [pltpu.make_async_copy(out_buf.at[k, r], outs[k * n + a], out_sems.at[k]) for k in range(4)]
            for cp in stores:
                cp.start()
            for cp in stores:
                cp.wait()
        finish()

    any_spec = pl.BlockSpec(memory_space=pl.ANY)
    out = pl.pallas_call(
        body, name="adamw_late_and_grad_exchange",
        out_shape=[jax.ShapeDtypeStruct(w.shape, F32) for w in ws] * 4 + _slot_shapes(blocks, scatter),
        in_specs=[any_spec] * (4 * n + ne), out_specs=[any_spec] * (4 * n + ne),
        scratch_shapes=[pltpu.VMEM((3, rows, cols), F32), pltpu.VMEM((N_DEV, rows, cols), BF16), pltpu.VMEM((4, rows, cols), F32),
                        pltpu.SemaphoreType.DMA((4,)), pltpu.SemaphoreType.DMA((4,))] + _comm_sems(ne),
        compiler_params=_params(),
    )(*ws, *recvs, *ms, *vs, *blocks)
    return out[:n], out[n:2 * n], out[2 * n:3 * n], out[3 * n:4 * n], out[4 * n:]


def _adamw_small(ws, gs, ms, vs):
    n = len(ws)

    def body(*refs):
        ins, outs = refs[:4 * n], refs[4 * n:]
        for a in range(n):
            d, nm, nv = _adam_update(ins[a][...], ins[n + a][...], ins[2 * n + a][...], ins[3 * n + a][...])
            outs[a][...], outs[n + a][...], outs[2 * n + a][...] = d, nm, nv

    out = pl.pallas_call(
        body, name="adamw_small", out_shape=[jax.ShapeDtypeStruct(w.shape, F32) for w in ws] * 3, compiler_params=_params(),
    )(*ws, *gs, *ms, *vs)
    return out[:n], out[n:2 * n], out[2 * n:]


def _rope_tables(seq):
    inv = 1.0 / (ROPE_THETA ** (jnp.arange(0, B_ROPE, 2, dtype=F32) / B_ROPE))
    ang = jnp.arange(seq, dtype=F32)[:, None] * inv[None, :]
    cos, sin = jnp.cos(ang), jnp.sin(ang)
    z32, z64 = jnp.zeros_like(cos), jnp.zeros((seq, 64), F32)
    cos_t = jnp.concatenate([cos, cos, z64], axis=1)
    sin_a = jnp.concatenate([-sin, z32, z64], axis=1)
    sin_b = jnp.concatenate([z32, sin, z64], axis=1)
    return cos_t, sin_a, sin_b


def _rope(t, cos_t, sin_a, sin_b):
    return t * cos_t + pltpu.roll(t, 96, 1) * sin_a + pltpu.roll(t, 32, 1) * sin_b


def _rope_t(d, cos_t, sin_a, sin_b):
    return d * cos_t + pltpu.roll(d * sin_a, 32, 1) + pltpu.roll(d * sin_b, 96, 1)


def _inproj_qkv(x, g1, w_in, g_qa, g_kva, w_q, w_kv, tables, seq, tm):
    t = x.shape[0]
    nblk = seq // tm
    n_plain = 7
    offs = [sum(IN_WIDTHS[:j]) for j in range(len(IN_WIDTHS))]

    def body(x_ref, g_ref, w_ref, gq_ref, gk_ref, wq_ref, wkv_ref, c_ref, sa_ref, sb_ref, *outs):
        q_out, k_out, v_out = outs[n_plain:]
        for j in range(tm // min(tm, ROW_SUB)):
            r = pl.ds(j * min(tm, ROW_SUB), min(tm, ROW_SUB))
            h = _rms(x_ref[r, :], g_ref[...]).astype(BF16)
            proj = lambda g: _dot_nt(h, w_ref[offs[g]:offs[g] + IN_WIDTHS[g], :])
            for g in range(5):
                outs[g][r, :] = proj(g)
            cq, ckv, kr = proj(5), proj(6), proj(7)
            outs[5][r, :] = cq
            outs[6][r, :] = ckv
            cos_t, sin_a, sin_b = c_ref[r, :], sa_ref[r, :], sb_ref[r, :]
            cqn = _rms(cq, gq_ref[...]).astype(BF16)
            ckn = _rms(ckv, gk_ref[...]).astype(BF16)
            kr_rot = _rope(kr, cos_t, sin_a, sin_b).astype(BF16)
            for hd in range(B_HEADS):
                lo = hd * QK_PAD
                q_out[r, lo:lo + 128] = (_dot_nt(cqn, wq_ref[lo:lo + 128, :]) * ATTN_SCALE).astype(BF16)
                qr = _rope(_dot_nt(cqn, wq_ref[lo + 128:lo + 256, :]), cos_t, sin_a, sin_b)
                q_out[r, lo + 128:lo + 256] = (qr * ATTN_SCALE).astype(BF16)
                k_out[r, lo:lo + 128] = _dot(ckn, wkv_ref[:, lo:lo + 128]).astype(BF16)
                k_out[r, lo + 128:lo + 256] = kr_rot
                v_out[r, hd * B_V:(hd + 1) * B_V] = _dot(ckn, wkv_ref[:, lo + 128:lo + 256]).astype(BF16)

    tok = lambda wd: pl.BlockSpec((tm, wd), lambda i: (i, 0))
    tab = pl.BlockSpec((tm, 128), lambda i: (i % nblk, 0))
    widths = list(IN_WIDTHS[:n_plain]) + [B_HEADS * QK_PAD, B_HEADS * QK_PAD, B_HEADS * B_V]
    dtypes = [F32] * n_plain + [BF16] * 3
    return pl.pallas_call(
        body, name="inproj_qkv_fwd", grid=(t // tm,),
        out_shape=[jax.ShapeDtypeStruct((t, wd), dt) for wd, dt in zip(widths, dtypes)],
        in_specs=[tok(D_MODEL), _const_spec((1, D_MODEL)), _const_spec((D_IN_PAD, D_MODEL)), _const_spec((1, Q_LORA)),
                  _const_spec((1, KV_LORA)), _const_spec((B_HEADS * QK_PAD, Q_LORA)), _const_spec((KV_LORA, 1024)), tab, tab, tab],
        out_specs=[tok(wd) for wd in widths],
        compiler_params=_params(),
    )(x, g1, w_in, g_qa, g_kva, w_q, w_kv, *tables)


def _step_index(nq):
    return (pl.program_id(0) * B_HEADS + pl.program_id(1)) * nq + pl.program_id(2)


def _attn_fwd(qcat, kcat, v, nb, seq, tq, gather=()):
    t = qcat.shape[0]
    nq = seq // tq
    ng = len(gather)
    steps = nb * B_HEADS * nq

    def body(q_ref, k_ref, v_ref, *rest):
        o_ref, lse_ref = rest[ng:ng + 2]
        if ng:
            start, forward, finish = _gather_protocol(rest[:ng], rest[ng + 2:2 * ng + 2], *rest[2 * ng + 2:])
            pl.when(_step_index(nq) == 0)(start)
            pl.when(_step_index(nq) == (3 * steps) // 4)(forward)
        for j in range(tq // ATTN_SUB):
            r = pl.ds(j * ATTN_SUB, ATTN_SUB)
            s = _dot_nt(q_ref[r, :], k_ref[...])
            m = jnp.max(s, axis=-1, keepdims=True)
            p = jnp.exp(s - m)
            l = jnp.sum(p, axis=-1, keepdims=True)
            o_ref[r, :] = _dot(p.astype(BF16), v_ref[...]) / l
            lse_ref[0, r, :] = m + jnp.log(l)
        if ng:
            pl.when(_step_index(nq) == steps - 1)(finish)

    any_spec = pl.BlockSpec(memory_space=pl.ANY)
    return pl.pallas_call(
        body, name="attn_fwd", grid=(nb, B_HEADS, nq),
        out_shape=[jax.ShapeDtypeStruct((t, B_HEADS * B_V), F32), jax.ShapeDtypeStruct((B_HEADS, t, 1), F32)] + _slot_shapes(gather),
        in_specs=[pl.BlockSpec((tq, QK_PAD), lambda b, h, i: (b * nq + i, h)),
                  pl.BlockSpec((seq, QK_PAD), lambda b, h, i: (b, h)),
                  pl.BlockSpec((seq, B_V), lambda b, h, i: (b, h))] + [any_spec] * ng,
        out_specs=[pl.BlockSpec((tq, B_V), lambda b, h, i: (b * nq + i, h)),
                   pl.BlockSpec((1, tq, 1), lambda b, h, i: (h, b * nq + i, 0))] + [any_spec] * ng,
        scratch_shapes=_comm_sems(ng) if ng else [],
        compiler_params=_params(),
    )(qcat, kcat, v, *gather)


def _attn_bwd(qcat, kcat, v, o, lse, do, nb, seq, tq, exchange=()):
    t = qcat.shape[0]
    nq = seq // tq
    ne = len(exchange)
    steps = nb * B_HEADS * nq

    def body(q_ref, k_ref, v_ref, o_ref, lse_ref, do_ref, *rest):
        dq_ref, dk_ref, dv_ref = rest[ne:ne + 3]
        if ne:
            start, finish = _exchange_protocol(rest[:ne], rest[ne + 3:2 * ne + 3], [True] * ne, *rest[2 * ne + 3:])
            pl.when(_step_index(nq) == 0)(start)

        @pl.when(pl.program_id(2) == 0)
        def _():
            dv_ref[...] = jnp.zeros_like(dv_ref)
            dk_ref[...] = jnp.zeros_like(dk_ref)

        for j in range(tq // ATTN_SUB_BWD):
            r = pl.ds(j * ATTN_SUB_BWD, ATTN_SUB_BWD)
            q, k = q_ref[r, :], k_ref[...]
            do_f = do_ref[r, :].astype(F32)
            delta = jnp.sum(do_f * o_ref[r, :], axis=-1, keepdims=True)
            dob = do_f.astype(BF16)
            p = jnp.exp(_dot_nt(q, k) - lse_ref[0, r, :])
            ds = (p * (_dot_nt(dob, v_ref[...]) - delta)).astype(BF16)
            dq_ref[r, :] = _dot(ds, k).astype(dq_ref.dtype)
            dv_ref[...] += _dot_tn(p.astype(BF16), dob)
            dk_ref[...] += _dot_tn(ds, q)
        if ne:
            pl.when(_step_index(nq) == steps - 1)(finish)

    qspec = lambda wd: pl.BlockSpec((tq, wd), lambda b, h, i: (b * nq + i, h))
    kspec = lambda wd: pl.BlockSpec((seq, wd), lambda b, h, i: (b, h))
    any_spec = pl.BlockSpec(memory_space=pl.ANY)
    return pl.pallas_call(
        body, name="attn_bwd", grid=(nb, B_HEADS, nq),
        out_shape=[jax.ShapeDtypeStruct((t, B_HEADS * QK_PAD), BF16), jax.ShapeDtypeStruct((t, B_HEADS * QK_PAD), F32),
                   jax.ShapeDtypeStruct((t, B_HEADS * B_V), F32)] + _slot_shapes(exchange, [True] * ne),
        in_specs=[qspec(QK_PAD), kspec(QK_PAD), kspec(B_V), qspec(B_V),
                  pl.BlockSpec((1, tq, 1), lambda b, h, i: (h, b * nq + i, 0)), qspec(B_V)] + [any_spec] * ne,
        out_specs=[qspec(QK_PAD), kspec(QK_PAD), kspec(B_V)] + [any_spec] * ne,
        scratch_shapes=_comm_sems(ne) if ne else [],
        compiler_params=_params(),
    )(qcat, kcat, v, o, lse, do, *exchange)


def _gla_consts(reverse):
    row = lax.broadcasted_iota(jnp.int32, (CHUNK, CHUNK), 0)
    col = lax.broadcasted_iota(jnp.int32, (CHUNK, CHUNK), 1)
    causal = (row <= col) if reverse else (row >= col)
    lane = lax.broadcasted_iota(jnp.int32, (1, HEAD_PAIR), 1)
    m0 = (lane < 64).astype(F32)
    m1 = 1.0 - m0
    r2 = lax.broadcasted_iota(jnp.int32, (HEAD_PAIR, HEAD_PAIR), 0)
    c2 = lax.broadcasted_iota(jnp.int32, (HEAD_PAIR, HEAD_PAIR), 1)
    same_head = ((r2 < 64) == (c2 < 64)).astype(F32)
    return causal, m0, m1, same_head


def _gla_chunk(hq, hi, z, l0, l1, st, consts, reverse):
    q_dec, k_inv, k_end, decay = _gla_gates(hq, z, l0, l1, reverse)
    o, st_new = _gla_state(q_dec, st, decay, _gla_increment(hi, k_end, consts))
    return o + _gla_intra(q_dec, k_inv, hi, consts), st_new


def _gla_gates(hq, z, l0, l1, reverse):
    mx = jnp.maximum(l0, l1)
    e0, e1 = jnp.exp(l0 - mx), jnp.exp(l1 - mx)
    lb = e0 / (e0 + e1)
    q = hq * _sigmoid(hq)
    sz = _sigmoid(z)
    log_f = jnp.log(lb + (1.0 - lb) * sz)
    k = (1.0 - lb) * (1.0 - sz)
    cum = _cumsum_rows(log_f, reverse)
    decay = jnp.exp(jnp.sum(log_f, axis=0, keepdims=True))
    k_inv = k * jnp.exp(-cum)
    return q * jnp.exp(cum), k_inv, k_inv * decay, decay


def _gla_intra(q_dec, k_inv, hi, consts):
    causal, m0, m1, _ = consts
    o = None
    for mh in (m0, m1):
        s = jnp.where(causal, _mm_nt(q_dec * mh, k_inv), 0.0)
        part = _mm(s, hi) * mh
        o = part if o is None else o + part
    return o


def _gla_increment(hi, k_end, consts):
    return _mm_tn(hi, k_end) * consts[3]


def _gla_state(q_dec, st, decay, inc):
    return _mm_nt(q_dec, st), st * decay + inc


GLA_DIRS = (False, True)
GLA_BATCH_FWD = 8
GLA_BATCH_BWD = 4


def _gla_fwd(hq, hi, zs, lbls, nb, seq, group):
    t = hq.shape[0]
    rows = group * CHUNK
    nblk = seq // rows
    n_chunks = seq // CHUNK
    nd = len(GLA_DIRS)

    def body(*refs):
        ins, outs, st_refs = refs[:4 * nd], refs[4 * nd:6 * nd], refs[6 * nd:]
        @pl.when(pl.program_id(2) == 0)
        def _():
            for st_ref in st_refs:
                st_ref[...] = jnp.zeros_like(st_ref)

        consts = [_gla_consts(rev) for rev in GLA_DIRS]
        work = [(d, rev, group - 1 - cc if rev else cc) for cc in range(group) for d, rev in enumerate(GLA_DIRS)]
        rows_of = lambda c: pl.ds(c * CHUNK, CHUNK)
        sts = [st_ref[...] for st_ref in st_refs]
        for w0 in range(0, len(work), GLA_BATCH_FWD):
            batch = work[w0:w0 + GLA_BATCH_FWD]
            gates, intra, incs = {}, {}, {}
            for d, rev, c in batch:
                hq_ref, _, z_ref, lbl_ref = ins[4 * d:4 * d + 4]
                gates[d, c] = _gla_gates(hq_ref[rows_of(c), :], z_ref[rows_of(c), :], lbl_ref[0:1, :], lbl_ref[1:2, :], rev)
            for d, rev, c in batch:
                hi_c = ins[4 * d + 1][rows_of(c), :]
                intra[d, c] = _gla_intra(gates[d, c][0], gates[d, c][1], hi_c, consts[d])
                incs[d, c] = _gla_increment(hi_c, gates[d, c][2], consts[d])
            for d, rev, c in batch:
                outs[nd + d][0, 0, c] = sts[d].astype(outs[nd + d].dtype)
                o_state, sts[d] = _gla_state(gates[d, c][0], sts[d], gates[d, c][3], incs[d, c])
                outs[d][rows_of(c), :] = (intra[d, c] + o_state).astype(outs[d].dtype)
        for st_ref, st in zip(st_refs, sts):
            st_ref[...] = st

    def tb(rev):
        return (lambda i: nblk - 1 - i) if rev else (lambda i: i)

    tok = lambda rev: pl.BlockSpec((rows, HEAD_PAIR), lambda b, p, i: (b * nblk + tb(rev)(i), p))
    lspec = pl.BlockSpec((2, HEAD_PAIR), lambda b, p, i: (0, p))
    sspec = lambda rev: pl.BlockSpec((1, 1, group, HEAD_PAIR, HEAD_PAIR), lambda b, p, i: (b, p, tb(rev)(i), 0, 0))
    args, in_specs = [], []
    for d, rev in enumerate(GLA_DIRS):
        args += [hq, hi, zs[d], lbls[d]]
        in_specs += [tok(rev), tok(rev), tok(rev), lspec]
    return pl.pallas_call(
        body, name="gla_fwd", grid=(nb, 4, nblk),
        out_shape=[jax.ShapeDtypeStruct((t, A_WIDTH), BF16)] * nd
        + [jax.ShapeDtypeStruct((nb, 4, n_chunks, HEAD_PAIR, HEAD_PAIR), BF16)] * nd,
        in_specs=in_specs, out_specs=[tok(rev) for rev in GLA_DIRS] + [sspec(rev) for rev in GLA_DIRS],
        scratch_shapes=[pltpu.VMEM((HEAD_PAIR, HEAD_PAIR), F32)] * nd,
        compiler_params=_params(),
    )(*args)


def _gla_bwd(hq, hi, zs, lbls, saved, do, nb, seq, group):
    t = hq.shape[0]
    rows = group * CHUNK
    nblk = seq // rows
    nd = len(GLA_DIRS)

    def body(*refs):
        ins, outs, dst_refs = refs[:6 * nd], refs[6 * nd:10 * nd], refs[10 * nd:]
        dl_refs = outs[3 * nd:]

        @pl.when(pl.program_id(2) == 0)
        def _():
            for dst_ref, dl_ref in zip(dst_refs, dl_refs):
                dst_ref[...] = jnp.zeros_like(dst_ref)
                dl_ref[...] = jnp.zeros_like(dl_ref)

        consts = [_gla_consts(rev) for rev in GLA_DIRS]
        dsts = [dst_ref[...] for dst_ref in dst_refs]
        dls = [[jnp.zeros((1, HEAD_PAIR), F32), jnp.zeros((1, HEAD_PAIR), F32)] for _ in GLA_DIRS]
        work = [(d, rev, cc if rev else group - 1 - cc) for cc in range(group) for d, rev in enumerate(GLA_DIRS)]
        for w0 in range(0, len(work), GLA_BATCH_BWD):
            vjps = {}
            for d, rev, c in work[w0:w0 + GLA_BATCH_BWD]:
                hq_ref, hi_ref, z_ref, lbl_ref, save_ref, _ = ins[6 * d:6 * d + 6]
                r = pl.ds(c * CHUNK, CHUNK)
                fn = functools.partial(_gla_chunk, consts=consts[d], reverse=rev)
                _, vjps[d, c] = jax.vjp(fn, hq_ref[r, :], hi_ref[r, :], z_ref[r, :], lbl_ref[0:1, :], lbl_ref[1:2, :],
                                         save_ref[0, 0, c].astype(F32))
            for d, rev, c in work[w0:w0 + GLA_BATCH_BWD]:
                dq_ref, dv_ref, dz_ref = outs[3 * d:3 * d + 3]
                r = pl.ds(c * CHUNK, CHUNK)
                d_hq, d_hi, d_z, d_l0, d_l1, dsts[d] = vjps[d, c]((ins[6 * d + 5][r, :].astype(F32), dsts[d]))
                dq_ref[r, :] = d_hq.astype(dq_ref.dtype)
                dv_ref[r, :] = d_hi.astype(dv_ref.dtype)
                dz_ref[r, :] = d_z.astype(dz_ref.dtype)
                dls[d] = [dls[d][0] + d_l0, dls[d][1] + d_l1]
        for d in range(nd):
            dst_refs[d][...] = dsts[d]
            dl_refs[d][0, 0:1, :] += dls[d][0]
            dl_refs[d][0, 1:2, :] += dls[d][1]

    def tb(rev):
        return (lambda i: i) if rev else (lambda i: nblk - 1 - i)

    tok = lambda rev: pl.BlockSpec((rows, HEAD_PAIR), lambda b, p, i: (b * nblk + tb(rev)(i), p))
    lspec = pl.BlockSpec((2, HEAD_PAIR), lambda b, p, i: (0, p))
    sspec = lambda rev: pl.BlockSpec((1, 1, group, HEAD_PAIR, HEAD_PAIR), lambda b, p, i: (b, p, tb(rev)(i), 0, 0))
    args, in_specs, out_specs = [], [], []
    for d, rev in enumerate(GLA_DIRS):
        args += [hq, hi, zs[d], lbls[d], saved[d], do]
        in_specs += [tok(rev), tok(rev), tok(rev), lspec, sspec(rev), tok(rev)]
        out_specs += [tok(rev)] * 3
    out_specs += [pl.BlockSpec((1, 2, HEAD_PAIR), lambda b, p, i: (b, 0, p))] * nd
    return pl.pallas_call(
        body, name="gla_bwd", grid=(nb, 4, nblk),
        out_shape=[jax.ShapeDtypeStruct((t, A_WIDTH), BF16)] * (3 * nd) + [jax.ShapeDtypeStruct((nb, 2, A_WIDTH), F32)] * nd,
        in_specs=in_specs, out_specs=out_specs,
        scratch_shapes=[pltpu.VMEM((HEAD_PAIR, HEAD_PAIR), F32)] * nd,
        compiler_params=_params(),
    )(*args)


def _head_mean_matrix():
    r = lax.broadcasted_iota(jnp.int32, (A_WIDTH, A_WIDTH), 0) // 64
    c = lax.broadcasted_iota(jnp.int32, (A_WIDTH, A_WIDTH), 1) // 64
    return jnp.where(r == c, 1.0 / 64.0, 0.0).astype(BF16)


def _gla_out(o_f, o_b, hg, g, mean_mat):
    o = o_f + o_b
    ms = _group_mean(o * o, mean_mat)
    return o * lax.rsqrt(ms + EPS) * g * (hg * _sigmoid(hg))


def _gla_combine(o_f, o_b, hg, g, tm):
    t = o_f.shape[0]

    def body(of_ref, ob_ref, hg_ref, g_ref, y_ref):
        y_ref[...] = _gla_out(of_ref[...].astype(F32), ob_ref[...].astype(F32), hg_ref[...], g_ref[...], _head_mean_matrix())

    tok = pl.BlockSpec((tm, A_WIDTH), lambda i: (i, 0))
    return pl.pallas_call(
        body, name="gla_combine_fwd", grid=(t // tm,), out_shape=jax.ShapeDtypeStruct((t, A_WIDTH), F32),
        in_specs=[tok, tok, tok, _const_spec((1, A_WIDTH))], out_specs=tok, compiler_params=_params(),
    )(o_f, o_b, hg, g)


def _gla_combine_bwd(o_f, o_b, hg, g, dy, tm):
    t = o_f.shape[0]

    def body(of_ref, ob_ref, hg_ref, g_ref, dy_ref, do_ref, dhg_ref, dg_ref):
        mean_mat = _head_mean_matrix()
        fn = lambda o, hgv, gv: _gla_out(o, jnp.zeros_like(o), hgv, gv, mean_mat)
        _, vjp = jax.vjp(fn, of_ref[...].astype(F32) + ob_ref[...].astype(F32), hg_ref[...], g_ref[...])
        d_o, d_hg, d_g = vjp(dy_ref[...].astype(F32))
        do_ref[...] = d_o.astype(do_ref.dtype)
        dhg_ref[...] = d_hg.astype(dhg_ref.dtype)

        @pl.when(pl.program_id(0) == 0)
        def _():
            dg_ref[...] = jnp.zeros_like(dg_ref)

        dg_ref[...] += d_g

    tok = pl.BlockSpec((tm, A_WIDTH), lambda i: (i, 0))
    vec = pl.BlockSpec((1, A_WIDTH), lambda i: (0, 0))
    return pl.pallas_call(
        body, name="gla_combine_bwd", grid=(t // tm,),
        out_shape=[jax.ShapeDtypeStruct((t, A_WIDTH), BF16), jax.ShapeDtypeStruct((t, A_WIDTH), BF16),
                   jax.ShapeDtypeStruct((1, A_WIDTH), F32)],
        in_specs=[tok, tok, tok, _const_spec((1, A_WIDTH)), tok], out_specs=[tok, tok, vec], compiler_params=_params(),
    )(o_f, o_b, hg, g, dy)


def _post_fwd(x, ya, oattn, tgt, g_mla, w_out, g2, w_gate, w_up, w_down, g_fin, tm):
    t = x.shape[0]

    def body(x_ref, ya_ref, oa_ref, tgt_ref, gm_ref, wo_ref, g2_ref, wg_ref, wu_ref, wd_ref, gf_ref,
             x1_ref, x2_ref, gate_ref, up_ref, loss_ref):
        part = jnp.zeros((1, 1), F32)
        for j in range(tm // min(tm, ROW_SUB)):
            r = pl.ds(j * min(tm, ROW_SUB), min(tm, ROW_SUB))
            yb = _rms(oa_ref[r, :], gm_ref[...])
            x1 = x_ref[r, :] + _dot(ya_ref[r, :].astype(BF16), wo_ref[0:A_WIDTH, :]) + _dot(yb.astype(BF16), wo_ref[A_WIDTH:, :])
            x1_ref[r, :] = x1
            h2 = _rms(x1, g2_ref[...]).astype(BF16)
            gate, up = _dot_nt(h2, wg_ref[...]), _dot_nt(h2, wu_ref[...])
            gate_ref[r, :] = gate.astype(BF16)
            up_ref[r, :] = up.astype(BF16)
            act = (gate * _sigmoid(gate) * up).astype(BF16)
            x2 = x1 + _dot(act, wd_ref[...])
            x2_ref[r, :] = x2
            err = _rms(x2, gf_ref[...]) - tgt_ref[r, :]
            part = part + 0.5 * jnp.sum(jnp.mean(err * err, axis=-1, keepdims=True), axis=0, keepdims=True)

        @pl.when(pl.program_id(0) == 0)
        def _():
            loss_ref[...] = jnp.zeros_like(loss_ref)

        loss_ref[...] += jnp.broadcast_to(part, loss_ref.shape)

    tok = lambda wd: pl.BlockSpec((tm, wd), lambda i: (i, 0))
    return pl.pallas_call(
        body, name="post_fwd", grid=(t // tm,),
        out_shape=[jax.ShapeDtypeStruct((t, D_MODEL), F32)] * 2 + [jax.ShapeDtypeStruct((t, D_FF), BF16)] * 2
        + [jax.ShapeDtypeStruct((1, 128), F32)],
        in_specs=[tok(D_MODEL), tok(A_WIDTH), tok(512), tok(D_MODEL), _const_spec((1, 512)), _const_spec((D_MODEL, D_MODEL)),
                  _const_spec((1, D_MODEL)), _const_spec((D_FF, D_MODEL)), _const_spec((D_FF, D_MODEL)),
                  _const_spec((D_FF, D_MODEL)), _const_spec((1, D_MODEL))],
        out_specs=[tok(D_MODEL), tok(D_MODEL), tok(D_FF), tok(D_FF), pl.BlockSpec((1, 128), lambda i: (0, 0))],
        compiler_params=_params(),
    )(x, ya, oattn, tgt, g_mla, w_out, g2, w_gate, w_up, w_down, g_fin)


def _post_bwd(x1, x2, gate_b, up_b, ya, oattn, tgt, g_mla, w_out, g2, w_gate, w_up, w_down, g_fin, tm):
    t = x1.shape[0]

    def body(x1_ref, x2_ref, gate_ref, up_ref, ya_ref, oa_ref, tgt_ref, gm_ref, wo_ref, g2_ref, wg_ref, wu_ref, wd_ref, gf_ref,
             dx1_ref, dya_ref, doa_ref, ycat_ref, dx1b_ref, h2_ref, dgate_ref, dup_ref, act_ref, dx2b_ref,
             dgm_ref, dg2_ref, dgf_ref):
        x1, x2 = x1_ref[...], x2_ref[...]
        dy = (_rms(x2, gf_ref[...]) - tgt_ref[...]) * (1.0 / D_MODEL)
        dx2, dgf = _rms_bwd(x2, gf_ref[...], dy)
        dx2b = dx2.astype(BF16)
        dx2b_ref[...] = dx2b
        h2_ref[...] = _rms(x1, g2_ref[...]).astype(BF16)
        gate, up = gate_ref[...].astype(F32), up_ref[...].astype(F32)
        sg = _sigmoid(gate)
        sl = gate * sg
        act_ref[...] = (sl * up).astype(BF16)
        dact = _dot_nt(dx2b, wd_ref[...])
        dup = (dact * sl).astype(BF16)
        dgate = (dact * up * (sg * (1.0 + gate * (1.0 - sg)))).astype(BF16)
        dup_ref[...] = dup
        dgate_ref[...] = dgate
        dh2 = _dot(dgate, wg_ref[...]) + _dot(dup, wu_ref[...])
        dx1n, dg2 = _rms_bwd(x1, g2_ref[...], dh2)
        dx1 = dx2 + dx1n
        dx1_ref[...] = dx1
        dx1b = dx1.astype(BF16)
        dx1b_ref[...] = dx1b
        oa = oa_ref[...]
        ycat_ref[:, 0:A_WIDTH] = ya_ref[...].astype(BF16)
        ycat_ref[:, A_WIDTH:] = _rms(oa, gm_ref[...]).astype(BF16)
        dya_ref[...] = _dot_nt(dx1b, wo_ref[0:A_WIDTH, :]).astype(dya_ref.dtype)
        doa, dgm = _rms_bwd(oa, gm_ref[...], _dot_nt(dx1b, wo_ref[A_WIDTH:, :]))
        doa_ref[...] = doa.astype(doa_ref.dtype)

        @pl.when(pl.program_id(0) == 0)
        def _():
            dgm_ref[...] = jnp.zeros_like(dgm_ref)
            dg2_ref[...] = jnp.zeros_like(dg2_ref)
            dgf_ref[...] = jnp.zeros_like(dgf_ref)

        dgm_ref[...] += dgm
        dg2_ref[...] += dg2
        dgf_ref[...] += dgf

    tok = lambda wd: pl.BlockSpec((tm, wd), lambda i: (i, 0))
    vec = lambda wd: pl.BlockSpec((1, wd), lambda i: (0, 0))
    sds = lambda wd, dt: jax.ShapeDtypeStruct((t, wd), dt)
    return pl.pallas_call(
        body, name="post_bwd", grid=(t // tm,),
        out_shape=[sds(D_MODEL, F32), sds(512, BF16), sds(512, BF16), sds(D_MODEL, BF16), sds(D_MODEL, BF16), sds(D_MODEL, BF16),
                   sds(D_FF, BF16), sds(D_FF, BF16), sds(D_FF, BF16), sds(D_MODEL, BF16),
                   jax.ShapeDtypeStruct((1, 512), F32), jax.ShapeDtypeStruct((1, D_MODEL), F32), jax.ShapeDtypeStruct((1, D_MODEL), F32)],
        in_specs=[tok(D_MODEL), tok(D_MODEL), tok(D_FF), tok(D_FF), tok(512), tok(512), tok(D_MODEL), _const_spec((1, 512)),
                  _const_spec((D_MODEL, D_MODEL)), _const_spec((1, D_MODEL)), _const_spec((D_FF, D_MODEL)),
                  _const_spec((D_FF, D_MODEL)), _const_spec((D_FF, D_MODEL)), _const_spec((1, D_MODEL))],
        out_specs=[tok(D_MODEL), tok(512), tok(512), tok(D_MODEL), tok(D_MODEL), tok(D_MODEL), tok(D_FF), tok(D_FF), tok(D_FF),
                   tok(D_MODEL), vec(512), vec(D_MODEL), vec(D_MODEL)],
        compiler_params=_params(),
    )(x1, x2, gate_b, up_b, ya, oattn, tgt, g_mla, w_out, g2, w_gate, w_up, w_down, g_fin)


def _matmul_tn(a, b, tn, tt, tag, b_cols=None, k_out=None, exchange=()):
    t, k = a.shape
    c0, n = (0, b.shape[1]) if b_cols is None else b_cols
    k_out = k if k_out is None else k_out
    last = t // tt - 1
    ne = len(exchange)
    n_j = n // tn

    def body(a_ref, b_ref, *rest):
        o_ref, acc_ref = rest[ne], rest[2 * ne + 1]
        if ne:
            start, finish = _exchange_protocol(rest[:ne], rest[ne + 1:2 * ne + 1], [True] * ne, *rest[2 * ne + 2:])
            pl.when((pl.program_id(0) == 0) & (pl.program_id(1) == 0))(start)
        part = _dot_tn(a_ref[...], b_ref[...])

        @pl.when(pl.program_id(1) == 0)
        def _():
            acc_ref[...] = part

        @pl.when(pl.program_id(1) > 0)
        def _():
            acc_ref[...] += part

        @pl.when(pl.program_id(1) == last)
        def _():
            o_ref[...] = acc_ref[0:k_out, :].astype(o_ref.dtype)

        if ne:
            pl.when((pl.program_id(0) == n_j - 1) & (pl.program_id(1) == last))(finish)

    any_spec = pl.BlockSpec(memory_space=pl.ANY)
    out = pl.pallas_call(
        body, name="wgrad_" + tag, grid=(n_j, t // tt),
        out_shape=[jax.ShapeDtypeStruct((k_out, n), BF16)] + _slot_shapes(exchange, [True] * ne),
        in_specs=[pl.BlockSpec((tt, k), lambda j, i: (i, 0)), pl.BlockSpec((tt, tn), lambda j, i: (i, j + c0 // tn))]
        + [any_spec] * ne,
        out_specs=[pl.BlockSpec((k_out, tn), lambda j, i: (0, j))] + [any_spec] * ne,
        scratch_shapes=[pltpu.VMEM((k, tn), F32)] + (_comm_sems(ne) if ne else []),
        compiler_params=_params(),
    )(a, b, *exchange)
    return out if ne else out[0]


def _inproj_qkv_bwd(x, g1, w_in, dx1, pieces, cq, ckv, g_qa, g_kva, w_q, w_kv, tables, dq, dk, dv, seq, tm):
    t = x.shape[0]
    nblk = seq // tm
    counts = [len(p) for p in pieces]
    flat = [a for p in pieces for a in p]
    n_flat = len(flat)
    offs = [sum(IN_WIDTHS[:j]) for j in range(len(IN_WIDTHS))]

    def body(x_ref, g_ref, w_ref, dx1_ref, cq_ref, ckv_ref, gq_ref, gk_ref, wq_ref, wkv_ref, c_ref, sa_ref, sb_ref,
             dq_ref, dk_ref, dv_ref, *refs):
        ins = refs[:n_flat]
        dx_ref, h_ref, dp_ref, cqn_ref, dqf_ref, ckn_ref, dkv_ref, dg_ref, dgq_ref, dgk_ref = refs[n_flat:]
        cos_t, sin_a, sin_b = c_ref[...], sa_ref[...], sb_ref[...]
        cqn_ref[...] = _rms(cq_ref[...], gq_ref[...]).astype(BF16)
        ckn_ref[...] = _rms(ckv_ref[...], gk_ref[...]).astype(BF16)
        dkr = jnp.zeros((tm, 128), F32)
        for hd in range(B_HEADS):
            lo = hd * QK_PAD
            dqf_ref[:, lo:lo + 128] = (dq_ref[:, lo:lo + 128].astype(F32) * ATTN_SCALE).astype(BF16)
            dq_rope = dq_ref[:, lo + 128:lo + 256].astype(F32) * ATTN_SCALE
            dqf_ref[:, lo + 128:lo + 256] = _rope_t(dq_rope, cos_t, sin_a, sin_b).astype(BF16)
            dkv_ref[:, lo:lo + 128] = dk_ref[:, lo:lo + 128].astype(BF16)
            dkv_ref[:, lo + 128:lo + 256] = dv_ref[:, hd * B_V:(hd + 1) * B_V].astype(BF16)
            dkr = dkr + dk_ref[:, lo + 128:lo + 256]
        dcq, dgq = _rms_bwd(cq_ref[...], gq_ref[...], _dot(dqf_ref[...], wq_ref[...]))
        dckv, dgk = _rms_bwd(ckv_ref[...], gk_ref[...], _dot_nt(dkv_ref[...], wkv_ref[...]))
        dp_ref[:, offs[5]:offs[6]] = dcq.astype(BF16)
        dp_ref[:, offs[6]:offs[7]] = dckv.astype(BF16)
        dp_ref[:, offs[7]:] = _rope_t(dkr, cos_t, sin_a, sin_b).astype(BF16)
        j = 0
        for g, cnt in enumerate(counts):
            acc = ins[j][...].astype(F32)
            for jj in range(1, cnt):
                acc = acc + ins[j + jj][...].astype(F32)
            dp_ref[:, offs[g]:offs[g] + IN_WIDTHS[g]] = acc.astype(BF16)
            j += cnt
        xv = x_ref[...]
        h_ref[...] = _rms(xv, g_ref[...]).astype(BF16)
        dxn, dg = _rms_bwd(xv, g_ref[...], _dot(dp_ref[...], w_ref[...]))
        dx_ref[...] = dx1_ref[...] + dxn

        @pl.when(pl.program_id(0) == 0)
        def _():
            dg_ref[...] = jnp.zeros_like(dg_ref)
            dgq_ref[...] = jnp.zeros_like(dgq_ref)
            dgk_ref[...] = jnp.zeros_like(dgk_ref)

        dg_ref[...] += dg
        dgq_ref[...] += dgq
        dgk_ref[...] += dgk

    tok = lambda wd: pl.BlockSpec((tm, wd), lambda i: (i, 0))
    vec = lambda wd: pl.BlockSpec((1, wd), lambda i: (0, 0))
    tab = pl.BlockSpec((tm, 128), lambda i: (i % nblk, 0))
    sds = lambda wd, dt: jax.ShapeDtypeStruct((t, wd), dt)
    return pl.pallas_call(
        body, name="inproj_qkv_bwd", grid=(t // tm,),
        out_shape=[sds(D_MODEL, F32), sds(D_MODEL, BF16), sds(D_IN_PAD, BF16), sds(Q_LORA, BF16), sds(1024, BF16),
                   sds(KV_LORA, BF16), sds(1024, BF16), jax.ShapeDtypeStruct((1, D_MODEL), F32),
                   jax.ShapeDtypeStruct((1, Q_LORA), F32), jax.ShapeDtypeStruct((1, KV_LORA), F32)],
        in_specs=[tok(D_MODEL), _const_spec((1, D_MODEL)), _const_spec((D_IN_PAD, D_MODEL)), tok(D_MODEL), tok(Q_LORA),
                  tok(KV_LORA), _const_spec((1, Q_LORA)), _const_spec((1, KV_LORA)), _const_spec((1024, Q_LORA)),
                  _const_spec((KV_LORA, 1024)), tab, tab, tab, tok(1024), tok(1024), tok(512)] + [tok(512)] * n_flat,
        out_specs=[tok(D_MODEL), tok(D_MODEL), tok(D_IN_PAD), tok(Q_LORA), tok(1024), tok(KV_LORA), tok(1024),
                   vec(D_MODEL), vec(Q_LORA), vec(KV_LORA)],
        compiler_params=_params(),
    )(x, g1, w_in, dx1, cq, ckv, g_qa, g_kva, w_q, w_kv, *tables, dq, dk, dv, *flat)


def _cols_from_slots(g):
    n, r, cs = g.shape
    return g.transpose(1, 0, 2).reshape(r, n * cs)


def _cols_to_slots(full):
    r, c = full.shape
    return full.reshape(r, N_DEV, c // N_DEV).transpose(1, 0, 2)


def _arrange_w_in_t(w_in_t):
    return jnp.concatenate([w_in_t, jnp.zeros((D_IN_PAD - D_IN, D_MODEL), w_in_t.dtype)], axis=0)


def _arrange_w_q_t(w_q_t):
    q3 = w_q_t.reshape(B_HEADS, B_NOPE + B_ROPE, Q_LORA)
    pad = jnp.zeros((B_HEADS, QK_PAD - B_NOPE - B_ROPE, Q_LORA), w_q_t.dtype)
    return jnp.concatenate([q3, pad], axis=1).reshape(B_HEADS * QK_PAD, Q_LORA)


def _unarrange_w_q_t(d_q_t):
    return d_q_t.reshape(B_HEADS, QK_PAD, Q_LORA)[:, :B_NOPE + B_ROPE].reshape(B_HEADS * (B_NOPE + B_ROPE), Q_LORA)


def _step_core(x, loss_target, small_w, lb_full, early_full, late, seq, group, tiles, distributed):
    g1, g_hgrn, g_qa, g_kva, g_mla, g2, g_fin = small_w
    w_in, w_q, w_kv = _arrange_w_in_t(early_full[0]), _arrange_w_q_t(early_full[1]), early_full[2]
    nb = x.shape[0]
    t = nb * seq
    tm, tm_fwd, tq_f, tq_b, tt = tiles
    xt = x.reshape(t, D_MODEL)
    tgt = loss_target.reshape(t, D_MODEL)
    tables = _rope_tables(seq)

    hq, hi, zf, zb, hg, cq, ckv, qcat, kcat, vv = _inproj_qkv(xt, g1, w_in, g_qa, g_kva, w_q, w_kv, tables, seq, tm_fwd)
    if distributed:
        oattn, lse, *late_slots = _attn_fwd(qcat, kcat, vv, nb, seq, tq_f, gather=tuple(late))
    else:
        oattn, lse = _attn_fwd(qcat, kcat, vv, nb, seq, tq_f)
        late_slots = late
    w_out = late_slots[0].reshape(D_MODEL, D_MODEL)
    w_gate, w_up = late_slots[1].reshape(D_FF, D_MODEL), late_slots[2].reshape(D_FF, D_MODEL)
    w_down = late_slots[3].reshape(D_FF, D_MODEL)
    lbl_f, lbl_b = lb_full[0], lb_full[1]
    o_f, o_b, save_f, save_b = _gla_fwd(hq, hi, (zf, zb), (lbl_f, lbl_b), nb, seq, group)
    ya = _gla_combine(o_f, o_b, hg, g_hgrn, tm_fwd)
    x1, x2, gate_b, up_b, loss_row = _post_fwd(xt, ya, oattn, tgt, g_mla, w_out, g2, w_gate, w_up, w_down, g_fin, tm_fwd)

    (dx1, d_ya, d_oattn, ycat_b, dx1_b, h2_b, dgate_b, dup_b, act_b, dx2_b, d_g_mla, d_g2, d_g_fin) = _post_bwd(
        x1, x2, gate_b, up_b, ya, oattn, tgt, g_mla, w_out, g2, w_gate, w_up, w_down, g_fin, tm)
    d_w_gate = _matmul_tn(dgate_b, h2_b, 512, tt, "gate")
    d_w_up = _matmul_tn(dup_b, h2_b, 512, tt, "up")
    d_w_down = _matmul_tn(act_b, dx2_b, 512, tt, "down")
    d_w_out = _matmul_tn(ycat_b, dx1_b, D_MODEL, tt, "out")
    late_g = [d_w_out.reshape(N_DEV, D_MODEL // N_DEV, D_MODEL)] + [
        g.reshape(N_DEV, D_FF // N_DEV, D_MODEL) for g in (d_w_gate, d_w_up, d_w_down)]
    if distributed:
        dq, dk, dv, *late_g = _attn_bwd(qcat, kcat, vv, oattn, lse, d_oattn, nb, seq, tq_b, exchange=tuple(late_g))
    else:
        dq, dk, dv = _attn_bwd(qcat, kcat, vv, oattn, lse, d_oattn, nb, seq, tq_b)
    d_o, d_hg, d_g_hgrn = _gla_combine_bwd(o_f, o_b, hg, g_hgrn, d_ya, tm_fwd)
    dq_f, dv_f, dz_f, dq_b, dv_b, dz_b, dl_f, dl_b = _gla_bwd(
        hq, hi, (zf, zb), (lbl_f, lbl_b), (save_f, save_b), d_o, nb, seq, group)
    grad_x, h1_b, dproj_b, cqn_b, dqf_b, ckn_b, dkv_b, d_g1, d_g_qa, d_g_kva = _inproj_qkv_bwd(
        xt, g1, w_in, dx1, [[dq_f, dq_b], [dv_f, dv_b], [dz_f], [dz_b], [d_hg]], cq, ckv, g_qa, g_kva, w_q, w_kv, tables,
        dq, dk, dv, seq, tm_fwd)
    d_w_q = _matmul_tn(dqf_b, cqn_b, Q_LORA, tt, "q_b")
    d_w_kv = _matmul_tn(ckn_b, dkv_b, B_HEADS * (B_NOPE + B_V), tt, "kv_b")
    half = D_MODEL // 2
    in_slots = lambda g: g.reshape(N_DEV, D_IN // N_DEV, half)
    g_in_a = in_slots(_matmul_tn(dproj_b, h1_b, half, tt, "in_a", b_cols=(0, half), k_out=D_IN))
    if distributed:
        d_w_in_b, g_in_a = _matmul_tn(dproj_b, h1_b, half, tt, "in_b", b_cols=(half, half), k_out=D_IN, exchange=(g_in_a,))
    else:
        d_w_in_b = _matmul_tn(dproj_b, h1_b, half, tt, "in_b", b_cols=(half, half), k_out=D_IN)

    early_g = [in_slots(d_w_in_b), _unarrange_w_q_t(d_w_q).reshape(N_DEV, 768 // N_DEV, Q_LORA), _cols_to_slots(d_w_kv)]
    d_lb = jnp.stack([jnp.sum(dl_f, axis=0), jnp.sum(dl_b, axis=0)], axis=0)
    small_grads = [d_g1, d_g_hgrn, d_g_qa, d_g_kva, d_g_mla, d_g2, d_g_fin]
    return loss_row, grad_x.reshape(nb, seq, D_MODEL), g_in_a, early_g, late_g, small_grads, d_lb


def kernel(x, norm1_g, w_in, lb_logits, hgrn_norm_g, q_a_norm_g, w_q_b, kv_a_norm_g, w_kv_b, mla_norm_g, w_out, norm2_g, w_gate, w_up, w_down, final_norm_g, loss_target, m_norm1_g, m_w_in, m_lb_logits, m_hgrn_norm_g, m_q_a_norm_g, m_w_q_b, m_kv_a_norm_g, m_w_kv_b, m_mla_norm_g, m_w_out, m_norm2_g, m_w_gate, m_w_up, m_w_down, m_final_norm_g, v_norm1_g, v_w_in, v_lb_logits, v_hgrn_norm_g, v_q_a_norm_g, v_w_q_b, v_kv_a_norm_g, v_w_kv_b, v_mla_norm_g, v_w_out, v_norm2_g, v_w_gate, v_w_up, v_w_down, v_final_norm_g):
    big_w = [w_in, w_q_b, w_kv_b, w_out, w_gate, w_up, w_down]
    big_m = [m_w_in, m_w_q_b, m_w_kv_b, m_w_out, m_w_gate, m_w_up, m_w_down]
    big_v = [v_w_in, v_w_q_b, v_w_kv_b, v_w_out, v_w_gate, v_w_up, v_w_down]
    small_w = [norm1_g, hgrn_norm_g, q_a_norm_g, kv_a_norm_g, mla_norm_g, norm2_g, final_norm_g]
    small_m = [m_norm1_g, m_hgrn_norm_g, m_q_a_norm_g, m_kv_a_norm_g, m_mla_norm_g, m_norm2_g, m_final_norm_g]
    small_v = [v_norm1_g, v_hgrn_norm_g, v_q_a_norm_g, v_kv_a_norm_g, v_mla_norm_g, v_norm2_g, v_final_norm_g]
    seq = x.shape[1]
    my_id = 4 * lax.axis_index("x") + 2 * lax.axis_index("y") + lax.axis_index("c")

    shard = lambda w: w[0].astype(BF16)
    col_t = lambda w: jnp.swapaxes(w, 1, 2)[0]
    shard_t = lambda w: col_t(w).astype(BF16)
    g_in, g_q, g_kv, g_lb = _all_gather_call([shard_t(w_in), shard_t(w_q_b), shard(w_kv_b), lb_logits.reshape(4, 64)])
    early_full = (g_in.reshape(D_IN, D_MODEL), g_q.reshape(768, Q_LORA), _cols_from_slots(g_kv))
    lb_full = g_lb.reshape(N_DEV, 2, 2, 64).transpose(1, 2, 0, 3).reshape(2, 2, 512)

    as_row = lambda a: a.reshape(1, -1)
    loss_row, grad_x, recv_in_a, early_g, late_recv, small_g, d_lb = _step_core(
        x, loss_target, [as_row(s) for s in small_w], lb_full, early_full,
        [shard(w_out), shard_t(w_gate), shard_t(w_up), shard(w_down)], seq, min(16, seq // CHUNK),
        (256, 512, min(1024, seq), min(1024, seq), min(2048, 2 * seq)), True)

    grads, deltas, new_ms, new_vs = {}, {}, {}, {}
    views = {name: (col_t if name in ("w_in", "w_q_b", "w_gate", "w_up") else (lambda a: a[0])) for name, _, _, _ in BIG}
    backs = {name: ((lambda a: jnp.swapaxes(a[None], 1, 2)) if name in ("w_in", "w_q_b", "w_gate", "w_up") else (lambda a: a[None]))
             for name, _, _, _ in BIG}
    by_name = {name: (w, m, v) for (name, _, _, _), w, m, v in zip(BIG, big_w, big_m, big_v)}
    late_names = ["w_out", "w_gate", "w_up", "w_down"]
    n_small = len(small_g)
    g_l, d_l, nm_l, nv_l, recv = _adamw_recv_hosting(
        [views[n](by_name[n][0]) for n in late_names], list(late_recv), [views[n](by_name[n][1]) for n in late_names],
        [views[n](by_name[n][2]) for n in late_names],
        early_g + small_g + [d_lb.reshape(4, 512), loss_row], [True] * 3 + [False] * (n_small + 2))
    for i, name in enumerate(late_names):
        grads[name], deltas[name], new_ms[name], new_vs[name] = (backs[name](a[i]) for a in (g_l, d_l, nm_l, nv_l))
    sums = _sum_slots_call(recv[3:])
    g_small = [g.reshape(s.shape) for g, s in zip(sums[:n_small], small_w)]
    g_lb_own = lax.dynamic_index_in_dim(sums[n_small].reshape(2, 2, N_DEV, 64), my_id, axis=2, keepdims=False)
    loss = sums[n_small + 1][0, 0]

    for name, r in zip(["w_in", "w_q_b", "w_kv_b"], recv[:3]):
        w, m, v = (views[name](a) for a in by_name[name])
        g, d, nm, nv = _adamw_recv_halves(w, (recv_in_a, r), m, v, name) if name == "w_in" else _adamw_recv(w, r, m, v, name)
        grads[name], deltas[name], new_ms[name], new_vs[name] = (backs[name](a) for a in (g, d, nm, nv))
    lb_rows = lambda a: a.reshape(4, 64)
    d_s, nm_s, nv_s = _adamw_small(
        [as_row(a) for a in small_w] + [lb_rows(lb_logits)], [as_row(a) for a in g_small] + [lb_rows(g_lb_own)],
        [as_row(a) for a in small_m] + [lb_rows(m_lb_logits)], [as_row(a) for a in small_v] + [lb_rows(v_lb_logits)])
    for i, (s, (name, _)) in enumerate(zip(small_w + [lb_logits], SMALL + (("lb_logits", 0),))):
        grads[name] = (g_small + [g_lb_own])[i]
        deltas[name], new_ms[name], new_vs[name] = d_s[i].reshape(s.shape), nm_s[i].reshape(s.shape), nv_s[i].reshape(s.shape)

    order = ["norm1_g", "w_in", "lb_logits", "hgrn_norm_g", "q_a_norm_g", "w_q_b", "kv_a_norm_g", "w_kv_b", "mla_norm_g",
             "w_out", "norm2_g", "w_gate", "w_up", "w_down", "final_norm_g"]
    return (loss, grad_x, *[grads[n] for n in order], *[deltas[n] for n in order],
            *[new_ms[n] for n in order], *[new_vs[n] for n in order])
```

```python
import functools

import jax
import jax.numpy as jnp
from jax import lax
from jax.experimental import pallas as pl
from jax.experimental.pallas import tpu as pltpu

F32 = jnp.float32
BF16 = jnp.bfloat16

N_DEV = 8
D_MODEL = 1024
D_FF = 2816
A_WIDTH = 512
HEAD_PAIR = 128
CHUNK = 64
B_HEADS = 4
B_NOPE = 128
B_ROPE = 64
B_V = 128
QK_PAD = 256
Q_LORA = 384
KV_LORA = 256
D_IN = 3264
D_IN_PAD = 3328
IN_WIDTHS = (512, 512, 512, 512, 512, Q_LORA, KV_LORA, 128)
ROPE_THETA = 10000.0
EPS = 1e-6
ATTN_SCALE = (B_NOPE + B_ROPE) ** -0.5
ATTN_SUB = 256
ATTN_SUB_BWD = 256
ROW_SUB = 256
ADAM_LR, ADAM_B1, ADAM_B2, ADAM_EPS, ADAM_WD, ADAM_STEP = 0.001, 0.9, 0.999, 1e-08, 0.01, 10
VMEM_LIMIT = 60 * 1024 * 1024
MESH = pl.DeviceIdType.MESH

BIG = (("w_in", 1024, D_IN, 1), ("w_q_b", Q_LORA, 768, 1), ("w_kv_b", KV_LORA, 1024, 1), ("w_out", 1024, 1024, 0),
       ("w_gate", 1024, D_FF, 1), ("w_up", 1024, D_FF, 1), ("w_down", D_FF, 1024, 0))
SMALL = (("norm1_g", 1024), ("hgrn_norm_g", 512), ("q_a_norm_g", 384), ("kv_a_norm_g", 256), ("mla_norm_g", 512),
         ("norm2_g", 1024), ("final_norm_g", 1024))


def _params(**kw):
    return pltpu.CompilerParams(vmem_limit_bytes=VMEM_LIMIT, **kw)


def _const_spec(shape):
    return pl.BlockSpec(shape, lambda *_: (0,) * len(shape), pipeline_mode=pl.Buffered(1))


def _dot(a, b):
    return jnp.dot(a, b, preferred_element_type=F32)


def _dot_nt(a, b):
    return lax.dot_general(a, b, (((1,), (1,)), ((), ())), preferred_element_type=F32)


def _dot_tn(a, b):
    return lax.dot_general(a, b, (((0,), (0,)), ((), ())), preferred_element_type=F32)


@jax.custom_vjp
def _mm(a, b):
    return _dot(a.astype(BF16), b.astype(BF16))


def _mm_fwd(a, b):
    return _mm(a, b), (a, b)


def _mm_bwd(res, g):
    a, b = res
    gb = g.astype(BF16)
    return _dot_nt(gb, b.astype(BF16)), _dot_tn(a.astype(BF16), gb)


_mm.defvjp(_mm_fwd, _mm_bwd)


@jax.custom_vjp
def _mm_nt(a, b):
    return _dot_nt(a.astype(BF16), b.astype(BF16))


def _mm_nt_fwd(a, b):
    return _mm_nt(a, b), (a, b)


def _mm_nt_bwd(res, g):
    a, b = res
    gb = g.astype(BF16)
    return _dot(gb, b.astype(BF16)), _dot_tn(gb, a.astype(BF16))


_mm_nt.defvjp(_mm_nt_fwd, _mm_nt_bwd)


@jax.custom_vjp
def _mm_tn(a, b):
    return _dot_tn(a.astype(BF16), b.astype(BF16))


def _mm_tn_fwd(a, b):
    return _mm_tn(a, b), (a, b)


def _mm_tn_bwd(res, g):
    a, b = res
    gb = g.astype(BF16)
    return _dot_nt(b.astype(BF16), gb), _dot(a.astype(BF16), gb)


_mm_tn.defvjp(_mm_tn_fwd, _mm_tn_bwd)


def _dot_exact_rhs(a, m):
    hi = a.astype(BF16)
    lo = (a - hi.astype(F32)).astype(BF16)
    return _dot(hi, m) + _dot(lo, m)


@jax.custom_vjp
def _group_mean(a, m):
    return _dot_exact_rhs(a, m)


def _group_mean_fwd(a, m):
    return _group_mean(a, m), m


def _group_mean_bwd(m, g):
    return _dot_exact_rhs(g, m), jnp.zeros_like(m)


_group_mean.defvjp(_group_mean_fwd, _group_mean_bwd)


def _roll_rows(a, shift):
    return pltpu.roll(a, shift, 0)


def _cumsum_rows_raw(a, reverse):
    n = a.shape[0]
    row = lax.broadcasted_iota(jnp.int32, a.shape, 0)
    s = 1
    while s < n:
        if reverse:
            a = a + jnp.where(row < n - s, _roll_rows(a, n - s), 0.0)
        else:
            a = a + jnp.where(row >= s, _roll_rows(a, s), 0.0)
        s *= 2
    return a


@functools.partial(jax.custom_vjp, nondiff_argnums=(1,))
def _cumsum_rows(a, reverse):
    return _cumsum_rows_raw(a, reverse)


def _cumsum_rows_fwd(a, reverse):
    return _cumsum_rows_raw(a, reverse), None


def _cumsum_rows_bwd(reverse, _, g):
    return (_cumsum_rows_raw(g, not reverse),)


_cumsum_rows.defvjp(_cumsum_rows_fwd, _cumsum_rows_bwd)


def _rms(x, g):
    r = lax.rsqrt(jnp.mean(x * x, axis=-1, keepdims=True) + EPS)
    return x * r * g


def _rms_bwd(x, g, dy):
    r = lax.rsqrt(jnp.mean(x * x, axis=-1, keepdims=True) + EPS)
    xh = x * r
    dg = jnp.sum(dy * xh, axis=0, keepdims=True)
    dxh = dy * g
    dx = r * (dxh - xh * jnp.mean(dxh * xh, axis=-1, keepdims=True))
    return dx, dg


def _sigmoid(a):
    return jax.nn.sigmoid(a)


def _mesh_place():
    x, y, c = lax.axis_index("x"), lax.axis_index("y"), lax.axis_index("c")
    return x, y, c


def _dev_index(p):
    return 4 * p[0] + 2 * p[1] + p[2]


def _comm_sems(n):
    return [pltpu.SemaphoreType.DMA((n, 7)), pltpu.SemaphoreType.DMA((n, 7)), pltpu.SemaphoreType.DMA((n,))]


def _gather_protocol(ins, outs, send_sems, recv_sems, local_sems):
    n = len(ins)
    x, y, c = _mesh_place()
    me, sibling = (x, y, c), (x, y, 1 - c)
    chips = [(1 - x, y), (x, 1 - y), (1 - x, 1 - y)]

    def copy(a, k, block, to, src=None):
        slot = outs[a].at[_dev_index(block)]
        return pltpu.make_async_remote_copy(
            src_ref=slot if src is None else src, dst_ref=slot,
            send_sem=send_sems.at[a, k], recv_sem=recv_sems.at[a, k], device_id=to, device_id_type=MESH)

    def mine(a):
        return pltpu.make_async_copy(ins[a], outs[a].at[_dev_index(me)], local_sems.at[a])

    def first(a):
        return [copy(a, 0, me, sibling, src=ins[a])] + [copy(a, 1 + j, me, (*chip, c), src=ins[a]) for j, chip in enumerate(chips)]

    def start():
        for a in range(n):
            mine(a).start()
            for cp in first(a):
                cp.start()

    def forward():
        for a in range(n):
            for j, chip in enumerate(chips):
                copy(a, 1 + j, (*chip, c), me).wait_recv()
                copy(a, 4 + j, (*chip, c), sibling).start()

    def finish():
        for a in range(n):
            copy(a, 0, sibling, me).wait_recv()
            for j, chip in enumerate(chips):
                copy(a, 4 + j, (*chip, 1 - c), me).wait_recv()
        for a in range(n):
            mine(a).wait()
            for cp in first(a):
                cp.wait_send()
            for j, chip in enumerate(chips):
                copy(a, 4 + j, (*chip, c), sibling).wait_send()

    return start, forward, finish


def _exchange_protocol(ins, outs, scatter, send_sems, recv_sems, local_sems):
    n = len(ins)
    x, y, c = _mesh_place()
    me = (x, y, c)
    my_id = _dev_index(me)
    rels = [(dx, dy, dc) for dx in (0, 1) for dy in (0, 1) for dc in (0, 1)][1:]

    def peer_of(rel):
        return tuple(1 - v if d else v for v, d in zip(me, rel))

    def src(a, dev):
        return ins[a].at[dev] if scatter[a] else ins[a]

    def send(a, k):
        peer = peer_of(rels[k])
        return pltpu.make_async_remote_copy(
            src_ref=src(a, _dev_index(peer)), dst_ref=outs[a].at[my_id],
            send_sem=send_sems.at[a, k], recv_sem=recv_sems.at[a, k], device_id=peer, device_id_type=MESH)

    def arrival(a, k):
        peer = peer_of(rels[k])
        return pltpu.make_async_remote_copy(
            src_ref=src(a, my_id), dst_ref=outs[a].at[_dev_index(peer)],
            send_sem=send_sems.at[a, k], recv_sem=recv_sems.at[a, k], device_id=peer, device_id_type=MESH)

    def own(a):
        return pltpu.make_async_copy(src(a, my_id), outs[a].at[my_id], local_sems.at[a])

    def start():
        for a in range(n):
            own(a).start()
            for k in range(7):
                send(a, k).start()

    def finish():
        for a in range(n):
            for k in range(7):
                arrival(a, k).wait_recv()
        for a in range(n):
            for k in range(7):
                send(a, k).wait_send()
            own(a).wait()

    return start, finish


def _slot_shapes(blocks, scatter=None):
    return [jax.ShapeDtypeStruct(b.shape if (scatter and scatter[a]) else (N_DEV,) + b.shape, b.dtype) for a, b in enumerate(blocks)]


def _all_gather_call(blocks):
    n = len(blocks)

    def body(*refs):
        start, forward, finish = _gather_protocol(refs[:n], refs[n:2 * n], *refs[2 * n:])
        start()
        forward()
        finish()

    any_spec = pl.BlockSpec(memory_space=pl.ANY)
    return pl.pallas_call(
        body, name="weights_all_gather", out_shape=_slot_shapes(blocks),
        in_specs=[any_spec] * n, out_specs=[any_spec] * n, scratch_shapes=_comm_sems(n),
    )(*blocks)


def _sum_slots_call(recvs):
    n = len(recvs)

    def body(*refs):
        for in_ref, out_ref in zip(refs[:n], refs[n:]):
            acc = in_ref[0]
            for j in range(1, N_DEV):
                acc = acc + in_ref[j]
            out_ref[...] = acc

    return pl.pallas_call(
        body, name="small_grad_sum", out_shape=[jax.ShapeDtypeStruct(r.shape[1:], F32) for r in recvs],
        compiler_params=_params(),
    )(*recvs)


def _adam_update(w, g, m, v):
    nm = ADAM_B1 * m + (1.0 - ADAM_B1) * g
    nv = ADAM_B2 * v + (1.0 - ADAM_B2) * (g * g)
    bc1 = 1.0 - ADAM_B1 ** ADAM_STEP
    bc2 = 1.0 - ADAM_B2 ** ADAM_STEP
    return -ADAM_LR * ((nm / bc1) / (jnp.sqrt(nv / bc2) + ADAM_EPS) + ADAM_WD * w), nm, nv


def _adamw_recv(w, recv, m, v, tag):
    r, c = w.shape
    tr = r
    for cand in (512, 256, 128):
        if r > cand and r % cand == 0:
            tr = cand
            break

    def body(w_ref, r_ref, m_ref, v_ref, g_ref, d_ref, nm_ref, nv_ref):
        g = r_ref[0].astype(F32)
        for j in range(1, N_DEV):
            g = g + r_ref[j].astype(F32)
        g_ref[...] = g
        d_ref[...], nm_ref[...], nv_ref[...] = _adam_update(w_ref[...], g, m_ref[...], v_ref[...])

    spec = pl.BlockSpec((tr, c), lambda i: (i, 0))
    return pl.pallas_call(
        body, name="adamw_" + tag, out_shape=[jax.ShapeDtypeStruct(w.shape, F32)] * 4, grid=(r // tr,),
        in_specs=[spec, pl.BlockSpec((N_DEV, tr, c), lambda i: (0, i, 0)), spec, spec], out_specs=[spec] * 4,
        compiler_params=_params(),
    )(w, recv, m, v)


def _adamw_recv_halves(w, recv_halves, m, v, tag):
    r, c = w.shape
    half = c // 2

    def body(w_ref, ra_ref, rb_ref, m_ref, v_ref, g_ref, d_ref, nm_ref, nv_ref):
        def update(r_ref):
            g = r_ref[0].astype(F32)
            for j in range(1, N_DEV):
                g = g + r_ref[j].astype(F32)
            g_ref[...] = g
            d_ref[...], nm_ref[...], nv_ref[...] = _adam_update(w_ref[...], g, m_ref[...], v_ref[...])

        pl.when(pl.program_id(0) == 0)(lambda: update(ra_ref))
        pl.when(pl.program_id(0) == 1)(lambda: update(rb_ref))

    spec = pl.BlockSpec((r, half), lambda j: (0, j))
    whole = pl.BlockSpec((N_DEV, r, half), lambda j: (0, 0, 0))
    return pl.pallas_call(
        body, name="adamw_" + tag, out_shape=[jax.ShapeDtypeStruct(w.shape, F32)] * 4, grid=(2,),
        in_specs=[spec, whole, whole, spec, spec], out_specs=[spec] * 4, compiler_params=_params(),
    )(w, *recv_halves, m, v)


def _adamw_recv_hosting(ws, recvs, ms, vs, blocks, scatter):
    n, ne = len(ws), len(blocks)
    rows = max(w.shape[0] for w in ws)
    cols = ws[0].shape[1]
    assert all(w.shape[1] == cols for w in ws)

    def body(*refs):
        ins, ex_in = refs[:4 * n], refs[4 * n:4 * n + ne]
        outs, ex_out = refs[4 * n + ne:8 * n + ne], refs[8 * n + ne:8 * n + 2 * ne]
        in_buf, recv_buf, out_buf, in_sems, out_sems = refs[8 * n + 2 * ne:8 * n + 2 * ne + 5]
        start, finish = _exchange_protocol(ex_in, ex_out, scatter, *refs[8 * n + 2 * ne + 5:])
        start()
        for a in range(n):
            r = pl.ds(0, ws[a].shape[0])
            loads = [pltpu.make_async_copy(ins[k * n + a], in_buf.at[j, r], in_sems.at[j]) for j, k in enumerate((0, 2, 3))]
            loads.append(pltpu.make_async_copy(ins[n + a], recv_buf.at[:, r], in_sems.at[3]))
            for cp in loads:
                cp.start()
            for cp in loads:
                cp.wait()
            g = recv_buf[0, r].astype(F32)
            for j in range(1, N_DEV):
                g = g + recv_buf[j, r].astype(F32)
            out_buf[0, r] = g
            out_buf[1, r], out_buf[2, r], out_buf[3, r] = _adam_update(in_buf[0, r], g, in_buf[1, r], in_buf[2, r])
            stores = [pltpu.make_async_copy(out_buf.at[k, r], outs[k * n + a], out_sems.at[k]) for k in range(4)]
            for cp in stores:
                cp.start()
            for cp in stores:
                cp.wait()
        finish()

    any_spec = pl.BlockSpec(memory_space=pl.ANY)
    out = pl.pallas_call(
        body, name="adamw_late_and_grad_exchange",
        out_shape=[jax.ShapeDtypeStruct(w.shape, F32) for w in ws] * 4 + _slot_shapes(blocks, scatter),
        in_specs=[any_spec] * (4 * n + ne), out_specs=[any_spec] * (4 * n + ne),
        scratch_shapes=[pltpu.VMEM((3, rows, cols), F32), pltpu.VMEM((N_DEV, rows, cols), BF16), pltpu.VMEM((4, rows, cols), F32),
                        pltpu.SemaphoreType.DMA((4,)), pltpu.SemaphoreType.DMA((4,))] + _comm_sems(ne),
        compiler_params=_params(),
    )(*ws, *recvs, *ms, *vs, *blocks)
    return out[:n], out[n:2 * n], out[2 * n:3 * n], out[3 * n:4 * n], out[4 * n:]


def _adamw_small(ws, gs, ms, vs):
    n = len(ws)

    def body(*refs):
        ins, outs = refs[:4 * n], refs[4 * n:]
        for a in range(n):
            d, nm, nv = _adam_update(ins[a][...], ins[n + a][...], ins[2 * n + a][...], ins[3 * n + a][...])
            outs[a][...], outs[n + a][...], outs[2 * n + a][...] = d, nm, nv

    out = pl.pallas_call(
        body, name="adamw_small", out_shape=[jax.ShapeDtypeStruct(w.shape, F32) for w in ws] * 3, compiler_params=_params(),
    )(*ws, *gs, *ms, *vs)
    return out[:n], out[n:2 * n], out[2 * n:]


def _rope_tables(seq):
    inv = 1.0 / (ROPE_THETA ** (jnp.arange(0, B_ROPE, 2, dtype=F32) / B_ROPE))
    ang = jnp.arange(seq, dtype=F32)[:, None] * inv[None, :]
    cos, sin = jnp.cos(ang), jnp.sin(ang)
    z32, z64 = jnp.zeros_like(cos), jnp.zeros((seq, 64), F32)
    cos_t = jnp.concatenate([cos, cos, z64], axis=1)
    sin_a = jnp.concatenate([-sin, z32, z64], axis=1)
    sin_b = jnp.concatenate([z32, sin, z64], axis=1)
    return cos_t, sin_a, sin_b


def _rope(t, cos_t, sin_a, sin_b):
    return t * cos_t + pltpu.roll(t, 96, 1) * sin_a + pltpu.roll(t, 32, 1) * sin_b


def _rope_t(d, cos_t, sin_a, sin_b):
    return d * cos_t + pltpu.roll(d * sin_a, 32, 1) + pltpu.roll(d * sin_b, 96, 1)


def _inproj_qkv(x, g1, w_in, g_qa, g_kva, w_q, w_kv, tables, seq, tm):
    t = x.shape[0]
    nblk = seq // tm
    n_plain = 7
    offs = [sum(IN_WIDTHS[:j]) for j in range(len(IN_WIDTHS))]

    def body(x_ref, g_ref, w_ref, gq_ref, gk_ref, wq_ref, wkv_ref, c_ref, sa_ref, sb_ref, *outs):
        q_out, k_out, v_out = outs[n_plain:]
        for j in range(tm // min(tm, ROW_SUB)):
            r = pl.ds(j * min(tm, ROW_SUB), min(tm, ROW_SUB))
            h = _rms(x_ref[r, :], g_ref[...]).astype(BF16)
            proj = lambda g: _dot_nt(h, w_ref[offs[g]:offs[g] + IN_WIDTHS[g], :])
            for g in range(5):
                outs[g][r, :] = proj(g)
            cq, ckv, kr = proj(5), proj(6), proj(7)
            outs[5][r, :] = cq
            outs[6][r, :] = ckv
            cos_t, sin_a, sin_b = c_ref[r, :], sa_ref[r, :], sb_ref[r, :]
            cqn = _rms(cq, gq_ref[...]).astype(BF16)
            ckn = _rms(ckv, gk_ref[...]).astype(BF16)
            kr_rot = _rope(kr, cos_t, sin_a, sin_b).astype(BF16)
            for hd in range(B_HEADS):
                lo = hd * QK_PAD
                q_out[r, lo:lo + 128] = (_dot_nt(cqn, wq_ref[lo:lo + 128, :]) * ATTN_SCALE).astype(BF16)
                qr = _rope(_dot_nt(cqn, wq_ref[lo + 128:lo + 256, :]), cos_t, sin_a, sin_b)
                q_out[r, lo + 128:lo + 256] = (qr * ATTN_SCALE).astype(BF16)
                k_out[r, lo:lo + 128] = _dot(ckn, wkv_ref[:, lo:lo + 128]).astype(BF16)
                k_out[r, lo + 128:lo + 256] = kr_rot
                v_out[r, hd * B_V:(hd + 1) * B_V] = _dot(ckn, wkv_ref[:, lo + 128:lo + 256]).astype(BF16)

    tok = lambda wd: pl.BlockSpec((tm, wd), lambda i: (i, 0))
    tab = pl.BlockSpec((tm, 128), lambda i: (i % nblk, 0))
    widths = list(IN_WIDTHS[:n_plain]) + [B_HEADS * QK_PAD, B_HEADS * QK_PAD, B_HEADS * B_V]
    dtypes = [F32] * n_plain + [BF16] * 3
    return pl.pallas_call(
        body, name="inproj_qkv_fwd", grid=(t // tm,),
        out_shape=[jax.ShapeDtypeStruct((t, wd), dt) for wd, dt in zip(widths, dtypes)],
        in_specs=[tok(D_MODEL), _const_spec((1, D_MODEL)), _const_spec((D_IN_PAD, D_MODEL)), _const_spec((1, Q_LORA)),
                  _const_spec((1, KV_LORA)), _const_spec((B_HEADS * QK_PAD, Q_LORA)), _const_spec((KV_LORA, 1024)), tab, tab, tab],
        out_specs=[tok(wd) for wd in widths],
        compiler_params=_params(),
    )(x, g1, w_in, g_qa, g_kva, w_q, w_kv, *tables)


def _step_index(nq):
    return (pl.program_id(0) * B_HEADS + pl.program_id(1)) * nq + pl.program_id(2)


def _attn_fwd(qcat, kcat, v, nb, seq, tq, gather=()):
    t = qcat.shape[0]
    nq = seq // tq
    ng = len(gather)
    steps = nb * B_HEADS * nq

    def body(q_ref, k_ref, v_ref, *rest):
        o_ref, lse_ref = rest[ng:ng + 2]
        if ng:
            start, forward, finish = _gather_protocol(rest[:ng], rest[ng + 2:2 * ng + 2], *rest[2 * ng + 2:])
            pl.when(_step_index(nq) == 0)(start)
            pl.when(_step_index(nq) == (3 * steps) // 4)(forward)
        for j in range(tq // ATTN_SUB):
            r = pl.ds(j * ATTN_SUB, ATTN_SUB)
            s = _dot_nt(q_ref[r, :], k_ref[...])
            m = jnp.max(s, axis=-1, keepdims=True)
            p = jnp.exp(s - m)
            l = jnp.sum(p, axis=-1, keepdims=True)
            o_ref[r, :] = _dot(p.astype(BF16), v_ref[...]) / l
            lse_ref[0, r, :] = m + jnp.log(l)
        if ng:
            pl.when(_step_index(nq) == steps - 1)(finish)

    any_spec = pl.BlockSpec(memory_space=pl.ANY)
    return pl.pallas_call(
        body, name="attn_fwd", grid=(nb, B_HEADS, nq),
        out_shape=[jax.ShapeDtypeStruct((t, B_HEADS * B_V), F32), jax.ShapeDtypeStruct((B_HEADS, t, 1), F32)] + _slot_shapes(gather),
        in_specs=[pl.BlockSpec((tq, QK_PAD), lambda b, h, i: (b * nq + i, h)),
                  pl.BlockSpec((seq, QK_PAD), lambda b, h, i: (b, h)),
                  pl.BlockSpec((seq, B_V), lambda b, h, i: (b, h))] + [any_spec] * ng,
        out_specs=[pl.BlockSpec((tq, B_V), lambda b, h, i: (b * nq + i, h)),
                   pl.BlockSpec((1, tq, 1), lambda b, h, i: (h, b * nq + i, 0))] + [any_spec] * ng,
        scratch_shapes=_comm_sems(ng) if ng else [],
        compiler_params=_params(),
    )(qcat, kcat, v, *gather)


def _attn_bwd(qcat, kcat, v, o, lse, do, nb, seq, tq, exchange=()):
    t = qcat.shape[0]
    nq = seq // tq
    ne = len(exchange)
    steps = nb * B_HEADS * nq

    def body(q_ref, k_ref, v_ref, o_ref, lse_ref, do_ref, *rest):
        dq_ref, dk_ref, dv_ref = rest[ne:ne + 3]
        if ne:
            start, finish = _exchange_protocol(rest[:ne], rest[ne + 3:2 * ne + 3], [True] * ne, *rest[2 * ne + 3:])
            pl.when(_step_index(nq) == 0)(start)

        @pl.when(pl.program_id(2) == 0)
        def _():
            dv_ref[...] = jnp.zeros_like(dv_ref)
            dk_ref[...] = jnp.zeros_like(dk_ref)

        for j in range(tq // ATTN_SUB_BWD):
            r = pl.ds(j * ATTN_SUB_BWD, ATTN_SUB_BWD)
            q, k = q_ref[r, :], k_ref[...]
            do_f = do_ref[r, :].astype(F32)
            delta = jnp.sum(do_f * o_ref[r, :], axis=-1, keepdims=True)
            dob = do_f.astype(BF16)
            p = jnp.exp(_dot_nt(q, k) - lse_ref[0, r, :])
            ds = (p * (_dot_nt(dob, v_ref[...]) - delta)).astype(BF16)
            dq_ref[r, :] = _dot(ds, k).astype(dq_ref.dtype)
            dv_ref[...] += _dot_tn(p.astype(BF16), dob)
            dk_ref[...] += _dot_tn(ds, q)
        if ne:
            pl.when(_step_index(nq) == steps - 1)(finish)

    qspec = lambda wd: pl.BlockSpec((tq, wd), lambda b, h, i: (b * nq + i, h))
    kspec = lambda wd: pl.BlockSpec((seq, wd), lambda b, h, i: (b, h))
    any_spec = pl.BlockSpec(memory_space=pl.ANY)
    return pl.pallas_call(
        body, name="attn_bwd", grid=(nb, B_HEADS, nq),
        out_shape=[jax.ShapeDtypeStruct((t, B_HEADS * QK_PAD), BF16), jax.ShapeDtypeStruct((t, B_HEADS * QK_PAD), F32),
                   jax.ShapeDtypeStruct((t, B_HEADS * B_V), F32)] + _slot_shapes(exchange, [True] * ne),
        in_specs=[qspec(QK_PAD), kspec(QK_PAD), kspec(B_V), qspec(B_V),
                  pl.BlockSpec((1, tq, 1), lambda b, h, i: (h, b * nq + i, 0)), qspec(B_V)] + [any_spec] * ne,
        out_specs=[qspec(QK_PAD), kspec(QK_PAD), kspec(B_V)] + [any_spec] * ne,
        scratch_shapes=_comm_sems(ne) if ne else [],
        compiler_params=_params(),
    )(qcat, kcat, v, o, lse, do, *exchange)


def _gla_consts(reverse):
    row = lax.broadcasted_iota(jnp.int32, (CHUNK, CHUNK), 0)
    col = lax.broadcasted_iota(jnp.int32, (CHUNK, CHUNK), 1)
    causal = (row <= col) if reverse else (row >= col)
    lane = lax.broadcasted_iota(jnp.int32, (1, HEAD_PAIR), 1)
    m0 = (lane < 64).astype(F32)
    m1 = 1.0 - m0
    r2 = lax.broadcasted_iota(jnp.int32, (HEAD_PAIR, HEAD_PAIR), 0)
    c2 = lax.broadcasted_iota(jnp.int32, (HEAD_PAIR, HEAD_PAIR), 1)
    same_head = ((r2 < 64) == (c2 < 64)).astype(F32)
    return causal, m0, m1, same_head


def _gla_chunk(hq, hi, z, l0, l1, st, consts, reverse):
    q_dec, k_inv, k_end, decay = _gla_gates(hq, z, l0, l1, reverse)
    o, st_new = _gla_state(q_dec, st, decay, _gla_increment(hi, k_end, consts))
    return o + _gla_intra(q_dec, k_inv, hi, consts), st_new


def _gla_gates(hq, z, l0, l1, reverse):
    mx = jnp.maximum(l0, l1)
    e0, e1 = jnp.exp(l0 - mx), jnp.exp(l1 - mx)
    lb = e0 / (e0 + e1)
    q = hq * _sigmoid(hq)
    sz = _sigmoid(z)
    log_f = jnp.log(lb + (1.0 - lb) * sz)
    k = (1.0 - lb) * (1.0 - sz)
    cum = _cumsum_rows(log_f, reverse)
    decay = jnp.exp(jnp.sum(log_f, axis=0, keepdims=True))
    k_inv = k * jnp.exp(-cum)
    return q * jnp.exp(cum), k_inv, k_inv * decay, decay


def _gla_intra(q_dec, k_inv, hi, consts):
    causal, m0, m1, _ = consts
    o = None
    for mh in (m0, m1):
        s = jnp.where(causal, _mm_nt(q_dec * mh, k_inv), 0.0)
        part = _mm(s, hi) * mh
        o = part if o is None else o + part
    return o


def _gla_increment(hi, k_end, consts):
    return _mm_tn(hi, k_end) * consts[3]


def _gla_state(q_dec, st, decay, inc):
    return _mm_nt(q_dec, st), st * decay + inc


GLA_DIRS = (False, True)
GLA_BATCH_FWD = 8
GLA_BATCH_BWD = 4


def _gla_fwd(hq, hi, zs, lbls, nb, seq, group):
    t = hq.shape[0]
    rows = group * CHUNK
    nblk = seq // rows
    n_chunks = seq // CHUNK
    nd = len(GLA_DIRS)

    def body(*refs):
        ins, outs, st_refs = refs[:4 * nd], refs[4 * nd:6 * nd], refs[6 * nd:]
        @pl.when(pl.program_id(2) == 0)
        def _():
            for st_ref in st_refs:
                st_ref[...] = jnp.zeros_like(st_ref)

        consts = [_gla_consts(rev) for rev in GLA_DIRS]
        work = [(d, rev, group - 1 - cc if rev else cc) for cc in range(group) for d, rev in enumerate(GLA_DIRS)]
        rows_of = lambda c: pl.ds(c * CHUNK, CHUNK)
        sts = [st_ref[...] for st_ref in st_refs]
        for w0 in range(0, len(work), GLA_BATCH_FWD):
            batch = work[w0:w0 + GLA_BATCH_FWD]
            gates, intra, incs = {}, {}, {}
            for d, rev, c in batch:
                hq_ref, _, z_ref, lbl_ref = ins[4 * d:4 * d + 4]
                gates[d, c] = _gla_gates(hq_ref[rows_of(c), :], z_ref[rows_of(c), :], lbl_ref[0:1, :], lbl_ref[1:2, :], rev)
            for d, rev, c in batch:
                hi_c = ins[4 * d + 1][rows_of(c), :]
                intra[d, c] = _gla_intra(gates[d, c][0], gates[d, c][1], hi_c, consts[d])
                incs[d, c] = _gla_increment(hi_c, gates[d, c][2], consts[d])
            for d, rev, c in batch:
                outs[nd + d][0, 0, c] = sts[d].astype(outs[nd + d].dtype)
                o_state, sts[d] = _gla_state(gates[d, c][0], sts[d], gates[d, c][3], incs[d, c])
                outs[d][rows_of(c), :] = (intra[d, c] + o_state).astype(outs[d].dtype)
        for st_ref, st in zip(st_refs, sts):
            st_ref[...] = st

    def tb(rev):
        return (lambda i: nblk - 1 - i) if rev else (lambda i: i)

    tok = lambda rev: pl.BlockSpec((rows, HEAD_PAIR), lambda b, p, i: (b * nblk + tb(rev)(i), p))
    lspec = pl.BlockSpec((2, HEAD_PAIR), lambda b, p, i: (0, p))
    sspec = lambda rev: pl.BlockSpec((1, 1, group, HEAD_PAIR, HEAD_PAIR), lambda b, p, i: (b, p, tb(rev)(i), 0, 0))
    args, in_specs = [], []
    for d, rev in enumerate(GLA_DIRS):
        args += [hq, hi, zs[d], lbls[d]]
        in_specs += [tok(rev), tok(rev), tok(rev), lspec]
    return pl.pallas_call(
        body, name="gla_fwd", grid=(nb, 4, nblk),
        out_shape=[jax.ShapeDtypeStruct((t, A_WIDTH), BF16)] * nd
        + [jax.ShapeDtypeStruct((nb, 4, n_chunks, HEAD_PAIR, HEAD_PAIR), BF16)] * nd,
        in_specs=in_specs, out_specs=[tok(rev) for rev in GLA_DIRS] + [sspec(rev) for rev in GLA_DIRS],
        scratch_shapes=[pltpu.VMEM((HEAD_PAIR, HEAD_PAIR), F32)] * nd,
        compiler_params=_params(),
    )(*args)


def _gla_bwd(hq, hi, zs, lbls, saved, do, nb, seq, group):
    t = hq.shape[0]
    rows = group * CHUNK
    nblk = seq // rows
    nd = len(GLA_DIRS)

    def body(*refs):
        ins, outs, dst_refs = refs[:6 * nd], refs[6 * nd:10 * nd], refs[10 * nd:]
        dl_refs = outs[3 * nd:]

        @pl.when(pl.program_id(2) == 0)
        def _():
            for dst_ref, dl_ref in zip(dst_refs, dl_refs):
                dst_ref[...] = jnp.zeros_like(dst_ref)
                dl_ref[...] = jnp.zeros_like(dl_ref)

        consts = [_gla_consts(rev) for rev in GLA_DIRS]
        dsts = [dst_ref[...] for dst_ref in dst_refs]
        dls = [[jnp.zeros((1, HEAD_PAIR), F32), jnp.zeros((1, HEAD_PAIR), F32)] for _ in GLA_DIRS]
        work = [(d, rev, cc if rev else group - 1 - cc) for cc in range(group) for d, rev in enumerate(GLA_DIRS)]
        for w0 in range(0, len(work), GLA_BATCH_BWD):
            vjps = {}
            for d, rev, c in work[w0:w0 + GLA_BATCH_BWD]:
                hq_ref, hi_ref, z_ref, lbl_ref, save_ref, _ = ins[6 * d:6 * d + 6]
                r = pl.ds(c * CHUNK, CHUNK)
                fn = functools.partial(_gla_chunk, consts=consts[d], reverse=rev)
                _, vjps[d, c] = jax.vjp(fn, hq_ref[r, :], hi_ref[r, :], z_ref[r, :], lbl_ref[0:1, :], lbl_ref[1:2, :],
                                         save_ref[0, 0, c].astype(F32))
            for d, rev, c in work[w0:w0 + GLA_BATCH_BWD]:
                dq_ref, dv_ref, dz_ref = outs[3 * d:3 * d + 3]
                r = pl.ds(c * CHUNK, CHUNK)
                d_hq, d_hi, d_z, d_l0, d_l1, dsts[d] = vjps[d, c]((ins[6 * d + 5][r, :].astype(F32), dsts[d]))
                dq_ref[r, :] = d_hq.astype(dq_ref.dtype)
                dv_ref[r, :] = d_hi.astype(dv_ref.dtype)
                dz_ref[r, :] = d_z.astype(dz_ref.dtype)
                dls[d] = [dls[d][0] + d_l0, dls[d][1] + d_l1]
        for d in range(nd):
            dst_refs[d][...] = dsts[d]
            dl_refs[d][0, 0:1, :] += dls[d][0]
            dl_refs[d][0, 1:2, :] += dls[d][1]

    def tb(rev):
        return (lambda i: i) if rev else (lambda i: nblk - 1 - i)

    tok = lambda rev: pl.BlockSpec((rows, HEAD_PAIR), lambda b, p, i: (b * nblk + tb(rev)(i), p))
    lspec = pl.BlockSpec((2, HEAD_PAIR), lambda b, p, i: (0, p))
    sspec = lambda rev: pl.BlockSpec((1, 1, group, HEAD_PAIR, HEAD_PAIR), lambda b, p, i: (b, p, tb(rev)(i), 0, 0))
    args, in_specs, out_specs = [], [], []
    for d, rev in enumerate(GLA_DIRS):
        args += [hq, hi, zs[d], lbls[d], saved[d], do]
        in_specs += [tok(rev), tok(rev), tok(rev), lspec, sspec(rev), tok(rev)]
        out_specs += [tok(rev)] * 3
    out_specs += [pl.BlockSpec((1, 2, HEAD_PAIR), lambda b, p, i: (b, 0, p))] * nd
    return pl.pallas_call(
        body, name="gla_bwd", grid=(nb, 4, nblk),
        out_shape=[jax.ShapeDtypeStruct((t, A_WIDTH), BF16)] * (3 * nd) + [jax.ShapeDtypeStruct((nb, 2, A_WIDTH), F32)] * nd,
        in_specs=in_specs, out_specs=out_specs,
        scratch_shapes=[pltpu.VMEM((HEAD_PAIR, HEAD_PAIR), F32)] * nd,
        compiler_params=_params(),
    )(*args)


def _head_mean_matrix():
    r = lax.broadcasted_iota(jnp.int32, (A_WIDTH, A_WIDTH), 0) // 64
    c = lax.broadcasted_iota(jnp.int32, (A_WIDTH, A_WIDTH), 1) // 64
    return jnp.where(r == c, 1.0 / 64.0, 0.0).astype(BF16)


def _gla_out(o_f, o_b, hg, g, mean_mat):
    o = o_f + o_b
    ms = _group_mean(o * o, mean_mat)
    return o * lax.rsqrt(ms + EPS) * g * (hg * _sigmoid(hg))


def _gla_combine(o_f, o_b, hg, g, tm):
    t = o_f.shape[0]

    def body(of_ref, ob_ref, hg_ref, g_ref, y_ref):
        y_ref[...] = _gla_out(of_ref[...].astype(F32), ob_ref[...].astype(F32), hg_ref[...], g_ref[...], _head_mean_matrix())

    tok = pl.BlockSpec((tm, A_WIDTH), lambda i: (i, 0))
    return pl.pallas_call(
        body, name="gla_combine_fwd", grid=(t // tm,), out_shape=jax.ShapeDtypeStruct((t, A_WIDTH), F32),
        in_specs=[tok, tok, tok, _const_spec((1, A_WIDTH))], out_specs=tok, compiler_params=_params(),
    )(o_f, o_b, hg, g)


def _gla_combine_bwd(o_f, o_b, hg, g, dy, tm):
    t = o_f.shape[0]

    def body(of_ref, ob_ref, hg_ref, g_ref, dy_ref, do_ref, dhg_ref, dg_ref):
        mean_mat = _head_mean_matrix()
        fn = lambda o, hgv, gv: _gla_out(o, jnp.zeros_like(o), hgv, gv, mean_mat)
        _, vjp = jax.vjp(fn, of_ref[...].astype(F32) + ob_ref[...].astype(F32), hg_ref[...], g_ref[...])
        d_o, d_hg, d_g = vjp(dy_ref[...].astype(F32))
        do_ref[...] = d_o.astype(do_ref.dtype)
        dhg_ref[...] = d_hg.astype(dhg_ref.dtype)

        @pl.when(pl.program_id(0) == 0)
        def _():
            dg_ref[...] = jnp.zeros_like(dg_ref)

        dg_ref[...] += d_g

    tok = pl.BlockSpec((tm, A_WIDTH), lambda i: (i, 0))
    vec = pl.BlockSpec((1, A_WIDTH), lambda i: (0, 0))
    return pl.pallas_call(
        body, name="gla_combine_bwd", grid=(t // tm,),
        out_shape=[jax.ShapeDtypeStruct((t, A_WIDTH), BF16), jax.ShapeDtypeStruct((t, A_WIDTH), BF16),
                   jax.ShapeDtypeStruct((1, A_WIDTH), F32)],
        in_specs=[tok, tok, tok, _const_spec((1, A_WIDTH)), tok], out_specs=[tok, tok, vec], compiler_params=_params(),
    )(o_f, o_b, hg, g, dy)


def _post_fwd(x, ya, oattn, tgt, g_mla, w_out, g2, w_gate, w_up, w_down, g_fin, tm):
    t = x.shape[0]

    def body(x_ref, ya_ref, oa_ref, tgt_ref, gm_ref, wo_ref, g2_ref, wg_ref, wu_ref, wd_ref, gf_ref,
             x1_ref, x2_ref, gate_ref, up_ref, loss_ref):
        part = jnp.zeros((1, 1), F32)
        for j in range(tm // min(tm, ROW_SUB)):
            r = pl.ds(j * min(tm, ROW_SUB), min(tm, ROW_SUB))
            yb = _rms(oa_ref[r, :], gm_ref[...])
            x1 = x_ref[r, :] + _dot(ya_ref[r, :].astype(BF16), wo_ref[0:A_WIDTH, :]) + _dot(yb.astype(BF16), wo_ref[A_WIDTH:, :])
            x1_ref[r, :] = x1
            h2 = _rms(x1, g2_ref[...]).astype(BF16)
            gate, up = _dot_nt(h2, wg_ref[...]), _dot_nt(h2, wu_ref[...])
            gate_ref[r, :] = gate.astype(BF16)
            up_ref[r, :] = up.astype(BF16)
            act = (gate * _sigmoid(gate) * up).astype(BF16)
            x2 = x1 + _dot(act, wd_ref[...])
            x2_ref[r, :] = x2
            err = _rms(x2, gf_ref[...]) - tgt_ref[r, :]
            part = part + 0.5 * jnp.sum(jnp.mean(err * err, axis=-1, keepdims=True), axis=0, keepdims=True)

        @pl.when(pl.program_id(0) == 0)
        def _():
            loss_ref[...] = jnp.zeros_like(loss_ref)

        loss_ref[...] += jnp.broadcast_to(part, loss_ref.shape)

    tok = lambda wd: pl.BlockSpec((tm, wd), lambda i: (i, 0))
    return pl.pallas_call(
        body, name="post_fwd", grid=(t // tm,),
        out_shape=[jax.ShapeDtypeStruct((t, D_MODEL), F32)] * 2 + [jax.ShapeDtypeStruct((t, D_FF), BF16)] * 2
        + [jax.ShapeDtypeStruct((1, 128), F32)],
        in_specs=[tok(D_MODEL), tok(A_WIDTH), tok(512), tok(D_MODEL), _const_spec((1, 512)), _const_spec((D_MODEL, D_MODEL)),
                  _const_spec((1, D_MODEL)), _const_spec((D_FF, D_MODEL)), _const_spec((D_FF, D_MODEL)),
                  _const_spec((D_FF, D_MODEL)), _const_spec((1, D_MODEL))],
        out_specs=[tok(D_MODEL), tok(D_MODEL), tok(D_FF), tok(D_FF), pl.BlockSpec((1, 128), lambda i: (0, 0))],
        compiler_params=_params(),
    )(x, ya, oattn, tgt, g_mla, w_out, g2, w_gate, w_up, w_down, g_fin)


def _post_bwd(x1, x2, gate_b, up_b, ya, oattn, tgt, g_mla, w_out, g2, w_gate, w_up, w_down, g_fin, tm):
    t = x1.shape[0]

    def body(x1_ref, x2_ref, gate_ref, up_ref, ya_ref, oa_ref, tgt_ref, gm_ref, wo_ref, g2_ref, wg_ref, wu_ref, wd_ref, gf_ref,
             dx1_ref, dya_ref, doa_ref, ycat_ref, dx1b_ref, h2_ref, dgate_ref, dup_ref, act_ref, dx2b_ref,
             dgm_ref, dg2_ref, dgf_ref):
        x1, x2 = x1_ref[...], x2_ref[...]
        dy = (_rms(x2, gf_ref[...]) - tgt_ref[...]) * (1.0 / D_MODEL)
        dx2, dgf = _rms_bwd(x2, gf_ref[...], dy)
        dx2b = dx2.astype(BF16)
        dx2b_ref[...] = dx2b
        h2_ref[...] = _rms(x1, g2_ref[...]).astype(BF16)
        gate, up = gate_ref[...].astype(F32), up_ref[...].astype(F32)
        sg = _sigmoid(gate)
        sl = gate * sg
        act_ref[...] = (sl * up).astype(BF16)
        dact = _dot_nt(dx2b, wd_ref[...])
        dup = (dact * sl).astype(BF16)
        dgate = (dact * up * (sg * (1.0 + gate * (1.0 - sg)))).astype(BF16)
        dup_ref[...] = dup
        dgate_ref[...] = dgate
        dh2 = _dot(dgate, wg_ref[...]) + _dot(dup, wu_ref[...])
        dx1n, dg2 = _rms_bwd(x1, g2_ref[...], dh2)
        dx1 = dx2 + dx1n
        dx1_ref[...] = dx1
        dx1b = dx1.astype(BF16)
        dx1b_ref[...] = dx1b
        oa = oa_ref[...]
        ycat_ref[:, 0:A_WIDTH] = ya_ref[...].astype(BF16)
        ycat_ref[:, A_WIDTH:] = _rms(oa, gm_ref[...]).astype(BF16)
        dya_ref[...] = _dot_nt(dx1b, wo_ref[0:A_WIDTH, :]).astype(dya_ref.dtype)
        doa, dgm = _rms_bwd(oa, gm_ref[...], _dot_nt(dx1b, wo_ref[A_WIDTH:, :]))
        doa_ref[...] = doa.astype(doa_ref.dtype)

        @pl.when(pl.program_id(0) == 0)
        def _():
            dgm_ref[...] = jnp.zeros_like(dgm_ref)
            dg2_ref[...] = jnp.zeros_like(dg2_ref)
            dgf_ref[...] = jnp.zeros_like(dgf_ref)

        dgm_ref[...] += dgm
        dg2_ref[...] += dg2
        dgf_ref[...] += dgf

    tok = lambda wd: pl.BlockSpec((tm, wd), lambda i: (i, 0))
    vec = lambda wd: pl.BlockSpec((1, wd), lambda i: (0, 0))
    sds = lambda wd, dt: jax.ShapeDtypeStruct((t, wd), dt)
    return pl.pallas_call(
        body, name="post_bwd", grid=(t // tm,),
        out_shape=[sds(D_MODEL, F32), sds(512, BF16), sds(512, BF16), sds(D_MODEL, BF16), sds(D_MODEL, BF16), sds(D_MODEL, BF16),
                   sds(D_FF, BF16), sds(D_FF, BF16), sds(D_FF, BF16), sds(D_MODEL, BF16),
                   jax.ShapeDtypeStruct((1, 512), F32), jax.ShapeDtypeStruct((1, D_MODEL), F32), jax.ShapeDtypeStruct((1, D_MODEL), F32)],
        in_specs=[tok(D_MODEL), tok(D_MODEL), tok(D_FF), tok(D_FF), tok(512), tok(512), tok(D_MODEL), _const_spec((1, 512)),
                  _const_spec((D_MODEL, D_MODEL)), _const_spec((1, D_MODEL)), _const_spec((D_FF, D_MODEL)),
                  _const_spec((D_FF, D_MODEL)), _const_spec((D_FF, D_MODEL)), _const_spec((1, D_MODEL))],
        out_specs=[tok(D_MODEL), tok(512), tok(512), tok(D_MODEL), tok(D_MODEL), tok(D_MODEL), tok(D_FF), tok(D_FF), tok(D_FF),
                   tok(D_MODEL), vec(512), vec(D_MODEL), vec(D_MODEL)],
        compiler_params=_params(),
    )(x1, x2, gate_b, up_b, ya, oattn, tgt, g_mla, w_out, g2, w_gate, w_up, w_down, g_fin)


def _matmul_tn(a, b, tn, tt, tag, b_cols=None, k_out=None, exchange=()):
    t, k = a.shape
    c0, n = (0, b.shape[1]) if b_cols is None else b_cols
    k_out = k if k_out is None else k_out
    last = t // tt - 1
    ne = len(exchange)
    n_j = n // tn

    def body(a_ref, b_ref, *rest):
        o_ref, acc_ref = rest[ne], rest[2 * ne + 1]
        if ne:
            start, finish = _exchange_protocol(rest[:ne], rest[ne + 1:2 * ne + 1], [True] * ne, *rest[2 * ne + 2:])
            pl.when((pl.program_id(0) == 0) & (pl.program_id(1) == 0))(start)
        part = _dot_tn(a_ref[...], b_ref[...])

        @pl.when(pl.program_id(1) == 0)
        def _():
            acc_ref[...] = part

        @pl.when(pl.program_id(1) > 0)
        def _():
            acc_ref[...] += part

        @pl.when(pl.program_id(1) == last)
        def _():
            o_ref[...] = acc_ref[0:k_out, :].astype(o_ref.dtype)

        if ne:
            pl.when((pl.program_id(0) == n_j - 1) & (pl.program_id(1) == last))(finish)

    any_spec = pl.BlockSpec(memory_space=pl.ANY)
    out = pl.pallas_call(
        body, name="wgrad_" + tag, grid=(n_j, t // tt),
        out_shape=[jax.ShapeDtypeStruct((k_out, n), BF16)] + _slot_shapes(exchange, [True] * ne),
        in_specs=[pl.BlockSpec((tt, k), lambda j, i: (i, 0)), pl.BlockSpec((tt, tn), lambda j, i: (i, j + c0 // tn))]
        + [any_spec] * ne,
        out_specs=[pl.BlockSpec((k_out, tn), lambda j, i: (0, j))] + [any_spec] * ne,
        scratch_shapes=[pltpu.VMEM((k, tn), F32)] + (_comm_sems(ne) if ne else []),
        compiler_params=_params(),
    )(a, b, *exchange)
    return out if ne else out[0]


def _inproj_qkv_bwd(x, g1, w_in, dx1, pieces, cq, ckv, g_qa, g_kva, w_q, w_kv, tables, dq, dk, dv, seq, tm):
    t = x.shape[0]
    nblk = seq // tm
    last = t // tm - 1
    counts = [len(p) for p in pieces]
    flat = [a for p in pieces for a in p]
    n_flat = len(flat)
    offs = [sum(IN_WIDTHS[:j]) for j in range(len(IN_WIDTHS))]

    def body(x_ref, g_ref, w_ref, dx1_ref, cq_ref, ckv_ref, gq_ref, gk_ref, wq_ref, wkv_ref, c_ref, sa_ref, sb_ref,
             dq_ref, dk_ref, dv_ref, *refs):
        ins = refs[:n_flat]
        (dx_ref, h_ref, dp_ref, dwq_ref, dwkv_ref, dg_ref, dgq_ref, dgk_ref,
         cqn_ref, dqf_ref, ckn_ref, dkv_ref, accq_ref, acckv_ref) = refs[n_flat:]
        cos_t, sin_a, sin_b = c_ref[...], sa_ref[...], sb_ref[...]
        cqn_ref[...] = _rms(cq_ref[...], gq_ref[...]).astype(BF16)
        ckn_ref[...] = _rms(ckv_ref[...], gk_ref[...]).astype(BF16)
        dkr = jnp.zeros((tm, 128), F32)
        for hd in range(B_HEADS):
            lo = hd * QK_PAD
            dqf_ref[:, lo:lo + 128] = (dq_ref[:, lo:lo + 128].astype(F32) * ATTN_SCALE).astype(BF16)
            dq_rope = dq_ref[:, lo + 128:lo + 256].astype(F32) * ATTN_SCALE
            dqf_ref[:, lo + 128:lo + 256] = _rope_t(dq_rope, cos_t, sin_a, sin_b).astype(BF16)
            dkv_ref[:, lo:lo + 128] = dk_ref[:, lo:lo + 128].astype(BF16)
            dkv_ref[:, lo + 128:lo + 256] = dv_ref[:, hd * B_V:(hd + 1) * B_V].astype(BF16)
            dkr = dkr + dk_ref[:, lo + 128:lo + 256]
        dcq, dgq = _rms_bwd(cq_ref[...], gq_ref[...], _dot(dqf_ref[...], wq_ref[...]))
        dckv, dgk = _rms_bwd(ckv_ref[...], gk_ref[...], _dot_nt(dkv_ref[...], wkv_ref[...]))
        dp_ref[:, offs[5]:offs[6]] = dcq.astype(BF16)
        dp_ref[:, offs[6]:offs[7]] = dckv.astype(BF16)
        dp_ref[:, offs[7]:] = _rope_t(dkr, cos_t, sin_a, sin_b).astype(BF16)
        j = 0
        for g, cnt in enumerate(counts):
            acc = ins[j][...].astype(F32)
            for jj in range(1, cnt):
                acc = acc + ins[j + jj][...].astype(F32)
            dp_ref[:, offs[g]:offs[g] + IN_WIDTHS[g]] = acc.astype(BF16)
            j += cnt
        xv = x_ref[...]
        h_ref[...] = _rms(xv, g_ref[...]).astype(BF16)
        dxn, dg = _rms_bwd(xv, g_ref[...], _dot(dp_ref[...], w_ref[...]))
        dx_ref[...] = dx1_ref[...] + dxn

        @pl.when(pl.program_id(0) == 0)
        def _():
            dg_ref[...] = jnp.zeros_like(dg_ref)
            dgq_ref[...] = jnp.zeros_like(dgq_ref)
            dgk_ref[...] = jnp.zeros_like(dgk_ref)
            accq_ref[...] = jnp.zeros_like(accq_ref)
            acckv_ref[...] = jnp.zeros_like(acckv_ref)

        dg_ref[...] += dg
        dgq_ref[...] += dgq
        dgk_ref[...] += dgk
        accq_ref[...] += _dot_tn(dqf_ref[...], cqn_ref[...])
        acckv_ref[...] += _dot_tn(ckn_ref[...], dkv_ref[...])

        @pl.when(pl.program_id(0) == last)
        def _():
            dwq_ref[...] = accq_ref[...].astype(dwq_ref.dtype)
            dwkv_ref[...] = acckv_ref[...].astype(dwkv_ref.dtype)

    tok = lambda wd: pl.BlockSpec((tm, wd), lambda i: (i, 0))
    vec = lambda wd: pl.BlockSpec((1, wd), lambda i: (0, 0))
    whole = lambda r, c: pl.BlockSpec((r, c), lambda i: (0, 0))
    tab = pl.BlockSpec((tm, 128), lambda i: (i % nblk, 0))
    sds = lambda wd, dt: jax.ShapeDtypeStruct((t, wd), dt)
    return pl.pallas_call(
        body, name="inproj_qkv_bwd", grid=(t // tm,),
        out_shape=[sds(D_MODEL, F32), sds(D_MODEL, BF16), sds(D_IN_PAD, BF16),
                   jax.ShapeDtypeStruct((B_HEADS * QK_PAD, Q_LORA), BF16), jax.ShapeDtypeStruct((KV_LORA, 1024), BF16),
                   jax.ShapeDtypeStruct((1, D_MODEL), F32), jax.ShapeDtypeStruct((1, Q_LORA), F32),
                   jax.ShapeDtypeStruct((1, KV_LORA), F32)],
        in_specs=[tok(D_MODEL), _const_spec((1, D_MODEL)), _const_spec((D_IN_PAD, D_MODEL)), tok(D_MODEL), tok(Q_LORA),
                  tok(KV_LORA), _const_spec((1, Q_LORA)), _const_spec((1, KV_LORA)), _const_spec((1024, Q_LORA)),
                  _const_spec((KV_LORA, 1024)), tab, tab, tab, tok(1024), tok(1024), tok(512)] + [tok(512)] * n_flat,
        out_specs=[tok(D_MODEL), tok(D_MODEL), tok(D_IN_PAD), whole(B_HEADS * QK_PAD, Q_LORA), whole(KV_LORA, 1024),
                   vec(D_MODEL), vec(Q_LORA), vec(KV_LORA)],
        scratch_shapes=[pltpu.VMEM((tm, Q_LORA), BF16), pltpu.VMEM((tm, 1024), BF16), pltpu.VMEM((tm, KV_LORA), BF16),
                        pltpu.VMEM((tm, 1024), BF16), pltpu.VMEM((B_HEADS * QK_PAD, Q_LORA), F32),
                        pltpu.VMEM((KV_LORA, 1024), F32)],
        compiler_params=_params(),
    )(x, g1, w_in, dx1, cq, ckv, g_qa, g_kva, w_q, w_kv, *tables, dq, dk, dv, *flat)


def _cols_from_slots(g):
    n, r, cs = g.shape
    return g.transpose(1, 0, 2).reshape(r, n * cs)


def _cols_to_slots(full):
    r, c = full.shape
    return full.reshape(r, N_DEV, c // N_DEV).transpose(1, 0, 2)


def _arrange_w_in_t(w_in_t):
    return jnp.concatenate([w_in_t, jnp.zeros((D_IN_PAD - D_IN, D_MODEL), w_in_t.dtype)], axis=0)


def _arrange_w_q_t(w_q_t):
    q3 = w_q_t.reshape(B_HEADS, B_NOPE + B_ROPE, Q_LORA)
    pad = jnp.zeros((B_HEADS, QK_PAD - B_NOPE - B_ROPE, Q_LORA), w_q_t.dtype)
    return jnp.concatenate([q3, pad], axis=1).reshape(B_HEADS * QK_PAD, Q_LORA)


def _unarrange_w_q_t(d_q_t):
    return d_q_t.reshape(B_HEADS, QK_PAD, Q_LORA)[:, :B_NOPE + B_ROPE].reshape(B_HEADS * (B_NOPE + B_ROPE), Q_LORA)


def _step_core(x, loss_target, small_w, lb_full, early_full, late, seq, group, tiles, distributed):
    g1, g_hgrn, g_qa, g_kva, g_mla, g2, g_fin = small_w
    w_in, w_q, w_kv = _arrange_w_in_t(early_full[0]), _arrange_w_q_t(early_full[1]), early_full[2]
    nb = x.shape[0]
    t = nb * seq
    tm, tm_fwd, tq_f, tq_b, tt = tiles
    xt = x.reshape(t, D_MODEL)
    tgt = loss_target.reshape(t, D_MODEL)
    tables = _rope_tables(seq)

    hq, hi, zf, zb, hg, cq, ckv, qcat, kcat, vv = _inproj_qkv(xt, g1, w_in, g_qa, g_kva, w_q, w_kv, tables, seq, tm_fwd)
    if distributed:
        oattn, lse, *late_slots = _attn_fwd(qcat, kcat, vv, nb, seq, tq_f, gather=tuple(late))
    else:
        oattn, lse = _attn_fwd(qcat, kcat, vv, nb, seq, tq_f)
        late_slots = late
    w_out = late_slots[0].reshape(D_MODEL, D_MODEL)
    w_gate, w_up = late_slots[1].reshape(D_FF, D_MODEL), late_slots[2].reshape(D_FF, D_MODEL)
    w_down = late_slots[3].reshape(D_FF, D_MODEL)
    lbl_f, lbl_b = lb_full[0], lb_full[1]
    o_f, o_b, save_f, save_b = _gla_fwd(hq, hi, (zf, zb), (lbl_f, lbl_b), nb, seq, group)
    ya = _gla_combine(o_f, o_b, hg, g_hgrn, tm_fwd)
    x1, x2, gate_b, up_b, loss_row = _post_fwd(xt, ya, oattn, tgt, g_mla, w_out, g2, w_gate, w_up, w_down, g_fin, tm_fwd)

    (dx1, d_ya, d_oattn, ycat_b, dx1_b, h2_b, dgate_b, dup_b, act_b, dx2_b, d_g_mla, d_g2, d_g_fin) = _post_bwd(
        x1, x2, gate_b, up_b, ya, oattn, tgt, g_mla, w_out, g2, w_gate, w_up, w_down, g_fin, tm)
    d_w_gate = _matmul_tn(dgate_b, h2_b, 512, tt, "gate")
    d_w_up = _matmul_tn(dup_b, h2_b, 512, tt, "up")
    d_w_down = _matmul_tn(act_b, dx2_b, 512, tt, "down")
    d_w_out = _matmul_tn(ycat_b, dx1_b, D_MODEL, tt, "out")
    late_g = [d_w_out.reshape(N_DEV, D_MODEL // N_DEV, D_MODEL)] + [
        g.reshape(N_DEV, D_FF // N_DEV, D_MODEL) for g in (d_w_gate, d_w_up, d_w_down)]
    if distributed:
        dq, dk, dv, *late_g = _attn_bwd(qcat, kcat, vv, oattn, lse, d_oattn, nb, seq, tq_b, exchange=tuple(late_g))
    else:
        dq, dk, dv = _attn_bwd(qcat, kcat, vv, oattn, lse, d_oattn, nb, seq, tq_b)
    d_o, d_hg, d_g_hgrn = _gla_combine_bwd(o_f, o_b, hg, g_hgrn, d_ya, tm_fwd)
    dq_f, dv_f, dz_f, dq_b, dv_b, dz_b, dl_f, dl_b = _gla_bwd(
        hq, hi, (zf, zb), (lbl_f, lbl_b), (save_f, save_b), d_o, nb, seq, group)
    grad_x, h1_b, dproj_b, d_w_q, d_w_kv, d_g1, d_g_qa, d_g_kva = _inproj_qkv_bwd(
        xt, g1, w_in, dx1, [[dq_f, dq_b], [dv_f, dv_b], [dz_f], [dz_b], [d_hg]], cq, ckv, g_qa, g_kva, w_q, w_kv, tables,
        dq, dk, dv, seq, tm_fwd)
    half = D_MODEL // 2
    in_slots = lambda g: g.reshape(N_DEV, D_IN // N_DEV, half)
    g_in_a = in_slots(_matmul_tn(dproj_b, h1_b, half, tt, "in_a", b_cols=(0, half), k_out=D_IN))
    if distributed:
        d_w_in_b, g_in_a = _matmul_tn(dproj_b, h1_b, half, tt, "in_b", b_cols=(half, half), k_out=D_IN, exchange=(g_in_a,))
    else:
        d_w_in_b = _matmul_tn(dproj_b, h1_b, half, tt, "in_b", b_cols=(half, half), k_out=D_IN)

    early_g = [in_slots(d_w_in_b), _unarrange_w_q_t(d_w_q).reshape(N_DEV, 768 // N_DEV, Q_LORA), _cols_to_slots(d_w_kv)]
    d_lb = jnp.stack([jnp.sum(dl_f, axis=0), jnp.sum(dl_b, axis=0)], axis=0)
    small_grads = [d_g1, d_g_hgrn, d_g_qa, d_g_kva, d_g_mla, d_g2, d_g_fin]
    return loss_row, grad_x.reshape(nb, seq, D_MODEL), g_in_a, early_g, late_g, small_grads, d_lb


def kernel(x, norm1_g, w_in, lb_logits, hgrn_norm_g, q_a_norm_g, w_q_b, kv_a_norm_g, w_kv_b, mla_norm_g, w_out, norm2_g, w_gate, w_up, w_down, final_norm_g, loss_target, m_norm1_g, m_w_in, m_lb_logits, m_hgrn_norm_g, m_q_a_norm_g, m_w_q_b, m_kv_a_norm_g, m_w_kv_b, m_mla_norm_g, m_w_out, m_norm2_g, m_w_gate, m_w_up, m_w_down, m_final_norm_g, v_norm1_g, v_w_in, v_lb_logits, v_hgrn_norm_g, v_q_a_norm_g, v_w_q_b, v_kv_a_norm_g, v_w_kv_b, v_mla_norm_g, v_w_out, v_norm2_g, v_w_gate, v_w_up, v_w_down, v_final_norm_g):
    big_w = [w_in, w_q_b, w_kv_b, w_out, w_gate, w_up, w_down]
    big_m = [m_w_in, m_w_q_b, m_w_kv_b, m_w_out, m_w_gate, m_w_up, m_w_down]
    big_v = [v_w_in, v_w_q_b, v_w_kv_b, v_w_out, v_w_gate, v_w_up, v_w_down]
    small_w = [norm1_g, hgrn_norm_g, q_a_norm_g, kv_a_norm_g, mla_norm_g, norm2_g, final_norm_g]
    small_m = [m_norm1_g, m_hgrn_norm_g, m_q_a_norm_g, m_kv_a_norm_g, m_mla_norm_g, m_norm2_g, m_final_norm_g]
    small_v = [v_norm1_g, v_hgrn_norm_g, v_q_a_norm_g, v_kv_a_norm_g, v_mla_norm_g, v_norm2_g, v_final_norm_g]
    seq = x.shape[1]
    my_id = 4 * lax.axis_index("x") + 2 * lax.axis_index("y") + lax.axis_index("c")

    shard = lambda w: w[0].astype(BF16)
    col_t = lambda w: jnp.swapaxes(w, 1, 2)[0]
    shard_t = lambda w: col_t(w).astype(BF16)
    g_in, g_q, g_kv, g_lb = _all_gather_call([shard_t(w_in), shard_t(w_q_b), shard(w_kv_b), lb_logits.reshape(4, 64)])
    early_full = (g_in.reshape(D_IN, D_MODEL), g_q.reshape(768, Q_LORA), _cols_from_slots(g_kv))
    lb_full = g_lb.reshape(N_DEV, 2, 2, 64).transpose(1, 2, 0, 3).reshape(2, 2, 512)

    as_row = lambda a: a.reshape(1, -1)
    loss_row, grad_x, recv_in_a, early_g, late_recv, small_g, d_lb = _step_core(
        x, loss_target, [as_row(s) for s in small_w], lb_full, early_full,
        [shard(w_out), shard_t(w_gate), shard_t(w_up), shard(w_down)], seq, min(16, seq // CHUNK),
        (256, 512, min(1024, seq), min(1024, seq), min(2048, 2 * seq)), True)

    grads, deltas, new_ms, new_vs = {}, {}, {}, {}
    views = {name: (col_t if name in ("w_in", "w_q_b", "w_gate", "w_up") else (lambda a: a[0])) for name, _, _, _ in BIG}
    backs = {name: ((lambda a: jnp.swapaxes(a[None], 1, 2)) if name in ("w_in", "w_q_b", "w_gate", "w_up") else (lambda a: a[None]))
             for name, _, _, _ in BIG}
    by_name = {name: (w, m, v) for (name, _, _, _), w, m, v in zip(BIG, big_w, big_m, big_v)}
    late_names = ["w_out", "w_gate", "w_up", "w_down"]
    n_small = len(small_g)
    g_l, d_l, nm_l, nv_l, recv = _adamw_recv_hosting(
        [views[n](by_name[n][0]) for n in late_names], list(late_recv), [views[n](by_name[n][1]) for n in late_names],
        [views[n](by_name[n][2]) for n in late_names],
        early_g + small_g + [d_lb.reshape(4, 512), loss_row], [True] * 3 + [False] * (n_small + 2))
    for i, name in enumerate(late_names):
        grads[name], deltas[name], new_ms[name], new_vs[name] = (backs[name](a[i]) for a in (g_l, d_l, nm_l, nv_l))
    sums = _sum_slots_call(recv[3:])
    g_small = [g.reshape(s.shape) for g, s in zip(sums[:n_small], small_w)]
    g_lb_own = lax.dynamic_index_in_dim(sums[n_small].reshape(2, 2, N_DEV, 64), my_id, axis=2, keepdims=False)
    loss = sums[n_small + 1][0, 0]

    for name, r in zip(["w_in", "w_q_b", "w_kv_b"], recv[:3]):
        w, m, v = (views[name](a) for a in by_name[name])
        g, d, nm, nv = _adamw_recv_halves(w, (recv_in_a, r), m, v, name) if name == "w_in" else _adamw_recv(w, r, m, v, name)
        grads[name], deltas[name], new_ms[name], new_vs[name] = (backs[name](a) for a in (g, d, nm, nv))
    lb_rows = lambda a: a.reshape(4, 64)
    d_s, nm_s, nv_s = _adamw_small(
        [as_row(a) for a in small_w] + [lb_rows(lb_logits)], [as_row(a) for a in g_small] + [lb_rows(g_lb_own)],
        [as_row(a) for a in small_m] + [lb_rows(m_lb_logits)], [as_row(a) for a in small_v] + [lb_rows(v_lb_logits)])
    for i, (s, (name, _)) in enumerate(zip(small_w + [lb_logits], SMALL + (("lb_logits", 0),))):
        grads[name] = (g_small + [g_lb_own])[i]
        deltas[name], new_ms[name], new_vs[name] = d_s[i].reshape(s.shape), nm_s[i].reshape(s.shape), nv_s[i].reshape(s.shape)

    order = ["norm1_g", "w_in", "lb_logits", "hgrn_norm_g", "q_a_norm_g", "w_q_b", "kv_a_norm_g", "w_kv_b", "mla_norm_g",
             "w_out", "norm2_g", "w_gate", "w_up", "w_down", "final_norm_g"]
    return (loss, grad_x, *[grads[n] for n in order], *[deltas[n] for n in order],
            *[new_ms[n] for n in order], *[new_vs[n] for n in order])
```

```python
import functools

import jax
import jax.numpy as jnp
from jax import lax
from jax.experimental import pallas as pl
from jax.experimental.pallas import tpu as pltpu

F32 = jnp.float32
BF16 = jnp.bfloat16

N_DEV = 8
D_MODEL = 1024
D_FF = 2816
A_WIDTH = 512
HEAD_PAIR = 128
CHUNK = 64
B_HEADS = 4
B_NOPE = 128
B_ROPE = 64
B_V = 128
QK_PAD = 256
Q_LORA = 384
KV_LORA = 256
D_IN = 3264
D_IN_PAD = 3328
IN_WIDTHS = (512, 512, 512, 512, 512, Q_LORA, KV_LORA, 128)
ROPE_THETA = 10000.0
EPS = 1e-6
ATTN_SCALE = (B_NOPE + B_ROPE) ** -0.5
ATTN_SUB = 256
ATTN_SUB_BWD = 256
ROW_SUB = 256
ADAM_LR, ADAM_B1, ADAM_B2, ADAM_EPS, ADAM_WD, ADAM_STEP = 0.001, 0.9, 0.999, 1e-08, 0.01, 10
VMEM_LIMIT = 60 * 1024 * 1024
MESH = pl.DeviceIdType.MESH

BIG = (("w_in", 1024, D_IN, 1), ("w_q_b", Q_LORA, 768, 1), ("w_kv_b", KV_LORA, 1024, 1), ("w_out", 1024, 1024, 0),
       ("w_gate", 1024, D_FF, 1), ("w_up", 1024, D_FF, 1), ("w_down", D_FF, 1024, 0))
SMALL = (("norm1_g", 1024), ("hgrn_norm_g", 512), ("q_a_norm_g", 384), ("kv_a_norm_g", 256), ("mla_norm_g", 512),
         ("norm2_g", 1024), ("final_norm_g", 1024))


def _params(**kw):
    return pltpu.CompilerParams(vmem_limit_bytes=VMEM_LIMIT, **kw)


def _const_spec(shape):
    return pl.BlockSpec(shape, lambda *_: (0,) * len(shape), pipeline_mode=pl.Buffered(1))


def _dot(a, b):
    return jnp.dot(a, b, preferred_element_type=F32)


def _dot_nt(a, b):
    return lax.dot_general(a, b, (((1,), (1,)), ((), ())), preferred_element_type=F32)


def _dot_tn(a, b):
    return lax.dot_general(a, b, (((0,), (0,)), ((), ())), preferred_element_type=F32)


@jax.custom_vjp
def _mm(a, b):
    return _dot(a.astype(BF16), b.astype(BF16))


def _mm_fwd(a, b):
    return _mm(a, b), (a, b)


def _mm_bwd(res, g):
    a, b = res
    gb = g.astype(BF16)
    return _dot_nt(gb, b.astype(BF16)), _dot_tn(a.astype(BF16), gb)


_mm.defvjp(_mm_fwd, _mm_bwd)


@jax.custom_vjp
def _mm_nt(a, b):
    return _dot_nt(a.astype(BF16), b.astype(BF16))


def _mm_nt_fwd(a, b):
    return _mm_nt(a, b), (a, b)


def _mm_nt_bwd(res, g):
    a, b = res
    gb = g.astype(BF16)
    return _dot(gb, b.astype(BF16)), _dot_tn(gb, a.astype(BF16))


_mm_nt.defvjp(_mm_nt_fwd, _mm_nt_bwd)


@jax.custom_vjp
def _mm_tn(a, b):
    return _dot_tn(a.astype(BF16), b.astype(BF16))


def _mm_tn_fwd(a, b):
    return _mm_tn(a, b), (a, b)


def _mm_tn_bwd(res, g):
    a, b = res
    gb = g.astype(BF16)
    return _dot_nt(b.astype(BF16), gb), _dot(a.astype(BF16), gb)


_mm_tn.defvjp(_mm_tn_fwd, _mm_tn_bwd)


def _dot_exact_rhs(a, m):
    hi = a.astype(BF16)
    lo = (a - hi.astype(F32)).astype(BF16)
    return _dot(hi, m) + _dot(lo, m)


@jax.custom_vjp
def _group_mean(a, m):
    return _dot_exact_rhs(a, m)


def _group_mean_fwd(a, m):
    return _group_mean(a, m), m


def _group_mean_bwd(m, g):
    return _dot_exact_rhs(g, m), jnp.zeros_like(m)


_group_mean.defvjp(_group_mean_fwd, _group_mean_bwd)


def _roll_rows(a, shift):
    return pltpu.roll(a, shift, 0)


def _cumsum_rows_raw(a, reverse):
    n = a.shape[0]
    row = lax.broadcasted_iota(jnp.int32, a.shape, 0)
    s = 1
    while s < n:
        if reverse:
            a = a + jnp.where(row < n - s, _roll_rows(a, n - s), 0.0)
        else:
            a = a + jnp.where(row >= s, _roll_rows(a, s), 0.0)
        s *= 2
    return a


@functools.partial(jax.custom_vjp, nondiff_argnums=(1,))
def _cumsum_rows(a, reverse):
    return _cumsum_rows_raw(a, reverse)


def _cumsum_rows_fwd(a, reverse):
    return _cumsum_rows_raw(a, reverse), None


def _cumsum_rows_bwd(reverse, _, g):
    return (_cumsum_rows_raw(g, not reverse),)


_cumsum_rows.defvjp(_cumsum_rows_fwd, _cumsum_rows_bwd)


def _rms(x, g):
    r = lax.rsqrt(jnp.mean(x * x, axis=-1, keepdims=True) + EPS)
    return x * r * g


def _rms_bwd(x, g, dy):
    r = lax.rsqrt(jnp.mean(x * x, axis=-1, keepdims=True) + EPS)
    xh = x * r
    dg = jnp.sum(dy * xh, axis=0, keepdims=True)
    dxh = dy * g
    dx = r * (dxh - xh * jnp.mean(dxh * xh, axis=-1, keepdims=True))
    return dx, dg


def _sigmoid(a):
    return jax.nn.sigmoid(a)


def _mesh_place():
    x, y, c = lax.axis_index("x"), lax.axis_index("y"), lax.axis_index("c")
    return x, y, c


def _dev_index(p):
    return 4 * p[0] + 2 * p[1] + p[2]


def _comm_sems(n):
    return [pltpu.SemaphoreType.DMA((n, 7)), pltpu.SemaphoreType.DMA((n, 7)), pltpu.SemaphoreType.DMA((n,))]


def _gather_protocol(ins, outs, send_sems, recv_sems, local_sems):
    n = len(ins)
    x, y, c = _mesh_place()
    me, sibling = (x, y, c), (x, y, 1 - c)
    chips = [(1 - x, y), (x, 1 - y), (1 - x, 1 - y)]

    def copy(a, k, block, to, src=None):
        slot = outs[a].at[_dev_index(block)]
        return pltpu.make_async_remote_copy(
            src_ref=slot if src is None else src, dst_ref=slot,
            send_sem=send_sems.at[a, k], recv_sem=recv_sems.at[a, k], device_id=to, device_id_type=MESH)

    def mine(a):
        return pltpu.make_async_copy(ins[a], outs[a].at[_dev_index(me)], local_sems.at[a])

    def first(a):
        return [copy(a, 0, me, sibling, src=ins[a])] + [copy(a, 1 + j, me, (*chip, c), src=ins[a]) for j, chip in enumerate(chips)]

    def start():
        for a in range(n):
            mine(a).start()
            for cp in first(a):
                cp.start()

    def forward():
        for a in range(n):
            for j, chip in enumerate(chips):
                copy(a, 1 + j, (*chip, c), me).wait_recv()
                copy(a, 4 + j, (*chip, c), sibling).start()

    def finish():
        for a in range(n):
            copy(a, 0, sibling, me).wait_recv()
            for j, chip in enumerate(chips):
                copy(a, 4 + j, (*chip, 1 - c), me).wait_recv()
        for a in range(n):
            mine(a).wait()
            for cp in first(a):
                cp.wait_send()
            for j, chip in enumerate(chips):
                copy(a, 4 + j, (*chip, c), sibling).wait_send()

    return start, forward, finish


def _exchange_protocol(ins, outs, scatter, send_sems, recv_sems, local_sems):
    n = len(ins)
    x, y, c = _mesh_place()
    me = (x, y, c)
    my_id = _dev_index(me)
    rels = [(dx, dy, dc) for dx in (0, 1) for dy in (0, 1) for dc in (0, 1)][1:]

    def peer_of(rel):
        return tuple(1 - v if d else v for v, d in zip(me, rel))

    def src(a, dev):
        return ins[a].at[dev] if scatter[a] else ins[a]

    def send(a, k):
        peer = peer_of(rels[k])
        return pltpu.make_async_remote_copy(
            src_ref=src(a, _dev_index(peer)), dst_ref=outs[a].at[my_id],
            send_sem=send_sems.at[a, k], recv_sem=recv_sems.at[a, k], device_id=peer, device_id_type=MESH)

    def arrival(a, k):
        peer = peer_of(rels[k])
        return pltpu.make_async_remote_copy(
            src_ref=src(a, my_id), dst_ref=outs[a].at[_dev_index(peer)],
            send_sem=send_sems.at[a, k], recv_sem=recv_sems.at[a, k], device_id=peer, device_id_type=MESH)

    def own(a):
        return pltpu.make_async_copy(src(a, my_id), outs[a].at[my_id], local_sems.at[a])

    def start():
        for a in range(n):
            own(a).start()
            for k in range(7):
                send(a, k).start()

    def finish():
        for a in range(n):
            for k in range(7):
                arrival(a, k).wait_recv()
        for a in range(n):
            for k in range(7):
                send(a, k).wait_send()
            own(a).wait()

    return start, finish


def _slot_shapes(blocks, scatter=None):
    return [jax.ShapeDtypeStruct(b.shape if (scatter and scatter[a]) else (N_DEV,) + b.shape, b.dtype) for a, b in enumerate(blocks)]


def _all_gather_call(blocks):
    n = len(blocks)

    def body(*refs):
        start, forward, finish = _gather_protocol(refs[:n], refs[n:2 * n], *refs[2 * n:])
        start()
        forward()
        finish()

    any_spec = pl.BlockSpec(memory_space=pl.ANY)
    return pl.pallas_call(
        body, name="weights_all_gather", out_shape=_slot_shapes(blocks),
        in_specs=[any_spec] * n, out_specs=[any_spec] * n, scratch_shapes=_comm_sems(n),
    )(*blocks)


def _sum_slots_call(recvs):
    n = len(recvs)

    def body(*refs):
        for in_ref, out_ref in zip(refs[:n], refs[n:]):
            acc = in_ref[0]
            for j in range(1, N_DEV):
                acc = acc + in_ref[j]
            out_ref[...] = acc

    return pl.pallas_call(
        body, name="small_grad_sum", out_shape=[jax.ShapeDtypeStruct(r.shape[1:], F32) for r in recvs],
        compiler_params=_params(),
    )(*recvs)


def _adam_update(w, g, m, v):
    nm = ADAM_B1 * m + (1.0 - ADAM_B1) * g
    nv = ADAM_B2 * v + (1.0 - ADAM_B2) * (g * g)
    bc1 = 1.0 - ADAM_B1 ** ADAM_STEP
    bc2 = 1.0 - ADAM_B2 ** ADAM_STEP
    return -ADAM_LR * ((nm / bc1) / (jnp.sqrt(nv / bc2) + ADAM_EPS) + ADAM_WD * w), nm, nv


def _adamw_recv(w, recv, m, v, tag):
    r, c = w.shape
    tr = r
    for cand in (512, 256, 128):
        if r > cand and r % cand == 0:
            tr = cand
            break

    def body(w_ref, r_ref, m_ref, v_ref, g_ref, d_ref, nm_ref, nv_ref):
        g = r_ref[0].astype(F32)
        for j in range(1, N_DEV):
            g = g + r_ref[j].astype(F32)
        g_ref[...] = g
        d_ref[...], nm_ref[...], nv_ref[...] = _adam_update(w_ref[...], g, m_ref[...], v_ref[...])

    spec = pl.BlockSpec((tr, c), lambda i: (i, 0))
    return pl.pallas_call(
        body, name="adamw_" + tag, out_shape=[jax.ShapeDtypeStruct(w.shape, F32)] * 4, grid=(r // tr,),
        in_specs=[spec, pl.BlockSpec((N_DEV, tr, c), lambda i: (0, i, 0)), spec, spec], out_specs=[spec] * 4,
        compiler_params=_params(),
    )(w, recv, m, v)


def _adamw_recv_halves(w, recv_halves, m, v, tag):
    r, c = w.shape
    half = c // 2

    def body(w_ref, ra_ref, rb_ref, m_ref, v_ref, g_ref, d_ref, nm_ref, nv_ref):
        def update(r_ref):
            g = r_ref[0].astype(F32)
            for j in range(1, N_DEV):
                g = g + r_ref[j].astype(F32)
            g_ref[...] = g
            d_ref[...], nm_ref[...], nv_ref[...] = _adam_update(w_ref[...], g, m_ref[...], v_ref[...])

        pl.when(pl.program_id(0) == 0)(lambda: update(ra_ref))
        pl.when(pl.program_id(0) == 1)(lambda: update(rb_ref))

    spec = pl.BlockSpec((r, half), lambda j: (0, j))
    whole = pl.BlockSpec((N_DEV, r, half), lambda j: (0, 0, 0))
    return pl.pallas_call(
        body, name="adamw_" + tag, out_shape=[jax.ShapeDtypeStruct(w.shape, F32)] * 4, grid=(2,),
        in_specs=[spec, whole, whole, spec, spec], out_specs=[spec] * 4, compiler_params=_params(),
    )(w, *recv_halves, m, v)


def _adamw_recv_hosting(ws, recvs, ms, vs, blocks, scatter):
    n, ne = len(ws), len(blocks)
    rows = max(w.shape[0] for w in ws)
    cols = ws[0].shape[1]
    assert all(w.shape[1] == cols for w in ws)

    def body(*refs):
        ins, ex_in = refs[:4 * n], refs[4 * n:4 * n + ne]
        outs, ex_out = refs[4 * n + ne:8 * n + ne], refs[8 * n + ne:8 * n + 2 * ne]
        in_buf, recv_buf, out_buf, in_sems, out_sems = refs[8 * n + 2 * ne:8 * n + 2 * ne + 5]
        start, finish = _exchange_protocol(ex_in, ex_out, scatter, *refs[8 * n + 2 * ne + 5:])
        start()
        for a in range(n):
            r = pl.ds(0, ws[a].shape[0])
            loads = [pltpu.make_async_copy(ins[k * n + a], in_buf.at[j, r], in_sems.at[j]) for j, k in enumerate((0, 2, 3))]
            loads.append(pltpu.make_async_copy(ins[n + a], recv_buf.at[:, r], in_sems.at[3]))
            for cp in loads:
                cp.start()
            for cp in loads:
                cp.wait()
            g = recv_buf[0, r].astype(F32)
            for j in range(1, N_DEV):
                g = g + recv_buf[j, r].astype(F32)
            out_buf[0, r] = g
            out_buf[1, r], out_buf[2, r], out_buf[3, r] = _adam_update(in_buf[0, r], g, in_buf[1, r], in_buf[2, r])
            stores = [pltpu.make_async_copy(out_buf.at[k, r], outs[k * n + a], out_sems.at[k]) for k in range(4)]
            for cp in stores:
                cp.start()
            for cp in stores:
                cp.wait()
        finish()

    any_spec = pl.BlockSpec(memory_space=pl.ANY)
    out = pl.pallas_call(
        body, name="adamw_late_and_grad_exchange",
        out_shape=[jax.ShapeDtypeStruct(w.shape, F32) for w in ws] * 4 + _slot_shapes(blocks, scatter),
        in_specs=[any_spec] * (4 * n + ne), out_specs=[any_spec] * (4 * n + ne),
        scratch_shapes=[pltpu.VMEM((3, rows, cols), F32), pltpu.VMEM((N_DEV, rows, cols), BF16), pltpu.VMEM((4, rows, cols), F32),
                        pltpu.SemaphoreType.DMA((4,)), pltpu.SemaphoreType.DMA((4,))] + _comm_sems(ne),
        compiler_params=_params(),
    )(*ws, *recvs, *ms, *vs, *blocks)
    return out[:n], out[n:2 * n], out[2 * n:3 * n], out[3 * n:4 * n], out[4 * n:]


def _adamw_small(ws, gs, ms, vs):
    n = len(ws)

    def body(*refs):
        ins, outs = refs[:4 * n], refs[4 * n:]
        for a in range(n):
            d, nm, nv = _adam_update(ins[a][...], ins[n + a][...], ins[2 * n + a][...], ins[3 * n + a][...])
            outs[a][...], outs[n + a][...], outs[2 * n + a][...] = d, nm, nv

    out = pl.pallas_call(
        body, name="adamw_small", out_shape=[jax.ShapeDtypeStruct(w.shape, F32) for w in ws] * 3, compiler_params=_params(),
    )(*ws, *gs, *ms, *vs)
    return out[:n], out[n:2 * n], out[2 * n:]


def _rope_tables(seq):
    inv = 1.0 / (ROPE_THETA ** (jnp.arange(0, B_ROPE, 2, dtype=F32) / B_ROPE))
    ang = jnp.arange(seq, dtype=F32)[:, None] * inv[None, :]
    cos, sin = jnp.cos(ang), jnp.sin(ang)
    z32, z64 = jnp.zeros_like(cos), jnp.zeros((seq, 64), F32)
    cos_t = jnp.concatenate([cos, cos, z64], axis=1)
    sin_a = jnp.concatenate([-sin, z32, z64], axis=1)
    sin_b = jnp.concatenate([z32, sin, z64], axis=1)
    return cos_t, sin_a, sin_b


def _rope(t, cos_t, sin_a, sin_b):
    return t * cos_t + pltpu.roll(t, 96, 1) * sin_a + pltpu.roll(t, 32, 1) * sin_b


def _rope_t(d, cos_t, sin_a, sin_b):
    return d * cos_t + pltpu.roll(d * sin_a, 32, 1) + pltpu.roll(d * sin_b, 96, 1)


def _inproj_qkv(x, g1, w_in, g_qa, g_kva, w_q, w_kv, tables, seq, tm):
    t = x.shape[0]
    nblk = seq // tm
    n_plain = 7
    offs = [sum(IN_WIDTHS[:j]) for j in range(len(IN_WIDTHS))]

    def body(x_ref, g_ref, w_ref, gq_ref, gk_ref, wq_ref, wkv_ref, c_ref, sa_ref, sb_ref, *outs):
        q_out, k_out, v_out = outs[n_plain:]
        for j in range(tm // min(tm, ROW_SUB)):
            r = pl.ds(j * min(tm, ROW_SUB), min(tm, ROW_SUB))
            h = _rms(x_ref[r, :], g_ref[...]).astype(BF16)
            proj = lambda g: _dot_nt(h, w_ref[offs[g]:offs[g] + IN_WIDTHS[g], :])
            for g in range(5):
                outs[g][r, :] = proj(g)
            cq, ckv, kr = proj(5), proj(6), proj(7)
            outs[5][r, :] = cq
            outs[6][r, :] = ckv
            cos_t, sin_a, sin_b = c_ref[r, :], sa_ref[r, :], sb_ref[r, :]
            cqn = _rms(cq, gq_ref[...]).astype(BF16)
            ckn = _rms(ckv, gk_ref[...]).astype(BF16)
            kr_rot = _rope(kr, cos_t, sin_a, sin_b).astype(BF16)
            for hd in range(B_HEADS):
                lo = hd * QK_PAD
                q_out[r, lo:lo + 128] = (_dot_nt(cqn, wq_ref[lo:lo + 128, :]) * ATTN_SCALE).astype(BF16)
                qr = _rope(_dot_nt(cqn, wq_ref[lo + 128:lo + 256, :]), cos_t, sin_a, sin_b)
                q_out[r, lo + 128:lo + 256] = (qr * ATTN_SCALE).astype(BF16)
                k_out[r, lo:lo + 128] = _dot(ckn, wkv_ref[:, lo:lo + 128]).astype(BF16)
                k_out[r, lo + 128:lo + 256] = kr_rot
                v_out[r, hd * B_V:(hd + 1) * B_V] = _dot(ckn, wkv_ref[:, lo + 128:lo + 256]).astype(BF16)

    tok = lambda wd: pl.BlockSpec((tm, wd), lambda i: (i, 0))
    tab = pl.BlockSpec((tm, 128), lambda i: (i % nblk, 0))
    widths = list(IN_WIDTHS[:n_plain]) + [B_HEADS * QK_PAD, B_HEADS * QK_PAD, B_HEADS * B_V]
    dtypes = [F32] * n_plain + [BF16] * 3
    return pl.pallas_call(
        body, name="inproj_qkv_fwd", grid=(t // tm,),
        out_shape=[jax.ShapeDtypeStruct((t, wd), dt) for wd, dt in zip(widths, dtypes)],
        in_specs=[tok(D_MODEL), _const_spec((1, D_MODEL)), _const_spec((D_IN_PAD, D_MODEL)), _const_spec((1, Q_LORA)),
                  _const_spec((1, KV_LORA)), _const_spec((B_HEADS * QK_PAD, Q_LORA)), _const_spec((KV_LORA, 1024)), tab, tab, tab],
        out_specs=[tok(wd) for wd in widths],
        compiler_params=_params(),
    )(x, g1, w_in, g_qa, g_kva, w_q, w_kv, *tables)


def _step_index(nq):
    return (pl.program_id(0) * B_HEADS + pl.program_id(1)) * nq + pl.program_id(2)


def _attn_fwd(qcat, kcat, v, nb, seq, tq, gather=()):
    t = qcat.shape[0]
    nq = seq // tq
    ng = len(gather)
    steps = nb * B_HEADS * nq

    def body(q_ref, k_ref, v_ref, *rest):
        o_ref, lse_ref = rest[ng:ng + 2]
        if ng:
            start, forward, finish = _gather_protocol(rest[:ng], rest[ng + 2:2 * ng + 2], *rest[2 * ng + 2:])
            pl.when(_step_index(nq) == 0)(start)
            pl.when(_step_index(nq) == (3 * steps) // 4)(forward)
        for j in range(tq // ATTN_SUB):
            r = pl.ds(j * ATTN_SUB, ATTN_SUB)
            s = _dot_nt(q_ref[r, :], k_ref[...])
            m = jnp.max(s, axis=-1, keepdims=True)
            p = jnp.exp(s - m)
            l = jnp.sum(p, axis=-1, keepdims=True)
            o_ref[r, :] = _dot(p.astype(BF16), v_ref[...]) / l
            lse_ref[0, r, :] = m + jnp.log(l)
        if ng:
            pl.when(_step_index(nq) == steps - 1)(finish)

    any_spec = pl.BlockSpec(memory_space=pl.ANY)
    return pl.pallas_call(
        body, name="attn_fwd", grid=(nb, B_HEADS, nq),
        out_shape=[jax.ShapeDtypeStruct((t, B_HEADS * B_V), F32), jax.ShapeDtypeStruct((B_HEADS, t, 1), F32)] + _slot_shapes(gather),
        in_specs=[pl.BlockSpec((tq, QK_PAD), lambda b, h, i: (b * nq + i, h)),
                  pl.BlockSpec((seq, QK_PAD), lambda b, h, i: (b, h)),
                  pl.BlockSpec((seq, B_V), lambda b, h, i: (b, h))] + [any_spec] * ng,
        out_specs=[pl.BlockSpec((tq, B_V), lambda b, h, i: (b * nq + i, h)),
                   pl.BlockSpec((1, tq, 1), lambda b, h, i: (h, b * nq + i, 0))] + [any_spec] * ng,
        scratch_shapes=_comm_sems(ng) if ng else [],
        compiler_params=_params(),
    )(qcat, kcat, v, *gather)


def _attn_bwd(qcat, kcat, v, o, lse, do, nb, seq, tq, exchange=()):
    t = qcat.shape[0]
    nq = seq // tq
    ne = len(exchange)
    steps = nb * B_HEADS * nq

    def body(q_ref, k_ref, v_ref, o_ref, lse_ref, do_ref, *rest):
        dq_ref, dk_ref, dv_ref = rest[ne:ne + 3]
        if ne:
            start, finish = _exchange_protocol(rest[:ne], rest[ne + 3:2 * ne + 3], [True] * ne, *rest[2 * ne + 3:])
            pl.when(_step_index(nq) == 0)(start)

        @pl.when(pl.program_id(2) == 0)
        def _():
            dv_ref[...] = jnp.zeros_like(dv_ref)
            dk_ref[...] = jnp.zeros_like(dk_ref)

        for j in range(tq // ATTN_SUB_BWD):
            r = pl.ds(j * ATTN_SUB_BWD, ATTN_SUB_BWD)
            q, k = q_ref[r, :], k_ref[...]
            do_f = do_ref[r, :].astype(F32)
            delta = jnp.sum(do_f * o_ref[r, :], axis=-1, keepdims=True)
            dob = do_f.astype(BF16)
            p = jnp.exp(_dot_nt(q, k) - lse_ref[0, r, :])
            ds = (p * (_dot_nt(dob, v_ref[...]) - delta)).astype(BF16)
            dq_ref[r, :] = _dot(ds, k).astype(dq_ref.dtype)
            dv_ref[...] += _dot_tn(p.astype(BF16), dob)
            dk_ref[...] += _dot_tn(ds, q)
        if ne:
            pl.when(_step_index(nq) == steps - 1)(finish)

    qspec = lambda wd: pl.BlockSpec((tq, wd), lambda b, h, i: (b * nq + i, h))
    kspec = lambda wd: pl.BlockSpec((seq, wd), lambda b, h, i: (b, h))
    any_spec = pl.BlockSpec(memory_space=pl.ANY)
    return pl.pallas_call(
        body, name="attn_bwd", grid=(nb, B_HEADS, nq),
        out_shape=[jax.ShapeDtypeStruct((t, B_HEADS * QK_PAD), BF16), jax.ShapeDtypeStruct((t, B_HEADS * QK_PAD), F32),
                   jax.ShapeDtypeStruct((t, B_HEADS * B_V), F32)] + _slot_shapes(exchange, [True] * ne),
        in_specs=[qspec(QK_PAD), kspec(QK_PAD), kspec(B_V), qspec(B_V),
                  pl.BlockSpec((1, tq, 1), lambda b, h, i: (h, b * nq + i, 0)), qspec(B_V)] + [any_spec] * ne,
        out_specs=[qspec(QK_PAD), kspec(QK_PAD), kspec(B_V)] + [any_spec] * ne,
        scratch_shapes=_comm_sems(ne) if ne else [],
        compiler_params=_params(),
    )(qcat, kcat, v, o, lse, do, *exchange)


def _gla_consts(reverse):
    row = lax.broadcasted_iota(jnp.int32, (CHUNK, CHUNK), 0)
    col = lax.broadcasted_iota(jnp.int32, (CHUNK, CHUNK), 1)
    causal = (row <= col) if reverse else (row >= col)
    lane = lax.broadcasted_iota(jnp.int32, (1, HEAD_PAIR), 1)
    m0 = (lane < 64).astype(F32)
    m1 = 1.0 - m0
    r2 = lax.broadcasted_iota(jnp.int32, (HEAD_PAIR, HEAD_PAIR), 0)
    c2 = lax.broadcasted_iota(jnp.int32, (HEAD_PAIR, HEAD_PAIR), 1)
    same_head = ((r2 < 64) == (c2 < 64)).astype(F32)
    return causal, m0, m1, same_head


def _gla_chunk(hq, hi, z, l0, l1, st, consts, reverse):
    q_dec, k_inv, k_end, decay = _gla_gates(hq, z, l0, l1, reverse)
    o, st_new = _gla_state(q_dec, st, decay, _gla_increment(hi, k_end, consts))
    return o + _gla_intra(q_dec, k_inv, hi, consts), st_new


def _gla_gates(hq, z, l0, l1, reverse):
    mx = jnp.maximum(l0, l1)
    e0, e1 = jnp.exp(l0 - mx), jnp.exp(l1 - mx)
    lb = e0 / (e0 + e1)
    q = hq * _sigmoid(hq)
    sz = _sigmoid(z)
    log_f = jnp.log(lb + (1.0 - lb) * sz)
    k = (1.0 - lb) * (1.0 - sz)
    cum = _cumsum_rows(log_f, reverse)
    decay = jnp.exp(jnp.sum(log_f, axis=0, keepdims=True))
    k_inv = k * jnp.exp(-cum)
    return q * jnp.exp(cum), k_inv, k_inv * decay, decay


def _gla_intra(q_dec, k_inv, hi, consts):
    causal, m0, m1, _ = consts
    o = None
    for mh in (m0, m1):
        s = jnp.where(causal, _mm_nt(q_dec * mh, k_inv), 0.0)
        part = _mm(s, hi) * mh
        o = part if o is None else o + part
    return o


def _gla_increment(hi, k_end, consts):
    return _mm_tn(hi, k_end) * consts[3]


def _gla_state(q_dec, st, decay, inc):
    return _mm_nt(q_dec, st), st * decay + inc


GLA_DIRS = (False, True)
GLA_BATCH_FWD = 8
GLA_BATCH_BWD = 4


def _gla_fwd(hq, hi, zs, lbls, nb, seq, group):
    t = hq.shape[0]
    rows = group * CHUNK
    nblk = seq // rows
    n_chunks = seq // CHUNK
    nd = len(GLA_DIRS)

    def body(*refs):
        ins, outs, st_refs = refs[:4 * nd], refs[4 * nd:6 * nd], refs[6 * nd:]
        @pl.when(pl.program_id(2) == 0)
        def _():
            for st_ref in st_refs:
                st_ref[...] = jnp.zeros_like(st_ref)

        consts = [_gla_consts(rev) for rev in GLA_DIRS]
        work = [(d, rev, group - 1 - cc if rev else cc) for cc in range(group) for d, rev in enumerate(GLA_DIRS)]
        rows_of = lambda c: pl.ds(c * CHUNK, CHUNK)
        sts = [st_ref[...] for st_ref in st_refs]
        for w0 in range(0, len(work), GLA_BATCH_FWD):
            batch = work[w0:w0 + GLA_BATCH_FWD]
            gates, intra, incs = {}, {}, {}
            for d, rev, c in batch:
                hq_ref, _, z_ref, lbl_ref = ins[4 * d:4 * d + 4]
                gates[d, c] = _gla_gates(hq_ref[rows_of(c), :], z_ref[rows_of(c), :], lbl_ref[0:1, :], lbl_ref[1:2, :], rev)
            for d, rev, c in batch:
                hi_c = ins[4 * d + 1][rows_of(c), :]
                intra[d, c] = _gla_intra(gates[d, c][0], gates[d, c][1], hi_c, consts[d])
                incs[d, c] = _gla_increment(hi_c, gates[d, c][2], consts[d])
            for d, rev, c in batch:
                outs[nd + d][0, 0, c] = sts[d].astype(outs[nd + d].dtype)
                o_state, sts[d] = _gla_state(gates[d, c][0], sts[d], gates[d, c][3], incs[d, c])
                outs[d][rows_of(c), :] = (intra[d, c] + o_state).astype(outs[d].dtype)
        for st_ref, st in zip(st_refs, sts):
            st_ref[...] = st

    def tb(rev):
        return (lambda i: nblk - 1 - i) if rev else (lambda i: i)

    tok = lambda rev: pl.BlockSpec((rows, HEAD_PAIR), lambda b, p, i: (b * nblk + tb(rev)(i), p))
    lspec = pl.BlockSpec((2, HEAD_PAIR), lambda b, p, i: (0, p))
    sspec = lambda rev: pl.BlockSpec((1, 1, group, HEAD_PAIR, HEAD_PAIR), lambda b, p, i: (b, p, tb(rev)(i), 0, 0))
    args, in_specs = [], []
    for d, rev in enumerate(GLA_DIRS):
        args += [hq, hi, zs[d], lbls[d]]
        in_specs += [tok(rev), tok(rev), tok(rev), lspec]
    return pl.pallas_call(
        body, name="gla_fwd", grid=(nb, 4, nblk),
        out_shape=[jax.ShapeDtypeStruct((t, A_WIDTH), BF16)] * nd
        + [jax.ShapeDtypeStruct((nb, 4, n_chunks, HEAD_PAIR, HEAD_PAIR), BF16)] * nd,
        in_specs=in_specs, out_specs=[tok(rev) for rev in GLA_DIRS] + [sspec(rev) for rev in GLA_DIRS],
        scratch_shapes=[pltpu.VMEM((HEAD_PAIR, HEAD_PAIR), F32)] * nd,
        compiler_params=_params(),
    )(*args)


def _gla_bwd(hq, hi, zs, lbls, saved, do, nb, seq, group):
    t = hq.shape[0]
    rows = group * CHUNK
    nblk = seq // rows
    nd = len(GLA_DIRS)

    def body(*refs):
        ins, outs, dst_refs = refs[:6 * nd], refs[6 * nd:10 * nd], refs[10 * nd:]
        dl_refs = outs[3 * nd:]

        @pl.when(pl.program_id(2) == 0)
        def _():
            for dst_ref, dl_ref in zip(dst_refs, dl_refs):
                dst_ref[...] = jnp.zeros_like(dst_ref)
                dl_ref[...] = jnp.zeros_like(dl_ref)

        consts = [_gla_consts(rev) for rev in GLA_DIRS]
        dsts = [dst_ref[...] for dst_ref in dst_refs]
        dls = [[jnp.zeros((1, HEAD_PAIR), F32), jnp.zeros((1, HEAD_PAIR), F32)] for _ in GLA_DIRS]
        work = [(d, rev, cc if rev else group - 1 - cc) for cc in range(group) for d, rev in enumerate(GLA_DIRS)]
        for w0 in range(0, len(work), GLA_BATCH_BWD):
            vjps = {}
            for d, rev, c in work[w0:w0 + GLA_BATCH_BWD]:
                hq_ref, hi_ref, z_ref, lbl_ref, save_ref, _ = ins[6 * d:6 * d + 6]
                r = pl.ds(c * CHUNK, CHUNK)
                fn = functools.partial(_gla_chunk, consts=consts[d], reverse=rev)
                _, vjps[d, c] = jax.vjp(fn, hq_ref[r, :], hi_ref[r, :], z_ref[r, :], lbl_ref[0:1, :], lbl_ref[1:2, :],
                                         save_ref[0, 0, c].astype(F32))
            for d, rev, c in work[w0:w0 + GLA_BATCH_BWD]:
                dq_ref, dv_ref, dz_ref = outs[3 * d:3 * d + 3]
                r = pl.ds(c * CHUNK, CHUNK)
                d_hq, d_hi, d_z, d_l0, d_l1, dsts[d] = vjps[d, c]((ins[6 * d + 5][r, :].astype(F32), dsts[d]))
                dq_ref[r, :] = d_hq.astype(dq_ref.dtype)
                dv_ref[r, :] = d_hi.astype(dv_ref.dtype)
                dz_ref[r, :] = d_z.astype(dz_ref.dtype)
                dls[d] = [dls[d][0] + d_l0, dls[d][1] + d_l1]
        for d in range(nd):
            dst_refs[d][...] = dsts[d]
            dl_refs[d][0, 0:1, :] += dls[d][0]
            dl_refs[d][0, 1:2, :] += dls[d][1]

    def tb(rev):
        return (lambda i: i) if rev else (lambda i: nblk - 1 - i)

    tok = lambda rev: pl.BlockSpec((rows, HEAD_PAIR), lambda b, p, i: (b * nblk + tb(rev)(i), p))
    lspec = pl.BlockSpec((2, HEAD_PAIR), lambda b, p, i: (0, p))
    sspec = lambda rev: pl.BlockSpec((1, 1, group, HEAD_PAIR, HEAD_PAIR), lambda b, p, i: (b, p, tb(rev)(i), 0, 0))
    args, in_specs, out_specs = [], [], []
    for d, rev in enumerate(GLA_DIRS):
        args += [hq, hi, zs[d], lbls[d], saved[d], do]
        in_specs += [tok(rev), tok(rev), tok(rev), lspec, sspec(rev), tok(rev)]
        out_specs += [tok(rev)] * 3
    out_specs += [pl.BlockSpec((1, 2, HEAD_PAIR), lambda b, p, i: (b, 0, p))] * nd
    return pl.pallas_call(
        body, name="gla_bwd", grid=(nb, 4, nblk),
        out_shape=[jax.ShapeDtypeStruct((t, A_WIDTH), BF16)] * (3 * nd) + [jax.ShapeDtypeStruct((nb, 2, A_WIDTH), F32)] * nd,
        in_specs=in_specs, out_specs=out_specs,
        scratch_shapes=[pltpu.VMEM((HEAD_PAIR, HEAD_PAIR), F32)] * nd,
        compiler_params=_params(),
    )(*args)


def _head_mean_matrix():
    r = lax.broadcasted_iota(jnp.int32, (A_WIDTH, A_WIDTH), 0) // 64
    c = lax.broadcasted_iota(jnp.int32, (A_WIDTH, A_WIDTH), 1) // 64
    return jnp.where(r == c, 1.0 / 64.0, 0.0).astype(BF16)


def _gla_out(o_f, o_b, hg, g, mean_mat):
    o = o_f + o_b
    ms = _group_mean(o * o, mean_mat)
    return o * lax.rsqrt(ms + EPS) * g * (hg * _sigmoid(hg))


def _gla_combine_bwd(o_f, o_b, hg, g, dy, tm):
    t = o_f.shape[0]

    def body(of_ref, ob_ref, hg_ref, g_ref, dy_ref, do_ref, dhg_ref, dg_ref):
        mean_mat = _head_mean_matrix()
        fn = lambda o, hgv, gv: _gla_out(o, jnp.zeros_like(o), hgv, gv, mean_mat)
        _, vjp = jax.vjp(fn, of_ref[...].astype(F32) + ob_ref[...].astype(F32), hg_ref[...], g_ref[...])
        d_o, d_hg, d_g = vjp(dy_ref[...].astype(F32))
        do_ref[...] = d_o.astype(do_ref.dtype)
        dhg_ref[...] = d_hg.astype(dhg_ref.dtype)

        @pl.when(pl.program_id(0) == 0)
        def _():
            dg_ref[...] = jnp.zeros_like(dg_ref)

        dg_ref[...] += d_g

    tok = pl.BlockSpec((tm, A_WIDTH), lambda i: (i, 0))
    vec = pl.BlockSpec((1, A_WIDTH), lambda i: (0, 0))
    return pl.pallas_call(
        body, name="gla_combine_bwd", grid=(t // tm,),
        out_shape=[jax.ShapeDtypeStruct((t, A_WIDTH), BF16), jax.ShapeDtypeStruct((t, A_WIDTH), BF16),
                   jax.ShapeDtypeStruct((1, A_WIDTH), F32)],
        in_specs=[tok, tok, tok, _const_spec((1, A_WIDTH)), tok], out_specs=[tok, tok, vec], compiler_params=_params(),
    )(o_f, o_b, hg, g, dy)


def _post_fwd(x, o_f, o_b, hg, oattn, tgt, g_hgrn, g_mla, w_out, g2, w_gate, w_up, w_down, g_fin, tm):
    t = x.shape[0]

    def body(x_ref, of_ref, ob_ref, hg_ref, oa_ref, tgt_ref, gh_ref, gm_ref, wo_ref, g2_ref, wg_ref, wu_ref, wd_ref, gf_ref,
             x1_ref, x2_ref, ycat_ref, gate_ref, up_ref, loss_ref):
        part = jnp.zeros((1, 1), F32)
        mean_mat = _head_mean_matrix()
        for j in range(tm // min(tm, ROW_SUB)):
            r = pl.ds(j * min(tm, ROW_SUB), min(tm, ROW_SUB))
            ya = _gla_out(of_ref[r, :].astype(F32), ob_ref[r, :].astype(F32), hg_ref[r, :], gh_ref[...], mean_mat).astype(BF16)
            yb = _rms(oa_ref[r, :], gm_ref[...]).astype(BF16)
            ycat_ref[r, 0:A_WIDTH] = ya
            ycat_ref[r, A_WIDTH:] = yb
            x1 = x_ref[r, :] + _dot(ya, wo_ref[0:A_WIDTH, :]) + _dot(yb, wo_ref[A_WIDTH:, :])
            x1_ref[r, :] = x1
            h2 = _rms(x1, g2_ref[...]).astype(BF16)
            gate, up = _dot_nt(h2, wg_ref[...]), _dot_nt(h2, wu_ref[...])
            gate_ref[r, :] = gate.astype(BF16)
            up_ref[r, :] = up.astype(BF16)
            act = (gate * _sigmoid(gate) * up).astype(BF16)
            x2 = x1 + _dot(act, wd_ref[...])
            x2_ref[r, :] = x2
            err = _rms(x2, gf_ref[...]) - tgt_ref[r, :]
            part = part + 0.5 * jnp.sum(jnp.mean(err * err, axis=-1, keepdims=True), axis=0, keepdims=True)

        @pl.when(pl.program_id(0) == 0)
        def _():
            loss_ref[...] = jnp.zeros_like(loss_ref)

        loss_ref[...] += jnp.broadcast_to(part, loss_ref.shape)

    tok = lambda wd: pl.BlockSpec((tm, wd), lambda i: (i, 0))
    return pl.pallas_call(
        body, name="post_fwd", grid=(t // tm,),
        out_shape=[jax.ShapeDtypeStruct((t, D_MODEL), F32)] * 2 + [jax.ShapeDtypeStruct((t, D_MODEL), BF16)]
        + [jax.ShapeDtypeStruct((t, D_FF), BF16)] * 2 + [jax.ShapeDtypeStruct((1, 128), F32)],
        in_specs=[tok(D_MODEL), tok(A_WIDTH), tok(A_WIDTH), tok(A_WIDTH), tok(512), tok(D_MODEL), _const_spec((1, A_WIDTH)),
                  _const_spec((1, 512)), _const_spec((D_MODEL, D_MODEL)),
                  _const_spec((1, D_MODEL)), _const_spec((D_FF, D_MODEL)), _const_spec((D_FF, D_MODEL)),
                  _const_spec((D_FF, D_MODEL)), _const_spec((1, D_MODEL))],
        out_specs=[tok(D_MODEL), tok(D_MODEL), tok(D_MODEL), tok(D_FF), tok(D_FF), pl.BlockSpec((1, 128), lambda i: (0, 0))],
        compiler_params=_params(),
    )(x, o_f, o_b, hg, oattn, tgt, g_hgrn, g_mla, w_out, g2, w_gate, w_up, w_down, g_fin)


def _post_bwd(x1, x2, gate_b, up_b, oattn, tgt, g_mla, w_out, g2, w_gate, w_up, w_down, g_fin, tm):
    t = x1.shape[0]

    def body(x1_ref, x2_ref, gate_ref, up_ref, oa_ref, tgt_ref, gm_ref, wo_ref, g2_ref, wg_ref, wu_ref, wd_ref, gf_ref,
             dx1_ref, dya_ref, doa_ref, dx1b_ref, h2_ref, dgate_ref, dup_ref, act_ref, dx2b_ref,
             dgm_ref, dg2_ref, dgf_ref):
        x1, x2 = x1_ref[...], x2_ref[...]
        dy = (_rms(x2, gf_ref[...]) - tgt_ref[...]) * (1.0 / D_MODEL)
        dx2, dgf = _rms_bwd(x2, gf_ref[...], dy)
        dx2b = dx2.astype(BF16)
        dx2b_ref[...] = dx2b
        h2_ref[...] = _rms(x1, g2_ref[...]).astype(BF16)
        gate, up = gate_ref[...].astype(F32), up_ref[...].astype(F32)
        sg = _sigmoid(gate)
        sl = gate * sg
        act_ref[...] = (sl * up).astype(BF16)
        dact = _dot_nt(dx2b, wd_ref[...])
        dup = (dact * sl).astype(BF16)
        dgate = (dact * up * (sg * (1.0 + gate * (1.0 - sg)))).astype(BF16)
        dup_ref[...] = dup
        dgate_ref[...] = dgate
        dh2 = _dot(dgate, wg_ref[...]) + _dot(dup, wu_ref[...])
        dx1n, dg2 = _rms_bwd(x1, g2_ref[...], dh2)
        dx1 = dx2 + dx1n
        dx1_ref[...] = dx1
        dx1b = dx1.astype(BF16)
        dx1b_ref[...] = dx1b
        oa = oa_ref[...]
        dya_ref[...] = _dot_nt(dx1b, wo_ref[0:A_WIDTH, :]).astype(dya_ref.dtype)
        doa, dgm = _rms_bwd(oa, gm_ref[...], _dot_nt(dx1b, wo_ref[A_WIDTH:, :]))
        doa_ref[...] = doa.astype(doa_ref.dtype)

        @pl.when(pl.program_id(0) == 0)
        def _():
            dgm_ref[...] = jnp.zeros_like(dgm_ref)
            dg2_ref[...] = jnp.zeros_like(dg2_ref)
            dgf_ref[...] = jnp.zeros_like(dgf_ref)

        dgm_ref[...] += dgm
        dg2_ref[...] += dg2
        dgf_ref[...] += dgf

    tok = lambda wd: pl.BlockSpec((tm, wd), lambda i: (i, 0))
    vec = lambda wd: pl.BlockSpec((1, wd), lambda i: (0, 0))
    sds = lambda wd, dt: jax.ShapeDtypeStruct((t, wd), dt)
    return pl.pallas_call(
        body, name="post_bwd", grid=(t // tm,),
        out_shape=[sds(D_MODEL, F32), sds(512, BF16), sds(512, BF16), sds(D_MODEL, BF16), sds(D_MODEL, BF16),
                   sds(D_FF, BF16), sds(D_FF, BF16), sds(D_FF, BF16), sds(D_MODEL, BF16),
                   jax.ShapeDtypeStruct((1, 512), F32), jax.ShapeDtypeStruct((1, D_MODEL), F32), jax.ShapeDtypeStruct((1, D_MODEL), F32)],
        in_specs=[tok(D_MODEL), tok(D_MODEL), tok(D_FF), tok(D_FF), tok(512), tok(D_MODEL), _const_spec((1, 512)),
                  _const_spec((D_MODEL, D_MODEL)), _const_spec((1, D_MODEL)), _const_spec((D_FF, D_MODEL)),
                  _const_spec((D_FF, D_MODEL)), _const_spec((D_FF, D_MODEL)), _const_spec((1, D_MODEL))],
        out_specs=[tok(D_MODEL), tok(512), tok(512), tok(D_MODEL), tok(D_MODEL), tok(D_FF), tok(D_FF), tok(D_FF),
                   tok(D_MODEL), vec(512), vec(D_MODEL), vec(D_MODEL)],
        compiler_params=_params(),
    )(x1, x2, gate_b, up_b, oattn, tgt, g_mla, w_out, g2, w_gate, w_up, w_down, g_fin)


def _matmul_tn(a, b, tn, tt, tag, b_cols=None, k_out=None, exchange=()):
    t, k = a.shape
    c0, n = (0, b.shape[1]) if b_cols is None else b_cols
    k_out = k if k_out is None else k_out
    last = t // tt - 1
    ne = len(exchange)
    n_j = n // tn

    def body(a_ref, b_ref, *rest):
        o_ref, acc_ref = rest[ne], rest[2 * ne + 1]
        if ne:
            start, finish = _exchange_protocol(rest[:ne], rest[ne + 1:2 * ne + 1], [True] * ne, *rest[2 * ne + 2:])
            pl.when((pl.program_id(0) == 0) & (pl.program_id(1) == 0))(start)
        part = _dot_tn(a_ref[...], b_ref[...])

        @pl.when(pl.program_id(1) == 0)
        def _():
            acc_ref[...] = part

        @pl.when(pl.program_id(1) > 0)
        def _():
            acc_ref[...] += part

        @pl.when(pl.program_id(1) == last)
        def _():
            o_ref[...] = acc_ref[0:k_out, :].astype(o_ref.dtype)

        if ne:
            pl.when((pl.program_id(0) == n_j - 1) & (pl.program_id(1) == last))(finish)

    any_spec = pl.BlockSpec(memory_space=pl.ANY)
    out = pl.pallas_call(
        body, name="wgrad_" + tag, grid=(n_j, t // tt),
        out_shape=[jax.ShapeDtypeStruct((k_out, n), BF16)] + _slot_shapes(exchange, [True] * ne),
        in_specs=[pl.BlockSpec((tt, k), lambda j, i: (i, 0)), pl.BlockSpec((tt, tn), lambda j, i: (i, j + c0 // tn))]
        + [any_spec] * ne,
        out_specs=[pl.BlockSpec((k_out, tn), lambda j, i: (0, j))] + [any_spec] * ne,
        scratch_shapes=[pltpu.VMEM((k, tn), F32)] + (_comm_sems(ne) if ne else []),
        compiler_params=_params(),
    )(a, b, *exchange)
    return out if ne else out[0]


def _inproj_qkv_bwd(x, g1, w_in, dx1, pieces, cq, ckv, g_qa, g_kva, w_q, w_kv, tables, dq, dk, dv, seq, tm):
    t = x.shape[0]
    nblk = seq // tm
    last = t // tm - 1
    counts = [len(p) for p in pieces]
    flat = [a for p in pieces for a in p]
    n_flat = len(flat)
    offs = [sum(IN_WIDTHS[:j]) for j in range(len(IN_WIDTHS))]

    def body(x_ref, g_ref, w_ref, dx1_ref, cq_ref, ckv_ref, gq_ref, gk_ref, wq_ref, wkv_ref, c_ref, sa_ref, sb_ref,
             dq_ref, dk_ref, dv_ref, *refs):
        ins = refs[:n_flat]
        (dx_ref, h_ref, dp_ref, dwq_ref, dwkv_ref, dg_ref, dgq_ref, dgk_ref,
         cqn_ref, dqf_ref, ckn_ref, dkv_ref, accq_ref, acckv_ref) = refs[n_flat:]
        cos_t, sin_a, sin_b = c_ref[...], sa_ref[...], sb_ref[...]
        cqn_ref[...] = _rms(cq_ref[...], gq_ref[...]).astype(BF16)
        ckn_ref[...] = _rms(ckv_ref[...], gk_ref[...]).astype(BF16)
        dkr = jnp.zeros((tm, 128), F32)
        for hd in range(B_HEADS):
            lo = hd * QK_PAD
            dqf_ref[:, lo:lo + 128] = (dq_ref[:, lo:lo + 128].astype(F32) * ATTN_SCALE).astype(BF16)
            dq_rope = dq_ref[:, lo + 128:lo + 256].astype(F32) * ATTN_SCALE
            dqf_ref[:, lo + 128:lo + 256] = _rope_t(dq_rope, cos_t, sin_a, sin_b).astype(BF16)
            dkv_ref[:, lo:lo + 128] = dk_ref[:, lo:lo + 128].astype(BF16)
            dkv_ref[:, lo + 128:lo + 256] = dv_ref[:, hd * B_V:(hd + 1) * B_V].astype(BF16)
            dkr = dkr + dk_ref[:, lo + 128:lo + 256]
        dcq, dgq = _rms_bwd(cq_ref[...], gq_ref[...], _dot(dqf_ref[...], wq_ref[...]))
        dckv, dgk = _rms_bwd(ckv_ref[...], gk_ref[...], _dot_nt(dkv_ref[...], wkv_ref[...]))
        dp_ref[:, offs[5]:offs[6]] = dcq.astype(BF16)
        dp_ref[:, offs[6]:offs[7]] = dckv.astype(BF16)
        dp_ref[:, offs[7]:] = _rope_t(dkr, cos_t, sin_a, sin_b).astype(BF16)
        j = 0
        for g, cnt in enumerate(counts):
            acc = ins[j][...].astype(F32)
            for jj in range(1, cnt):
                acc = acc + ins[j + jj][...].astype(F32)
            dp_ref[:, offs[g]:offs[g] + IN_WIDTHS[g]] = acc.astype(BF16)
            j += cnt
        xv = x_ref[...]
        h_ref[...] = _rms(xv, g_ref[...]).astype(BF16)
        dxn, dg = _rms_bwd(xv, g_ref[...], _dot(dp_ref[...], w_ref[...]))
        dx_ref[...] = dx1_ref[...] + dxn

        @pl.when(pl.program_id(0) == 0)
        def _():
            dg_ref[...] = jnp.zeros_like(dg_ref)
            dgq_ref[...] = jnp.zeros_like(dgq_ref)
            dgk_ref[...] = jnp.zeros_like(dgk_ref)
            accq_ref[...] = jnp.zeros_like(accq_ref)
            acckv_ref[...] = jnp.zeros_like(acckv_ref)

        dg_ref[...] += dg
        dgq_ref[...] += dgq
        dgk_ref[...] += dgk
        accq_ref[...] += _dot_tn(dqf_ref[...], cqn_ref[...])
        acckv_ref[...] += _dot_tn(ckn_ref[...], dkv_ref[...])

        @pl.when(pl.program_id(0) == last)
        def _():
            dwq_ref[...] = accq_ref[...].astype(dwq_ref.dtype)
            dwkv_ref[...] = acckv_ref[...].astype(dwkv_ref.dtype)

    tok = lambda wd: pl.BlockSpec((tm, wd), lambda i: (i, 0))
    vec = lambda wd: pl.BlockSpec((1, wd), lambda i: (0, 0))
    whole = lambda r, c: pl.BlockSpec((r, c), lambda i: (0, 0))
    tab = pl.BlockSpec((tm, 128), lambda i: (i % nblk, 0))
    sds = lambda wd, dt: jax.ShapeDtypeStruct((t, wd), dt)
    return pl.pallas_call(
        body, name="inproj_qkv_bwd", grid=(t // tm,),
        out_shape=[sds(D_MODEL, F32), sds(D_MODEL, BF16), sds(D_IN_PAD, BF16),
                   jax.ShapeDtypeStruct((B_HEADS * QK_PAD, Q_LORA), BF16), jax.ShapeDtypeStruct((KV_LORA, 1024), BF16),
                   jax.ShapeDtypeStruct((1, D_MODEL), F32), jax.ShapeDtypeStruct((1, Q_LORA), F32),
                   jax.ShapeDtypeStruct((1, KV_LORA), F32)],
        in_specs=[tok(D_MODEL), _const_spec((1, D_MODEL)), _const_spec((D_IN_PAD, D_MODEL)), tok(D_MODEL), tok(Q_LORA),
                  tok(KV_LORA), _const_spec((1, Q_LORA)), _const_spec((1, KV_LORA)), _const_spec((1024, Q_LORA)),
                  _const_spec((KV_LORA, 1024)), tab, tab, tab, tok(1024), tok(1024), tok(512)] + [tok(512)] * n_flat,
        out_specs=[tok(D_MODEL), tok(D_MODEL), tok(D_IN_PAD), whole(B_HEADS * QK_PAD, Q_LORA), whole(KV_LORA, 1024),
                   vec(D_MODEL), vec(Q_LORA), vec(KV_LORA)],
        scratch_shapes=[pltpu.VMEM((tm, Q_LORA), BF16), pltpu.VMEM((tm, 1024), BF16), pltpu.VMEM((tm, KV_LORA), BF16),
                        pltpu.VMEM((tm, 1024), BF16), pltpu.VMEM((B_HEADS * QK_PAD, Q_LORA), F32),
                        pltpu.VMEM((KV_LORA, 1024), F32)],
        compiler_params=_params(),
    )(x, g1, w_in, dx1, cq, ckv, g_qa, g_kva, w_q, w_kv, *tables, dq, dk, dv, *flat)


def _cols_from_slots(g):
    n, r, cs = g.shape
    return g.transpose(1, 0, 2).reshape(r, n * cs)


def _cols_to_slots(full):
    r, c = full.shape
    return full.reshape(r, N_DEV, c // N_DEV).transpose(1, 0, 2)


def _arrange_w_in_t(w_in_t):
    return jnp.concatenate([w_in_t, jnp.zeros((D_IN_PAD - D_IN, D_MODEL), w_in_t.dtype)], axis=0)


def _arrange_w_q_t(w_q_t):
    q3 = w_q_t.reshape(B_HEADS, B_NOPE + B_ROPE, Q_LORA)
    pad = jnp.zeros((B_HEADS, QK_PAD - B_NOPE - B_ROPE, Q_LORA), w_q_t.dtype)
    return jnp.concatenate([q3, pad], axis=1).reshape(B_HEADS * QK_PAD, Q_LORA)


def _unarrange_w_q_t(d_q_t):
    return d_q_t.reshape(B_HEADS, QK_PAD, Q_LORA)[:, :B_NOPE + B_ROPE].reshape(B_HEADS * (B_NOPE + B_ROPE), Q_LORA)


def _step_core(x, loss_target, small_w, lb_full, early_full, late, seq, group, tiles, distributed):
    g1, g_hgrn, g_qa, g_kva, g_mla, g2, g_fin = small_w
    w_in, w_q, w_kv = _arrange_w_in_t(early_full[0]), _arrange_w_q_t(early_full[1]), early_full[2]
    nb = x.shape[0]
    t = nb * seq
    tm, tm_fwd, tq_f, tq_b, tt = tiles
    xt = x.reshape(t, D_MODEL)
    tgt = loss_target.reshape(t, D_MODEL)
    tables = _rope_tables(seq)

    hq, hi, zf, zb, hg, cq, ckv, qcat, kcat, vv = _inproj_qkv(xt, g1, w_in, g_qa, g_kva, w_q, w_kv, tables, seq, tm_fwd)
    if distributed:
        oattn, lse, *late_slots = _attn_fwd(qcat, kcat, vv, nb, seq, tq_f, gather=tuple(late))
    else:
        oattn, lse = _attn_fwd(qcat, kcat, vv, nb, seq, tq_f)
        late_slots = late
    w_out = late_slots[0].reshape(D_MODEL, D_MODEL)
    w_gate, w_up = late_slots[1].reshape(D_FF, D_MODEL), late_slots[2].reshape(D_FF, D_MODEL)
    w_down = late_slots[3].reshape(D_FF, D_MODEL)
    lbl_f, lbl_b = lb_full[0], lb_full[1]
    o_f, o_b, save_f, save_b = _gla_fwd(hq, hi, (zf, zb), (lbl_f, lbl_b), nb, seq, group)
    x1, x2, ycat_b, gate_b, up_b, loss_row = _post_fwd(
        xt, o_f, o_b, hg, oattn, tgt, g_hgrn, g_mla, w_out, g2, w_gate, w_up, w_down, g_fin, tm_fwd)

    (dx1, d_ya, d_oattn, dx1_b, h2_b, dgate_b, dup_b, act_b, dx2_b, d_g_mla, d_g2, d_g_fin) = _post_bwd(
        x1, x2, gate_b, up_b, oattn, tgt, g_mla, w_out, g2, w_gate, w_up, w_down, g_fin, tm)
    d_w_gate = _matmul_tn(dgate_b, h2_b, 512, tt, "gate")
    d_w_up = _matmul_tn(dup_b, h2_b, 512, tt, "up")
    d_w_down = _matmul_tn(act_b, dx2_b, 512, tt, "down")
    d_w_out = _matmul_tn(ycat_b, dx1_b, D_MODEL, tt, "out")
    late_g = [d_w_out.reshape(N_DEV, D_MODEL // N_DEV, D_MODEL)] + [
        g.reshape(N_DEV, D_FF // N_DEV, D_MODEL) for g in (d_w_gate, d_w_up, d_w_down)]
    if distributed:
        dq, dk, dv, *late_g = _attn_bwd(qcat, kcat, vv, oattn, lse, d_oattn, nb, seq, tq_b, exchange=tuple(late_g))
    else:
        dq, dk, dv = _attn_bwd(qcat, kcat, vv, oattn, lse, d_oattn, nb, seq, tq_b)
    d_o, d_hg, d_g_hgrn = _gla_combine_bwd(o_f, o_b, hg, g_hgrn, d_ya, tm_fwd)
    dq_f, dv_f, dz_f, dq_b, dv_b, dz_b, dl_f, dl_b = _gla_bwd(
        hq, hi, (zf, zb), (lbl_f, lbl_b), (save_f, save_b), d_o, nb, seq, group)
    grad_x, h1_b, dproj_b, d_w_q, d_w_kv, d_g1, d_g_qa, d_g_kva = _inproj_qkv_bwd(
        xt, g1, w_in, dx1, [[dq_f, dq_b], [dv_f, dv_b], [dz_f], [dz_b], [d_hg]], cq, ckv, g_qa, g_kva, w_q, w_kv, tables,
        dq, dk, dv, seq, tm_fwd)
    half = D_MODEL // 2
    in_slots = lambda g: g.reshape(N_DEV, D_IN // N_DEV, half)
    g_in_a = in_slots(_matmul_tn(dproj_b, h1_b, half, tt, "in_a", b_cols=(0, half), k_out=D_IN))
    if distributed:
        d_w_in_b, g_in_a = _matmul_tn(dproj_b, h1_b, half, tt, "in_b", b_cols=(half, half), k_out=D_IN, exchange=(g_in_a,))
    else:
        d_w_in_b = _matmul_tn(dproj_b, h1_b, half, tt, "in_b", b_cols=(half, half), k_out=D_IN)

    early_g = [in_slots(d_w_in_b), _unarrange_w_q_t(d_w_q).reshape(N_DEV, 768 // N_DEV, Q_LORA), _cols_to_slots(d_w_kv)]
    d_lb = jnp.stack([jnp.sum(dl_f, axis=0), jnp.sum(dl_b, axis=0)], axis=0)
    small_grads = [d_g1, d_g_hgrn, d_g_qa, d_g_kva, d_g_mla, d_g2, d_g_fin]
    return loss_row, grad_x.reshape(nb, seq, D_MODEL), g_in_a, early_g, late_g, small_grads, d_lb


def kernel(x, norm1_g, w_in, lb_logits, hgrn_norm_g, q_a_norm_g, w_q_b, kv_a_norm_g, w_kv_b, mla_norm_g, w_out, norm2_g, w_gate, w_up, w_down, final_norm_g, loss_target, m_norm1_g, m_w_in, m_lb_logits, m_hgrn_norm_g, m_q_a_norm_g, m_w_q_b, m_kv_a_norm_g, m_w_kv_b, m_mla_norm_g, m_w_out, m_norm2_g, m_w_gate, m_w_up, m_w_down, m_final_norm_g, v_norm1_g, v_w_in, v_lb_logits, v_hgrn_norm_g, v_q_a_norm_g, v_w_q_b, v_kv_a_norm_g, v_w_kv_b, v_mla_norm_g, v_w_out, v_norm2_g, v_w_gate, v_w_up, v_w_down, v_final_norm_g):
    big_w = [w_in, w_q_b, w_kv_b, w_out, w_gate, w_up, w_down]
    big_m = [m_w_in, m_w_q_b, m_w_kv_b, m_w_out, m_w_gate, m_w_up, m_w_down]
    big_v = [v_w_in, v_w_q_b, v_w_kv_b, v_w_out, v_w_gate, v_w_up, v_w_down]
    small_w = [norm1_g, hgrn_norm_g, q_a_norm_g, kv_a_norm_g, mla_norm_g, norm2_g, final_norm_g]
    small_m = [m_norm1_g, m_hgrn_norm_g, m_q_a_norm_g, m_kv_a_norm_g, m_mla_norm_g, m_norm2_g, m_final_norm_g]
    small_v = [v_norm1_g, v_hgrn_norm_g, v_q_a_norm_g, v_kv_a_norm_g, v_mla_norm_g, v_norm2_g, v_final_norm_g]
    seq = x.shape[1]
    my_id = 4 * lax.axis_index("x") + 2 * lax.axis_index("y") + lax.axis_index("c")

    shard = lambda w: w[0].astype(BF16)
    col_t = lambda w: jnp.swapaxes(w, 1, 2)[0]
    shard_t = lambda w: col_t(w).astype(BF16)
    g_in, g_q, g_kv, g_lb = _all_gather_call([shard_t(w_in), shard_t(w_q_b), shard(w_kv_b), lb_logits.reshape(4, 64)])
    early_full = (g_in.reshape(D_IN, D_MODEL), g_q.reshape(768, Q_LORA), _cols_from_slots(g_kv))
    lb_full = g_lb.reshape(N_DEV, 2, 2, 64).transpose(1, 2, 0, 3).reshape(2, 2, 512)

    as_row = lambda a: a.reshape(1, -1)
    loss_row, grad_x, recv_in_a, early_g, late_recv, small_g, d_lb = _step_core(
        x, loss_target, [as_row(s) for s in small_w], lb_full, early_full,
        [shard(w_out), shard_t(w_gate), shard_t(w_up), shard(w_down)], seq, min(16, seq // CHUNK),
        (256, 512, min(1024, seq), min(1024, seq), min(2048, 2 * seq)), True)

    grads, deltas, new_ms, new_vs = {}, {}, {}, {}
    views = {name: (col_t if name in ("w_in", "w_q_b", "w_gate", "w_up") else (lambda a: a[0])) for name, _, _, _ in BIG}
    backs = {name: ((lambda a: jnp.swapaxes(a[None], 1, 2)) if name in ("w_in", "w_q_b", "w_gate", "w_up") else (lambda a: a[None]))
             for name, _, _, _ in BIG}
    by_name = {name: (w, m, v) for (name, _, _, _), w, m, v in zip(BIG, big_w, big_m, big_v)}
    late_names = ["w_out", "w_gate", "w_up", "w_down"]
    n_small = len(small_g)
    g_l, d_l, nm_l, nv_l, recv = _adamw_recv_hosting(
        [views[n](by_name[n][0]) for n in late_names], list(late_recv), [views[n](by_name[n][1]) for n in late_names],
        [views[n](by_name[n][2]) for n in late_names],
        early_g + small_g + [d_lb.reshape(4, 512), loss_row], [True] * 3 + [False] * (n_small + 2))
    for i, name in enumerate(late_names):
        grads[name], deltas[name], new_ms[name], new_vs[name] = (backs[name](a[i]) for a in (g_l, d_l, nm_l, nv_l))
    sums = _sum_slots_call(recv[3:])
    g_small = [g.reshape(s.shape) for g, s in zip(sums[:n_small], small_w)]
    g_lb_own = lax.dynamic_index_in_dim(sums[n_small].reshape(2, 2, N_DEV, 64), my_id, axis=2, keepdims=False)
    loss = sums[n_small + 1][0, 0]

    for name, r in zip(["w_in", "w_q_b", "w_kv_b"], recv[:3]):
        w, m, v = (views[name](a) for a in by_name[name])
        g, d, nm, nv = _adamw_recv_halves(w, (recv_in_a, r), m, v, name) if name == "w_in" else _adamw_recv(w, r, m, v, name)
        grads[name], deltas[name], new_ms[name], new_vs[name] = (backs[name](a) for a in (g, d, nm, nv))
    lb_rows = lambda a: a.reshape(4, 64)
    d_s, nm_s, nv_s = _adamw_small(
        [as_row(a) for a in small_w] + [lb_rows(lb_logits)], [as_row(a) for a in g_small] + [lb_rows(g_lb_own)],
        [as_row(a) for a in small_m] + [lb_rows(m_lb_logits)], [as_row(a) for a in small_v] + [lb_rows(v_lb_logits)])
    for i, (s, (name, _)) in enumerate(zip(small_w + [lb_logits], SMALL + (("lb_logits", 0),))):
        grads[name] = (g_small + [g_lb_own])[i]
        deltas[name], new_ms[name], new_vs[name] = d_s[i].reshape(s.shape), nm_s[i].reshape(s.shape), nv_s[i].reshape(s.shape)

    order = ["norm1_g", "w_in", "lb_logits", "hgrn_norm_g", "q_a_norm_g", "w_q_b", "kv_a_norm_g", "w_kv_b", "mla_norm_g",
             "w_out", "norm2_g", "w_gate", "w_up", "w_down", "final_norm_g"]
    return (loss, grad_x, *[grads[n] for n in order], *[deltas[n] for n in order],
            *[new_ms[n] for n in order], *[new_vs[n] for n in order])
```

```python
import functools

import jax
import jax.numpy as jnp
from jax import lax
from jax.experimental import pallas as pl
from jax.experimental.pallas import tpu as pltpu

F32 = jnp.float32
BF16 = jnp.bfloat16

N_DEV = 8
D_MODEL = 1024
D_FF = 2816
A_WIDTH = 512
HEAD_PAIR = 128
CHUNK = 64
B_HEADS = 4
B_NOPE = 128
B_ROPE = 64
B_V = 128
QK_PAD = 256
Q_LORA = 384
KV_LORA = 256
D_IN = 3264
D_IN_PAD = 3328
IN_WIDTHS = (512, 512, 512, 512, 512, Q_LORA, KV_LORA, 128)
ROPE_THETA = 10000.0
EPS = 1e-6
ATTN_SCALE = (B_NOPE + B_ROPE) ** -0.5
ATTN_SUB = 256
ATTN_SUB_BWD = 256
ROW_SUB = 256
ADAM_LR, ADAM_B1, ADAM_B2, ADAM_EPS, ADAM_WD, ADAM_STEP = 0.001, 0.9, 0.999, 1e-08, 0.01, 10
VMEM_LIMIT = 60 * 1024 * 1024
MESH = pl.DeviceIdType.MESH

BIG = (("w_in", 1024, D_IN, 1), ("w_q_b", Q_LORA, 768, 1), ("w_kv_b", KV_LORA, 1024, 1), ("w_out", 1024, 1024, 0),
       ("w_gate", 1024, D_FF, 1), ("w_up", 1024, D_FF, 1), ("w_down", D_FF, 1024, 0))
SMALL = (("norm1_g", 1024), ("hgrn_norm_g", 512), ("q_a_norm_g", 384), ("kv_a_norm_g", 256), ("mla_norm_g", 512),
         ("norm2_g", 1024), ("final_norm_g", 1024))


def _params(**kw):
    return pltpu.CompilerParams(vmem_limit_bytes=VMEM_LIMIT, **kw)


def _const_spec(shape):
    return pl.BlockSpec(shape, lambda *_: (0,) * len(shape), pipeline_mode=pl.Buffered(1))


def _dot(a, b):
    return jnp.dot(a, b, preferred_element_type=F32)


def _dot_nt(a, b):
    return lax.dot_general(a, b, (((1,), (1,)), ((), ())), preferred_element_type=F32)


def _dot_tn(a, b):
    return lax.dot_general(a, b, (((0,), (0,)), ((), ())), preferred_element_type=F32)


@jax.custom_vjp
def _mm(a, b):
    return _dot(a.astype(BF16), b.astype(BF16))


def _mm_fwd(a, b):
    return _mm(a, b), (a, b)


def _mm_bwd(res, g):
    a, b = res
    gb = g.astype(BF16)
    return _dot_nt(gb, b.astype(BF16)), _dot_tn(a.astype(BF16), gb)


_mm.defvjp(_mm_fwd, _mm_bwd)


@jax.custom_vjp
def _mm_nt(a, b):
    return _dot_nt(a.astype(BF16), b.astype(BF16))


def _mm_nt_fwd(a, b):
    return _mm_nt(a, b), (a, b)


def _mm_nt_bwd(res, g):
    a, b = res
    gb = g.astype(BF16)
    return _dot(gb, b.astype(BF16)), _dot_tn(gb, a.astype(BF16))


_mm_nt.defvjp(_mm_nt_fwd, _mm_nt_bwd)


@jax.custom_vjp
def _mm_tn(a, b):
    return _dot_tn(a.astype(BF16), b.astype(BF16))


def _mm_tn_fwd(a, b):
    return _mm_tn(a, b), (a, b)


def _mm_tn_bwd(res, g):
    a, b = res
    gb = g.astype(BF16)
    return _dot_nt(b.astype(BF16), gb), _dot(a.astype(BF16), gb)


_mm_tn.defvjp(_mm_tn_fwd, _mm_tn_bwd)


def _dot_exact_rhs(a, m):
    hi = a.astype(BF16)
    lo = (a - hi.astype(F32)).astype(BF16)
    return _dot(hi, m) + _dot(lo, m)


@jax.custom_vjp
def _group_mean(a, m):
    return _dot_exact_rhs(a, m)


def _group_mean_fwd(a, m):
    return _group_mean(a, m), m


def _group_mean_bwd(m, g):
    return _dot_exact_rhs(g, m), jnp.zeros_like(m)


_group_mean.defvjp(_group_mean_fwd, _group_mean_bwd)


def _roll_rows(a, shift):
    return pltpu.roll(a, shift, 0)


def _cumsum_rows_raw(a, reverse):
    n = a.shape[0]
    row = lax.broadcasted_iota(jnp.int32, a.shape, 0)
    s = 1
    while s < n:
        if reverse:
            a = a + jnp.where(row < n - s, _roll_rows(a, n - s), 0.0)
        else:
            a = a + jnp.where(row >= s, _roll_rows(a, s), 0.0)
        s *= 2
    return a


@functools.partial(jax.custom_vjp, nondiff_argnums=(1,))
def _cumsum_rows(a, reverse):
    return _cumsum_rows_raw(a, reverse)


def _cumsum_rows_fwd(a, reverse):
    return _cumsum_rows_raw(a, reverse), None


def _cumsum_rows_bwd(reverse, _, g):
    return (_cumsum_rows_raw(g, not reverse),)


_cumsum_rows.defvjp(_cumsum_rows_fwd, _cumsum_rows_bwd)


def _rms(x, g):
    r = lax.rsqrt(jnp.mean(x * x, axis=-1, keepdims=True) + EPS)
    return x * r * g


def _rms_bwd(x, g, dy):
    r = lax.rsqrt(jnp.mean(x * x, axis=-1, keepdims=True) + EPS)
    xh = x * r
    dg = jnp.sum(dy * xh, axis=0, keepdims=True)
    dxh = dy * g
    dx = r * (dxh - xh * jnp.mean(dxh * xh, axis=-1, keepdims=True))
    return dx, dg


def _sigmoid(a):
    return jax.nn.sigmoid(a)


def _mesh_place():
    x, y, c = lax.axis_index("x"), lax.axis_index("y"), lax.axis_index("c")
    return x, y, c


def _dev_index(p):
    return 4 * p[0] + 2 * p[1] + p[2]


def _comm_sems(n):
    return [pltpu.SemaphoreType.DMA((n, 7)), pltpu.SemaphoreType.DMA((n, 7)), pltpu.SemaphoreType.DMA((n,))]


def _gather_protocol(ins, outs, send_sems, recv_sems, local_sems):
    n = len(ins)
    x, y, c = _mesh_place()
    me, sibling = (x, y, c), (x, y, 1 - c)
    chips = [(1 - x, y), (x, 1 - y), (1 - x, 1 - y)]

    def copy(a, k, block, to, src=None):
        slot = outs[a].at[_dev_index(block)]
        return pltpu.make_async_remote_copy(
            src_ref=slot if src is None else src, dst_ref=slot,
            send_sem=send_sems.at[a, k], recv_sem=recv_sems.at[a, k], device_id=to, device_id_type=MESH)

    def mine(a):
        return pltpu.make_async_copy(ins[a], outs[a].at[_dev_index(me)], local_sems.at[a])

    def first(a):
        return [copy(a, 0, me, sibling, src=ins[a])] + [copy(a, 1 + j, me, (*chip, c), src=ins[a]) for j, chip in enumerate(chips)]

    def start():
        for a in range(n):
            mine(a).start()
            for cp in first(a):
                cp.start()

    def forward():
        for a in range(n):
            for j, chip in enumerate(chips):
                copy(a, 1 + j, (*chip, c), me).wait_recv()
                copy(a, 4 + j, (*chip, c), sibling).start()

    def finish():
        for a in range(n):
            copy(a, 0, sibling, me).wait_recv()
            for j, chip in enumerate(chips):
                copy(a, 4 + j, (*chip, 1 - c), me).wait_recv()
        for a in range(n):
            mine(a).wait()
            for cp in first(a):
                cp.wait_send()
            for j, chip in enumerate(chips):
                copy(a, 4 + j, (*chip, c), sibling).wait_send()

    return start, forward, finish


def _exchange_protocol(ins, outs, scatter, send_sems, recv_sems, local_sems):
    n = len(ins)
    x, y, c = _mesh_place()
    me = (x, y, c)
    my_id = _dev_index(me)
    rels = [(dx, dy, dc) for dx in (0, 1) for dy in (0, 1) for dc in (0, 1)][1:]

    def peer_of(rel):
        return tuple(1 - v if d else v for v, d in zip(me, rel))

    def src(a, dev):
        return ins[a].at[dev] if scatter[a] else ins[a]

    def send(a, k):
        peer = peer_of(rels[k])
        return pltpu.make_async_remote_copy(
            src_ref=src(a, _dev_index(peer)), dst_ref=outs[a].at[my_id],
            send_sem=send_sems.at[a, k], recv_sem=recv_sems.at[a, k], device_id=peer, device_id_type=MESH)

    def arrival(a, k):
        peer = peer_of(rels[k])
        return pltpu.make_async_remote_copy(
            src_ref=src(a, my_id), dst_ref=outs[a].at[_dev_index(peer)],
            send_sem=send_sems.at[a, k], recv_sem=recv_sems.at[a, k], device_id=peer, device_id_type=MESH)

    def own(a):
        return pltpu.make_async_copy(src(a, my_id), outs[a].at[my_id], local_sems.at[a])

    def start():
        for a in range(n):
            own(a).start()
            for k in range(7):
                send(a, k).start()

    def finish():
        for a in range(n):
            for k in range(7):
                arrival(a, k).wait_recv()
        for a in range(n):
            for k in range(7):
                send(a, k).wait_send()
            own(a).wait()

    return start, finish


def _slot_shapes(blocks, scatter=None):
    return [jax.ShapeDtypeStruct(b.shape if (scatter and scatter[a]) else (N_DEV,) + b.shape, b.dtype) for a, b in enumerate(blocks)]


def _all_gather_call(blocks):
    n = len(blocks)

    def body(*refs):
        start, forward, finish = _gather_protocol(refs[:n], refs[n:2 * n], *refs[2 * n:])
        start()
        forward()
        finish()

    any_spec = pl.BlockSpec(memory_space=pl.ANY)
    return pl.pallas_call(
        body, name="weights_all_gather", out_shape=_slot_shapes(blocks),
        in_specs=[any_spec] * n, out_specs=[any_spec] * n, scratch_shapes=_comm_sems(n),
    )(*blocks)


def _sum_slots_call(recvs):
    n = len(recvs)

    def body(*refs):
        for in_ref, out_ref in zip(refs[:n], refs[n:]):
            acc = in_ref[0]
            for j in range(1, N_DEV):
                acc = acc + in_ref[j]
            out_ref[...] = acc

    return pl.pallas_call(
        body, name="small_grad_sum", out_shape=[jax.ShapeDtypeStruct(r.shape[1:], F32) for r in recvs],
        compiler_params=_params(),
    )(*recvs)


def _adam_update(w, g, m, v):
    nm = ADAM_B1 * m + (1.0 - ADAM_B1) * g
    nv = ADAM_B2 * v + (1.0 - ADAM_B2) * (g * g)
    bc1 = 1.0 - ADAM_B1 ** ADAM_STEP
    bc2 = 1.0 - ADAM_B2 ** ADAM_STEP
    return -ADAM_LR * ((nm / bc1) / (jnp.sqrt(nv / bc2) + ADAM_EPS) + ADAM_WD * w), nm, nv


def _adamw_recv(w, recv, m, v, tag):
    r, c = w.shape
    tr = r
    for cand in (512, 256, 128):
        if r > cand and r % cand == 0:
            tr = cand
            break

    def body(w_ref, r_ref, m_ref, v_ref, g_ref, d_ref, nm_ref, nv_ref):
        g = r_ref[0].astype(F32)
        for j in range(1, N_DEV):
            g = g + r_ref[j].astype(F32)
        g_ref[...] = g
        d_ref[...], nm_ref[...], nv_ref[...] = _adam_update(w_ref[...], g, m_ref[...], v_ref[...])

    spec = pl.BlockSpec((tr, c), lambda i: (i, 0))
    return pl.pallas_call(
        body, name="adamw_" + tag, out_shape=[jax.ShapeDtypeStruct(w.shape, F32)] * 4, grid=(r // tr,),
        in_specs=[spec, pl.BlockSpec((N_DEV, tr, c), lambda i: (0, i, 0)), spec, spec], out_specs=[spec] * 4,
        compiler_params=_params(),
    )(w, recv, m, v)


def _adamw_recv_halves(w, recv_halves, m, v, tag):
    r, c = w.shape
    half = c // 2

    def body(w_ref, ra_ref, rb_ref, m_ref, v_ref, g_ref, d_ref, nm_ref, nv_ref):
        def update(r_ref):
            g = r_ref[0].astype(F32)
            for j in range(1, N_DEV):
                g = g + r_ref[j].astype(F32)
            g_ref[...] = g
            d_ref[...], nm_ref[...], nv_ref[...] = _adam_update(w_ref[...], g, m_ref[...], v_ref[...])

        pl.when(pl.program_id(0) == 0)(lambda: update(ra_ref))
        pl.when(pl.program_id(0) == 1)(lambda: update(rb_ref))

    spec = pl.BlockSpec((r, half), lambda j: (0, j))
    whole = pl.BlockSpec((N_DEV, r, half), lambda j: (0, 0, 0))
    return pl.pallas_call(
        body, name="adamw_" + tag, out_shape=[jax.ShapeDtypeStruct(w.shape, F32)] * 4, grid=(2,),
        in_specs=[spec, whole, whole, spec, spec], out_specs=[spec] * 4, compiler_params=_params(),
    )(w, *recv_halves, m, v)


def _adamw_recv_hosting(ws, recvs, ms, vs, blocks, scatter):
    n, ne = len(ws), len(blocks)
    rows = max(w.shape[0] for w in ws)
    cols = ws[0].shape[1]
    assert all(w.shape[1] == cols for w in ws)

    def body(*refs):
        ins, ex_in = refs[:4 * n], refs[4 * n:4 * n + ne]
        outs, ex_out = refs[4 * n + ne:8 * n + ne], refs[8 * n + ne:8 * n + 2 * ne]
        in_buf, recv_buf, out_buf, in_sems, out_sems = refs[8 * n + 2 * ne:8 * n + 2 * ne + 5]
        start, finish = _exchange_protocol(ex_in, ex_out, scatter, *refs[8 * n + 2 * ne + 5:])
        start()
        for a in range(n):
            r = pl.ds(0, ws[a].shape[0])
            loads = [pltpu.make_async_copy(ins[k * n + a], in_buf.at[j, r], in_sems.at[j]) for j, k in enumerate((0, 2, 3))]
            loads.append(pltpu.make_async_copy(ins[n + a], recv_buf.at[:, r], in_sems.at[3]))
            for cp in loads:
                cp.start()
            for cp in loads:
                cp.wait()
            g = recv_buf[0, r].astype(F32)
            for j in range(1, N_DEV):
                g = g + recv_buf[j, r].astype(F32)
            out_buf[0, r] = g
            out_buf[1, r], out_buf[2, r], out_buf[3, r] = _adam_update(in_buf[0, r], g, in_buf[1, r], in_buf[2, r])
            stores = [pltpu.make_async_copy(out_buf.at[k, r], outs[k * n + a], out_sems.at[k]) for k in range(4)]
            for cp in stores:
                cp.start()
            for cp in stores:
                cp.wait()
        finish()

    any_spec = pl.BlockSpec(memory_space=pl.ANY)
    out = pl.pallas_call(
        body, name="adamw_late_and_grad_exchange",
        out_shape=[jax.ShapeDtypeStruct(w.shape, F32) for w in ws] * 4 + _slot_shapes(blocks, scatter),
        in_specs=[any_spec] * (4 * n + ne), out_specs=[any_spec] * (4 * n + ne),
        scratch_shapes=[pltpu.VMEM((3, rows, cols), F32), pltpu.VMEM((N_DEV, rows, cols), BF16), pltpu.VMEM((4, rows, cols), F32),
                        pltpu.SemaphoreType.DMA((4,)), pltpu.SemaphoreType.DMA((4,))] + _comm_sems(ne),
        compiler_params=_params(),
    )(*ws, *recvs, *ms, *vs, *blocks)
    return out[:n], out[n:2 * n], out[2 * n:3 * n], out[3 * n:4 * n], out[4 * n:]


def _adamw_small(ws, gs, ms, vs):
    n = len(ws)

    def body(*refs):
        ins, outs = refs[:4 * n], refs[4 * n:]
        for a in range(n):
            d, nm, nv = _adam_update(ins[a][...], ins[n + a][...], ins[2 * n + a][...], ins[3 * n + a][...])
            outs[a][...], outs[n + a][...], outs[2 * n + a][...] = d, nm, nv

    out = pl.pallas_call(
        body, name="adamw_small", out_shape=[jax.ShapeDtypeStruct(w.shape, F32) for w in ws] * 3, compiler_params=_params(),
    )(*ws, *gs, *ms, *vs)
    return out[:n], out[n:2 * n], out[2 * n:]


def _rope_tables(seq):
    inv = 1.0 / (ROPE_THETA ** (jnp.arange(0, B_ROPE, 2, dtype=F32) / B_ROPE))
    ang = jnp.arange(seq, dtype=F32)[:, None] * inv[None, :]
    cos, sin = jnp.cos(ang), jnp.sin(ang)
    z32, z64 = jnp.zeros_like(cos), jnp.zeros((seq, 64), F32)
    cos_t = jnp.concatenate([cos, cos, z64], axis=1)
    sin_a = jnp.concatenate([-sin, z32, z64], axis=1)
    sin_b = jnp.concatenate([z32, sin, z64], axis=1)
    return cos_t, sin_a, sin_b


def _rope(t, cos_t, sin_a, sin_b):
    return t * cos_t + pltpu.roll(t, 96, 1) * sin_a + pltpu.roll(t, 32, 1) * sin_b


def _rope_t(d, cos_t, sin_a, sin_b):
    return d * cos_t + pltpu.roll(d * sin_a, 32, 1) + pltpu.roll(d * sin_b, 96, 1)


def _inproj_qkv(x, g1, w_in, g_qa, g_kva, w_q, w_kv, tables, seq, tm):
    t = x.shape[0]
    nblk = seq // tm
    n_plain = 7
    offs = [sum(IN_WIDTHS[:j]) for j in range(len(IN_WIDTHS))]

    def body(x_ref, g_ref, w_ref, gq_ref, gk_ref, wq_ref, wkv_ref, c_ref, sa_ref, sb_ref, *outs):
        q_out, k_out, v_out = outs[n_plain:]
        for j in range(tm // min(tm, ROW_SUB)):
            r = pl.ds(j * min(tm, ROW_SUB), min(tm, ROW_SUB))
            h = _rms(x_ref[r, :], g_ref[...]).astype(BF16)
            proj = lambda g: _dot_nt(h, w_ref[offs[g]:offs[g] + IN_WIDTHS[g], :])
            for g in range(5):
                outs[g][r, :] = proj(g)
            cq, ckv, kr = proj(5), proj(6), proj(7)
            outs[5][r, :] = cq
            outs[6][r, :] = ckv
            cos_t, sin_a, sin_b = c_ref[r, :], sa_ref[r, :], sb_ref[r, :]
            cqn = _rms(cq, gq_ref[...]).astype(BF16)
            ckn = _rms(ckv, gk_ref[...]).astype(BF16)
            kr_rot = _rope(kr, cos_t, sin_a, sin_b).astype(BF16)
            for hd in range(B_HEADS):
                lo = hd * QK_PAD
                q_out[r, lo:lo + 128] = (_dot_nt(cqn, wq_ref[lo:lo + 128, :]) * ATTN_SCALE).astype(BF16)
                qr = _rope(_dot_nt(cqn, wq_ref[lo + 128:lo + 256, :]), cos_t, sin_a, sin_b)
                q_out[r, lo + 128:lo + 256] = (qr * ATTN_SCALE).astype(BF16)
                k_out[r, lo:lo + 128] = _dot(ckn, wkv_ref[:, lo:lo + 128]).astype(BF16)
                k_out[r, lo + 128:lo + 256] = kr_rot
                v_out[r, hd * B_V:(hd + 1) * B_V] = _dot(ckn, wkv_ref[:, lo + 128:lo + 256]).astype(BF16)

    tok = lambda wd: pl.BlockSpec((tm, wd), lambda i: (i, 0))
    tab = pl.BlockSpec((tm, 128), lambda i: (i % nblk, 0))
    widths = list(IN_WIDTHS[:n_plain]) + [B_HEADS * QK_PAD, B_HEADS * QK_PAD, B_HEADS * B_V]
    dtypes = [F32] * n_plain + [BF16] * 3
    return pl.pallas_call(
        body, name="inproj_qkv_fwd", grid=(t // tm,),
        out_shape=[jax.ShapeDtypeStruct((t, wd), dt) for wd, dt in zip(widths, dtypes)],
        in_specs=[tok(D_MODEL), _const_spec((1, D_MODEL)), _const_spec((D_IN_PAD, D_MODEL)), _const_spec((1, Q_LORA)),
                  _const_spec((1, KV_LORA)), _const_spec((B_HEADS * QK_PAD, Q_LORA)), _const_spec((KV_LORA, 1024)), tab, tab, tab],
        out_specs=[tok(wd) for wd in widths],
        compiler_params=_params(),
    )(x, g1, w_in, g_qa, g_kva, w_q, w_kv, *tables)


def _step_index(nq):
    return (pl.program_id(0) * B_HEADS + pl.program_id(1)) * nq + pl.program_id(2)


def _attn_fwd(qcat, kcat, v, nb, seq, tq, gather=()):
    t = qcat.shape[0]
    nq = seq // tq
    ng = len(gather)
    steps = nb * B_HEADS * nq

    def body(q_ref, k_ref, v_ref, *rest):
        o_ref, lse_ref = rest[ng:ng + 2]
        if ng:
            start, forward, finish = _gather_protocol(rest[:ng], rest[ng + 2:2 * ng + 2], *rest[2 * ng + 2:])
            pl.when(_step_index(nq) == 0)(start)
            pl.when(_step_index(nq) == (3 * steps) // 4)(forward)
        for j in range(tq // ATTN_SUB):
            r = pl.ds(j * ATTN_SUB, ATTN_SUB)
            s = _dot_nt(q_ref[r, :], k_ref[...])
            m = jnp.max(s, axis=-1, keepdims=True)
            p = jnp.exp(s - m)
            l = jnp.sum(p, axis=-1, keepdims=True)
            o_ref[r, :] = _dot(p.astype(BF16), v_ref[...]) / l
            lse_ref[0, r, :] = m + jnp.log(l)
        if ng:
            pl.when(_step_index(nq) == steps - 1)(finish)

    any_spec = pl.BlockSpec(memory_space=pl.ANY)
    return pl.pallas_call(
        body, name="attn_fwd", grid=(nb, B_HEADS, nq),
        out_shape=[jax.ShapeDtypeStruct((t, B_HEADS * B_V), F32), jax.ShapeDtypeStruct((B_HEADS, t, 1), F32)] + _slot_shapes(gather),
        in_specs=[pl.BlockSpec((tq, QK_PAD), lambda b, h, i: (b * nq + i, h)),
                  pl.BlockSpec((seq, QK_PAD), lambda b, h, i: (b, h)),
                  pl.BlockSpec((seq, B_V), lambda b, h, i: (b, h))] + [any_spec] * ng,
        out_specs=[pl.BlockSpec((tq, B_V), lambda b, h, i: (b * nq + i, h)),
                   pl.BlockSpec((1, tq, 1), lambda b, h, i: (h, b * nq + i, 0))] + [any_spec] * ng,
        scratch_shapes=_comm_sems(ng) if ng else [],
        compiler_params=_params(),
    )(qcat, kcat, v, *gather)


def _attn_bwd(qcat, kcat, v, o, lse, do, nb, seq, tq, exchange=()):
    t = qcat.shape[0]
    nq = seq // tq
    ne = len(exchange)
    steps = nb * B_HEADS * nq

    def body(q_ref, k_ref, v_ref, o_ref, lse_ref, do_ref, *rest):
        dq_ref, dk_ref, dv_ref = rest[ne:ne + 3]
        p_ref, ds_ref = rest[2 * ne + 3:2 * ne + 5]
        if ne:
            start, finish = _exchange_protocol(rest[:ne], rest[ne + 3:2 * ne + 3], [True] * ne, *rest[2 * ne + 5:])
            pl.when(_step_index(nq) == 0)(start)

        @pl.when(pl.program_id(2) == 0)
        def _():
            dv_ref[...] = jnp.zeros_like(dv_ref)
            dk_ref[...] = jnp.zeros_like(dk_ref)

        for j in range(tq // ATTN_SUB_BWD):
            r = pl.ds(j * ATTN_SUB_BWD, ATTN_SUB_BWD)
            q, k = q_ref[r, :], k_ref[...]
            do_f = do_ref[r, :].astype(F32)
            delta = jnp.sum(do_f * o_ref[r, :], axis=-1, keepdims=True)
            dob = do_f.astype(BF16)
            p = jnp.exp(_dot_nt(q, k) - lse_ref[0, r, :])
            ds = (p * (_dot_nt(dob, v_ref[...]) - delta)).astype(BF16)
            dq_ref[r, :] = _dot(ds, k).astype(dq_ref.dtype)
            p_ref[r, :] = p.astype(BF16)
            ds_ref[r, :] = ds
        dv_ref[...] += _dot_tn(p_ref[...], do_ref[...].astype(BF16))
        dk_ref[...] += _dot_tn(ds_ref[...], q_ref[...])
        if ne:
            pl.when(_step_index(nq) == steps - 1)(finish)

    qspec = lambda wd: pl.BlockSpec((tq, wd), lambda b, h, i: (b * nq + i, h))
    kspec = lambda wd: pl.BlockSpec((seq, wd), lambda b, h, i: (b, h))
    any_spec = pl.BlockSpec(memory_space=pl.ANY)
    return pl.pallas_call(
        body, name="attn_bwd", grid=(nb, B_HEADS, nq),
        out_shape=[jax.ShapeDtypeStruct((t, B_HEADS * QK_PAD), BF16), jax.ShapeDtypeStruct((t, B_HEADS * QK_PAD), F32),
                   jax.ShapeDtypeStruct((t, B_HEADS * B_V), F32)] + _slot_shapes(exchange, [True] * ne),
        in_specs=[qspec(QK_PAD), kspec(QK_PAD), kspec(B_V), qspec(B_V),
                  pl.BlockSpec((1, tq, 1), lambda b, h, i: (h, b * nq + i, 0)), qspec(B_V)] + [any_spec] * ne,
        out_specs=[qspec(QK_PAD), kspec(QK_PAD), kspec(B_V)] + [any_spec] * ne,
        scratch_shapes=[pltpu.VMEM((tq, seq), BF16), pltpu.VMEM((tq, seq), BF16)] + (_comm_sems(ne) if ne else []),
        compiler_params=_params(),
    )(qcat, kcat, v, o, lse, do, *exchange)


def _gla_consts(reverse):
    row = lax.broadcasted_iota(jnp.int32, (CHUNK, CHUNK), 0)
    col = lax.broadcasted_iota(jnp.int32, (CHUNK, CHUNK), 1)
    causal = (row <= col) if reverse else (row >= col)
    lane = lax.broadcasted_iota(jnp.int32, (1, HEAD_PAIR), 1)
    m0 = (lane < 64).astype(F32)
    m1 = 1.0 - m0
    r2 = lax.broadcasted_iota(jnp.int32, (HEAD_PAIR, HEAD_PAIR), 0)
    c2 = lax.broadcasted_iota(jnp.int32, (HEAD_PAIR, HEAD_PAIR), 1)
    same_head = ((r2 < 64) == (c2 < 64)).astype(F32)
    return causal, m0, m1, same_head


def _gla_chunk(hq, hi, z, l0, l1, st, consts, reverse):
    q_dec, k_inv, k_end, decay = _gla_gates(hq, z, l0, l1, reverse)
    o, st_new = _gla_state(q_dec, st, decay, _gla_increment(hi, k_end, consts))
    return o + _gla_intra(q_dec, k_inv, hi, consts), st_new


def _gla_gates(hq, z, l0, l1, reverse):
    mx = jnp.maximum(l0, l1)
    e0, e1 = jnp.exp(l0 - mx), jnp.exp(l1 - mx)
    lb = e0 / (e0 + e1)
    q = hq * _sigmoid(hq)
    sz = _sigmoid(z)
    log_f = jnp.log(lb + (1.0 - lb) * sz)
    k = (1.0 - lb) * (1.0 - sz)
    cum = _cumsum_rows(log_f, reverse)
    decay = jnp.exp(jnp.sum(log_f, axis=0, keepdims=True))
    k_inv = k * jnp.exp(-cum)
    return q * jnp.exp(cum), k_inv, k_inv * decay, decay


def _gla_intra(q_dec, k_inv, hi, consts):
    causal, m0, m1, _ = consts
    o = None
    for mh in (m0, m1):
        s = jnp.where(causal, _mm_nt(q_dec * mh, k_inv), 0.0)
        part = _mm(s, hi) * mh
        o = part if o is None else o + part
    return o


def _gla_increment(hi, k_end, consts):
    return _mm_tn(hi, k_end) * consts[3]


def _gla_state(q_dec, st, decay, inc):
    return _mm_nt(q_dec, st), st * decay + inc


GLA_DIRS = (False, True)
GLA_BATCH_FWD = 8
GLA_BATCH_BWD = 4


def _gla_fwd(hq, hi, zs, lbls, nb, seq, group):
    t = hq.shape[0]
    rows = group * CHUNK
    nblk = seq // rows
    n_chunks = seq // CHUNK
    nd = len(GLA_DIRS)

    def body(*refs):
        ins, outs, st_refs = refs[:4 * nd], refs[4 * nd:6 * nd], refs[6 * nd:]
        @pl.when(pl.program_id(2) == 0)
        def _():
            for st_ref in st_refs:
                st_ref[...] = jnp.zeros_like(st_ref)

        consts = [_gla_consts(rev) for rev in GLA_DIRS]
        work = [(d, rev, group - 1 - cc if rev else cc) for cc in range(group) for d, rev in enumerate(GLA_DIRS)]
        rows_of = lambda c: pl.ds(c * CHUNK, CHUNK)
        sts = [st_ref[...] for st_ref in st_refs]
        for w0 in range(0, len(work), GLA_BATCH_FWD):
            batch = work[w0:w0 + GLA_BATCH_FWD]
            gates, intra, incs = {}, {}, {}
            for d, rev, c in batch:
                hq_ref, _, z_ref, lbl_ref = ins[4 * d:4 * d + 4]
                gates[d, c] = _gla_gates(hq_ref[rows_of(c), :], z_ref[rows_of(c), :], lbl_ref[0:1, :], lbl_ref[1:2, :], rev)
            for d, rev, c in batch:
                hi_c = ins[4 * d + 1][rows_of(c), :]
                intra[d, c] = _gla_intra(gates[d, c][0], gates[d, c][1], hi_c, consts[d])
                incs[d, c] = _gla_increment(hi_c, gates[d, c][2], consts[d])
            for d, rev, c in batch:
                outs[nd + d][0, 0, c] = sts[d].astype(outs[nd + d].dtype)
                o_state, sts[d] = _gla_state(gates[d, c][0], sts[d], gates[d, c][3], incs[d, c])
                outs[d][rows_of(c), :] = (intra[d, c] + o_state).astype(outs[d].dtype)
        for st_ref, st in zip(st_refs, sts):
            st_ref[...] = st

    def tb(rev):
        return (lambda i: nblk - 1 - i) if rev else (lambda i: i)

    tok = lambda rev: pl.BlockSpec((rows, HEAD_PAIR), lambda b, p, i: (b * nblk + tb(rev)(i), p))
    lspec = pl.BlockSpec((2, HEAD_PAIR), lambda b, p, i: (0, p))
    sspec = lambda rev: pl.BlockSpec((1, 1, group, HEAD_PAIR, HEAD_PAIR), lambda b, p, i: (b, p, tb(rev)(i), 0, 0))
    args, in_specs = [], []
    for d, rev in enumerate(GLA_DIRS):
        args += [hq, hi, zs[d], lbls[d]]
        in_specs += [tok(rev), tok(rev), tok(rev), lspec]
    return pl.pallas_call(
        body, name="gla_fwd", grid=(nb, 4, nblk),
        out_shape=[jax.ShapeDtypeStruct((t, A_WIDTH), BF16)] * nd
        + [jax.ShapeDtypeStruct((nb, 4, n_chunks, HEAD_PAIR, HEAD_PAIR), BF16)] * nd,
        in_specs=in_specs, out_specs=[tok(rev) for rev in GLA_DIRS] + [sspec(rev) for rev in GLA_DIRS],
        scratch_shapes=[pltpu.VMEM((HEAD_PAIR, HEAD_PAIR), F32)] * nd,
        compiler_params=_params(),
    )(*args)


def _gla_bwd(hq, hi, zs, lbls, saved, do, nb, seq, group):
    t = hq.shape[0]
    rows = group * CHUNK
    nblk = seq // rows
    nd = len(GLA_DIRS)

    def body(*refs):
        ins, outs, dst_refs = refs[:6 * nd], refs[6 * nd:10 * nd], refs[10 * nd:]
        dl_refs = outs[3 * nd:]

        @pl.when(pl.program_id(2) == 0)
        def _():
            for dst_ref, dl_ref in zip(dst_refs, dl_refs):
                dst_ref[...] = jnp.zeros_like(dst_ref)
                dl_ref[...] = jnp.zeros_like(dl_ref)

        consts = [_gla_consts(rev) for rev in GLA_DIRS]
        dsts = [dst_ref[...] for dst_ref in dst_refs]
        dls = [[jnp.zeros((1, HEAD_PAIR), F32), jnp.zeros((1, HEAD_PAIR), F32)] for _ in GLA_DIRS]
        work = [(d, rev, cc if rev else group - 1 - cc) for cc in range(group) for d, rev in enumerate(GLA_DIRS)]
        for w0 in range(0, len(work), GLA_BATCH_BWD):
            vjps = {}
            for d, rev, c in work[w0:w0 + GLA_BATCH_BWD]:
                hq_ref, hi_ref, z_ref, lbl_ref, save_ref, _ = ins[6 * d:6 * d + 6]
                r = pl.ds(c * CHUNK, CHUNK)
                fn = functools.partial(_gla_chunk, consts=consts[d], reverse=rev)
                _, vjps[d, c] = jax.vjp(fn, hq_ref[r, :], hi_ref[r, :], z_ref[r, :], lbl_ref[0:1, :], lbl_ref[1:2, :],
                                         save_ref[0, 0, c].astype(F32))
            for d, rev, c in work[w0:w0 + GLA_BATCH_BWD]:
                dq_ref, dv_ref, dz_ref = outs[3 * d:3 * d + 3]
                r = pl.ds(c * CHUNK, CHUNK)
                d_hq, d_hi, d_z, d_l0, d_l1, dsts[d] = vjps[d, c]((ins[6 * d + 5][r, :].astype(F32), dsts[d]))
                dq_ref[r, :] = d_hq.astype(dq_ref.dtype)
                dv_ref[r, :] = d_hi.astype(dv_ref.dtype)
                dz_ref[r, :] = d_z.astype(dz_ref.dtype)
                dls[d] = [dls[d][0] + d_l0, dls[d][1] + d_l1]
        for d in range(nd):
            dst_refs[d][...] = dsts[d]
            dl_refs[d][0, 0:1, :] += dls[d][0]
            dl_refs[d][0, 1:2, :] += dls[d][1]

    def tb(rev):
        return (lambda i: i) if rev else (lambda i: nblk - 1 - i)

    tok = lambda rev: pl.BlockSpec((rows, HEAD_PAIR), lambda b, p, i: (b * nblk + tb(rev)(i), p))
    lspec = pl.BlockSpec((2, HEAD_PAIR), lambda b, p, i: (0, p))
    sspec = lambda rev: pl.BlockSpec((1, 1, group, HEAD_PAIR, HEAD_PAIR), lambda b, p, i: (b, p, tb(rev)(i), 0, 0))
    args, in_specs, out_specs = [], [], []
    for d, rev in enumerate(GLA_DIRS):
        args += [hq, hi, zs[d], lbls[d], saved[d], do]
        in_specs += [tok(rev), tok(rev), tok(rev), lspec, sspec(rev), tok(rev)]
        out_specs += [tok(rev)] * 3
    out_specs += [pl.BlockSpec((1, 2, HEAD_PAIR), lambda b, p, i: (b, 0, p))] * nd
    return pl.pallas_call(
        body, name="gla_bwd", grid=(nb, 4, nblk),
        out_shape=[jax.ShapeDtypeStruct((t, A_WIDTH), BF16)] * (3 * nd) + [jax.ShapeDtypeStruct((nb, 2, A_WIDTH), F32)] * nd,
        in_specs=in_specs, out_specs=out_specs,
        scratch_shapes=[pltpu.VMEM((HEAD_PAIR, HEAD_PAIR), F32)] * nd,
        compiler_params=_params(),
    )(*args)


def _head_mean_matrix():
    r = lax.broadcasted_iota(jnp.int32, (A_WIDTH, A_WIDTH), 0) // 64
    c = lax.broadcasted_iota(jnp.int32, (A_WIDTH, A_WIDTH), 1) // 64
    return jnp.where(r == c, 1.0 / 64.0, 0.0).astype(BF16)


def _gla_out(o_f, o_b, hg, g, mean_mat):
    o = o_f + o_b
    ms = _group_mean(o * o, mean_mat)
    return o * lax.rsqrt(ms + EPS) * g * (hg * _sigmoid(hg))


def _gla_combine_bwd(o_f, o_b, hg, g, dy, tm):
    t = o_f.shape[0]

    def body(of_ref, ob_ref, hg_ref, g_ref, dy_ref, do_ref, dhg_ref, dg_ref):
        mean_mat = _head_mean_matrix()
        fn = lambda o, hgv, gv: _gla_out(o, jnp.zeros_like(o), hgv, gv, mean_mat)
        _, vjp = jax.vjp(fn, of_ref[...].astype(F32) + ob_ref[...].astype(F32), hg_ref[...], g_ref[...])
        d_o, d_hg, d_g = vjp(dy_ref[...].astype(F32))
        do_ref[...] = d_o.astype(do_ref.dtype)
        dhg_ref[...] = d_hg.astype(dhg_ref.dtype)

        @pl.when(pl.program_id(0) == 0)
        def _():
            dg_ref[...] = jnp.zeros_like(dg_ref)

        dg_ref[...] += d_g

    tok = pl.BlockSpec((tm, A_WIDTH), lambda i: (i, 0))
    vec = pl.BlockSpec((1, A_WIDTH), lambda i: (0, 0))
    return pl.pallas_call(
        body, name="gla_combine_bwd", grid=(t // tm,),
        out_shape=[jax.ShapeDtypeStruct((t, A_WIDTH), BF16), jax.ShapeDtypeStruct((t, A_WIDTH), BF16),
                   jax.ShapeDtypeStruct((1, A_WIDTH), F32)],
        in_specs=[tok, tok, tok, _const_spec((1, A_WIDTH)), tok], out_specs=[tok, tok, vec], compiler_params=_params(),
    )(o_f, o_b, hg, g, dy)


def _post_fwd(x, o_f, o_b, hg, oattn, tgt, g_hgrn, g_mla, w_out, g2, w_gate, w_up, w_down, g_fin, tm):
    t = x.shape[0]

    def body(x_ref, of_ref, ob_ref, hg_ref, oa_ref, tgt_ref, gh_ref, gm_ref, wo_ref, g2_ref, wg_ref, wu_ref, wd_ref, gf_ref,
             x1_ref, x2_ref, ycat_ref, gate_ref, up_ref, loss_ref):
        part = jnp.zeros((1, 1), F32)
        mean_mat = _head_mean_matrix()
        for j in range(tm // min(tm, ROW_SUB)):
            r = pl.ds(j * min(tm, ROW_SUB), min(tm, ROW_SUB))
            ya = _gla_out(of_ref[r, :].astype(F32), ob_ref[r, :].astype(F32), hg_ref[r, :], gh_ref[...], mean_mat).astype(BF16)
            yb = _rms(oa_ref[r, :], gm_ref[...]).astype(BF16)
            ycat_ref[r, 0:A_WIDTH] = ya
            ycat_ref[r, A_WIDTH:] = yb
            x1 = x_ref[r, :] + _dot(ya, wo_ref[0:A_WIDTH, :]) + _dot(yb, wo_ref[A_WIDTH:, :])
            x1_ref[r, :] = x1
            h2 = _rms(x1, g2_ref[...]).astype(BF16)
            gate, up = _dot_nt(h2, wg_ref[...]), _dot_nt(h2, wu_ref[...])
            gate_ref[r, :] = gate.astype(BF16)
            up_ref[r, :] = up.astype(BF16)
            act = (gate * _sigmoid(gate) * up).astype(BF16)
            x2 = x1 + _dot(act, wd_ref[...])
            x2_ref[r, :] = x2
            err = _rms(x2, gf_ref[...]) - tgt_ref[r, :]
            part = part + 0.5 * jnp.sum(jnp.mean(err * err, axis=-1, keepdims=True), axis=0, keepdims=True)

        @pl.when(pl.program_id(0) == 0)
        def _():
            loss_ref[...] = jnp.zeros_like(loss_ref)

        loss_ref[...] += jnp.broadcast_to(part, loss_ref.shape)

    tok = lambda wd: pl.BlockSpec((tm, wd), lambda i: (i, 0))
    return pl.pallas_call(
        body, name="post_fwd", grid=(t // tm,),
        out_shape=[jax.ShapeDtypeStruct((t, D_MODEL), F32)] * 2 + [jax.ShapeDtypeStruct((t, D_MODEL), BF16)]
        + [jax.ShapeDtypeStruct((t, D_FF), BF16)] * 2 + [jax.ShapeDtypeStruct((1, 128), F32)],
        in_specs=[tok(D_MODEL), tok(A_WIDTH), tok(A_WIDTH), tok(A_WIDTH), tok(512), tok(D_MODEL), _const_spec((1, A_WIDTH)),
                  _const_spec((1, 512)), _const_spec((D_MODEL, D_MODEL)),
                  _const_spec((1, D_MODEL)), _const_spec((D_FF, D_MODEL)), _const_spec((D_FF, D_MODEL)),
                  _const_spec((D_FF, D_MODEL)), _const_spec((1, D_MODEL))],
        out_specs=[tok(D_MODEL), tok(D_MODEL), tok(D_MODEL), tok(D_FF), tok(D_FF), pl.BlockSpec((1, 128), lambda i: (0, 0))],
        compiler_params=_params(),
    )(x, o_f, o_b, hg, oattn, tgt, g_hgrn, g_mla, w_out, g2, w_gate, w_up, w_down, g_fin)


def _post_bwd(x1, x2, gate_b, up_b, oattn, tgt, g_mla, w_out, g2, w_gate, w_up, w_down, g_fin, tm):
    t = x1.shape[0]

    def body(x1_ref, x2_ref, gate_ref, up_ref, oa_ref, tgt_ref, gm_ref, wo_ref, g2_ref, wg_ref, wu_ref, wd_ref, gf_ref,
             dx1_ref, dya_ref, doa_ref, dx1b_ref, h2_ref, dgate_ref, dup_ref, act_ref, dx2b_ref,
             dgm_ref, dg2_ref, dgf_ref):
        x1, x2 = x1_ref[...], x2_ref[...]
        dy = (_rms(x2, gf_ref[...]) - tgt_ref[...]) * (1.0 / D_MODEL)
        dx2, dgf = _rms_bwd(x2, gf_ref[...], dy)
        dx2b = dx2.astype(BF16)
        dx2b_ref[...] = dx2b
        h2_ref[...] = _rms(x1, g2_ref[...]).astype(BF16)
        gate, up = gate_ref[...].astype(F32), up_ref[...].astype(F32)
        sg = _sigmoid(gate)
        sl = gate * sg
        act_ref[...] = (sl * up).astype(BF16)
        dact = _dot_nt(dx2b, wd_ref[...])
        dup = (dact * sl).astype(BF16)
        dgate = (dact * up * (sg * (1.0 + gate * (1.0 - sg)))).astype(BF16)
        dup_ref[...] = dup
        dgate_ref[...] = dgate
        dh2 = _dot(dgate, wg_ref[...]) + _dot(dup, wu_ref[...])
        dx1n, dg2 = _rms_bwd(x1, g2_ref[...], dh2)
        dx1 = dx2 + dx1n
        dx1_ref[...] = dx1
        dx1b = dx1.astype(BF16)
        dx1b_ref[...] = dx1b
        oa = oa_ref[...]
        dya_ref[...] = _dot_nt(dx1b, wo_ref[0:A_WIDTH, :]).astype(dya_ref.dtype)
        doa, dgm = _rms_bwd(oa, gm_ref[...], _dot_nt(dx1b, wo_ref[A_WIDTH:, :]))
        doa_ref[...] = doa.astype(doa_ref.dtype)

        @pl.when(pl.program_id(0) == 0)
        def _():
            dgm_ref[...] = jnp.zeros_like(dgm_ref)
            dg2_ref[...] = jnp.zeros_like(dg2_ref)
            dgf_ref[...] = jnp.zeros_like(dgf_ref)

        dgm_ref[...] += dgm
        dg2_ref[...] += dg2
        dgf_ref[...] += dgf

    tok = lambda wd: pl.BlockSpec((tm, wd), lambda i: (i, 0))
    vec = lambda wd: pl.BlockSpec((1, wd), lambda i: (0, 0))
    sds = lambda wd, dt: jax.ShapeDtypeStruct((t, wd), dt)
    return pl.pallas_call(
        body, name="post_bwd", grid=(t // tm,),
        out_shape=[sds(D_MODEL, F32), sds(512, BF16), sds(512, BF16), sds(D_MODEL, BF16), sds(D_MODEL, BF16),
                   sds(D_FF, BF16), sds(D_FF, BF16), sds(D_FF, BF16), sds(D_MODEL, BF16),
                   jax.ShapeDtypeStruct((1, 512), F32), jax.ShapeDtypeStruct((1, D_MODEL), F32), jax.ShapeDtypeStruct((1, D_MODEL), F32)],
        in_specs=[tok(D_MODEL), tok(D_MODEL), tok(D_FF), tok(D_FF), tok(512), tok(D_MODEL), _const_spec((1, 512)),
                  _const_spec((D_MODEL, D_MODEL)), _const_spec((1, D_MODEL)), _const_spec((D_FF, D_MODEL)),
                  _const_spec((D_FF, D_MODEL)), _const_spec((D_FF, D_MODEL)), _const_spec((1, D_MODEL))],
        out_specs=[tok(D_MODEL), tok(512), tok(512), tok(D_MODEL), tok(D_MODEL), tok(D_FF), tok(D_FF), tok(D_FF),
                   tok(D_MODEL), vec(512), vec(D_MODEL), vec(D_MODEL)],
        compiler_params=_params(),
    )(x1, x2, gate_b, up_b, oattn, tgt, g_mla, w_out, g2, w_gate, w_up, w_down, g_fin)


def _matmul_tn(a, b, tn, tt, tag, b_cols=None, k_out=None, exchange=()):
    t, k = a.shape
    c0, n = (0, b.shape[1]) if b_cols is None else b_cols
    k_out = k if k_out is None else k_out
    last = t // tt - 1
    ne = len(exchange)
    n_j = n // tn

    def body(a_ref, b_ref, *rest):
        o_ref, acc_ref = rest[ne], rest[2 * ne + 1]
        if ne:
            start, finish = _exchange_protocol(rest[:ne], rest[ne + 1:2 * ne + 1], [True] * ne, *rest[2 * ne + 2:])
            pl.when((pl.program_id(0) == 0) & (pl.program_id(1) == 0))(start)
        part = _dot_tn(a_ref[...], b_ref[...])

        @pl.when(pl.program_id(1) == 0)
        def _():
            acc_ref[...] = part

        @pl.when(pl.program_id(1) > 0)
        def _():
            acc_ref[...] += part

        @pl.when(pl.program_id(1) == last)
        def _():
            o_ref[...] = acc_ref[0:k_out, :].astype(o_ref.dtype)

        if ne:
            pl.when((pl.program_id(0) == n_j - 1) & (pl.program_id(1) == last))(finish)

    any_spec = pl.BlockSpec(memory_space=pl.ANY)
    out = pl.pallas_call(
        body, name="wgrad_" + tag, grid=(n_j, t // tt),
        out_shape=[jax.ShapeDtypeStruct((k_out, n), BF16)] + _slot_shapes(exchange, [True] * ne),
        in_specs=[pl.BlockSpec((tt, k), lambda j, i: (i, 0)), pl.BlockSpec((tt, tn), lambda j, i: (i, j + c0 // tn))]
        + [any_spec] * ne,
        out_specs=[pl.BlockSpec((k_out, tn), lambda j, i: (0, j))] + [any_spec] * ne,
        scratch_shapes=[pltpu.VMEM((k, tn), F32)] + (_comm_sems(ne) if ne else []),
        compiler_params=_params(),
    )(a, b, *exchange)
    return out if ne else out[0]


def _inproj_qkv_bwd(x, g1, w_in, dx1, pieces, cq, ckv, g_qa, g_kva, w_q, w_kv, tables, dq, dk, dv, seq, tm):
    t = x.shape[0]
    nblk = seq // tm
    last = t // tm - 1
    counts = [len(p) for p in pieces]
    flat = [a for p in pieces for a in p]
    n_flat = len(flat)
    offs = [sum(IN_WIDTHS[:j]) for j in range(len(IN_WIDTHS))]

    def body(x_ref, g_ref, w_ref, dx1_ref, cq_ref, ckv_ref, gq_ref, gk_ref, wq_ref, wkv_ref, c_ref, sa_ref, sb_ref,
             dq_ref, dk_ref, dv_ref, *refs):
        ins = refs[:n_flat]
        (dx_ref, h_ref, dp_ref, dwq_ref, dwkv_ref, dg_ref, dgq_ref, dgk_ref,
         cqn_ref, dqf_ref, ckn_ref, dkv_ref, accq_ref, acckv_ref) = refs[n_flat:]
        cos_t, sin_a, sin_b = c_ref[...], sa_ref[...], sb_ref[...]
        cqn_ref[...] = _rms(cq_ref[...], gq_ref[...]).astype(BF16)
        ckn_ref[...] = _rms(ckv_ref[...], gk_ref[...]).astype(BF16)
        dkr = jnp.zeros((tm, 128), F32)
        for hd in range(B_HEADS):
            lo = hd * QK_PAD
            dqf_ref[:, lo:lo + 128] = (dq_ref[:, lo:lo + 128].astype(F32) * ATTN_SCALE).astype(BF16)
            dq_rope = dq_ref[:, lo + 128:lo + 256].astype(F32) * ATTN_SCALE
            dqf_ref[:, lo + 128:lo + 256] = _rope_t(dq_rope, cos_t, sin_a, sin_b).astype(BF16)
            dkv_ref[:, lo:lo + 128] = dk_ref[:, lo:lo + 128].astype(BF16)
            dkv_ref[:, lo + 128:lo + 256] = dv_ref[:, hd * B_V:(hd + 1) * B_V].astype(BF16)
            dkr = dkr + dk_ref[:, lo + 128:lo + 256]
        dcq, dgq = _rms_bwd(cq_ref[...], gq_ref[...], _dot(dqf_ref[...], wq_ref[...]))
        dckv, dgk = _rms_bwd(ckv_ref[...], gk_ref[...], _dot_nt(dkv_ref[...], wkv_ref[...]))
        dp_ref[:, offs[5]:offs[6]] = dcq.astype(BF16)
        dp_ref[:, offs[6]:offs[7]] = dckv.astype(BF16)
        dp_ref[:, offs[7]:] = _rope_t(dkr, cos_t, sin_a, sin_b).astype(BF16)
        j = 0
        for g, cnt in enumerate(counts):
            acc = ins[j][...].astype(F32)
            for jj in range(1, cnt):
                acc = acc + ins[j + jj][...].astype(F32)
            dp_ref[:, offs[g]:offs[g] + IN_WIDTHS[g]] = acc.astype(BF16)
            j += cnt
        xv = x_ref[...]
        h_ref[...] = _rms(xv, g_ref[...]).astype(BF16)
        dxn, dg = _rms_bwd(xv, g_ref[...], _dot(dp_ref[...], w_ref[...]))
        dx_ref[...] = dx1_ref[...] + dxn

        @pl.when(pl.program_id(0) == 0)
        def _():
            dg_ref[...] = jnp.zeros_like(dg_ref)
            dgq_ref[...] = jnp.zeros_like(dgq_ref)
            dgk_ref[...] = jnp.zeros_like(dgk_ref)
            accq_ref[...] = jnp.zeros_like(accq_ref)
            acckv_ref[...] = jnp.zeros_like(acckv_ref)

        dg_ref[...] += dg
        dgq_ref[...] += dgq
        dgk_ref[...] += dgk
        accq_ref[...] += _dot_tn(dqf_ref[...], cqn_ref[...])
        acckv_ref[...] += _dot_tn(ckn_ref[...], dkv_ref[...])

        @pl.when(pl.program_id(0) == last)
        def _():
            dwq_ref[...] = accq_ref[...].astype(dwq_ref.dtype)
            dwkv_ref[...] = acckv_ref[...].astype(dwkv_ref.dtype)

    tok = lambda wd: pl.BlockSpec((tm, wd), lambda i: (i, 0))
    vec = lambda wd: pl.BlockSpec((1, wd), lambda i: (0, 0))
    whole = lambda r, c: pl.BlockSpec((r, c), lambda i: (0, 0))
    tab = pl.BlockSpec((tm, 128), lambda i: (i % nblk, 0))
    sds = lambda wd, dt: jax.ShapeDtypeStruct((t, wd), dt)
    return pl.pallas_call(
        body, name="inproj_qkv_bwd", grid=(t // tm,),
        out_shape=[sds(D_MODEL, F32), sds(D_MODEL, BF16), sds(D_IN_PAD, BF16),
                   jax.ShapeDtypeStruct((B_HEADS * QK_PAD, Q_LORA), BF16), jax.ShapeDtypeStruct((KV_LORA, 1024), BF16),
                   jax.ShapeDtypeStruct((1, D_MODEL), F32), jax.ShapeDtypeStruct((1, Q_LORA), F32),
                   jax.ShapeDtypeStruct((1, KV_LORA), F32)],
        in_specs=[tok(D_MODEL), _const_spec((1, D_MODEL)), _const_spec((D_IN_PAD, D_MODEL)), tok(D_MODEL), tok(Q_LORA),
                  tok(KV_LORA), _const_spec((1, Q_LORA)), _const_spec((1, KV_LORA)), _const_spec((1024, Q_LORA)),
                  _const_spec((KV_LORA, 1024)), tab, tab, tab, tok(1024), tok(1024), tok(512)] + [tok(512)] * n_flat,
        out_specs=[tok(D_MODEL), tok(D_MODEL), tok(D_IN_PAD), whole(B_HEADS * QK_PAD, Q_LORA), whole(KV_LORA, 1024),
                   vec(D_MODEL), vec(Q_LORA), vec(KV_LORA)],
        scratch_shapes=[pltpu.VMEM((tm, Q_LORA), BF16), pltpu.VMEM((tm, 1024), BF16), pltpu.VMEM((tm, KV_LORA), BF16),
                        pltpu.VMEM((tm, 1024), BF16), pltpu.VMEM((B_HEADS * QK_PAD, Q_LORA), F32),
                        pltpu.VMEM((KV_LORA, 1024), F32)],
        compiler_params=_params(),
    )(x, g1, w_in, dx1, cq, ckv, g_qa, g_kva, w_q, w_kv, *tables, dq, dk, dv, *flat)


def _cols_from_slots(g):
    n, r, cs = g.shape
    return g.transpose(1, 0, 2).reshape(r, n * cs)


def _cols_to_slots(full):
    r, c = full.shape
    return full.reshape(r, N_DEV, c // N_DEV).transpose(1, 0, 2)


def _arrange_w_in_t(w_in_t):
    return jnp.concatenate([w_in_t, jnp.zeros((D_IN_PAD - D_IN, D_MODEL), w_in_t.dtype)], axis=0)


def _arrange_w_q_t(w_q_t):
    q3 = w_q_t.reshape(B_HEADS, B_NOPE + B_ROPE, Q_LORA)
    pad = jnp.zeros((B_HEADS, QK_PAD - B_NOPE - B_ROPE, Q_LORA), w_q_t.dtype)
    return jnp.concatenate([q3, pad], axis=1).reshape(B_HEADS * QK_PAD, Q_LORA)


def _unarrange_w_q_t(d_q_t):
    return d_q_t.reshape(B_HEADS, QK_PAD, Q_LORA)[:, :B_NOPE + B_ROPE].reshape(B_HEADS * (B_NOPE + B_ROPE), Q_LORA)


def _step_core(x, loss_target, small_w, lb_full, early_full, late, seq, group, tiles, distributed):
    g1, g_hgrn, g_qa, g_kva, g_mla, g2, g_fin = small_w
    w_in, w_q, w_kv = _arrange_w_in_t(early_full[0]), _arrange_w_q_t(early_full[1]), early_full[2]
    nb = x.shape[0]
    t = nb * seq
    tm, tm_fwd, tq_f, tq_b, tt = tiles
    xt = x.reshape(t, D_MODEL)
    tgt = loss_target.reshape(t, D_MODEL)
    tables = _rope_tables(seq)

    hq, hi, zf, zb, hg, cq, ckv, qcat, kcat, vv = _inproj_qkv(xt, g1, w_in, g_qa, g_kva, w_q, w_kv, tables, seq, tm_fwd)
    if distributed:
        oattn, lse, *late_slots = _attn_fwd(qcat, kcat, vv, nb, seq, tq_f, gather=tuple(late))
    else:
        oattn, lse = _attn_fwd(qcat, kcat, vv, nb, seq, tq_f)
        late_slots = late
    w_out = late_slots[0].reshape(D_MODEL, D_MODEL)
    w_gate, w_up = late_slots[1].reshape(D_FF, D_MODEL), late_slots[2].reshape(D_FF, D_MODEL)
    w_down = late_slots[3].reshape(D_FF, D_MODEL)
    lbl_f, lbl_b = lb_full[0], lb_full[1]
    o_f, o_b, save_f, save_b = _gla_fwd(hq, hi, (zf, zb), (lbl_f, lbl_b), nb, seq, group)
    x1, x2, ycat_b, gate_b, up_b, loss_row = _post_fwd(
        xt, o_f, o_b, hg, oattn, tgt, g_hgrn, g_mla, w_out, g2, w_gate, w_up, w_down, g_fin, tm_fwd)

    (dx1, d_ya, d_oattn, dx1_b, h2_b, dgate_b, dup_b, act_b, dx2_b, d_g_mla, d_g2, d_g_fin) = _post_bwd(
        x1, x2, gate_b, up_b, oattn, tgt, g_mla, w_out, g2, w_gate, w_up, w_down, g_fin, tm)
    d_w_gate = _matmul_tn(dgate_b, h2_b, 512, tt, "gate")
    d_w_up = _matmul_tn(dup_b, h2_b, 512, tt, "up")
    d_w_down = _matmul_tn(act_b, dx2_b, 512, tt, "down")
    d_w_out = _matmul_tn(ycat_b, dx1_b, D_MODEL, tt, "out")
    late_g = [d_w_out.reshape(N_DEV, D_MODEL // N_DEV, D_MODEL)] + [
        g.reshape(N_DEV, D_FF // N_DEV, D_MODEL) for g in (d_w_gate, d_w_up, d_w_down)]
    if distributed:
        dq, dk, dv, *late_g = _attn_bwd(qcat, kcat, vv, oattn, lse, d_oattn, nb, seq, tq_b, exchange=tuple(late_g))
    else:
        dq, dk, dv = _attn_bwd(qcat, kcat, vv, oattn, lse, d_oattn, nb, seq, tq_b)
    d_o, d_hg, d_g_hgrn = _gla_combine_bwd(o_f, o_b, hg, g_hgrn, d_ya, tm_fwd)
    dq_f, dv_f, dz_f, dq_b, dv_b, dz_b, dl_f, dl_b = _gla_bwd(
        hq, hi, (zf, zb), (lbl_f, lbl_b), (save_f, save_b), d_o, nb, seq, group)
    grad_x, h1_b, dproj_b, d_w_q, d_w_kv, d_g1, d_g_qa, d_g_kva = _inproj_qkv_bwd(
        xt, g1, w_in, dx1, [[dq_f, dq_b], [dv_f, dv_b], [dz_f], [dz_b], [d_hg]], cq, ckv, g_qa, g_kva, w_q, w_kv, tables,
        dq, dk, dv, seq, tm_fwd)
    half = D_MODEL // 2
    in_slots = lambda g: g.reshape(N_DEV, D_IN // N_DEV, half)
    g_in_a = in_slots(_matmul_tn(dproj_b, h1_b, half, tt, "in_a", b_cols=(0, half), k_out=D_IN))
    if distributed:
        d_w_in_b, g_in_a = _matmul_tn(dproj_b, h1_b, half, tt, "in_b", b_cols=(half, half), k_out=D_IN, exchange=(g_in_a,))
    else:
        d_w_in_b = _matmul_tn(dproj_b, h1_b, half, tt, "in_b", b_cols=(half, half), k_out=D_IN)

    early_g = [in_slots(d_w_in_b), _unarrange_w_q_t(d_w_q).reshape(N_DEV, 768 // N_DEV, Q_LORA), _cols_to_slots(d_w_kv)]
    d_lb = jnp.stack([jnp.sum(dl_f, axis=0), jnp.sum(dl_b, axis=0)], axis=0)
    small_grads = [d_g1, d_g_hgrn, d_g_qa, d_g_kva, d_g_mla, d_g2, d_g_fin]
    return loss_row, grad_x.reshape(nb, seq, D_MODEL), g_in_a, early_g, late_g, small_grads, d_lb


def kernel(x, norm1_g, w_in, lb_logits, hgrn_norm_g, q_a_norm_g, w_q_b, kv_a_norm_g, w_kv_b, mla_norm_g, w_out, norm2_g, w_gate, w_up, w_down, final_norm_g, loss_target, m_norm1_g, m_w_in, m_lb_logits, m_hgrn_norm_g, m_q_a_norm_g, m_w_q_b, m_kv_a_norm_g, m_w_kv_b, m_mla_norm_g, m_w_out, m_norm2_g, m_w_gate, m_w_up, m_w_down, m_final_norm_g, v_norm1_g, v_w_in, v_lb_logits, v_hgrn_norm_g, v_q_a_norm_g, v_w_q_b, v_kv_a_norm_g, v_w_kv_b, v_mla_norm_g, v_w_out, v_norm2_g, v_w_gate, v_w_up, v_w_down, v_final_norm_g):
    big_w = [w_in, w_q_b, w_kv_b, w_out, w_gate, w_up, w_down]
    big_m = [m_w_in, m_w_q_b, m_w_kv_b, m_w_out, m_w_gate, m_w_up, m_w_down]
    big_v = [v_w_in, v_w_q_b, v_w_kv_b, v_w_out, v_w_gate, v_w_up, v_w_down]
    small_w = [norm1_g, hgrn_norm_g, q_a_norm_g, kv_a_norm_g, mla_norm_g, norm2_g, final_norm_g]
    small_m = [m_norm1_g, m_hgrn_norm_g, m_q_a_norm_g, m_kv_a_norm_g, m_mla_norm_g, m_norm2_g, m_final_norm_g]
    small_v = [v_norm1_g, v_hgrn_norm_g, v_q_a_norm_g, v_kv_a_norm_g, v_mla_norm_g, v_norm2_g, v_final_norm_g]
    seq = x.shape[1]
    my_id = 4 * lax.axis_index("x") + 2 * lax.axis_index("y") + lax.axis_index("c")

    shard = lambda w: w[0].astype(BF16)
    col_t = lambda w: jnp.swapaxes(w, 1, 2)[0]
    shard_t = lambda w: col_t(w).astype(BF16)
    g_in, g_q, g_kv, g_lb = _all_gather_call([shard_t(w_in), shard_t(w_q_b), shard(w_kv_b), lb_logits.reshape(4, 64)])
    early_full = (g_in.reshape(D_IN, D_MODEL), g_q.reshape(768, Q_LORA), _cols_from_slots(g_kv))
    lb_full = g_lb.reshape(N_DEV, 2, 2, 64).transpose(1, 2, 0, 3).reshape(2, 2, 512)

    as_row = lambda a: a.reshape(1, -1)
    loss_row, grad_x, recv_in_a, early_g, late_recv, small_g, d_lb = _step_core(
        x, loss_target, [as_row(s) for s in small_w], lb_full, early_full,
        [shard(w_out), shard_t(w_gate), shard_t(w_up), shard(w_down)], seq, min(16, seq // CHUNK),
        (256, 512, min(1024, seq), min(1024, seq), min(2048, 2 * seq)), True)

    grads, deltas, new_ms, new_vs = {}, {}, {}, {}
    views = {name: (col_t if name in ("w_in", "w_q_b", "w_gate", "w_up") else (lambda a: a[0])) for name, _, _, _ in BIG}
    backs = {name: ((lambda a: jnp.swapaxes(a[None], 1, 2)) if name in ("w_in", "w_q_b", "w_gate", "w_up") else (lambda a: a[None]))
             for name, _, _, _ in BIG}
    by_name = {name: (w, m, v) for (name, _, _, _), w, m, v in zip(BIG, big_w, big_m, big_v)}
    late_names = ["w_out", "w_gate", "w_up", "w_down"]
    n_small = len(small_g)
    g_l, d_l, nm_l, nv_l, recv = _adamw_recv_hosting(
        [views[n](by_name[n][0]) for n in late_names], list(late_recv), [views[n](by_name[n][1]) for n in late_names],
        [views[n](by_name[n][2]) for n in late_names],
        early_g + small_g + [d_lb.reshape(4, 512), loss_row], [True] * 3 + [False] * (n_small + 2))
    for i, name in enumerate(late_names):
        grads[name], deltas[name], new_ms[name], new_vs[name] = (backs[name](a[i]) for a in (g_l, d_l, nm_l, nv_l))
    sums = _sum_slots_call(recv[3:])
    g_small = [g.reshape(s.shape) for g, s in zip(sums[:n_small], small_w)]
    g_lb_own = lax.dynamic_index_in_dim(sums[n_small].reshape(2, 2, N_DEV, 64), my_id, axis=2, keepdims=False)
    loss = sums[n_small + 1][0, 0]

    for name, r in zip(["w_in", "w_q_b", "w_kv_b"], recv[:3]):
        w, m, v = (views[name](a) for a in by_name[name])
        g, d, nm, nv = _adamw_recv_halves(w, (recv_in_a, r), m, v, name) if name == "w_in" else _adamw_recv(w, r, m, v, name)
        grads[name], deltas[name], new_ms[name], new_vs[name] = (backs[name](a) for a in (g, d, nm, nv))
    lb_rows = lambda a: a.reshape(4, 64)
    d_s, nm_s, nv_s = _adamw_small(
        [as_row(a) for a in small_w] + [lb_rows(lb_logits)], [as_row(a) for a in g_small] + [lb_rows(g_lb_own)],
        [as_row(a) for a in small_m] + [lb_rows(m_lb_logits)], [as_row(a) for a in small_v] + [lb_rows(v_lb_logits)])
    for i, (s, (name, _)) in enumerate(zip(small_w + [lb_logits], SMALL + (("lb_logits", 0),))):
        grads[name] = (g_small + [g_lb_own])[i]
        deltas[name], new_ms[name], new_vs[name] = d_s[i].reshape(s.shape), nm_s[i].reshape(s.shape), nv_s[i].reshape(s.shape)

    order = ["norm1_g", "w_in", "lb_logits", "hgrn_norm_g", "q_a_norm_g", "w_q_b", "kv_a_norm_g", "w_kv_b", "mla_norm_g",
             "w_out", "norm2_g", "w_gate", "w_up", "w_down", "final_norm_g"]
    return (loss, grad_x, *[grads[n] for n in order], *[deltas[n] for n in order],
            *[new_ms[n] for n in order], *[new_vs[n] for n in order])
```

```python
import functools

import jax
import jax.numpy as jnp
from jax import lax
from jax.experimental import pallas as pl
from jax.experimental.pallas import tpu as pltpu

F32 = jnp.float32
BF16 = jnp.bfloat16

N_DEV = 8
D_MODEL = 1024
D_FF = 2816
A_WIDTH = 512
HEAD_PAIR = 128
CHUNK = 64
B_HEADS = 4
B_NOPE = 128
B_ROPE = 64
B_V = 128
QK_PAD = 256
Q_LORA = 384
KV_LORA = 256
D_IN = 3264
D_IN_PAD = 3328
IN_WIDTHS = (512, 512, 512, 512, 512, Q_LORA, KV_LORA, 128)
ROPE_THETA = 10000.0
EPS = 1e-6
ATTN_SCALE = (B_NOPE + B_ROPE) ** -0.5
ATTN_SUB = 256
ATTN_SUB_BWD = 256
ROW_SUB = 256
ADAM_LR, ADAM_B1, ADAM_B2, ADAM_EPS, ADAM_WD, ADAM_STEP = 0.001, 0.9, 0.999, 1e-08, 0.01, 10
VMEM_LIMIT = 60 * 1024 * 1024
MESH = pl.DeviceIdType.MESH

BIG = (("w_in", 1024, D_IN, 1), ("w_q_b", Q_LORA, 768, 1), ("w_kv_b", KV_LORA, 1024, 1), ("w_out", 1024, 1024, 0),
       ("w_gate", 1024, D_FF, 1), ("w_up", 1024, D_FF, 1), ("w_down", D_FF, 1024, 0))
SMALL = (("norm1_g", 1024), ("hgrn_norm_g", 512), ("q_a_norm_g", 384), ("kv_a_norm_g", 256), ("mla_norm_g", 512),
         ("norm2_g", 1024), ("final_norm_g", 1024))


def _params(**kw):
    return pltpu.CompilerParams(vmem_limit_bytes=VMEM_LIMIT, **kw)


def _const_spec(shape):
    return pl.BlockSpec(shape, lambda *_: (0,) * len(shape), pipeline_mode=pl.Buffered(1))


def _dot(a, b):
    return jnp.dot(a, b, preferred_element_type=F32)


def _dot_nt(a, b):
    return lax.dot_general(a, b, (((1,), (1,)), ((), ())), preferred_element_type=F32)


def _dot_tn(a, b):
    return lax.dot_general(a, b, (((0,), (0,)), ((), ())), preferred_element_type=F32)


@jax.custom_vjp
def _mm(a, b):
    return _dot(a.astype(BF16), b.astype(BF16))


def _mm_fwd(a, b):
    return _mm(a, b), (a, b)


def _mm_bwd(res, g):
    a, b = res
    gb = g.astype(BF16)
    return _dot_nt(gb, b.astype(BF16)), _dot_tn(a.astype(BF16), gb)


_mm.defvjp(_mm_fwd, _mm_bwd)


@jax.custom_vjp
def _mm_nt(a, b):
    return _dot_nt(a.astype(BF16), b.astype(BF16))


def _mm_nt_fwd(a, b):
    return _mm_nt(a, b), (a, b)


def _mm_nt_bwd(res, g):
    a, b = res
    gb = g.astype(BF16)
    return _dot(gb, b.astype(BF16)), _dot_tn(gb, a.astype(BF16))


_mm_nt.defvjp(_mm_nt_fwd, _mm_nt_bwd)


@jax.custom_vjp
def _mm_tn(a, b):
    return _dot_tn(a.astype(BF16), b.astype(BF16))


def _mm_tn_fwd(a, b):
    return _mm_tn(a, b), (a, b)


def _mm_tn_bwd(res, g):
    a, b = res
    gb = g.astype(BF16)
    return _dot_nt(b.astype(BF16), gb), _dot(a.astype(BF16), gb)


_mm_tn.defvjp(_mm_tn_fwd, _mm_tn_bwd)


def _dot_exact_rhs(a, m):
    hi = a.astype(BF16)
    lo = (a - hi.astype(F32)).astype(BF16)
    return _dot(hi, m) + _dot(lo, m)


@jax.custom_vjp
def _group_mean(a, m):
    return _dot_exact_rhs(a, m)


def _group_mean_fwd(a, m):
    return _group_mean(a, m), m


def _group_mean_bwd(m, g):
    return _dot_exact_rhs(g, m), jnp.zeros_like(m)


_group_mean.defvjp(_group_mean_fwd, _group_mean_bwd)


def _roll_rows(a, shift):
    return pltpu.roll(a, shift, 0)


def _cumsum_rows_raw(a, reverse):
    n = a.shape[0]
    row = lax.broadcasted_iota(jnp.int32, a.shape, 0)
    s = 1
    while s < n:
        if reverse:
            a = a + jnp.where(row < n - s, _roll_rows(a, n - s), 0.0)
        else:
            a = a + jnp.where(row >= s, _roll_rows(a, s), 0.0)
        s *= 2
    return a


@functools.partial(jax.custom_vjp, nondiff_argnums=(1,))
def _cumsum_rows(a, reverse):
    return _cumsum_rows_raw(a, reverse)


def _cumsum_rows_fwd(a, reverse):
    return _cumsum_rows_raw(a, reverse), None


def _cumsum_rows_bwd(reverse, _, g):
    return (_cumsum_rows_raw(g, not reverse),)


_cumsum_rows.defvjp(_cumsum_rows_fwd, _cumsum_rows_bwd)


def _rms(x, g):
    r = lax.rsqrt(jnp.mean(x * x, axis=-1, keepdims=True) + EPS)
    return x * r * g


def _rms_bwd(x, g, dy):
    r = lax.rsqrt(jnp.mean(x * x, axis=-1, keepdims=True) + EPS)
    xh = x * r
    dg = jnp.sum(dy * xh, axis=0, keepdims=True)
    dxh = dy * g
    dx = r * (dxh - xh * jnp.mean(dxh * xh, axis=-1, keepdims=True))
    return dx, dg


def _sigmoid(a):
    return jax.nn.sigmoid(a)


def _mesh_place():
    x, y, c = lax.axis_index("x"), lax.axis_index("y"), lax.axis_index("c")
    return x, y, c


def _dev_index(p):
    return 4 * p[0] + 2 * p[1] + p[2]


def _comm_sems(n):
    return [pltpu.SemaphoreType.DMA((n, 7)), pltpu.SemaphoreType.DMA((n, 7)), pltpu.SemaphoreType.DMA((n,))]


def _gather_protocol(ins, outs, send_sems, recv_sems, local_sems):
    n = len(ins)
    x, y, c = _mesh_place()
    me, sibling = (x, y, c), (x, y, 1 - c)
    chips = [(1 - x, y), (x, 1 - y), (1 - x, 1 - y)]

    def copy(a, k, block, to, src=None):
        slot = outs[a].at[_dev_index(block)]
        return pltpu.make_async_remote_copy(
            src_ref=slot if src is None else src, dst_ref=slot,
            send_sem=send_sems.at[a, k], recv_sem=recv_sems.at[a, k], device_id=to, device_id_type=MESH)

    def mine(a):
        return pltpu.make_async_copy(ins[a], outs[a].at[_dev_index(me)], local_sems.at[a])

    def first(a):
        return [copy(a, 0, me, sibling, src=ins[a])] + [copy(a, 1 + j, me, (*chip, c), src=ins[a]) for j, chip in enumerate(chips)]

    def start():
        for a in range(n):
            mine(a).start()
            for cp in first(a):
                cp.start()

    def forward():
        for a in range(n):
            for j, chip in enumerate(chips):
                copy(a, 1 + j, (*chip, c), me).wait_recv()
                copy(a, 4 + j, (*chip, c), sibling).start()

    def finish():
        for a in range(n):
            copy(a, 0, sibling, me).wait_recv()
            for j, chip in enumerate(chips):
                copy(a, 4 + j, (*chip, 1 - c), me).wait_recv()
        for a in range(n):
            mine(a).wait()
            for cp in first(a):
                cp.wait_send()
            for j, chip in enumerate(chips):
                copy(a, 4 + j, (*chip, c), sibling).wait_send()

    return start, forward, finish


def _exchange_protocol(ins, outs, scatter, send_sems, recv_sems, local_sems):
    n = len(ins)
    x, y, c = _mesh_place()
    me = (x, y, c)
    my_id = _dev_index(me)
    rels = [(dx, dy, dc) for dx in (0, 1) for dy in (0, 1) for dc in (0, 1)][1:]

    def peer_of(rel):
        return tuple(1 - v if d else v for v, d in zip(me, rel))

    def src(a, dev):
        return ins[a].at[dev] if scatter[a] else ins[a]

    def send(a, k):
        peer = peer_of(rels[k])
        return pltpu.make_async_remote_copy(
            src_ref=src(a, _dev_index(peer)), dst_ref=outs[a].at[my_id],
            send_sem=send_sems.at[a, k], recv_sem=recv_sems.at[a, k], device_id=peer, device_id_type=MESH)

    def arrival(a, k):
        peer = peer_of(rels[k])
        return pltpu.make_async_remote_copy(
            src_ref=src(a, my_id), dst_ref=outs[a].at[_dev_index(peer)],
            send_sem=send_sems.at[a, k], recv_sem=recv_sems.at[a, k], device_id=peer, device_id_type=MESH)

    def own(a):
        return pltpu.make_async_copy(src(a, my_id), outs[a].at[my_id], local_sems.at[a])

    def start():
        for a in range(n):
            own(a).start()
            for k in range(7):
                send(a, k).start()

    def finish():
        for a in range(n):
            for k in range(7):
                arrival(a, k).wait_recv()
        for a in range(n):
            for k in range(7):
                send(a, k).wait_send()
            own(a).wait()

    return start, finish


def _slot_shapes(blocks, scatter=None):
    return [jax.ShapeDtypeStruct(b.shape if (scatter and scatter[a]) else (N_DEV,) + b.shape, b.dtype) for a, b in enumerate(blocks)]


def _all_gather_call(blocks):
    n = len(blocks)

    def body(*refs):
        start, forward, finish = _gather_protocol(refs[:n], refs[n:2 * n], *refs[2 * n:])
        start()
        forward()
        finish()

    any_spec = pl.BlockSpec(memory_space=pl.ANY)
    return pl.pallas_call(
        body, name="weights_all_gather", out_shape=_slot_shapes(blocks),
        in_specs=[any_spec] * n, out_specs=[any_spec] * n, scratch_shapes=_comm_sems(n),
    )(*blocks)


def _sum_slots_call(recvs):
    n = len(recvs)

    def body(*refs):
        for in_ref, out_ref in zip(refs[:n], refs[n:]):
            acc = in_ref[0]
            for j in range(1, N_DEV):
                acc = acc + in_ref[j]
            out_ref[...] = acc

    return pl.pallas_call(
        body, name="small_grad_sum", out_shape=[jax.ShapeDtypeStruct(r.shape[1:], F32) for r in recvs],
        compiler_params=_params(),
    )(*recvs)


def _adam_update(w, g, m, v):
    nm = ADAM_B1 * m + (1.0 - ADAM_B1) * g
    nv = ADAM_B2 * v + (1.0 - ADAM_B2) * (g * g)
    bc1 = 1.0 - ADAM_B1 ** ADAM_STEP
    bc2 = 1.0 - ADAM_B2 ** ADAM_STEP
    return -ADAM_LR * ((nm / bc1) / (jnp.sqrt(nv / bc2) + ADAM_EPS) + ADAM_WD * w), nm, nv


def _adamw_recv(w, recv, m, v, tag):
    r, c = w.shape
    tr = r
    for cand in (512, 256, 128):
        if r > cand and r % cand == 0:
            tr = cand
            break

    def body(w_ref, r_ref, m_ref, v_ref, g_ref, d_ref, nm_ref, nv_ref):
        g = r_ref[0].astype(F32)
        for j in range(1, N_DEV):
            g = g + r_ref[j].astype(F32)
        g_ref[...] = g
        d_ref[...], nm_ref[...], nv_ref[...] = _adam_update(w_ref[...], g, m_ref[...], v_ref[...])

    spec = pl.BlockSpec((tr, c), lambda i: (i, 0))
    return pl.pallas_call(
        body, name="adamw_" + tag, out_shape=[jax.ShapeDtypeStruct(w.shape, F32)] * 4, grid=(r // tr,),
        in_specs=[spec, pl.BlockSpec((N_DEV, tr, c), lambda i: (0, i, 0)), spec, spec], out_specs=[spec] * 4,
        compiler_params=_params(),
    )(w, recv, m, v)


def _adamw_recv_halves(w, recv_halves, m, v, tag):
    r, c = w.shape
    half = c // 2

    def body(w_ref, ra_ref, rb_ref, m_ref, v_ref, g_ref, d_ref, nm_ref, nv_ref):
        def update(r_ref):
            g = r_ref[0].astype(F32)
            for j in range(1, N_DEV):
                g = g + r_ref[j].astype(F32)
            g_ref[...] = g
            d_ref[...], nm_ref[...], nv_ref[...] = _adam_update(w_ref[...], g, m_ref[...], v_ref[...])

        pl.when(pl.program_id(0) == 0)(lambda: update(ra_ref))
        pl.when(pl.program_id(0) == 1)(lambda: update(rb_ref))

    spec = pl.BlockSpec((r, half), lambda j: (0, j))
    whole = pl.BlockSpec((N_DEV, r, half), lambda j: (0, 0, 0))
    return pl.pallas_call(
        body, name="adamw_" + tag, out_shape=[jax.ShapeDtypeStruct(w.shape, F32)] * 4, grid=(2,),
        in_specs=[spec, whole, whole, spec, spec], out_specs=[spec] * 4, compiler_params=_params(),
    )(w, *recv_halves, m, v)


def _adamw_recv_hosting(ws, recvs, ms, vs, blocks, scatter):
    n, ne = len(ws), len(blocks)
    rows = max(w.shape[0] for w in ws)
    cols = ws[0].shape[1]
    assert all(w.shape[1] == cols for w in ws)

    def body(*refs):
        ins, ex_in = refs[:4 * n], refs[4 * n:4 * n + ne]
        outs, ex_out = refs[4 * n + ne:8 * n + ne], refs[8 * n + ne:8 * n + 2 * ne]
        in_buf, recv_buf, out_buf, in_sems, out_sems = refs[8 * n + 2 * ne:8 * n + 2 * ne + 5]
        start, finish = _exchange_protocol(ex_in, ex_out, scatter, *refs[8 * n + 2 * ne + 5:])
        start()
        for a in range(n):
            r = pl.ds(0, ws[a].shape[0])
            loads = [pltpu.make_async_copy(ins[k * n + a], in_buf.at[j, r], in_sems.at[j]) for j, k in enumerate((0, 2, 3))]
            loads.append(pltpu.make_async_copy(ins[n + a], recv_buf.at[:, r], in_sems.at[3]))
            for cp in loads:
                cp.start()
            for cp in loads:
                cp.wait()
            g = recv_buf[0, r].astype(F32)
            for j in range(1, N_DEV):
                g = g + recv_buf[j, r].astype(F32)
            out_buf[0, r] = g
            out_buf[1, r], out_buf[2, r], out_buf[3, r] = _adam_update(in_buf[0, r], g, in_buf[1, r], in_buf[2, r])
            stores = [pltpu.make_async_copy(out_buf.at[k, r], outs[k * n + a], out_sems.at[k]) for k in range(4)]
            for cp in stores:
                cp.start()
            for cp in stores:
                cp.wait()
        finish()

    any_spec = pl.BlockSpec(memory_space=pl.ANY)
    out = pl.pallas_call(
        body, name="adamw_late_and_grad_exchange",
        out_shape=[jax.ShapeDtypeStruct(w.shape, F32) for w in ws] * 4 + _slot_shapes(blocks, scatter),
        in_specs=[any_spec] * (4 * n + ne), out_specs=[any_spec] * (4 * n + ne),
        scratch_shapes=[pltpu.VMEM((3, rows, cols), F32), pltpu.VMEM((N_DEV, rows, cols), BF16), pltpu.VMEM((4, rows, cols), F32),
                        pltpu.SemaphoreType.DMA((4,)), pltpu.SemaphoreType.DMA((4,))] + _comm_sems(ne),
        compiler_params=_params(),
    )(*ws, *recvs, *ms, *vs, *blocks)
    return out[:n], out[n:2 * n], out[2 * n:3 * n], out[3 * n:4 * n], out[4 * n:]


def _adamw_small(ws, gs, ms, vs):
    n = len(ws)

    def body(*refs):
        ins, outs = refs[:4 * n], refs[4 * n:]
        for a in range(n):
            d, nm, nv = _adam_update(ins[a][...], ins[n + a][...], ins[2 * n + a][...], ins[3 * n + a][...])
            outs[a][...], outs[n + a][...], outs[2 * n + a][...] = d, nm, nv

    out = pl.pallas_call(
        body, name="adamw_small", out_shape=[jax.ShapeDtypeStruct(w.shape, F32) for w in ws] * 3, compiler_params=_params(),
    )(*ws, *gs, *ms, *vs)
    return out[:n], out[n:2 * n], out[2 * n:]


def _rope_tables(seq):
    inv = 1.0 / (ROPE_THETA ** (jnp.arange(0, B_ROPE, 2, dtype=F32) / B_ROPE))
    ang = jnp.arange(seq, dtype=F32)[:, None] * inv[None, :]
    cos, sin = jnp.cos(ang), jnp.sin(ang)
    z32, z64 = jnp.zeros_like(cos), jnp.zeros((seq, 64), F32)
    cos_t = jnp.concatenate([cos, cos, z64], axis=1)
    sin_a = jnp.concatenate([-sin, z32, z64], axis=1)
    sin_b = jnp.concatenate([z32, sin, z64], axis=1)
    return cos_t, sin_a, sin_b


def _rope(t, cos_t, sin_a, sin_b):
    return t * cos_t + pltpu.roll(t, 96, 1) * sin_a + pltpu.roll(t, 32, 1) * sin_b


def _rope_t(d, cos_t, sin_a, sin_b):
    return d * cos_t + pltpu.roll(d * sin_a, 32, 1) + pltpu.roll(d * sin_b, 96, 1)


def _inproj_qkv(x, g1, w_in, g_qa, g_kva, w_q, w_kv, tables, seq, tm):
    t = x.shape[0]
    nblk = seq // tm
    n_plain = 7
    offs = [sum(IN_WIDTHS[:j]) for j in range(len(IN_WIDTHS))]

    def body(x_ref, g_ref, w_ref, gq_ref, gk_ref, wq_ref, wkv_ref, c_ref, sa_ref, sb_ref, *outs):
        q_out, k_out, v_out = outs[n_plain:]
        for j in range(tm // min(tm, ROW_SUB)):
            r = pl.ds(j * min(tm, ROW_SUB), min(tm, ROW_SUB))
            h = _rms(x_ref[r, :], g_ref[...]).astype(BF16)
            proj = lambda g: _dot_nt(h, w_ref[offs[g]:offs[g] + IN_WIDTHS[g], :])
            for g in range(5):
                outs[g][r, :] = proj(g)
            cq, ckv, kr = proj(5), proj(6), proj(7)
            outs[5][r, :] = cq
            outs[6][r, :] = ckv
            cos_t, sin_a, sin_b = c_ref[r, :], sa_ref[r, :], sb_ref[r, :]
            cqn = _rms(cq, gq_ref[...]).astype(BF16)
            ckn = _rms(ckv, gk_ref[...]).astype(BF16)
            kr_rot = _rope(kr, cos_t, sin_a, sin_b).astype(BF16)
            for hd in range(B_HEADS):
                lo = hd * QK_PAD
                q_out[r, lo:lo + 128] = (_dot_nt(cqn, wq_ref[lo:lo + 128, :]) * ATTN_SCALE).astype(BF16)
                qr = _rope(_dot_nt(cqn, wq_ref[lo + 128:lo + 256, :]), cos_t, sin_a, sin_b)
                q_out[r, lo + 128:lo + 256] = (qr * ATTN_SCALE).astype(BF16)
                k_out[r, lo:lo + 128] = _dot(ckn, wkv_ref[:, lo:lo + 128]).astype(BF16)
                k_out[r, lo + 128:lo + 256] = kr_rot
                v_out[r, hd * B_V:(hd + 1) * B_V] = _dot(ckn, wkv_ref[:, lo + 128:lo + 256]).astype(BF16)

    tok = lambda wd: pl.BlockSpec((tm, wd), lambda i: (i, 0))
    tab = pl.BlockSpec((tm, 128), lambda i: (i % nblk, 0))
    widths = list(IN_WIDTHS[:n_plain]) + [B_HEADS * QK_PAD, B_HEADS * QK_PAD, B_HEADS * B_V]
    dtypes = [F32] * n_plain + [BF16] * 3
    return pl.pallas_call(
        body, name="inproj_qkv_fwd", grid=(t // tm,),
        out_shape=[jax.ShapeDtypeStruct((t, wd), dt) for wd, dt in zip(widths, dtypes)],
        in_specs=[tok(D_MODEL), _const_spec((1, D_MODEL)), _const_spec((D_IN_PAD, D_MODEL)), _const_spec((1, Q_LORA)),
                  _const_spec((1, KV_LORA)), _const_spec((B_HEADS * QK_PAD, Q_LORA)), _const_spec((KV_LORA, 1024)), tab, tab, tab],
        out_specs=[tok(wd) for wd in widths],
        compiler_params=_params(),
    )(x, g1, w_in, g_qa, g_kva, w_q, w_kv, *tables)


def _step_index(nq):
    return (pl.program_id(0) * B_HEADS + pl.program_id(1)) * nq + pl.program_id(2)


def _attn_fwd(qcat, kcat, v, nb, seq, tq, gather=()):
    t = qcat.shape[0]
    nq = seq // tq
    ng = len(gather)
    steps = nb * B_HEADS * nq

    def body(q_ref, k_ref, v_ref, *rest):
        o_ref, lse_ref = rest[ng:ng + 2]
        if ng:
            start, forward, finish = _gather_protocol(rest[:ng], rest[ng + 2:2 * ng + 2], *rest[2 * ng + 2:])
            pl.when(_step_index(nq) == 0)(start)
            pl.when(_step_index(nq) == (3 * steps) // 4)(forward)
        for j in range(tq // ATTN_SUB):
            r = pl.ds(j * ATTN_SUB, ATTN_SUB)
            s = _dot_nt(q_ref[r, :], k_ref[...])
            m = jnp.max(s, axis=-1, keepdims=True)
            p = jnp.exp(s - m)
            l = jnp.sum(p, axis=-1, keepdims=True)
            o_ref[r, :] = _dot(p.astype(BF16), v_ref[...]) / l
            lse_ref[0, r, :] = m + jnp.log(l)
        if ng:
            pl.when(_step_index(nq) == steps - 1)(finish)

    any_spec = pl.BlockSpec(memory_space=pl.ANY)
    return pl.pallas_call(
        body, name="attn_fwd", grid=(nb, B_HEADS, nq),
        out_shape=[jax.ShapeDtypeStruct((t, B_HEADS * B_V), F32), jax.ShapeDtypeStruct((B_HEADS, t, 1), F32)] + _slot_shapes(gather),
        in_specs=[pl.BlockSpec((tq, QK_PAD), lambda b, h, i: (b * nq + i, h)),
                  pl.BlockSpec((seq, QK_PAD), lambda b, h, i: (b, h)),
                  pl.BlockSpec((seq, B_V), lambda b, h, i: (b, h))] + [any_spec] * ng,
        out_specs=[pl.BlockSpec((tq, B_V), lambda b, h, i: (b * nq + i, h)),
                   pl.BlockSpec((1, tq, 1), lambda b, h, i: (h, b * nq + i, 0))] + [any_spec] * ng,
        scratch_shapes=_comm_sems(ng) if ng else [],
        compiler_params=_params(),
    )(qcat, kcat, v, *gather)


def _attn_bwd(qcat, kcat, v, o, lse, do, nb, seq, tq, exchange=()):
    t = qcat.shape[0]
    nq = seq // tq
    ne = len(exchange)
    steps = nb * B_HEADS * nq

    def body(q_ref, k_ref, v_ref, o_ref, lse_ref, do_ref, *rest):
        dq_ref, dk_ref, dv_ref = rest[ne:ne + 3]
        p_ref, ds_ref = rest[2 * ne + 3:2 * ne + 5]
        if ne:
            start, finish = _exchange_protocol(rest[:ne], rest[ne + 3:2 * ne + 3], [True] * ne, *rest[2 * ne + 5:])
            pl.when(_step_index(nq) == 0)(start)

        @pl.when(pl.program_id(2) == 0)
        def _():
            dv_ref[...] = jnp.zeros_like(dv_ref)
            dk_ref[...] = jnp.zeros_like(dk_ref)

        for j in range(tq // ATTN_SUB_BWD):
            r = pl.ds(j * ATTN_SUB_BWD, ATTN_SUB_BWD)
            q, k = q_ref[r, :], k_ref[...]
            do_f = do_ref[r, :].astype(F32)
            delta = jnp.sum(do_f * o_ref[r, :], axis=-1, keepdims=True)
            dob = do_f.astype(BF16)
            p = jnp.exp(_dot_nt(q, k) - lse_ref[0, r, :])
            ds = (p * (_dot_nt(dob, v_ref[...]) - delta)).astype(BF16)
            dq_ref[r, :] = _dot(ds, k).astype(dq_ref.dtype)
            p_ref[r, :] = p.astype(BF16)
            ds_ref[r, :] = ds
        dv_ref[...] += _dot_tn(p_ref[...], do_ref[...].astype(BF16))
        dk_ref[...] += _dot_tn(ds_ref[...], q_ref[...])
        if ne:
            pl.when(_step_index(nq) == steps - 1)(finish)

    qspec = lambda wd: pl.BlockSpec((tq, wd), lambda b, h, i: (b * nq + i, h))
    kspec = lambda wd: pl.BlockSpec((seq, wd), lambda b, h, i: (b, h))
    any_spec = pl.BlockSpec(memory_space=pl.ANY)
    return pl.pallas_call(
        body, name="attn_bwd", grid=(nb, B_HEADS, nq),
        out_shape=[jax.ShapeDtypeStruct((t, B_HEADS * QK_PAD), BF16), jax.ShapeDtypeStruct((t, B_HEADS * QK_PAD), F32),
                   jax.ShapeDtypeStruct((t, B_HEADS * B_V), F32)] + _slot_shapes(exchange, [True] * ne),
        in_specs=[qspec(QK_PAD), kspec(QK_PAD), kspec(B_V), qspec(B_V),
                  pl.BlockSpec((1, tq, 1), lambda b, h, i: (h, b * nq + i, 0)), qspec(B_V)] + [any_spec] * ne,
        out_specs=[qspec(QK_PAD), kspec(QK_PAD), kspec(B_V)] + [any_spec] * ne,
        scratch_shapes=[pltpu.VMEM((tq, seq), BF16), pltpu.VMEM((tq, seq), BF16)] + (_comm_sems(ne) if ne else []),
        compiler_params=_params(),
    )(qcat, kcat, v, o, lse, do, *exchange)


def _gla_consts(reverse):
    row = lax.broadcasted_iota(jnp.int32, (CHUNK, CHUNK), 0)
    col = lax.broadcasted_iota(jnp.int32, (CHUNK, CHUNK), 1)
    causal = (row <= col) if reverse else (row >= col)
    lane = lax.broadcasted_iota(jnp.int32, (1, HEAD_PAIR), 1)
    m0 = (lane < 64).astype(F32)
    m1 = 1.0 - m0
    r2 = lax.broadcasted_iota(jnp.int32, (HEAD_PAIR, HEAD_PAIR), 0)
    c2 = lax.broadcasted_iota(jnp.int32, (HEAD_PAIR, HEAD_PAIR), 1)
    same_head = ((r2 < 64) == (c2 < 64)).astype(F32)
    return causal, m0, m1, same_head


def _gla_chunk(hq, hi, z, l0, l1, st, consts, reverse):
    q_dec, k_inv, k_end, decay = _gla_gates(hq, z, l0, l1, reverse)
    o, st_new = _gla_state(q_dec, st, decay, _gla_increment(hi, k_end, consts))
    return o + _gla_intra(q_dec, k_inv, hi, consts), st_new


def _gla_gates(hq, z, l0, l1, reverse):
    mx = jnp.maximum(l0, l1)
    e0, e1 = jnp.exp(l0 - mx), jnp.exp(l1 - mx)
    lb = e0 / (e0 + e1)
    q = hq * _sigmoid(hq)
    sz = _sigmoid(z)
    log_f = jnp.log(lb + (1.0 - lb) * sz)
    k = (1.0 - lb) * (1.0 - sz)
    cum = _cumsum_rows(log_f, reverse)
    decay = jnp.exp(jnp.sum(log_f, axis=0, keepdims=True))
    k_inv = k * jnp.exp(-cum)
    return q * jnp.exp(cum), k_inv, k_inv * decay, decay


def _gla_intra(q_dec, k_inv, hi, consts):
    causal, m0, m1, _ = consts
    o = None
    for mh in (m0, m1):
        s = jnp.where(causal, _mm_nt(q_dec * mh, k_inv), 0.0)
        part = _mm(s, hi) * mh
        o = part if o is None else o + part
    return o


def _gla_increment(hi, k_end, consts):
    return _mm_tn(hi, k_end) * consts[3]


def _gla_state(q_dec, st, decay, inc):
    return _mm_nt(q_dec, st), st * decay + inc


GLA_DIRS = (False, True)
GLA_BATCH_FWD = 8
GLA_BATCH_BWD = 4


def _gla_fwd(hq, hi, zs, lbls, nb, seq, group):
    t = hq.shape[0]
    rows = group * CHUNK
    nblk = seq // rows
    n_chunks = seq // CHUNK
    nd = len(GLA_DIRS)

    def body(*refs):
        ins, outs, st_refs = refs[:4 * nd], refs[4 * nd:6 * nd], refs[6 * nd:]
        @pl.when(pl.program_id(2) == 0)
        def _():
            for st_ref in st_refs:
                st_ref[...] = jnp.zeros_like(st_ref)

        consts = [_gla_consts(rev) for rev in GLA_DIRS]
        work = [(d, rev, group - 1 - cc if rev else cc) for cc in range(group) for d, rev in enumerate(GLA_DIRS)]
        rows_of = lambda c: pl.ds(c * CHUNK, CHUNK)
        sts = [st_ref[...] for st_ref in st_refs]
        for w0 in range(0, len(work), GLA_BATCH_FWD):
            batch = work[w0:w0 + GLA_BATCH_FWD]
            gates, intra, incs = {}, {}, {}
            for d, rev, c in batch:
                hq_ref, _, z_ref, lbl_ref = ins[4 * d:4 * d + 4]
                gates[d, c] = _gla_gates(hq_ref[rows_of(c), :], z_ref[rows_of(c), :], lbl_ref[0:1, :], lbl_ref[1:2, :], rev)
            for d, rev, c in batch:
                hi_c = ins[4 * d + 1][rows_of(c), :]
                intra[d, c] = _gla_intra(gates[d, c][0], gates[d, c][1], hi_c, consts[d])
                incs[d, c] = _gla_increment(hi_c, gates[d, c][2], consts[d])
            for d, rev, c in batch:
                outs[nd + d][0, 0, c] = sts[d].astype(outs[nd + d].dtype)
                o_state, sts[d] = _gla_state(gates[d, c][0], sts[d], gates[d, c][3], incs[d, c])
                outs[d][rows_of(c), :] = (intra[d, c] + o_state).astype(outs[d].dtype)
        for st_ref, st in zip(st_refs, sts):
            st_ref[...] = st

    def tb(rev):
        return (lambda i: nblk - 1 - i) if rev else (lambda i: i)

    tok = lambda rev: pl.BlockSpec((rows, HEAD_PAIR), lambda b, p, i: (b * nblk + tb(rev)(i), p))
    lspec = pl.BlockSpec((2, HEAD_PAIR), lambda b, p, i: (0, p))
    sspec = lambda rev: pl.BlockSpec((1, 1, group, HEAD_PAIR, HEAD_PAIR), lambda b, p, i: (b, p, tb(rev)(i), 0, 0))
    args, in_specs = [], []
    for d, rev in enumerate(GLA_DIRS):
        args += [hq, hi, zs[d], lbls[d]]
        in_specs += [tok(rev), tok(rev), tok(rev), lspec]
    return pl.pallas_call(
        body, name="gla_fwd", grid=(nb, 4, nblk),
        out_shape=[jax.ShapeDtypeStruct((t, A_WIDTH), BF16)] * nd
        + [jax.ShapeDtypeStruct((nb, 4, n_chunks, HEAD_PAIR, HEAD_PAIR), BF16)] * nd,
        in_specs=in_specs, out_specs=[tok(rev) for rev in GLA_DIRS] + [sspec(rev) for rev in GLA_DIRS],
        scratch_shapes=[pltpu.VMEM((HEAD_PAIR, HEAD_PAIR), F32)] * nd,
        compiler_params=_params(),
    )(*args)


def _gla_bwd(hq, hi, zs, lbls, saved, do, nb, seq, group):
    t = hq.shape[0]
    rows = group * CHUNK
    nblk = seq // rows
    nd = len(GLA_DIRS)

    def body(*refs):
        ins, outs, dst_refs = refs[:6 * nd], refs[6 * nd:10 * nd], refs[10 * nd:]
        dl_refs = outs[3 * nd:]

        @pl.when(pl.program_id(2) == 0)
        def _():
            for dst_ref, dl_ref in zip(dst_refs, dl_refs):
                dst_ref[...] = jnp.zeros_like(dst_ref)
                dl_ref[...] = jnp.zeros_like(dl_ref)

        consts = [_gla_consts(rev) for rev in GLA_DIRS]
        dsts = [dst_ref[...] for dst_ref in dst_refs]
        dls = [[jnp.zeros((1, HEAD_PAIR), F32), jnp.zeros((1, HEAD_PAIR), F32)] for _ in GLA_DIRS]
        work = [(d, rev, cc if rev else group - 1 - cc) for cc in range(group) for d, rev in enumerate(GLA_DIRS)]
        for w0 in range(0, len(work), GLA_BATCH_BWD):
            vjps = {}
            for d, rev, c in work[w0:w0 + GLA_BATCH_BWD]:
                hq_ref, hi_ref, z_ref, lbl_ref, save_ref, _ = ins[6 * d:6 * d + 6]
                r = pl.ds(c * CHUNK, CHUNK)
                fn = functools.partial(_gla_chunk, consts=consts[d], reverse=rev)
                _, vjps[d, c] = jax.vjp(fn, hq_ref[r, :], hi_ref[r, :], z_ref[r, :], lbl_ref[0:1, :], lbl_ref[1:2, :],
                                         save_ref[0, 0, c].astype(F32))
            for d, rev, c in work[w0:w0 + GLA_BATCH_BWD]:
                dq_ref, dv_ref, dz_ref = outs[3 * d:3 * d + 3]
                r = pl.ds(c * CHUNK, CHUNK)
                d_hq, d_hi, d_z, d_l0, d_l1, dsts[d] = vjps[d, c]((ins[6 * d + 5][r, :].astype(F32), dsts[d]))
                dq_ref[r, :] = d_hq.astype(dq_ref.dtype)
                dv_ref[r, :] = d_hi.astype(dv_ref.dtype)
                dz_ref[r, :] = d_z.astype(dz_ref.dtype)
                dls[d] = [dls[d][0] + d_l0, dls[d][1] + d_l1]
        for d in range(nd):
            dst_refs[d][...] = dsts[d]
            dl_refs[d][0, 0:1, :] += dls[d][0]
            dl_refs[d][0, 1:2, :] += dls[d][1]

    def tb(rev):
        return (lambda i: i) if rev else (lambda i: nblk - 1 - i)

    tok = lambda rev: pl.BlockSpec((rows, HEAD_PAIR), lambda b, p, i: (b * nblk + tb(rev)(i), p))
    lspec = pl.BlockSpec((2, HEAD_PAIR), lambda b, p, i: (0, p))
    sspec = lambda rev: pl.BlockSpec((1, 1, group, HEAD_PAIR, HEAD_PAIR), lambda b, p, i: (b, p, tb(rev)(i), 0, 0))
    args, in_specs, out_specs = [], [], []
    for d, rev in enumerate(GLA_DIRS):
        args += [hq, hi, zs[d], lbls[d], saved[d], do]
        in_specs += [tok(rev), tok(rev), tok(rev), lspec, sspec(rev), tok(rev)]
        out_specs += [tok(rev)] * 3
    out_specs += [pl.BlockSpec((1, 2, HEAD_PAIR), lambda b, p, i: (b, 0, p))] * nd
    return pl.pallas_call(
        body, name="gla_bwd", grid=(nb, 4, nblk),
        out_shape=[jax.ShapeDtypeStruct((t, A_WIDTH), BF16)] * (3 * nd) + [jax.ShapeDtypeStruct((nb, 2, A_WIDTH), F32)] * nd,
        in_specs=in_specs, out_specs=out_specs,
        scratch_shapes=[pltpu.VMEM((HEAD_PAIR, HEAD_PAIR), F32)] * nd,
        compiler_params=_params(),
    )(*args)


def _head_mean_matrix():
    r = lax.broadcasted_iota(jnp.int32, (A_WIDTH, A_WIDTH), 0) // 64
    c = lax.broadcasted_iota(jnp.int32, (A_WIDTH, A_WIDTH), 1) // 64
    return jnp.where(r == c, 1.0 / 64.0, 0.0).astype(BF16)


def _gla_out(o_f, o_b, hg, g, mean_mat):
    o = o_f + o_b
    ms = _group_mean(o * o, mean_mat)
    return o * lax.rsqrt(ms + EPS) * g * (hg * _sigmoid(hg))


def _gla_combine_bwd(o_f, o_b, hg, g, dy, tm):
    t = o_f.shape[0]

    def body(of_ref, ob_ref, hg_ref, g_ref, dy_ref, do_ref, dhg_ref, dg_ref):
        mean_mat = _head_mean_matrix()
        fn = lambda o, hgv, gv: _gla_out(o, jnp.zeros_like(o), hgv, gv, mean_mat)
        _, vjp = jax.vjp(fn, of_ref[...].astype(F32) + ob_ref[...].astype(F32), hg_ref[...], g_ref[...])
        d_o, d_hg, d_g = vjp(dy_ref[...].astype(F32))
        do_ref[...] = d_o.astype(do_ref.dtype)
        dhg_ref[...] = d_hg.astype(dhg_ref.dtype)

        @pl.when(pl.program_id(0) == 0)
        def _():
            dg_ref[...] = jnp.zeros_like(dg_ref)

        dg_ref[...] += d_g

    tok = pl.BlockSpec((tm, A_WIDTH), lambda i: (i, 0))
    vec = pl.BlockSpec((1, A_WIDTH), lambda i: (0, 0))
    return pl.pallas_call(
        body, name="gla_combine_bwd", grid=(t // tm,),
        out_shape=[jax.ShapeDtypeStruct((t, A_WIDTH), BF16), jax.ShapeDtypeStruct((t, A_WIDTH), BF16),
                   jax.ShapeDtypeStruct((1, A_WIDTH), F32)],
        in_specs=[tok, tok, tok, _const_spec((1, A_WIDTH)), tok], out_specs=[tok, tok, vec], compiler_params=_params(),
    )(o_f, o_b, hg, g, dy)


def _post_fwd(x, o_f, o_b, hg, oattn, tgt, g_hgrn, g_mla, w_out, g2, w_gate, w_up, w_down, g_fin, tm):
    t = x.shape[0]

    def body(x_ref, of_ref, ob_ref, hg_ref, oa_ref, tgt_ref, gh_ref, gm_ref, wo_ref, g2_ref, wg_ref, wu_ref, wd_ref, gf_ref,
             x1_ref, x2_ref, ycat_ref, gate_ref, up_ref, loss_ref):
        part = jnp.zeros((1, 1), F32)
        mean_mat = _head_mean_matrix()
        for j in range(tm // min(tm, ROW_SUB)):
            r = pl.ds(j * min(tm, ROW_SUB), min(tm, ROW_SUB))
            ya = _gla_out(of_ref[r, :].astype(F32), ob_ref[r, :].astype(F32), hg_ref[r, :], gh_ref[...], mean_mat).astype(BF16)
            yb = _rms(oa_ref[r, :], gm_ref[...]).astype(BF16)
            ycat_ref[r, 0:A_WIDTH] = ya
            ycat_ref[r, A_WIDTH:] = yb
            x1 = x_ref[r, :] + _dot(ya, wo_ref[0:A_WIDTH, :]) + _dot(yb, wo_ref[A_WIDTH:, :])
            x1_ref[r, :] = x1
            h2 = _rms(x1, g2_ref[...]).astype(BF16)
            gate, up = _dot_nt(h2, wg_ref[...]), _dot_nt(h2, wu_ref[...])
            gate_ref[r, :] = gate.astype(BF16)
            up_ref[r, :] = up.astype(BF16)
            act = (gate * _sigmoid(gate) * up).astype(BF16)
            x2 = x1 + _dot(act, wd_ref[...])
            x2_ref[r, :] = x2
            err = _rms(x2, gf_ref[...]) - tgt_ref[r, :]
            part = part + 0.5 * jnp.sum(jnp.mean(err * err, axis=-1, keepdims=True), axis=0, keepdims=True)

        @pl.when(pl.program_id(0) == 0)
        def _():
            loss_ref[...] = jnp.zeros_like(loss_ref)

        loss_ref[...] += jnp.broadcast_to(part, loss_ref.shape)

    tok = lambda wd: pl.BlockSpec((tm, wd), lambda i: (i, 0))
    return pl.pallas_call(
        body, name="post_fwd", grid=(t // tm,),
        out_shape=[jax.ShapeDtypeStruct((t, D_MODEL), F32)] * 2 + [jax.ShapeDtypeStruct((t, D_MODEL), BF16)]
        + [jax.ShapeDtypeStruct((t, D_FF), BF16)] * 2 + [jax.ShapeDtypeStruct((1, 128), F32)],
        in_specs=[tok(D_MODEL), tok(A_WIDTH), tok(A_WIDTH), tok(A_WIDTH), tok(512), tok(D_MODEL), _const_spec((1, A_WIDTH)),
                  _const_spec((1, 512)), _const_spec((D_MODEL, D_MODEL)),
                  _const_spec((1, D_MODEL)), _const_spec((D_FF, D_MODEL)), _const_spec((D_FF, D_MODEL)),
                  _const_spec((D_FF, D_MODEL)), _const_spec((1, D_MODEL))],
        out_specs=[tok(D_MODEL), tok(D_MODEL), tok(D_MODEL), tok(D_FF), tok(D_FF), pl.BlockSpec((1, 128), lambda i: (0, 0))],
        compiler_params=_params(),
    )(x, o_f, o_b, hg, oattn, tgt, g_hgrn, g_mla, w_out, g2, w_gate, w_up, w_down, g_fin)


def _post_bwd(x1, x2, gate_b, up_b, oattn, tgt, g_mla, w_out, g2, w_gate, w_up, w_down, g_fin, tm):
    t = x1.shape[0]

    def body(x1_ref, x2_ref, gate_ref, up_ref, oa_ref, tgt_ref, gm_ref, wo_ref, g2_ref, wg_ref, wu_ref, wd_ref, gf_ref,
             dx1_ref, dya_ref, doa_ref, dx1b_ref, h2_ref, dgate_ref, dup_ref, act_ref, dx2b_ref,
             dgm_ref, dg2_ref, dgf_ref):
        x1, x2 = x1_ref[...], x2_ref[...]
        dy = (_rms(x2, gf_ref[...]) - tgt_ref[...]) * (1.0 / D_MODEL)
        dx2, dgf = _rms_bwd(x2, gf_ref[...], dy)
        dx2b = dx2.astype(BF16)
        dx2b_ref[...] = dx2b
        h2_ref[...] = _rms(x1, g2_ref[...]).astype(BF16)
        gate, up = gate_ref[...].astype(F32), up_ref[...].astype(F32)
        sg = _sigmoid(gate)
        sl = gate * sg
        act_ref[...] = (sl * up).astype(BF16)
        dact = _dot_nt(dx2b, wd_ref[...])
        dup = (dact * sl).astype(BF16)
        dgate = (dact * up * (sg * (1.0 + gate * (1.0 - sg)))).astype(BF16)
        dup_ref[...] = dup
        dgate_ref[...] = dgate
        dh2 = _dot(dgate, wg_ref[...]) + _dot(dup, wu_ref[...])
        dx1n, dg2 = _rms_bwd(x1, g2_ref[...], dh2)
        dx1 = dx2 + dx1n
        dx1_ref[...] = dx1
        dx1b = dx1.astype(BF16)
        dx1b_ref[...] = dx1b
        oa = oa_ref[...]
        dya_ref[...] = _dot_nt(dx1b, wo_ref[0:A_WIDTH, :]).astype(dya_ref.dtype)
        doa, dgm = _rms_bwd(oa, gm_ref[...], _dot_nt(dx1b, wo_ref[A_WIDTH:, :]))
        doa_ref[...] = doa.astype(doa_ref.dtype)

        @pl.when(pl.program_id(0) == 0)
        def _():
            dgm_ref[...] = jnp.zeros_like(dgm_ref)
            dg2_ref[...] = jnp.zeros_like(dg2_ref)
            dgf_ref[...] = jnp.zeros_like(dgf_ref)

        dgm_ref[...] += dgm
        dg2_ref[...] += dg2
        dgf_ref[...] += dgf

    tok = lambda wd: pl.BlockSpec((tm, wd), lambda i: (i, 0))
    vec = lambda wd: pl.BlockSpec((1, wd), lambda i: (0, 0))
    sds = lambda wd, dt: jax.ShapeDtypeStruct((t, wd), dt)
    return pl.pallas_call(
        body, name="post_bwd", grid=(t // tm,),
        out_shape=[sds(D_MODEL, F32), sds(512, BF16), sds(512, BF16), sds(D_MODEL, BF16), sds(D_MODEL, BF16),
                   sds(D_FF, BF16), sds(D_FF, BF16), sds(D_FF, BF16), sds(D_MODEL, BF16),
                   jax.ShapeDtypeStruct((1, 512), F32), jax.ShapeDtypeStruct((1, D_MODEL), F32), jax.ShapeDtypeStruct((1, D_MODEL), F32)],
        in_specs=[tok(D_MODEL), tok(D_MODEL), tok(D_FF), tok(D_FF), tok(512), tok(D_MODEL), _const_spec((1, 512)),
                  _const_spec((D_MODEL, D_MODEL)), _const_spec((1, D_MODEL)), _const_spec((D_FF, D_MODEL)),
                  _const_spec((D_FF, D_MODEL)), _const_spec((D_FF, D_MODEL)), _const_spec((1, D_MODEL))],
        out_specs=[tok(D_MODEL), tok(512), tok(512), tok(D_MODEL), tok(D_MODEL), tok(D_FF), tok(D_FF), tok(D_FF),
                   tok(D_MODEL), vec(512), vec(D_MODEL), vec(D_MODEL)],
        compiler_params=_params(),
    )(x1, x2, gate_b, up_b, oattn, tgt, g_mla, w_out, g2, w_gate, w_up, w_down, g_fin)


def _matmul_tn(a, b, tn, tt, tag, b_cols=None, k_out=None, exchange=()):
    t, k = a.shape
    c0, n = (0, b.shape[1]) if b_cols is None else b_cols
    k_out = k if k_out is None else k_out
    last = t // tt - 1
    ne = len(exchange)
    n_j = n // tn

    def body(a_ref, b_ref, *rest):
        o_ref, acc_ref = rest[ne], rest[2 * ne + 1]
        if ne:
            start, finish = _exchange_protocol(rest[:ne], rest[ne + 1:2 * ne + 1], [True] * ne, *rest[2 * ne + 2:])
            pl.when((pl.program_id(0) == 0) & (pl.program_id(1) == 0))(start)
        part = _dot_tn(a_ref[...], b_ref[...])

        @pl.when(pl.program_id(1) == 0)
        def _():
            acc_ref[...] = part

        @pl.when(pl.program_id(1) > 0)
        def _():
            acc_ref[...] += part

        @pl.when(pl.program_id(1) == last)
        def _():
            o_ref[...] = acc_ref[0:k_out, :].astype(o_ref.dtype)

        if ne:
            pl.when((pl.program_id(0) == n_j - 1) & (pl.program_id(1) == last))(finish)

    any_spec = pl.BlockSpec(memory_space=pl.ANY)
    out = pl.pallas_call(
        body, name="wgrad_" + tag, grid=(n_j, t // tt),
        out_shape=[jax.ShapeDtypeStruct((k_out, n), BF16)] + _slot_shapes(exchange, [True] * ne),
        in_specs=[pl.BlockSpec((tt, k), lambda j, i: (i, 0)), pl.BlockSpec((tt, tn), lambda j, i: (i, j + c0 // tn))]
        + [any_spec] * ne,
        out_specs=[pl.BlockSpec((k_out, tn), lambda j, i: (0, j))] + [any_spec] * ne,
        scratch_shapes=[pltpu.VMEM((k, tn), F32)] + (_comm_sems(ne) if ne else []),
        compiler_params=_params(),
    )(a, b, *exchange)
    return out if ne else out[0]


def _inproj_qkv_bwd(x, g1, w_in, dx1, pieces, cq, ckv, g_qa, g_kva, w_q, w_kv, tables, dq, dk, dv, seq, tm):
    t = x.shape[0]
    nblk = seq // tm
    last = t // tm - 1
    counts = [len(p) for p in pieces]
    flat = [a for p in pieces for a in p]
    n_flat = len(flat)
    offs = [sum(IN_WIDTHS[:j]) for j in range(len(IN_WIDTHS))]

    def body(x_ref, g_ref, w_ref, dx1_ref, cq_ref, ckv_ref, gq_ref, gk_ref, wq_ref, wkv_ref, c_ref, sa_ref, sb_ref,
             dq_ref, dk_ref, dv_ref, *refs):
        ins = refs[:n_flat]
        (dx_ref, h_ref, dp_ref, dwq_ref, dwkv_ref, dg_ref, dgq_ref, dgk_ref,
         cqn_ref, dqf_ref, ckn_ref, dkv_ref, accq_ref, acckv_ref) = refs[n_flat:]
        cos_t, sin_a, sin_b = c_ref[...], sa_ref[...], sb_ref[...]
        cqn_ref[...] = _rms(cq_ref[...], gq_ref[...]).astype(BF16)
        ckn_ref[...] = _rms(ckv_ref[...], gk_ref[...]).astype(BF16)
        dkr = jnp.zeros((tm, 128), F32)
        for hd in range(B_HEADS):
            lo = hd * QK_PAD
            dqf_ref[:, lo:lo + 128] = (dq_ref[:, lo:lo + 128].astype(F32) * ATTN_SCALE).astype(BF16)
            dq_rope = dq_ref[:, lo + 128:lo + 256].astype(F32) * ATTN_SCALE
            dqf_ref[:, lo + 128:lo + 256] = _rope_t(dq_rope, cos_t, sin_a, sin_b).astype(BF16)
            dkv_ref[:, lo:lo + 128] = dk_ref[:, lo:lo + 128].astype(BF16)
            dkv_ref[:, lo + 128:lo + 256] = dv_ref[:, hd * B_V:(hd + 1) * B_V].astype(BF16)
            dkr = dkr + dk_ref[:, lo + 128:lo + 256]
        dcq, dgq = _rms_bwd(cq_ref[...], gq_ref[...], _dot(dqf_ref[...], wq_ref[...]))
        dckv, dgk = _rms_bwd(ckv_ref[...], gk_ref[...], _dot_nt(dkv_ref[...], wkv_ref[...]))
        dp_ref[:, offs[5]:offs[6]] = dcq.astype(BF16)
        dp_ref[:, offs[6]:offs[7]] = dckv.astype(BF16)
        dp_ref[:, offs[7]:] = _rope_t(dkr, cos_t, sin_a, sin_b).astype(BF16)
        j = 0
        for g, cnt in enumerate(counts):
            acc = ins[j][...].astype(F32)
            for jj in range(1, cnt):
                acc = acc + ins[j + jj][...].astype(F32)
            dp_ref[:, offs[g]:offs[g] + IN_WIDTHS[g]] = acc.astype(BF16)
            j += cnt
        xv = x_ref[...]
        h_ref[...] = _rms(xv, g_ref[...]).astype(BF16)
        dxn, dg = _rms_bwd(xv, g_ref[...], _dot(dp_ref[...], w_ref[...]))
        dx_ref[...] = dx1_ref[...] + dxn

        @pl.when(pl.program_id(0) == 0)
        def _():
            dg_ref[...] = jnp.zeros_like(dg_ref)
            dgq_ref[...] = jnp.zeros_like(dgq_ref)
            dgk_ref[...] = jnp.zeros_like(dgk_ref)
            accq_ref[...] = jnp.zeros_like(accq_ref)
            acckv_ref[...] = jnp.zeros_like(acckv_ref)

        dg_ref[...] += dg
        dgq_ref[...] += dgq
        dgk_ref[...] += dgk
        accq_ref[...] += _dot_tn(dqf_ref[...], cqn_ref[...])
        acckv_ref[...] += _dot_tn(ckn_ref[...], dkv_ref[...])

        @pl.when(pl.program_id(0) == last)
        def _():
            dwq_ref[...] = accq_ref[...].astype(dwq_ref.dtype)
            dwkv_ref[...] = acckv_ref[...].astype(dwkv_ref.dtype)

    tok = lambda wd: pl.BlockSpec((tm, wd), lambda i: (i, 0))
    vec = lambda wd: pl.BlockSpec((1, wd), lambda i: (0, 0))
    whole = lambda r, c: pl.BlockSpec((r, c), lambda i: (0, 0))
    tab = pl.BlockSpec((tm, 128), lambda i: (i % nblk, 0))
    sds = lambda wd, dt: jax.ShapeDtypeStruct((t, wd), dt)
    return pl.pallas_call(
        body, name="inproj_qkv_bwd", grid=(t // tm,),
        out_shape=[sds(D_MODEL, F32), sds(D_MODEL, BF16), sds(D_IN_PAD, BF16),
                   jax.ShapeDtypeStruct((B_HEADS * QK_PAD, Q_LORA), BF16), jax.ShapeDtypeStruct((KV_LORA, 1024), BF16),
                   jax.ShapeDtypeStruct((1, D_MODEL), F32), jax.ShapeDtypeStruct((1, Q_LORA), F32),
                   jax.ShapeDtypeStruct((1, KV_LORA), F32)],
        in_specs=[tok(D_MODEL), _const_spec((1, D_MODEL)), _const_spec((D_IN_PAD, D_MODEL)), tok(D_MODEL), tok(Q_LORA),
                  tok(KV_LORA), _const_spec((1, Q_LORA)), _const_spec((1, KV_LORA)), _const_spec((1024, Q_LORA)),
                  _const_spec((KV_LORA, 1024)), tab, tab, tab, tok(1024), tok(1024), tok(512)] + [tok(512)] * n_flat,
        out_specs=[tok(D_MODEL), tok(D_MODEL), tok(D_IN_PAD), whole(B_HEADS * QK_PAD, Q_LORA), whole(KV_LORA, 1024),
                   vec(D_MODEL), vec(Q_LORA), vec(KV_LORA)],
        scratch_shapes=[pltpu.VMEM((tm, Q_LORA), BF16), pltpu.VMEM((tm, 1024), BF16), pltpu.VMEM((tm, KV_LORA), BF16),
                        pltpu.VMEM((tm, 1024), BF16), pltpu.VMEM((B_HEADS * QK_PAD, Q_LORA), F32),
                        pltpu.VMEM((KV_LORA, 1024), F32)],
        compiler_params=_params(),
    )(x, g1, w_in, dx1, cq, ckv, g_qa, g_kva, w_q, w_kv, *tables, dq, dk, dv, *flat)


def _cols_from_slots(g):
    n, r, cs = g.shape
    return g.transpose(1, 0, 2).reshape(r, n * cs)


def _cols_to_slots(full):
    r, c = full.shape
    return full.reshape(r, N_DEV, c // N_DEV).transpose(1, 0, 2)


def _arrange_w_in_t(w_in_t):
    return jnp.concatenate([w_in_t, jnp.zeros((D_IN_PAD - D_IN, D_MODEL), w_in_t.dtype)], axis=0)


def _arrange_w_q_t(w_q_t):
    q3 = w_q_t.reshape(B_HEADS, B_NOPE + B_ROPE, Q_LORA)
    pad = jnp.zeros((B_HEADS, QK_PAD - B_NOPE - B_ROPE, Q_LORA), w_q_t.dtype)
    return jnp.concatenate([q3, pad], axis=1).reshape(B_HEADS * QK_PAD, Q_LORA)


def _unarrange_w_q_t(d_q_t):
    return d_q_t.reshape(B_HEADS, QK_PAD, Q_LORA)[:, :B_NOPE + B_ROPE].reshape(B_HEADS * (B_NOPE + B_ROPE), Q_LORA)


def _step_core(x, loss_target, small_w, lb_full, early_full, late, seq, group, tiles, distributed):
    g1, g_hgrn, g_qa, g_kva, g_mla, g2, g_fin = small_w
    w_in, w_q, w_kv = _arrange_w_in_t(early_full[0]), _arrange_w_q_t(early_full[1]), early_full[2]
    nb = x.shape[0]
    t = nb * seq
    tm, tm_fwd, tq_f, tq_b, tt = tiles
    xt = x.reshape(t, D_MODEL)
    tgt = loss_target.reshape(t, D_MODEL)
    tables = _rope_tables(seq)

    hq, hi, zf, zb, hg, cq, ckv, qcat, kcat, vv = _inproj_qkv(xt, g1, w_in, g_qa, g_kva, w_q, w_kv, tables, seq, tm_fwd)
    if distributed:
        oattn, lse, *late_slots = _attn_fwd(qcat, kcat, vv, nb, seq, tq_f, gather=tuple(late))
    else:
        oattn, lse = _attn_fwd(qcat, kcat, vv, nb, seq, tq_f)
        late_slots = late
    w_out = late_slots[0].reshape(D_MODEL, D_MODEL)
    w_gate, w_up = late_slots[1].reshape(D_FF, D_MODEL), late_slots[2].reshape(D_FF, D_MODEL)
    w_down = late_slots[3].reshape(D_FF, D_MODEL)
    lbl_f, lbl_b = lb_full[0], lb_full[1]
    o_f, o_b, save_f, save_b = _gla_fwd(hq, hi, (zf, zb), (lbl_f, lbl_b), nb, seq, group)
    x1, x2, ycat_b, gate_b, up_b, loss_row = _post_fwd(
        xt, o_f, o_b, hg, oattn, tgt, g_hgrn, g_mla, w_out, g2, w_gate, w_up, w_down, g_fin, tm_fwd)

    (dx1, d_ya, d_oattn, dx1_b, h2_b, dgate_b, dup_b, act_b, dx2_b, d_g_mla, d_g2, d_g_fin) = _post_bwd(
        x1, x2, gate_b, up_b, oattn, tgt, g_mla, w_out, g2, w_gate, w_up, w_down, g_fin, tm)
    d_w_gate = _matmul_tn(dgate_b, h2_b, 512, tt, "gate")
    d_w_up = _matmul_tn(dup_b, h2_b, 512, tt, "up")
    d_w_down = _matmul_tn(act_b, dx2_b, 512, tt, "down")
    d_w_out = _matmul_tn(ycat_b, dx1_b, D_MODEL, tt, "out")
    late_g = [d_w_out.reshape(N_DEV, D_MODEL // N_DEV, D_MODEL)] + [
        g.reshape(N_DEV, D_FF // N_DEV, D_MODEL) for g in (d_w_gate, d_w_up, d_w_down)]
    if distributed:
        dq, dk, dv, *late_g = _attn_bwd(qcat, kcat, vv, oattn, lse, d_oattn, nb, seq, tq_b, exchange=tuple(late_g))
    else:
        dq, dk, dv = _attn_bwd(qcat, kcat, vv, oattn, lse, d_oattn, nb, seq, tq_b)
    d_o, d_hg, d_g_hgrn = _gla_combine_bwd(o_f, o_b, hg, g_hgrn, d_ya, tm_fwd)
    dq_f, dv_f, dz_f, dq_b, dv_b, dz_b, dl_f, dl_b = _gla_bwd(
        hq, hi, (zf, zb), (lbl_f, lbl_b), (save_f, save_b), d_o, nb, seq, group)
    grad_x, h1_b, dproj_b, d_w_q, d_w_kv, d_g1, d_g_qa, d_g_kva = _inproj_qkv_bwd(
        xt, g1, w_in, dx1, [[dq_f, dq_b], [dv_f, dv_b], [dz_f], [dz_b], [d_hg]], cq, ckv, g_qa, g_kva, w_q, w_kv, tables,
        dq, dk, dv, seq, tm_fwd)
    half = D_MODEL // 2
    in_slots = lambda g: g.reshape(N_DEV, D_IN // N_DEV, half)
    g_in_a = in_slots(_matmul_tn(dproj_b, h1_b, half, tt, "in_a", b_cols=(0, half), k_out=D_IN))
    if distributed:
        d_w_in_b, g_in_a = _matmul_tn(dproj_b, h1_b, half, tt, "in_b", b_cols=(half, half), k_out=D_IN, exchange=(g_in_a,))
    else:
        d_w_in_b = _matmul_tn(dproj_b, h1_b, half, tt, "in_b", b_cols=(half, half), k_out=D_IN)

    early_g = [in_slots(d_w_in_b), _unarrange_w_q_t(d_w_q).reshape(N_DEV, 768 // N_DEV, Q_LORA), _cols_to_slots(d_w_kv)]
    d_lb = jnp.stack([jnp.sum(dl_f, axis=0), jnp.sum(dl_b, axis=0)], axis=0)
    small_grads = [d_g1, d_g_hgrn, d_g_qa, d_g_kva, d_g_mla, d_g2, d_g_fin]
    return loss_row, grad_x.reshape(nb, seq, D_MODEL), g_in_a, early_g, late_g, small_grads, d_lb


def kernel(x, norm1_g, w_in, lb_logits, hgrn_norm_g, q_a_norm_g, w_q_b, kv_a_norm_g, w_kv_b, mla_norm_g, w_out, norm2_g, w_gate, w_up, w_down, final_norm_g, loss_target, m_norm1_g, m_w_in, m_lb_logits, m_hgrn_norm_g, m_q_a_norm_g, m_w_q_b, m_kv_a_norm_g, m_w_kv_b, m_mla_norm_g, m_w_out, m_norm2_g, m_w_gate, m_w_up, m_w_down, m_final_norm_g, v_norm1_g, v_w_in, v_lb_logits, v_hgrn_norm_g, v_q_a_norm_g, v_w_q_b, v_kv_a_norm_g, v_w_kv_b, v_mla_norm_g, v_w_out, v_norm2_g, v_w_gate, v_w_up, v_w_down, v_final_norm_g):
    big_w = [w_in, w_q_b, w_kv_b, w_out, w_gate, w_up, w_down]
    big_m = [m_w_in, m_w_q_b, m_w_kv_b, m_w_out, m_w_gate, m_w_up, m_w_down]
    big_v = [v_w_in, v_w_q_b, v_w_kv_b, v_w_out, v_w_gate, v_w_up, v_w_down]
    small_w = [norm1_g, hgrn_norm_g, q_a_norm_g, kv_a_norm_g, mla_norm_g, norm2_g, final_norm_g]
    small_m = [m_norm1_g, m_hgrn_norm_g, m_q_a_norm_g, m_kv_a_norm_g, m_mla_norm_g, m_norm2_g, m_final_norm_g]
    small_v = [v_norm1_g, v_hgrn_norm_g, v_q_a_norm_g, v_kv_a_norm_g, v_mla_norm_g, v_norm2_g, v_final_norm_g]
    seq = x.shape[1]
    my_id = 4 * lax.axis_index("x") + 2 * lax.axis_index("y") + lax.axis_index("c")

    shard = lambda w: w[0].astype(BF16)
    col_t = lambda w: jnp.swapaxes(w, 1, 2)[0]
    shard_t = lambda w: col_t(w).astype(BF16)
    g_in, g_q, g_kv, g_lb = _all_gather_call([shard_t(w_in), shard_t(w_q_b), shard(w_kv_b), lb_logits.reshape(4, 64)])
    early_full = (g_in.reshape(D_IN, D_MODEL), g_q.reshape(768, Q_LORA), _cols_from_slots(g_kv))
    lb_full = g_lb.reshape(N_DEV, 2, 2, 64).transpose(1, 2, 0, 3).reshape(2, 2, 512)

    as_row = lambda a: a.reshape(1, -1)
    loss_row, grad_x, recv_in_a, early_g, late_recv, small_g, d_lb = _step_core(
        x, loss_target, [as_row(s) for s in small_w], lb_full, early_full,
        [shard(w_out), shard_t(w_gate), shard_t(w_up), shard(w_down)], seq, min(32, seq // CHUNK),
        (256, 512, min(1024, seq), min(1024, seq), min(2048, 2 * seq)), True)

    grads, deltas, new_ms, new_vs = {}, {}, {}, {}
    views = {name: (col_t if name in ("w_in", "w_q_b", "w_gate", "w_up") else (lambda a: a[0])) for name, _, _, _ in BIG}
    backs = {name: ((lambda a: jnp.swapaxes(a[None], 1, 2)) if name in ("w_in", "w_q_b", "w_gate", "w_up") else (lambda a: a[None]))
             for name, _, _, _ in BIG}
    by_name = {name: (w, m, v) for (name, _, _, _), w, m, v in zip(BIG, big_w, big_m, big_v)}
    late_names = ["w_out", "w_gate", "w_up", "w_down"]
    n_small = len(small_g)
    g_l, d_l, nm_l, nv_l, recv = _adamw_recv_hosting(
        [views[n](by_name[n][0]) for n in late_names], list(late_recv), [views[n](by_name[n][1]) for n in late_names],
        [views[n](by_name[n][2]) for n in late_names],
        early_g + small_g + [d_lb.reshape(4, 512), loss_row], [True] * 3 + [False] * (n_small + 2))
    for i, name in enumerate(late_names):
        grads[name], deltas[name], new_ms[name], new_vs[name] = (backs[name](a[i]) for a in (g_l, d_l, nm_l, nv_l))
    sums = _sum_slots_call(recv[3:])
    g_small = [g.reshape(s.shape) for g, s in zip(sums[:n_small], small_w)]
    g_lb_own = lax.dynamic_index_in_dim(sums[n_small].reshape(2, 2, N_DEV, 64), my_id, axis=2, keepdims=False)
    loss = sums[n_small + 1][0, 0]

    for name, r in zip(["w_in", "w_q_b", "w_kv_b"], recv[:3]):
        w, m, v = (views[name](a) for a in by_name[name])
        g, d, nm, nv = _adamw_recv_halves(w, (recv_in_a, r), m, v, name) if name == "w_in" else _adamw_recv(w, r, m, v, name)
        grads[name], deltas[name], new_ms[name], new_vs[name] = (backs[name](a) for a in (g, d, nm, nv))
    lb_rows = lambda a: a.reshape(4, 64)
    d_s, nm_s, nv_s = _adamw_small(
        [as_row(a) for a in small_w] + [lb_rows(lb_logits)], [as_row(a) for a in g_small] + [lb_rows(g_lb_own)],
        [as_row(a) for a in small_m] + [lb_rows(m_lb_logits)], [as_row(a) for a in small_v] + [lb_rows(v_lb_logits)])
    for i, (s, (name, _)) in enumerate(zip(small_w + [lb_logits], SMALL + (("lb_logits", 0),))):
        grads[name] = (g_small + [g_lb_own])[i]
        deltas[name], new_ms[name], new_vs[name] = d_s[i].reshape(s.shape), nm_s[i].reshape(s.shape), nv_s[i].reshape(s.shape)

    order = ["norm1_g", "w_in", "lb_logits", "hgrn_norm_g", "q_a_norm_g", "w_q_b", "kv_a_norm_g", "w_kv_b", "mla_norm_g",
             "w_out", "norm2_g", "w_gate", "w_up", "w_down", "final_norm_g"]
    return (loss, grad_x, *[grads[n] for n in order], *[deltas[n] for n in order],
            *[new_ms[n] for n in order], *[new_vs[n] for n in order])
```

```python
import functools

import jax
import jax.numpy as jnp
from jax import lax
from jax.experimental import pallas as pl
from jax.experimental.pallas import tpu as pltpu

F32 = jnp.float32
BF16 = jnp.bfloat16

N_DEV = 8
D_MODEL = 1024
D_FF = 2816
A_WIDTH = 512
HEAD_PAIR = 128
CHUNK = 64
B_HEADS = 4
B_NOPE = 128
B_ROPE = 64
B_V = 128
QK_PAD = 256
Q_LORA = 384
KV_LORA = 256
D_IN = 3264
D_IN_PAD = 3328
IN_WIDTHS = (512, 512, 512, 512, 512, Q_LORA, KV_LORA, 128)
ROPE_THETA = 10000.0
EPS = 1e-6
ATTN_SCALE = (B_NOPE + B_ROPE) ** -0.5
ATTN_SUB = 256
ATTN_SUB_BWD = 256
ROW_SUB = 256
ADAM_LR, ADAM_B1, ADAM_B2, ADAM_EPS, ADAM_WD, ADAM_STEP = 0.001, 0.9, 0.999, 1e-08, 0.01, 10
VMEM_LIMIT = 60 * 1024 * 1024
MESH = pl.DeviceIdType.MESH

BIG = (("w_in", 1024, D_IN, 1), ("w_q_b", Q_LORA, 768, 1), ("w_kv_b", KV_LORA, 1024, 1), ("w_out", 1024, 1024, 0),
       ("w_gate", 1024, D_FF, 1), ("w_up", 1024, D_FF, 1), ("w_down", D_FF, 1024, 0))
SMALL = (("norm1_g", 1024), ("hgrn_norm_g", 512), ("q_a_norm_g", 384), ("kv_a_norm_g", 256), ("mla_norm_g", 512),
         ("norm2_g", 1024), ("final_norm_g", 1024))


def _params(**kw):
    return pltpu.CompilerParams(vmem_limit_bytes=VMEM_LIMIT, **kw)


def _const_spec(shape):
    return pl.BlockSpec(shape, lambda *_: (0,) * len(shape), pipeline_mode=pl.Buffered(1))


def _dot(a, b):
    return jnp.dot(a, b, preferred_element_type=F32)


def _dot_nt(a, b):
    return lax.dot_general(a, b, (((1,), (1,)), ((), ())), preferred_element_type=F32)


def _dot_tn(a, b):
    return lax.dot_general(a, b, (((0,), (0,)), ((), ())), preferred_element_type=F32)


@jax.custom_vjp
def _mm(a, b):
    return _dot(a.astype(BF16), b.astype(BF16))


def _mm_fwd(a, b):
    return _mm(a, b), (a, b)


def _mm_bwd(res, g):
    a, b = res
    gb = g.astype(BF16)
    return _dot_nt(gb, b.astype(BF16)), _dot_tn(a.astype(BF16), gb)


_mm.defvjp(_mm_fwd, _mm_bwd)


@jax.custom_vjp
def _mm_nt(a, b):
    return _dot_nt(a.astype(BF16), b.astype(BF16))


def _mm_nt_fwd(a, b):
    return _mm_nt(a, b), (a, b)


def _mm_nt_bwd(res, g):
    a, b = res
    gb = g.astype(BF16)
    return _dot(gb, b.astype(BF16)), _dot_tn(gb, a.astype(BF16))


_mm_nt.defvjp(_mm_nt_fwd, _mm_nt_bwd)


@jax.custom_vjp
def _mm_tn(a, b):
    return _dot_tn(a.astype(BF16), b.astype(BF16))


def _mm_tn_fwd(a, b):
    return _mm_tn(a, b), (a, b)


def _mm_tn_bwd(res, g):
    a, b = res
    gb = g.astype(BF16)
    return _dot_nt(b.astype(BF16), gb), _dot(a.astype(BF16), gb)


_mm_tn.defvjp(_mm_tn_fwd, _mm_tn_bwd)


def _dot_exact_rhs(a, m):
    hi = a.astype(BF16)
    lo = (a - hi.astype(F32)).astype(BF16)
    return _dot(hi, m) + _dot(lo, m)


@jax.custom_vjp
def _group_mean(a, m):
    return _dot_exact_rhs(a, m)


def _group_mean_fwd(a, m):
    return _group_mean(a, m), m


def _group_mean_bwd(m, g):
    return _dot_exact_rhs(g, m), jnp.zeros_like(m)


_group_mean.defvjp(_group_mean_fwd, _group_mean_bwd)


def _roll_rows(a, shift):
    return pltpu.roll(a, shift, 0)


def _cumsum_rows_raw(a, reverse):
    n = a.shape[0]
    row = lax.broadcasted_iota(jnp.int32, a.shape, 0)
    s = 1
    while s < n:
        if reverse:
            a = a + jnp.where(row < n - s, _roll_rows(a, n - s), 0.0)
        else:
            a = a + jnp.where(row >= s, _roll_rows(a, s), 0.0)
        s *= 2
    return a


@functools.partial(jax.custom_vjp, nondiff_argnums=(1,))
def _cumsum_rows(a, reverse):
    return _cumsum_rows_raw(a, reverse)


def _cumsum_rows_fwd(a, reverse):
    return _cumsum_rows_raw(a, reverse), None


def _cumsum_rows_bwd(reverse, _, g):
    return (_cumsum_rows_raw(g, not reverse),)


_cumsum_rows.defvjp(_cumsum_rows_fwd, _cumsum_rows_bwd)


def _rms(x, g):
    r = lax.rsqrt(jnp.mean(x * x, axis=-1, keepdims=True) + EPS)
    return x * r * g


def _rms_bwd(x, g, dy):
    r = lax.rsqrt(jnp.mean(x * x, axis=-1, keepdims=True) + EPS)
    xh = x * r
    dg = jnp.sum(dy * xh, axis=0, keepdims=True)
    dxh = dy * g
    dx = r * (dxh - xh * jnp.mean(dxh * xh, axis=-1, keepdims=True))
    return dx, dg


def _sigmoid(a):
    return jax.nn.sigmoid(a)


def _mesh_place():
    x, y, c = lax.axis_index("x"), lax.axis_index("y"), lax.axis_index("c")
    return x, y, c


def _dev_index(p):
    return 4 * p[0] + 2 * p[1] + p[2]


def _comm_sems(n):
    return [pltpu.SemaphoreType.DMA((n, 7)), pltpu.SemaphoreType.DMA((n, 7)), pltpu.SemaphoreType.DMA((n,))]


def _gather_protocol(ins, outs, send_sems, recv_sems, local_sems):
    n = len(ins)
    x, y, c = _mesh_place()
    me, sibling = (x, y, c), (x, y, 1 - c)
    chips = [(1 - x, y), (x, 1 - y), (1 - x, 1 - y)]

    def copy(a, k, block, to, src=None):
        slot = outs[a].at[_dev_index(block)]
        return pltpu.make_async_remote_copy(
            src_ref=slot if src is None else src, dst_ref=slot,
            send_sem=send_sems.at[a, k], recv_sem=recv_sems.at[a, k], device_id=to, device_id_type=MESH)

    def mine(a):
        return pltpu.make_async_copy(ins[a], outs[a].at[_dev_index(me)], local_sems.at[a])

    def first(a):
        return [copy(a, 0, me, sibling, src=ins[a])] + [copy(a, 1 + j, me, (*chip, c), src=ins[a]) for j, chip in enumerate(chips)]

    def start():
        for a in range(n):
            mine(a).start()
            for cp in first(a):
                cp.start()

    def forward():
        for a in range(n):
            for j, chip in enumerate(chips):
                copy(a, 1 + j, (*chip, c), me).wait_recv()
                copy(a, 4 + j, (*chip, c), sibling).start()

    def finish():
        for a in range(n):
            copy(a, 0, sibling, me).wait_recv()
            for j, chip in enumerate(chips):
                copy(a, 4 + j, (*chip, 1 - c), me).wait_recv()
        for a in range(n):
            mine(a).wait()
            for cp in first(a):
                cp.wait_send()
            for j, chip in enumerate(chips):
                copy(a, 4 + j, (*chip, c), sibling).wait_send()

    return start, forward, finish


def _exchange_protocol(ins, outs, scatter, send_sems, recv_sems, local_sems):
    n = len(ins)
    x, y, c = _mesh_place()
    me = (x, y, c)
    my_id = _dev_index(me)
    rels = [(dx, dy, dc) for dx in (0, 1) for dy in (0, 1) for dc in (0, 1)][1:]

    def peer_of(rel):
        return tuple(1 - v if d else v for v, d in zip(me, rel))

    def src(a, dev):
        return ins[a].at[dev] if scatter[a] else ins[a]

    def send(a, k):
        peer = peer_of(rels[k])
        return pltpu.make_async_remote_copy(
            src_ref=src(a, _dev_index(peer)), dst_ref=outs[a].at[my_id],
            send_sem=send_sems.at[a, k], recv_sem=recv_sems.at[a, k], device_id=peer, device_id_type=MESH)

    def arrival(a, k):
        peer = peer_of(rels[k])
        return pltpu.make_async_remote_copy(
            src_ref=src(a, my_id), dst_ref=outs[a].at[_dev_index(peer)],
            send_sem=send_sems.at[a, k], recv_sem=recv_sems.at[a, k], device_id=peer, device_id_type=MESH)

    def own(a):
        return pltpu.make_async_copy(src(a, my_id), outs[a].at[my_id], local_sems.at[a])

    def start():
        for a in range(n):
            own(a).start()
            for k in range(7):
                send(a, k).start()

    def finish():
        for a in range(n):
            for k in range(7):
                arrival(a, k).wait_recv()
        for a in range(n):
            for k in range(7):
                send(a, k).wait_send()
            own(a).wait()

    return start, finish


def _slot_shapes(blocks, scatter=None):
    return [jax.ShapeDtypeStruct(b.shape if (scatter and scatter[a]) else (N_DEV,) + b.shape, b.dtype) for a, b in enumerate(blocks)]


def _all_gather_call(blocks):
    n = len(blocks)

    def body(*refs):
        start, forward, finish = _gather_protocol(refs[:n], refs[n:2 * n], *refs[2 * n:])
        start()
        forward()
        finish()

    any_spec = pl.BlockSpec(memory_space=pl.ANY)
    return pl.pallas_call(
        body, name="weights_all_gather", out_shape=_slot_shapes(blocks),
        in_specs=[any_spec] * n, out_specs=[any_spec] * n, scratch_shapes=_comm_sems(n),
    )(*blocks)


def _sum_slots_call(recvs):
    n = len(recvs)

    def body(*refs):
        for in_ref, out_ref in zip(refs[:n], refs[n:]):
            acc = in_ref[0]
            for j in range(1, N_DEV):
                acc = acc + in_ref[j]
            out_ref[...] = acc

    return pl.pallas_call(
        body, name="small_grad_sum", out_shape=[jax.ShapeDtypeStruct(r.shape[1:], F32) for r in recvs],
        compiler_params=_params(),
    )(*recvs)


def _adam_update(w, g, m, v):
    nm = ADAM_B1 * m + (1.0 - ADAM_B1) * g
    nv = ADAM_B2 * v + (1.0 - ADAM_B2) * (g * g)
    bc1 = 1.0 - ADAM_B1 ** ADAM_STEP
    bc2 = 1.0 - ADAM_B2 ** ADAM_STEP
    return -ADAM_LR * ((nm / bc1) / (jnp.sqrt(nv / bc2) + ADAM_EPS) + ADAM_WD * w), nm, nv


def _adamw_recv(w, recv, m, v, tag):
    r, c = w.shape
    tr = r
    for cand in (512, 256, 128):
        if r > cand and r % cand == 0:
            tr = cand
            break

    def body(w_ref, r_ref, m_ref, v_ref, g_ref, d_ref, nm_ref, nv_ref):
        g = r_ref[0].astype(F32)
        for j in range(1, N_DEV):
            g = g + r_ref[j].astype(F32)
        g_ref[...] = g
        d_ref[...], nm_ref[...], nv_ref[...] = _adam_update(w_ref[...], g, m_ref[...], v_ref[...])

    spec = pl.BlockSpec((tr, c), lambda i: (i, 0))
    return pl.pallas_call(
        body, name="adamw_" + tag, out_shape=[jax.ShapeDtypeStruct(w.shape, F32)] * 4, grid=(r // tr,),
        in_specs=[spec, pl.BlockSpec((N_DEV, tr, c), lambda i: (0, i, 0)), spec, spec], out_specs=[spec] * 4,
        compiler_params=_params(),
    )(w, recv, m, v)


def _adamw_recv_halves(w, recv_halves, m, v, tag):
    r, c = w.shape
    half = c // 2

    def body(w_ref, ra_ref, rb_ref, m_ref, v_ref, g_ref, d_ref, nm_ref, nv_ref):
        def update(r_ref):
            g = r_ref[0].astype(F32)
            for j in range(1, N_DEV):
                g = g + r_ref[j].astype(F32)
            g_ref[...] = g
            d_ref[...], nm_ref[...], nv_ref[...] = _adam_update(w_ref[...], g, m_ref[...], v_ref[...])

        pl.when(pl.program_id(0) == 0)(lambda: update(ra_ref))
        pl.when(pl.program_id(0) == 1)(lambda: update(rb_ref))

    spec = pl.BlockSpec((r, half), lambda j: (0, j))
    whole = pl.BlockSpec((N_DEV, r, half), lambda j: (0, 0, 0))
    return pl.pallas_call(
        body, name="adamw_" + tag, out_shape=[jax.ShapeDtypeStruct(w.shape, F32)] * 4, grid=(2,),
        in_specs=[spec, whole, whole, spec, spec], out_specs=[spec] * 4, compiler_params=_params(),
    )(w, *recv_halves, m, v)


def _adamw_recv_hosting(ws, recvs, ms, vs, blocks, scatter):
    n, ne = len(ws), len(blocks)
    rows = max(w.shape[0] for w in ws)
    cols = ws[0].shape[1]
    assert all(w.shape[1] == cols for w in ws)

    def body(*refs):
        ins, ex_in = refs[:4 * n], refs[4 * n:4 * n + ne]
        outs, ex_out = refs[4 * n + ne:8 * n + ne], refs[8 * n + ne:8 * n + 2 * ne]
        in_buf, recv_buf, out_buf, in_sems, out_sems = refs[8 * n + 2 * ne:8 * n + 2 * ne + 5]
        start, finish = _exchange_protocol(ex_in, ex_out, scatter, *refs[8 * n + 2 * ne + 5:])
        start()
        for a in range(n):
            r = pl.ds(0, ws[a].shape[0])
            loads = [pltpu.make_async_copy(ins[k * n + a], in_buf.at[j, r], in_sems.at[j]) for j, k in enumerate((0, 2, 3))]
            loads.append(pltpu.make_async_copy(ins[n + a], recv_buf.at[:, r], in_sems.at[3]))
            for cp in loads:
                cp.start()
            for cp in loads:
                cp.wait()
            g = recv_buf[0, r].astype(F32)
            for j in range(1, N_DEV):
                g = g + recv_buf[j, r].astype(F32)
            out_buf[0, r] = g
            out_buf[1, r], out_buf[2, r], out_buf[3, r] = _adam_update(in_buf[0, r], g, in_buf[1, r], in_buf[2, r])
            stores = [pltpu.make_async_copy(out_buf.at[k, r], outs[k * n + a], out_sems.at[k]) for k in range(4)]
            for cp in stores:
                cp.start()
            for cp in stores:
                cp.wait()
        finish()

    any_spec = pl.BlockSpec(memory_space=pl.ANY)
    out = pl.pallas_call(
        body, name="adamw_late_and_grad_exchange",
        out_shape=[jax.ShapeDtypeStruct(w.shape, F32) for w in ws] * 4 + _slot_shapes(blocks, scatter),
        in_specs=[any_spec] * (4 * n + ne), out_specs=[any_spec] * (4 * n + ne),
        scratch_shapes=[pltpu.VMEM((3, rows, cols), F32), pltpu.VMEM((N_DEV, rows, cols), BF16), pltpu.VMEM((4, rows, cols), F32),
                        pltpu.SemaphoreType.DMA((4,)), pltpu.SemaphoreType.DMA((4,))] + _comm_sems(ne),
        compiler_params=_params(),
    )(*ws, *recvs, *ms, *vs, *blocks)
    return out[:n], out[n:2 * n], out[2 * n:3 * n], out[3 * n:4 * n], out[4 * n:]


def _adamw_small(ws, gs, ms, vs):
    n = len(ws)

    def body(*refs):
        ins, outs = refs[:4 * n], refs[4 * n:]
        for a in range(n):
            d, nm, nv = _adam_update(ins[a][...], ins[n + a][...], ins[2 * n + a][...], ins[3 * n + a][...])
            outs[a][...], outs[n + a][...], outs[2 * n + a][...] = d, nm, nv

    out = pl.pallas_call(
        body, name="adamw_small", out_shape=[jax.ShapeDtypeStruct(w.shape, F32) for w in ws] * 3, compiler_params=_params(),
    )(*ws, *gs, *ms, *vs)
    return out[:n], out[n:2 * n], out[2 * n:]


def _rope_tables(seq):
    inv = 1.0 / (ROPE_THETA ** (jnp.arange(0, B_ROPE, 2, dtype=F32) / B_ROPE))
    ang = jnp.arange(seq, dtype=F32)[:, None] * inv[None, :]
    cos, sin = jnp.cos(ang), jnp.sin(ang)
    z32, z64 = jnp.zeros_like(cos), jnp.zeros((seq, 64), F32)
    cos_t = jnp.concatenate([cos, cos, z64], axis=1)
    sin_a = jnp.concatenate([-sin, z32, z64], axis=1)
    sin_b = jnp.concatenate([z32, sin, z64], axis=1)
    return cos_t, sin_a, sin_b


def _rope(t, cos_t, sin_a, sin_b):
    return t * cos_t + pltpu.roll(t, 96, 1) * sin_a + pltpu.roll(t, 32, 1) * sin_b


def _rope_t(d, cos_t, sin_a, sin_b):
    return d * cos_t + pltpu.roll(d * sin_a, 32, 1) + pltpu.roll(d * sin_b, 96, 1)


def _inproj_qkv(x, g1, w_in, g_qa, g_kva, w_q, w_kv, tables, seq, tm):
    t = x.shape[0]
    nblk = seq // tm
    n_plain = 7
    offs = [sum(IN_WIDTHS[:j]) for j in range(len(IN_WIDTHS))]

    def body(x_ref, g_ref, w_ref, gq_ref, gk_ref, wq_ref, wkv_ref, c_ref, sa_ref, sb_ref, *outs):
        q_out, k_out, v_out = outs[n_plain:]
        for j in range(tm // min(tm, ROW_SUB)):
            r = pl.ds(j * min(tm, ROW_SUB), min(tm, ROW_SUB))
            h = _rms(x_ref[r, :], g_ref[...]).astype(BF16)
            proj = lambda g: _dot_nt(h, w_ref[offs[g]:offs[g] + IN_WIDTHS[g], :])
            for g in range(5):
                outs[g][r, :] = proj(g)
            cq, ckv, kr = proj(5), proj(6), proj(7)
            outs[5][r, :] = cq
            outs[6][r, :] = ckv
            cos_t, sin_a, sin_b = c_ref[r, :], sa_ref[r, :], sb_ref[r, :]
            cqn = _rms(cq, gq_ref[...]).astype(BF16)
            ckn = _rms(ckv, gk_ref[...]).astype(BF16)
            kr_rot = _rope(kr, cos_t, sin_a, sin_b).astype(BF16)
            for hd in range(B_HEADS):
                lo = hd * QK_PAD
                q_out[r, lo:lo + 128] = (_dot_nt(cqn, wq_ref[lo:lo + 128, :]) * ATTN_SCALE).astype(BF16)
                qr = _rope(_dot_nt(cqn, wq_ref[lo + 128:lo + 256, :]), cos_t, sin_a, sin_b)
                q_out[r, lo + 128:lo + 256] = (qr * ATTN_SCALE).astype(BF16)
                k_out[r, lo:lo + 128] = _dot(ckn, wkv_ref[:, lo:lo + 128]).astype(BF16)
                k_out[r, lo + 128:lo + 256] = kr_rot
                v_out[r, hd * B_V:(hd + 1) * B_V] = _dot(ckn, wkv_ref[:, lo + 128:lo + 256]).astype(BF16)

    tok = lambda wd: pl.BlockSpec((tm, wd), lambda i: (i, 0))
    tab = pl.BlockSpec((tm, 128), lambda i: (i % nblk, 0))
    widths = list(IN_WIDTHS[:n_plain]) + [B_HEADS * QK_PAD, B_HEADS * QK_PAD, B_HEADS * B_V]
    dtypes = [F32] * n_plain + [BF16] * 3
    return pl.pallas_call(
        body, name="inproj_qkv_fwd", grid=(t // tm,),
        out_shape=[jax.ShapeDtypeStruct((t, wd), dt) for wd, dt in zip(widths, dtypes)],
        in_specs=[tok(D_MODEL), _const_spec((1, D_MODEL)), _const_spec((D_IN_PAD, D_MODEL)), _const_spec((1, Q_LORA)),
                  _const_spec((1, KV_LORA)), _const_spec((B_HEADS * QK_PAD, Q_LORA)), _const_spec((KV_LORA, 1024)), tab, tab, tab],
        out_specs=[tok(wd) for wd in widths],
        compiler_params=_params(),
    )(x, g1, w_in, g_qa, g_kva, w_q, w_kv, *tables)


def _step_index(nq):
    return (pl.program_id(0) * B_HEADS + pl.program_id(1)) * nq + pl.program_id(2)


def _attn_fwd(qcat, kcat, v, nb, seq, tq, gather=()):
    t = qcat.shape[0]
    nq = seq // tq
    ng = len(gather)
    steps = nb * B_HEADS * nq

    def body(q_ref, k_ref, v_ref, *rest):
        o_ref, lse_ref = rest[ng:ng + 2]
        if ng:
            start, forward, finish = _gather_protocol(rest[:ng], rest[ng + 2:2 * ng + 2], *rest[2 * ng + 2:])
            pl.when(_step_index(nq) == 0)(start)
            pl.when(_step_index(nq) == (3 * steps) // 4)(forward)
        for j in range(tq // ATTN_SUB):
            r = pl.ds(j * ATTN_SUB, ATTN_SUB)
            s = _dot_nt(q_ref[r, :], k_ref[...])
            m = jnp.max(s, axis=-1, keepdims=True)
            p = jnp.exp(s - m)
            l = jnp.sum(p, axis=-1, keepdims=True)
            o_ref[r, :] = _dot(p.astype(BF16), v_ref[...]) / l
            lse_ref[0, r, :] = m + jnp.log(l)
        if ng:
            pl.when(_step_index(nq) == steps - 1)(finish)

    any_spec = pl.BlockSpec(memory_space=pl.ANY)
    return pl.pallas_call(
        body, name="attn_fwd", grid=(nb, B_HEADS, nq),
        out_shape=[jax.ShapeDtypeStruct((t, B_HEADS * B_V), F32), jax.ShapeDtypeStruct((B_HEADS, t, 1), F32)] + _slot_shapes(gather),
        in_specs=[pl.BlockSpec((tq, QK_PAD), lambda b, h, i: (b * nq + i, h)),
                  pl.BlockSpec((seq, QK_PAD), lambda b, h, i: (b, h)),
                  pl.BlockSpec((seq, B_V), lambda b, h, i: (b, h))] + [any_spec] * ng,
        out_specs=[pl.BlockSpec((tq, B_V), lambda b, h, i: (b * nq + i, h)),
                   pl.BlockSpec((1, tq, 1), lambda b, h, i: (h, b * nq + i, 0))] + [any_spec] * ng,
        scratch_shapes=_comm_sems(ng) if ng else [],
        compiler_params=_params(),
    )(qcat, kcat, v, *gather)


def _attn_bwd(qcat, kcat, v, o, lse, do, nb, seq, tq, exchange=()):
    t = qcat.shape[0]
    nq = seq // tq
    ne = len(exchange)
    steps = nb * B_HEADS * nq

    def body(q_ref, k_ref, v_ref, o_ref, lse_ref, do_ref, *rest):
        dq_ref, dk_ref, dv_ref = rest[ne:ne + 3]
        p_ref, ds_ref = rest[2 * ne + 3:2 * ne + 5]
        if ne:
            start, finish = _exchange_protocol(rest[:ne], rest[ne + 3:2 * ne + 3], [True] * ne, *rest[2 * ne + 5:])
            pl.when(_step_index(nq) == 0)(start)

        @pl.when(pl.program_id(2) == 0)
        def _():
            dv_ref[...] = jnp.zeros_like(dv_ref)
            dk_ref[...] = jnp.zeros_like(dk_ref)

        for j in range(tq // ATTN_SUB_BWD):
            r = pl.ds(j * ATTN_SUB_BWD, ATTN_SUB_BWD)
            q, k = q_ref[r, :], k_ref[...]
            do_f = do_ref[r, :].astype(F32)
            delta = jnp.sum(do_f * o_ref[r, :], axis=-1, keepdims=True)
            dob = do_f.astype(BF16)
            p = jnp.exp(_dot_nt(q, k) - lse_ref[0, r, :])
            ds = (p * (_dot_nt(dob, v_ref[...]) - delta)).astype(BF16)
            dq_ref[r, :] = _dot(ds, k).astype(dq_ref.dtype)
            p_ref[r, :] = p.astype(BF16)
            ds_ref[r, :] = ds
        dv_ref[...] += _dot_tn(p_ref[...], do_ref[...].astype(BF16))
        dk_ref[...] += _dot_tn(ds_ref[...], q_ref[...])
        if ne:
            pl.when(_step_index(nq) == steps - 1)(finish)

    qspec = lambda wd: pl.BlockSpec((tq, wd), lambda b, h, i: (b * nq + i, h))
    kspec = lambda wd: pl.BlockSpec((seq, wd), lambda b, h, i: (b, h))
    any_spec = pl.BlockSpec(memory_space=pl.ANY)
    return pl.pallas_call(
        body, name="attn_bwd", grid=(nb, B_HEADS, nq),
        out_shape=[jax.ShapeDtypeStruct((t, B_HEADS * QK_PAD), BF16), jax.ShapeDtypeStruct((t, B_HEADS * QK_PAD), F32),
                   jax.ShapeDtypeStruct((t, B_HEADS * B_V), F32)] + _slot_shapes(exchange, [True] * ne),
        in_specs=[qspec(QK_PAD), kspec(QK_PAD), kspec(B_V), qspec(B_V),
                  pl.BlockSpec((1, tq, 1), lambda b, h, i: (h, b * nq + i, 0)), qspec(B_V)] + [any_spec] * ne,
        out_specs=[qspec(QK_PAD), kspec(QK_PAD), kspec(B_V)] + [any_spec] * ne,
        scratch_shapes=[pltpu.VMEM((tq, seq), BF16), pltpu.VMEM((tq, seq), BF16)] + (_comm_sems(ne) if ne else []),
        compiler_params=_params(),
    )(qcat, kcat, v, o, lse, do, *exchange)


def _gla_consts(reverse):
    row = lax.broadcasted_iota(jnp.int32, (CHUNK, CHUNK), 0)
    col = lax.broadcasted_iota(jnp.int32, (CHUNK, CHUNK), 1)
    causal = (row <= col) if reverse else (row >= col)
    lane = lax.broadcasted_iota(jnp.int32, (1, HEAD_PAIR), 1)
    m0 = (lane < 64).astype(F32)
    m1 = 1.0 - m0
    r2 = lax.broadcasted_iota(jnp.int32, (HEAD_PAIR, HEAD_PAIR), 0)
    c2 = lax.broadcasted_iota(jnp.int32, (HEAD_PAIR, HEAD_PAIR), 1)
    same_head = ((r2 < 64) == (c2 < 64)).astype(F32)
    return causal, m0, m1, same_head


def _gla_chunk(hq, hi, z, l0, l1, st, consts, reverse):
    q_dec, k_inv, k_end, decay = _gla_gates(hq, z, l0, l1, reverse)
    o, st_new = _gla_state(q_dec, st, decay, _gla_increment(hi, k_end, consts))
    return o + _gla_intra(q_dec, k_inv, hi, consts), st_new


def _gla_gates(hq, z, l0, l1, reverse):
    mx = jnp.maximum(l0, l1)
    e0, e1 = jnp.exp(l0 - mx), jnp.exp(l1 - mx)
    lb = e0 / (e0 + e1)
    q = hq * _sigmoid(hq)
    sz = _sigmoid(z)
    log_f = jnp.log(lb + (1.0 - lb) * sz)
    k = (1.0 - lb) * (1.0 - sz)
    cum = _cumsum_rows(log_f, reverse)
    decay = jnp.exp(jnp.sum(log_f, axis=0, keepdims=True))
    k_inv = k * jnp.exp(-cum)
    return q * jnp.exp(cum), k_inv, k_inv * decay, decay


def _gla_intra(q_dec, k_inv, hi, consts):
    causal, m0, m1, _ = consts
    o = None
    for mh in (m0, m1):
        s = jnp.where(causal, _mm_nt(q_dec * mh, k_inv), 0.0)
        part = _mm(s, hi) * mh
        o = part if o is None else o + part
    return o


def _gla_increment(hi, k_end, consts):
    return _mm_tn(hi, k_end) * consts[3]


def _gla_state(q_dec, st, decay, inc):
    return _mm_nt(q_dec, st), st * decay + inc


GLA_DIRS = (False, True)
GLA_BATCH_FWD = 8
GLA_BATCH_BWD = 4


def _gla_fwd(hq, hi, zs, lbls, nb, seq, group):
    t = hq.shape[0]
    rows = group * CHUNK
    nblk = seq // rows
    n_chunks = seq // CHUNK
    nd = len(GLA_DIRS)

    def body(*refs):
        ins, outs, st_refs = refs[:4 * nd], refs[4 * nd:6 * nd], refs[6 * nd:]
        @pl.when(pl.program_id(2) == 0)
        def _():
            for st_ref in st_refs:
                st_ref[...] = jnp.zeros_like(st_ref)

        consts = [_gla_consts(rev) for rev in GLA_DIRS]
        work = [(d, rev, group - 1 - cc if rev else cc) for cc in range(group) for d, rev in enumerate(GLA_DIRS)]
        rows_of = lambda c: pl.ds(c * CHUNK, CHUNK)
        sts = [st_ref[...] for st_ref in st_refs]
        for w0 in range(0, len(work), GLA_BATCH_FWD):
            batch = work[w0:w0 + GLA_BATCH_FWD]
            gates, intra, incs = {}, {}, {}
            for d, rev, c in batch:
                hq_ref, _, z_ref, lbl_ref = ins[4 * d:4 * d + 4]
                gates[d, c] = _gla_gates(hq_ref[rows_of(c), :], z_ref[rows_of(c), :], lbl_ref[0:1, :], lbl_ref[1:2, :], rev)
            for d, rev, c in batch:
                hi_c = ins[4 * d + 1][rows_of(c), :]
                intra[d, c] = _gla_intra(gates[d, c][0], gates[d, c][1], hi_c, consts[d])
                incs[d, c] = _gla_increment(hi_c, gates[d, c][2], consts[d])
            for d, rev, c in batch:
                outs[nd + d][0, 0, c] = sts[d].astype(outs[nd + d].dtype)
                o_state, sts[d] = _gla_state(gates[d, c][0], sts[d], gates[d, c][3], incs[d, c])
                outs[d][rows_of(c), :] = (intra[d, c] + o_state).astype(outs[d].dtype)
        for st_ref, st in zip(st_refs, sts):
            st_ref[...] = st

    def tb(rev):
        return (lambda i: nblk - 1 - i) if rev else (lambda i: i)

    tok = lambda rev: pl.BlockSpec((rows, HEAD_PAIR), lambda b, p, i: (b * nblk + tb(rev)(i), p))
    lspec = pl.BlockSpec((2, HEAD_PAIR), lambda b, p, i: (0, p))
    sspec = lambda rev: pl.BlockSpec((1, 1, group, HEAD_PAIR, HEAD_PAIR), lambda b, p, i: (b, p, tb(rev)(i), 0, 0))
    args, in_specs = [], []
    for d, rev in enumerate(GLA_DIRS):
        args += [hq, hi, zs[d], lbls[d]]
        in_specs += [tok(rev), tok(rev), tok(rev), lspec]
    return pl.pallas_call(
        body, name="gla_fwd", grid=(nb, 4, nblk),
        out_shape=[jax.ShapeDtypeStruct((t, A_WIDTH), BF16)] * nd
        + [jax.ShapeDtypeStruct((nb, 4, n_chunks, HEAD_PAIR, HEAD_PAIR), BF16)] * nd,
        in_specs=in_specs, out_specs=[tok(rev) for rev in GLA_DIRS] + [sspec(rev) for rev in GLA_DIRS],
        scratch_shapes=[pltpu.VMEM((HEAD_PAIR, HEAD_PAIR), F32)] * nd,
        compiler_params=_params(),
    )(*args)


def _gla_bwd(hq, hi, zs, lbls, saved, do, nb, seq, group):
    t = hq.shape[0]
    rows = group * CHUNK
    nblk = seq // rows
    nd = len(GLA_DIRS)

    def body(*refs):
        ins, outs, dst_refs = refs[:6 * nd], refs[6 * nd:10 * nd], refs[10 * nd:]
        dl_refs = outs[3 * nd:]

        @pl.when(pl.program_id(2) == 0)
        def _():
            for dst_ref, dl_ref in zip(dst_refs, dl_refs):
                dst_ref[...] = jnp.zeros_like(dst_ref)
                dl_ref[...] = jnp.zeros_like(dl_ref)

        consts = [_gla_consts(rev) for rev in GLA_DIRS]
        dsts = [dst_ref[...] for dst_ref in dst_refs]
        dls = [[jnp.zeros((1, HEAD_PAIR), F32), jnp.zeros((1, HEAD_PAIR), F32)] for _ in GLA_DIRS]
        work = [(d, rev, cc if rev else group - 1 - cc) for cc in range(group) for d, rev in enumerate(GLA_DIRS)]
        for w0 in range(0, len(work), GLA_BATCH_BWD):
            vjps = {}
            for d, rev, c in work[w0:w0 + GLA_BATCH_BWD]:
                hq_ref, hi_ref, z_ref, lbl_ref, save_ref, _ = ins[6 * d:6 * d + 6]
                r = pl.ds(c * CHUNK, CHUNK)
                fn = functools.partial(_gla_chunk, consts=consts[d], reverse=rev)
                _, vjps[d, c] = jax.vjp(fn, hq_ref[r, :], hi_ref[r, :], z_ref[r, :], lbl_ref[0:1, :], lbl_ref[1:2, :],
                                         save_ref[0, 0, c].astype(F32))
            for d, rev, c in work[w0:w0 + GLA_BATCH_BWD]:
                dq_ref, dv_ref, dz_ref = outs[3 * d:3 * d + 3]
                r = pl.ds(c * CHUNK, CHUNK)
                d_hq, d_hi, d_z, d_l0, d_l1, dsts[d] = vjps[d, c]((ins[6 * d + 5][r, :].astype(F32), dsts[d]))
                dq_ref[r, :] = d_hq.astype(dq_ref.dtype)
                dv_ref[r, :] = d_hi.astype(dv_ref.dtype)
                dz_ref[r, :] = d_z.astype(dz_ref.dtype)
                dls[d] = [dls[d][0] + d_l0, dls[d][1] + d_l1]
        for d in range(nd):
            dst_refs[d][...] = dsts[d]
            dl_refs[d][0, 0:1, :] += dls[d][0]
            dl_refs[d][0, 1:2, :] += dls[d][1]

    def tb(rev):
        return (lambda i: i) if rev else (lambda i: nblk - 1 - i)

    tok = lambda rev: pl.BlockSpec((rows, HEAD_PAIR), lambda b, p, i: (b * nblk + tb(rev)(i), p))
    lspec = pl.BlockSpec((2, HEAD_PAIR), lambda b, p, i: (0, p))
    sspec = lambda rev: pl.BlockSpec((1, 1, group, HEAD_PAIR, HEAD_PAIR), lambda b, p, i: (b, p, tb(rev)(i), 0, 0))
    args, in_specs, out_specs = [], [], []
    for d, rev in enumerate(GLA_DIRS):
        args += [hq, hi, zs[d], lbls[d], saved[d], do]
        in_specs += [tok(rev), tok(rev), tok(rev), lspec, sspec(rev), tok(rev)]
        out_specs += [tok(rev)] * 3
    out_specs += [pl.BlockSpec((1, 2, HEAD_PAIR), lambda b, p, i: (b, 0, p))] * nd
    return pl.pallas_call(
        body, name="gla_bwd", grid=(nb, 4, nblk),
        out_shape=[jax.ShapeDtypeStruct((t, A_WIDTH), BF16)] * (3 * nd) + [jax.ShapeDtypeStruct((nb, 2, A_WIDTH), F32)] * nd,
        in_specs=in_specs, out_specs=out_specs,
        scratch_shapes=[pltpu.VMEM((HEAD_PAIR, HEAD_PAIR), F32)] * nd,
        compiler_params=_params(),
    )(*args)


def _head_mean_matrix():
    r = lax.broadcasted_iota(jnp.int32, (A_WIDTH, A_WIDTH), 0) // 64
    c = lax.broadcasted_iota(jnp.int32, (A_WIDTH, A_WIDTH), 1) // 64
    return jnp.where(r == c, 1.0 / 64.0, 0.0).astype(BF16)


def _gla_out(o_f, o_b, hg, g, mean_mat):
    o = o_f + o_b
    ms = _group_mean(o * o, mean_mat)
    return o * lax.rsqrt(ms + EPS) * g * (hg * _sigmoid(hg))


def _gla_combine_bwd(o_f, o_b, hg, g, dy, tm):
    t = o_f.shape[0]

    def body(of_ref, ob_ref, hg_ref, g_ref, dy_ref, do_ref, dhg_ref, dg_ref):
        mean_mat = _head_mean_matrix()
        fn = lambda o, hgv, gv: _gla_out(o, jnp.zeros_like(o), hgv, gv, mean_mat)
        _, vjp = jax.vjp(fn, of_ref[...].astype(F32) + ob_ref[...].astype(F32), hg_ref[...], g_ref[...])
        d_o, d_hg, d_g = vjp(dy_ref[...].astype(F32))
        do_ref[...] = d_o.astype(do_ref.dtype)
        dhg_ref[...] = d_hg.astype(dhg_ref.dtype)

        @pl.when(pl.program_id(0) == 0)
        def _():
            dg_ref[...] = jnp.zeros_like(dg_ref)

        dg_ref[...] += d_g

    tok = pl.BlockSpec((tm, A_WIDTH), lambda i: (i, 0))
    vec = pl.BlockSpec((1, A_WIDTH), lambda i: (0, 0))
    return pl.pallas_call(
        body, name="gla_combine_bwd", grid=(t // tm,),
        out_shape=[jax.ShapeDtypeStruct((t, A_WIDTH), BF16), jax.ShapeDtypeStruct((t, A_WIDTH), BF16),
                   jax.ShapeDtypeStruct((1, A_WIDTH), F32)],
        in_specs=[tok, tok, tok, _const_spec((1, A_WIDTH)), tok], out_specs=[tok, tok, vec], compiler_params=_params(),
    )(o_f, o_b, hg, g, dy)


def _post_fwd(x, o_f, o_b, hg, oattn, tgt, g_hgrn, g_mla, w_out, g2, w_gate, w_up, w_down, g_fin, tm):
    t = x.shape[0]

    def body(x_ref, of_ref, ob_ref, hg_ref, oa_ref, tgt_ref, gh_ref, gm_ref, wo_ref, g2_ref, wg_ref, wu_ref, wd_ref, gf_ref,
             x1_ref, x2_ref, ycat_ref, gate_ref, up_ref, loss_ref):
        part = jnp.zeros((1, 1), F32)
        mean_mat = _head_mean_matrix()
        for j in range(tm // min(tm, ROW_SUB)):
            r = pl.ds(j * min(tm, ROW_SUB), min(tm, ROW_SUB))
            ya = _gla_out(of_ref[r, :].astype(F32), ob_ref[r, :].astype(F32), hg_ref[r, :], gh_ref[...], mean_mat).astype(BF16)
            yb = _rms(oa_ref[r, :], gm_ref[...]).astype(BF16)
            ycat_ref[r, 0:A_WIDTH] = ya
            ycat_ref[r, A_WIDTH:] = yb
            x1 = x_ref[r, :] + _dot(ya, wo_ref[0:A_WIDTH, :]) + _dot(yb, wo_ref[A_WIDTH:, :])
            x1_ref[r, :] = x1
            h2 = _rms(x1, g2_ref[...]).astype(BF16)
            gate, up = _dot_nt(h2, wg_ref[...]), _dot_nt(h2, wu_ref[...])
            gate_ref[r, :] = gate.astype(BF16)
            up_ref[r, :] = up.astype(BF16)
            act = (gate * _sigmoid(gate) * up).astype(BF16)
            x2 = x1 + _dot(act, wd_ref[...])
            x2_ref[r, :] = x2
            err = _rms(x2, gf_ref[...]) - tgt_ref[r, :]
            part = part + 0.5 * jnp.sum(jnp.mean(err * err, axis=-1, keepdims=True), axis=0, keepdims=True)

        @pl.when(pl.program_id(0) == 0)
        def _():
            loss_ref[...] = jnp.zeros_like(loss_ref)

        loss_ref[...] += jnp.broadcast_to(part, loss_ref.shape)

    tok = lambda wd: pl.BlockSpec((tm, wd), lambda i: (i, 0))
    return pl.pallas_call(
        body, name="post_fwd", grid=(t // tm,),
        out_shape=[jax.ShapeDtypeStruct((t, D_MODEL), F32)] * 2 + [jax.ShapeDtypeStruct((t, D_MODEL), BF16)]
        + [jax.ShapeDtypeStruct((t, D_FF), BF16)] * 2 + [jax.ShapeDtypeStruct((1, 128), F32)],
        in_specs=[tok(D_MODEL), tok(A_WIDTH), tok(A_WIDTH), tok(A_WIDTH), tok(512), tok(D_MODEL), _const_spec((1, A_WIDTH)),
                  _const_spec((1, 512)), _const_spec((D_MODEL, D_MODEL)),
                  _const_spec((1, D_MODEL)), _const_spec((D_FF, D_MODEL)), _const_spec((D_FF, D_MODEL)),
                  _const_spec((D_FF, D_MODEL)), _const_spec((1, D_MODEL))],
        out_specs=[tok(D_MODEL), tok(D_MODEL), tok(D_MODEL), tok(D_FF), tok(D_FF), pl.BlockSpec((1, 128), lambda i: (0, 0))],
        compiler_params=_params(),
    )(x, o_f, o_b, hg, oattn, tgt, g_hgrn, g_mla, w_out, g2, w_gate, w_up, w_down, g_fin)


def _post_bwd(x1, x2, gate_b, up_b, oattn, tgt, g_mla, w_out, g2, w_gate, w_up, w_down, g_fin, tm):
    t = x1.shape[0]

    def body(x1_ref, x2_ref, gate_ref, up_ref, oa_ref, tgt_ref, gm_ref, wo_ref, g2_ref, wg_ref, wu_ref, wd_ref, gf_ref,
             dx1_ref, dya_ref, doa_ref, dx1b_ref, h2_ref, dgate_ref, dup_ref, act_ref, dx2b_ref,
             dgm_ref, dg2_ref, dgf_ref):
        x1, x2 = x1_ref[...], x2_ref[...]
        dy = (_rms(x2, gf_ref[...]) - tgt_ref[...]) * (1.0 / D_MODEL)
        dx2, dgf = _rms_bwd(x2, gf_ref[...], dy)
        dx2b = dx2.astype(BF16)
        dx2b_ref[...] = dx2b
        h2_ref[...] = _rms(x1, g2_ref[...]).astype(BF16)
        gate, up = gate_ref[...].astype(F32), up_ref[...].astype(F32)
        sg = _sigmoid(gate)
        sl = gate * sg
        act_ref[...] = (sl * up).astype(BF16)
        dact = _dot_nt(dx2b, wd_ref[...])
        dup = (dact * sl).astype(BF16)
        dgate = (dact * up * (sg * (1.0 + gate * (1.0 - sg)))).astype(BF16)
        dup_ref[...] = dup
        dgate_ref[...] = dgate
        dh2 = _dot(dgate, wg_ref[...]) + _dot(dup, wu_ref[...])
        dx1n, dg2 = _rms_bwd(x1, g2_ref[...], dh2)
        dx1 = dx2 + dx1n
        dx1_ref[...] = dx1
        dx1b = dx1.astype(BF16)
        dx1b_ref[...] = dx1b
        oa = oa_ref[...]
        dya_ref[...] = _dot_nt(dx1b, wo_ref[0:A_WIDTH, :]).astype(dya_ref.dtype)
        doa, dgm = _rms_bwd(oa, gm_ref[...], _dot_nt(dx1b, wo_ref[A_WIDTH:, :]))
        doa_ref[...] = doa.astype(doa_ref.dtype)

        @pl.when(pl.program_id(0) == 0)
        def _():
            dgm_ref[...] = jnp.zeros_like(dgm_ref)
            dg2_ref[...] = jnp.zeros_like(dg2_ref)
            dgf_ref[...] = jnp.zeros_like(dgf_ref)

        dgm_ref[...] += dgm
        dg2_ref[...] += dg2
        dgf_ref[...] += dgf

    tok = lambda wd: pl.BlockSpec((tm, wd), lambda i: (i, 0))
    vec = lambda wd: pl.BlockSpec((1, wd), lambda i: (0, 0))
    sds = lambda wd, dt: jax.ShapeDtypeStruct((t, wd), dt)
    return pl.pallas_call(
        body, name="post_bwd", grid=(t // tm,),
        out_shape=[sds(D_MODEL, F32), sds(512, BF16), sds(512, BF16), sds(D_MODEL, BF16), sds(D_MODEL, BF16),
                   sds(D_FF, BF16), sds(D_FF, BF16), sds(D_FF, BF16), sds(D_MODEL, BF16),
                   jax.ShapeDtypeStruct((1, 512), F32), jax.ShapeDtypeStruct((1, D_MODEL), F32), jax.ShapeDtypeStruct((1, D_MODEL), F32)],
        in_specs=[tok(D_MODEL), tok(D_MODEL), tok(D_FF), tok(D_FF), tok(512), tok(D_MODEL), _const_spec((1, 512)),
                  _const_spec((D_MODEL, D_MODEL)), _const_spec((1, D_MODEL)), _const_spec((D_FF, D_MODEL)),
                  _const_spec((D_FF, D_MODEL)), _const_spec((D_FF, D_MODEL)), _const_spec((1, D_MODEL))],
        out_specs=[tok(D_MODEL), tok(512), tok(512), tok(D_MODEL), tok(D_MODEL), tok(D_FF), tok(D_FF), tok(D_FF),
                   tok(D_MODEL), vec(512), vec(D_MODEL), vec(D_MODEL)],
        compiler_params=_params(),
    )(x1, x2, gate_b, up_b, oattn, tgt, g_mla, w_out, g2, w_gate, w_up, w_down, g_fin)


def _matmul_tn(a, b, tn, tt, tag, b_cols=None, k_out=None, exchange=()):
    t, k = a.shape
    c0, n = (0, b.shape[1]) if b_cols is None else b_cols
    k_out = k if k_out is None else k_out
    last = t // tt - 1
    ne = len(exchange)
    n_j = n // tn

    def body(a_ref, b_ref, *rest):
        o_ref, acc_ref = rest[ne], rest[2 * ne + 1]
        if ne:
            start, finish = _exchange_protocol(rest[:ne], rest[ne + 1:2 * ne + 1], [True] * ne, *rest[2 * ne + 2:])
            pl.when((pl.program_id(0) == 0) & (pl.program_id(1) == 0))(start)
        part = _dot_tn(a_ref[...], b_ref[...])

        @pl.when(pl.program_id(1) == 0)
        def _():
            acc_ref[...] = part

        @pl.when(pl.program_id(1) > 0)
        def _():
            acc_ref[...] += part

        @pl.when(pl.program_id(1) == last)
        def _():
            o_ref[...] = acc_ref[0:k_out, :].astype(o_ref.dtype)

        if ne:
            pl.when((pl.program_id(0) == n_j - 1) & (pl.program_id(1) == last))(finish)

    any_spec = pl.BlockSpec(memory_space=pl.ANY)
    out = pl.pallas_call(
        body, name="wgrad_" + tag, grid=(n_j, t // tt),
        out_shape=[jax.ShapeDtypeStruct((k_out, n), BF16)] + _slot_shapes(exchange, [True] * ne),
        in_specs=[pl.BlockSpec((tt, k), lambda j, i: (i, 0)), pl.BlockSpec((tt, tn), lambda j, i: (i, j + c0 // tn))]
        + [any_spec] * ne,
        out_specs=[pl.BlockSpec((k_out, tn), lambda j, i: (0, j))] + [any_spec] * ne,
        scratch_shapes=[pltpu.VMEM((k, tn), F32)] + (_comm_sems(ne) if ne else []),
        compiler_params=_params(),
    )(a, b, *exchange)
    return out if ne else out[0]


def _inproj_qkv_bwd(x, g1, w_in, dx1, pieces, cq, ckv, g_qa, g_kva, w_q, w_kv, tables, dq, dk, dv, seq, tm):
    t = x.shape[0]
    nblk = seq // tm
    last = t // tm - 1
    counts = [len(p) for p in pieces]
    flat = [a for p in pieces for a in p]
    n_flat = len(flat)
    offs = [sum(IN_WIDTHS[:j]) for j in range(len(IN_WIDTHS))]

    def body(x_ref, g_ref, w_ref, dx1_ref, cq_ref, ckv_ref, gq_ref, gk_ref, wq_ref, wkv_ref, c_ref, sa_ref, sb_ref,
             dq_ref, dk_ref, dv_ref, *refs):
        ins = refs[:n_flat]
        (dx_ref, h_ref, dp_ref, dwq_ref, dwkv_ref, dg_ref, dgq_ref, dgk_ref,
         cqn_ref, dqf_ref, ckn_ref, dkv_ref, accq_ref, acckv_ref) = refs[n_flat:]
        cos_t, sin_a, sin_b = c_ref[...], sa_ref[...], sb_ref[...]
        cqn_ref[...] = _rms(cq_ref[...], gq_ref[...]).astype(BF16)
        ckn_ref[...] = _rms(ckv_ref[...], gk_ref[...]).astype(BF16)
        dkr = jnp.zeros((tm, 128), F32)
        for hd in range(B_HEADS):
            lo = hd * QK_PAD
            dqf_ref[:, lo:lo + 128] = (dq_ref[:, lo:lo + 128].astype(F32) * ATTN_SCALE).astype(BF16)
            dq_rope = dq_ref[:, lo + 128:lo + 256].astype(F32) * ATTN_SCALE
            dqf_ref[:, lo + 128:lo + 256] = _rope_t(dq_rope, cos_t, sin_a, sin_b).astype(BF16)
            dkv_ref[:, lo:lo + 128] = dk_ref[:, lo:lo + 128].astype(BF16)
            dkv_ref[:, lo + 128:lo + 256] = dv_ref[:, hd * B_V:(hd + 1) * B_V].astype(BF16)
            dkr = dkr + dk_ref[:, lo + 128:lo + 256]
        dcq, dgq = _rms_bwd(cq_ref[...], gq_ref[...], _dot(dqf_ref[...], wq_ref[...]))
        dckv, dgk = _rms_bwd(ckv_ref[...], gk_ref[...], _dot_nt(dkv_ref[...], wkv_ref[...]))
        dp_ref[:, offs[5]:offs[6]] = dcq.astype(BF16)
        dp_ref[:, offs[6]:offs[7]] = dckv.astype(BF16)
        dp_ref[:, offs[7]:] = _rope_t(dkr, cos_t, sin_a, sin_b).astype(BF16)
        j = 0
        for g, cnt in enumerate(counts):
            acc = ins[j][...].astype(F32)
            for jj in range(1, cnt):
                acc = acc + ins[j + jj][...].astype(F32)
            dp_ref[:, offs[g]:offs[g] + IN_WIDTHS[g]] = acc.astype(BF16)
            j += cnt
        xv = x_ref[...]
        h_ref[...] = _rms(xv, g_ref[...]).astype(BF16)
        dxn, dg = _rms_bwd(xv, g_ref[...], _dot(dp_ref[...], w_ref[...]))
        dx_ref[...] = dx1_ref[...] + dxn

        @pl.when(pl.program_id(0) == 0)
        def _():
            dg_ref[...] = jnp.zeros_like(dg_ref)
            dgq_ref[...] = jnp.zeros_like(dgq_ref)
            dgk_ref[...] = jnp.zeros_like(dgk_ref)
            accq_ref[...] = jnp.zeros_like(accq_ref)
            acckv_ref[...] = jnp.zeros_like(acckv_ref)

        dg_ref[...] += dg
        dgq_ref[...] += dgq
        dgk_ref[...] += dgk
        accq_ref[...] += _dot_tn(dqf_ref[...], cqn_ref[...])
        acckv_ref[...] += _dot_tn(ckn_ref[...], dkv_ref[...])

        @pl.when(pl.program_id(0) == last)
        def _():
            dwq_ref[...] = accq_ref[...].astype(dwq_ref.dtype)
            dwkv_ref[...] = acckv_ref[...].astype(dwkv_ref.dtype)

    tok = lambda wd: pl.BlockSpec((tm, wd), lambda i: (i, 0))
    vec = lambda wd: pl.BlockSpec((1, wd), lambda i: (0, 0))
    whole = lambda r, c: pl.BlockSpec((r, c), lambda i: (0, 0))
    tab = pl.BlockSpec((tm, 128), lambda i: (i % nblk, 0))
    sds = lambda wd, dt: jax.ShapeDtypeStruct((t, wd), dt)
    return pl.pallas_call(
        body, name="inproj_qkv_bwd", grid=(t // tm,),
        out_shape=[sds(D_MODEL, F32), sds(D_MODEL, BF16), sds(D_IN_PAD, BF16),
                   jax.ShapeDtypeStruct((B_HEADS * QK_PAD, Q_LORA), BF16), jax.ShapeDtypeStruct((KV_LORA, 1024), BF16),
                   jax.ShapeDtypeStruct((1, D_MODEL), F32), jax.ShapeDtypeStruct((1, Q_LORA), F32),
                   jax.ShapeDtypeStruct((1, KV_LORA), F32)],
        in_specs=[tok(D_MODEL), _const_spec((1, D_MODEL)), _const_spec((D_IN_PAD, D_MODEL)), tok(D_MODEL), tok(Q_LORA),
                  tok(KV_LORA), _const_spec((1, Q_LORA)), _const_spec((1, KV_LORA)), _const_spec((1024, Q_LORA)),
                  _const_spec((KV_LORA, 1024)), tab, tab, tab, tok(1024), tok(1024), tok(512)] + [tok(512)] * n_flat,
        out_specs=[tok(D_MODEL), tok(D_MODEL), tok(D_IN_PAD), whole(B_HEADS * QK_PAD, Q_LORA), whole(KV_LORA, 1024),
                   vec(D_MODEL), vec(Q_LORA), vec(KV_LORA)],
        scratch_shapes=[pltpu.VMEM((tm, Q_LORA), BF16), pltpu.VMEM((tm, 1024), BF16), pltpu.VMEM((tm, KV_LORA), BF16),
                        pltpu.VMEM((tm, 1024), BF16), pltpu.VMEM((B_HEADS * QK_PAD, Q_LORA), F32),
                        pltpu.VMEM((KV_LORA, 1024), F32)],
        compiler_params=_params(),
    )(x, g1, w_in, dx1, cq, ckv, g_qa, g_kva, w_q, w_kv, *tables, dq, dk, dv, *flat)


def _cols_from_slots(g):
    n, r, cs = g.shape
    return g.transpose(1, 0, 2).reshape(r, n * cs)


def _cols_to_slots(full):
    r, c = full.shape
    return full.reshape(r, N_DEV, c // N_DEV).transpose(1, 0, 2)


def _arrange_w_in_t(w_in_t):
    return jnp.concatenate([w_in_t, jnp.zeros((D_IN_PAD - D_IN, D_MODEL), w_in_t.dtype)], axis=0)


def _arrange_w_q_t(w_q_t):
    q3 = w_q_t.reshape(B_HEADS, B_NOPE + B_ROPE, Q_LORA)
    pad = jnp.zeros((B_HEADS, QK_PAD - B_NOPE - B_ROPE, Q_LORA), w_q_t.dtype)
    return jnp.concatenate([q3, pad], axis=1).reshape(B_HEADS * QK_PAD, Q_LORA)


def _unarrange_w_q_t(d_q_t):
    return d_q_t.reshape(B_HEADS, QK_PAD, Q_LORA)[:, :B_NOPE + B_ROPE].reshape(B_HEADS * (B_NOPE + B_ROPE), Q_LORA)


def _step_core(x, loss_target, small_w, lb_full, early_full, late, seq, group, tiles, distributed):
    g1, g_hgrn, g_qa, g_kva, g_mla, g2, g_fin = small_w
    w_in, w_q, w_kv = _arrange_w_in_t(early_full[0]), _arrange_w_q_t(early_full[1]), early_full[2]
    nb = x.shape[0]
    t = nb * seq
    tm, tm_fwd, tq_f, tq_b, tt = tiles
    xt = x.reshape(t, D_MODEL)
    tgt = loss_target.reshape(t, D_MODEL)
    tables = _rope_tables(seq)

    hq, hi, zf, zb, hg, cq, ckv, qcat, kcat, vv = _inproj_qkv(xt, g1, w_in, g_qa, g_kva, w_q, w_kv, tables, seq, tm_fwd)
    if distributed:
        oattn, lse, *late_slots = _attn_fwd(qcat, kcat, vv, nb, seq, tq_f, gather=tuple(late))
    else:
        oattn, lse = _attn_fwd(qcat, kcat, vv, nb, seq, tq_f)
        late_slots = late
    w_out = late_slots[0].reshape(D_MODEL, D_MODEL)
    w_gate, w_up = late_slots[1].reshape(D_FF, D_MODEL), late_slots[2].reshape(D_FF, D_MODEL)
    w_down = late_slots[3].reshape(D_FF, D_MODEL)
    lbl_f, lbl_b = lb_full[0], lb_full[1]
    o_f, o_b, save_f, save_b = _gla_fwd(hq, hi, (zf, zb), (lbl_f, lbl_b), nb, seq, group)
    x1, x2, ycat_b, gate_b, up_b, loss_row = _post_fwd(
        xt, o_f, o_b, hg, oattn, tgt, g_hgrn, g_mla, w_out, g2, w_gate, w_up, w_down, g_fin, tm_fwd)

    (dx1, d_ya, d_oattn, dx1_b, h2_b, dgate_b, dup_b, act_b, dx2_b, d_g_mla, d_g2, d_g_fin) = _post_bwd(
        x1, x2, gate_b, up_b, oattn, tgt, g_mla, w_out, g2, w_gate, w_up, w_down, g_fin, tm)
    d_w_gate = _matmul_tn(dgate_b, h2_b, 512, tt, "gate")
    d_w_up = _matmul_tn(dup_b, h2_b, 512, tt, "up")
    d_w_down = _matmul_tn(act_b, dx2_b, 512, tt, "down")
    d_w_out = _matmul_tn(ycat_b, dx1_b, D_MODEL, tt, "out")
    late_g = [d_w_out.reshape(N_DEV, D_MODEL // N_DEV, D_MODEL)] + [
        g.reshape(N_DEV, D_FF // N_DEV, D_MODEL) for g in (d_w_gate, d_w_up, d_w_down)]
    if distributed:
        dq, dk, dv, *late_g = _attn_bwd(qcat, kcat, vv, oattn, lse, d_oattn, nb, seq, tq_b, exchange=tuple(late_g))
    else:
        dq, dk, dv = _attn_bwd(qcat, kcat, vv, oattn, lse, d_oattn, nb, seq, tq_b)
    d_o, d_hg, d_g_hgrn = _gla_combine_bwd(o_f, o_b, hg, g_hgrn, d_ya, tm_fwd)
    dq_f, dv_f, dz_f, dq_b, dv_b, dz_b, dl_f, dl_b = _gla_bwd(
        hq, hi, (zf, zb), (lbl_f, lbl_b), (save_f, save_b), d_o, nb, seq, group)
    grad_x, h1_b, dproj_b, d_w_q, d_w_kv, d_g1, d_g_qa, d_g_kva = _inproj_qkv_bwd(
        xt, g1, w_in, dx1, [[dq_f, dq_b], [dv_f, dv_b], [dz_f], [dz_b], [d_hg]], cq, ckv, g_qa, g_kva, w_q, w_kv, tables,
        dq, dk, dv, seq, tm_fwd)
    half = D_MODEL // 2
    in_slots = lambda g: g.reshape(N_DEV, D_IN // N_DEV, half)
    g_in_a = in_slots(_matmul_tn(dproj_b, h1_b, half, tt, "in_a", b_cols=(0, half), k_out=D_IN))
    if distributed:
        d_w_in_b, g_in_a = _matmul_tn(dproj_b, h1_b, half, tt, "in_b", b_cols=(half, half), k_out=D_IN, exchange=(g_in_a,))
    else:
        d_w_in_b = _matmul_tn(dproj_b, h1_b, half, tt, "in_b", b_cols=(half, half), k_out=D_IN)

    early_g = [in_slots(d_w_in_b), _unarrange_w_q_t(d_w_q).reshape(N_DEV, 768 // N_DEV, Q_LORA), _cols_to_slots(d_w_kv)]
    d_lb = jnp.stack([jnp.sum(dl_f, axis=0), jnp.sum(dl_b, axis=0)], axis=0)
    small_grads = [d_g1, d_g_hgrn, d_g_qa, d_g_kva, d_g_mla, d_g2, d_g_fin]
    return loss_row, grad_x.reshape(nb, seq, D_MODEL), g_in_a, early_g, late_g, small_grads, d_lb


def kernel(x, norm1_g, w_in, lb_logits, hgrn_norm_g, q_a_norm_g, w_q_b, kv_a_norm_g, w_kv_b, mla_norm_g, w_out, norm2_g, w_gate, w_up, w_down, final_norm_g, loss_target, m_norm1_g, m_w_in, m_lb_logits, m_hgrn_norm_g, m_q_a_norm_g, m_w_q_b, m_kv_a_norm_g, m_w_kv_b, m_mla_norm_g, m_w_out, m_norm2_g, m_w_gate, m_w_up, m_w_down, m_final_norm_g, v_norm1_g, v_w_in, v_lb_logits, v_hgrn_norm_g, v_q_a_norm_g, v_w_q_b, v_kv_a_norm_g, v_w_kv_b, v_mla_norm_g, v_w_out, v_norm2_g, v_w_gate, v_w_up, v_w_down, v_final_norm_g):
    big_w = [w_in, w_q_b, w_kv_b, w_out, w_gate, w_up, w_down]
    big_m = [m_w_in, m_w_q_b, m_w_kv_b, m_w_out, m_w_gate, m_w_up, m_w_down]
    big_v = [v_w_in, v_w_q_b, v_w_kv_b, v_w_out, v_w_gate, v_w_up, v_w_down]
    small_w = [norm1_g, hgrn_norm_g, q_a_norm_g, kv_a_norm_g, mla_norm_g, norm2_g, final_norm_g]
    small_m = [m_norm1_g, m_hgrn_norm_g, m_q_a_norm_g, m_kv_a_norm_g, m_mla_norm_g, m_norm2_g, m_final_norm_g]
    small_v = [v_norm1_g, v_hgrn_norm_g, v_q_a_norm_g, v_kv_a_norm_g, v_mla_norm_g, v_norm2_g, v_final_norm_g]
    seq = x.shape[1]
    my_id = 4 * lax.axis_index("x") + 2 * lax.axis_index("y") + lax.axis_index("c")

    shard = lambda w: w[0].astype(BF16)
    col_t = lambda w: jnp.swapaxes(w, 1, 2)[0]
    shard_t = lambda w: col_t(w).astype(BF16)
    g_in, g_q, g_kv, g_lb = _all_gather_call([shard_t(w_in), shard_t(w_q_b), shard(w_kv_b), lb_logits.reshape(4, 64)])
    early_full = (g_in.reshape(D_IN, D_MODEL), g_q.reshape(768, Q_LORA), _cols_from_slots(g_kv))
    lb_full = g_lb.reshape(N_DEV, 2, 2, 64).transpose(1, 2, 0, 3).reshape(2, 2, 512)

    as_row = lambda a: a.reshape(1, -1)
    loss_row, grad_x, recv_in_a, early_g, late_recv, small_g, d_lb = _step_core(
        x, loss_target, [as_row(s) for s in small_w], lb_full, early_full,
        [shard(w_out), shard_t(w_gate), shard_t(w_up), shard(w_down)], seq, min(64, seq // CHUNK),
        (256, 512, min(1024, seq), min(1024, seq), min(2048, 2 * seq)), True)

    grads, deltas, new_ms, new_vs = {}, {}, {}, {}
    views = {name: (col_t if name in ("w_in", "w_q_b", "w_gate", "w_up") else (lambda a: a[0])) for name, _, _, _ in BIG}
    backs = {name: ((lambda a: jnp.swapaxes(a[None], 1, 2)) if name in ("w_in", "w_q_b", "w_gate", "w_up") else (lambda a: a[None]))
             for name, _, _, _ in BIG}
    by_name = {name: (w, m, v) for (name, _, _, _), w, m, v in zip(BIG, big_w, big_m, big_v)}
    late_names = ["w_out", "w_gate", "w_up", "w_down"]
    n_small = len(small_g)
    g_l, d_l, nm_l, nv_l, recv = _adamw_recv_hosting(
        [views[n](by_name[n][0]) for n in late_names], list(late_recv), [views[n](by_name[n][1]) for n in late_names],
        [views[n](by_name[n][2]) for n in late_names],
        early_g + small_g + [d_lb.reshape(4, 512), loss_row], [True] * 3 + [False] * (n_small + 2))
    for i, name in enumerate(late_names):
        grads[name], deltas[name], new_ms[name], new_vs[name] = (backs[name](a[i]) for a in (g_l, d_l, nm_l, nv_l))
    sums = _sum_slots_call(recv[3:])
    g_small = [g.reshape(s.shape) for g, s in zip(sums[:n_small], small_w)]
    g_lb_own = lax.dynamic_index_in_dim(sums[n_small].reshape(2, 2, N_DEV, 64), my_id, axis=2, keepdims=False)
    loss = sums[n_small + 1][0, 0]

    for name, r in zip(["w_in", "w_q_b", "w_kv_b"], recv[:3]):
        w, m, v = (views[name](a) for a in by_name[name])
        g, d, nm, nv = _adamw_recv_halves(w, (recv_in_a, r), m, v, name) if name == "w_in" else _adamw_recv(w, r, m, v, name)
        grads[name], deltas[name], new_ms[name], new_vs[name] = (backs[name](a) for a in (g, d, nm, nv))
    lb_rows = lambda a: a.reshape(4, 64)
    d_s, nm_s, nv_s = _adamw_small(
        [as_row(a) for a in small_w] + [lb_rows(lb_logits)], [as_row(a) for a in g_small] + [lb_rows(g_lb_own)],
        [as_row(a) for a in small_m] + [lb_rows(m_lb_logits)], [as_row(a) for a in small_v] + [lb_rows(v_lb_logits)])
    for i, (s, (name, _)) in enumerate(zip(small_w + [lb_logits], SMALL + (("lb_logits", 0),))):
        grads[name] = (g_small + [g_lb_own])[i]
        deltas[name], new_ms[name], new_vs[name] = d_s[i].reshape(s.shape), nm_s[i].reshape(s.shape), nv_s[i].reshape(s.shape)

    order = ["norm1_g", "w_in", "lb_logits", "hgrn_norm_g", "q_a_norm_g", "w_q_b", "kv_a_norm_g", "w_kv_b", "mla_norm_g",
             "w_out", "norm2_g", "w_gate", "w_up", "w_down", "final_norm_g"]
    return (loss, grad_x, *[grads[n] for n in order], *[deltas[n] for n in order],
            *[new_ms[n] for n in order], *[new_vs[n] for n in order])
```

```python
import functools

import jax
import jax.numpy as jnp
from jax import lax
from jax.experimental import pallas as pl
from jax.experimental.pallas import tpu as pltpu

F32 = jnp.float32
BF16 = jnp.bfloat16

N_DEV = 8
D_MODEL = 1024
D_FF = 2816
A_WIDTH = 512
HEAD_PAIR = 128
CHUNK = 64
B_HEADS = 4
B_NOPE = 128
B_ROPE = 64
B_V = 128
QK_PAD = 256
Q_LORA = 384
KV_LORA = 256
D_IN = 3264
D_IN_PAD = 3328
IN_WIDTHS = (512, 512, 512, 512, 512, Q_LORA, KV_LORA, 128)
ROPE_THETA = 10000.0
EPS = 1e-6
ATTN_SCALE = (B_NOPE + B_ROPE) ** -0.5
ATTN_SUB = 256
ATTN_SUB_BWD = 256
ROW_SUB = 256
ADAM_LR, ADAM_B1, ADAM_B2, ADAM_EPS, ADAM_WD, ADAM_STEP = 0.001, 0.9, 0.999, 1e-08, 0.01, 10
VMEM_LIMIT = 60 * 1024 * 1024
MESH = pl.DeviceIdType.MESH

BIG = (("w_in", 1024, D_IN, 1), ("w_q_b", Q_LORA, 768, 1), ("w_kv_b", KV_LORA, 1024, 1), ("w_out", 1024, 1024, 0),
       ("w_gate", 1024, D_FF, 1), ("w_up", 1024, D_FF, 1), ("w_down", D_FF, 1024, 0))
SMALL = (("norm1_g", 1024), ("hgrn_norm_g", 512), ("q_a_norm_g", 384), ("kv_a_norm_g", 256), ("mla_norm_g", 512),
         ("norm2_g", 1024), ("final_norm_g", 1024))


def _params(**kw):
    return pltpu.CompilerParams(vmem_limit_bytes=VMEM_LIMIT, **kw)


def _const_spec(shape):
    return pl.BlockSpec(shape, lambda *_: (0,) * len(shape), pipeline_mode=pl.Buffered(1))


def _dot(a, b):
    return jnp.dot(a, b, preferred_element_type=F32)


def _dot_nt(a, b):
    return lax.dot_general(a, b, (((1,), (1,)), ((), ())), preferred_element_type=F32)


def _dot_tn(a, b):
    return lax.dot_general(a, b, (((0,), (0,)), ((), ())), preferred_element_type=F32)


@jax.custom_vjp
def _mm(a, b):
    return _dot(a.astype(BF16), b.astype(BF16))


def _mm_fwd(a, b):
    return _mm(a, b), (a, b)


def _mm_bwd(res, g):
    a, b = res
    gb = g.astype(BF16)
    return _dot_nt(gb, b.astype(BF16)), _dot_tn(a.astype(BF16), gb)


_mm.defvjp(_mm_fwd, _mm_bwd)


@jax.custom_vjp
def _mm_nt(a, b):
    return _dot_nt(a.astype(BF16), b.astype(BF16))


def _mm_nt_fwd(a, b):
    return _mm_nt(a, b), (a, b)


def _mm_nt_bwd(res, g):
    a, b = res
    gb = g.astype(BF16)
    return _dot(gb, b.astype(BF16)), _dot_tn(gb, a.astype(BF16))


_mm_nt.defvjp(_mm_nt_fwd, _mm_nt_bwd)


@jax.custom_vjp
def _mm_tn(a, b):
    return _dot_tn(a.astype(BF16), b.astype(BF16))


def _mm_tn_fwd(a, b):
    return _mm_tn(a, b), (a, b)


def _mm_tn_bwd(res, g):
    a, b = res
    gb = g.astype(BF16)
    return _dot_nt(b.astype(BF16), gb), _dot(a.astype(BF16), gb)


_mm_tn.defvjp(_mm_tn_fwd, _mm_tn_bwd)


def _dot_exact_rhs(a, m):
    hi = a.astype(BF16)
    lo = (a - hi.astype(F32)).astype(BF16)
    return _dot(hi, m) + _dot(lo, m)


@jax.custom_vjp
def _group_mean(a, m):
    return _dot_exact_rhs(a, m)


def _group_mean_fwd(a, m):
    return _group_mean(a, m), m


def _group_mean_bwd(m, g):
    return _dot_exact_rhs(g, m), jnp.zeros_like(m)


_group_mean.defvjp(_group_mean_fwd, _group_mean_bwd)


def _roll_rows(a, shift):
    return pltpu.roll(a, shift, 0)


def _cumsum_rows_raw(a, reverse):
    n = a.shape[0]
    row = lax.broadcasted_iota(jnp.int32, a.shape, 0)
    s = 1
    while s < n:
        if reverse:
            a = a + jnp.where(row < n - s, _roll_rows(a, n - s), 0.0)
        else:
            a = a + jnp.where(row >= s, _roll_rows(a, s), 0.0)
        s *= 2
    return a


@functools.partial(jax.custom_vjp, nondiff_argnums=(1,))
def _cumsum_rows(a, reverse):
    return _cumsum_rows_raw(a, reverse)


def _cumsum_rows_fwd(a, reverse):
    return _cumsum_rows_raw(a, reverse), None


def _cumsum_rows_bwd(reverse, _, g):
    return (_cumsum_rows_raw(g, not reverse),)


_cumsum_rows.defvjp(_cumsum_rows_fwd, _cumsum_rows_bwd)


def _rms(x, g):
    r = lax.rsqrt(jnp.mean(x * x, axis=-1, keepdims=True) + EPS)
    return x * r * g


def _rms_bwd(x, g, dy):
    r = lax.rsqrt(jnp.mean(x * x, axis=-1, keepdims=True) + EPS)
    xh = x * r
    dg = jnp.sum(dy * xh, axis=0, keepdims=True)
    dxh = dy * g
    dx = r * (dxh - xh * jnp.mean(dxh * xh, axis=-1, keepdims=True))
    return dx, dg


def _sigmoid(a):
    return jax.nn.sigmoid(a)


def _mesh_place():
    x, y, c = lax.axis_index("x"), lax.axis_index("y"), lax.axis_index("c")
    return x, y, c


def _dev_index(p):
    return 4 * p[0] + 2 * p[1] + p[2]


def _comm_sems(n):
    return [pltpu.SemaphoreType.DMA((n, 7)), pltpu.SemaphoreType.DMA((n, 7)), pltpu.SemaphoreType.DMA((n,))]


def _gather_protocol(ins, outs, send_sems, recv_sems, local_sems):
    n = len(ins)
    x, y, c = _mesh_place()
    me, sibling = (x, y, c), (x, y, 1 - c)
    chips = [(1 - x, y), (x, 1 - y), (1 - x, 1 - y)]

    def copy(a, k, block, to, src=None):
        slot = outs[a].at[_dev_index(block)]
        return pltpu.make_async_remote_copy(
            src_ref=slot if src is None else src, dst_ref=slot,
            send_sem=send_sems.at[a, k], recv_sem=recv_sems.at[a, k], device_id=to, device_id_type=MESH)

    def mine(a):
        return pltpu.make_async_copy(ins[a], outs[a].at[_dev_index(me)], local_sems.at[a])

    def first(a):
        return [copy(a, 0, me, sibling, src=ins[a])] + [copy(a, 1 + j, me, (*chip, c), src=ins[a]) for j, chip in enumerate(chips)]

    def start():
        for a in range(n):
            mine(a).start()
            for cp in first(a):
                cp.start()

    def forward():
        for a in range(n):
            for j, chip in enumerate(chips):
                copy(a, 1 + j, (*chip, c), me).wait_recv()
                copy(a, 4 + j, (*chip, c), sibling).start()

    def finish():
        for a in range(n):
            copy(a, 0, sibling, me).wait_recv()
            for j, chip in enumerate(chips):
                copy(a, 4 + j, (*chip, 1 - c), me).wait_recv()
        for a in range(n):
            mine(a).wait()
            for cp in first(a):
                cp.wait_send()
            for j, chip in enumerate(chips):
                copy(a, 4 + j, (*chip, c), sibling).wait_send()

    return start, forward, finish


def _exchange_protocol(ins, outs, scatter, send_sems, recv_sems, local_sems):
    n = len(ins)
    x, y, c = _mesh_place()
    me = (x, y, c)
    my_id = _dev_index(me)
    rels = [(dx, dy, dc) for dx in (0, 1) for dy in (0, 1) for dc in (0, 1)][1:]

    def peer_of(rel):
        return tuple(1 - v if d else v for v, d in zip(me, rel))

    def src(a, dev):
        return ins[a].at[dev] if scatter[a] else ins[a]

    def send(a, k):
        peer = peer_of(rels[k])
        return pltpu.make_async_remote_copy(
            src_ref=src(a, _dev_index(peer)), dst_ref=outs[a].at[my_id],
            send_sem=send_sems.at[a, k], recv_sem=recv_sems.at[a, k], device_id=peer, device_id_type=MESH)

    def arrival(a, k):
        peer = peer_of(rels[k])
        return pltpu.make_async_remote_copy(
            src_ref=src(a, my_id), dst_ref=outs[a].at[_dev_index(peer)],
            send_sem=send_sems.at[a, k], recv_sem=recv_sems.at[a, k], device_id=peer, device_id_type=MESH)

    def own(a):
        return pltpu.make_async_copy(src(a, my_id), outs[a].at[my_id], local_sems.at[a])

    def start():
        for a in range(n):
            own(a).start()
            for k in range(7):
                send(a, k).start()

    def finish():
        for a in range(n):
            for k in range(7):
                arrival(a, k).wait_recv()
        for a in range(n):
            for k in range(7):
                send(a, k).wait_send()
            own(a).wait()

    return start, finish


def _slot_shapes(blocks, scatter=None):
    return [jax.ShapeDtypeStruct(b.shape if (scatter and scatter[a]) else (N_DEV,) + b.shape, b.dtype) for a, b in enumerate(blocks)]


def _all_gather_call(blocks):
    n = len(blocks)

    def body(*refs):
        start, forward, finish = _gather_protocol(refs[:n], refs[n:2 * n], *refs[2 * n:])
        start()
        forward()
        finish()

    any_spec = pl.BlockSpec(memory_space=pl.ANY)
    return pl.pallas_call(
        body, name="weights_all_gather", out_shape=_slot_shapes(blocks),
        in_specs=[any_spec] * n, out_specs=[any_spec] * n, scratch_shapes=_comm_sems(n),
    )(*blocks)


def _sum_slots_call(recvs):
    n = len(recvs)

    def body(*refs):
        for in_ref, out_ref in zip(refs[:n], refs[n:]):
            acc = in_ref[0]
            for j in range(1, N_DEV):
                acc = acc + in_ref[j]
            out_ref[...] = acc

    return pl.pallas_call(
        body, name="small_grad_sum", out_shape=[jax.ShapeDtypeStruct(r.shape[1:], F32) for r in recvs],
        compiler_params=_params(),
    )(*recvs)


def _adam_update(w, g, m, v):
    nm = ADAM_B1 * m + (1.0 - ADAM_B1) * g
    nv = ADAM_B2 * v + (1.0 - ADAM_B2) * (g * g)
    bc1 = 1.0 - ADAM_B1 ** ADAM_STEP
    bc2 = 1.0 - ADAM_B2 ** ADAM_STEP
    return -ADAM_LR * ((nm / bc1) / (jnp.sqrt(nv / bc2) + ADAM_EPS) + ADAM_WD * w), nm, nv


def _adamw_recv(w, recv, m, v, tag):
    r, c = w.shape
    tr = r
    for cand in (512, 256, 128):
        if r > cand and r % cand == 0:
            tr = cand
            break

    def body(w_ref, r_ref, m_ref, v_ref, g_ref, d_ref, nm_ref, nv_ref):
        g = r_ref[0].astype(F32)
        for j in range(1, N_DEV):
            g = g + r_ref[j].astype(F32)
        g_ref[...] = g
        d_ref[...], nm_ref[...], nv_ref[...] = _adam_update(w_ref[...], g, m_ref[...], v_ref[...])

    spec = pl.BlockSpec((tr, c), lambda i: (i, 0))
    return pl.pallas_call(
        body, name="adamw_" + tag, out_shape=[jax.ShapeDtypeStruct(w.shape, F32)] * 4, grid=(r // tr,),
        in_specs=[spec, pl.BlockSpec((N_DEV, tr, c), lambda i: (0, i, 0)), spec, spec], out_specs=[spec] * 4,
        compiler_params=_params(),
    )(w, recv, m, v)


def _adamw_recv_halves(w, recv_halves, m, v, tag):
    r, c = w.shape
    half = c // 2

    def body(w_ref, ra_ref, rb_ref, m_ref, v_ref, g_ref, d_ref, nm_ref, nv_ref):
        def update(r_ref):
            g = r_ref[0].astype(F32)
            for j in range(1, N_DEV):
                g = g + r_ref[j].astype(F32)
            g_ref[...] = g
            d_ref[...], nm_ref[...], nv_ref[...] = _adam_update(w_ref[...], g, m_ref[...], v_ref[...])

        pl.when(pl.program_id(0) == 0)(lambda: update(ra_ref))
        pl.when(pl.program_id(0) == 1)(lambda: update(rb_ref))

    spec = pl.BlockSpec((r, half), lambda j: (0, j))
    whole = pl.BlockSpec((N_DEV, r, half), lambda j: (0, 0, 0))
    return pl.pallas_call(
        body, name="adamw_" + tag, out_shape=[jax.ShapeDtypeStruct(w.shape, F32)] * 4, grid=(2,),
        in_specs=[spec, whole, whole, spec, spec], out_specs=[spec] * 4, compiler_params=_params(),
    )(w, *recv_halves, m, v)


def _adamw_recv_hosting(ws, recvs, ms, vs, blocks, scatter):
    n, ne = len(ws), len(blocks)
    rows = max(w.shape[0] for w in ws)
    cols = ws[0].shape[1]
    assert all(w.shape[1] == cols for w in ws)

    def body(*refs):
        ins, ex_in = refs[:4 * n], refs[4 * n:4 * n + ne]
        outs, ex_out = refs[4 * n + ne:8 * n + ne], refs[8 * n + ne:8 * n + 2 * ne]
        in_buf, recv_buf, out_buf, in_sems, out_sems = refs[8 * n + 2 * ne:8 * n + 2 * ne + 5]
        start, finish = _exchange_protocol(ex_in, ex_out, scatter, *refs[8 * n + 2 * ne + 5:])
        start()
        for a in range(n):
            r = pl.ds(0, ws[a].shape[0])
            loads = [pltpu.make_async_copy(ins[k * n + a], in_buf.at[j, r], in_sems.at[j]) for j, k in enumerate((0, 2, 3))]
            loads.append(pltpu.make_async_copy(ins[n + a], recv_buf.at[:, r], in_sems.at[3]))
            for cp in loads:
                cp.start()
            for cp in loads:
                cp.wait()
            g = recv_buf[0, r].astype(F32)
            for j in range(1, N_DEV):
                g = g + recv_buf[j, r].astype(F32)
            out_buf[0, r] = g
            out_buf[1, r], out_buf[2, r], out_buf[3, r] = _adam_update(in_buf[0, r], g, in_buf[1, r], in_buf[2, r])
            stores = [pltpu.make_async_copy(out_buf.at[k, r], outs[k * n + a], out_sems.at[k]) for k in range(4)]
            for cp in stores:
                cp.start()
            for cp in stores:
                cp.wait()
        finish()

    any_spec = pl.BlockSpec(memory_space=pl.ANY)
    out = pl.pallas_call(
        body, name="adamw_late_and_grad_exchange",
        out_shape=[jax.ShapeDtypeStruct(w.shape, F32) for w in ws] * 4 + _slot_shapes(blocks, scatter),
        in_specs=[any_spec] * (4 * n + ne), out_specs=[any_spec] * (4 * n + ne),
        scratch_shapes=[pltpu.VMEM((3, rows, cols), F32), pltpu.VMEM((N_DEV, rows, cols), BF16), pltpu.VMEM((4, rows, cols), F32),
                        pltpu.SemaphoreType.DMA((4,)), pltpu.SemaphoreType.DMA((4,))] + _comm_sems(ne),
        compiler_params=_params(),
    )(*ws, *recvs, *ms, *vs, *blocks)
    return out[:n], out[n:2 * n], out[2 * n:3 * n], out[3 * n:4 * n], out[4 * n:]


def _adamw_small(ws, gs, ms, vs):
    n = len(ws)

    def body(*refs):
        ins, outs = refs[:4 * n], refs[4 * n:]
        for a in range(n):
            d, nm, nv = _adam_update(ins[a][...], ins[n + a][...], ins[2 * n + a][...], ins[3 * n + a][...])
            outs[a][...], outs[n + a][...], outs[2 * n + a][...] = d, nm, nv

    out = pl.pallas_call(
        body, name="adamw_small", out_shape=[jax.ShapeDtypeStruct(w.shape, F32) for w in ws] * 3, compiler_params=_params(),
    )(*ws, *gs, *ms, *vs)
    return out[:n], out[n:2 * n], out[2 * n:]


def _rope_tables(seq):
    inv = 1.0 / (ROPE_THETA ** (jnp.arange(0, B_ROPE, 2, dtype=F32) / B_ROPE))
    ang = jnp.arange(seq, dtype=F32)[:, None] * inv[None, :]
    cos, sin = jnp.cos(ang), jnp.sin(ang)
    z32, z64 = jnp.zeros_like(cos), jnp.zeros((seq, 64), F32)
    cos_t = jnp.concatenate([cos, cos, z64], axis=1)
    sin_a = jnp.concatenate([-sin, z32, z64], axis=1)
    sin_b = jnp.concatenate([z32, sin, z64], axis=1)
    return cos_t, sin_a, sin_b


def _rope(t, cos_t, sin_a, sin_b):
    return t * cos_t + pltpu.roll(t, 96, 1) * sin_a + pltpu.roll(t, 32, 1) * sin_b


def _rope_t(d, cos_t, sin_a, sin_b):
    return d * cos_t + pltpu.roll(d * sin_a, 32, 1) + pltpu.roll(d * sin_b, 96, 1)


def _inproj_qkv(x, g1, w_in, g_qa, g_kva, w_q, w_kv, tables, seq, tm):
    t = x.shape[0]
    nblk = seq // tm
    n_plain = 7
    offs = [sum(IN_WIDTHS[:j]) for j in range(len(IN_WIDTHS))]

    def body(x_ref, g_ref, w_ref, gq_ref, gk_ref, wq_ref, wkv_ref, c_ref, sa_ref, sb_ref, *outs):
        q_out, k_out, v_out = outs[n_plain:]
        for j in range(tm // min(tm, ROW_SUB)):
            r = pl.ds(j * min(tm, ROW_SUB), min(tm, ROW_SUB))
            h = _rms(x_ref[r, :], g_ref[...]).astype(BF16)
            proj = lambda g: _dot_nt(h, w_ref[offs[g]:offs[g] + IN_WIDTHS[g], :])
            for g in range(5):
                outs[g][r, :] = proj(g)
            cq, ckv, kr = proj(5), proj(6), _dot_nt(h, _rotary_key_rows(w_ref))
            outs[5][r, :] = cq
            outs[6][r, :] = ckv
            cos_t, sin_a, sin_b = c_ref[r, :], sa_ref[r, :], sb_ref[r, :]
            cqn = _rms(cq, gq_ref[...]).astype(BF16)
            ckn = _rms(ckv, gk_ref[...]).astype(BF16)
            kr_rot = _rope(kr, cos_t, sin_a, sin_b).astype(BF16)
            for hd in range(B_HEADS):
                lo = hd * QK_PAD
                q_out[r, lo:lo + 128] = (_dot_nt(cqn, wq_ref[lo:lo + 128, :]) * ATTN_SCALE).astype(BF16)
                qr = _rope(_dot_nt(cqn, wq_ref[lo + 128:lo + 256, :]), cos_t, sin_a, sin_b)
                q_out[r, lo + 128:lo + 256] = (qr * ATTN_SCALE).astype(BF16)
                k_out[r, lo:lo + 128] = _dot(ckn, wkv_ref[:, lo:lo + 128]).astype(BF16)
                k_out[r, lo + 128:lo + 256] = kr_rot
                v_out[r, hd * B_V:(hd + 1) * B_V] = _dot(ckn, wkv_ref[:, lo + 128:lo + 256]).astype(BF16)

    tok = lambda wd: pl.BlockSpec((tm, wd), lambda i: (i, 0))
    tab = pl.BlockSpec((tm, 128), lambda i: (i % nblk, 0))
    widths = list(IN_WIDTHS[:n_plain]) + [B_HEADS * QK_PAD, B_HEADS * QK_PAD, B_HEADS * B_V]
    dtypes = [F32] * n_plain + [BF16] * 3
    return pl.pallas_call(
        body, name="inproj_qkv_fwd", grid=(t // tm,),
        out_shape=[jax.ShapeDtypeStruct((t, wd), dt) for wd, dt in zip(widths, dtypes)],
        in_specs=[tok(D_MODEL), _const_spec((1, D_MODEL)), _const_spec((D_IN, D_MODEL)), _const_spec((1, Q_LORA)),
                  _const_spec((1, KV_LORA)), _const_spec((B_HEADS * QK_PAD, Q_LORA)), _const_spec((KV_LORA, 1024)), tab, tab, tab],
        out_specs=[tok(wd) for wd in widths],
        compiler_params=_params(),
    )(x, g1, w_in, g_qa, g_kva, w_q, w_kv, *tables)


def _step_index(nq):
    return (pl.program_id(0) * B_HEADS + pl.program_id(1)) * nq + pl.program_id(2)


def _attn_fwd(qcat, kcat, v, nb, seq, tq, gather=()):
    t = qcat.shape[0]
    nq = seq // tq
    ng = len(gather)
    steps = nb * B_HEADS * nq

    def body(q_ref, k_ref, v_ref, *rest):
        o_ref, lse_ref = rest[ng:ng + 2]
        if ng:
            start, forward, finish = _gather_protocol(rest[:ng], rest[ng + 2:2 * ng + 2], *rest[2 * ng + 2:])
            pl.when(_step_index(nq) == 0)(start)
            pl.when(_step_index(nq) == (3 * steps) // 4)(forward)
        for j in range(tq // ATTN_SUB):
            r = pl.ds(j * ATTN_SUB, ATTN_SUB)
            s = _dot_nt(q_ref[r, :], k_ref[...])
            m = jnp.max(s, axis=-1, keepdims=True)
            p = jnp.exp(s - m)
            l = jnp.sum(p, axis=-1, keepdims=True)
            o_ref[r, :] = _dot(p.astype(BF16), v_ref[...]) / l
            lse_ref[0, r, :] = m + jnp.log(l)
        if ng:
            pl.when(_step_index(nq) == steps - 1)(finish)

    any_spec = pl.BlockSpec(memory_space=pl.ANY)
    return pl.pallas_call(
        body, name="attn_fwd", grid=(nb, B_HEADS, nq),
        out_shape=[jax.ShapeDtypeStruct((t, B_HEADS * B_V), F32), jax.ShapeDtypeStruct((B_HEADS, t, 1), F32)] + _slot_shapes(gather),
        in_specs=[pl.BlockSpec((tq, QK_PAD), lambda b, h, i: (b * nq + i, h)),
                  pl.BlockSpec((seq, QK_PAD), lambda b, h, i: (b, h)),
                  pl.BlockSpec((seq, B_V), lambda b, h, i: (b, h))] + [any_spec] * ng,
        out_specs=[pl.BlockSpec((tq, B_V), lambda b, h, i: (b * nq + i, h)),
                   pl.BlockSpec((1, tq, 1), lambda b, h, i: (h, b * nq + i, 0))] + [any_spec] * ng,
        scratch_shapes=_comm_sems(ng) if ng else [],
        compiler_params=_params(),
    )(qcat, kcat, v, *gather)


def _attn_bwd(qcat, kcat, v, o, lse, do, nb, seq, tq, exchange=()):
    t = qcat.shape[0]
    nq = seq // tq
    ne = len(exchange)
    steps = nb * B_HEADS * nq

    def body(q_ref, k_ref, v_ref, o_ref, lse_ref, do_ref, *rest):
        dq_ref, dk_ref, dv_ref = rest[ne:ne + 3]
        p_ref, ds_ref = rest[2 * ne + 3:2 * ne + 5]
        if ne:
            start, finish = _exchange_protocol(rest[:ne], rest[ne + 3:2 * ne + 3], [True] * ne, *rest[2 * ne + 5:])
            pl.when(_step_index(nq) == 0)(start)

        @pl.when(pl.program_id(2) == 0)
        def _():
            dv_ref[...] = jnp.zeros_like(dv_ref)
            dk_ref[...] = jnp.zeros_like(dk_ref)

        for j in range(tq // ATTN_SUB_BWD):
            r = pl.ds(j * ATTN_SUB_BWD, ATTN_SUB_BWD)
            q, k = q_ref[r, :], k_ref[...]
            do_f = do_ref[r, :].astype(F32)
            delta = jnp.sum(do_f * o_ref[r, :], axis=-1, keepdims=True)
            dob = do_f.astype(BF16)
            p = jnp.exp(_dot_nt(q, k) - lse_ref[0, r, :])
            ds = (p * (_dot_nt(dob, v_ref[...]) - delta)).astype(BF16)
            dq_ref[r, :] = _dot(ds, k).astype(dq_ref.dtype)
            p_ref[r, :] = p.astype(BF16)
            ds_ref[r, :] = ds
        dv_ref[...] += _dot_tn(p_ref[...], do_ref[...].astype(BF16))
        dk_ref[...] += _dot_tn(ds_ref[...], q_ref[...])
        if ne:
            pl.when(_step_index(nq) == steps - 1)(finish)

    qspec = lambda wd: pl.BlockSpec((tq, wd), lambda b, h, i: (b * nq + i, h))
    kspec = lambda wd: pl.BlockSpec((seq, wd), lambda b, h, i: (b, h))
    any_spec = pl.BlockSpec(memory_space=pl.ANY)
    return pl.pallas_call(
        body, name="attn_bwd", grid=(nb, B_HEADS, nq),
        out_shape=[jax.ShapeDtypeStruct((t, B_HEADS * QK_PAD), BF16), jax.ShapeDtypeStruct((t, B_HEADS * QK_PAD), F32),
                   jax.ShapeDtypeStruct((t, B_HEADS * B_V), F32)] + _slot_shapes(exchange, [True] * ne),
        in_specs=[qspec(QK_PAD), kspec(QK_PAD), kspec(B_V), qspec(B_V),
                  pl.BlockSpec((1, tq, 1), lambda b, h, i: (h, b * nq + i, 0)), qspec(B_V)] + [any_spec] * ne,
        out_specs=[qspec(QK_PAD), kspec(QK_PAD), kspec(B_V)] + [any_spec] * ne,
        scratch_shapes=[pltpu.VMEM((tq, seq), BF16), pltpu.VMEM((tq, seq), BF16)] + (_comm_sems(ne) if ne else []),
        compiler_params=_params(),
    )(qcat, kcat, v, o, lse, do, *exchange)


def _gla_consts(reverse):
    row = lax.broadcasted_iota(jnp.int32, (CHUNK, CHUNK), 0)
    col = lax.broadcasted_iota(jnp.int32, (CHUNK, CHUNK), 1)
    causal = (row <= col) if reverse else (row >= col)
    lane = lax.broadcasted_iota(jnp.int32, (1, HEAD_PAIR), 1)
    m0 = (lane < 64).astype(F32)
    m1 = 1.0 - m0
    r2 = lax.broadcasted_iota(jnp.int32, (HEAD_PAIR, HEAD_PAIR), 0)
    c2 = lax.broadcasted_iota(jnp.int32, (HEAD_PAIR, HEAD_PAIR), 1)
    same_head = ((r2 < 64) == (c2 < 64)).astype(F32)
    return causal, m0, m1, same_head


def _gla_chunk(hq, hi, z, l0, l1, st, consts, reverse):
    q_dec, k_inv, k_end, decay = _gla_gates(hq, z, l0, l1, reverse)
    o, st_new = _gla_state(q_dec, st, decay, _gla_increment(hi, k_end, consts))
    return o + _gla_intra(q_dec, k_inv, hi, consts), st_new


def _gla_gates(hq, z, l0, l1, reverse):
    mx = jnp.maximum(l0, l1)
    e0, e1 = jnp.exp(l0 - mx), jnp.exp(l1 - mx)
    lb = e0 / (e0 + e1)
    q = hq * _sigmoid(hq)
    sz = _sigmoid(z)
    log_f = jnp.log(lb + (1.0 - lb) * sz)
    k = (1.0 - lb) * (1.0 - sz)
    cum = _cumsum_rows(log_f, reverse)
    decay = jnp.exp(jnp.sum(log_f, axis=0, keepdims=True))
    k_inv = k * jnp.exp(-cum)
    return q * jnp.exp(cum), k_inv, k_inv * decay, decay


def _gla_intra(q_dec, k_inv, hi, consts):
    causal, m0, m1, _ = consts
    o = None
    for mh in (m0, m1):
        s = jnp.where(causal, _mm_nt(q_dec * mh, k_inv), 0.0)
        part = _mm(s, hi) * mh
        o = part if o is None else o + part
    return o


def _gla_increment(hi, k_end, consts):
    return _mm_tn(hi, k_end) * consts[3]


def _gla_state(q_dec, st, decay, inc):
    return _mm_nt(q_dec, st), st * decay + inc


GLA_DIRS = (False, True)
GLA_BATCH_FWD = 8
GLA_BATCH_BWD = 4


def _gla_fwd(hq, hi, zs, lbls, nb, seq, group):
    t = hq.shape[0]
    rows = group * CHUNK
    nblk = seq // rows
    n_chunks = seq // CHUNK
    nd = len(GLA_DIRS)

    def body(*refs):
        ins, outs, st_refs = refs[:4 * nd], refs[4 * nd:6 * nd], refs[6 * nd:]
        @pl.when(pl.program_id(2) == 0)
        def _():
            for st_ref in st_refs:
                st_ref[...] = jnp.zeros_like(st_ref)

        consts = [_gla_consts(rev) for rev in GLA_DIRS]
        work = [(d, rev, group - 1 - cc if rev else cc) for cc in range(group) for d, rev in enumerate(GLA_DIRS)]
        rows_of = lambda c: pl.ds(c * CHUNK, CHUNK)
        sts = [st_ref[...] for st_ref in st_refs]
        for w0 in range(0, len(work), GLA_BATCH_FWD):
            batch = work[w0:w0 + GLA_BATCH_FWD]
            gates, intra, incs = {}, {}, {}
            for d, rev, c in batch:
                hq_ref, _, z_ref, lbl_ref = ins[4 * d:4 * d + 4]
                gates[d, c] = _gla_gates(hq_ref[rows_of(c), :], z_ref[rows_of(c), :], lbl_ref[0:1, :], lbl_ref[1:2, :], rev)
            for d, rev, c in batch:
                hi_c = ins[4 * d + 1][rows_of(c), :]
                intra[d, c] = _gla_intra(gates[d, c][0], gates[d, c][1], hi_c, consts[d])
                incs[d, c] = _gla_increment(hi_c, gates[d, c][2], consts[d])
            for d, rev, c in batch:
                outs[nd + d][0, 0, c] = sts[d].astype(outs[nd + d].dtype)
                o_state, sts[d] = _gla_state(gates[d, c][0], sts[d], gates[d, c][3], incs[d, c])
                outs[d][rows_of(c), :] = (intra[d, c] + o_state).astype(outs[d].dtype)
        for st_ref, st in zip(st_refs, sts):
            st_ref[...] = st

    def tb(rev):
        return (lambda i: nblk - 1 - i) if rev else (lambda i: i)

    tok = lambda rev: pl.BlockSpec((rows, HEAD_PAIR), lambda b, p, i: (b * nblk + tb(rev)(i), p))
    lspec = pl.BlockSpec((2, HEAD_PAIR), lambda b, p, i: (0, p))
    sspec = lambda rev: pl.BlockSpec((1, 1, group, HEAD_PAIR, HEAD_PAIR), lambda b, p, i: (b, p, tb(rev)(i), 0, 0))
    args, in_specs = [], []
    for d, rev in enumerate(GLA_DIRS):
        args += [hq, hi, zs[d], lbls[d]]
        in_specs += [tok(rev), tok(rev), tok(rev), lspec]
    return pl.pallas_call(
        body, name="gla_fwd", grid=(nb, 4, nblk),
        out_shape=[jax.ShapeDtypeStruct((t, A_WIDTH), BF16)] * nd
        + [jax.ShapeDtypeStruct((nb, 4, n_chunks, HEAD_PAIR, HEAD_PAIR), BF16)] * nd,
        in_specs=in_specs, out_specs=[tok(rev) for rev in GLA_DIRS] + [sspec(rev) for rev in GLA_DIRS],
        scratch_shapes=[pltpu.VMEM((HEAD_PAIR, HEAD_PAIR), F32)] * nd,
        compiler_params=_params(),
    )(*args)


def _gla_bwd(hq, hi, zs, lbls, saved, do, nb, seq, group):
    t = hq.shape[0]
    rows = group * CHUNK
    nblk = seq // rows
    nd = len(GLA_DIRS)

    def body(*refs):
        ins, outs, dst_refs = refs[:6 * nd], refs[6 * nd:10 * nd], refs[10 * nd:]
        dl_refs = outs[3 * nd:]

        @pl.when(pl.program_id(2) == 0)
        def _():
            for dst_ref, dl_ref in zip(dst_refs, dl_refs):
                dst_ref[...] = jnp.zeros_like(dst_ref)
                dl_ref[...] = jnp.zeros_like(dl_ref)

        consts = [_gla_consts(rev) for rev in GLA_DIRS]
        dsts = [dst_ref[...] for dst_ref in dst_refs]
        dls = [[jnp.zeros((1, HEAD_PAIR), F32), jnp.zeros((1, HEAD_PAIR), F32)] for _ in GLA_DIRS]
        work = [(d, rev, cc if rev else group - 1 - cc) for cc in range(group) for d, rev in enumerate(GLA_DIRS)]
        for w0 in range(0, len(work), GLA_BATCH_BWD):
            vjps = {}
            for d, rev, c in work[w0:w0 + GLA_BATCH_BWD]:
                hq_ref, hi_ref, z_ref, lbl_ref, save_ref, _ = ins[6 * d:6 * d + 6]
                r = pl.ds(c * CHUNK, CHUNK)
                fn = functools.partial(_gla_chunk, consts=consts[d], reverse=rev)
                _, vjps[d, c] = jax.vjp(fn, hq_ref[r, :], hi_ref[r, :], z_ref[r, :], lbl_ref[0:1, :], lbl_ref[1:2, :],
                                         save_ref[0, 0, c].astype(F32))
            for d, rev, c in work[w0:w0 + GLA_BATCH_BWD]:
                dq_ref, dv_ref, dz_ref = outs[3 * d:3 * d + 3]
                r = pl.ds(c * CHUNK, CHUNK)
                d_hq, d_hi, d_z, d_l0, d_l1, dsts[d] = vjps[d, c]((ins[6 * d + 5][r, :].astype(F32), dsts[d]))
                dq_ref[r, :] = d_hq.astype(dq_ref.dtype)
                dv_ref[r, :] = d_hi.astype(dv_ref.dtype)
                dz_ref[r, :] = d_z.astype(dz_ref.dtype)
                dls[d] = [dls[d][0] + d_l0, dls[d][1] + d_l1]
        for d in range(nd):
            dst_refs[d][...] = dsts[d]
            dl_refs[d][0, 0:1, :] += dls[d][0]
            dl_refs[d][0, 1:2, :] += dls[d][1]

    def tb(rev):
        return (lambda i: i) if rev else (lambda i: nblk - 1 - i)

    tok = lambda rev: pl.BlockSpec((rows, HEAD_PAIR), lambda b, p, i: (b * nblk + tb(rev)(i), p))
    lspec = pl.BlockSpec((2, HEAD_PAIR), lambda b, p, i: (0, p))
    sspec = lambda rev: pl.BlockSpec((1, 1, group, HEAD_PAIR, HEAD_PAIR), lambda b, p, i: (b, p, tb(rev)(i), 0, 0))
    args, in_specs, out_specs = [], [], []
    for d, rev in enumerate(GLA_DIRS):
        args += [hq, hi, zs[d], lbls[d], saved[d], do]
        in_specs += [tok(rev), tok(rev), tok(rev), lspec, sspec(rev), tok(rev)]
        out_specs += [tok(rev)] * 3
    out_specs += [pl.BlockSpec((1, 2, HEAD_PAIR), lambda b, p, i: (b, 0, p))] * nd
    return pl.pallas_call(
        body, name="gla_bwd", grid=(nb, 4, nblk),
        out_shape=[jax.ShapeDtypeStruct((t, A_WIDTH), BF16)] * (3 * nd) + [jax.ShapeDtypeStruct((nb, 2, A_WIDTH), F32)] * nd,
        in_specs=in_specs, out_specs=out_specs,
        scratch_shapes=[pltpu.VMEM((HEAD_PAIR, HEAD_PAIR), F32)] * nd,
        compiler_params=_params(),
    )(*args)


def _head_mean_matrix():
    r = lax.broadcasted_iota(jnp.int32, (A_WIDTH, A_WIDTH), 0) // 64
    c = lax.broadcasted_iota(jnp.int32, (A_WIDTH, A_WIDTH), 1) // 64
    return jnp.where(r == c, 1.0 / 64.0, 0.0).astype(BF16)


def _gla_out(o_f, o_b, hg, g, mean_mat):
    o = o_f + o_b
    ms = _group_mean(o * o, mean_mat)
    return o * lax.rsqrt(ms + EPS) * g * (hg * _sigmoid(hg))


def _gla_combine_bwd(o_f, o_b, hg, g, dy, tm):
    t = o_f.shape[0]

    def body(of_ref, ob_ref, hg_ref, g_ref, dy_ref, do_ref, dhg_ref, dg_ref):
        mean_mat = _head_mean_matrix()
        fn = lambda o, hgv, gv: _gla_out(o, jnp.zeros_like(o), hgv, gv, mean_mat)
        _, vjp = jax.vjp(fn, of_ref[...].astype(F32) + ob_ref[...].astype(F32), hg_ref[...], g_ref[...])
        d_o, d_hg, d_g = vjp(dy_ref[...].astype(F32))
        do_ref[...] = d_o.astype(do_ref.dtype)
        dhg_ref[...] = d_hg.astype(dhg_ref.dtype)

        @pl.when(pl.program_id(0) == 0)
        def _():
            dg_ref[...] = jnp.zeros_like(dg_ref)

        dg_ref[...] += d_g

    tok = pl.BlockSpec((tm, A_WIDTH), lambda i: (i, 0))
    vec = pl.BlockSpec((1, A_WIDTH), lambda i: (0, 0))
    return pl.pallas_call(
        body, name="gla_combine_bwd", grid=(t // tm,),
        out_shape=[jax.ShapeDtypeStruct((t, A_WIDTH), BF16), jax.ShapeDtypeStruct((t, A_WIDTH), BF16),
                   jax.ShapeDtypeStruct((1, A_WIDTH), F32)],
        in_specs=[tok, tok, tok, _const_spec((1, A_WIDTH)), tok], out_specs=[tok, tok, vec], compiler_params=_params(),
    )(o_f, o_b, hg, g, dy)


def _post_fwd(x, o_f, o_b, hg, oattn, tgt, g_hgrn, g_mla, w_out, g2, w_gate, w_up, w_down, g_fin, tm):
    t = x.shape[0]

    def body(x_ref, of_ref, ob_ref, hg_ref, oa_ref, tgt_ref, gh_ref, gm_ref, wo_ref, g2_ref, wg_ref, wu_ref, wd_ref, gf_ref,
             x1_ref, x2_ref, ycat_ref, gate_ref, up_ref, loss_ref):
        part = jnp.zeros((1, 1), F32)
        mean_mat = _head_mean_matrix()
        for j in range(tm // min(tm, ROW_SUB)):
            r = pl.ds(j * min(tm, ROW_SUB), min(tm, ROW_SUB))
            ya = _gla_out(of_ref[r, :].astype(F32), ob_ref[r, :].astype(F32), hg_ref[r, :], gh_ref[...], mean_mat).astype(BF16)
            yb = _rms(oa_ref[r, :], gm_ref[...]).astype(BF16)
            ycat_ref[r, 0:A_WIDTH] = ya
            ycat_ref[r, A_WIDTH:] = yb
            x1 = x_ref[r, :] + _dot(ya, wo_ref[0:A_WIDTH, :]) + _dot(yb, wo_ref[A_WIDTH:, :])
            x1_ref[r, :] = x1
            h2 = _rms(x1, g2_ref[...]).astype(BF16)
            gate, up = _dot_nt(h2, wg_ref[...]), _dot_nt(h2, wu_ref[...])
            gate_ref[r, :] = gate.astype(BF16)
            up_ref[r, :] = up.astype(BF16)
            act = (gate * _sigmoid(gate) * up).astype(BF16)
            x2 = x1 + _dot(act, wd_ref[...])
            x2_ref[r, :] = x2
            err = _rms(x2, gf_ref[...]) - tgt_ref[r, :]
            part = part + 0.5 * jnp.sum(jnp.mean(err * err, axis=-1, keepdims=True), axis=0, keepdims=True)

        @pl.when(pl.program_id(0) == 0)
        def _():
            loss_ref[...] = jnp.zeros_like(loss_ref)

        loss_ref[...] += jnp.broadcast_to(part, loss_ref.shape)

    tok = lambda wd: pl.BlockSpec((tm, wd), lambda i: (i, 0))
    return pl.pallas_call(
        body, name="post_fwd", grid=(t // tm,),
        out_shape=[jax.ShapeDtypeStruct((t, D_MODEL), F32)] * 2 + [jax.ShapeDtypeStruct((t, D_MODEL), BF16)]
        + [jax.ShapeDtypeStruct((t, D_FF), BF16)] * 2 + [jax.ShapeDtypeStruct((1, 128), F32)],
        in_specs=[tok(D_MODEL), tok(A_WIDTH), tok(A_WIDTH), tok(A_WIDTH), tok(512), tok(D_MODEL), _const_spec((1, A_WIDTH)),
                  _const_spec((1, 512)), _const_spec((D_MODEL, D_MODEL)),
                  _const_spec((1, D_MODEL)), _const_spec((D_FF, D_MODEL)), _const_spec((D_FF, D_MODEL)),
                  _const_spec((D_FF, D_MODEL)), _const_spec((1, D_MODEL))],
        out_specs=[tok(D_MODEL), tok(D_MODEL), tok(D_MODEL), tok(D_FF), tok(D_FF), pl.BlockSpec((1, 128), lambda i: (0, 0))],
        compiler_params=_params(),
    )(x, o_f, o_b, hg, oattn, tgt, g_hgrn, g_mla, w_out, g2, w_gate, w_up, w_down, g_fin)


def _post_bwd(x1, x2, gate_b, up_b, oattn, tgt, g_mla, w_out, g2, w_gate, w_up, w_down, g_fin, tm):
    t = x1.shape[0]

    def body(x1_ref, x2_ref, gate_ref, up_ref, oa_ref, tgt_ref, gm_ref, wo_ref, g2_ref, wg_ref, wu_ref, wd_ref, gf_ref,
             dx1_ref, dya_ref, doa_ref, dx1b_ref, h2_ref, dgate_ref, dup_ref, act_ref, dx2b_ref,
             dgm_ref, dg2_ref, dgf_ref):
        x1, x2 = x1_ref[...], x2_ref[...]
        dy = (_rms(x2, gf_ref[...]) - tgt_ref[...]) * (1.0 / D_MODEL)
        dx2, dgf = _rms_bwd(x2, gf_ref[...], dy)
        dx2b = dx2.astype(BF16)
        dx2b_ref[...] = dx2b
        h2_ref[...] = _rms(x1, g2_ref[...]).astype(BF16)
        gate, up = gate_ref[...].astype(F32), up_ref[...].astype(F32)
        sg = _sigmoid(gate)
        sl = gate * sg
        act_ref[...] = (sl * up).astype(BF16)
        dact = _dot_nt(dx2b, wd_ref[...])
        dup = (dact * sl).astype(BF16)
        dgate = (dact * up * (sg * (1.0 + gate * (1.0 - sg)))).astype(BF16)
        dup_ref[...] = dup
        dgate_ref[...] = dgate
        dh2 = _dot(dgate, wg_ref[...]) + _dot(dup, wu_ref[...])
        dx1n, dg2 = _rms_bwd(x1, g2_ref[...], dh2)
        dx1 = dx2 + dx1n
        dx1_ref[...] = dx1
        dx1b = dx1.astype(BF16)
        dx1b_ref[...] = dx1b
        oa = oa_ref[...]
        dya_ref[...] = _dot_nt(dx1b, wo_ref[0:A_WIDTH, :]).astype(dya_ref.dtype)
        doa, dgm = _rms_bwd(oa, gm_ref[...], _dot_nt(dx1b, wo_ref[A_WIDTH:, :]))
        doa_ref[...] = doa.astype(doa_ref.dtype)

        @pl.when(pl.program_id(0) == 0)
        def _():
            dgm_ref[...] = jnp.zeros_like(dgm_ref)
            dg2_ref[...] = jnp.zeros_like(dg2_ref)
            dgf_ref[...] = jnp.zeros_like(dgf_ref)

        dgm_ref[...] += dgm
        dg2_ref[...] += dg2
        dgf_ref[...] += dgf

    tok = lambda wd: pl.BlockSpec((tm, wd), lambda i: (i, 0))
    vec = lambda wd: pl.BlockSpec((1, wd), lambda i: (0, 0))
    sds = lambda wd, dt: jax.ShapeDtypeStruct((t, wd), dt)
    return pl.pallas_call(
        body, name="post_bwd", grid=(t // tm,),
        out_shape=[sds(D_MODEL, F32), sds(512, BF16), sds(512, BF16), sds(D_MODEL, BF16), sds(D_MODEL, BF16),
                   sds(D_FF, BF16), sds(D_FF, BF16), sds(D_FF, BF16), sds(D_MODEL, BF16),
                   jax.ShapeDtypeStruct((1, 512), F32), jax.ShapeDtypeStruct((1, D_MODEL), F32), jax.ShapeDtypeStruct((1, D_MODEL), F32)],
        in_specs=[tok(D_MODEL), tok(D_MODEL), tok(D_FF), tok(D_FF), tok(512), tok(D_MODEL), _const_spec((1, 512)),
                  _const_spec((D_MODEL, D_MODEL)), _const_spec((1, D_MODEL)), _const_spec((D_FF, D_MODEL)),
                  _const_spec((D_FF, D_MODEL)), _const_spec((D_FF, D_MODEL)), _const_spec((1, D_MODEL))],
        out_specs=[tok(D_MODEL), tok(512), tok(512), tok(D_MODEL), tok(D_MODEL), tok(D_FF), tok(D_FF), tok(D_FF),
                   tok(D_MODEL), vec(512), vec(D_MODEL), vec(D_MODEL)],
        compiler_params=_params(),
    )(x1, x2, gate_b, up_b, oattn, tgt, g_mla, w_out, g2, w_gate, w_up, w_down, g_fin)


def _matmul_tn(a, b, tn, tt, tag, b_cols=None, k_out=None, exchange=()):
    t, k = a.shape
    c0, n = (0, b.shape[1]) if b_cols is None else b_cols
    k_out = k if k_out is None else k_out
    last = t // tt - 1
    ne = len(exchange)
    n_j = n // tn

    def body(a_ref, b_ref, *rest):
        o_ref, acc_ref = rest[ne], rest[2 * ne + 1]
        if ne:
            start, finish = _exchange_protocol(rest[:ne], rest[ne + 1:2 * ne + 1], [True] * ne, *rest[2 * ne + 2:])
            pl.when((pl.program_id(0) == 0) & (pl.program_id(1) == 0))(start)
        part = _dot_tn(a_ref[...], b_ref[...])

        @pl.when(pl.program_id(1) == 0)
        def _():
            acc_ref[...] = part

        @pl.when(pl.program_id(1) > 0)
        def _():
            acc_ref[...] += part

        @pl.when(pl.program_id(1) == last)
        def _():
            o_ref[...] = acc_ref[0:k_out, :].astype(o_ref.dtype)

        if ne:
            pl.when((pl.program_id(0) == n_j - 1) & (pl.program_id(1) == last))(finish)

    any_spec = pl.BlockSpec(memory_space=pl.ANY)
    out = pl.pallas_call(
        body, name="wgrad_" + tag, grid=(n_j, t // tt),
        out_shape=[jax.ShapeDtypeStruct((k_out, n), BF16)] + _slot_shapes(exchange, [True] * ne),
        in_specs=[pl.BlockSpec((tt, k), lambda j, i: (i, 0)), pl.BlockSpec((tt, tn), lambda j, i: (i, j + c0 // tn))]
        + [any_spec] * ne,
        out_specs=[pl.BlockSpec((k_out, tn), lambda j, i: (0, j))] + [any_spec] * ne,
        scratch_shapes=[pltpu.VMEM((k, tn), F32)] + (_comm_sems(ne) if ne else []),
        compiler_params=_params(),
    )(a, b, *exchange)
    return out if ne else out[0]


def _inproj_qkv_bwd(x, g1, w_in, dx1, pieces, cq, ckv, g_qa, g_kva, w_q, w_kv, tables, dq, dk, dv, seq, tm):
    t = x.shape[0]
    nblk = seq // tm
    last = t // tm - 1
    counts = [len(p) for p in pieces]
    flat = [a for p in pieces for a in p]
    n_flat = len(flat)
    offs = [sum(IN_WIDTHS[:j]) for j in range(len(IN_WIDTHS))]

    def body(x_ref, g_ref, w_ref, dx1_ref, cq_ref, ckv_ref, gq_ref, gk_ref, wq_ref, wkv_ref, c_ref, sa_ref, sb_ref,
             dq_ref, dk_ref, dv_ref, *refs):
        ins = refs[:n_flat]
        (dx_ref, h_ref, dp_ref, dwq_ref, dwkv_ref, dg_ref, dgq_ref, dgk_ref,
         cqn_ref, dqf_ref, ckn_ref, dkv_ref, accq_ref, acckv_ref) = refs[n_flat:]
        cos_t, sin_a, sin_b = c_ref[...], sa_ref[...], sb_ref[...]
        cqn_ref[...] = _rms(cq_ref[...], gq_ref[...]).astype(BF16)
        ckn_ref[...] = _rms(ckv_ref[...], gk_ref[...]).astype(BF16)
        dkr = jnp.zeros((tm, 128), F32)
        for hd in range(B_HEADS):
            lo = hd * QK_PAD
            dqf_ref[:, lo:lo + 128] = (dq_ref[:, lo:lo + 128].astype(F32) * ATTN_SCALE).astype(BF16)
            dq_rope = dq_ref[:, lo + 128:lo + 256].astype(F32) * ATTN_SCALE
            dqf_ref[:, lo + 128:lo + 256] = _rope_t(dq_rope, cos_t, sin_a, sin_b).astype(BF16)
            dkv_ref[:, lo:lo + 128] = dk_ref[:, lo:lo + 128].astype(BF16)
            dkv_ref[:, lo + 128:lo + 256] = dv_ref[:, hd * B_V:(hd + 1) * B_V].astype(BF16)
            dkr = dkr + dk_ref[:, lo + 128:lo + 256]
        dcq, dgq = _rms_bwd(cq_ref[...], gq_ref[...], _dot(dqf_ref[...], wq_ref[...]))
        dckv, dgk = _rms_bwd(ckv_ref[...], gk_ref[...], _dot_nt(dkv_ref[...], wkv_ref[...]))
        dp_ref[:, offs[5]:offs[6]] = dcq.astype(BF16)
        dp_ref[:, offs[6]:offs[7]] = dckv.astype(BF16)
        dp_ref[:, offs[7]:] = _rope_t(dkr, cos_t, sin_a, sin_b).astype(BF16)
        j = 0
        for g, cnt in enumerate(counts):
            acc = ins[j][...].astype(F32)
            for jj in range(1, cnt):
                acc = acc + ins[j + jj][...].astype(F32)
            dp_ref[:, offs[g]:offs[g] + IN_WIDTHS[g]] = acc.astype(BF16)
            j += cnt
        xv = x_ref[...]
        h_ref[...] = _rms(xv, g_ref[...]).astype(BF16)
        split = offs[7]
        dh = _dot(dp_ref[:, 0:split], w_ref[0:split, :]) + _dot(dp_ref[:, split:], _rotary_key_rows(w_ref))
        dxn, dg = _rms_bwd(xv, g_ref[...], dh)
        dx_ref[...] = dx1_ref[...] + dxn

        @pl.when(pl.program_id(0) == 0)
        def _():
            dg_ref[...] = jnp.zeros_like(dg_ref)
            dgq_ref[...] = jnp.zeros_like(dgq_ref)
            dgk_ref[...] = jnp.zeros_like(dgk_ref)
            accq_ref[...] = jnp.zeros_like(accq_ref)
            acckv_ref[...] = jnp.zeros_like(acckv_ref)

        dg_ref[...] += dg
        dgq_ref[...] += dgq
        dgk_ref[...] += dgk
        accq_ref[...] += _dot_tn(dqf_ref[...], cqn_ref[...])
        acckv_ref[...] += _dot_tn(ckn_ref[...], dkv_ref[...])

        @pl.when(pl.program_id(0) == last)
        def _():
            dwq_ref[...] = accq_ref[...].astype(dwq_ref.dtype)
            dwkv_ref[...] = acckv_ref[...].astype(dwkv_ref.dtype)

    tok = lambda wd: pl.BlockSpec((tm, wd), lambda i: (i, 0))
    vec = lambda wd: pl.BlockSpec((1, wd), lambda i: (0, 0))
    whole = lambda r, c: pl.BlockSpec((r, c), lambda i: (0, 0))
    tab = pl.BlockSpec((tm, 128), lambda i: (i % nblk, 0))
    sds = lambda wd, dt: jax.ShapeDtypeStruct((t, wd), dt)
    return pl.pallas_call(
        body, name="inproj_qkv_bwd", grid=(t // tm,),
        out_shape=[sds(D_MODEL, F32), sds(D_MODEL, BF16), sds(D_IN_PAD, BF16),
                   jax.ShapeDtypeStruct((B_HEADS * QK_PAD, Q_LORA), BF16), jax.ShapeDtypeStruct((KV_LORA, 1024), BF16),
                   jax.ShapeDtypeStruct((1, D_MODEL), F32), jax.ShapeDtypeStruct((1, Q_LORA), F32),
                   jax.ShapeDtypeStruct((1, KV_LORA), F32)],
        in_specs=[tok(D_MODEL), _const_spec((1, D_MODEL)), _const_spec((D_IN, D_MODEL)), tok(D_MODEL), tok(Q_LORA),
                  tok(KV_LORA), _const_spec((1, Q_LORA)), _const_spec((1, KV_LORA)), _const_spec((1024, Q_LORA)),
                  _const_spec((KV_LORA, 1024)), tab, tab, tab, tok(1024), tok(1024), tok(512)] + [tok(512)] * n_flat,
        out_specs=[tok(D_MODEL), tok(D_MODEL), tok(D_IN_PAD), whole(B_HEADS * QK_PAD, Q_LORA), whole(KV_LORA, 1024),
                   vec(D_MODEL), vec(Q_LORA), vec(KV_LORA)],
        scratch_shapes=[pltpu.VMEM((tm, Q_LORA), BF16), pltpu.VMEM((tm, 1024), BF16), pltpu.VMEM((tm, KV_LORA), BF16),
                        pltpu.VMEM((tm, 1024), BF16), pltpu.VMEM((B_HEADS * QK_PAD, Q_LORA), F32),
                        pltpu.VMEM((KV_LORA, 1024), F32)],
        compiler_params=_params(),
    )(x, g1, w_in, dx1, cq, ckv, g_qa, g_kva, w_q, w_kv, *tables, dq, dk, dv, *flat)


def _cols_from_slots(g):
    n, r, cs = g.shape
    return g.transpose(1, 0, 2).reshape(r, n * cs)


def _cols_to_slots(full):
    r, c = full.shape
    return full.reshape(r, N_DEV, c // N_DEV).transpose(1, 0, 2)


def _rotary_key_rows(w_ref):
    rows = w_ref[D_IN - B_ROPE:D_IN, :]
    return jnp.concatenate([rows, jnp.zeros((D_IN_PAD - D_IN, D_MODEL), rows.dtype)], axis=0)


def _arrange_w_q_t(w_q_t):
    q3 = w_q_t.reshape(B_HEADS, B_NOPE + B_ROPE, Q_LORA)
    pad = jnp.zeros((B_HEADS, QK_PAD - B_NOPE - B_ROPE, Q_LORA), w_q_t.dtype)
    return jnp.concatenate([q3, pad], axis=1).reshape(B_HEADS * QK_PAD, Q_LORA)


def _unarrange_w_q_t(d_q_t):
    return d_q_t.reshape(B_HEADS, QK_PAD, Q_LORA)[:, :B_NOPE + B_ROPE].reshape(B_HEADS * (B_NOPE + B_ROPE), Q_LORA)


def _step_core(x, loss_target, small_w, lb_full, early_full, late, seq, group, tiles, distributed):
    g1, g_hgrn, g_qa, g_kva, g_mla, g2, g_fin = small_w
    w_in, w_q, w_kv = early_full[0], _arrange_w_q_t(early_full[1]), early_full[2]
    nb = x.shape[0]
    t = nb * seq
    tm, tm_fwd, tq_f, tq_b, tt = tiles
    xt = x.reshape(t, D_MODEL)
    tgt = loss_target.reshape(t, D_MODEL)
    tables = _rope_tables(seq)

    hq, hi, zf, zb, hg, cq, ckv, qcat, kcat, vv = _inproj_qkv(xt, g1, w_in, g_qa, g_kva, w_q, w_kv, tables, seq, tm_fwd)
    if distributed:
        oattn, lse, *late_slots = _attn_fwd(qcat, kcat, vv, nb, seq, tq_f, gather=tuple(late))
    else:
        oattn, lse = _attn_fwd(qcat, kcat, vv, nb, seq, tq_f)
        late_slots = late
    w_out = late_slots[0].reshape(D_MODEL, D_MODEL)
    w_gate, w_up = late_slots[1].reshape(D_FF, D_MODEL), late_slots[2].reshape(D_FF, D_MODEL)
    w_down = late_slots[3].reshape(D_FF, D_MODEL)
    lbl_f, lbl_b = lb_full[0], lb_full[1]
    o_f, o_b, save_f, save_b = _gla_fwd(hq, hi, (zf, zb), (lbl_f, lbl_b), nb, seq, group)
    x1, x2, ycat_b, gate_b, up_b, loss_row = _post_fwd(
        xt, o_f, o_b, hg, oattn, tgt, g_hgrn, g_mla, w_out, g2, w_gate, w_up, w_down, g_fin, tm_fwd)

    (dx1, d_ya, d_oattn, dx1_b, h2_b, dgate_b, dup_b, act_b, dx2_b, d_g_mla, d_g2, d_g_fin) = _post_bwd(
        x1, x2, gate_b, up_b, oattn, tgt, g_mla, w_out, g2, w_gate, w_up, w_down, g_fin, tm)
    d_w_gate = _matmul_tn(dgate_b, h2_b, 512, tt, "gate")
    d_w_up = _matmul_tn(dup_b, h2_b, 512, tt, "up")
    d_w_down = _matmul_tn(act_b, dx2_b, 512, tt, "down")
    d_w_out = _matmul_tn(ycat_b, dx1_b, D_MODEL, tt, "out")
    late_g = [d_w_out.reshape(N_DEV, D_MODEL // N_DEV, D_MODEL)] + [
        g.reshape(N_DEV, D_FF // N_DEV, D_MODEL) for g in (d_w_gate, d_w_up, d_w_down)]
    if distributed:
        dq, dk, dv, *late_g = _attn_bwd(qcat, kcat, vv, oattn, lse, d_oattn, nb, seq, tq_b, exchange=tuple(late_g))
    else:
        dq, dk, dv = _attn_bwd(qcat, kcat, vv, oattn, lse, d_oattn, nb, seq, tq_b)
    d_o, d_hg, d_g_hgrn = _gla_combine_bwd(o_f, o_b, hg, g_hgrn, d_ya, tm_fwd)
    dq_f, dv_f, dz_f, dq_b, dv_b, dz_b, dl_f, dl_b = _gla_bwd(
        hq, hi, (zf, zb), (lbl_f, lbl_b), (save_f, save_b), d_o, nb, seq, group)
    grad_x, h1_b, dproj_b, d_w_q, d_w_kv, d_g1, d_g_qa, d_g_kva = _inproj_qkv_bwd(
        xt, g1, w_in, dx1, [[dq_f, dq_b], [dv_f, dv_b], [dz_f], [dz_b], [d_hg]], cq, ckv, g_qa, g_kva, w_q, w_kv, tables,
        dq, dk, dv, seq, tm_fwd)
    half = D_MODEL // 2
    in_slots = lambda g: g.reshape(N_DEV, D_IN // N_DEV, half)
    g_in_a = in_slots(_matmul_tn(dproj_b, h1_b, half, tt, "in_a", b_cols=(0, half), k_out=D_IN))
    if distributed:
        d_w_in_b, g_in_a = _matmul_tn(dproj_b, h1_b, half, tt, "in_b", b_cols=(half, half), k_out=D_IN, exchange=(g_in_a,))
    else:
        d_w_in_b = _matmul_tn(dproj_b, h1_b, half, tt, "in_b", b_cols=(half, half), k_out=D_IN)

    early_g = [in_slots(d_w_in_b), _unarrange_w_q_t(d_w_q).reshape(N_DEV, 768 // N_DEV, Q_LORA), _cols_to_slots(d_w_kv)]
    d_lb = jnp.stack([jnp.sum(dl_f, axis=0), jnp.sum(dl_b, axis=0)], axis=0)
    small_grads = [d_g1, d_g_hgrn, d_g_qa, d_g_kva, d_g_mla, d_g2, d_g_fin]
    return loss_row, grad_x.reshape(nb, seq, D_MODEL), g_in_a, early_g, late_g, small_grads, d_lb


def kernel(x, norm1_g, w_in, lb_logits, hgrn_norm_g, q_a_norm_g, w_q_b, kv_a_norm_g, w_kv_b, mla_norm_g, w_out, norm2_g, w_gate, w_up, w_down, final_norm_g, loss_target, m_norm1_g, m_w_in, m_lb_logits, m_hgrn_norm_g, m_q_a_norm_g, m_w_q_b, m_kv_a_norm_g, m_w_kv_b, m_mla_norm_g, m_w_out, m_norm2_g, m_w_gate, m_w_up, m_w_down, m_final_norm_g, v_norm1_g, v_w_in, v_lb_logits, v_hgrn_norm_g, v_q_a_norm_g, v_w_q_b, v_kv_a_norm_g, v_w_kv_b, v_mla_norm_g, v_w_out, v_norm2_g, v_w_gate, v_w_up, v_w_down, v_final_norm_g):
    big_w = [w_in, w_q_b, w_kv_b, w_out, w_gate, w_up, w_down]
    big_m = [m_w_in, m_w_q_b, m_w_kv_b, m_w_out, m_w_gate, m_w_up, m_w_down]
    big_v = [v_w_in, v_w_q_b, v_w_kv_b, v_w_out, v_w_gate, v_w_up, v_w_down]
    small_w = [norm1_g, hgrn_norm_g, q_a_norm_g, kv_a_norm_g, mla_norm_g, norm2_g, final_norm_g]
    small_m = [m_norm1_g, m_hgrn_norm_g, m_q_a_norm_g, m_kv_a_norm_g, m_mla_norm_g, m_norm2_g, m_final_norm_g]
    small_v = [v_norm1_g, v_hgrn_norm_g, v_q_a_norm_g, v_kv_a_norm_g, v_mla_norm_g, v_norm2_g, v_final_norm_g]
    seq = x.shape[1]
    my_id = 4 * lax.axis_index("x") + 2 * lax.axis_index("y") + lax.axis_index("c")

    shard = lambda w: w[0].astype(BF16)
    col_t = lambda w: jnp.swapaxes(w, 1, 2)[0]
    shard_t = lambda w: col_t(w).astype(BF16)
    g_in, g_q, g_kv, g_lb = _all_gather_call([shard_t(w_in), shard_t(w_q_b), shard(w_kv_b), lb_logits.reshape(4, 64)])
    early_full = (g_in.reshape(D_IN, D_MODEL), g_q.reshape(768, Q_LORA), _cols_from_slots(g_kv))
    lb_full = g_lb.reshape(N_DEV, 2, 2, 64).transpose(1, 2, 0, 3).reshape(2, 2, 512)

    as_row = lambda a: a.reshape(1, -1)
    loss_row, grad_x, recv_in_a, early_g, late_recv, small_g, d_lb = _step_core(
        x, loss_target, [as_row(s) for s in small_w], lb_full, early_full,
        [shard(w_out), shard_t(w_gate), shard_t(w_up), shard(w_down)], seq, min(32, seq // CHUNK),
        (256, 512, min(1024, seq), min(1024, seq), min(2048, 2 * seq)), True)

    grads, deltas, new_ms, new_vs = {}, {}, {}, {}
    views = {name: (col_t if name in ("w_in", "w_q_b", "w_gate", "w_up") else (lambda a: a[0])) for name, _, _, _ in BIG}
    backs = {name: ((lambda a: jnp.swapaxes(a[None], 1, 2)) if name in ("w_in", "w_q_b", "w_gate", "w_up") else (lambda a: a[None]))
             for name, _, _, _ in BIG}
    by_name = {name: (w, m, v) for (name, _, _, _), w, m, v in zip(BIG, big_w, big_m, big_v)}
    late_names = ["w_out", "w_gate", "w_up", "w_down"]
    n_small = len(small_g)
    g_l, d_l, nm_l, nv_l, recv = _adamw_recv_hosting(
        [views[n](by_name[n][0]) for n in late_names], list(late_recv), [views[n](by_name[n][1]) for n in late_names],
        [views[n](by_name[n][2]) for n in late_names],
        early_g + small_g + [d_lb.reshape(4, 512), loss_row], [True] * 3 + [False] * (n_small + 2))
    for i, name in enumerate(late_names):
        grads[name], deltas[name], new_ms[name], new_vs[name] = (backs[name](a[i]) for a in (g_l, d_l, nm_l, nv_l))
    sums = _sum_slots_call(recv[3:])
    g_small = [g.reshape(s.shape) for g, s in zip(sums[:n_small], small_w)]
    g_lb_own = lax.dynamic_index_in_dim(sums[n_small].reshape(2, 2, N_DEV, 64), my_id, axis=2, keepdims=False)
    loss = sums[n_small + 1][0, 0]

    for name, r in zip(["w_in", "w_q_b", "w_kv_b"], recv[:3]):
        w, m, v = (views[name](a) for a in by_name[name])
        g, d, nm, nv = _adamw_recv_halves(w, (recv_in_a, r), m, v, name) if name == "w_in" else _adamw_recv(w, r, m, v, name)
        grads[name], deltas[name], new_ms[name], new_vs[name] = (backs[name](a) for a in (g, d, nm, nv))
    lb_rows = lambda a: a.reshape(4, 64)
    d_s, nm_s, nv_s = _adamw_small(
        [as_row(a) for a in small_w] + [lb_rows(lb_logits)], [as_row(a) for a in g_small] + [lb_rows(g_lb_own)],
        [as_row(a) for a in small_m] + [lb_rows(m_lb_logits)], [as_row(a) for a in small_v] + [lb_rows(v_lb_logits)])
    for i, (s, (name, _)) in enumerate(zip(small_w + [lb_logits], SMALL + (("lb_logits", 0),))):
        grads[name] = (g_small + [g_lb_own])[i]
        deltas[name], new_ms[name], new_vs[name] = d_s[i].reshape(s.shape), nm_s[i].reshape(s.shape), nv_s[i].reshape(s.shape)

    order = ["norm1_g", "w_in", "lb_logits", "hgrn_norm_g", "q_a_norm_g", "w_q_b", "kv_a_norm_g", "w_kv_b", "mla_norm_g",
             "w_out", "norm2_g", "w_gate", "w_up", "w_down", "final_norm_g"]
    return (loss, grad_x, *[grads[n] for n in order], *[deltas[n] for n in order],
            *[new_ms[n] for n in order], *[new_vs[n] for n in order])
```

```python
import functools

import jax
import jax.numpy as jnp
from jax import lax
from jax.experimental import pallas as pl
from jax.experimental.pallas import tpu as pltpu

F32 = jnp.float32
BF16 = jnp.bfloat16

N_DEV = 8
D_MODEL = 1024
D_FF = 2816
A_WIDTH = 512
HEAD_PAIR = 128
CHUNK = 64
B_HEADS = 4
B_NOPE = 128
B_ROPE = 64
B_V = 128
QK_PAD = 256
Q_LORA = 384
KV_LORA = 256
D_IN = 3264
D_IN_PAD = 3328
IN_WIDTHS = (512, 512, 512, 512, 512, Q_LORA, KV_LORA, 128)
ROPE_THETA = 10000.0
EPS = 1e-6
ATTN_SCALE = (B_NOPE + B_ROPE) ** -0.5
ATTN_SUB = 256
ATTN_SUB_BWD = 256
ROW_SUB = 256
ADAM_LR, ADAM_B1, ADAM_B2, ADAM_EPS, ADAM_WD, ADAM_STEP = 0.001, 0.9, 0.999, 1e-08, 0.01, 10
VMEM_LIMIT = 60 * 1024 * 1024
MESH = pl.DeviceIdType.MESH

BIG = (("w_in", 1024, D_IN, 1), ("w_q_b", Q_LORA, 768, 1), ("w_kv_b", KV_LORA, 1024, 1), ("w_out", 1024, 1024, 0),
       ("w_gate", 1024, D_FF, 1), ("w_up", 1024, D_FF, 1), ("w_down", D_FF, 1024, 0))
SMALL = (("norm1_g", 1024), ("hgrn_norm_g", 512), ("q_a_norm_g", 384), ("kv_a_norm_g", 256), ("mla_norm_g", 512),
         ("norm2_g", 1024), ("final_norm_g", 1024))


def _params(**kw):
    return pltpu.CompilerParams(vmem_limit_bytes=VMEM_LIMIT, **kw)


def _const_spec(shape):
    return pl.BlockSpec(shape, lambda *_: (0,) * len(shape), pipeline_mode=pl.Buffered(1))


def _dot(a, b):
    return jnp.dot(a, b, preferred_element_type=F32)


def _dot_nt(a, b):
    return lax.dot_general(a, b, (((1,), (1,)), ((), ())), preferred_element_type=F32)


def _dot_tn(a, b):
    return lax.dot_general(a, b, (((0,), (0,)), ((), ())), preferred_element_type=F32)


@jax.custom_vjp
def _mm(a, b):
    return _dot(a.astype(BF16), b.astype(BF16))


def _mm_fwd(a, b):
    return _mm(a, b), (a, b)


def _mm_bwd(res, g):
    a, b = res
    gb = g.astype(BF16)
    return _dot_nt(gb, b.astype(BF16)), _dot_tn(a.astype(BF16), gb)


_mm.defvjp(_mm_fwd, _mm_bwd)


@jax.custom_vjp
def _mm_nt(a, b):
    return _dot_nt(a.astype(BF16), b.astype(BF16))


def _mm_nt_fwd(a, b):
    return _mm_nt(a, b), (a, b)


def _mm_nt_bwd(res, g):
    a, b = res
    gb = g.astype(BF16)
    return _dot(gb, b.astype(BF16)), _dot_tn(gb, a.astype(BF16))


_mm_nt.defvjp(_mm_nt_fwd, _mm_nt_bwd)


@jax.custom_vjp
def _mm_tn(a, b):
    return _dot_tn(a.astype(BF16), b.astype(BF16))


def _mm_tn_fwd(a, b):
    return _mm_tn(a, b), (a, b)


def _mm_tn_bwd(res, g):
    a, b = res
    gb = g.astype(BF16)
    return _dot_nt(b.astype(BF16), gb), _dot(a.astype(BF16), gb)


_mm_tn.defvjp(_mm_tn_fwd, _mm_tn_bwd)


def _dot_exact_rhs(a, m):
    hi = a.astype(BF16)
    lo = (a - hi.astype(F32)).astype(BF16)
    return _dot(hi, m) + _dot(lo, m)


@jax.custom_vjp
def _group_mean(a, m):
    return _dot_exact_rhs(a, m)


def _group_mean_fwd(a, m):
    return _group_mean(a, m), m


def _group_mean_bwd(m, g):
    return _dot_exact_rhs(g, m), jnp.zeros_like(m)


_group_mean.defvjp(_group_mean_fwd, _group_mean_bwd)


def _roll_rows(a, shift):
    return pltpu.roll(a, shift, 0)


def _cumsum_rows_raw(a, reverse):
    n = a.shape[0]
    row = lax.broadcasted_iota(jnp.int32, a.shape, 0)
    s = 1
    while s < n:
        if reverse:
            a = a + jnp.where(row < n - s, _roll_rows(a, n - s), 0.0)
        else:
            a = a + jnp.where(row >= s, _roll_rows(a, s), 0.0)
        s *= 2
    return a


@functools.partial(jax.custom_vjp, nondiff_argnums=(1,))
def _cumsum_rows(a, reverse):
    return _cumsum_rows_raw(a, reverse)


def _cumsum_rows_fwd(a, reverse):
    return _cumsum_rows_raw(a, reverse), None


def _cumsum_rows_bwd(reverse, _, g):
    return (_cumsum_rows_raw(g, not reverse),)


_cumsum_rows.defvjp(_cumsum_rows_fwd, _cumsum_rows_bwd)


def _rms(x, g):
    r = lax.rsqrt(jnp.mean(x * x, axis=-1, keepdims=True) + EPS)
    return x * r * g


def _rms_bwd(x, g, dy):
    r = lax.rsqrt(jnp.mean(x * x, axis=-1, keepdims=True) + EPS)
    xh = x * r
    dg = jnp.sum(dy * xh, axis=0, keepdims=True)
    dxh = dy * g
    dx = r * (dxh - xh * jnp.mean(dxh * xh, axis=-1, keepdims=True))
    return dx, dg


def _sigmoid(a):
    return jax.nn.sigmoid(a)


def _mesh_place():
    x, y, c = lax.axis_index("x"), lax.axis_index("y"), lax.axis_index("c")
    return x, y, c


def _dev_index(p):
    return 4 * p[0] + 2 * p[1] + p[2]


def _comm_sems(n):
    return [pltpu.SemaphoreType.DMA((n, 7)), pltpu.SemaphoreType.DMA((n, 7)), pltpu.SemaphoreType.DMA((n,))]


def _gather_protocol(ins, outs, send_sems, recv_sems, local_sems):
    n = len(ins)
    x, y, c = _mesh_place()
    me, sibling = (x, y, c), (x, y, 1 - c)
    chips = [(1 - x, y), (x, 1 - y), (1 - x, 1 - y)]

    def copy(a, k, block, to, src=None):
        slot = outs[a].at[_dev_index(block)]
        return pltpu.make_async_remote_copy(
            src_ref=slot if src is None else src, dst_ref=slot,
            send_sem=send_sems.at[a, k], recv_sem=recv_sems.at[a, k], device_id=to, device_id_type=MESH)

    def mine(a):
        return pltpu.make_async_copy(ins[a], outs[a].at[_dev_index(me)], local_sems.at[a])

    def first(a):
        return [copy(a, 0, me, sibling, src=ins[a])] + [copy(a, 1 + j, me, (*chip, c), src=ins[a]) for j, chip in enumerate(chips)]

    def start():
        for a in range(n):
            mine(a).start()
            for cp in first(a):
                cp.start()

    def forward():
        for a in range(n):
            for j, chip in enumerate(chips):
                copy(a, 1 + j, (*chip, c), me).wait_recv()
                copy(a, 4 + j, (*chip, c), sibling).start()

    def finish():
        for a in range(n):
            copy(a, 0, sibling, me).wait_recv()
            for j, chip in enumerate(chips):
                copy(a, 4 + j, (*chip, 1 - c), me).wait_recv()
        for a in range(n):
            mine(a).wait()
            for cp in first(a):
                cp.wait_send()
            for j, chip in enumerate(chips):
                copy(a, 4 + j, (*chip, c), sibling).wait_send()

    return start, forward, finish


def _exchange_protocol(ins, outs, scatter, send_sems, recv_sems, local_sems):
    n = len(ins)
    x, y, c = _mesh_place()
    me = (x, y, c)
    my_id = _dev_index(me)
    rels = [(dx, dy, dc) for dx in (0, 1) for dy in (0, 1) for dc in (0, 1)][1:]

    def peer_of(rel):
        return tuple(1 - v if d else v for v, d in zip(me, rel))

    def src(a, dev):
        return ins[a].at[dev] if scatter[a] else ins[a]

    def send(a, k):
        peer = peer_of(rels[k])
        return pltpu.make_async_remote_copy(
            src_ref=src(a, _dev_index(peer)), dst_ref=outs[a].at[my_id],
            send_sem=send_sems.at[a, k], recv_sem=recv_sems.at[a, k], device_id=peer, device_id_type=MESH)

    def arrival(a, k):
        peer = peer_of(rels[k])
        return pltpu.make_async_remote_copy(
            src_ref=src(a, my_id), dst_ref=outs[a].at[_dev_index(peer)],
            send_sem=send_sems.at[a, k], recv_sem=recv_sems.at[a, k], device_id=peer, device_id_type=MESH)

    def own(a):
        return pltpu.make_async_copy(src(a, my_id), outs[a].at[my_id], local_sems.at[a])

    def start():
        for a in range(n):
            own(a).start()
            for k in range(7):
                send(a, k).start()

    def finish():
        for a in range(n):
            for k in range(7):
                arrival(a, k).wait_recv()
        for a in range(n):
            for k in range(7):
                send(a, k).wait_send()
            own(a).wait()

    return start, finish


def _slot_shapes(blocks, scatter=None):
    return [jax.ShapeDtypeStruct(b.shape if (scatter and scatter[a]) else (N_DEV,) + b.shape, b.dtype) for a, b in enumerate(blocks)]


def _all_gather_call(blocks):
    n = len(blocks)

    def body(*refs):
        start, forward, finish = _gather_protocol(refs[:n], refs[n:2 * n], *refs[2 * n:])
        start()
        forward()
        finish()

    any_spec = pl.BlockSpec(memory_space=pl.ANY)
    return pl.pallas_call(
        body, name="weights_all_gather", out_shape=_slot_shapes(blocks),
        in_specs=[any_spec] * n, out_specs=[any_spec] * n, scratch_shapes=_comm_sems(n),
    )(*blocks)


def _sum_slots_call(recvs):
    n = len(recvs)

    def body(*refs):
        for in_ref, out_ref in zip(refs[:n], refs[n:]):
            acc = in_ref[0]
            for j in range(1, N_DEV):
                acc = acc + in_ref[j]
            out_ref[...] = acc

    return pl.pallas_call(
        body, name="small_grad_sum", out_shape=[jax.ShapeDtypeStruct(r.shape[1:], F32) for r in recvs],
        compiler_params=_params(),
    )(*recvs)


def _adam_update(w, g, m, v):
    nm = ADAM_B1 * m + (1.0 - ADAM_B1) * g
    nv = ADAM_B2 * v + (1.0 - ADAM_B2) * (g * g)
    bc1 = 1.0 - ADAM_B1 ** ADAM_STEP
    bc2 = 1.0 - ADAM_B2 ** ADAM_STEP
    return -ADAM_LR * ((nm / bc1) / (jnp.sqrt(nv / bc2) + ADAM_EPS) + ADAM_WD * w), nm, nv


def _adamw_recv(w, recv, m, v, tag):
    r, c = w.shape
    tr = r
    for cand in (512, 256, 128):
        if r > cand and r % cand == 0:
            tr = cand
            break

    def body(w_ref, r_ref, m_ref, v_ref, g_ref, d_ref, nm_ref, nv_ref):
        g = r_ref[0].astype(F32)
        for j in range(1, N_DEV):
            g = g + r_ref[j].astype(F32)
        g_ref[...] = g
        d_ref[...], nm_ref[...], nv_ref[...] = _adam_update(w_ref[...], g, m_ref[...], v_ref[...])

    spec = pl.BlockSpec((tr, c), lambda i: (i, 0))
    return pl.pallas_call(
        body, name="adamw_" + tag, out_shape=[jax.ShapeDtypeStruct(w.shape, F32)] * 4, grid=(r // tr,),
        in_specs=[spec, pl.BlockSpec((N_DEV, tr, c), lambda i: (0, i, 0)), spec, spec], out_specs=[spec] * 4,
        compiler_params=_params(),
    )(w, recv, m, v)


def _adamw_recv_halves(w, recv_halves, m, v, tag):
    r, c = w.shape
    half = c // 2

    def body(w_ref, ra_ref, rb_ref, m_ref, v_ref, g_ref, d_ref, nm_ref, nv_ref):
        def update(r_ref):
            g = r_ref[0].astype(F32)
            for j in range(1, N_DEV):
                g = g + r_ref[j].astype(F32)
            g_ref[...] = g
            d_ref[...], nm_ref[...], nv_ref[...] = _adam_update(w_ref[...], g, m_ref[...], v_ref[...])

        pl.when(pl.program_id(0) == 0)(lambda: update(ra_ref))
        pl.when(pl.program_id(0) == 1)(lambda: update(rb_ref))

    spec = pl.BlockSpec((r, half), lambda j: (0, j))
    whole = pl.BlockSpec((N_DEV, r, half), lambda j: (0, 0, 0))
    return pl.pallas_call(
        body, name="adamw_" + tag, out_shape=[jax.ShapeDtypeStruct(w.shape, F32)] * 4, grid=(2,),
        in_specs=[spec, whole, whole, spec, spec], out_specs=[spec] * 4, compiler_params=_params(),
    )(w, *recv_halves, m, v)


def _adamw_recv_hosting(ws, recvs, ms, vs, blocks, scatter):
    n, ne = len(ws), len(blocks)
    rows = max(w.shape[0] for w in ws)
    cols = ws[0].shape[1]
    assert all(w.shape[1] == cols for w in ws)

    def body(*refs):
        ins, ex_in = refs[:4 * n], refs[4 * n:4 * n + ne]
        outs, ex_out = refs[4 * n + ne:8 * n + ne], refs[8 * n + ne:8 * n + 2 * ne]
        in_buf, recv_buf, out_buf, in_sems, out_sems = refs[8 * n + 2 * ne:8 * n + 2 * ne + 5]
        start, finish = _exchange_protocol(ex_in, ex_out, scatter, *refs[8 * n + 2 * ne + 5:])
        start()
        for a in range(n):
            r = pl.ds(0, ws[a].shape[0])
            loads = [pltpu.make_async_copy(ins[k * n + a], in_buf.at[j, r], in_sems.at[j]) for j, k in enumerate((0, 2, 3))]
            loads.append(pltpu.make_async_copy(ins[n + a], recv_buf.at[:, r], in_sems.at[3]))
            for cp in loads:
                cp.start()
            for cp in loads:
                cp.wait()
            g = recv_buf[0, r].astype(F32)
            for j in range(1, N_DEV):
                g = g + recv_buf[j, r].astype(F32)
            out_buf[0, r] = g
            out_buf[1, r], out_buf[2, r], out_buf[3, r] = _adam_update(in_buf[0, r], g, in_buf[1, r], in_buf[2, r])
            stores = [pltpu.make_async_copy(out_buf.at[k, r], outs[k * n + a], out_sems.at[k]) for k in range(4)]
            for cp in stores:
                cp.start()
            for cp in stores:
                cp.wait()
        finish()

    any_spec = pl.BlockSpec(memory_space=pl.ANY)
    out = pl.pallas_call(
        body, name="adamw_late_and_grad_exchange",
        out_shape=[jax.ShapeDtypeStruct(w.shape, F32) for w in ws] * 4 + _slot_shapes(blocks, scatter),
        in_specs=[any_spec] * (4 * n + ne), out_specs=[any_spec] * (4 * n + ne),
        scratch_shapes=[pltpu.VMEM((3, rows, cols), F32), pltpu.VMEM((N_DEV, rows, cols), BF16), pltpu.VMEM((4, rows, cols), F32),
                        pltpu.SemaphoreType.DMA((4,)), pltpu.SemaphoreType.DMA((4,))] + _comm_sems(ne),
        compiler_params=_params(),
    )(*ws, *recvs, *ms, *vs, *blocks)
    return out[:n], out[n:2 * n], out[2 * n:3 * n], out[3 * n:4 * n], out[4 * n:]


def _adamw_small(ws, gs, ms, vs):
    n = len(ws)

    def body(*refs):
        ins, outs = refs[:4 * n], refs[4 * n:]
        for a in range(n):
            d, nm, nv = _adam_update(ins[a][...], ins[n + a][...], ins[2 * n + a][...], ins[3 * n + a][...])
            outs[a][...], outs[n + a][...], outs[2 * n + a][...] = d, nm, nv

    out = pl.pallas_call(
        body, name="adamw_small", out_shape=[jax.ShapeDtypeStruct(w.shape, F32) for w in ws] * 3, compiler_params=_params(),
    )(*ws, *gs, *ms, *vs)
    return out[:n], out[n:2 * n], out[2 * n:]


def _rope_tables(seq):
    inv = 1.0 / (ROPE_THETA ** (jnp.arange(0, B_ROPE, 2, dtype=F32) / B_ROPE))
    ang = jnp.arange(seq, dtype=F32)[:, None] * inv[None, :]
    cos, sin = jnp.cos(ang), jnp.sin(ang)
    z32, z64 = jnp.zeros_like(cos), jnp.zeros((seq, 64), F32)
    cos_t = jnp.concatenate([cos, cos, z64], axis=1)
    sin_a = jnp.concatenate([-sin, z32, z64], axis=1)
    sin_b = jnp.concatenate([z32, sin, z64], axis=1)
    return cos_t, sin_a, sin_b


def _rope(t, cos_t, sin_a, sin_b):
    return t * cos_t + pltpu.roll(t, 96, 1) * sin_a + pltpu.roll(t, 32, 1) * sin_b


def _rope_t(d, cos_t, sin_a, sin_b):
    return d * cos_t + pltpu.roll(d * sin_a, 32, 1) + pltpu.roll(d * sin_b, 96, 1)


def _inproj_qkv(x, g1, w_in, g_qa, g_kva, w_q, w_kv, tables, seq, tm):
    t = x.shape[0]
    nblk = seq // tm
    n_plain = 7
    offs = [sum(IN_WIDTHS[:j]) for j in range(len(IN_WIDTHS))]

    def body(x_ref, g_ref, w_ref, gq_ref, gk_ref, wq_ref, wkv_ref, c_ref, sa_ref, sb_ref, *outs):
        q_out, k_out, v_out = outs[n_plain:]
        for j in range(tm // min(tm, ROW_SUB)):
            r = pl.ds(j * min(tm, ROW_SUB), min(tm, ROW_SUB))
            h = _rms(x_ref[r, :], g_ref[...]).astype(BF16)
            proj = lambda g: _dot_nt(h, w_ref[offs[g]:offs[g] + IN_WIDTHS[g], :])
            for g in range(5):
                outs[g][r, :] = proj(g)
            cq, ckv, kr = proj(5), proj(6), proj(7)
            outs[5][r, :] = cq
            outs[6][r, :] = ckv
            cos_t, sin_a, sin_b = c_ref[r, :], sa_ref[r, :], sb_ref[r, :]
            cqn = _rms(cq, gq_ref[...]).astype(BF16)
            ckn = _rms(ckv, gk_ref[...]).astype(BF16)
            kr_rot = _rope(kr, cos_t, sin_a, sin_b).astype(BF16)
            for hd in range(B_HEADS):
                lo = hd * QK_PAD
                q_out[r, lo:lo + 128] = (_dot_nt(cqn, wq_ref[lo:lo + 128, :]) * ATTN_SCALE).astype(BF16)
                qr = _rope(_dot_nt(cqn, wq_ref[lo + 128:lo + 256, :]), cos_t, sin_a, sin_b)
                q_out[r, lo + 128:lo + 256] = (qr * ATTN_SCALE).astype(BF16)
                k_out[r, lo:lo + 128] = _dot(ckn, wkv_ref[:, lo:lo + 128]).astype(BF16)
                k_out[r, lo + 128:lo + 256] = kr_rot
                v_out[r, hd * B_V:(hd + 1) * B_V] = _dot(ckn, wkv_ref[:, lo + 128:lo + 256]).astype(BF16)

    tok = lambda wd: pl.BlockSpec((tm, wd), lambda i: (i, 0))
    tab = pl.BlockSpec((tm, 128), lambda i: (i % nblk, 0))
    widths = list(IN_WIDTHS[:n_plain]) + [B_HEADS * QK_PAD, B_HEADS * QK_PAD, B_HEADS * B_V]
    dtypes = [F32] * n_plain + [BF16] * 3
    return pl.pallas_call(
        body, name="inproj_qkv_fwd", grid=(t // tm,),
        out_shape=[jax.ShapeDtypeStruct((t, wd), dt) for wd, dt in zip(widths, dtypes)],
        in_specs=[tok(D_MODEL), _const_spec((1, D_MODEL)), _const_spec((D_IN_PAD, D_MODEL)), _const_spec((1, Q_LORA)),
                  _const_spec((1, KV_LORA)), _const_spec((B_HEADS * QK_PAD, Q_LORA)), _const_spec((KV_LORA, 1024)), tab, tab, tab],
        out_specs=[tok(wd) for wd in widths],
        compiler_params=_params(),
    )(x, g1, w_in, g_qa, g_kva, w_q, w_kv, *tables)


def _step_index(nq):
    return (pl.program_id(0) * B_HEADS + pl.program_id(1)) * nq + pl.program_id(2)


def _attn_fwd(qcat, kcat, v, nb, seq, tq, gather=()):
    t = qcat.shape[0]
    nq = seq // tq
    ng = len(gather)
    steps = nb * B_HEADS * nq

    def body(q_ref, k_ref, v_ref, *rest):
        o_ref, lse_ref = rest[ng:ng + 2]
        if ng:
            start, finish = _exchange_protocol(rest[:ng], rest[ng + 2:2 * ng + 2], [False] * ng, *rest[2 * ng + 2:])
            pl.when(_step_index(nq) == 0)(start)
        for j in range(tq // ATTN_SUB):
            r = pl.ds(j * ATTN_SUB, ATTN_SUB)
            s = _dot_nt(q_ref[r, :], k_ref[...])
            m = jnp.max(s, axis=-1, keepdims=True)
            p = jnp.exp(s - m)
            l = jnp.sum(p, axis=-1, keepdims=True)
            o_ref[r, :] = _dot(p.astype(BF16), v_ref[...]) / l
            lse_ref[0, r, :] = m + jnp.log(l)
        if ng:
            pl.when(_step_index(nq) == steps - 1)(finish)

    any_spec = pl.BlockSpec(memory_space=pl.ANY)
    return pl.pallas_call(
        body, name="attn_fwd", grid=(nb, B_HEADS, nq),
        out_shape=[jax.ShapeDtypeStruct((t, B_HEADS * B_V), F32), jax.ShapeDtypeStruct((B_HEADS, t, 1), F32)] + _slot_shapes(gather),
        in_specs=[pl.BlockSpec((tq, QK_PAD), lambda b, h, i: (b * nq + i, h)),
                  pl.BlockSpec((seq, QK_PAD), lambda b, h, i: (b, h)),
                  pl.BlockSpec((seq, B_V), lambda b, h, i: (b, h))] + [any_spec] * ng,
        out_specs=[pl.BlockSpec((tq, B_V), lambda b, h, i: (b * nq + i, h)),
                   pl.BlockSpec((1, tq, 1), lambda b, h, i: (h, b * nq + i, 0))] + [any_spec] * ng,
        scratch_shapes=_comm_sems(ng) if ng else [],
        compiler_params=_params(),
    )(qcat, kcat, v, *gather)


def _attn_bwd(qcat, kcat, v, o, lse, do, nb, seq, tq, exchange=()):
    t = qcat.shape[0]
    nq = seq // tq
    ne = len(exchange)
    steps = nb * B_HEADS * nq

    def body(q_ref, k_ref, v_ref, o_ref, lse_ref, do_ref, *rest):
        dq_ref, dk_ref, dv_ref = rest[ne:ne + 3]
        p_ref, ds_ref = rest[2 * ne + 3:2 * ne + 5]
        if ne:
            start, finish = _exchange_protocol(rest[:ne], rest[ne + 3:2 * ne + 3], [True] * ne, *rest[2 * ne + 5:])
            pl.when(_step_index(nq) == 0)(start)

        @pl.when(pl.program_id(2) == 0)
        def _():
            dv_ref[...] = jnp.zeros_like(dv_ref)
            dk_ref[...] = jnp.zeros_like(dk_ref)

        for j in range(tq // ATTN_SUB_BWD):
            r = pl.ds(j * ATTN_SUB_BWD, ATTN_SUB_BWD)
            q, k = q_ref[r, :], k_ref[...]
            do_f = do_ref[r, :].astype(F32)
            delta = jnp.sum(do_f * o_ref[r, :], axis=-1, keepdims=True)
            dob = do_f.astype(BF16)
            p = jnp.exp(_dot_nt(q, k) - lse_ref[0, r, :])
            ds = (p * (_dot_nt(dob, v_ref[...]) - delta)).astype(BF16)
            dq_ref[r, :] = _dot(ds, k).astype(dq_ref.dtype)
            p_ref[r, :] = p.astype(BF16)
            ds_ref[r, :] = ds
        dv_ref[...] += _dot_tn(p_ref[...], do_ref[...].astype(BF16))
        dk_ref[...] += _dot_tn(ds_ref[...], q_ref[...])
        if ne:
            pl.when(_step_index(nq) == steps - 1)(finish)

    qspec = lambda wd: pl.BlockSpec((tq, wd), lambda b, h, i: (b * nq + i, h))
    kspec = lambda wd: pl.BlockSpec((seq, wd), lambda b, h, i: (b, h))
    any_spec = pl.BlockSpec(memory_space=pl.ANY)
    return pl.pallas_call(
        body, name="attn_bwd", grid=(nb, B_HEADS, nq),
        out_shape=[jax.ShapeDtypeStruct((t, B_HEADS * QK_PAD), BF16), jax.ShapeDtypeStruct((t, B_HEADS * QK_PAD), F32),
                   jax.ShapeDtypeStruct((t, B_HEADS * B_V), F32)] + _slot_shapes(exchange, [True] * ne),
        in_specs=[qspec(QK_PAD), kspec(QK_PAD), kspec(B_V), qspec(B_V),
                  pl.BlockSpec((1, tq, 1), lambda b, h, i: (h, b * nq + i, 0)), qspec(B_V)] + [any_spec] * ne,
        out_specs=[qspec(QK_PAD), kspec(QK_PAD), kspec(B_V)] + [any_spec] * ne,
        scratch_shapes=[pltpu.VMEM((tq, seq), BF16), pltpu.VMEM((tq, seq), BF16)] + (_comm_sems(ne) if ne else []),
        compiler_params=_params(),
    )(qcat, kcat, v, o, lse, do, *exchange)


def _gla_consts(reverse):
    row = lax.broadcasted_iota(jnp.int32, (CHUNK, CHUNK), 0)
    col = lax.broadcasted_iota(jnp.int32, (CHUNK, CHUNK), 1)
    causal = (row <= col) if reverse else (row >= col)
    lane = lax.broadcasted_iota(jnp.int32, (1, HEAD_PAIR), 1)
    m0 = (lane < 64).astype(F32)
    m1 = 1.0 - m0
    r2 = lax.broadcasted_iota(jnp.int32, (HEAD_PAIR, HEAD_PAIR), 0)
    c2 = lax.broadcasted_iota(jnp.int32, (HEAD_PAIR, HEAD_PAIR), 1)
    same_head = ((r2 < 64) == (c2 < 64)).astype(F32)
    return causal, m0, m1, same_head


def _gla_chunk(hq, hi, z, l0, l1, st, consts, reverse):
    q_dec, k_inv, k_end, decay = _gla_gates(hq, z, l0, l1, reverse)
    o, st_new = _gla_state(q_dec, st, decay, _gla_increment(hi, k_end, consts))
    return o + _gla_intra(q_dec, k_inv, hi, consts), st_new


def _gla_gates(hq, z, l0, l1, reverse):
    mx = jnp.maximum(l0, l1)
    e0, e1 = jnp.exp(l0 - mx), jnp.exp(l1 - mx)
    lb = e0 / (e0 + e1)
    q = hq * _sigmoid(hq)
    sz = _sigmoid(z)
    log_f = jnp.log(lb + (1.0 - lb) * sz)
    k = (1.0 - lb) * (1.0 - sz)
    cum = _cumsum_rows(log_f, reverse)
    decay = jnp.exp(jnp.sum(log_f, axis=0, keepdims=True))
    k_inv = k * jnp.exp(-cum)
    return q * jnp.exp(cum), k_inv, k_inv * decay, decay


def _gla_intra(q_dec, k_inv, hi, consts):
    causal, m0, m1, _ = consts
    o = None
    for mh in (m0, m1):
        s = jnp.where(causal, _mm_nt(q_dec * mh, k_inv), 0.0)
        part = _mm(s, hi) * mh
        o = part if o is None else o + part
    return o


def _gla_increment(hi, k_end, consts):
    return _mm_tn(hi, k_end) * consts[3]


def _gla_state(q_dec, st, decay, inc):
    return _mm_nt(q_dec, st), st * decay + inc


GLA_DIRS = (False, True)
GLA_BATCH_FWD = 8
GLA_BATCH_BWD = 4


def _gla_fwd(hq, hi, zs, lbls, nb, seq, group):
    t = hq.shape[0]
    rows = group * CHUNK
    nblk = seq // rows
    n_chunks = seq // CHUNK
    nd = len(GLA_DIRS)

    def body(*refs):
        ins, outs, st_refs = refs[:4 * nd], refs[4 * nd:6 * nd], refs[6 * nd:]
        @pl.when(pl.program_id(2) == 0)
        def _():
            for st_ref in st_refs:
                st_ref[...] = jnp.zeros_like(st_ref)

        consts = [_gla_consts(rev) for rev in GLA_DIRS]
        work = [(d, rev, group - 1 - cc if rev else cc) for cc in range(group) for d, rev in enumerate(GLA_DIRS)]
        rows_of = lambda c: pl.ds(c * CHUNK, CHUNK)
        sts = [st_ref[...] for st_ref in st_refs]
        for w0 in range(0, len(work), GLA_BATCH_FWD):
            batch = work[w0:w0 + GLA_BATCH_FWD]
            gates, intra, incs = {}, {}, {}
            for d, rev, c in batch:
                hq_ref, _, z_ref, lbl_ref = ins[4 * d:4 * d + 4]
                gates[d, c] = _gla_gates(hq_ref[rows_of(c), :], z_ref[rows_of(c), :], lbl_ref[0:1, :], lbl_ref[1:2, :], rev)
            for d, rev, c in batch:
                hi_c = ins[4 * d + 1][rows_of(c), :]
                intra[d, c] = _gla_intra(gates[d, c][0], gates[d, c][1], hi_c, consts[d])
                incs[d, c] = _gla_increment(hi_c, gates[d, c][2], consts[d])
            for d, rev, c in batch:
                outs[nd + d][0, 0, c] = sts[d].astype(outs[nd + d].dtype)
                o_state, sts[d] = _gla_state(gates[d, c][0], sts[d], gates[d, c][3], incs[d, c])
                outs[d][rows_of(c), :] = (intra[d, c] + o_state).astype(outs[d].dtype)
        for st_ref, st in zip(st_refs, sts):
            st_ref[...] = st

    def tb(rev):
        return (lambda i: nblk - 1 - i) if rev else (lambda i: i)

    tok = lambda rev: pl.BlockSpec((rows, HEAD_PAIR), lambda b, p, i: (b * nblk + tb(rev)(i), p))
    lspec = pl.BlockSpec((2, HEAD_PAIR), lambda b, p, i: (0, p))
    sspec = lambda rev: pl.BlockSpec((1, 1, group, HEAD_PAIR, HEAD_PAIR), lambda b, p, i: (b, p, tb(rev)(i), 0, 0))
    args, in_specs = [], []
    for d, rev in enumerate(GLA_DIRS):
        args += [hq, hi, zs[d], lbls[d]]
        in_specs += [tok(rev), tok(rev), tok(rev), lspec]
    return pl.pallas_call(
        body, name="gla_fwd", grid=(nb, 4, nblk),
        out_shape=[jax.ShapeDtypeStruct((t, A_WIDTH), BF16)] * nd
        + [jax.ShapeDtypeStruct((nb, 4, n_chunks, HEAD_PAIR, HEAD_PAIR), BF16)] * nd,
        in_specs=in_specs, out_specs=[tok(rev) for rev in GLA_DIRS] + [sspec(rev) for rev in GLA_DIRS],
        scratch_shapes=[pltpu.VMEM((HEAD_PAIR, HEAD_PAIR), F32)] * nd,
        compiler_params=_params(),
    )(*args)


def _gla_bwd(hq, hi, zs, lbls, saved, do, nb, seq, group):
    t = hq.shape[0]
    rows = group * CHUNK
    nblk = seq // rows
    nd = len(GLA_DIRS)

    def body(*refs):
        ins, outs, dst_refs = refs[:6 * nd], refs[6 * nd:10 * nd], refs[10 * nd:]
        dl_refs = outs[3 * nd:]

        @pl.when(pl.program_id(2) == 0)
        def _():
            for dst_ref, dl_ref in zip(dst_refs, dl_refs):
                dst_ref[...] = jnp.zeros_like(dst_ref)
                dl_ref[...] = jnp.zeros_like(dl_ref)

        consts = [_gla_consts(rev) for rev in GLA_DIRS]
        dsts = [dst_ref[...] for dst_ref in dst_refs]
        dls = [[jnp.zeros((1, HEAD_PAIR), F32), jnp.zeros((1, HEAD_PAIR), F32)] for _ in GLA_DIRS]
        work = [(d, rev, cc if rev else group - 1 - cc) for cc in range(group) for d, rev in enumerate(GLA_DIRS)]
        for w0 in range(0, len(work), GLA_BATCH_BWD):
            vjps = {}
            for d, rev, c in work[w0:w0 + GLA_BATCH_BWD]:
                hq_ref, hi_ref, z_ref, lbl_ref, save_ref, _ = ins[6 * d:6 * d + 6]
                r = pl.ds(c * CHUNK, CHUNK)
                fn = functools.partial(_gla_chunk, consts=consts[d], reverse=rev)
                _, vjps[d, c] = jax.vjp(fn, hq_ref[r, :], hi_ref[r, :], z_ref[r, :], lbl_ref[0:1, :], lbl_ref[1:2, :],
                                         save_ref[0, 0, c].astype(F32))
            for d, rev, c in work[w0:w0 + GLA_BATCH_BWD]:
                dq_ref, dv_ref, dz_ref = outs[3 * d:3 * d + 3]
                r = pl.ds(c * CHUNK, CHUNK)
                d_hq, d_hi, d_z, d_l0, d_l1, dsts[d] = vjps[d, c]((ins[6 * d + 5][r, :].astype(F32), dsts[d]))
                dq_ref[r, :] = d_hq.astype(dq_ref.dtype)
                dv_ref[r, :] = d_hi.astype(dv_ref.dtype)
                dz_ref[r, :] = d_z.astype(dz_ref.dtype)
                dls[d] = [dls[d][0] + d_l0, dls[d][1] + d_l1]
        for d in range(nd):
            dst_refs[d][...] = dsts[d]
            dl_refs[d][0, 0:1, :] += dls[d][0]
            dl_refs[d][0, 1:2, :] += dls[d][1]

    def tb(rev):
        return (lambda i: i) if rev else (lambda i: nblk - 1 - i)

    tok = lambda rev: pl.BlockSpec((rows, HEAD_PAIR), lambda b, p, i: (b * nblk + tb(rev)(i), p))
    lspec = pl.BlockSpec((2, HEAD_PAIR), lambda b, p, i: (0, p))
    sspec = lambda rev: pl.BlockSpec((1, 1, group, HEAD_PAIR, HEAD_PAIR), lambda b, p, i: (b, p, tb(rev)(i), 0, 0))
    args, in_specs, out_specs = [], [], []
    for d, rev in enumerate(GLA_DIRS):
        args += [hq, hi, zs[d], lbls[d], saved[d], do]
        in_specs += [tok(rev), tok(rev), tok(rev), lspec, sspec(rev), tok(rev)]
        out_specs += [tok(rev)] * 3
    out_specs += [pl.BlockSpec((1, 2, HEAD_PAIR), lambda b, p, i: (b, 0, p))] * nd
    return pl.pallas_call(
        body, name="gla_bwd", grid=(nb, 4, nblk),
        out_shape=[jax.ShapeDtypeStruct((t, A_WIDTH), BF16)] * (3 * nd) + [jax.ShapeDtypeStruct((nb, 2, A_WIDTH), F32)] * nd,
        in_specs=in_specs, out_specs=out_specs,
        scratch_shapes=[pltpu.VMEM((HEAD_PAIR, HEAD_PAIR), F32)] * nd,
        compiler_params=_params(),
    )(*args)


def _head_mean_matrix():
    r = lax.broadcasted_iota(jnp.int32, (A_WIDTH, A_WIDTH), 0) // 64
    c = lax.broadcasted_iota(jnp.int32, (A_WIDTH, A_WIDTH), 1) // 64
    return jnp.where(r == c, 1.0 / 64.0, 0.0).astype(BF16)


def _gla_out(o_f, o_b, hg, g, mean_mat):
    o = o_f + o_b
    ms = _group_mean(o * o, mean_mat)
    return o * lax.rsqrt(ms + EPS) * g * (hg * _sigmoid(hg))


def _gla_combine_bwd(o_f, o_b, hg, g, dy, tm):
    t = o_f.shape[0]

    def body(of_ref, ob_ref, hg_ref, g_ref, dy_ref, do_ref, dhg_ref, dg_ref):
        mean_mat = _head_mean_matrix()
        fn = lambda o, hgv, gv: _gla_out(o, jnp.zeros_like(o), hgv, gv, mean_mat)
        _, vjp = jax.vjp(fn, of_ref[...].astype(F32) + ob_ref[...].astype(F32), hg_ref[...], g_ref[...])
        d_o, d_hg, d_g = vjp(dy_ref[...].astype(F32))
        do_ref[...] = d_o.astype(do_ref.dtype)
        dhg_ref[...] = d_hg.astype(dhg_ref.dtype)

        @pl.when(pl.program_id(0) == 0)
        def _():
            dg_ref[...] = jnp.zeros_like(dg_ref)

        dg_ref[...] += d_g

    tok = pl.BlockSpec((tm, A_WIDTH), lambda i: (i, 0))
    vec = pl.BlockSpec((1, A_WIDTH), lambda i: (0, 0))
    return pl.pallas_call(
        body, name="gla_combine_bwd", grid=(t // tm,),
        out_shape=[jax.ShapeDtypeStruct((t, A_WIDTH), BF16), jax.ShapeDtypeStruct((t, A_WIDTH), BF16),
                   jax.ShapeDtypeStruct((1, A_WIDTH), F32)],
        in_specs=[tok, tok, tok, _const_spec((1, A_WIDTH)), tok], out_specs=[tok, tok, vec], compiler_params=_params(),
    )(o_f, o_b, hg, g, dy)


def _post_fwd(x, o_f, o_b, hg, oattn, tgt, g_hgrn, g_mla, w_out, g2, w_gate, w_up, w_down, g_fin, tm):
    t = x.shape[0]

    def body(x_ref, of_ref, ob_ref, hg_ref, oa_ref, tgt_ref, gh_ref, gm_ref, wo_ref, g2_ref, wg_ref, wu_ref, wd_ref, gf_ref,
             x1_ref, x2_ref, ycat_ref, gate_ref, up_ref, loss_ref):
        part = jnp.zeros((1, 1), F32)
        mean_mat = _head_mean_matrix()
        for j in range(tm // min(tm, ROW_SUB)):
            r = pl.ds(j * min(tm, ROW_SUB), min(tm, ROW_SUB))
            ya = _gla_out(of_ref[r, :].astype(F32), ob_ref[r, :].astype(F32), hg_ref[r, :], gh_ref[...], mean_mat).astype(BF16)
            yb = _rms(oa_ref[r, :], gm_ref[...]).astype(BF16)
            ycat_ref[r, 0:A_WIDTH] = ya
            ycat_ref[r, A_WIDTH:] = yb
            x1 = x_ref[r, :] + _dot(ya, wo_ref[0:A_WIDTH, :]) + _dot(yb, wo_ref[A_WIDTH:, :])
            x1_ref[r, :] = x1
            h2 = _rms(x1, g2_ref[...]).astype(BF16)
            gate, up = _dot_nt(h2, wg_ref[...]), _dot_nt(h2, wu_ref[...])
            gate_ref[r, :] = gate.astype(BF16)
            up_ref[r, :] = up.astype(BF16)
            act = (gate * _sigmoid(gate) * up).astype(BF16)
            x2 = x1 + _dot(act, wd_ref[...])
            x2_ref[r, :] = x2
            err = _rms(x2, gf_ref[...]) - tgt_ref[r, :]
            part = part + 0.5 * jnp.sum(jnp.mean(err * err, axis=-1, keepdims=True), axis=0, keepdims=True)

        @pl.when(pl.program_id(0) == 0)
        def _():
            loss_ref[...] = jnp.zeros_like(loss_ref)

        loss_ref[...] += jnp.broadcast_to(part, loss_ref.shape)

    tok = lambda wd: pl.BlockSpec((tm, wd), lambda i: (i, 0))
    return pl.pallas_call(
        body, name="post_fwd", grid=(t // tm,),
        out_shape=[jax.ShapeDtypeStruct((t, D_MODEL), F32)] * 2 + [jax.ShapeDtypeStruct((t, D_MODEL), BF16)]
        + [jax.ShapeDtypeStruct((t, D_FF), BF16)] * 2 + [jax.ShapeDtypeStruct((1, 128), F32)],
        in_specs=[tok(D_MODEL), tok(A_WIDTH), tok(A_WIDTH), tok(A_WIDTH), tok(512), tok(D_MODEL), _const_spec((1, A_WIDTH)),
                  _const_spec((1, 512)), _const_spec((D_MODEL, D_MODEL)),
                  _const_spec((1, D_MODEL)), _const_spec((D_FF, D_MODEL)), _const_spec((D_FF, D_MODEL)),
                  _const_spec((D_FF, D_MODEL)), _const_spec((1, D_MODEL))],
        out_specs=[tok(D_MODEL), tok(D_MODEL), tok(D_MODEL), tok(D_FF), tok(D_FF), pl.BlockSpec((1, 128), lambda i: (0, 0))],
        compiler_params=_params(),
    )(x, o_f, o_b, hg, oattn, tgt, g_hgrn, g_mla, w_out, g2, w_gate, w_up, w_down, g_fin)


def _post_bwd(x1, x2, gate_b, up_b, oattn, tgt, g_mla, w_out, g2, w_gate, w_up, w_down, g_fin, tm):
    t = x1.shape[0]

    def body(x1_ref, x2_ref, gate_ref, up_ref, oa_ref, tgt_ref, gm_ref, wo_ref, g2_ref, wg_ref, wu_ref, wd_ref, gf_ref,
             dx1_ref, dya_ref, doa_ref, dx1b_ref, h2_ref, dgate_ref, dup_ref, act_ref, dx2b_ref,
             dgm_ref, dg2_ref, dgf_ref):
        x1, x2 = x1_ref[...], x2_ref[...]
        dy = (_rms(x2, gf_ref[...]) - tgt_ref[...]) * (1.0 / D_MODEL)
        dx2, dgf = _rms_bwd(x2, gf_ref[...], dy)
        dx2b = dx2.astype(BF16)
        dx2b_ref[...] = dx2b
        h2_ref[...] = _rms(x1, g2_ref[...]).astype(BF16)
        gate, up = gate_ref[...].astype(F32), up_ref[...].astype(F32)
        sg = _sigmoid(gate)
        sl = gate * sg
        act_ref[...] = (sl * up).astype(BF16)
        dact = _dot_nt(dx2b, wd_ref[...])
        dup = (dact * sl).astype(BF16)
        dgate = (dact * up * (sg * (1.0 + gate * (1.0 - sg)))).astype(BF16)
        dup_ref[...] = dup
        dgate_ref[...] = dgate
        dh2 = _dot(dgate, wg_ref[...]) + _dot(dup, wu_ref[...])
        dx1n, dg2 = _rms_bwd(x1, g2_ref[...], dh2)
        dx1 = dx2 + dx1n
        dx1_ref[...] = dx1
        dx1b = dx1.astype(BF16)
        dx1b_ref[...] = dx1b
        oa = oa_ref[...]
        dya_ref[...] = _dot_nt(dx1b, wo_ref[0:A_WIDTH, :]).astype(dya_ref.dtype)
        doa, dgm = _rms_bwd(oa, gm_ref[...], _dot_nt(dx1b, wo_ref[A_WIDTH:, :]))
        doa_ref[...] = doa.astype(doa_ref.dtype)

        @pl.when(pl.program_id(0) == 0)
        def _():
            dgm_ref[...] = jnp.zeros_like(dgm_ref)
            dg2_ref[...] = jnp.zeros_like(dg2_ref)
            dgf_ref[...] = jnp.zeros_like(dgf_ref)

        dgm_ref[...] += dgm
        dg2_ref[...] += dg2
        dgf_ref[...] += dgf

    tok = lambda wd: pl.BlockSpec((tm, wd), lambda i: (i, 0))
    vec = lambda wd: pl.BlockSpec((1, wd), lambda i: (0, 0))
    sds = lambda wd, dt: jax.ShapeDtypeStruct((t, wd), dt)
    return pl.pallas_call(
        body, name="post_bwd", grid=(t // tm,),
        out_shape=[sds(D_MODEL, F32), sds(512, BF16), sds(512, BF16), sds(D_MODEL, BF16), sds(D_MODEL, BF16),
                   sds(D_FF, BF16), sds(D_FF, BF16), sds(D_FF, BF16), sds(D_MODEL, BF16),
                   jax.ShapeDtypeStruct((1, 512), F32), jax.ShapeDtypeStruct((1, D_MODEL), F32), jax.ShapeDtypeStruct((1, D_MODEL), F32)],
        in_specs=[tok(D_MODEL), tok(D_MODEL), tok(D_FF), tok(D_FF), tok(512), tok(D_MODEL), _const_spec((1, 512)),
                  _const_spec((D_MODEL, D_MODEL)), _const_spec((1, D_MODEL)), _const_spec((D_FF, D_MODEL)),
                  _const_spec((D_FF, D_MODEL)), _const_spec((D_FF, D_MODEL)), _const_spec((1, D_MODEL))],
        out_specs=[tok(D_MODEL), tok(512), tok(512), tok(D_MODEL), tok(D_MODEL), tok(D_FF), tok(D_FF), tok(D_FF),
                   tok(D_MODEL), vec(512), vec(D_MODEL), vec(D_MODEL)],
        compiler_params=_params(),
    )(x1, x2, gate_b, up_b, oattn, tgt, g_mla, w_out, g2, w_gate, w_up, w_down, g_fin)


def _matmul_tn(a, b, tn, tt, tag, b_cols=None, k_out=None, exchange=()):
    t, k = a.shape
    c0, n = (0, b.shape[1]) if b_cols is None else b_cols
    k_out = k if k_out is None else k_out
    last = t // tt - 1
    ne = len(exchange)
    n_j = n // tn

    def body(a_ref, b_ref, *rest):
        o_ref, acc_ref = rest[ne], rest[2 * ne + 1]
        if ne:
            start, finish = _exchange_protocol(rest[:ne], rest[ne + 1:2 * ne + 1], [True] * ne, *rest[2 * ne + 2:])
            pl.when((pl.program_id(0) == 0) & (pl.program_id(1) == 0))(start)
        part = _dot_tn(a_ref[...], b_ref[...])

        @pl.when(pl.program_id(1) == 0)
        def _():
            acc_ref[...] = part

        @pl.when(pl.program_id(1) > 0)
        def _():
            acc_ref[...] += part

        @pl.when(pl.program_id(1) == last)
        def _():
            o_ref[...] = acc_ref[0:k_out, :].astype(o_ref.dtype)

        if ne:
            pl.when((pl.program_id(0) == n_j - 1) & (pl.program_id(1) == last))(finish)

    any_spec = pl.BlockSpec(memory_space=pl.ANY)
    out = pl.pallas_call(
        body, name="wgrad_" + tag, grid=(n_j, t // tt),
        out_shape=[jax.ShapeDtypeStruct((k_out, n), BF16)] + _slot_shapes(exchange, [True] * ne),
        in_specs=[pl.BlockSpec((tt, k), lambda j, i: (i, 0)), pl.BlockSpec((tt, tn), lambda j, i: (i, j + c0 // tn))]
        + [any_spec] * ne,
        out_specs=[pl.BlockSpec((k_out, tn), lambda j, i: (0, j))] + [any_spec] * ne,
        scratch_shapes=[pltpu.VMEM((k, tn), F32)] + (_comm_sems(ne) if ne else []),
        compiler_params=_params(),
    )(a, b, *exchange)
    return out if ne else out[0]


def _inproj_qkv_bwd(x, g1, w_in, dx1, pieces, cq, ckv, g_qa, g_kva, w_q, w_kv, tables, dq, dk, dv, seq, tm):
    t = x.shape[0]
    nblk = seq // tm
    last = t // tm - 1
    counts = [len(p) for p in pieces]
    flat = [a for p in pieces for a in p]
    n_flat = len(flat)
    offs = [sum(IN_WIDTHS[:j]) for j in range(len(IN_WIDTHS))]

    def body(x_ref, g_ref, w_ref, dx1_ref, cq_ref, ckv_ref, gq_ref, gk_ref, wq_ref, wkv_ref, c_ref, sa_ref, sb_ref,
             dq_ref, dk_ref, dv_ref, *refs):
        ins = refs[:n_flat]
        (dx_ref, h_ref, dp_ref, dwq_ref, dwkv_ref, dg_ref, dgq_ref, dgk_ref,
         cqn_ref, dqf_ref, ckn_ref, dkv_ref, accq_ref, acckv_ref) = refs[n_flat:]
        cos_t, sin_a, sin_b = c_ref[...], sa_ref[...], sb_ref[...]
        cqn_ref[...] = _rms(cq_ref[...], gq_ref[...]).astype(BF16)
        ckn_ref[...] = _rms(ckv_ref[...], gk_ref[...]).astype(BF16)
        dkr = jnp.zeros((tm, 128), F32)
        for hd in range(B_HEADS):
            lo = hd * QK_PAD
            dqf_ref[:, lo:lo + 128] = (dq_ref[:, lo:lo + 128].astype(F32) * ATTN_SCALE).astype(BF16)
            dq_rope = dq_ref[:, lo + 128:lo + 256].astype(F32) * ATTN_SCALE
            dqf_ref[:, lo + 128:lo + 256] = _rope_t(dq_rope, cos_t, sin_a, sin_b).astype(BF16)
            dkv_ref[:, lo:lo + 128] = dk_ref[:, lo:lo + 128].astype(BF16)
            dkv_ref[:, lo + 128:lo + 256] = dv_ref[:, hd * B_V:(hd + 1) * B_V].astype(BF16)
            dkr = dkr + dk_ref[:, lo + 128:lo + 256]
        dcq, dgq = _rms_bwd(cq_ref[...], gq_ref[...], _dot(dqf_ref[...], wq_ref[...]))
        dckv, dgk = _rms_bwd(ckv_ref[...], gk_ref[...], _dot_nt(dkv_ref[...], wkv_ref[...]))
        dp_ref[:, offs[5]:offs[6]] = dcq.astype(BF16)
        dp_ref[:, offs[6]:offs[7]] = dckv.astype(BF16)
        dp_ref[:, offs[7]:] = _rope_t(dkr, cos_t, sin_a, sin_b).astype(BF16)
        j = 0
        for g, cnt in enumerate(counts):
            acc = ins[j][...].astype(F32)
            for jj in range(1, cnt):
                acc = acc + ins[j + jj][...].astype(F32)
            dp_ref[:, offs[g]:offs[g] + IN_WIDTHS[g]] = acc.astype(BF16)
            j += cnt
        xv = x_ref[...]
        h_ref[...] = _rms(xv, g_ref[...]).astype(BF16)
        dxn, dg = _rms_bwd(xv, g_ref[...], _dot(dp_ref[...], w_ref[...]))
        dx_ref[...] = dx1_ref[...] + dxn

        @pl.when(pl.program_id(0) == 0)
        def _():
            dg_ref[...] = jnp.zeros_like(dg_ref)
            dgq_ref[...] = jnp.zeros_like(dgq_ref)
            dgk_ref[...] = jnp.zeros_like(dgk_ref)
            accq_ref[...] = jnp.zeros_like(accq_ref)
            acckv_ref[...] = jnp.zeros_like(acckv_ref)

        dg_ref[...] += dg
        dgq_ref[...] += dgq
        dgk_ref[...] += dgk
        accq_ref[...] += _dot_tn(dqf_ref[...], cqn_ref[...])
        acckv_ref[...] += _dot_tn(ckn_ref[...], dkv_ref[...])

        @pl.when(pl.program_id(0) == last)
        def _():
            dwq_ref[...] = accq_ref[...].astype(dwq_ref.dtype)
            dwkv_ref[...] = acckv_ref[...].astype(dwkv_ref.dtype)

    tok = lambda wd: pl.BlockSpec((tm, wd), lambda i: (i, 0))
    vec = lambda wd: pl.BlockSpec((1, wd), lambda i: (0, 0))
    whole = lambda r, c: pl.BlockSpec((r, c), lambda i: (0, 0))
    tab = pl.BlockSpec((tm, 128), lambda i: (i % nblk, 0))
    sds = lambda wd, dt: jax.ShapeDtypeStruct((t, wd), dt)
    return pl.pallas_call(
        body, name="inproj_qkv_bwd", grid=(t // tm,),
        out_shape=[sds(D_MODEL, F32), sds(D_MODEL, BF16), sds(D_IN_PAD, BF16),
                   jax.ShapeDtypeStruct((B_HEADS * QK_PAD, Q_LORA), BF16), jax.ShapeDtypeStruct((KV_LORA, 1024), BF16),
                   jax.ShapeDtypeStruct((1, D_MODEL), F32), jax.ShapeDtypeStruct((1, Q_LORA), F32),
                   jax.ShapeDtypeStruct((1, KV_LORA), F32)],
        in_specs=[tok(D_MODEL), _const_spec((1, D_MODEL)), _const_spec((D_IN_PAD, D_MODEL)), tok(D_MODEL), tok(Q_LORA),
                  tok(KV_LORA), _const_spec((1, Q_LORA)), _const_spec((1, KV_LORA)), _const_spec((1024, Q_LORA)),
                  _const_spec((KV_LORA, 1024)), tab, tab, tab, tok(1024), tok(1024), tok(512)] + [tok(512)] * n_flat,
        out_specs=[tok(D_MODEL), tok(D_MODEL), tok(D_IN_PAD), whole(B_HEADS * QK_PAD, Q_LORA), whole(KV_LORA, 1024),
                   vec(D_MODEL), vec(Q_LORA), vec(KV_LORA)],
        scratch_shapes=[pltpu.VMEM((tm, Q_LORA), BF16), pltpu.VMEM((tm, 1024), BF16), pltpu.VMEM((tm, KV_LORA), BF16),
                        pltpu.VMEM((tm, 1024), BF16), pltpu.VMEM((B_HEADS * QK_PAD, Q_LORA), F32),
                        pltpu.VMEM((KV_LORA, 1024), F32)],
        compiler_params=_params(),
    )(x, g1, w_in, dx1, cq, ckv, g_qa, g_kva, w_q, w_kv, *tables, dq, dk, dv, *flat)


def _cols_from_slots(g):
    n, r, cs = g.shape
    return g.transpose(1, 0, 2).reshape(r, n * cs)


def _cols_to_slots(full):
    r, c = full.shape
    return full.reshape(r, N_DEV, c // N_DEV).transpose(1, 0, 2)


def _arrange_w_in_t(w_in_t):
    return jnp.concatenate([w_in_t, jnp.zeros((D_IN_PAD - D_IN, D_MODEL), w_in_t.dtype)], axis=0)


def _arrange_w_q_t(w_q_t):
    q3 = w_q_t.reshape(B_HEADS, B_NOPE + B_ROPE, Q_LORA)
    pad = jnp.zeros((B_HEADS, QK_PAD - B_NOPE - B_ROPE, Q_LORA), w_q_t.dtype)
    return jnp.concatenate([q3, pad], axis=1).reshape(B_HEADS * QK_PAD, Q_LORA)


def _unarrange_w_q_t(d_q_t):
    return d_q_t.reshape(B_HEADS, QK_PAD, Q_LORA)[:, :B_NOPE + B_ROPE].reshape(B_HEADS * (B_NOPE + B_ROPE), Q_LORA)


def _step_core(x, loss_target, small_w, lb_full, early_full, late, seq, group, tiles, distributed):
    g1, g_hgrn, g_qa, g_kva, g_mla, g2, g_fin = small_w
    w_in, w_q, w_kv = _arrange_w_in_t(early_full[0]), _arrange_w_q_t(early_full[1]), early_full[2]
    nb = x.shape[0]
    t = nb * seq
    tm, tm_fwd, tq_f, tq_b, tt = tiles
    xt = x.reshape(t, D_MODEL)
    tgt = loss_target.reshape(t, D_MODEL)
    tables = _rope_tables(seq)

    hq, hi, zf, zb, hg, cq, ckv, qcat, kcat, vv = _inproj_qkv(xt, g1, w_in, g_qa, g_kva, w_q, w_kv, tables, seq, tm_fwd)
    if distributed:
        oattn, lse, *late_slots = _attn_fwd(qcat, kcat, vv, nb, seq, tq_f, gather=tuple(late))
    else:
        oattn, lse = _attn_fwd(qcat, kcat, vv, nb, seq, tq_f)
        late_slots = late
    w_out = late_slots[0].reshape(D_MODEL, D_MODEL)
    w_gate, w_up = late_slots[1].reshape(D_FF, D_MODEL), late_slots[2].reshape(D_FF, D_MODEL)
    w_down = late_slots[3].reshape(D_FF, D_MODEL)
    lbl_f, lbl_b = lb_full[0], lb_full[1]
    o_f, o_b, save_f, save_b = _gla_fwd(hq, hi, (zf, zb), (lbl_f, lbl_b), nb, seq, group)
    x1, x2, ycat_b, gate_b, up_b, loss_row = _post_fwd(
        xt, o_f, o_b, hg, oattn, tgt, g_hgrn, g_mla, w_out, g2, w_gate, w_up, w_down, g_fin, tm_fwd)

    (dx1, d_ya, d_oattn, dx1_b, h2_b, dgate_b, dup_b, act_b, dx2_b, d_g_mla, d_g2, d_g_fin) = _post_bwd(
        x1, x2, gate_b, up_b, oattn, tgt, g_mla, w_out, g2, w_gate, w_up, w_down, g_fin, tm)
    d_w_gate = _matmul_tn(dgate_b, h2_b, 512, tt, "gate")
    d_w_up = _matmul_tn(dup_b, h2_b, 512, tt, "up")
    d_w_down = _matmul_tn(act_b, dx2_b, 512, tt, "down")
    d_w_out = _matmul_tn(ycat_b, dx1_b, D_MODEL, tt, "out")
    late_g = [d_w_out.reshape(N_DEV, D_MODEL // N_DEV, D_MODEL)] + [
        g.reshape(N_DEV, D_FF // N_DEV, D_MODEL) for g in (d_w_gate, d_w_up, d_w_down)]
    if distributed:
        dq, dk, dv, *late_g = _attn_bwd(qcat, kcat, vv, oattn, lse, d_oattn, nb, seq, tq_b, exchange=tuple(late_g))
    else:
        dq, dk, dv = _attn_bwd(qcat, kcat, vv, oattn, lse, d_oattn, nb, seq, tq_b)
    d_o, d_hg, d_g_hgrn = _gla_combine_bwd(o_f, o_b, hg, g_hgrn, d_ya, tm_fwd)
    dq_f, dv_f, dz_f, dq_b, dv_b, dz_b, dl_f, dl_b = _gla_bwd(
        hq, hi, (zf, zb), (lbl_f, lbl_b), (save_f, save_b), d_o, nb, seq, group)
    grad_x, h1_b, dproj_b, d_w_q, d_w_kv, d_g1, d_g_qa, d_g_kva = _inproj_qkv_bwd(
        xt, g1, w_in, dx1, [[dq_f, dq_b], [dv_f, dv_b], [dz_f], [dz_b], [d_hg]], cq, ckv, g_qa, g_kva, w_q, w_kv, tables,
        dq, dk, dv, seq, tm_fwd)
    half = D_MODEL // 2
    in_slots = lambda g: g.reshape(N_DEV, D_IN // N_DEV, half)
    g_in_a = in_slots(_matmul_tn(dproj_b, h1_b, half, tt, "in_a", b_cols=(0, half), k_out=D_IN))
    if distributed:
        d_w_in_b, g_in_a = _matmul_tn(dproj_b, h1_b, half, tt, "in_b", b_cols=(half, half), k_out=D_IN, exchange=(g_in_a,))
    else:
        d_w_in_b = _matmul_tn(dproj_b, h1_b, half, tt, "in_b", b_cols=(half, half), k_out=D_IN)

    early_g = [in_slots(d_w_in_b), _unarrange_w_q_t(d_w_q).reshape(N_DEV, 768 // N_DEV, Q_LORA), _cols_to_slots(d_w_kv)]
    d_lb = jnp.stack([jnp.sum(dl_f, axis=0), jnp.sum(dl_b, axis=0)], axis=0)
    small_grads = [d_g1, d_g_hgrn, d_g_qa, d_g_kva, d_g_mla, d_g2, d_g_fin]
    return loss_row, grad_x.reshape(nb, seq, D_MODEL), g_in_a, early_g, late_g, small_grads, d_lb


def kernel(x, norm1_g, w_in, lb_logits, hgrn_norm_g, q_a_norm_g, w_q_b, kv_a_norm_g, w_kv_b, mla_norm_g, w_out, norm2_g, w_gate, w_up, w_down, final_norm_g, loss_target, m_norm1_g, m_w_in, m_lb_logits, m_hgrn_norm_g, m_q_a_norm_g, m_w_q_b, m_kv_a_norm_g, m_w_kv_b, m_mla_norm_g, m_w_out, m_norm2_g, m_w_gate, m_w_up, m_w_down, m_final_norm_g, v_norm1_g, v_w_in, v_lb_logits, v_hgrn_norm_g, v_q_a_norm_g, v_w_q_b, v_kv_a_norm_g, v_w_kv_b, v_mla_norm_g, v_w_out, v_norm2_g, v_w_gate, v_w_up, v_w_down, v_final_norm_g):
    big_w = [w_in, w_q_b, w_kv_b, w_out, w_gate, w_up, w_down]
    big_m = [m_w_in, m_w_q_b, m_w_kv_b, m_w_out, m_w_gate, m_w_up, m_w_down]
    big_v = [v_w_in, v_w_q_b, v_w_kv_b, v_w_out, v_w_gate, v_w_up, v_w_down]
    small_w = [norm1_g, hgrn_norm_g, q_a_norm_g, kv_a_norm_g, mla_norm_g, norm2_g, final_norm_g]
    small_m = [m_norm1_g, m_hgrn_norm_g, m_q_a_norm_g, m_kv_a_norm_g, m_mla_norm_g, m_norm2_g, m_final_norm_g]
    small_v = [v_norm1_g, v_hgrn_norm_g, v_q_a_norm_g, v_kv_a_norm_g, v_mla_norm_g, v_norm2_g, v_final_norm_g]
    seq = x.shape[1]
    my_id = 4 * lax.axis_index("x") + 2 * lax.axis_index("y") + lax.axis_index("c")

    shard = lambda w: w[0].astype(BF16)
    col_t = lambda w: jnp.swapaxes(w, 1, 2)[0]
    shard_t = lambda w: col_t(w).astype(BF16)
    g_in, g_q, g_kv, g_lb = _all_gather_call([shard_t(w_in), shard_t(w_q_b), shard(w_kv_b), lb_logits.reshape(4, 64)])
    early_full = (g_in.reshape(D_IN, D_MODEL), g_q.reshape(768, Q_LORA), _cols_from_slots(g_kv))
    lb_full = g_lb.reshape(N_DEV, 2, 2, 64).transpose(1, 2, 0, 3).reshape(2, 2, 512)

    as_row = lambda a: a.reshape(1, -1)
    loss_row, grad_x, recv_in_a, early_g, late_recv, small_g, d_lb = _step_core(
        x, loss_target, [as_row(s) for s in small_w], lb_full, early_full,
        [shard(w_out), shard_t(w_gate), shard_t(w_up), shard(w_down)], seq, min(32, seq // CHUNK),
        (256, 512, min(1024, seq), min(1024, seq), min(2048, 2 * seq)), True)

    grads, deltas, new_ms, new_vs = {}, {}, {}, {}
    views = {name: (col_t if name in ("w_in", "w_q_b", "w_gate", "w_up") else (lambda a: a[0])) for name, _, _, _ in BIG}
    backs = {name: ((lambda a: jnp.swapaxes(a[None], 1, 2)) if name in ("w_in", "w_q_b", "w_gate", "w_up") else (lambda a: a[None]))
             for name, _, _, _ in BIG}
    by_name = {name: (w, m, v) for (name, _, _, _), w, m, v in zip(BIG, big_w, big_m, big_v)}
    late_names = ["w_out", "w_gate", "w_up", "w_down"]
    n_small = len(small_g)
    g_l, d_l, nm_l, nv_l, recv = _adamw_recv_hosting(
        [views[n](by_name[n][0]) for n in late_names], list(late_recv), [views[n](by_name[n][1]) for n in late_names],
        [views[n](by_name[n][2]) for n in late_names],
        early_g + small_g + [d_lb.reshape(4, 512), loss_row], [True] * 3 + [False] * (n_small + 2))
    for i, name in enumerate(late_names):
        grads[name], deltas[name], new_ms[name], new_vs[name] = (backs[name](a[i]) for a in (g_l, d_l, nm_l, nv_l))
    sums = _sum_slots_call(recv[3:])
    g_small = [g.reshape(s.shape) for g, s in zip(sums[:n_small], small_w)]
    g_lb_own = lax.dynamic_index_in_dim(sums[n_small].reshape(2, 2, N_DEV, 64), my_id, axis=2, keepdims=False)
    loss = sums[n_small + 1][0, 0]

    for name, r in zip(["w_in", "w_q_b", "w_kv_b"], recv[:3]):
        w, m, v = (views[name](a) for a in by_name[name])
        g, d, nm, nv = _adamw_recv_halves(w, (recv_in_a, r), m, v, name) if name == "w_in" else _adamw_recv(w, r, m, v, name)
        grads[name], deltas[name], new_ms[name], new_vs[name] = (backs[name](a) for a in (g, d, nm, nv))
    lb_rows = lambda a: a.reshape(4, 64)
    d_s, nm_s, nv_s = _adamw_small(
        [as_row(a) for a in small_w] + [lb_rows(lb_logits)], [as_row(a) for a in g_small] + [lb_rows(g_lb_own)],
        [as_row(a) for a in small_m] + [lb_rows(m_lb_logits)], [as_row(a) for a in small_v] + [lb_rows(v_lb_logits)])
    for i, (s, (name, _)) in enumerate(zip(small_w + [lb_logits], SMALL + (("lb_logits", 0),))):
        grads[name] = (g_small + [g_lb_own])[i]
        deltas[name], new_ms[name], new_vs[name] = d_s[i].reshape(s.shape), nm_s[i].reshape(s.shape), nv_s[i].reshape(s.shape)

    order = ["norm1_g", "w_in", "lb_logits", "hgrn_norm_g", "q_a_norm_g", "w_q_b", "kv_a_norm_g", "w_kv_b", "mla_norm_g",
             "w_out", "norm2_g", "w_gate", "w_up", "w_down", "final_norm_g"]
    return (loss, grad_x, *[grads[n] for n in order], *[deltas[n] for n in order],
            *[new_ms[n] for n in order], *[new_vs[n] for n in order])
```
